```python
import jax, jax.numpy as jnp
from jax import lax
import numpy as np

D_MODEL = 1024
BATCH = 8
SEQ = 4096
DEPTH = 1

GRID_W = 64
HEAD_DIM = 64
ATTN_HEADS = 8
ATTN_KV_HEADS = 2
ATTN_GROUP = ATTN_HEADS // ATTN_KV_HEADS
ROPE_THETA = 10000.0
Q_BLOCK = 128
HGRN_HEADS = 8
HGRN_KEY_DIM = 64
HGRN_VAL_DIM = 64
HGRN_CHUNK = 32
ATTN_WIDTH = ATTN_HEADS * HEAD_DIM
KV_WIDTH = ATTN_KV_HEADS * HEAD_DIM
HGRN_KW = HGRN_HEADS * HGRN_KEY_DIM
HGRN_VW = HGRN_HEADS * HGRN_VAL_DIM
D_FF = 2816
NORM_EPS = 1e-6
IN_SPLITS = (ATTN_WIDTH, KV_WIDTH, KV_WIDTH, HGRN_KW, HGRN_KW, HGRN_KW, HGRN_VW, HGRN_VW, D_MODEL, D_MODEL)
IN_WIDTH = 3 * KV_WIDTH + ATTN_WIDTH - KV_WIDTH + 3 * HGRN_KW + 2 * HGRN_VW + 2 * D_MODEL

kernel_name = "hybrid_gqa_axial_hgrn2_macaron"


def rmsnorm(x, g):
    x32 = x.astype(jnp.float32)
    y = x32 * lax.rsqrt(jnp.mean(x32 * x32, axis=-1, keepdims=True) + NORM_EPS)
    return (y * g.astype(jnp.float32)).astype(x.dtype)


def swiglu(h, w_gate, w_up, w_down):
    return (jax.nn.silu(h @ w_gate) * (h @ w_up)) @ w_down


def split_columns(p):
    offsets = [int(o) for o in np.cumsum(IN_SPLITS)[:-1]]
    return jnp.split(p, offsets, axis=-1)


def axial_rope_tables(seq_len):
    rows = seq_len // GRID_W
    row = jnp.repeat(jnp.arange(rows, dtype=jnp.float32), GRID_W)
    col = jnp.tile(jnp.arange(GRID_W, dtype=jnp.float32), rows)
    n_freq = HEAD_DIM // 4
    inv = ROPE_THETA ** (-jnp.arange(n_freq, dtype=jnp.float32) / n_freq)
    ang = jnp.concatenate([row[:, None] * inv, col[:, None] * inv], axis=-1)
    return jnp.cos(ang), jnp.sin(ang)


def apply_rope(x, cos, sin):
    x32 = x.astype(jnp.float32)
    x1, x2 = jnp.split(x32, 2, axis=-1)
    c = cos[:, None, :]
    s = sin[:, None, :]
    return jnp.concatenate([x1 * c - x2 * s, x2 * c + x1 * s], axis=-1).astype(x.dtype)


def axial_gqa_attention(q, k, v, q_g, k_g, cos, sin):
    B, S, _ = q.shape
    q = apply_rope(rmsnorm(q.reshape(B, S, ATTN_HEADS, HEAD_DIM), q_g), cos, sin)
    k = apply_rope(rmsnorm(k.reshape(B, S, ATTN_KV_HEADS, HEAD_DIM), k_g), cos, sin)
    v = v.reshape(B, S, ATTN_KV_HEADS, HEAD_DIM)
    q = (q * (HEAD_DIM ** -0.5)).reshape(B, S // Q_BLOCK, Q_BLOCK, ATTN_KV_HEADS, ATTN_GROUP, HEAD_DIM)
    q_blocks = jnp.moveaxis(q, 1, 0)

    def attend(qb):
        s = jnp.einsum('bqgrd,bkgd->bgrqk', qb, k).astype(jnp.float32)
        p = jax.nn.softmax(s, axis=-1).astype(v.dtype)
        return jnp.einsum('bgrqk,bkgd->bqgrd', p, v)

    o = lax.map(attend, q_blocks)
    return jnp.moveaxis(o, 0, 1).reshape(B, S, ATTN_WIDTH)


def gated_scan_chunked(q, k, v, log_f):
    B, H, S, Dk = q.shape
    Dv = v.shape[-1]
    C = HGRN_CHUNK
    N = S // C
    q, k, v, log_f = [t.astype(jnp.float32).reshape(B, H, N, C, t.shape[-1]) for t in (q, k, v, log_f)]
    b = jnp.cumsum(log_f, axis=3)
    b_mid = b[:, :, :, C // 2 - 1:C // 2, :]
    b_last = b[:, :, :, C - 1:, :]
    scores = jnp.einsum('bhntd,bhnsd->bhnts', q * jnp.exp(b - b_mid), k * jnp.exp(b_mid - b))
    lower = jnp.tril(jnp.ones((C, C), dtype=bool))
    scores = jnp.where(lower, scores, 0.0)
    o_intra = jnp.einsum('bhnts,bhnse->bhnte', scores, v)
    contrib = jnp.einsum('bhnsd,bhnse->bhnde', k * jnp.exp(b_last - b), v)
    decay = jnp.exp(b_last[:, :, :, 0, :])

    def step(state, inp):
        dec, add = inp
        return dec[..., None] * state + add, state

    _, states_in = lax.scan(step, jnp.zeros((B, H, Dk, Dv), jnp.float32),
                            (jnp.moveaxis(decay, 2, 0), jnp.moveaxis(contrib, 2, 0)))
    states_in = jnp.moveaxis(states_in, 0, 2)
    o_inter = jnp.einsum('bhntd,bhnde->bhnte', q * jnp.exp(b), states_in)
    return (o_intra + o_inter).reshape(B, H, S, Dv)


def hgrn2_bidirectional(q, f_fwd, f_bwd, i, g, lb, norm_g):
    B, S, _ = q.shape

    def heads(t, d):
        return t.reshape(B, S, HGRN_HEADS, d).transpose(0, 2, 1, 3)

    qh = heads(jax.nn.silu(q), HGRN_KEY_DIM)
    vh = heads(i, HGRN_VAL_DIM)

    def direction(f_logits, lb_dir, reverse):
        f = lb_dir + (1.0 - lb_dir) * jax.nn.sigmoid(f_logits.astype(jnp.float32))
        kh = heads(1.0 - f, HGRN_KEY_DIM)
        lfh = heads(jnp.log(f), HGRN_KEY_DIM)
        ts = (qh, kh, vh, lfh)
        if reverse:
            ts = tuple(jnp.flip(t, axis=2) for t in ts)
        o = gated_scan_chunked(*ts)
        return jnp.flip(o, axis=2) if reverse else o

    o = direction(f_fwd, lb[0], False) + direction(f_bwd, lb[1], True)
    o = rmsnorm(o.transpose(0, 2, 1, 3), norm_g.reshape(HGRN_HEADS, HGRN_VAL_DIM))
    o = o.reshape(B, S, HGRN_VW) * jax.nn.silu(g.astype(jnp.float32))
    return o.astype(g.dtype)


def _fwd_setup_inputs(seed: int = 0) -> dict:
    key = jax.random.key(seed)
    ks = jax.random.split(key, 20)
    f32 = jnp.float32

    def w(k, shape, fan_in):
        return jax.random.normal(k, shape, f32) * (fan_in ** -0.5)

    def gain(k, shape):
        return 1.0 + 0.02 * jax.random.normal(k, shape, f32)

    return {
        "x": jax.random.normal(ks[0], (BATCH, SEQ, D_MODEL), f32),
        "ffn1_norm_g": gain(ks[1], (DEPTH, D_MODEL)),
        "ffn1_w_gate": w(ks[2], (DEPTH, D_MODEL, D_FF), D_MODEL),
        "ffn1_w_up": w(ks[3], (DEPTH, D_MODEL, D_FF), D_MODEL),
        "ffn1_w_down": w(ks[4], (DEPTH, D_FF, D_MODEL), D_FF),
        "mix_norm_g": gain(ks[5], (DEPTH, D_MODEL)),
        "w_in": w(ks[6], (DEPTH, D_MODEL, IN_WIDTH), D_MODEL),
        "q_norm_g": gain(ks[7], (DEPTH, HEAD_DIM)),
        "k_norm_g": gain(ks[8], (DEPTH, HEAD_DIM)),
        "hgrn_lb_logits": 0.1 * jax.random.normal(ks[9], (2, DEPTH + 1, HGRN_KW), f32),
        "hgrn_out_norm_g": gain(ks[10], (DEPTH, HGRN_VW)),
        "w_branch_attn": w(ks[11], (DEPTH, ATTN_WIDTH, D_MODEL), ATTN_WIDTH),
        "w_branch_hgrn": w(ks[12], (DEPTH, HGRN_VW, D_MODEL), HGRN_VW),
        "w_out": w(ks[13], (DEPTH, D_MODEL, D_MODEL), D_MODEL),
        "ffn2_norm_g": gain(ks[14], (DEPTH, D_MODEL)),
        "ffn2_w_gate": w(ks[15], (DEPTH, D_MODEL, D_FF), D_MODEL),
        "ffn2_w_up": w(ks[16], (DEPTH, D_MODEL, D_FF), D_MODEL),
        "ffn2_w_down": w(ks[17], (DEPTH, D_FF, D_MODEL), D_FF),
        "final_norm_g": gain(ks[18], (D_MODEL,)),
    }


def _fwd_reference(x, ffn1_norm_g, ffn1_w_gate, ffn1_w_up, ffn1_w_down, mix_norm_g, w_in,
              q_norm_g, k_norm_g, hgrn_lb_logits, hgrn_out_norm_g, w_branch_attn,
              w_branch_hgrn, w_out, ffn2_norm_g, ffn2_w_gate, ffn2_w_up, ffn2_w_down,
              final_norm_g):
    S = x.shape[1]
    cos, sin = axial_rope_tables(S)
    lb_all = jnp.cumsum(jax.nn.softmax(hgrn_lb_logits.astype(jnp.float32), axis=1), axis=1)
    for l in range(DEPTH):
        x = x + 0.5 * swiglu(rmsnorm(x, ffn1_norm_g[l]), ffn1_w_gate[l], ffn1_w_up[l], ffn1_w_down[l])
        h = rmsnorm(x, mix_norm_g[l])
        aq, ak, av, hq, hff, hfb, hi, hg, ga, gb = split_columns(h @ w_in[l])
        y_attn = axial_gqa_attention(aq, ak, av, q_norm_g[l], k_norm_g[l], cos, sin) @ w_branch_attn[l]
        y_hgrn = hgrn2_bidirectional(hq, hff, hfb, hi, hg, lb_all[:, l, :], hgrn_out_norm_g[l]) @ w_branch_hgrn[l]
        merged = jax.nn.sigmoid(ga) * y_attn + jax.nn.sigmoid(gb) * y_hgrn
        x = x + merged @ w_out[l]
        x = x + 0.5 * swiglu(rmsnorm(x, ffn2_norm_g[l]), ffn2_w_gate[l], ffn2_w_up[l], ffn2_w_down[l])
    return rmsnorm(x, final_norm_g)


import jax as _jax
import jax.numpy as _jnp

TWIN_FORMAT = 'train_step'
FWD_PARAMS = ['x', 'ffn1_norm_g', 'ffn1_w_gate', 'ffn1_w_up', 'ffn1_w_down', 'mix_norm_g', 'w_in', 'q_norm_g', 'k_norm_g', 'hgrn_lb_logits', 'hgrn_out_norm_g', 'w_branch_attn', 'w_branch_hgrn', 'w_out', 'ffn2_norm_g', 'ffn2_w_gate', 'ffn2_w_up', 'ffn2_w_down', 'final_norm_g']
TWIN_WEIGHTS = ['ffn1_norm_g', 'ffn1_w_gate', 'ffn1_w_up', 'ffn1_w_down', 'mix_norm_g', 'w_in', 'q_norm_g', 'k_norm_g', 'hgrn_lb_logits', 'hgrn_out_norm_g', 'w_branch_attn', 'w_branch_hgrn', 'w_out', 'ffn2_norm_g', 'ffn2_w_gate', 'ffn2_w_up', 'ffn2_w_down', 'final_norm_g']
TWIN_DIFF_INPUT = 'x'
TWIN_INPUTS = ['x', 'ffn1_norm_g', 'ffn1_w_gate', 'ffn1_w_up', 'ffn1_w_down', 'mix_norm_g', 'w_in', 'q_norm_g', 'k_norm_g', 'hgrn_lb_logits', 'hgrn_out_norm_g', 'w_branch_attn', 'w_branch_hgrn', 'w_out', 'ffn2_norm_g', 'ffn2_w_gate', 'ffn2_w_up', 'ffn2_w_down', 'final_norm_g', 'loss_target', 'm_ffn1_norm_g', 'm_ffn1_w_gate', 'm_ffn1_w_up', 'm_ffn1_w_down', 'm_mix_norm_g', 'm_w_in', 'm_q_norm_g', 'm_k_norm_g', 'm_hgrn_lb_logits', 'm_hgrn_out_norm_g', 'm_w_branch_attn', 'm_w_branch_hgrn', 'm_w_out', 'm_ffn2_norm_g', 'm_ffn2_w_gate', 'm_ffn2_w_up', 'm_ffn2_w_down', 'm_final_norm_g', 'v_ffn1_norm_g', 'v_ffn1_w_gate', 'v_ffn1_w_up', 'v_ffn1_w_down', 'v_mix_norm_g', 'v_w_in', 'v_q_norm_g', 'v_k_norm_g', 'v_hgrn_lb_logits', 'v_hgrn_out_norm_g', 'v_w_branch_attn', 'v_w_branch_hgrn', 'v_w_out', 'v_ffn2_norm_g', 'v_ffn2_w_gate', 'v_ffn2_w_up', 'v_ffn2_w_down', 'v_final_norm_g']
TWIN_OUTPUTS = ['loss', 'grad_x', 'grad_ffn1_norm_g', 'grad_ffn1_w_gate', 'grad_ffn1_w_up', 'grad_ffn1_w_down', 'grad_mix_norm_g', 'grad_w_in', 'grad_q_norm_g', 'grad_k_norm_g', 'grad_hgrn_lb_logits', 'grad_hgrn_out_norm_g', 'grad_w_branch_attn', 'grad_w_branch_hgrn', 'grad_w_out', 'grad_ffn2_norm_g', 'grad_ffn2_w_gate', 'grad_ffn2_w_up', 'grad_ffn2_w_down', 'grad_final_norm_g', 'delta_ffn1_norm_g', 'delta_ffn1_w_gate', 'delta_ffn1_w_up', 'delta_ffn1_w_down', 'delta_mix_norm_g', 'delta_w_in', 'delta_q_norm_g', 'delta_k_norm_g', 'delta_hgrn_lb_logits', 'delta_hgrn_out_norm_g', 'delta_w_branch_attn', 'delta_w_branch_hgrn', 'delta_w_out', 'delta_ffn2_norm_g', 'delta_ffn2_w_gate', 'delta_ffn2_w_up', 'delta_ffn2_w_down', 'delta_final_norm_g', 'new_m_ffn1_norm_g', 'new_m_ffn1_w_gate', 'new_m_ffn1_w_up', 'new_m_ffn1_w_down', 'new_m_mix_norm_g', 'new_m_w_in', 'new_m_q_norm_g', 'new_m_k_norm_g', 'new_m_hgrn_lb_logits', 'new_m_hgrn_out_norm_g', 'new_m_w_branch_attn', 'new_m_w_branch_hgrn', 'new_m_w_out', 'new_m_ffn2_norm_g', 'new_m_ffn2_w_gate', 'new_m_ffn2_w_up', 'new_m_ffn2_w_down', 'new_m_final_norm_g', 'new_v_ffn1_norm_g', 'new_v_ffn1_w_gate', 'new_v_ffn1_w_up', 'new_v_ffn1_w_down', 'new_v_mix_norm_g', 'new_v_w_in', 'new_v_q_norm_g', 'new_v_k_norm_g', 'new_v_hgrn_lb_logits', 'new_v_hgrn_out_norm_g', 'new_v_w_branch_attn', 'new_v_w_branch_hgrn', 'new_v_w_out', 'new_v_ffn2_norm_g', 'new_v_ffn2_w_gate', 'new_v_ffn2_w_up', 'new_v_ffn2_w_down', 'new_v_final_norm_g']
TWIN_LEAF_KINDS = {'loss': 'loss', 'grad_x': 'grad_x', 'grad_ffn1_norm_g': 'grad_w', 'grad_ffn1_w_gate': 'grad_w', 'grad_ffn1_w_up': 'grad_w', 'grad_ffn1_w_down': 'grad_w', 'grad_mix_norm_g': 'grad_w', 'grad_w_in': 'grad_w', 'grad_q_norm_g': 'grad_w', 'grad_k_norm_g': 'grad_w', 'grad_hgrn_lb_logits': 'grad_w', 'grad_hgrn_out_norm_g': 'grad_w', 'grad_w_branch_attn': 'grad_w', 'grad_w_branch_hgrn': 'grad_w', 'grad_w_out': 'grad_w', 'grad_ffn2_norm_g': 'grad_w', 'grad_ffn2_w_gate': 'grad_w', 'grad_ffn2_w_up': 'grad_w', 'grad_ffn2_w_down': 'grad_w', 'grad_final_norm_g': 'grad_w', 'delta_ffn1_norm_g': 'delta_w', 'delta_ffn1_w_gate': 'delta_w', 'delta_ffn1_w_up': 'delta_w', 'delta_ffn1_w_down': 'delta_w', 'delta_mix_norm_g': 'delta_w', 'delta_w_in': 'delta_w', 'delta_q_norm_g': 'delta_w', 'delta_k_norm_g': 'delta_w', 'delta_hgrn_lb_logits': 'delta_w', 'delta_hgrn_out_norm_g': 'delta_w', 'delta_w_branch_attn': 'delta_w', 'delta_w_branch_hgrn': 'delta_w', 'delta_w_out': 'delta_w', 'delta_ffn2_norm_g': 'delta_w', 'delta_ffn2_w_gate': 'delta_w', 'delta_ffn2_w_up': 'delta_w', 'delta_ffn2_w_down': 'delta_w', 'delta_final_norm_g': 'delta_w', 'new_m_ffn1_norm_g': 'new_m', 'new_m_ffn1_w_gate': 'new_m', 'new_m_ffn1_w_up': 'new_m', 'new_m_ffn1_w_down': 'new_m', 'new_m_mix_norm_g': 'new_m', 'new_m_w_in': 'new_m', 'new_m_q_norm_g': 'new_m', 'new_m_k_norm_g': 'new_m', 'new_m_hgrn_lb_logits': 'new_m', 'new_m_hgrn_out_norm_g': 'new_m', 'new_m_w_branch_attn': 'new_m', 'new_m_w_branch_hgrn': 'new_m', 'new_m_w_out': 'new_m', 'new_m_ffn2_norm_g': 'new_m', 'new_m_ffn2_w_gate': 'new_m', 'new_m_ffn2_w_up': 'new_m', 'new_m_ffn2_w_down': 'new_m', 'new_m_final_norm_g': 'new_m', 'new_v_ffn1_norm_g': 'new_v', 'new_v_ffn1_w_gate': 'new_v', 'new_v_ffn1_w_up': 'new_v', 'new_v_ffn1_w_down': 'new_v', 'new_v_mix_norm_g': 'new_v', 'new_v_w_in': 'new_v', 'new_v_q_norm_g': 'new_v', 'new_v_k_norm_g': 'new_v', 'new_v_hgrn_lb_logits': 'new_v', 'new_v_hgrn_out_norm_g': 'new_v', 'new_v_w_branch_attn': 'new_v', 'new_v_w_branch_hgrn': 'new_v', 'new_v_w_out': 'new_v', 'new_v_ffn2_norm_g': 'new_v', 'new_v_ffn2_w_gate': 'new_v', 'new_v_ffn2_w_up': 'new_v', 'new_v_ffn2_w_down': 'new_v', 'new_v_final_norm_g': 'new_v'}


def _forward(args):
    return _fwd_reference(*[args[k] for k in FWD_PARAMS])


def _output_shape():
    out = _jax.eval_shape(lambda: _forward(_fwd_setup_inputs(0)))
    return out.shape, out.dtype

N_MICROBATCH = 1
ADAM_LR = 0.001
ADAM_B1 = 0.9
ADAM_B2 = 0.999
ADAM_EPS = 1e-08
ADAM_WD = 0.01
ADAM_STEP = 10
PER_EXAMPLE_BATCH_AXIS = {'x': 0, 'loss_target': 0}
SHARED_INPUTS = []
_WEIGHT_DTYPES = {'ffn1_norm_g': _jnp.float32, 'ffn1_w_gate': _jnp.float32, 'ffn1_w_up': _jnp.float32, 'ffn1_w_down': _jnp.float32, 'mix_norm_g': _jnp.float32, 'w_in': _jnp.float32, 'q_norm_g': _jnp.float32, 'k_norm_g': _jnp.float32, 'hgrn_lb_logits': _jnp.float32, 'hgrn_out_norm_g': _jnp.float32, 'w_branch_attn': _jnp.float32, 'w_branch_hgrn': _jnp.float32, 'w_out': _jnp.float32, 'ffn2_norm_g': _jnp.float32, 'ffn2_w_gate': _jnp.float32, 'ffn2_w_up': _jnp.float32, 'ffn2_w_down': _jnp.float32, 'final_norm_g': _jnp.float32}
MOMENT_SCALE = {'ffn1_norm_g': 9.085821e-02, 'ffn1_w_gate': 3.458706e-02, 'ffn1_w_up': 3.344731e-02, 'ffn1_w_down': 5.549956e-02, 'mix_norm_g': 8.900983e-02, 'w_in': 3.660167e-02, 'q_norm_g': 3.282097e-02, 'k_norm_g': 3.614377e-02, 'hgrn_lb_logits': 6.609029e-03, 'hgrn_out_norm_g': 7.928955e-02, 'w_branch_attn': 9.203329e-03, 'w_branch_hgrn': 5.477933e-02, 'w_out': 5.483047e-02, 'ffn2_norm_g': 6.827240e-02, 'ffn2_w_gate': 2.904785e-02, 'ffn2_w_up': 2.817069e-02, 'ffn2_w_down': 4.678702e-02, 'final_norm_g': 3.195633e+01}


def _to_microbatches(a, axis):
    t = _jnp.moveaxis(a, axis, 0)
    t = t.reshape((N_MICROBATCH, t.shape[0] // N_MICROBATCH) + t.shape[1:])
    return _jnp.moveaxis(t, 1, axis + 1)


def setup_inputs(seed: int = 0) -> dict:
    inp = _fwd_setup_inputs(seed)
    key = _jax.random.fold_in(_jax.random.key(seed), 7919)
    shape, _ = _output_shape()
    out = dict(inp)
    out["loss_target"] = _jax.random.normal(_jax.random.fold_in(key, 0), shape, _jnp.float32)
    for i, name in enumerate(TWIN_WEIGHTS):
        w = inp[name].astype(_jnp.float32)
        if MOMENT_SCALE is None:
            s = _jnp.sqrt(_jnp.mean(_jnp.square(w)) + 1e-30)
        else:
            s = MOMENT_SCALE[name]
        km, kv = _jax.random.split(_jax.random.fold_in(key, i + 1))
        out[name] = w
        out["m_" + name] = s * _jax.random.normal(km, w.shape, _jnp.float32)
        out["v_" + name] = (s * s) * _jax.random.uniform(kv, w.shape, _jnp.float32, 0.5, 1.5)
    if N_MICROBATCH > 1:
        for name, axis in PER_EXAMPLE_BATCH_AXIS.items():
            out[name] = _to_microbatches(out[name], axis)
    return {'x': out['x'], 'ffn1_norm_g': out['ffn1_norm_g'], 'ffn1_w_gate': out['ffn1_w_gate'], 'ffn1_w_up': out['ffn1_w_up'], 'ffn1_w_down': out['ffn1_w_down'], 'mix_norm_g': out['mix_norm_g'], 'w_in': out['w_in'], 'q_norm_g': out['q_norm_g'], 'k_norm_g': out['k_norm_g'], 'hgrn_lb_logits': out['hgrn_lb_logits'], 'hgrn_out_norm_g': out['hgrn_out_norm_g'], 'w_branch_attn': out['w_branch_attn'], 'w_branch_hgrn': out['w_branch_hgrn'], 'w_out': out['w_out'], 'ffn2_norm_g': out['ffn2_norm_g'], 'ffn2_w_gate': out['ffn2_w_gate'], 'ffn2_w_up': out['ffn2_w_up'], 'ffn2_w_down': out['ffn2_w_down'], 'final_norm_g': out['final_norm_g'], 'loss_target': out['loss_target'], 'm_ffn1_norm_g': out['m_ffn1_norm_g'], 'm_ffn1_w_gate': out['m_ffn1_w_gate'], 'm_ffn1_w_up': out['m_ffn1_w_up'], 'm_ffn1_w_down': out['m_ffn1_w_down'], 'm_mix_norm_g': out['m_mix_norm_g'], 'm_w_in': out['m_w_in'], 'm_q_norm_g': out['m_q_norm_g'], 'm_k_norm_g': out['m_k_norm_g'], 'm_hgrn_lb_logits': out['m_hgrn_lb_logits'], 'm_hgrn_out_norm_g': out['m_hgrn_out_norm_g'], 'm_w_branch_attn': out['m_w_branch_attn'], 'm_w_branch_hgrn': out['m_w_branch_hgrn'], 'm_w_out': out['m_w_out'], 'm_ffn2_norm_g': out['m_ffn2_norm_g'], 'm_ffn2_w_gate': out['m_ffn2_w_gate'], 'm_ffn2_w_up': out['m_ffn2_w_up'], 'm_ffn2_w_down': out['m_ffn2_w_down'], 'm_final_norm_g': out['m_final_norm_g'], 'v_ffn1_norm_g': out['v_ffn1_norm_g'], 'v_ffn1_w_gate': out['v_ffn1_w_gate'], 'v_ffn1_w_up': out['v_ffn1_w_up'], 'v_ffn1_w_down': out['v_ffn1_w_down'], 'v_mix_norm_g': out['v_mix_norm_g'], 'v_w_in': out['v_w_in'], 'v_q_norm_g': out['v_q_norm_g'], 'v_k_norm_g': out['v_k_norm_g'], 'v_hgrn_lb_logits': out['v_hgrn_lb_logits'], 'v_hgrn_out_norm_g': out['v_hgrn_out_norm_g'], 'v_w_branch_attn': out['v_w_branch_attn'], 'v_w_branch_hgrn': out['v_w_branch_hgrn'], 'v_w_out': out['v_w_out'], 'v_ffn2_norm_g': out['v_ffn2_norm_g'], 'v_ffn2_w_gate': out['v_ffn2_w_gate'], 'v_ffn2_w_up': out['v_ffn2_w_up'], 'v_ffn2_w_down': out['v_ffn2_w_down'], 'v_final_norm_g': out['v_final_norm_g']}


def _loss(weights, diff, rest, loss_target):
    with _jax.named_scope("forward"):
        args = {**rest, TWIN_DIFF_INPUT: diff, **{k: w.astype(_WEIGHT_DTYPES[k]) for k, w in weights.items()}}
        y = _forward(args)
    with _jax.named_scope("loss_head"):
        err = _jnp.square(y.astype(_jnp.float32) - loss_target)
        return 0.5 * _jnp.sum(_jnp.mean(err, axis=-1)) if err.ndim else 0.5 * err


def _adamw(w, g, m, v):
    m = ADAM_B1 * m + (1.0 - ADAM_B1) * g
    v = ADAM_B2 * v + (1.0 - ADAM_B2) * _jnp.square(g)
    m_hat = m / (1.0 - ADAM_B1 ** ADAM_STEP)
    v_hat = v / (1.0 - ADAM_B2 ** ADAM_STEP)
    delta = -ADAM_LR * (m_hat / (_jnp.sqrt(v_hat) + ADAM_EPS) + ADAM_WD * w)
    return delta, m, v


def reference(x, ffn1_norm_g, ffn1_w_gate, ffn1_w_up, ffn1_w_down, mix_norm_g, w_in, q_norm_g, k_norm_g, hgrn_lb_logits, hgrn_out_norm_g, w_branch_attn, w_branch_hgrn, w_out, ffn2_norm_g, ffn2_w_gate, ffn2_w_up, ffn2_w_down, final_norm_g, loss_target, m_ffn1_norm_g, m_ffn1_w_gate, m_ffn1_w_up, m_ffn1_w_down, m_mix_norm_g, m_w_in, m_q_norm_g, m_k_norm_g, m_hgrn_lb_logits, m_hgrn_out_norm_g, m_w_branch_attn, m_w_branch_hgrn, m_w_out, m_ffn2_norm_g, m_ffn2_w_gate, m_ffn2_w_up, m_ffn2_w_down, m_final_norm_g, v_ffn1_norm_g, v_ffn1_w_gate, v_ffn1_w_up, v_ffn1_w_down, v_mix_norm_g, v_w_in, v_q_norm_g, v_k_norm_g, v_hgrn_lb_logits, v_hgrn_out_norm_g, v_w_branch_attn, v_w_branch_hgrn, v_w_out, v_ffn2_norm_g, v_ffn2_w_gate, v_ffn2_w_up, v_ffn2_w_down, v_final_norm_g):
    given = dict(x=x, ffn1_norm_g=ffn1_norm_g, ffn1_w_gate=ffn1_w_gate, ffn1_w_up=ffn1_w_up, ffn1_w_down=ffn1_w_down, mix_norm_g=mix_norm_g, w_in=w_in, q_norm_g=q_norm_g, k_norm_g=k_norm_g, hgrn_lb_logits=hgrn_lb_logits, hgrn_out_norm_g=hgrn_out_norm_g, w_branch_attn=w_branch_attn, w_branch_hgrn=w_branch_hgrn, w_out=w_out, ffn2_norm_g=ffn2_norm_g, ffn2_w_gate=ffn2_w_gate, ffn2_w_up=ffn2_w_up, ffn2_w_down=ffn2_w_down, final_norm_g=final_norm_g, loss_target=loss_target, m_ffn1_norm_g=m_ffn1_norm_g, m_ffn1_w_gate=m_ffn1_w_gate, m_ffn1_w_up=m_ffn1_w_up, m_ffn1_w_down=m_ffn1_w_down, m_mix_norm_g=m_mix_norm_g, m_w_in=m_w_in, m_q_norm_g=m_q_norm_g, m_k_norm_g=m_k_norm_g, m_hgrn_lb_logits=m_hgrn_lb_logits, m_hgrn_out_norm_g=m_hgrn_out_norm_g, m_w_branch_attn=m_w_branch_attn, m_w_branch_hgrn=m_w_branch_hgrn, m_w_out=m_w_out, m_ffn2_norm_g=m_ffn2_norm_g, m_ffn2_w_gate=m_ffn2_w_gate, m_ffn2_w_up=m_ffn2_w_up, m_ffn2_w_down=m_ffn2_w_down, m_final_norm_g=m_final_norm_g, v_ffn1_norm_g=v_ffn1_norm_g, v_ffn1_w_gate=v_ffn1_w_gate, v_ffn1_w_up=v_ffn1_w_up, v_ffn1_w_down=v_ffn1_w_down, v_mix_norm_g=v_mix_norm_g, v_w_in=v_w_in, v_q_norm_g=v_q_norm_g, v_k_norm_g=v_k_norm_g, v_hgrn_lb_logits=v_hgrn_lb_logits, v_hgrn_out_norm_g=v_hgrn_out_norm_g, v_w_branch_attn=v_w_branch_attn, v_w_branch_hgrn=v_w_branch_hgrn, v_w_out=v_w_out, v_ffn2_norm_g=v_ffn2_norm_g, v_ffn2_w_gate=v_ffn2_w_gate, v_ffn2_w_up=v_ffn2_w_up, v_ffn2_w_down=v_ffn2_w_down, v_final_norm_g=v_final_norm_g)
    weights = {n: given[n] for n in TWIN_WEIGHTS}
    shared = {n: given[n] for n in SHARED_INPUTS}
    per_example = {n: given[n] for n in ['x']}
    grad_fn = _jax.value_and_grad(_loss, argnums=(0, 1))

    def one_microbatch(ex, loss_target):
        ex = dict(ex)
        diff = ex.pop(TWIN_DIFF_INPUT)
        return grad_fn(weights, diff, {**shared, **ex}, loss_target)

    if N_MICROBATCH == 1:
        loss, (grad_w, grad_x) = one_microbatch(per_example, given["loss_target"])
    else:
        def body(carry, xs):
            loss_sum, grad_sum = carry
            l_k, (gw_k, gx_k) = one_microbatch(xs[0], xs[1])
            with _jax.named_scope("update"):
                return (loss_sum + l_k, _jax.tree.map(_jnp.add, grad_sum, gw_k)), gx_k

        init = (_jnp.zeros((), _jnp.float32), _jax.tree.map(_jnp.zeros_like, weights))
        (loss, grad_w), grad_x = _jax.lax.scan(body, init, (per_example, given["loss_target"]))
    with _jax.named_scope("update"):
        delta_w, new_m, new_v = {}, {}, {}
        for n in TWIN_WEIGHTS:
            delta_w[n], new_m[n], new_v[n] = _adamw(weights[n], grad_w[n], given["m_" + n], given["v_" + n])
    return (loss, grad_x, *[grad_w[n] for n in TWIN_WEIGHTS], *[delta_w[n] for n in TWIN_WEIGHTS],
            *[new_m[n] for n in TWIN_WEIGHTS], *[new_v[n] for n in TWIN_WEIGHTS])
```

```python
import functools

import numpy as np
import jax
import jax.numpy as jnp
from jax import lax
from jax.experimental import pallas as pl
from jax.experimental.pallas import tpu as pltpu

F32 = jnp.float32
BF16 = jnp.bfloat16
CD = jnp.bfloat16

EPS = 1e-6
D_MODEL = 1024
HEAD_DIM = 64
GRID_W = 64
ROPE_THETA = 10000.0
CHUNK = 32
N_SHARD = 4
N_DEV = 8
VMEM_LIMIT = 56 * 1024 * 1024
LANES = 128
HG_TILE = 256

ADAM_LR = 0.001
ADAM_B1 = 0.9
ADAM_B2 = 0.999
ADAM_EPS = 1e-08
ADAM_WD = 0.01
ADAM_STEP = 10

NN = (((1,), (0,)), ((), ()))
NT = (((1,), (1,)), ((), ()))
TN = (((0,), (0,)), ((), ()))
MESH = pl.DeviceIdType.MESH
ANY = pl.BlockSpec(memory_space=pl.ANY)


def _mm(a, b, dn):
    return lax.dot_general(a.astype(CD), b.astype(CD), dn, preferred_element_type=F32)


def _split3(x):
    hi = x.astype(BF16)
    r = x - hi.astype(F32)
    mid = r.astype(BF16)
    lo = (r - mid.astype(F32)).astype(BF16)
    return hi, mid, lo


def _xdot(x, m):
    hi, mid, lo = _split3(x)
    d = lambda t: lax.dot_general(t, m, NN, preferred_element_type=F32)
    return d(hi) + d(mid) + d(lo)


def _xdot_l(m, x):
    hi, mid, lo = _split3(x)
    d = lambda t: lax.dot_general(m, t, NN, preferred_element_type=F32)
    return d(hi) + d(mid) + d(lo)


def _params(n_grid):
    return pltpu.CompilerParams(dimension_semantics=("arbitrary",) * n_grid, vmem_limit_bytes=VMEM_LIMIT)


def _sigmoid(x):
    return jax.nn.sigmoid(x)


def _np_blocksum(n):
    i = np.arange(n)
    return (i[:, None] // HEAD_DIM == i[None, :] // HEAD_DIM).astype(np.float32)


def _np_swap32(n):
    i = np.arange(n)
    partner = np.where(i % HEAD_DIM < HEAD_DIM // 2, i + HEAD_DIM // 2, i - HEAD_DIM // 2)
    m = np.zeros((n, n), np.float32)
    m[i, partner] = 1.0
    return m


def _np_expand_q():
    m = np.zeros((512, 1024), np.float32)
    for h in range(8):
        g = h // 4
        for d in range(HEAD_DIM):
            m[64 * h + d, 128 * h + 64 * g + d] = 1.0
    return m


def _np_bcast_head():
    m = np.zeros((512, 1024), np.float32)
    for h in range(8):
        m[64 * h, 128 * h:128 * h + 128] = 1.0
    return m


def _np_swap_halves():
    m = np.zeros((128, 128), np.float32)
    i = np.arange(128)
    m[i, (i + 64) % 128] = 1.0
    return m


def _np_hgrn_cums(t, rev):
    r = np.arange(t)[:, None]
    c = np.arange(t)[None, :]
    same = (r // CHUNK) == (c // CHUNK)
    if not rev:
        cum = same & (c <= r)
        mid = same & (c % CHUNK <= CHUNK // 2 - 1)
    else:
        cum = same & (c >= r)
        mid = same & (c % CHUNK >= CHUNK // 2)
    return np.concatenate([cum, mid, same], axis=0).astype(np.float32)


def _bf(a):
    return jnp.asarray(a, dtype=BF16)


def _rope_tables(seq_len):
    rows = seq_len // GRID_W
    row = jnp.repeat(jnp.arange(rows, dtype=F32), GRID_W)
    col = jnp.tile(jnp.arange(GRID_W, dtype=F32), rows)
    n_freq = HEAD_DIM // 4
    inv = ROPE_THETA ** (-jnp.arange(n_freq, dtype=F32) / n_freq)
    ang = jnp.concatenate([row[:, None] * inv, col[:, None] * inv], axis=-1)
    cos, sin = jnp.cos(ang), jnp.sin(ang)
    c64 = jnp.concatenate([cos, cos], axis=-1)
    s64 = jnp.concatenate([-sin, sin], axis=-1)
    return jnp.tile(c64, (1, 2)), jnp.tile(s64, (1, 2))


def _ffn_fwd(x, g, wg, wu, wd, tm):
    s, d = x.shape
    nsh, _, fs = wg.shape

    def body(x_ref, g_ref, wg_ref, wu_ref, wd_ref, xo_ref, a_ref, b_ref, hb_ref, acc, hs):
        j = pl.program_id(1)

        @pl.when(j == 0)
        def _():
            xv = x_ref[...]
            r = lax.rsqrt(jnp.mean(xv * xv, axis=-1, keepdims=True) + EPS)
            h = (xv * r * g_ref[...]).astype(CD)
            hs[...] = h
            hb_ref[...] = h
            acc[...] = jnp.zeros_like(acc)

        h = hs[...]
        a = _mm(h, wg_ref[0], NN)
        b = _mm(h, wu_ref[0], NN)
        f = a * _sigmoid(a) * b
        acc[...] += _mm(f, wd_ref[0], NN)
        a_ref[0] = a.astype(CD)
        b_ref[0] = b.astype(CD)

        @pl.when(j == nsh - 1)
        def _():
            xo_ref[...] = x_ref[...] + 0.5 * acc[...]

    return pl.pallas_call(
        body, name="ffn_fwd", grid=(s // tm, nsh),
        in_specs=[pl.BlockSpec((tm, d), lambda i, j: (i, 0)), pl.BlockSpec((1, d), lambda i, j: (0, 0)),
                  pl.BlockSpec((1, d, fs), lambda i, j: (j, 0, 0)), pl.BlockSpec((1, d, fs), lambda i, j: (j, 0, 0)),
                  pl.BlockSpec((1, fs, d), lambda i, j: (j, 0, 0))],
        out_specs=[pl.BlockSpec((tm, d), lambda i, j: (i, 0)), pl.BlockSpec((1, tm, fs), lambda i, j: (j, i, 0)),
                   pl.BlockSpec((1, tm, fs), lambda i, j: (j, i, 0)), pl.BlockSpec((tm, d), lambda i, j: (i, 0))],
        out_shape=[jax.ShapeDtypeStruct((s, d), F32), jax.ShapeDtypeStruct((nsh, s, fs), CD),
                   jax.ShapeDtypeStruct((nsh, s, fs), CD), jax.ShapeDtypeStruct((s, d), CD)],
        scratch_shapes=[pltpu.VMEM((tm, d), F32), pltpu.VMEM((tm, d), CD)],
        compiler_params=_params(2),
    )(x, g, wg, wu, wd)


def _ffn_bwd(dout, x, g, a, b, wg, wu, wd, tm):
    s, d = x.shape
    nsh, _, fs = wg.shape

    def body(do_ref, x_ref, g_ref, a_ref, b_ref, wg_ref, wu_ref, wd_ref, dx_ref, da_ref, db_ref, f_ref, dg_ref, dh):
        i = pl.program_id(0)
        j = pl.program_id(1)

        @pl.when(j == 0)
        def _():
            dh[...] = jnp.zeros_like(dh)

        @pl.when((i == 0) & (j == 0))
        def _():
            dg_ref[...] = jnp.zeros_like(dg_ref)

        av = a_ref[0].astype(F32)
        bv = b_ref[0].astype(F32)
        sg = _sigmoid(av)
        sl = av * sg
        df = 0.5 * _mm(do_ref[...], wd_ref[0], NT)
        da = df * bv * (sg * (1.0 + av * (1.0 - sg)))
        db = df * sl
        dh[...] += _mm(da, wg_ref[0], NT) + _mm(db, wu_ref[0], NT)
        da_ref[0] = da.astype(CD)
        db_ref[0] = db.astype(CD)
        f_ref[0] = (sl * bv).astype(CD)

        @pl.when(j == nsh - 1)
        def _():
            xv = x_ref[...]
            r = lax.rsqrt(jnp.mean(xv * xv, axis=-1, keepdims=True) + EPS)
            dhv = dh[...]
            u = dhv * g_ref[...]
            dx_ref[...] = do_ref[...] + r * u - xv * (r * r * r) * jnp.mean(u * xv, axis=-1, keepdims=True)
            dg_ref[...] += jnp.sum(dhv * xv * r, axis=0, keepdims=True)

    act = pl.BlockSpec((1, tm, fs), lambda i, j: (j, i, 0))
    row = pl.BlockSpec((tm, d), lambda i, j: (i, 0))
    return pl.pallas_call(
        body, name="ffn_bwd", grid=(s // tm, nsh),
        in_specs=[row, row, pl.BlockSpec((1, d), lambda i, j: (0, 0)), act, act,
                  pl.BlockSpec((1, d, fs), lambda i, j: (j, 0, 0)), pl.BlockSpec((1, d, fs), lambda i, j: (j, 0, 0)),
                  pl.BlockSpec((1, fs, d), lambda i, j: (j, 0, 0))],
        out_specs=[row, act, act, act, pl.BlockSpec((1, d), lambda i, j: (0, 0))],
        out_shape=[jax.ShapeDtypeStruct((s, d), F32), jax.ShapeDtypeStruct((nsh, s, fs), CD),
                   jax.ShapeDtypeStruct((nsh, s, fs), CD), jax.ShapeDtypeStruct((nsh, s, fs), CD),
                   jax.ShapeDtypeStruct((1, d), F32)],
        scratch_shapes=[pltpu.VMEM((tm, d), F32)],
        compiler_params=_params(2),
    )(dout, x, g, a, b, wg, wu, wd)


def _tn_call(name, operands, in_specs, out_shape, out_spec, grid, acc_shape, pick, scale=1.0):
    nk = grid[-1]
    n_in = len(operands)

    def body(*refs):
        out_ref, acc = refs[n_in], refs[n_in + 1]
        k = pl.program_id(len(grid) - 1)

        @pl.when(k == 0)
        def _():
            acc[...] = jnp.zeros_like(acc)

        pick(refs[:n_in], acc)

        @pl.when(k == nk - 1)
        def _():
            res = acc[...] if scale == 1.0 else acc[...] * scale
            out_ref[...] = res.reshape(out_ref.shape)

    return pl.pallas_call(
        body, name=name, grid=grid, in_specs=in_specs, out_specs=out_spec, out_shape=out_shape,
        scratch_shapes=[pltpu.VMEM(acc_shape, F32)], compiler_params=_params(len(grid)),
    )(*operands)


def _dw_shared_a(name, a, b3, tk):
    s, m = a.shape
    nj, _, n = b3.shape

    def pick(refs, acc):
        acc[...] += _mm(refs[0][...], refs[1][0], TN)

    return _tn_call(name, (a, b3),
                    [pl.BlockSpec((tk, m), lambda j, k: (k, 0)), pl.BlockSpec((1, tk, n), lambda j, k: (j, k, 0))],
                    jax.ShapeDtypeStruct((nj, m, n), F32), pl.BlockSpec((1, m, n), lambda j, k: (j, 0, 0)),
                    (nj, s // tk), (m, n), pick)


def _dw_shared_b(name, a3, b, tk, scale):
    nj, s, m = a3.shape
    n = b.shape[1]

    def pick(refs, acc):
        acc[...] += _mm(refs[0][0], refs[1][...], TN)

    return _tn_call(name, (a3, b),
                    [pl.BlockSpec((1, tk, m), lambda j, k: (j, k, 0)), pl.BlockSpec((tk, n), lambda j, k: (k, 0))],
                    jax.ShapeDtypeStruct((nj, m, n), F32), pl.BlockSpec((1, m, n), lambda j, k: (j, 0, 0)),
                    (nj, s // tk), (m, n), pick, scale)


def _dw_colblocks(name, a, b, nj, tk):
    s, m = a.shape
    n = b.shape[1] // nj

    def pick(refs, acc):
        acc[...] += _mm(refs[0][...], refs[1][...], TN)

    return _tn_call(name, (a, b),
                    [pl.BlockSpec((tk, m), lambda j, k: (k, 0)), pl.BlockSpec((tk, n), lambda j, k: (k, j))],
                    jax.ShapeDtypeStruct((nj, m, n), F32), pl.BlockSpec((1, m, n), lambda j, k: (j, 0, 0)),
                    (nj, s // tk), (m, n), pick)


DP_WIDTHS = (768, 512, 512, 512, 512, 512, 2048)
DP_CHUNK = 256


def _dp_chunk_maps():
    starts, counts, off = [], [], 0
    for w in DP_WIDTHS:
        starts.append(off // DP_CHUNK)
        counts.append(w // DP_CHUNK)
        off += w
    return starts, counts


def _dp_specs(tm, row_axis, chunk_axis):
    starts, counts = _dp_chunk_maps()
    specs = []
    for st, cnt in zip(starts, counts):
        def imap(*ids, st=st, cnt=cnt):
            return (ids[row_axis], jnp.clip(ids[chunk_axis] - st, 0, cnt - 1))
        specs.append(pl.BlockSpec((tm, DP_CHUNK), imap))
    return specs


def _dp_select(n, refs, fn):
    starts, counts = _dp_chunk_maps()
    for ref, st, cnt in zip(refs, starts, counts):
        @pl.when((n >= st) & (n < st + cnt))
        def _(ref=ref):
            fn(ref)


def _dw_in(dps, hb, tk):
    s, d = hb.shape
    n_chunks = sum(DP_WIDTHS) // DP_CHUNK

    def pick(refs, acc):
        def add(ref):
            acc[...] += _mm(ref[...], refs[7][...], TN)

        _dp_select(pl.program_id(0), refs[:7], add)

    return _tn_call("dw_in", (*dps, hb),
                    _dp_specs(tk, 1, 0) + [pl.BlockSpec((tk, d), lambda n, k: (k, 0))],
                    jax.ShapeDtypeStruct((n_chunks * DP_CHUNK, d), F32), pl.BlockSpec((DP_CHUNK, d), lambda n, k: (n, 0)),
                    (n_chunks, s // tk), (DP_CHUNK, d), pick)


def _mix_in_fwd(x, g, w_t, tm):
    s, d = x.shape
    n_chunks = w_t.shape[0] // DP_CHUNK
    bounds = (3, 13)

    def body(x_ref, g_ref, w_ref, qkv_ref, hg_ref, gt_ref, hb_ref, hs):
        n = pl.program_id(1)

        @pl.when(n == 0)
        def _():
            xv = x_ref[...]
            r = lax.rsqrt(jnp.mean(xv * xv, axis=-1, keepdims=True) + EPS)
            h = (xv * r * g_ref[...]).astype(CD)
            hs[...] = h
            hb_ref[...] = h

        res = _mm(hs[...], w_ref[...], NT)

        @pl.when(n < bounds[0])
        def _():
            qkv_ref[...] = res

        @pl.when((n >= bounds[0]) & (n < bounds[1]))
        def _():
            hg_ref[...] = res

        @pl.when(n >= bounds[1])
        def _():
            gt_ref[...] = res

    blk = lambda lo, hi: pl.BlockSpec((tm, DP_CHUNK), lambda i, n: (i, jnp.clip(n - lo, 0, hi - lo - 1)))
    return pl.pallas_call(
        body, name="mix_in_fwd", grid=(s // tm, n_chunks),
        in_specs=[pl.BlockSpec((tm, d), lambda i, n: (i, 0)), pl.BlockSpec((1, d), lambda i, n: (0, 0)),
                  pl.BlockSpec((DP_CHUNK, d), lambda i, n: (n, 0))],
        out_specs=[blk(0, 3), blk(3, 13), blk(13, 21), pl.BlockSpec((tm, d), lambda i, n: (i, 0))],
        out_shape=[jax.ShapeDtypeStruct((s, 768), F32), jax.ShapeDtypeStruct((s, 2560), F32),
                   jax.ShapeDtypeStruct((s, 2048), F32), jax.ShapeDtypeStruct((s, d), CD)],
        scratch_shapes=[pltpu.VMEM((tm, d), CD)],
        compiler_params=_params(2),
    )(x, g, w_t)


def _mix_in_bwd(dps, w_t, x, dres, g, tm):
    s, d = x.shape
    n_chunks = w_t.shape[0] // DP_CHUNK

    def body(*refs):
        dp_refs = refs[:7]
        w_ref, x_ref, dr_ref, g_ref, dx_ref, dg_ref, dh = refs[7:]
        i = pl.program_id(0)
        n = pl.program_id(1)

        @pl.when(n == 0)
        def _():
            dh[...] = jnp.zeros_like(dh)

        @pl.when((i == 0) & (n == 0))
        def _():
            dg_ref[...] = jnp.zeros_like(dg_ref)

        def add(ref):
            dh[...] += _mm(ref[...], w_ref[...], NN)

        _dp_select(n, dp_refs, add)

        @pl.when(n == n_chunks - 1)
        def _():
            xv = x_ref[...]
            r = lax.rsqrt(jnp.mean(xv * xv, axis=-1, keepdims=True) + EPS)
            dhv = dh[...]
            u = dhv * g_ref[...]
            dx_ref[...] = dr_ref[...] + r * u - xv * (r * r * r) * jnp.mean(u * xv, axis=-1, keepdims=True)
            dg_ref[...] += jnp.sum(dhv * xv * r, axis=0, keepdims=True)

    row = pl.BlockSpec((tm, d), lambda i, n: (i, 0))
    vec = pl.BlockSpec((1, d), lambda i, n: (0, 0))
    return pl.pallas_call(
        body, name="mix_in_bwd", grid=(s // tm, n_chunks),
        in_specs=_dp_specs(tm, 0, 1) + [pl.BlockSpec((DP_CHUNK, d), lambda i, n: (n, 0)), row, row, vec],
        out_specs=[row, vec],
        out_shape=[jax.ShapeDtypeStruct((s, d), F32), jax.ShapeDtypeStruct((1, d), F32)],
        scratch_shapes=[pltpu.VMEM((tm, d), F32)],
        compiler_params=_params(2),
    )(*dps, w_t, x, dres, g)


def _headnorm_rope(x, gain, cos, sin, blocksum, swap):
    ss = _xdot(x * x, blocksum)
    r = lax.rsqrt(ss * (1.0 / HEAD_DIM) + EPS)
    y = x * r * gain
    return y * cos + _xdot(y, swap) * sin, r


def _headnorm_rope_bwd(dz, x, gain, cos, sin, blocksum, swap):
    ss = _xdot(x * x, blocksum)
    r = lax.rsqrt(ss * (1.0 / HEAD_DIM) + EPS)
    dy = dz * cos + _xdot(dz * sin, swap)
    u = dy * gain
    mean_ux = _xdot(u * x, blocksum) * (1.0 / HEAD_DIM)
    dx = r * u - x * (r * r * r) * mean_ux
    return dx, jnp.sum(dy * x * r, axis=0, keepdims=True)


def _qk_prep(pqkv, gq, gk, cos2, sin2, tm):
    s = pqkv.shape[0]
    bs512, sw512, eq, swh = _bf(_np_blocksum(512)), _bf(_np_swap32(512)), _bf(_np_expand_q()), _bf(_np_swap_halves())

    def body(q_ref, kv_ref, gq_ref, gk_ref, c_ref, s_ref, bs_ref, sw_ref, eq_ref, swh_ref, qe_ref, k_ref, v_ref, vs_ref):
        c2, s2 = c_ref[...], s_ref[...]
        c8, s8 = jnp.tile(c2, (1, 4)), jnp.tile(s2, (1, 4))
        bs, sw = bs_ref[...], sw_ref[...]
        zq, _ = _headnorm_rope(q_ref[...], gq_ref[...], c8, s8, bs, sw)
        qe_ref[...] = _mm(zq * (HEAD_DIM ** -0.5), eq_ref[...], NN).astype(CD)
        kv = kv_ref[...]
        zk, _ = _headnorm_rope(kv[:, :LANES], gk_ref[...], c2, s2, bs[:LANES, :LANES], sw[:LANES, :LANES])
        k_ref[...] = zk.astype(CD)
        v = kv[:, LANES:]
        v_ref[...] = v.astype(CD)
        vs_ref[...] = _mm(v, swh_ref[...], NN).astype(CD)

    full = lambda a: pl.BlockSpec(a.shape, lambda i: (0,) * a.ndim)
    tab = pl.BlockSpec((tm, LANES), lambda i: (i, 0))
    return pl.pallas_call(
        body, name="qk_prep", grid=(s // tm,),
        in_specs=[pl.BlockSpec((tm, 512), lambda i: (i, 0)), pl.BlockSpec((tm, 256), lambda i: (i, 2)),
                  full(gq), full(gk), tab, tab, full(bs512), full(sw512), full(eq), full(swh)],
        out_specs=[pl.BlockSpec((tm, 1024), lambda i: (i, 0)), tab, tab, tab],
        out_shape=[jax.ShapeDtypeStruct((s, 1024), CD)] + [jax.ShapeDtypeStruct((s, LANES), CD)] * 3,
        compiler_params=_params(1),
    )(pqkv, pqkv, gq, gk, cos2, sin2, bs512, sw512, eq, swh)


def _qk_prep_bwd(pqkv, dqe, dk, dv, gq, gk, cos2, sin2, tm):
    s = pqkv.shape[0]
    bs512, sw512, eqt = _bf(_np_blocksum(512)), _bf(_np_swap32(512)), _bf(_np_expand_q().T)

    def body(q_ref, kv_ref, dqe_ref, dk_ref, dv_ref, gq_ref, gk_ref, c_ref, s_ref, bs_ref, sw_ref, eqt_ref,
             dp_ref, dgq_ref, dgk_ref):
        @pl.when(pl.program_id(0) == 0)
        def _():
            dgq_ref[...] = jnp.zeros_like(dgq_ref)
            dgk_ref[...] = jnp.zeros_like(dgk_ref)

        c2, s2 = c_ref[...], s_ref[...]
        c8, s8 = jnp.tile(c2, (1, 4)), jnp.tile(s2, (1, 4))
        bs, sw = bs_ref[...], sw_ref[...]
        dzq = _xdot(dqe_ref[...], eqt_ref[...]) * (HEAD_DIM ** -0.5)
        dxq, dgq = _headnorm_rope_bwd(dzq, q_ref[...], gq_ref[...], c8, s8, bs, sw)
        kv = kv_ref[...]
        dxk, dgk = _headnorm_rope_bwd(dk_ref[...], kv[:, :LANES], gk_ref[...], c2, s2, bs[:LANES, :LANES], sw[:LANES, :LANES])
        dp_ref[...] = jnp.concatenate([dxq, dxk, dv_ref[...]], axis=1).astype(CD)
        dgq_ref[...] += dgq
        dgk_ref[...] += dgk

    full = lambda a: pl.BlockSpec(a.shape, lambda i: (0,) * a.ndim)
    tab = pl.BlockSpec((tm, LANES), lambda i: (i, 0))
    return pl.pallas_call(
        body, name="qk_prep_bwd", grid=(s // tm,),
        in_specs=[pl.BlockSpec((tm, 512), lambda i: (i, 0)), pl.BlockSpec((tm, 256), lambda i: (i, 2)),
                  pl.BlockSpec((tm, 1024), lambda i: (i, 0)), tab, tab, full(gq), full(gk), tab, tab,
                  full(bs512), full(sw512), full(eqt)],
        out_specs=[pl.BlockSpec((tm, 768), lambda i: (i, 0)), pl.BlockSpec((1, 512), lambda i: (0, 0)),
                   pl.BlockSpec((1, LANES), lambda i: (0, 0))],
        out_shape=[jax.ShapeDtypeStruct((s, 768), CD), jax.ShapeDtypeStruct((1, 512), F32),
                   jax.ShapeDtypeStruct((1, LANES), F32)],
        compiler_params=_params(1),
    )(pqkv, pqkv, dqe, dk, dv, gq, gk, cos2, sin2, bs512, sw512, eqt)


def _attn_fwd(qe, k, v, vs, tq):
    s = k.shape[0]

    def body(qa_ref, qb_ref, k_ref, v_ref, vs_ref, o_ref, lse_ref):
        m = pl.program_id(0)
        grp = m // 2
        kk = k_ref[...]
        outs = []
        for idx, q_ref in enumerate((qa_ref, qb_ref)):
            sc = _mm(q_ref[...], kk, NT)
            mx = jnp.max(sc, axis=-1, keepdims=True)
            e = jnp.exp(sc - mx)
            l = jnp.sum(e, axis=-1, keepdims=True)
            lse_ref[idx] = mx + jnp.log(l)
            p = e * (1.0 / l)
            vsel = jnp.where(grp != idx, vs_ref[...], v_ref[...])
            outs.append(_mm(p, vsel, NN))
        lane = lax.broadcasted_iota(jnp.int32, (1, LANES), 1)
        o_ref[...] = jnp.where(lane < HEAD_DIM, outs[0], outs[1])

    kv = pl.BlockSpec((s, LANES), lambda m, i: (0, 0))
    return pl.pallas_call(
        body, name="attn_fwd", grid=(4, s // tq),
        in_specs=[pl.BlockSpec((tq, LANES), lambda m, i: (i, 2 * m)), pl.BlockSpec((tq, LANES), lambda m, i: (i, 2 * m + 1)),
                  kv, kv, kv],
        out_specs=[pl.BlockSpec((tq, LANES), lambda m, i: (i, m)), pl.BlockSpec((2, tq, 1), lambda m, i: (m, i, 0))],
        out_shape=[jax.ShapeDtypeStruct((s, 512), F32), jax.ShapeDtypeStruct((8, s, 1), F32)],
        compiler_params=_params(2),
    )(qe, qe, k, v, vs)


def _attn_bwd(qe, k, v, doe, delta, lse, tq):
    s = k.shape[0]

    def body(q_ref, k_ref, v_ref, do_ref, dl_ref, lse_ref, dq_ref, dk_ref, dv_ref):
        @pl.when((pl.program_id(0) == 0) & (pl.program_id(1) == 0))
        def _():
            dk_ref[...] = jnp.zeros_like(dk_ref)
            dv_ref[...] = jnp.zeros_like(dv_ref)

        q, kk, do = q_ref[...], k_ref[...], do_ref[...]
        p = jnp.exp(_mm(q, kk, NT) - lse_ref[0])
        dp = _mm(do, v_ref[...], NT)
        ds = p * (dp - jnp.max(dl_ref[...], axis=-1, keepdims=True))
        dq_ref[...] = _mm(ds, kk, NN)
        dk_ref[...] += _mm(ds, q, TN)
        dv_ref[...] += _mm(p, do, TN)

    kv = pl.BlockSpec((s, LANES), lambda h, i: (0, 0))
    blk = pl.BlockSpec((tq, LANES), lambda h, i: (i, h))
    return pl.pallas_call(
        body, name="attn_bwd", grid=(8, s // tq),
        in_specs=[blk, kv, kv, blk, blk, pl.BlockSpec((1, tq, 1), lambda h, i: (h, i, 0))],
        out_specs=[blk, kv, kv],
        out_shape=[jax.ShapeDtypeStruct((s, 1024), F32), jax.ShapeDtypeStruct((s, LANES), F32),
                   jax.ShapeDtypeStruct((s, LANES), F32)],
        compiler_params=_params(2),
    )(qe, k, v, doe, delta, lse)


@jax.custom_vjp
def _mm_nn(a, b):
    return _mm(a, b, NN)


_mm_nn.defvjp(lambda a, b: (_mm(a, b, NN), (a, b)),
              lambda res, g: (_mm(g, res[1], NT), _mm(res[0], g, TN)))


@jax.custom_vjp
def _mm_nt(a, b):
    return _mm(a, b, NT)


_mm_nt.defvjp(lambda a, b: (_mm(a, b, NT), (a, b)),
              lambda res, g: (_mm(g, res[1], NN), _mm(g, res[0], TN)))


@jax.custom_vjp
def _mm_tn(a, b):
    return _mm(a, b, TN)


_mm_tn.defvjp(lambda a, b: (_mm(a, b, TN), (a, b)),
              lambda res, g: (_mm(res[1], g, NT), _mm(res[0], g, NN)))


@jax.custom_vjp
def _cmm(m, mt, x):
    return _xdot_l(m, x)


_cmm.defvjp(lambda m, mt, x: (_xdot_l(m, x), (m, mt)),
            lambda res, g: (jnp.zeros_like(res[0]), jnp.zeros_like(res[1]), _xdot_l(res[1], g)))


def _hgrn_masks(t, rev):
    r = lax.broadcasted_iota(jnp.int32, (t, t), 0)
    c = lax.broadcasted_iota(jnp.int32, (t, t), 1)
    same = jnp.right_shift(r, 5) == jnp.right_shift(c, 5)
    tri = same & ((c >= r) if rev else (c <= r))
    pr = lax.broadcasted_iota(jnp.int32, (LANES, LANES), 0)
    pc = lax.broadcasted_iota(jnp.int32, (LANES, LANES), 1)
    diag = jnp.right_shift(pr, 6) == jnp.right_shift(pc, 6)
    return tri, diag


def _hgrn_gates(xf, lb):
    f = lb + (1.0 - lb) * _sigmoid(xf)
    return 1.0 - f, jnp.log(f)


def _chunk_decay(bl, r0):
    dec = jnp.exp(bl[r0:r0 + CHUNK])
    return jnp.concatenate([dec] * (LANES // CHUNK), axis=0)


def _hgrn_dir(xq, xf, v, lb, state, cm, cmt, tri, diag, rev):
    t = xq.shape[0]
    lo = lax.broadcasted_iota(jnp.int32, (1, LANES), 1) < HEAD_DIM
    q = xq * _sigmoid(xq)
    k, lf = _hgrn_gates(xf, lb)
    cs = _cmm(cm, cmt, lf)
    b, bm, bl = cs[:t], cs[t:2 * t], cs[2 * t:]
    qd = q * jnp.exp(b - bm)
    kd = k * jnp.exp(bm - b)
    kc = k * jnp.exp(bl - b)
    qe = q * jnp.exp(b)
    o = jnp.zeros((t, LANES), F32)
    for half in (lo, ~lo):
        a = _mm_nt(jnp.where(half, qd, 0.0), kd)
        a = jnp.where(tri, a, 0.0)
        o = o + jnp.where(half, _mm_nn(a, v), 0.0)
    n_ch = t // CHUNK
    inter = [None] * n_ch
    for c in (range(n_ch - 1, -1, -1) if rev else range(n_ch)):
        r0 = c * CHUNK
        inter[c] = _mm_nt(qe[r0:r0 + CHUNK], state)
        contrib = _mm_tn(v[r0:r0 + CHUNK], kc[r0:r0 + CHUNK])
        state = _chunk_decay(bl, r0) * state + jnp.where(diag, contrib, 0.0)
    return o + jnp.concatenate(inter, axis=0), state


def _hgrn_state(xf, v, lb, state, cm, diag, rev):
    t = xf.shape[0]
    k, lf = _hgrn_gates(xf, lb)
    cs = _xdot_l(cm, lf)
    b, bl = cs[:t], cs[2 * t:]
    kc = k * jnp.exp(bl - b)
    n_ch = t // CHUNK
    for c in (range(n_ch - 1, -1, -1) if rev else range(n_ch)):
        r0 = c * CHUNK
        contrib = _mm(v[r0:r0 + CHUNK], kc[r0:r0 + CHUNK], TN)
        state = _chunk_decay(bl, r0) * state + jnp.where(diag, contrib, 0.0)
    return state


def _hgrn_lower_bounds(l):
    out = []
    for d in (0, 1):
        l0, l1 = l[2 * d:2 * d + 1, :], l[2 * d + 1:2 * d + 2, :]
        mx = jnp.maximum(l0, l1)
        e0, e1 = jnp.exp(l0 - mx), jnp.exp(l1 - mx)
        out.append(e0 / (e0 + e1))
    return out


def _hgrn_consts(t):
    cf, cb = _np_hgrn_cums(t, False), _np_hgrn_cums(t, True)
    return (_bf(cf), _bf(cf.T), _bf(cb), _bf(cb.T), _bf(_np_blocksum(LANES)))


def _hgrn_fwd(ph, lbl, ng):
    s = ph.shape[0]
    t = min(HG_TILE, s)
    nt = s // t
    consts = _hgrn_consts(t)

    def body(xq_ref, xff_ref, xfb_ref, xi_ref, xg_ref, lbl_ref, ng_ref, cf_ref, cft_ref, cb_ref, cbt_ref, bs_ref,
             o_ref, pre_ref, acc):
        lbf, lbb = _hgrn_lower_bounds(lbl_ref)
        tri_f, diag = _hgrn_masks(t, False)
        tri_b, _ = _hgrn_masks(t, True)
        zero = jnp.zeros((LANES, LANES), F32)

        def rows_of(i):
            return pl.ds(pl.multiple_of(i * t, t), t)

        def step_f(i, state):
            rows = rows_of(i)
            o, state = _hgrn_dir(xq_ref[rows, :], xff_ref[rows, :], xi_ref[rows, :], lbf, state,
                                 cf_ref[...], cft_ref[...], tri_f, diag, False)
            acc[rows, :] = o
            return state

        lax.fori_loop(0, nt, step_f, zero)

        def step_b(i, state):
            rows = rows_of(nt - 1 - i)
            o, state = _hgrn_dir(xq_ref[rows, :], xfb_ref[rows, :], xi_ref[rows, :], lbb, state,
                                 cb_ref[...], cbt_ref[...], tri_b, diag, True)
            acc[rows, :] += o
            return state

        lax.fori_loop(0, nt, step_b, zero)

        def step_n(i, carry):
            rows = rows_of(i)
            o = acc[rows, :]
            ss = _xdot(o * o, bs_ref[...])
            r = lax.rsqrt(ss * (1.0 / HEAD_DIM) + EPS)
            xg = xg_ref[rows, :]
            pre_ref[rows, :] = o
            o_ref[rows, :] = (o * r * ng_ref[...]) * (xg * _sigmoid(xg))
            return carry

        lax.fori_loop(0, nt, step_n, 0)

    col = lambda off: pl.BlockSpec((s, LANES), lambda m: (0, off + m))
    full = lambda a: pl.BlockSpec(a.shape, lambda m: (0,) * a.ndim)
    return pl.pallas_call(
        body, name="hgrn_fwd", grid=(4,),
        in_specs=[col(0), col(4), col(8), col(12), col(16), pl.BlockSpec((4, LANES), lambda m: (0, m)),
                  pl.BlockSpec((1, LANES), lambda m: (0, m))] + [full(c) for c in consts],
        out_specs=[col(0), col(0)],
        out_shape=[jax.ShapeDtypeStruct((s, 512), F32), jax.ShapeDtypeStruct((s, 512), F32)],
        scratch_shapes=[pltpu.VMEM((s, LANES), F32)],
        compiler_params=_params(1),
    )(ph, ph, ph, ph, ph, lbl, ng, *consts)


def _hgrn_bwd(ph, pre, dout, lbl, ng):
    s = ph.shape[0]
    t = min(HG_TILE, s)
    nt = s // t
    consts = _hgrn_consts(t)

    def body(xq_ref, xff_ref, xfb_ref, xi_ref, xg_ref, pre_ref, do_ref, lbl_ref, ng_ref,
             cf_ref, cft_ref, cb_ref, cbt_ref, bs_ref,
             dq_ref, dff_ref, dfb_ref, di_ref, dg_ref, dlb_ref, dng_ref, dpre, dq_acc, dv_acc, states):
        lbf, lbb = _hgrn_lower_bounds(lbl_ref)
        tri_f, diag = _hgrn_masks(t, False)
        tri_b, _ = _hgrn_masks(t, True)
        zero = jnp.zeros((LANES, LANES), F32)
        zrow = jnp.zeros((1, LANES), F32)

        def rows_of(i):
            return pl.ds(pl.multiple_of(i * t, t), t)

        def step_n(i, dng):
            rows = rows_of(i)
            o, xg, do = pre_ref[rows, :], xg_ref[rows, :], do_ref[rows, :]
            bs = bs_ref[...]
            r = lax.rsqrt(_xdot(o * o, bs) * (1.0 / HEAD_DIM) + EPS)
            sg = _sigmoid(xg)
            gate = xg * sg
            don = do * gate
            dg_ref[rows, :] = (do * (o * r * ng_ref[...]) * (sg * (1.0 + xg * (1.0 - sg)))).astype(CD)
            u = don * ng_ref[...]
            dpre[rows, :] = r * u - o * (r * r * r) * (_xdot(u * o, bs) * (1.0 / HEAD_DIM))
            return dng + jnp.sum(don * o * r, axis=0, keepdims=True)

        dng_ref[...] = lax.fori_loop(0, nt, step_n, zrow)

        def sweep(xf_ref, df_ref, lb, cm_ref, cmt_ref, tri, rev, first):
            tile = (lambda i: nt - 1 - i) if rev else (lambda i: i)

            def step_s(i, state):
                states[tile(i)] = state
                rows = rows_of(tile(i))
                return _hgrn_state(xf_ref[rows, :], xi_ref[rows, :], lb, state, cm_ref[...], diag, rev)

            lax.fori_loop(0, nt, step_s, zero)

            def step_g(i, carry):
                dstate, dlb = carry
                ti = tile(nt - 1 - i)
                rows = rows_of(ti)
                cm, cmt = cm_ref[...], cmt_ref[...]
                fn = lambda xq, xf, v, lbv, st: _hgrn_dir(xq, xf, v, lbv, st, cm, cmt, tri, diag, rev)
                _, vjp = jax.vjp(fn, xq_ref[rows, :], xf_ref[rows, :], xi_ref[rows, :], lb, states[ti])
                dxq, dxf, dv, dlb_t, dstate = vjp((dpre[rows, :], dstate))
                df_ref[rows, :] = dxf.astype(CD)
                if first:
                    dq_acc[rows, :] = dxq
                    dv_acc[rows, :] = dv
                else:
                    dq_acc[rows, :] += dxq
                    dv_acc[rows, :] += dv
                return dstate, dlb + dlb_t

            return lax.fori_loop(0, nt, step_g, (zero, zrow))[1]

        dlb_ref[0:1, :] = sweep(xff_ref, dff_ref, lbf, cf_ref, cft_ref, tri_f, False, True)
        dlb_ref[1:2, :] = sweep(xfb_ref, dfb_ref, lbb, cb_ref, cbt_ref, tri_b, True, False)
        dq_ref[...] = dq_acc[...].astype(CD)
        di_ref[...] = dv_acc[...].astype(CD)

    col = lambda off: pl.BlockSpec((s, LANES), lambda m: (0, off + m))
    full = lambda a: pl.BlockSpec(a.shape, lambda m: (0,) * a.ndim)
    stream = jax.ShapeDtypeStruct((s, 512), CD)
    return pl.pallas_call(
        body, name="hgrn_bwd", grid=(4,),
        in_specs=[col(0), col(4), col(8), col(12), col(16), col(0), col(0), pl.BlockSpec((4, LANES), lambda m: (0, m)),
                  pl.BlockSpec((1, LANES), lambda m: (0, m))] + [full(c) for c in consts],
        out_specs=[col(0)] * 5 + [pl.BlockSpec((2, LANES), lambda m: (0, m)), pl.BlockSpec((1, LANES), lambda m: (0, m))],
        out_shape=[stream] * 5 + [jax.ShapeDtypeStruct((2, 512), F32), jax.ShapeDtypeStruct((1, 512), F32)],
        scratch_shapes=[pltpu.VMEM((s, LANES), F32), pltpu.VMEM((s, LANES), F32), pltpu.VMEM((s, LANES), F32),
                        pltpu.VMEM((nt, LANES, LANES), F32)],
        compiler_params=_params(1),
    )(ph, ph, ph, ph, ph, pre, dout, lbl, ng, *consts)


def _branch_out(o, w4):
    return jnp.concatenate([_mm(o, w4[j], NN) for j in range(N_SHARD)], axis=1)


def _mix_out_fwd(x, oa, ob, pg, wa, wb, wo, tm):
    s, d = x.shape

    def body(x_ref, oa_ref, ob_ref, ga_ref, gb_ref, wa_ref, wb_ref, wo_ref, xo_ref):
        ya = _branch_out(oa_ref[...], wa_ref)
        yb = _branch_out(ob_ref[...], wb_ref)
        merged = _sigmoid(ga_ref[...]) * ya + _sigmoid(gb_ref[...]) * yb
        xo_ref[...] = x_ref[...] + _mm(merged, wo_ref[...], NN)

    row = pl.BlockSpec((tm, d), lambda i: (i, 0))
    half = pl.BlockSpec((tm, 512), lambda i: (i, 0))
    full = lambda a: pl.BlockSpec(a.shape, lambda i: (0,) * a.ndim)
    return pl.pallas_call(
        body, name="mix_out_fwd", grid=(s // tm,),
        in_specs=[row, half, half, row, pl.BlockSpec((tm, d), lambda i: (i, 1)), full(wa), full(wb), full(wo)],
        out_specs=row, out_shape=jax.ShapeDtypeStruct((s, d), F32),
        compiler_params=_params(1),
    )(x, oa, ob, pg, pg, wa, wb, wo)


def _mix_out_bwd(dx, oa, ob, pg, wa, wb, wo, tm):
    s, d = dx.shape
    bs512, eq, ebc = _bf(_np_blocksum(512)), _bf(_np_expand_q()), _bf(_np_bcast_head())

    def body(dx_ref, oa_ref, ob_ref, ga_ref, gb_ref, wa_ref, wb_ref, wo_ref, bs_ref, eq_ref, ebc_ref,
             dpg_ref, mg_ref, dya_ref, dyb_ref, doe_ref, dl_ref, dob_ref):
        oa = oa_ref[...]
        ya = _branch_out(oa, wa_ref)
        yb = _branch_out(ob_ref[...], wb_ref)
        sa, sb = _sigmoid(ga_ref[...]), _sigmoid(gb_ref[...])
        mg_ref[...] = (sa * ya + sb * yb).astype(CD)
        dm = _mm(dx_ref[...], wo_ref[...], NT)
        dpg_ref[...] = jnp.concatenate([dm * ya * sa * (1.0 - sa), dm * yb * sb * (1.0 - sb)], axis=1).astype(CD)
        dya, dyb = dm * sa, dm * sb
        dya_ref[...] = dya.astype(CD)
        dyb_ref[...] = dyb.astype(CD)
        doa = jnp.zeros(oa.shape, F32)
        dob = jnp.zeros(oa.shape, F32)
        for j in range(N_SHARD):
            doa = doa + _mm(dya[:, 256 * j:256 * j + 256], wa_ref[j], NT)
            dob = dob + _mm(dyb[:, 256 * j:256 * j + 256], wb_ref[j], NT)
        dob_ref[...] = dob
        doe_ref[...] = _mm(doa, eq_ref[...], NN).astype(CD)
        dl_ref[...] = _xdot(_xdot(doa * oa, bs_ref[...]), ebc_ref[...])

    row = pl.BlockSpec((tm, d), lambda i: (i, 0))
    half = pl.BlockSpec((tm, 512), lambda i: (i, 0))
    full = lambda a: pl.BlockSpec(a.shape, lambda i: (0,) * a.ndim)
    wide = jax.ShapeDtypeStruct((s, d), CD)
    return pl.pallas_call(
        body, name="mix_out_bwd", grid=(s // tm,),
        in_specs=[row, half, half, row, pl.BlockSpec((tm, d), lambda i: (i, 1)), full(wa), full(wb), full(wo),
                  full(bs512), full(eq), full(ebc)],
        out_specs=[pl.BlockSpec((tm, 2048), lambda i: (i, 0)), row, row, row, row, row, half],
        out_shape=[jax.ShapeDtypeStruct((s, 2048), CD), wide, wide, wide, wide, jax.ShapeDtypeStruct((s, d), F32),
                   jax.ShapeDtypeStruct((s, 512), F32)],
        compiler_params=_params(1),
    )(dx, oa, ob, pg, pg, wa, wb, wo, bs512, eq, ebc)


def _loss_head(x, g, target, tm):
    s, d = x.shape

    def body(x_ref, g_ref, t_ref, dx_ref, loss_ref, dg_ref):
        @pl.when(pl.program_id(0) == 0)
        def _():
            loss_ref[...] = jnp.zeros_like(loss_ref)
            dg_ref[...] = jnp.zeros_like(dg_ref)

        xv = x_ref[...]
        r = lax.rsqrt(jnp.mean(xv * xv, axis=-1, keepdims=True) + EPS)
        err = xv * r * g_ref[...] - t_ref[...]
        loss_ref[...] += 0.5 * jnp.sum(jnp.mean(err * err, axis=-1, keepdims=True))
        dy = err * (1.0 / d)
        u = dy * g_ref[...]
        dx_ref[...] = r * u - xv * (r * r * r) * jnp.mean(u * xv, axis=-1, keepdims=True)
        dg_ref[...] += jnp.sum(dy * xv * r, axis=0, keepdims=True)

    row = pl.BlockSpec((tm, d), lambda i: (i, 0))
    vec = pl.BlockSpec((1, d), lambda i: (0, 0))
    return pl.pallas_call(
        body, name="loss_head", grid=(s // tm,),
        in_specs=[row, vec, row], out_specs=[row, pl.BlockSpec((8, LANES), lambda i: (0, 0)), vec],
        out_shape=[jax.ShapeDtypeStruct((s, d), F32), jax.ShapeDtypeStruct((8, LANES), F32),
                   jax.ShapeDtypeStruct((1, d), F32)],
        compiler_params=_params(1),
    )(x, g, target)


def _position():
    x, y, c = lax.axis_index("x"), lax.axis_index("y"), lax.axis_index("c")
    return x, y, c, [(1 - x, y), (x, 1 - y), (1 - x, 1 - y)]


def _weight_gather(shards):
    n = len(shards)

    def body(*refs):
        src, dst = refs[:n], refs[n:2 * n]
        send, recv, loc = refs[2 * n:]
        x, y, c, chips = _position()
        me = 2 * x + y
        local = [pltpu.make_async_copy(src[a], dst[a].at[me], loc.at[a]) for a in range(n)]
        for cp in local:
            cp.start()
        copies = []
        for a in range(n):
            for k, (px, py) in enumerate(chips):
                cp = pltpu.make_async_remote_copy(src_ref=src[a], dst_ref=dst[a].at[me], send_sem=send.at[a, k],
                                                  recv_sem=recv.at[a, k], device_id=(px, py, c), device_id_type=MESH)
                cp.start()
                copies.append(cp)
        for cp in copies:
            cp.wait()
        for cp in local:
            cp.wait()

    return pl.pallas_call(
        body, name="weight_gather", in_specs=[ANY] * n, out_specs=[ANY] * n,
        out_shape=[jax.ShapeDtypeStruct((N_SHARD,) + w.shape, w.dtype) for w in shards],
        scratch_shapes=[pltpu.SemaphoreType.DMA((n, 3)), pltpu.SemaphoreType.DMA((n, 3)), pltpu.SemaphoreType.DMA((n,))],
    )(*shards)


def _pair_exchange(grads):
    n = len(grads)

    def body(*refs):
        src, dst = refs[:n], refs[n:2 * n]
        send, recv = refs[2 * n:]
        x, y, c, _ = _position()
        copies = []
        for a in range(n):
            half = src[a].shape[1] // 2
            cp = pltpu.make_async_remote_copy(
                src_ref=src[a].at[:, pl.ds((1 - c) * half, half), :], dst_ref=dst[a], send_sem=send.at[a],
                recv_sem=recv.at[a], device_id=(x, y, 1 - c), device_id_type=MESH)
            cp.start()
            copies.append(cp)
        for cp in copies:
            cp.wait()

    return pl.pallas_call(
        body, name="grad_pair_exchange", in_specs=[ANY] * n, out_specs=[ANY] * n,
        out_shape=[jax.ShapeDtypeStruct((g.shape[0], g.shape[1] // 2, g.shape[2]), g.dtype) for g in grads],
        scratch_shapes=[pltpu.SemaphoreType.DMA((n,)), pltpu.SemaphoreType.DMA((n,))],
    )(*grads)


def _pair_sum(g, got, c_idx, me_idx):
    nsh, rows, cols = g.shape
    half = rows // 2

    def body(c_ref, me_ref, g_ref, got_ref, s_ref, own_ref):
        sm = g_ref[...] + got_ref[...]
        s_ref[...] = sm.astype(CD)

        @pl.when(pl.program_id(0) == me_ref[0])
        def _():
            own_ref[...] = sm[0]

    return pl.pallas_call(
        body, name="grad_pair_sum",
        grid_spec=pltpu.PrefetchScalarGridSpec(
            num_scalar_prefetch=2, grid=(nsh,),
            in_specs=[pl.BlockSpec((1, half, cols), lambda j, c, me: (j, c[0], 0)),
                      pl.BlockSpec((1, half, cols), lambda j, c, me: (j, 0, 0))],
            out_specs=[pl.BlockSpec((1, half, cols), lambda j, c, me: (j, 0, 0)),
                       pl.BlockSpec((half, cols), lambda j, c, me: (0, 0))]),
        out_shape=[jax.ShapeDtypeStruct((nsh, half, cols), CD), jax.ShapeDtypeStruct((half, cols), F32)],
        compiler_params=_params(1),
    )(c_idx, me_idx, g, got)


def _chip_exchange(sums):
    n = len(sums)

    def body(*refs):
        src, dst = refs[:n], refs[n:2 * n]
        send, recv = refs[2 * n:]
        x, y, c, chips = _position()
        me = 2 * x + y
        copies = []
        for a in range(n):
            for k, (px, py) in enumerate(chips):
                cp = pltpu.make_async_remote_copy(src_ref=src[a].at[2 * px + py], dst_ref=dst[a].at[me],
                                                  send_sem=send.at[a, k], recv_sem=recv.at[a, k],
                                                  device_id=(px, py, c), device_id_type=MESH)
                cp.start()
                copies.append(cp)
        for cp in copies:
            cp.wait()

    return pl.pallas_call(
        body, name="grad_chip_exchange", in_specs=[ANY] * n, out_specs=[ANY] * n,
        out_shape=[jax.ShapeDtypeStruct(s_.shape, s_.dtype) for s_ in sums],
        scratch_shapes=[pltpu.SemaphoreType.DMA((n, 3)), pltpu.SemaphoreType.DMA((n, 3))],
    )(*sums)


def _chip_sum(own, got, me_idx):
    nsh, half, cols = got.shape

    def body(me_ref, own_ref, got_ref, out_ref):
        j = pl.program_id(0)
        term = jnp.where(j == me_ref[0], own_ref[...], got_ref[0].astype(F32))

        @pl.when(j == 0)
        def _():
            out_ref[...] = term

        @pl.when(j > 0)
        def _():
            out_ref[...] += term

    return pl.pallas_call(
        body, name="grad_chip_sum",
        grid_spec=pltpu.PrefetchScalarGridSpec(
            num_scalar_prefetch=1, grid=(nsh,),
            in_specs=[pl.BlockSpec((half, cols), lambda j, me: (0, 0)), pl.BlockSpec((1, half, cols), lambda j, me: (j, 0, 0))],
            out_specs=pl.BlockSpec((half, cols), lambda j, me: (0, 0))),
        out_shape=jax.ShapeDtypeStruct((half, cols), F32),
        compiler_params=_params(1),
    )(me_idx, own, got)


def _pair_share(halves):
    n = len(halves)

    def body(*refs):
        src, dst = refs[:n], refs[n:2 * n]
        send, recv, loc = refs[2 * n:]
        x, y, c, _ = _position()
        copies, local = [], []
        for a in range(n):
            lc = pltpu.make_async_copy(src[a], dst[a].at[c], loc.at[a])
            lc.start()
            local.append(lc)
            cp = pltpu.make_async_remote_copy(src_ref=src[a], dst_ref=dst[a].at[c], send_sem=send.at[a],
                                              recv_sem=recv.at[a], device_id=(x, y, 1 - c), device_id_type=MESH)
            cp.start()
            copies.append(cp)
        for cp in copies:
            cp.wait()
        for lc in local:
            lc.wait()

    return pl.pallas_call(
        body, name="grad_pair_share", in_specs=[ANY] * n, out_specs=[ANY] * n,
        out_shape=[jax.ShapeDtypeStruct((2,) + h.shape, h.dtype) for h in halves],
        scratch_shapes=[pltpu.SemaphoreType.DMA((n,)), pltpu.SemaphoreType.DMA((n,)), pltpu.SemaphoreType.DMA((n,))],
    )(*halves)


def _small_allreduce(buf):
    rows, cols = buf.shape

    def body(src_ref, out_ref, slots, send, recv):
        x, y, c, _ = _position()
        me = 4 * x + 2 * y + c
        slots[me] = src_ref[...]
        copies = []
        k = 0
        for dx in (0, 1):
            for dy in (0, 1):
                for dc in (0, 1):
                    if (dx, dy, dc) == (0, 0, 0):
                        continue
                    peer = (jnp.where(dx, 1 - x, x), jnp.where(dy, 1 - y, y), jnp.where(dc, 1 - c, c))
                    cp = pltpu.make_async_remote_copy(src_ref=src_ref, dst_ref=slots.at[me], send_sem=send.at[k],
                                                      recv_sem=recv.at[k], device_id=peer, device_id_type=MESH)
                    cp.start()
                    copies.append(cp)
                    k += 1
        for cp in copies:
            cp.wait()
        total = slots[0]
        for dev in range(1, N_DEV):
            total = total + slots[dev]
        out_ref[...] = total

    vm = pl.BlockSpec(memory_space=pltpu.VMEM)
    return pl.pallas_call(
        body, name="small_allreduce", in_specs=[vm], out_specs=vm,
        out_shape=jax.ShapeDtypeStruct((rows, cols), F32),
        scratch_shapes=[pltpu.VMEM((N_DEV, rows, cols), F32), pltpu.SemaphoreType.DMA((N_DEV - 1,)),
                        pltpu.SemaphoreType.DMA((N_DEV - 1,))],
    )(buf)


def _adamw(w, g, m, v):
    rows, cols = w.shape
    tr = rows
    for cand in (256, 128, 64, 32, 16, 8):
        if rows % cand == 0 and rows > cand:
            tr = cand
            break

    def body(w_ref, g_ref, m_ref, v_ref, d_ref, mo_ref, vo_ref):
        gv = g_ref[...]
        mn = ADAM_B1 * m_ref[...] + (1.0 - ADAM_B1) * gv
        vn = ADAM_B2 * v_ref[...] + (1.0 - ADAM_B2) * (gv * gv)
        m_hat = mn / (1.0 - ADAM_B1 ** ADAM_STEP)
        v_hat = vn / (1.0 - ADAM_B2 ** ADAM_STEP)
        d_ref[...] = -ADAM_LR * (m_hat / (jnp.sqrt(v_hat) + ADAM_EPS) + ADAM_WD * w_ref[...])
        mo_ref[...] = mn
        vo_ref[...] = vn

    blk = pl.BlockSpec((tr, cols), lambda i: (i, 0))
    shp = jax.ShapeDtypeStruct((rows, cols), F32)
    return pl.pallas_call(
        body, name="adamw", grid=(rows // tr,), in_specs=[blk] * 4, out_specs=[blk] * 3, out_shape=[shp] * 3,
        compiler_params=_params(1),
    )(w, g, m, v)


def _local_step(x, target, norm_gains, q_g, k_g, lbl, ng, w):
    s = x.shape[0]
    tm = min(512, s)
    tq = min(256, s)
    g1, gm, g2, gf = norm_gains
    cos2, sin2 = _rope_tables(s)
    gq8 = jnp.tile(q_g, (1, 8))
    gk2 = jnp.tile(k_g, (1, 2))

    x1, a1, b1, h1 = _ffn_fwd(x, g1, w["g1"], w["u1"], w["d1"], tm)
    pqkv, ph, pg, hm = _mix_in_fwd(x1, gm, w["in"], tm)
    qe, kr, vr, vs = _qk_prep(pqkv, gq8, gk2, cos2, sin2, tm)
    oa, lse = _attn_fwd(qe, kr, vr, vs, tq)
    ob, pre = _hgrn_fwd(ph, lbl, ng)
    x2 = _mix_out_fwd(x1, oa, ob, pg, w["a"], w["b"], w["o"], tm)
    x3, a2, b2, h2 = _ffn_fwd(x2, g2, w["g2"], w["u2"], w["d2"], tm)
    dx3, loss, dgf = _loss_head(x3, gf, target, tm)

    grads = {}
    dx2, da2, db2, f2, dg2 = _ffn_bwd(dx3, x2, g2, a2, b2, w["g2"], w["u2"], w["d2"], tm)
    grads["g2"] = _dw_shared_a("dw_gate", h2, da2, tm)
    grads["u2"] = _dw_shared_a("dw_gate", h2, db2, tm)
    grads["d2"] = _dw_shared_b("dw_down", f2, dx3, tm, 0.5)

    dpg, mg, dya, dyb, doe, delta, dob = _mix_out_bwd(dx2, oa, ob, pg, w["a"], w["b"], w["o"], tm)
    grads["o"] = _dw_colblocks("dw_out", mg, dx2, 1, tm).reshape(N_SHARD, D_MODEL // N_SHARD, D_MODEL)
    grads["a"] = _dw_colblocks("dw_branch", oa, dya, N_SHARD, tm)
    grads["b"] = _dw_colblocks("dw_branch", ob, dyb, N_SHARD, tm)
    dqe, dk, dv = _attn_bwd(qe, kr, vr, doe, delta, lse, tq)
    dqkv, dgq, dgk = _qk_prep_bwd(pqkv, dqe, dk, dv, gq8, gk2, cos2, sin2, tm)
    dhq, dhff, dhfb, dhi, dhg, dlb, dng = _hgrn_bwd(ph, pre, dob, lbl, ng)
    dps = (dqkv, dhq, dhff, dhfb, dhi, dhg, dpg)
    grads["in"] = _dw_in(dps, hm, tm).reshape(N_SHARD, -1, D_MODEL)
    dx1, dgm = _mix_in_bwd(dps, w["in"], x1, dx2, gm, tm)

    dx0, da1, db1, f1, dg1 = _ffn_bwd(dx1, x, g1, a1, b1, w["g1"], w["u1"], w["d1"], tm)
    grads["g1"] = _dw_shared_a("dw_gate", h1, da1, tm)
    grads["u1"] = _dw_shared_a("dw_gate", h1, db1, tm)
    grads["d1"] = _dw_shared_b("dw_down", f1, dx1, tm, 0.5)
    small = dict(g1=dg1, gm=dgm, g2=dg2, gf=dgf, gq=dgq, gk=dgk, lb=dlb, ng=dng)
    return loss, dx0, grads, small


BIG = ("g1", "u1", "d1", "in", "a", "b", "o", "g2", "u2", "d2")


def _pack_rows(vectors, width):
    rows = []
    for vct in vectors:
        flat = vct.reshape(-1)
        pad = (-flat.shape[0]) % width
        rows.append(jnp.pad(flat, (0, pad)).reshape(-1, width))
    return jnp.concatenate(rows, axis=0)


def kernel(x, ffn1_norm_g, ffn1_w_gate, ffn1_w_up, ffn1_w_down, mix_norm_g, w_in, q_norm_g, k_norm_g, hgrn_lb_logits, hgrn_out_norm_g, w_branch_attn, w_branch_hgrn, w_out, ffn2_norm_g, ffn2_w_gate, ffn2_w_up, ffn2_w_down, final_norm_g, loss_target, m_ffn1_norm_g, m_ffn1_w_gate, m_ffn1_w_up, m_ffn1_w_down, m_mix_norm_g, m_w_in, m_q_norm_g, m_k_norm_g, m_hgrn_lb_logits, m_hgrn_out_norm_g, m_w_branch_attn, m_w_branch_hgrn, m_w_out, m_ffn2_norm_g, m_ffn2_w_gate, m_ffn2_w_up, m_ffn2_w_down, m_final_norm_g, v_ffn1_norm_g, v_ffn1_w_gate, v_ffn1_w_up, v_ffn1_w_down, v_mix_norm_g, v_w_in, v_q_norm_g, v_k_norm_g, v_hgrn_lb_logits, v_hgrn_out_norm_g, v_w_branch_attn, v_w_branch_hgrn, v_w_out, v_ffn2_norm_g, v_ffn2_w_gate, v_ffn2_w_up, v_ffn2_w_down, v_final_norm_g):
    xi, yi, ci = lax.axis_index("x"), lax.axis_index("y"), lax.axis_index("c")
    me = 2 * xi + yi
    c_idx = jnp.reshape(ci, (1,)).astype(jnp.int32)
    me_idx = jnp.reshape(me, (1,)).astype(jnp.int32)

    big_w = dict(g1=ffn1_w_gate[0], u1=ffn1_w_up[0], d1=ffn1_w_down[0], a=w_branch_attn[0], b=w_branch_hgrn[0],
                 o=w_out[0], g2=ffn2_w_gate[0], u2=ffn2_w_up[0], d2=ffn2_w_down[0])
    big_w["in"] = w_in[0]
    big_m = dict(g1=m_ffn1_w_gate[0], u1=m_ffn1_w_up[0], d1=m_ffn1_w_down[0], a=m_w_branch_attn[0], b=m_w_branch_hgrn[0],
                 o=m_w_out[0], g2=m_ffn2_w_gate[0], u2=m_ffn2_w_up[0], d2=m_ffn2_w_down[0])
    big_m["in"] = m_w_in[0]
    big_v = dict(g1=v_ffn1_w_gate[0], u1=v_ffn1_w_up[0], d1=v_ffn1_w_down[0], a=v_w_branch_attn[0], b=v_w_branch_hgrn[0],
                 o=v_w_out[0], g2=v_ffn2_w_gate[0], u2=v_ffn2_w_up[0], d2=v_ffn2_w_down[0])
    big_v["in"] = v_w_in[0]

    shards = [(big_w[n].T if n == "in" else big_w[n]).astype(CD) for n in BIG]
    shards.append(hgrn_lb_logits.reshape(4, LANES))
    gathered = _weight_gather(shards)
    w = dict(zip(BIG, gathered[:-1]))
    w["in"] = w["in"].reshape(-1, D_MODEL)
    w["o"] = w["o"].reshape(D_MODEL, D_MODEL)
    lbl = jnp.transpose(gathered[-1], (1, 0, 2)).reshape(4, N_SHARD * LANES)

    loss, dx, grads, small = _local_step(
        x[0], loss_target[0], (ffn1_norm_g, mix_norm_g, ffn2_norm_g, final_norm_g.reshape(1, -1)),
        q_norm_g, k_norm_g, lbl, hgrn_out_norm_g, w)

    glist = [grads[n] for n in BIG]
    got = _pair_exchange(glist)
    sums, owns = zip(*[_pair_sum(g, r, c_idx, me_idx) for g, r in zip(glist, got)])
    parts = _chip_exchange(list(sums))
    halves = [_chip_sum(o, p, me_idx) for o, p in zip(owns, parts)]
    whole = _pair_share(halves)
    gbig = {n: wh.reshape(-1, wh.shape[-1]) for n, wh in zip(BIG, whole)}
    gbig["in"] = gbig["in"].T

    dgq = small["gq"].reshape(8, HEAD_DIM).sum(axis=0)
    dgk = small["gk"].reshape(2, HEAD_DIM).sum(axis=0)
    lb_full = _hgrn_lower_bounds(lbl)
    dlog = []
    for d in (0, 1):
        t = small["lb"][d:d + 1] * lb_full[d] * (1.0 - lb_full[d])
        dlog += [t, -t]
    small_list = [small["g1"], small["gm"], small["g2"], small["gf"], small["ng"], dgq, dgk, jnp.concatenate(dlog, axis=0), loss[0, 0]]
    packed = _pack_rows(small_list, D_MODEL)
    n_rows = packed.shape[0]
    packed = jnp.pad(packed, ((0, (-n_rows) % 8), (0, 0)))
    red = _small_allreduce(packed)
    loss_out = red[n_rows - 1, 0]
    sg = dict(g1=red[0:1], gm=red[1:2], g2=red[2:3], gf=red[3], ng=red[4:5, :512], gq=red[5:6, :HEAD_DIM],
              gk=red[6:7, :HEAD_DIM])
    dlog_full = red[7:9].reshape(2, 2, 512)
    sg["lb"] = lax.dynamic_slice_in_dim(dlog_full, me * LANES, LANES, axis=2)

    small_w = dict(g1=ffn1_norm_g, gm=mix_norm_g, g2=ffn2_norm_g, gf=final_norm_g, ng=hgrn_out_norm_g, gq=q_norm_g,
                   gk=k_norm_g, lb=hgrn_lb_logits)
    small_m = dict(g1=m_ffn1_norm_g, gm=m_mix_norm_g, g2=m_ffn2_norm_g, gf=m_final_norm_g, ng=m_hgrn_out_norm_g,
                   gq=m_q_norm_g, gk=m_k_norm_g, lb=m_hgrn_lb_logits)
    small_v = dict(g1=v_ffn1_norm_g, gm=v_mix_norm_g, g2=v_ffn2_norm_g, gf=v_final_norm_g, ng=v_hgrn_out_norm_g,
                   gq=v_q_norm_g, gk=v_k_norm_g, lb=v_hgrn_lb_logits)
    small_names = ("g1", "gm", "g2", "gf", "ng", "gq", "gk", "lb")
    pack = lambda dct: _pack_rows([dct[n] for n in small_names], D_MODEL)
    pw, pgr, pm, pv = pack(small_w), pack(sg), pack(small_m), pack(small_v)
    pad8 = lambda a: jnp.pad(a, ((0, (-a.shape[0]) % 8), (0, 0)))
    sd, sm_, sv_ = _adamw(pad8(pw), pad8(pgr), pad8(pm), pad8(pv))

    def unpack(buf):
        out, r = {}, 0
        for n in small_names:
            size = small_w[n].size
            nr = -(-size // D_MODEL)
            out[n] = buf[r:r + nr].reshape(-1)[:size].reshape(small_w[n].shape)
            r += nr
        return out

    sdelta, snew_m, snew_v = unpack(sd), unpack(sm_), unpack(sv_)
    sgrad = {n: sg[n].reshape(small_w[n].shape) for n in small_names}

    bdelta, bnew_m, bnew_v, bgrad = {}, {}, {}, {}
    for n in BIG:
        dlt, mn, vn = _adamw(big_w[n], gbig[n], big_m[n], big_v[n])
        bdelta[n], bnew_m[n], bnew_v[n], bgrad[n] = dlt[None], mn[None], vn[None], gbig[n][None]

    order = [("s", "g1"), ("b", "g1"), ("b", "u1"), ("b", "d1"), ("s", "gm"), ("b", "in"), ("s", "gq"), ("s", "gk"),
             ("s", "lb"), ("s", "ng"), ("b", "a"), ("b", "b"), ("b", "o"), ("s", "g2"), ("b", "g2"), ("b", "u2"),
             ("b", "d2"), ("s", "gf")]
    outs = [loss_out, dx[None]]
    for table_s, table_b in ((sgrad, bgrad), (sdelta, bdelta), (snew_m, bnew_m), (snew_v, bnew_v)):
        outs += [(table_s if kind == "s" else table_b)[n] for kind, n in order]
    return tuple(outs)
```

```python
import functools

import numpy as np
import jax
import jax.numpy as jnp
from jax import lax
from jax.experimental import pallas as pl
from jax.experimental.pallas import tpu as pltpu

F32 = jnp.float32
BF16 = jnp.bfloat16
CD = jnp.bfloat16

EPS = 1e-6
D_MODEL = 1024
HEAD_DIM = 64
GRID_W = 64
ROPE_THETA = 10000.0
CHUNK = 32
N_SHARD = 4
N_DEV = 8
VMEM_LIMIT = 56 * 1024 * 1024
LANES = 128
HG_TILE = 256

ADAM_LR = 0.001
ADAM_B1 = 0.9
ADAM_B2 = 0.999
ADAM_EPS = 1e-08
ADAM_WD = 0.01
ADAM_STEP = 10

NN = (((1,), (0,)), ((), ()))
NT = (((1,), (1,)), ((), ()))
TN = (((0,), (0,)), ((), ()))
MESH = pl.DeviceIdType.MESH
ANY = pl.BlockSpec(memory_space=pl.ANY)


def _mm(a, b, dn):
    return lax.dot_general(a.astype(CD), b.astype(CD), dn, preferred_element_type=F32)


def _split3(x):
    hi = x.astype(BF16)
    r = x - hi.astype(F32)
    mid = r.astype(BF16)
    lo = (r - mid.astype(F32)).astype(BF16)
    return hi, mid, lo


def _xdot(x, m):
    hi, mid, lo = _split3(x)
    d = lambda t: lax.dot_general(t, m, NN, preferred_element_type=F32)
    return d(hi) + d(mid) + d(lo)


def _xdot_l(m, x):
    hi, mid, lo = _split3(x)
    d = lambda t: lax.dot_general(m, t, NN, preferred_element_type=F32)
    return d(hi) + d(mid) + d(lo)


def _params(n_grid):
    return pltpu.CompilerParams(dimension_semantics=("arbitrary",) * n_grid, vmem_limit_bytes=VMEM_LIMIT)


def _sigmoid(x):
    return jax.nn.sigmoid(x)


def _np_blocksum(n):
    i = np.arange(n)
    return (i[:, None] // HEAD_DIM == i[None, :] // HEAD_DIM).astype(np.float32)


def _np_swap32(n):
    i = np.arange(n)
    partner = np.where(i % HEAD_DIM < HEAD_DIM // 2, i + HEAD_DIM // 2, i - HEAD_DIM // 2)
    m = np.zeros((n, n), np.float32)
    m[i, partner] = 1.0
    return m


def _np_expand_q():
    m = np.zeros((512, 1024), np.float32)
    for h in range(8):
        g = h // 4
        for d in range(HEAD_DIM):
            m[64 * h + d, 128 * h + 64 * g + d] = 1.0
    return m


def _np_bcast_head():
    m = np.zeros((512, 1024), np.float32)
    for h in range(8):
        m[64 * h, 128 * h:128 * h + 128] = 1.0
    return m


def _np_swap_halves():
    m = np.zeros((128, 128), np.float32)
    i = np.arange(128)
    m[i, (i + 64) % 128] = 1.0
    return m


def _np_hgrn_cums(t, rev):
    r = np.arange(t)[:, None]
    c = np.arange(t)[None, :]
    same = (r // CHUNK) == (c // CHUNK)
    if not rev:
        cum = same & (c <= r)
        mid = same & (c % CHUNK <= CHUNK // 2 - 1)
    else:
        cum = same & (c >= r)
        mid = same & (c % CHUNK >= CHUNK // 2)
    return np.concatenate([cum, mid, same], axis=0).astype(np.float32)


def _bf(a):
    return jnp.asarray(a, dtype=BF16)


def _rope_tables(seq_len):
    rows = seq_len // GRID_W
    row = jnp.repeat(jnp.arange(rows, dtype=F32), GRID_W)
    col = jnp.tile(jnp.arange(GRID_W, dtype=F32), rows)
    n_freq = HEAD_DIM // 4
    inv = ROPE_THETA ** (-jnp.arange(n_freq, dtype=F32) / n_freq)
    ang = jnp.concatenate([row[:, None] * inv, col[:, None] * inv], axis=-1)
    cos, sin = jnp.cos(ang), jnp.sin(ang)
    c64 = jnp.concatenate([cos, cos], axis=-1)
    s64 = jnp.concatenate([-sin, sin], axis=-1)
    return jnp.tile(c64, (1, 2)), jnp.tile(s64, (1, 2))


def _ffn_fwd(x, g, wg, wu, wd, tm):
    s, d = x.shape
    nsh, _, fs = wg.shape

    def body(x_ref, g_ref, wg_ref, wu_ref, wd_ref, xo_ref, a_ref, b_ref, hb_ref, acc, hs):
        j = pl.program_id(1)

        @pl.when(j == 0)
        def _():
            xv = x_ref[...]
            r = lax.rsqrt(jnp.mean(xv * xv, axis=-1, keepdims=True) + EPS)
            h = (xv * r * g_ref[...]).astype(CD)
            hs[...] = h
            hb_ref[...] = h
            acc[...] = jnp.zeros_like(acc)

        h = hs[...]
        a = _mm(h, wg_ref[0], NN)
        b = _mm(h, wu_ref[0], NN)
        f = a * _sigmoid(a) * b
        acc[...] += _mm(f, wd_ref[0], NN)
        a_ref[0] = a.astype(CD)
        b_ref[0] = b.astype(CD)

        @pl.when(j == nsh - 1)
        def _():
            xo_ref[...] = x_ref[...] + 0.5 * acc[...]

    return pl.pallas_call(
        body, name="ffn_fwd", grid=(s // tm, nsh),
        in_specs=[pl.BlockSpec((tm, d), lambda i, j: (i, 0)), pl.BlockSpec((1, d), lambda i, j: (0, 0)),
                  pl.BlockSpec((1, d, fs), lambda i, j: (j, 0, 0)), pl.BlockSpec((1, d, fs), lambda i, j: (j, 0, 0)),
                  pl.BlockSpec((1, fs, d), lambda i, j: (j, 0, 0))],
        out_specs=[pl.BlockSpec((tm, d), lambda i, j: (i, 0)), pl.BlockSpec((1, tm, fs), lambda i, j: (j, i, 0)),
                   pl.BlockSpec((1, tm, fs), lambda i, j: (j, i, 0)), pl.BlockSpec((tm, d), lambda i, j: (i, 0))],
        out_shape=[jax.ShapeDtypeStruct((s, d), F32), jax.ShapeDtypeStruct((nsh, s, fs), CD),
                   jax.ShapeDtypeStruct((nsh, s, fs), CD), jax.ShapeDtypeStruct((s, d), CD)],
        scratch_shapes=[pltpu.VMEM((tm, d), F32), pltpu.VMEM((tm, d), CD)],
        compiler_params=_params(2),
    )(x, g, wg, wu, wd)


def _ffn_bwd(dout, x, g, a, b, wg, wu, wd, tm):
    s, d = x.shape
    nsh, _, fs = wg.shape

    def body(do_ref, x_ref, g_ref, a_ref, b_ref, wg_ref, wu_ref, wd_ref, dx_ref, da_ref, db_ref, f_ref, dg_ref, dh):
        i = pl.program_id(0)
        j = pl.program_id(1)

        @pl.when(j == 0)
        def _():
            dh[...] = jnp.zeros_like(dh)

        @pl.when((i == 0) & (j == 0))
        def _():
            dg_ref[...] = jnp.zeros_like(dg_ref)

        av = a_ref[0].astype(F32)
        bv = b_ref[0].astype(F32)
        sg = _sigmoid(av)
        sl = av * sg
        df = 0.5 * _mm(do_ref[...], wd_ref[0], NT)
        da = df * bv * (sg * (1.0 + av * (1.0 - sg)))
        db = df * sl
        dh[...] += _mm(da, wg_ref[0], NT) + _mm(db, wu_ref[0], NT)
        da_ref[0] = da.astype(CD)
        db_ref[0] = db.astype(CD)
        f_ref[0] = (sl * bv).astype(CD)

        @pl.when(j == nsh - 1)
        def _():
            xv = x_ref[...]
            r = lax.rsqrt(jnp.mean(xv * xv, axis=-1, keepdims=True) + EPS)
            dhv = dh[...]
            u = dhv * g_ref[...]
            dx_ref[...] = do_ref[...] + r * u - xv * (r * r * r) * jnp.mean(u * xv, axis=-1, keepdims=True)
            dg_ref[...] += jnp.sum(dhv * xv * r, axis=0, keepdims=True)

    act = pl.BlockSpec((1, tm, fs), lambda i, j: (j, i, 0))
    row = pl.BlockSpec((tm, d), lambda i, j: (i, 0))
    return pl.pallas_call(
        body, name="ffn_bwd", grid=(s // tm, nsh),
        in_specs=[row, row, pl.BlockSpec((1, d), lambda i, j: (0, 0)), act, act,
                  pl.BlockSpec((1, d, fs), lambda i, j: (j, 0, 0)), pl.BlockSpec((1, d, fs), lambda i, j: (j, 0, 0)),
                  pl.BlockSpec((1, fs, d), lambda i, j: (j, 0, 0))],
        out_specs=[row, act, act, act, pl.BlockSpec((1, d), lambda i, j: (0, 0))],
        out_shape=[jax.ShapeDtypeStruct((s, d), F32), jax.ShapeDtypeStruct((nsh, s, fs), CD),
                   jax.ShapeDtypeStruct((nsh, s, fs), CD), jax.ShapeDtypeStruct((nsh, s, fs), CD),
                   jax.ShapeDtypeStruct((1, d), F32)],
        scratch_shapes=[pltpu.VMEM((tm, d), F32)],
        compiler_params=_params(2),
    )(dout, x, g, a, b, wg, wu, wd)


def _tn_call(name, operands, in_specs, out_shape, out_spec, grid, acc_shape, pick, scale=1.0):
    nk = grid[-1]
    n_in = len(operands)

    def body(*refs):
        out_ref, acc = refs[n_in], refs[n_in + 1]
        k = pl.program_id(len(grid) - 1)

        @pl.when(k == 0)
        def _():
            acc[...] = jnp.zeros_like(acc)

        pick(refs[:n_in], acc)

        @pl.when(k == nk - 1)
        def _():
            res = acc[...] if scale == 1.0 else acc[...] * scale
            out_ref[...] = res.reshape(out_ref.shape)

    return pl.pallas_call(
        body, name=name, grid=grid, in_specs=in_specs, out_specs=out_spec, out_shape=out_shape,
        scratch_shapes=[pltpu.VMEM(acc_shape, F32)], compiler_params=_params(len(grid)),
    )(*operands)


def _dw_shared_a(name, a, b3, tk):
    s, m = a.shape
    nj, _, n = b3.shape

    def pick(refs, acc):
        acc[...] += _mm(refs[0][...], refs[1][0], TN)

    return _tn_call(name, (a, b3),
                    [pl.BlockSpec((tk, m), lambda j, k: (k, 0)), pl.BlockSpec((1, tk, n), lambda j, k: (j, k, 0))],
                    jax.ShapeDtypeStruct((nj, m, n), F32), pl.BlockSpec((1, m, n), lambda j, k: (j, 0, 0)),
                    (nj, s // tk), (m, n), pick)


def _dw_shared_b(name, a3, b, tk, scale):
    nj, s, m = a3.shape
    n = b.shape[1]

    def pick(refs, acc):
        acc[...] += _mm(refs[0][0], refs[1][...], TN)

    return _tn_call(name, (a3, b),
                    [pl.BlockSpec((1, tk, m), lambda j, k: (j, k, 0)), pl.BlockSpec((tk, n), lambda j, k: (k, 0))],
                    jax.ShapeDtypeStruct((nj, m, n), F32), pl.BlockSpec((1, m, n), lambda j, k: (j, 0, 0)),
                    (nj, s // tk), (m, n), pick, scale)


def _dw_colblocks(name, a, b, nj, tk):
    s, m = a.shape
    n = b.shape[1] // nj

    def pick(refs, acc):
        acc[...] += _mm(refs[0][...], refs[1][...], TN)

    return _tn_call(name, (a, b),
                    [pl.BlockSpec((tk, m), lambda j, k: (k, 0)), pl.BlockSpec((tk, n), lambda j, k: (k, j))],
                    jax.ShapeDtypeStruct((nj, m, n), F32), pl.BlockSpec((1, m, n), lambda j, k: (j, 0, 0)),
                    (nj, s // tk), (m, n), pick)


DP_WIDTHS = (768, 512, 512, 512, 512, 512, 2048)
DP_CHUNK = 256


def _dp_chunk_maps():
    starts, counts, off = [], [], 0
    for w in DP_WIDTHS:
        starts.append(off // DP_CHUNK)
        counts.append(w // DP_CHUNK)
        off += w
    return starts, counts


def _dp_specs(tm, row_axis, chunk_axis):
    starts, counts = _dp_chunk_maps()
    specs = []
    for st, cnt in zip(starts, counts):
        def imap(*ids, st=st, cnt=cnt):
            return (ids[row_axis], jnp.clip(ids[chunk_axis] - st, 0, cnt - 1))
        specs.append(pl.BlockSpec((tm, DP_CHUNK), imap))
    return specs


def _dp_select(n, refs, fn):
    starts, counts = _dp_chunk_maps()
    for ref, st, cnt in zip(refs, starts, counts):
        @pl.when((n >= st) & (n < st + cnt))
        def _(ref=ref):
            fn(ref)


def _dw_in(dps, hb, tk):
    s, d = hb.shape
    n_chunks = sum(DP_WIDTHS) // DP_CHUNK

    def pick(refs, acc):
        def add(ref):
            acc[...] += _mm(ref[...], refs[7][...], TN)

        _dp_select(pl.program_id(0), refs[:7], add)

    return _tn_call("dw_in", (*dps, hb),
                    _dp_specs(tk, 1, 0) + [pl.BlockSpec((tk, d), lambda n, k: (k, 0))],
                    jax.ShapeDtypeStruct((n_chunks * DP_CHUNK, d), F32), pl.BlockSpec((DP_CHUNK, d), lambda n, k: (n, 0)),
                    (n_chunks, s // tk), (DP_CHUNK, d), pick)


def _mix_in_fwd(x, g, w_t, tm):
    s, d = x.shape
    n_in = w_t.shape[0]

    def body(x_ref, g_ref, w_ref, qkv_ref, hg_ref, gt_ref, hb_ref):
        xv = x_ref[...]
        r = lax.rsqrt(jnp.mean(xv * xv, axis=-1, keepdims=True) + EPS)
        h = (xv * r * g_ref[...]).astype(CD)
        hb_ref[...] = h
        qkv_ref[...] = _mm(h, w_ref[0:768, :], NT)
        for c in range(5):
            hg_ref[:, 512 * c:512 * c + 512] = _mm(h, w_ref[768 + 512 * c:768 + 512 * c + 512, :], NT)
        for c in range(2):
            gt_ref[:, 1024 * c:1024 * c + 1024] = _mm(h, w_ref[3328 + 1024 * c:3328 + 1024 * c + 1024, :], NT)

    row = lambda w: pl.BlockSpec((tm, w), lambda i: (i, 0))
    return pl.pallas_call(
        body, name="mix_in_fwd", grid=(s // tm,),
        in_specs=[row(d), pl.BlockSpec((1, d), lambda i: (0, 0)), pl.BlockSpec((n_in, d), lambda i: (0, 0))],
        out_specs=[row(768), row(2560), row(2048), row(d)],
        out_shape=[jax.ShapeDtypeStruct((s, 768), F32), jax.ShapeDtypeStruct((s, 2560), F32),
                   jax.ShapeDtypeStruct((s, 2048), F32), jax.ShapeDtypeStruct((s, d), CD)],
        compiler_params=_params(1),
    )(x, g, w_t)


def _mix_in_bwd(dps, w_t, x, dres, g, tm):
    s, d = x.shape
    n_in = w_t.shape[0]

    def body(*refs):
        dp_refs = refs[:7]
        w_ref, x_ref, dr_ref, g_ref, dx_ref, dg_ref = refs[7:]

        @pl.when(pl.program_id(0) == 0)
        def _():
            dg_ref[...] = jnp.zeros_like(dg_ref)

        dhv = jnp.zeros((tm, d), F32)
        off = 0
        for ref, width in zip(dp_refs, DP_WIDTHS):
            dhv = dhv + _mm(ref[...], w_ref[off:off + width, :], NN)
            off += width
        xv = x_ref[...]
        r = lax.rsqrt(jnp.mean(xv * xv, axis=-1, keepdims=True) + EPS)
        u = dhv * g_ref[...]
        dx_ref[...] = dr_ref[...] + r * u - xv * (r * r * r) * jnp.mean(u * xv, axis=-1, keepdims=True)
        dg_ref[...] += jnp.sum(dhv * xv * r, axis=0, keepdims=True)

    row = pl.BlockSpec((tm, d), lambda i: (i, 0))
    vec = pl.BlockSpec((1, d), lambda i: (0, 0))
    return pl.pallas_call(
        body, name="mix_in_bwd", grid=(s // tm,),
        in_specs=[pl.BlockSpec((tm, w), lambda i: (i, 0)) for w in DP_WIDTHS]
        + [pl.BlockSpec((n_in, d), lambda i: (0, 0)), row, row, vec],
        out_specs=[row, vec],
        out_shape=[jax.ShapeDtypeStruct((s, d), F32), jax.ShapeDtypeStruct((1, d), F32)],
        compiler_params=_params(1),
    )(*dps, w_t, x, dres, g)


def _headnorm_rope(x, gain, cos, sin, blocksum, swap):
    ss = _xdot(x * x, blocksum)
    r = lax.rsqrt(ss * (1.0 / HEAD_DIM) + EPS)
    y = x * r * gain
    return y * cos + _xdot(y, swap) * sin, r


def _headnorm_rope_bwd(dz, x, gain, cos, sin, blocksum, swap):
    ss = _xdot(x * x, blocksum)
    r = lax.rsqrt(ss * (1.0 / HEAD_DIM) + EPS)
    dy = dz * cos + _xdot(dz * sin, swap)
    u = dy * gain
    mean_ux = _xdot(u * x, blocksum) * (1.0 / HEAD_DIM)
    dx = r * u - x * (r * r * r) * mean_ux
    return dx, jnp.sum(dy * x * r, axis=0, keepdims=True)


def _qk_prep(pqkv, gq, gk, cos2, sin2, tm):
    s = pqkv.shape[0]
    bs512, sw512, eq, swh = _bf(_np_blocksum(512)), _bf(_np_swap32(512)), _bf(_np_expand_q()), _bf(_np_swap_halves())

    def body(q_ref, kv_ref, gq_ref, gk_ref, c_ref, s_ref, bs_ref, sw_ref, eq_ref, swh_ref, qe_ref, k_ref, v_ref, vs_ref):
        c2, s2 = c_ref[...], s_ref[...]
        c8, s8 = jnp.tile(c2, (1, 4)), jnp.tile(s2, (1, 4))
        bs, sw = bs_ref[...], sw_ref[...]
        zq, _ = _headnorm_rope(q_ref[...], gq_ref[...], c8, s8, bs, sw)
        qe_ref[...] = _mm(zq * (HEAD_DIM ** -0.5), eq_ref[...], NN).astype(CD)
        kv = kv_ref[...]
        zk, _ = _headnorm_rope(kv[:, :LANES], gk_ref[...], c2, s2, bs[:LANES, :LANES], sw[:LANES, :LANES])
        k_ref[...] = zk.astype(CD)
        v = kv[:, LANES:]
        v_ref[...] = v.astype(CD)
        vs_ref[...] = _mm(v, swh_ref[...], NN).astype(CD)

    full = lambda a: pl.BlockSpec(a.shape, lambda i: (0,) * a.ndim)
    tab = pl.BlockSpec((tm, LANES), lambda i: (i, 0))
    return pl.pallas_call(
        body, name="qk_prep", grid=(s // tm,),
        in_specs=[pl.BlockSpec((tm, 512), lambda i: (i, 0)), pl.BlockSpec((tm, 256), lambda i: (i, 2)),
                  full(gq), full(gk), tab, tab, full(bs512), full(sw512), full(eq), full(swh)],
        out_specs=[pl.BlockSpec((tm, 1024), lambda i: (i, 0)), tab, tab, tab],
        out_shape=[jax.ShapeDtypeStruct((s, 1024), CD)] + [jax.ShapeDtypeStruct((s, LANES), CD)] * 3,
        compiler_params=_params(1),
    )(pqkv, pqkv, gq, gk, cos2, sin2, bs512, sw512, eq, swh)


def _qk_prep_bwd(pqkv, dqe, dk, dv, gq, gk, cos2, sin2, tm):
    s = pqkv.shape[0]
    bs512, sw512, eqt = _bf(_np_blocksum(512)), _bf(_np_swap32(512)), _bf(_np_expand_q().T)

    def body(q_ref, kv_ref, dqe_ref, dk_ref, dv_ref, gq_ref, gk_ref, c_ref, s_ref, bs_ref, sw_ref, eqt_ref,
             dp_ref, dgq_ref, dgk_ref):
        @pl.when(pl.program_id(0) == 0)
        def _():
            dgq_ref[...] = jnp.zeros_like(dgq_ref)
            dgk_ref[...] = jnp.zeros_like(dgk_ref)

        c2, s2 = c_ref[...], s_ref[...]
        c8, s8 = jnp.tile(c2, (1, 4)), jnp.tile(s2, (1, 4))
        bs, sw = bs_ref[...], sw_ref[...]
        dzq = _xdot(dqe_ref[...], eqt_ref[...]) * (HEAD_DIM ** -0.5)
        dxq, dgq = _headnorm_rope_bwd(dzq, q_ref[...], gq_ref[...], c8, s8, bs, sw)
        kv = kv_ref[...]
        dxk, dgk = _headnorm_rope_bwd(dk_ref[...], kv[:, :LANES], gk_ref[...], c2, s2, bs[:LANES, :LANES], sw[:LANES, :LANES])
        dp_ref[...] = jnp.concatenate([dxq, dxk, dv_ref[...]], axis=1).astype(CD)
        dgq_ref[...] += dgq
        dgk_ref[...] += dgk

    full = lambda a: pl.BlockSpec(a.shape, lambda i: (0,) * a.ndim)
    tab = pl.BlockSpec((tm, LANES), lambda i: (i, 0))
    return pl.pallas_call(
        body, name="qk_prep_bwd", grid=(s // tm,),
        in_specs=[pl.BlockSpec((tm, 512), lambda i: (i, 0)), pl.BlockSpec((tm, 256), lambda i: (i, 2)),
                  pl.BlockSpec((tm, 1024), lambda i: (i, 0)), tab, tab, full(gq), full(gk), tab, tab,
                  full(bs512), full(sw512), full(eqt)],
        out_specs=[pl.BlockSpec((tm, 768), lambda i: (i, 0)), pl.BlockSpec((1, 512), lambda i: (0, 0)),
                   pl.BlockSpec((1, LANES), lambda i: (0, 0))],
        out_shape=[jax.ShapeDtypeStruct((s, 768), CD), jax.ShapeDtypeStruct((1, 512), F32),
                   jax.ShapeDtypeStruct((1, LANES), F32)],
        compiler_params=_params(1),
    )(pqkv, pqkv, dqe, dk, dv, gq, gk, cos2, sin2, bs512, sw512, eqt)


def _attn_fwd(qe, k, v, vs, tq):
    s = k.shape[0]

    def body(qa_ref, qb_ref, k_ref, v_ref, vs_ref, o_ref, lse_ref):
        m = pl.program_id(0)
        grp = m // 2
        kk = k_ref[...]
        outs = []
        for idx, q_ref in enumerate((qa_ref, qb_ref)):
            sc = _mm(q_ref[...], kk, NT)
            mx = jnp.max(sc, axis=-1, keepdims=True)
            e = jnp.exp(sc - mx)
            l = jnp.sum(e, axis=-1, keepdims=True)
            lse_ref[idx] = mx + jnp.log(l)
            p = e * (1.0 / l)
            vsel = jnp.where(grp != idx, vs_ref[...], v_ref[...])
            outs.append(_mm(p, vsel, NN))
        lane = lax.broadcasted_iota(jnp.int32, (1, LANES), 1)
        o_ref[...] = jnp.where(lane < HEAD_DIM, outs[0], outs[1])

    kv = pl.BlockSpec((s, LANES), lambda m, i: (0, 0))
    return pl.pallas_call(
        body, name="attn_fwd", grid=(4, s // tq),
        in_specs=[pl.BlockSpec((tq, LANES), lambda m, i: (i, 2 * m)), pl.BlockSpec((tq, LANES), lambda m, i: (i, 2 * m + 1)),
                  kv, kv, kv],
        out_specs=[pl.BlockSpec((tq, LANES), lambda m, i: (i, m)), pl.BlockSpec((2, tq, 1), lambda m, i: (m, i, 0))],
        out_shape=[jax.ShapeDtypeStruct((s, 512), F32), jax.ShapeDtypeStruct((8, s, 1), F32)],
        compiler_params=_params(2),
    )(qe, qe, k, v, vs)


def _attn_bwd(qe, k, v, doe, delta, lse, tq):
    s = k.shape[0]

    def body(q_ref, k_ref, v_ref, do_ref, dl_ref, lse_ref, dq_ref, dk_ref, dv_ref):
        @pl.when((pl.program_id(0) == 0) & (pl.program_id(1) == 0))
        def _():
            dk_ref[...] = jnp.zeros_like(dk_ref)
            dv_ref[...] = jnp.zeros_like(dv_ref)

        q, kk, do = q_ref[...], k_ref[...], do_ref[...]
        p = jnp.exp(_mm(q, kk, NT) - lse_ref[0])
        dp = _mm(do, v_ref[...], NT)
        ds = p * (dp - jnp.max(dl_ref[...], axis=-1, keepdims=True))
        dq_ref[...] = _mm(ds, kk, NN)
        dk_ref[...] += _mm(ds, q, TN)
        dv_ref[...] += _mm(p, do, TN)

    kv = pl.BlockSpec((s, LANES), lambda h, i: (0, 0))
    blk = pl.BlockSpec((tq, LANES), lambda h, i: (i, h))
    return pl.pallas_call(
        body, name="attn_bwd", grid=(8, s // tq),
        in_specs=[blk, kv, kv, blk, blk, pl.BlockSpec((1, tq, 1), lambda h, i: (h, i, 0))],
        out_specs=[blk, kv, kv],
        out_shape=[jax.ShapeDtypeStruct((s, 1024), F32), jax.ShapeDtypeStruct((s, LANES), F32),
                   jax.ShapeDtypeStruct((s, LANES), F32)],
        compiler_params=_params(2),
    )(qe, k, v, doe, delta, lse)


@jax.custom_vjp
def _mm_nn(a, b):
    return _mm(a, b, NN)


_mm_nn.defvjp(lambda a, b: (_mm(a, b, NN), (a, b)),
              lambda res, g: (_mm(g, res[1], NT), _mm(res[0], g, TN)))


@jax.custom_vjp
def _mm_nt(a, b):
    return _mm(a, b, NT)


_mm_nt.defvjp(lambda a, b: (_mm(a, b, NT), (a, b)),
              lambda res, g: (_mm(g, res[1], NN), _mm(g, res[0], TN)))


@jax.custom_vjp
def _mm_tn(a, b):
    return _mm(a, b, TN)


_mm_tn.defvjp(lambda a, b: (_mm(a, b, TN), (a, b)),
              lambda res, g: (_mm(res[1], g, NT), _mm(res[0], g, NN)))


@jax.custom_vjp
def _cmm(m, mt, x):
    return _xdot_l(m, x)


_cmm.defvjp(lambda m, mt, x: (_xdot_l(m, x), (m, mt)),
            lambda res, g: (jnp.zeros_like(res[0]), jnp.zeros_like(res[1]), _xdot_l(res[1], g)))


def _hgrn_masks(t, rev):
    r = lax.broadcasted_iota(jnp.int32, (t, t), 0)
    c = lax.broadcasted_iota(jnp.int32, (t, t), 1)
    same = jnp.right_shift(r, 5) == jnp.right_shift(c, 5)
    tri = same & ((c >= r) if rev else (c <= r))
    pr = lax.broadcasted_iota(jnp.int32, (LANES, LANES), 0)
    pc = lax.broadcasted_iota(jnp.int32, (LANES, LANES), 1)
    diag = jnp.right_shift(pr, 6) == jnp.right_shift(pc, 6)
    return tri, diag


def _hgrn_gates(xf, lb):
    f = lb + (1.0 - lb) * _sigmoid(xf)
    return 1.0 - f, jnp.log(f)


def _chunk_decay(bl, r0):
    dec = jnp.exp(bl[r0:r0 + CHUNK])
    return jnp.concatenate([dec] * (LANES // CHUNK), axis=0)


def _hgrn_dir(xq, xf, v, lb, state, cm, cmt, tri, diag, rev):
    t = xq.shape[0]
    lo = lax.broadcasted_iota(jnp.int32, (1, LANES), 1) < HEAD_DIM
    q = xq * _sigmoid(xq)
    k, lf = _hgrn_gates(xf, lb)
    cs = _cmm(cm, cmt, lf)
    b, bm, bl = cs[:t], cs[t:2 * t], cs[2 * t:]
    qd = q * jnp.exp(b - bm)
    kd = k * jnp.exp(bm - b)
    kc = k * jnp.exp(bl - b)
    qe = q * jnp.exp(b)
    o = jnp.zeros((t, LANES), F32)
    for half in (lo, ~lo):
        a = _mm_nt(jnp.where(half, qd, 0.0), kd)
        a = jnp.where(tri, a, 0.0)
        o = o + jnp.where(half, _mm_nn(a, v), 0.0)
    n_ch = t // CHUNK
    inter = [None] * n_ch
    for c in (range(n_ch - 1, -1, -1) if rev else range(n_ch)):
        r0 = c * CHUNK
        inter[c] = _mm_nt(qe[r0:r0 + CHUNK], state)
        contrib = _mm_tn(v[r0:r0 + CHUNK], kc[r0:r0 + CHUNK])
        state = _chunk_decay(bl, r0) * state + jnp.where(diag, contrib, 0.0)
    return o + jnp.concatenate(inter, axis=0), state


def _hgrn_state(xf, v, lb, state, cm, diag, rev):
    t = xf.shape[0]
    k, lf = _hgrn_gates(xf, lb)
    cs = _xdot_l(cm, lf)
    b, bl = cs[:t], cs[2 * t:]
    kc = k * jnp.exp(bl - b)
    n_ch = t // CHUNK
    for c in (range(n_ch - 1, -1, -1) if rev else range(n_ch)):
        r0 = c * CHUNK
        contrib = _mm(v[r0:r0 + CHUNK], kc[r0:r0 + CHUNK], TN)
        state = _chunk_decay(bl, r0) * state + jnp.where(diag, contrib, 0.0)
    return state


def _hgrn_lower_bounds(l):
    out = []
    for d in (0, 1):
        l0, l1 = l[2 * d:2 * d + 1, :], l[2 * d + 1:2 * d + 2, :]
        mx = jnp.maximum(l0, l1)
        e0, e1 = jnp.exp(l0 - mx), jnp.exp(l1 - mx)
        out.append(e0 / (e0 + e1))
    return out


def _hgrn_consts(t):
    cf, cb = _np_hgrn_cums(t, False), _np_hgrn_cums(t, True)
    return (_bf(cf), _bf(cf.T), _bf(cb), _bf(cb.T), _bf(_np_blocksum(LANES)))


def _hgrn_fwd(ph, lbl, ng):
    s = ph.shape[0]
    t = min(HG_TILE, s)
    nt = s // t
    consts = _hgrn_consts(t)

    def body(xq_ref, xff_ref, xfb_ref, xi_ref, xg_ref, lbl_ref, ng_ref, cf_ref, cft_ref, cb_ref, cbt_ref, bs_ref,
             o_ref, pre_ref, acc):
        lbf, lbb = _hgrn_lower_bounds(lbl_ref)
        tri_f, diag = _hgrn_masks(t, False)
        tri_b, _ = _hgrn_masks(t, True)
        zero = jnp.zeros((LANES, LANES), F32)

        def rows_of(i):
            return pl.ds(pl.multiple_of(i * t, t), t)

        acc[...] = jnp.zeros_like(acc)

        def step(i, states):
            rf, rb = rows_of(i), rows_of(nt - 1 - i)
            of, sf = _hgrn_dir(xq_ref[rf, :], xff_ref[rf, :], xi_ref[rf, :], lbf, states[0],
                               cf_ref[...], cft_ref[...], tri_f, diag, False)
            ob, sb = _hgrn_dir(xq_ref[rb, :], xfb_ref[rb, :], xi_ref[rb, :], lbb, states[1],
                               cb_ref[...], cbt_ref[...], tri_b, diag, True)
            acc[rf, :] += of
            acc[rb, :] += ob
            return sf, sb

        lax.fori_loop(0, nt, step, (zero, zero))

        def step_n(i, carry):
            rows = rows_of(i)
            o = acc[rows, :]
            ss = _xdot(o * o, bs_ref[...])
            r = lax.rsqrt(ss * (1.0 / HEAD_DIM) + EPS)
            xg = xg_ref[rows, :]
            pre_ref[rows, :] = o
            o_ref[rows, :] = (o * r * ng_ref[...]) * (xg * _sigmoid(xg))
            return carry

        lax.fori_loop(0, nt, step_n, 0)

    col = lambda off: pl.BlockSpec((s, LANES), lambda m: (0, off + m))
    full = lambda a: pl.BlockSpec(a.shape, lambda m: (0,) * a.ndim)
    return pl.pallas_call(
        body, name="hgrn_fwd", grid=(4,),
        in_specs=[col(0), col(4), col(8), col(12), col(16), pl.BlockSpec((4, LANES), lambda m: (0, m)),
                  pl.BlockSpec((1, LANES), lambda m: (0, m))] + [full(c) for c in consts],
        out_specs=[col(0), col(0)],
        out_shape=[jax.ShapeDtypeStruct((s, 512), F32), jax.ShapeDtypeStruct((s, 512), F32)],
        scratch_shapes=[pltpu.VMEM((s, LANES), F32)],
        compiler_params=_params(1),
    )(ph, ph, ph, ph, ph, lbl, ng, *consts)


def _hgrn_bwd(ph, pre, dout, lbl, ng):
    s = ph.shape[0]
    t = min(HG_TILE, s)
    nt = s // t
    consts = _hgrn_consts(t)

    def body(xq_ref, xff_ref, xfb_ref, xi_ref, xg_ref, pre_ref, do_ref, lbl_ref, ng_ref,
             cf_ref, cft_ref, cb_ref, cbt_ref, bs_ref,
             dq_ref, dff_ref, dfb_ref, di_ref, dg_ref, dlb_ref, dng_ref, dpre, dq_acc, dv_acc, states):
        lbf, lbb = _hgrn_lower_bounds(lbl_ref)
        tri_f, diag = _hgrn_masks(t, False)
        tri_b, _ = _hgrn_masks(t, True)
        zero = jnp.zeros((LANES, LANES), F32)
        zrow = jnp.zeros((1, LANES), F32)

        def rows_of(i):
            return pl.ds(pl.multiple_of(i * t, t), t)

        def step_n(i, dng):
            rows = rows_of(i)
            o, xg, do = pre_ref[rows, :], xg_ref[rows, :], do_ref[rows, :]
            bs = bs_ref[...]
            r = lax.rsqrt(_xdot(o * o, bs) * (1.0 / HEAD_DIM) + EPS)
            sg = _sigmoid(xg)
            gate = xg * sg
            don = do * gate
            dg_ref[rows, :] = (do * (o * r * ng_ref[...]) * (sg * (1.0 + xg * (1.0 - sg)))).astype(CD)
            u = don * ng_ref[...]
            dpre[rows, :] = r * u - o * (r * r * r) * (_xdot(u * o, bs) * (1.0 / HEAD_DIM))
            return dng + jnp.sum(don * o * r, axis=0, keepdims=True)

        dng_ref[...] = lax.fori_loop(0, nt, step_n, zrow)

        def step_s(i, carry):
            sf, sb = carry
            tb = nt - 1 - i
            states[0, i] = sf
            states[1, tb] = sb
            rf, rb = rows_of(i), rows_of(tb)
            sf = _hgrn_state(xff_ref[rf, :], xi_ref[rf, :], lbf, sf, cf_ref[...], diag, False)
            sb = _hgrn_state(xfb_ref[rb, :], xi_ref[rb, :], lbb, sb, cb_ref[...], diag, True)
            return sf, sb

        lax.fori_loop(0, nt, step_s, (zero, zero))
        dq_acc[...] = jnp.zeros_like(dq_acc)
        dv_acc[...] = jnp.zeros_like(dv_acc)

        def grad_tile(ti, xf_ref, df_ref, lb, cm, cmt, tri, rev, st, dstate):
            rows = rows_of(ti)
            fn = lambda xq, xf, v, lbv, s_in: _hgrn_dir(xq, xf, v, lbv, s_in, cm, cmt, tri, diag, rev)
            _, vjp = jax.vjp(fn, xq_ref[rows, :], xf_ref[rows, :], xi_ref[rows, :], lb, st)
            dxq, dxf, dv, dlb_t, dstate = vjp((dpre[rows, :], dstate))
            df_ref[rows, :] = dxf.astype(CD)
            dq_acc[rows, :] += dxq
            dv_acc[rows, :] += dv
            return dstate, dlb_t

        def step_g(i, carry):
            dsf, dsb, dlbf, dlbb = carry
            tf, tb = nt - 1 - i, i
            dsf, gf = grad_tile(tf, xff_ref, dff_ref, lbf, cf_ref[...], cft_ref[...], tri_f, False, states[0, tf], dsf)
            dsb, gb = grad_tile(tb, xfb_ref, dfb_ref, lbb, cb_ref[...], cbt_ref[...], tri_b, True, states[1, tb], dsb)
            return dsf, dsb, dlbf + gf, dlbb + gb

        _, _, dlbf, dlbb = lax.fori_loop(0, nt, step_g, (zero, zero, zrow, zrow))
        dlb_ref[0:1, :] = dlbf
        dlb_ref[1:2, :] = dlbb
        dq_ref[...] = dq_acc[...].astype(CD)
        di_ref[...] = dv_acc[...].astype(CD)

    col = lambda off: pl.BlockSpec((s, LANES), lambda m: (0, off + m))
    full = lambda a: pl.BlockSpec(a.shape, lambda m: (0,) * a.ndim)
    stream = jax.ShapeDtypeStruct((s, 512), CD)
    return pl.pallas_call(
        body, name="hgrn_bwd", grid=(4,),
        in_specs=[col(0), col(4), col(8), col(12), col(16), col(0), col(0), pl.BlockSpec((4, LANES), lambda m: (0, m)),
                  pl.BlockSpec((1, LANES), lambda m: (0, m))] + [full(c) for c in consts],
        out_specs=[col(0)] * 5 + [pl.BlockSpec((2, LANES), lambda m: (0, m)), pl.BlockSpec((1, LANES), lambda m: (0, m))],
        out_shape=[stream] * 5 + [jax.ShapeDtypeStruct((2, 512), F32), jax.ShapeDtypeStruct((1, 512), F32)],
        scratch_shapes=[pltpu.VMEM((s, LANES), F32), pltpu.VMEM((s, LANES), F32), pltpu.VMEM((s, LANES), F32),
                        pltpu.VMEM((2, nt, LANES, LANES), F32)],
        compiler_params=_params(1),
    )(ph, ph, ph, ph, ph, pre, dout, lbl, ng, *consts)


def _branch_out(o, w4):
    return jnp.concatenate([_mm(o, w4[j], NN) for j in range(N_SHARD)], axis=1)


def _mix_out_fwd(x, oa, ob, pg, wa, wb, wo, tm):
    s, d = x.shape

    def body(x_ref, oa_ref, ob_ref, ga_ref, gb_ref, wa_ref, wb_ref, wo_ref, xo_ref):
        ya = _branch_out(oa_ref[...], wa_ref)
        yb = _branch_out(ob_ref[...], wb_ref)
        merged = _sigmoid(ga_ref[...]) * ya + _sigmoid(gb_ref[...]) * yb
        xo_ref[...] = x_ref[...] + _mm(merged, wo_ref[...], NN)

    row = pl.BlockSpec((tm, d), lambda i: (i, 0))
    half = pl.BlockSpec((tm, 512), lambda i: (i, 0))
    full = lambda a: pl.BlockSpec(a.shape, lambda i: (0,) * a.ndim)
    return pl.pallas_call(
        body, name="mix_out_fwd", grid=(s // tm,),
        in_specs=[row, half, half, row, pl.BlockSpec((tm, d), lambda i: (i, 1)), full(wa), full(wb), full(wo)],
        out_specs=row, out_shape=jax.ShapeDtypeStruct((s, d), F32),
        compiler_params=_params(1),
    )(x, oa, ob, pg, pg, wa, wb, wo)


def _mix_out_bwd(dx, oa, ob, pg, wa, wb, wo, tm):
    s, d = dx.shape
    bs512, eq, ebc = _bf(_np_blocksum(512)), _bf(_np_expand_q()), _bf(_np_bcast_head())

    def body(dx_ref, oa_ref, ob_ref, ga_ref, gb_ref, wa_ref, wb_ref, wo_ref, bs_ref, eq_ref, ebc_ref,
             dpg_ref, mg_ref, dya_ref, dyb_ref, doe_ref, dl_ref, dob_ref):
        oa = oa_ref[...]
        ya = _branch_out(oa, wa_ref)
        yb = _branch_out(ob_ref[...], wb_ref)
        sa, sb = _sigmoid(ga_ref[...]), _sigmoid(gb_ref[...])
        mg_ref[...] = (sa * ya + sb * yb).astype(CD)
        dm = _mm(dx_ref[...], wo_ref[...], NT)
        dpg_ref[...] = jnp.concatenate([dm * ya * sa * (1.0 - sa), dm * yb * sb * (1.0 - sb)], axis=1).astype(CD)
        dya, dyb = dm * sa, dm * sb
        dya_ref[...] = dya.astype(CD)
        dyb_ref[...] = dyb.astype(CD)
        doa = jnp.zeros(oa.shape, F32)
        dob = jnp.zeros(oa.shape, F32)
        for j in range(N_SHARD):
            doa = doa + _mm(dya[:, 256 * j:256 * j + 256], wa_ref[j], NT)
            dob = dob + _mm(dyb[:, 256 * j:256 * j + 256], wb_ref[j], NT)
        dob_ref[...] = dob
        doe_ref[...] = _mm(doa, eq_ref[...], NN).astype(CD)
        dl_ref[...] = _xdot(_xdot(doa * oa, bs_ref[...]), ebc_ref[...])

    row = pl.BlockSpec((tm, d), lambda i: (i, 0))
    half = pl.BlockSpec((tm, 512), lambda i: (i, 0))
    full = lambda a: pl.BlockSpec(a.shape, lambda i: (0,) * a.ndim)
    wide = jax.ShapeDtypeStruct((s, d), CD)
    return pl.pallas_call(
        body, name="mix_out_bwd", grid=(s // tm,),
        in_specs=[row, half, half, row, pl.BlockSpec((tm, d), lambda i: (i, 1)), full(wa), full(wb), full(wo),
                  full(bs512), full(eq), full(ebc)],
        out_specs=[pl.BlockSpec((tm, 2048), lambda i: (i, 0)), row, row, row, row, row, half],
        out_shape=[jax.ShapeDtypeStruct((s, 2048), CD), wide, wide, wide, wide, jax.ShapeDtypeStruct((s, d), F32),
                   jax.ShapeDtypeStruct((s, 512), F32)],
        compiler_params=_params(1),
    )(dx, oa, ob, pg, pg, wa, wb, wo, bs512, eq, ebc)


def _loss_head(x, g, target, tm):
    s, d = x.shape

    def body(x_ref, g_ref, t_ref, dx_ref, loss_ref, dg_ref):
        @pl.when(pl.program_id(0) == 0)
        def _():
            loss_ref[...] = jnp.zeros_like(loss_ref)
            dg_ref[...] = jnp.zeros_like(dg_ref)

        xv = x_ref[...]
        r = lax.rsqrt(jnp.mean(xv * xv, axis=-1, keepdims=True) + EPS)
        err = xv * r * g_ref[...] - t_ref[...]
        loss_ref[...] += 0.5 * jnp.sum(jnp.mean(err * err, axis=-1, keepdims=True))
        dy = err * (1.0 / d)
        u = dy * g_ref[...]
        dx_ref[...] = r * u - xv * (r * r * r) * jnp.mean(u * xv, axis=-1, keepdims=True)
        dg_ref[...] += jnp.sum(dy * xv * r, axis=0, keepdims=True)

    row = pl.BlockSpec((tm, d), lambda i: (i, 0))
    vec = pl.BlockSpec((1, d), lambda i: (0, 0))
    return pl.pallas_call(
        body, name="loss_head", grid=(s // tm,),
        in_specs=[row, vec, row], out_specs=[row, pl.BlockSpec((8, LANES), lambda i: (0, 0)), vec],
        out_shape=[jax.ShapeDtypeStruct((s, d), F32), jax.ShapeDtypeStruct((8, LANES), F32),
                   jax.ShapeDtypeStruct((1, d), F32)],
        compiler_params=_params(1),
    )(x, g, target)


def _position():
    x, y, c = lax.axis_index("x"), lax.axis_index("y"), lax.axis_index("c")
    return x, y, c, [(1 - x, y), (x, 1 - y), (1 - x, 1 - y)]


def _row_tile(rows, cap=256):
    best = rows
    for cand in range(8, min(rows, cap) + 1, 8):
        if rows % cand == 0:
            best = cand
    return best


def _cast_into_slot(shard, me_idx, dtype):
    rows, cols = shard.shape
    tr = _row_tile(rows)

    def body(me_ref, src_ref, out_ref):
        out_ref[0] = src_ref[...].astype(dtype)

    return pl.pallas_call(
        body, name="cast_into_slot",
        grid_spec=pltpu.PrefetchScalarGridSpec(
            num_scalar_prefetch=1, grid=(rows // tr,),
            in_specs=[pl.BlockSpec((tr, cols), lambda i, me: (i, 0))],
            out_specs=pl.BlockSpec((1, tr, cols), lambda i, me: (me[0], i, 0))),
        out_shape=jax.ShapeDtypeStruct((N_SHARD, rows, cols), dtype),
        compiler_params=_params(1),
    )(me_idx, shard)


HBM_SPEC = pl.BlockSpec(memory_space=pltpu.HBM)
SEM_SPEC = pl.BlockSpec(memory_space=pltpu.SEMAPHORE)
DATAFLOW = pltpu.SideEffectType.DATAFLOW_SIDE_EFFECTING


def _exchange_copies(srcs, lands, send, recv, gather):
    x, y, c, chips = _position()
    me = 2 * x + y
    out = []
    for a in range(len(lands)):
        for k, (px, py) in enumerate(chips):
            src = lands[a].at[me] if gather else srcs[a].at[2 * px + py]
            out.append(pltpu.make_async_remote_copy(src_ref=src, dst_ref=lands[a].at[me], send_sem=send.at[3 * a + k],
                                                    recv_sem=recv.at[3 * a + k], device_id=(px, py, c), device_id_type=MESH))
    return out


def _exchange_start(name, srcs, lands, after):
    ns, nl, na = len(srcs), len(lands), len(after)
    gather = ns == 0

    def body(*refs):
        src_refs, land_refs = refs[:ns], refs[ns:ns + nl]
        send, recv = refs[ns + nl + na], refs[ns + nl + na + 1]
        token = refs[-1]
        for cp in _exchange_copies(src_refs, land_refs, send, recv, gather):
            cp.start()
        token[...] = jnp.zeros_like(token)

    arrays = [pltpu.with_memory_space_constraint(a, pltpu.HBM) for a in list(srcs) + list(lands)]
    outs = pl.pallas_call(
        body, name=name,
        out_shape=(pltpu.SemaphoreType.DMA((3 * nl,)), pltpu.SemaphoreType.DMA((3 * nl,)),
                   *[pltpu.HBM(a.shape, a.dtype) for a in arrays], jax.ShapeDtypeStruct((8, LANES), F32)),
        in_specs=[HBM_SPEC] * (ns + nl) + [ANY] * na,
        out_specs=(SEM_SPEC, SEM_SPEC, *[HBM_SPEC] * (ns + nl), pl.BlockSpec(memory_space=pltpu.VMEM)),
        input_output_aliases={i: 2 + i for i in range(ns + nl)},
        compiler_params=pltpu.CompilerParams(has_side_effects=DATAFLOW),
    )(*arrays, *after)
    return outs[0], outs[1], list(outs[2:2 + ns]), list(outs[2 + ns:2 + ns + nl]), outs[-1]


def _exchange_wait(name, send, recv, srcs, lands, after):
    ns, nl, na = len(srcs), len(lands), len(after)
    gather = ns == 0

    def body(*refs):
        src_refs, land_refs = refs[:ns], refs[ns:ns + nl]
        send_ref, recv_ref = refs[ns + nl], refs[ns + nl + 1]
        for cp in _exchange_copies(src_refs, land_refs, send_ref, recv_ref, gather):
            cp.wait_send()
            cp.wait_recv()

    outs = pl.pallas_call(
        body, name=name,
        out_shape=tuple(pltpu.HBM(a.shape, a.dtype) for a in list(srcs) + list(lands)),
        in_specs=[HBM_SPEC] * (ns + nl) + [SEM_SPEC, SEM_SPEC] + [ANY] * na,
        out_specs=tuple([HBM_SPEC] * (ns + nl)),
        input_output_aliases={i: i for i in range(ns + nl)},
        compiler_params=pltpu.CompilerParams(has_side_effects=DATAFLOW),
    )(*srcs, *lands, send, recv, *after)
    return list(outs[ns:])


def _pair_exchange(grads):
    n = len(grads)

    def body(*refs):
        src, dst = refs[:n], refs[n:2 * n]
        send, recv = refs[2 * n:]
        x, y, c, _ = _position()
        copies = []
        for a in range(n):
            half = src[a].shape[1] // 2
            cp = pltpu.make_async_remote_copy(
                src_ref=src[a].at[:, pl.ds((1 - c) * half, half), :], dst_ref=dst[a], send_sem=send.at[a],
                recv_sem=recv.at[a], device_id=(x, y, 1 - c), device_id_type=MESH)
            cp.start()
            copies.append(cp)
        for cp in copies:
            cp.wait()

    return pl.pallas_call(
        body, name="grad_pair_exchange", in_specs=[ANY] * n, out_specs=[ANY] * n,
        out_shape=[jax.ShapeDtypeStruct((g.shape[0], g.shape[1] // 2, g.shape[2]), g.dtype) for g in grads],
        scratch_shapes=[pltpu.SemaphoreType.DMA((n,)), pltpu.SemaphoreType.DMA((n,))],
    )(*grads)


def _pair_sum(g, got, c_idx, me_idx):
    nsh, rows, cols = g.shape
    half = rows // 2

    def body(c_ref, me_ref, g_ref, got_ref, s_ref, own_ref):
        sm = g_ref[...] + got_ref[...]
        s_ref[...] = sm.astype(CD)

        @pl.when(pl.program_id(0) == me_ref[0])
        def _():
            own_ref[...] = sm[0]

    return pl.pallas_call(
        body, name="grad_pair_sum",
        grid_spec=pltpu.PrefetchScalarGridSpec(
            num_scalar_prefetch=2, grid=(nsh,),
            in_specs=[pl.BlockSpec((1, half, cols), lambda j, c, me: (j, c[0], 0)),
                      pl.BlockSpec((1, half, cols), lambda j, c, me: (j, 0, 0))],
            out_specs=[pl.BlockSpec((1, half, cols), lambda j, c, me: (j, 0, 0)),
                       pl.BlockSpec((half, cols), lambda j, c, me: (0, 0))]),
        out_shape=[jax.ShapeDtypeStruct((nsh, half, cols), CD), jax.ShapeDtypeStruct((half, cols), F32)],
        compiler_params=_params(1),
    )(c_idx, me_idx, g, got)


def _chip_sum(own, got, me_idx):
    nsh, half, cols = got.shape

    def body(me_ref, own_ref, got_ref, out_ref):
        j = pl.program_id(0)
        term = jnp.where(j == me_ref[0], own_ref[...], got_ref[0].astype(F32))

        @pl.when(j == 0)
        def _():
            out_ref[...] = term

        @pl.when(j > 0)
        def _():
            out_ref[...] += term

    return pl.pallas_call(
        body, name="grad_chip_sum",
        grid_spec=pltpu.PrefetchScalarGridSpec(
            num_scalar_prefetch=1, grid=(nsh,),
            in_specs=[pl.BlockSpec((half, cols), lambda j, me: (0, 0)), pl.BlockSpec((1, half, cols), lambda j, me: (j, 0, 0))],
            out_specs=pl.BlockSpec((half, cols), lambda j, me: (0, 0))),
        out_shape=jax.ShapeDtypeStruct((half, cols), F32),
        compiler_params=_params(1),
    )(me_idx, own, got)


def _pair_share(halves):
    n = len(halves)

    def body(*refs):
        src, dst = refs[:n], refs[n:2 * n]
        send, recv = refs[2 * n:]
        x, y, c, _ = _position()
        copies = []
        for a in range(n):
            cp = pltpu.make_async_remote_copy(src_ref=src[a], dst_ref=dst[a], send_sem=send.at[a],
                                              recv_sem=recv.at[a], device_id=(x, y, 1 - c), device_id_type=MESH)
            cp.start()
            copies.append(cp)
        for cp in copies:
            cp.wait()

    return pl.pallas_call(
        body, name="grad_pair_share", in_specs=[ANY] * n, out_specs=[ANY] * n,
        out_shape=[jax.ShapeDtypeStruct(h.shape, h.dtype) for h in halves],
        scratch_shapes=[pltpu.SemaphoreType.DMA((n,)), pltpu.SemaphoreType.DMA((n,))],
    )(*halves)


def _small_allreduce(buf):
    rows, cols = buf.shape

    def body(src_ref, out_ref, slots, send, recv):
        x, y, c, _ = _position()
        me = 4 * x + 2 * y + c
        slots[me] = src_ref[...]
        copies = []
        k = 0
        for dx in (0, 1):
            for dy in (0, 1):
                for dc in (0, 1):
                    if (dx, dy, dc) == (0, 0, 0):
                        continue
                    peer = (jnp.where(dx, 1 - x, x), jnp.where(dy, 1 - y, y), jnp.where(dc, 1 - c, c))
                    cp = pltpu.make_async_remote_copy(src_ref=src_ref, dst_ref=slots.at[me], send_sem=send.at[k],
                                                      recv_sem=recv.at[k], device_id=peer, device_id_type=MESH)
                    cp.start()
                    copies.append(cp)
                    k += 1
        for cp in copies:
            cp.wait()
        total = slots[0]
        for dev in range(1, N_DEV):
            total = total + slots[dev]
        out_ref[...] = total

    vm = pl.BlockSpec(memory_space=pltpu.VMEM)
    return pl.pallas_call(
        body, name="small_allreduce", in_specs=[vm], out_specs=vm,
        out_shape=jax.ShapeDtypeStruct((rows, cols), F32),
        scratch_shapes=[pltpu.VMEM((N_DEV, rows, cols), F32), pltpu.SemaphoreType.DMA((N_DEV - 1,)),
                        pltpu.SemaphoreType.DMA((N_DEV - 1,))],
    )(buf)


def _adamw_math(w, gv, m, v):
    mn = ADAM_B1 * m + (1.0 - ADAM_B1) * gv
    vn = ADAM_B2 * v + (1.0 - ADAM_B2) * (gv * gv)
    m_hat = mn / (1.0 - ADAM_B1 ** ADAM_STEP)
    v_hat = vn / (1.0 - ADAM_B2 ** ADAM_STEP)
    return -ADAM_LR * (m_hat / (jnp.sqrt(v_hat) + ADAM_EPS) + ADAM_WD * w), mn, vn


def _adamw(w, g, m, v):
    rows, cols = w.shape
    tr = _row_tile(rows)

    def body(w_ref, g_ref, m_ref, v_ref, d_ref, mo_ref, vo_ref):
        d_ref[...], mo_ref[...], vo_ref[...] = _adamw_math(w_ref[...], g_ref[...], m_ref[...], v_ref[...])

    blk = pl.BlockSpec((tr, cols), lambda i: (i, 0))
    shp = jax.ShapeDtypeStruct((rows, cols), F32)
    return pl.pallas_call(
        body, name="adamw", grid=(rows // tr,), in_specs=[blk] * 4, out_specs=[blk] * 3, out_shape=[shp] * 3,
        compiler_params=_params(1),
    )(w, g, m, v)


def _adamw_halves(w, own, got, m, v, c_idx):
    rows, cols = w.shape
    tr = _row_tile(rows // 2)
    per_half = rows // 2 // tr

    def body(c_ref, w_ref, own_ref, got_ref, m_ref, v_ref, d_ref, mo_ref, vo_ref, g_ref):
        mine = (pl.program_id(0) // per_half) == c_ref[0]
        gv = jnp.where(mine, own_ref[...], got_ref[...])
        g_ref[...] = gv
        d_ref[...], mo_ref[...], vo_ref[...] = _adamw_math(w_ref[...], gv, m_ref[...], v_ref[...])

    blk = pl.BlockSpec((tr, cols), lambda i, c: (i, 0))
    hblk = pl.BlockSpec((tr, cols), lambda i, c: (i % per_half, 0))
    shp = jax.ShapeDtypeStruct((rows, cols), F32)
    return pl.pallas_call(
        body, name="adamw_halves",
        grid_spec=pltpu.PrefetchScalarGridSpec(num_scalar_prefetch=1, grid=(rows // tr,),
                                               in_specs=[blk, hblk, hblk, blk, blk], out_specs=[blk] * 4),
        out_shape=[shp] * 4, compiler_params=_params(1),
    )(c_idx, w, own, got, m, v)


def _local_step(x, target, norm_gains, q_g, k_g, ng, weights_of, grads_done):
    s = x.shape[0]
    tm = min(512, s)
    tq = min(256, s)
    g1, gm, g2, gf = norm_gains
    cos2, sin2 = _rope_tables(s)
    gq8 = jnp.tile(q_g, (1, 8))
    gk2 = jnp.tile(k_g, (1, 2))

    tn = min(256, s)
    tk = min(1024, s)
    w1 = weights_of(1, ())
    x1, a1, b1, h1 = _ffn_fwd(x, g1, w1["g1"], w1["u1"], w1["d1"], tm)
    w2 = weights_of(2, (x1,))
    lbl = w2["lbl"]
    pqkv, ph, pg, hm = _mix_in_fwd(x1, gm, w2["in"], tn)
    qe, kr, vr, vs = _qk_prep(pqkv, gq8, gk2, cos2, sin2, tm)
    oa, lse = _attn_fwd(qe, kr, vr, vs, tq)
    ob, pre = _hgrn_fwd(ph, lbl, ng)
    x2 = _mix_out_fwd(x1, oa, ob, pg, w2["a"], w2["b"], w2["o"], tm)
    w3 = weights_of(3, (x2,))
    x3, a2, b2, h2 = _ffn_fwd(x2, g2, w3["g2"], w3["u2"], w3["d2"], tm)
    dx3, loss, dgf = _loss_head(x3, gf, target, tm)

    dx2, da2, db2, f2, dg2 = _ffn_bwd(dx3, x2, g2, a2, b2, w3["g2"], w3["u2"], w3["d2"], tm)
    grads_done(3, dict(g2=_dw_shared_a("dw_gate", h2, da2, tk), u2=_dw_shared_a("dw_gate", h2, db2, tk),
                       d2=_dw_shared_b("dw_down", f2, dx3, tk, 0.5)))

    dpg, mg, dya, dyb, doe, delta, dob = _mix_out_bwd(dx2, oa, ob, pg, w2["a"], w2["b"], w2["o"], tm)
    g_o = _dw_colblocks("dw_out", mg, dx2, 1, tk).reshape(N_SHARD, D_MODEL // N_SHARD, D_MODEL)
    g_a = _dw_colblocks("dw_branch", oa, dya, N_SHARD, tk)
    g_b = _dw_colblocks("dw_branch", ob, dyb, N_SHARD, tk)
    dqe, dk, dv = _attn_bwd(qe, kr, vr, doe, delta, lse, tq)
    dqkv, dgq, dgk = _qk_prep_bwd(pqkv, dqe, dk, dv, gq8, gk2, cos2, sin2, tm)
    dhq, dhff, dhfb, dhi, dhg, dlb, dng = _hgrn_bwd(ph, pre, dob, lbl, ng)
    dps = (dqkv, dhq, dhff, dhfb, dhi, dhg, dpg)
    g_in = _dw_in(dps, hm, min(2048, s)).reshape(N_SHARD, -1, D_MODEL)
    grads_done(2, {"in": g_in, "a": g_a, "b": g_b, "o": g_o})
    dx1, dgm = _mix_in_bwd(dps, w2["in"], x1, dx2, gm, tn)

    dx0, da1, db1, f1, dg1 = _ffn_bwd(dx1, x, g1, a1, b1, w1["g1"], w1["u1"], w1["d1"], tm)
    grads_done(1, dict(g1=_dw_shared_a("dw_gate", h1, da1, tk), u1=_dw_shared_a("dw_gate", h1, db1, tk),
                       d1=_dw_shared_b("dw_down", f1, dx1, tk, 0.5)))
    small = dict(g1=dg1, gm=dgm, g2=dg2, gf=dgf, gq=dgq, gk=dgk, lb=dlb, ng=dng)
    return loss, dx0, small, lbl


GROUPS = {1: ("g1", "u1", "d1"), 2: ("in", "a", "b", "o"), 3: ("g2", "u2", "d2")}
BIG = GROUPS[1] + GROUPS[2] + GROUPS[3]


def _pack_rows(vectors, width):
    rows = []
    for vct in vectors:
        flat = vct.reshape(-1)
        pad = (-flat.shape[0]) % width
        rows.append(jnp.pad(flat, (0, pad)).reshape(-1, width))
    return jnp.concatenate(rows, axis=0)


def kernel(x, ffn1_norm_g, ffn1_w_gate, ffn1_w_up, ffn1_w_down, mix_norm_g, w_in, q_norm_g, k_norm_g, hgrn_lb_logits, hgrn_out_norm_g, w_branch_attn, w_branch_hgrn, w_out, ffn2_norm_g, ffn2_w_gate, ffn2_w_up, ffn2_w_down, final_norm_g, loss_target, m_ffn1_norm_g, m_ffn1_w_gate, m_ffn1_w_up, m_ffn1_w_down, m_mix_norm_g, m_w_in, m_q_norm_g, m_k_norm_g, m_hgrn_lb_logits, m_hgrn_out_norm_g, m_w_branch_attn, m_w_branch_hgrn, m_w_out, m_ffn2_norm_g, m_ffn2_w_gate, m_ffn2_w_up, m_ffn2_w_down, m_final_norm_g, v_ffn1_norm_g, v_ffn1_w_gate, v_ffn1_w_up, v_ffn1_w_down, v_mix_norm_g, v_w_in, v_q_norm_g, v_k_norm_g, v_hgrn_lb_logits, v_hgrn_out_norm_g, v_w_branch_attn, v_w_branch_hgrn, v_w_out, v_ffn2_norm_g, v_ffn2_w_gate, v_ffn2_w_up, v_ffn2_w_down, v_final_norm_g):
    xi, yi, ci = lax.axis_index("x"), lax.axis_index("y"), lax.axis_index("c")
    me = 2 * xi + yi
    c_idx = jnp.reshape(ci, (1,)).astype(jnp.int32)
    me_idx = jnp.reshape(me, (1,)).astype(jnp.int32)

    big_w = dict(g1=ffn1_w_gate[0], u1=ffn1_w_up[0], d1=ffn1_w_down[0], a=w_branch_attn[0], b=w_branch_hgrn[0],
                 o=w_out[0], g2=ffn2_w_gate[0], u2=ffn2_w_up[0], d2=ffn2_w_down[0])
    big_w["in"] = w_in[0]
    big_m = dict(g1=m_ffn1_w_gate[0], u1=m_ffn1_w_up[0], d1=m_ffn1_w_down[0], a=m_w_branch_attn[0], b=m_w_branch_hgrn[0],
                 o=m_w_out[0], g2=m_ffn2_w_gate[0], u2=m_ffn2_w_up[0], d2=m_ffn2_w_down[0])
    big_m["in"] = m_w_in[0]
    big_v = dict(g1=v_ffn1_w_gate[0], u1=v_ffn1_w_up[0], d1=v_ffn1_w_down[0], a=v_w_branch_attn[0], b=v_w_branch_hgrn[0],
                 o=v_w_out[0], g2=v_ffn2_w_gate[0], u2=v_ffn2_w_up[0], d2=v_ffn2_w_down[0])
    big_v["in"] = v_w_in[0]

    slots = {n: _cast_into_slot(big_w[n].T if n == "in" else big_w[n], me_idx, CD) for n in BIG}
    lbl_slot = _cast_into_slot(hgrn_lb_logits.reshape(4, LANES), me_idx, F32)
    started, token = {}, ()
    for grp in (1, 2, 3):
        lands = [slots[n] for n in GROUPS[grp]] + ([lbl_slot] if grp == 2 else [])
        send, recv, _, lands, tok = _exchange_start("gather%d_start" % grp, [], lands, token)
        started[grp], token = (send, recv, lands), (tok,)

    def weights_of(grp, after):
        send, recv, lands = started[grp]
        got = _exchange_wait("gather%d_wait" % grp, send, recv, [], lands, tuple(after) + (token if grp == 1 else ()))
        w = dict(zip(GROUPS[grp], got))
        if grp == 2:
            w["in"] = w["in"].reshape(-1, D_MODEL)
            w["o"] = w["o"].reshape(D_MODEL, D_MODEL)
            w["lbl"] = jnp.transpose(got[-1], (1, 0, 2)).reshape(4, N_SHARD * LANES)
        return w

    pending = {}

    def grads_done(grp, grads):
        names = list(grads)
        glist = [grads[n] for n in names]
        got = _pair_exchange(glist)
        sums, owns = zip(*[_pair_sum(g, r, c_idx, me_idx) for g, r in zip(glist, got)])
        lands = [lax.empty(s_.shape, s_.dtype) for s_ in sums]
        send, recv, srcs, lands, tok = _exchange_start("reduce%d_start" % grp, list(sums), lands, ())
        pending[grp] = (names, send, recv, srcs, lands, owns, tok)

    def reduced_halves(grp, after):
        names, send, recv, srcs, lands, owns, _ = pending[grp]
        parts = _exchange_wait("reduce%d_wait" % grp, send, recv, srcs, lands, after)
        return names, [_chip_sum(o, p, me_idx) for o, p in zip(owns, parts)]

    loss, dx, small, lbl = _local_step(
        x[0], loss_target[0], (ffn1_norm_g, mix_norm_g, ffn2_norm_g, final_norm_g.reshape(1, -1)),
        q_norm_g, k_norm_g, hgrn_out_norm_g, weights_of, grads_done)

    dgq = small["gq"].reshape(8, HEAD_DIM).sum(axis=0)
    dgk = small["gk"].reshape(2, HEAD_DIM).sum(axis=0)
    lb_full = _hgrn_lower_bounds(lbl)
    dlog = []
    for d in (0, 1):
        t = small["lb"][d:d + 1] * lb_full[d] * (1.0 - lb_full[d])
        dlog += [t, -t]
    small_list = [small["g1"], small["gm"], small["g2"], small["gf"], small["ng"], dgq, dgk, jnp.concatenate(dlog, axis=0), loss[0, 0]]
    packed = _pack_rows(small_list, D_MODEL)
    n_rows = packed.shape[0]
    packed = jnp.pad(packed, ((0, (-n_rows) % 8), (0, 0)))
    red = _small_allreduce(packed)
    loss_out = red[n_rows - 1, 0]
    sg = dict(g1=red[0:1], gm=red[1:2], g2=red[2:3], gf=red[3], ng=red[4:5, :512], gq=red[5:6, :HEAD_DIM],
              gk=red[6:7, :HEAD_DIM])
    dlog_full = red[7:9].reshape(2, 2, 512)
    sg["lb"] = lax.dynamic_slice_in_dim(dlog_full, me * LANES, LANES, axis=2)

    small_w = dict(g1=ffn1_norm_g, gm=mix_norm_g, g2=ffn2_norm_g, gf=final_norm_g, ng=hgrn_out_norm_g, gq=q_norm_g,
                   gk=k_norm_g, lb=hgrn_lb_logits)
    small_m = dict(g1=m_ffn1_norm_g, gm=m_mix_norm_g, g2=m_ffn2_norm_g, gf=m_final_norm_g, ng=m_hgrn_out_norm_g,
                   gq=m_q_norm_g, gk=m_k_norm_g, lb=m_hgrn_lb_logits)
    small_v = dict(g1=v_ffn1_norm_g, gm=v_mix_norm_g, g2=v_ffn2_norm_g, gf=v_final_norm_g, ng=v_hgrn_out_norm_g,
                   gq=v_q_norm_g, gk=v_k_norm_g, lb=v_hgrn_lb_logits)
    small_names = ("g1", "gm", "g2", "gf", "ng", "gq", "gk", "lb")
    pack = lambda dct: _pack_rows([dct[n] for n in small_names], D_MODEL)
    pw, pgr, pm, pv = pack(small_w), pack(sg), pack(small_m), pack(small_v)
    pad8 = lambda a: jnp.pad(a, ((0, (-a.shape[0]) % 8), (0, 0)))
    sd, sm_, sv_ = _adamw(pad8(pw), pad8(pgr), pad8(pm), pad8(pv))

    def unpack(buf):
        out, r = {}, 0
        for n in small_names:
            size = small_w[n].size
            nr = -(-size // D_MODEL)
            out[n] = buf[r:r + nr].reshape(-1)[:size].reshape(small_w[n].shape)
            r += nr
        return out

    sdelta, snew_m, snew_v = unpack(sd), unpack(sm_), unpack(sv_)
    sgrad = {n: sg[n].reshape(small_w[n].shape) for n in small_names}

    bdelta, bnew_m, bnew_v, bgrad = {}, {}, {}, {}

    def update(names, halves):
        for n, own, got in zip(names, halves, _pair_share(halves)):
            if n == "in":
                whole = jnp.concatenate([jnp.where(ci == 0, own, got), jnp.where(ci == 0, got, own)], axis=0).T
                dlt, mn, vn = _adamw(big_w[n], whole, big_m[n], big_v[n])
            else:
                dlt, mn, vn, whole = _adamw_halves(big_w[n], own, got, big_m[n], big_v[n], c_idx)
            bdelta[n], bnew_m[n], bnew_v[n], bgrad[n] = dlt[None], mn[None], vn[None], whole[None]

    names3, halves3 = reduced_halves(3, (pending[1][-1],))
    names2, halves2 = reduced_halves(2, (halves3[0],))
    update(names3 + names2, halves3 + halves2)
    names1, halves1 = reduced_halves(1, (bdelta[names2[-1]],))
    update(names1, halves1)

    order = [("s", "g1"), ("b", "g1"), ("b", "u1"), ("b", "d1"), ("s", "gm"), ("b", "in"), ("s", "gq"), ("s", "gk"),
             ("s", "lb"), ("s", "ng"), ("b", "a"), ("b", "b"), ("b", "o"), ("s", "g2"), ("b", "g2"), ("b", "u2"),
             ("b", "d2"), ("s", "gf")]
    outs = [loss_out, dx[None]]
    for table_s, table_b in ((sgrad, bgrad), (sdelta, bdelta), (snew_m, bnew_m), (snew_v, bnew_v)):
        outs += [(table_s if kind == "s" else table_b)[n] for kind, n in order]
    return tuple(outs)
```

```python
import functools

import numpy as np
import jax
import jax.numpy as jnp
from jax import lax
from jax.experimental import pallas as pl
from jax.experimental.pallas import tpu as pltpu

F32 = jnp.float32
BF16 = jnp.bfloat16
CD = jnp.bfloat16

EPS = 1e-6
D_MODEL = 1024
HEAD_DIM = 64
GRID_W = 64
ROPE_THETA = 10000.0
CHUNK = 32
N_SHARD = 4
N_DEV = 8
VMEM_LIMIT = 56 * 1024 * 1024
LANES = 128
HG_TILE = 256

ADAM_LR = 0.001
ADAM_B1 = 0.9
ADAM_B2 = 0.999
ADAM_EPS = 1e-08
ADAM_WD = 0.01
ADAM_STEP = 10

NN = (((1,), (0,)), ((), ()))
NT = (((1,), (1,)), ((), ()))
TN = (((0,), (0,)), ((), ()))
MESH = pl.DeviceIdType.MESH
ANY = pl.BlockSpec(memory_space=pl.ANY)


def _mm(a, b, dn):
    return lax.dot_general(a.astype(CD), b.astype(CD), dn, preferred_element_type=F32)


def _split3(x):
    hi = x.astype(BF16)
    r = x - hi.astype(F32)
    mid = r.astype(BF16)
    lo = (r - mid.astype(F32)).astype(BF16)
    return hi, mid, lo


def _xdot(x, m):
    rows = x.shape[0]
    r = lax.dot_general(jnp.concatenate(_split3(x), axis=0), m, NN, preferred_element_type=F32)
    return r[:rows] + r[rows:2 * rows] + r[2 * rows:]


def _xdot_l(m, x):
    cols = x.shape[1]
    r = lax.dot_general(m, jnp.concatenate(_split3(x), axis=1), NN, preferred_element_type=F32)
    return r[:, :cols] + r[:, cols:2 * cols] + r[:, 2 * cols:]


def _params(n_grid):
    return pltpu.CompilerParams(dimension_semantics=("arbitrary",) * n_grid, vmem_limit_bytes=VMEM_LIMIT)


def _sigmoid(x):
    return jax.nn.sigmoid(x)


def _np_blocksum(n):
    i = np.arange(n)
    return (i[:, None] // HEAD_DIM == i[None, :] // HEAD_DIM).astype(np.float32)


def _np_swap32(n):
    i = np.arange(n)
    partner = np.where(i % HEAD_DIM < HEAD_DIM // 2, i + HEAD_DIM // 2, i - HEAD_DIM // 2)
    m = np.zeros((n, n), np.float32)
    m[i, partner] = 1.0
    return m


def _np_expand_q():
    m = np.zeros((512, 1024), np.float32)
    for h in range(8):
        g = h // 4
        for d in range(HEAD_DIM):
            m[64 * h + d, 128 * h + 64 * g + d] = 1.0
    return m


def _np_bcast_head():
    m = np.zeros((512, 1024), np.float32)
    for h in range(8):
        m[64 * h, 128 * h:128 * h + 128] = 1.0
    return m


def _np_swap_halves():
    m = np.zeros((128, 128), np.float32)
    i = np.arange(128)
    m[i, (i + 64) % 128] = 1.0
    return m


def _np_hgrn_cums(t, rev):
    r = np.arange(t)[:, None]
    c = np.arange(t)[None, :]
    same = (r // CHUNK) == (c // CHUNK)
    if not rev:
        cum = same & (c <= r)
        mid = same & (c % CHUNK <= CHUNK // 2 - 1)
    else:
        cum = same & (c >= r)
        mid = same & (c % CHUNK >= CHUNK // 2)
    return np.concatenate([cum, mid, same], axis=0).astype(np.float32)


def _bf(a):
    return jnp.asarray(a, dtype=BF16)


def _rope_tables(seq_len):
    rows = seq_len // GRID_W
    row = jnp.repeat(jnp.arange(rows, dtype=F32), GRID_W)
    col = jnp.tile(jnp.arange(GRID_W, dtype=F32), rows)
    n_freq = HEAD_DIM // 4
    inv = ROPE_THETA ** (-jnp.arange(n_freq, dtype=F32) / n_freq)
    ang = jnp.concatenate([row[:, None] * inv, col[:, None] * inv], axis=-1)
    cos, sin = jnp.cos(ang), jnp.sin(ang)
    c64 = jnp.concatenate([cos, cos], axis=-1)
    s64 = jnp.concatenate([-sin, sin], axis=-1)
    return jnp.tile(c64, (1, 2)), jnp.tile(s64, (1, 2))


def _ffn_fwd(x, g, wg, wu, wd, tm):
    s, d = x.shape
    nsh, fs, _ = wg.shape

    def body(x_ref, g_ref, wg_ref, wu_ref, wd_ref, xo_ref, a_ref, b_ref, hb_ref, acc, hs):
        j = pl.program_id(1)

        @pl.when(j == 0)
        def _():
            xv = x_ref[...]
            r = lax.rsqrt(jnp.mean(xv * xv, axis=-1, keepdims=True) + EPS)
            h = (xv * r * g_ref[...]).astype(CD)
            hs[...] = h
            hb_ref[...] = h
            acc[...] = jnp.zeros_like(acc)

        h = hs[...]
        a = _mm(h, wg_ref[0], NT)
        b = _mm(h, wu_ref[0], NT)
        f = a * _sigmoid(a) * b
        acc[...] += _mm(f, wd_ref[0], NN)
        a_ref[0] = a.astype(CD)
        b_ref[0] = b.astype(CD)

        @pl.when(j == nsh - 1)
        def _():
            xo_ref[...] = x_ref[...] + 0.5 * acc[...]

    return pl.pallas_call(
        body, name="ffn_fwd", grid=(s // tm, nsh),
        in_specs=[pl.BlockSpec((tm, d), lambda i, j: (i, 0)), pl.BlockSpec((1, d), lambda i, j: (0, 0))]
        + [pl.BlockSpec((1, fs, d), lambda i, j: (j, 0, 0))] * 3,
        out_specs=[pl.BlockSpec((tm, d), lambda i, j: (i, 0)), pl.BlockSpec((1, tm, fs), lambda i, j: (j, i, 0)),
                   pl.BlockSpec((1, tm, fs), lambda i, j: (j, i, 0)), pl.BlockSpec((tm, d), lambda i, j: (i, 0))],
        out_shape=[jax.ShapeDtypeStruct((s, d), F32), jax.ShapeDtypeStruct((nsh, s, fs), CD),
                   jax.ShapeDtypeStruct((nsh, s, fs), CD), jax.ShapeDtypeStruct((s, d), CD)],
        scratch_shapes=[pltpu.VMEM((tm, d), F32), pltpu.VMEM((tm, d), CD)],
        compiler_params=_params(2),
    )(x, g, wg, wu, wd)


def _ffn_bwd(dout, x, g, a, b, wg, wu, wd, tm):
    s, d = x.shape
    nsh, fs, _ = wg.shape

    def body(do_ref, x_ref, g_ref, a_ref, b_ref, wg_ref, wu_ref, wd_ref, dx_ref, da_ref, db_ref, f_ref, dg_ref, dh):
        i = pl.program_id(0)
        j = pl.program_id(1)

        @pl.when(j == 0)
        def _():
            dh[...] = jnp.zeros_like(dh)

        @pl.when((i == 0) & (j == 0))
        def _():
            dg_ref[...] = jnp.zeros_like(dg_ref)

        av = a_ref[0].astype(F32)
        bv = b_ref[0].astype(F32)
        sg = _sigmoid(av)
        sl = av * sg
        df = 0.5 * _mm(do_ref[...], wd_ref[0], NT)
        da = df * bv * (sg * (1.0 + av * (1.0 - sg)))
        db = df * sl
        dh[...] += _mm(da, wg_ref[0], NN) + _mm(db, wu_ref[0], NN)
        da_ref[0] = da.astype(CD)
        db_ref[0] = db.astype(CD)
        f_ref[0] = (sl * bv).astype(CD)

        @pl.when(j == nsh - 1)
        def _():
            xv = x_ref[...]
            r = lax.rsqrt(jnp.mean(xv * xv, axis=-1, keepdims=True) + EPS)
            dhv = dh[...]
            u = dhv * g_ref[...]
            dx_ref[...] = do_ref[...] + r * u - xv * (r * r * r) * jnp.mean(u * xv, axis=-1, keepdims=True)
            dg_ref[...] += jnp.sum(dhv * xv * r, axis=0, keepdims=True)

    act = pl.BlockSpec((1, tm, fs), lambda i, j: (j, i, 0))
    row = pl.BlockSpec((tm, d), lambda i, j: (i, 0))
    return pl.pallas_call(
        body, name="ffn_bwd", grid=(s // tm, nsh),
        in_specs=[row, row, pl.BlockSpec((1, d), lambda i, j: (0, 0)), act, act]
        + [pl.BlockSpec((1, fs, d), lambda i, j: (j, 0, 0))] * 3,
        out_specs=[row, act, act, act, pl.BlockSpec((1, d), lambda i, j: (0, 0))],
        out_shape=[jax.ShapeDtypeStruct((s, d), F32), jax.ShapeDtypeStruct((nsh, s, fs), CD),
                   jax.ShapeDtypeStruct((nsh, s, fs), CD), jax.ShapeDtypeStruct((nsh, s, fs), CD),
                   jax.ShapeDtypeStruct((1, d), F32)],
        scratch_shapes=[pltpu.VMEM((tm, d), F32)],
        compiler_params=_params(2),
    )(dout, x, g, a, b, wg, wu, wd)


def _tn_call(name, operands, in_specs, out_shape, out_spec, grid, acc_shape, pick, scale=1.0):
    nk = grid[-1]
    n_in = len(operands)

    def body(*refs):
        out_ref, acc = refs[n_in], refs[n_in + 1]
        k = pl.program_id(len(grid) - 1)

        @pl.when(k == 0)
        def _():
            acc[...] = jnp.zeros_like(acc)

        pick(refs[:n_in], acc)

        @pl.when(k == nk - 1)
        def _():
            res = acc[...] if scale == 1.0 else acc[...] * scale
            out_ref[...] = res.reshape(out_ref.shape)

    return pl.pallas_call(
        body, name=name, grid=grid, in_specs=in_specs, out_specs=out_spec, out_shape=out_shape,
        scratch_shapes=[pltpu.VMEM(acc_shape, F32)], compiler_params=_params(len(grid)),
    )(*operands)


def _dw_shared_b(name, a3, b, tk, scale):
    nj, s, m = a3.shape
    n = b.shape[1]

    def pick(refs, acc):
        acc[...] += _mm(refs[0][0], refs[1][...], TN)

    return _tn_call(name, (a3, b),
                    [pl.BlockSpec((1, tk, m), lambda j, k: (j, k, 0)), pl.BlockSpec((tk, n), lambda j, k: (k, 0))],
                    jax.ShapeDtypeStruct((nj, m, n), F32), pl.BlockSpec((1, m, n), lambda j, k: (j, 0, 0)),
                    (nj, s // tk), (m, n), pick, scale)


def _dw_colblocks(name, a, b, nj, tk):
    s, m = a.shape
    n = b.shape[1] // nj

    def pick(refs, acc):
        acc[...] += _mm(refs[0][...], refs[1][...], TN)

    return _tn_call(name, (a, b),
                    [pl.BlockSpec((tk, m), lambda j, k: (k, 0)), pl.BlockSpec((tk, n), lambda j, k: (k, j))],
                    jax.ShapeDtypeStruct((nj, m, n), F32), pl.BlockSpec((1, m, n), lambda j, k: (j, 0, 0)),
                    (nj, s // tk), (m, n), pick)


DP_WIDTHS = (768, 512, 512, 512, 512, 512, 2048)
DP_CHUNK = 256


def _dp_chunk_maps():
    starts, counts, off = [], [], 0
    for w in DP_WIDTHS:
        starts.append(off // DP_CHUNK)
        counts.append(w // DP_CHUNK)
        off += w
    return starts, counts


def _dp_specs(tm, row_axis, chunk_axis):
    starts, counts = _dp_chunk_maps()
    specs = []
    for st, cnt in zip(starts, counts):
        def imap(*ids, st=st, cnt=cnt):
            return (ids[row_axis], jnp.clip(ids[chunk_axis] - st, 0, cnt - 1))
        specs.append(pl.BlockSpec((tm, DP_CHUNK), imap))
    return specs


def _dp_select(n, refs, fn):
    starts, counts = _dp_chunk_maps()
    for ref, st, cnt in zip(refs, starts, counts):
        @pl.when((n >= st) & (n < st + cnt))
        def _(ref=ref):
            fn(ref)


def _dw_in(dps, hb, tk):
    s, d = hb.shape
    n_chunks = sum(DP_WIDTHS) // DP_CHUNK

    def pick(refs, acc):
        def add(ref):
            acc[...] += _mm(ref[...], refs[7][...], TN)

        _dp_select(pl.program_id(0), refs[:7], add)

    return _tn_call("dw_in", (*dps, hb),
                    _dp_specs(tk, 1, 0) + [pl.BlockSpec((tk, d), lambda n, k: (k, 0))],
                    jax.ShapeDtypeStruct((n_chunks * DP_CHUNK, d), F32), pl.BlockSpec((DP_CHUNK, d), lambda n, k: (n, 0)),
                    (n_chunks, s // tk), (DP_CHUNK, d), pick)


def _mix_in_fwd(x, g, w_t, tm):
    s, d = x.shape
    n_in = w_t.shape[0]

    def body(x_ref, g_ref, w_ref, qkv_ref, hg_ref, gt_ref, hb_ref):
        xv = x_ref[...]
        r = lax.rsqrt(jnp.mean(xv * xv, axis=-1, keepdims=True) + EPS)
        h = (xv * r * g_ref[...]).astype(CD)
        hb_ref[...] = h
        qkv_ref[...] = _mm(h, w_ref[0:768, :], NT)
        for c in range(5):
            hg_ref[:, 512 * c:512 * c + 512] = _mm(h, w_ref[768 + 512 * c:768 + 512 * c + 512, :], NT)
        for c in range(2):
            gt_ref[:, 1024 * c:1024 * c + 1024] = _mm(h, w_ref[3328 + 1024 * c:3328 + 1024 * c + 1024, :], NT)

    row = lambda w: pl.BlockSpec((tm, w), lambda i: (i, 0))
    return pl.pallas_call(
        body, name="mix_in_fwd", grid=(s // tm,),
        in_specs=[row(d), pl.BlockSpec((1, d), lambda i: (0, 0)), pl.BlockSpec((n_in, d), lambda i: (0, 0))],
        out_specs=[row(768), row(2560), row(2048), row(d)],
        out_shape=[jax.ShapeDtypeStruct((s, 768), F32), jax.ShapeDtypeStruct((s, 2560), F32),
                   jax.ShapeDtypeStruct((s, 2048), F32), jax.ShapeDtypeStruct((s, d), CD)],
        compiler_params=_params(1),
    )(x, g, w_t)


def _mix_in_bwd(dps, w_t, x, dres, g, tm):
    s, d = x.shape
    n_in = w_t.shape[0]

    def body(*refs):
        dp_refs = refs[:7]
        w_ref, x_ref, dr_ref, g_ref, dx_ref, dg_ref = refs[7:]

        @pl.when(pl.program_id(0) == 0)
        def _():
            dg_ref[...] = jnp.zeros_like(dg_ref)

        dhv = jnp.zeros((tm, d), F32)
        off = 0
        for ref, width in zip(dp_refs, DP_WIDTHS):
            dhv = dhv + _mm(ref[...], w_ref[off:off + width, :], NN)
            off += width
        xv = x_ref[...]
        r = lax.rsqrt(jnp.mean(xv * xv, axis=-1, keepdims=True) + EPS)
        u = dhv * g_ref[...]
        dx_ref[...] = dr_ref[...] + r * u - xv * (r * r * r) * jnp.mean(u * xv, axis=-1, keepdims=True)
        dg_ref[...] += jnp.sum(dhv * xv * r, axis=0, keepdims=True)

    row = pl.BlockSpec((tm, d), lambda i: (i, 0))
    vec = pl.BlockSpec((1, d), lambda i: (0, 0))
    return pl.pallas_call(
        body, name="mix_in_bwd", grid=(s // tm,),
        in_specs=[pl.BlockSpec((tm, w), lambda i: (i, 0)) for w in DP_WIDTHS]
        + [pl.BlockSpec((n_in, d), lambda i: (0, 0)), row, row, vec],
        out_specs=[row, vec],
        out_shape=[jax.ShapeDtypeStruct((s, d), F32), jax.ShapeDtypeStruct((1, d), F32)],
        compiler_params=_params(1),
    )(*dps, w_t, x, dres, g)


def _headnorm_rope(x, gain, cos, sin, blocksum, swap):
    ss = _xdot(x * x, blocksum)
    r = lax.rsqrt(ss * (1.0 / HEAD_DIM) + EPS)
    y = x * r * gain
    return y * cos + _xdot(y, swap) * sin, r


def _headnorm_rope_bwd(dz, x, gain, cos, sin, blocksum, swap):
    ss = _xdot(x * x, blocksum)
    r = lax.rsqrt(ss * (1.0 / HEAD_DIM) + EPS)
    dy = dz * cos + _xdot(dz * sin, swap)
    u = dy * gain
    mean_ux = _xdot(u * x, blocksum) * (1.0 / HEAD_DIM)
    dx = r * u - x * (r * r * r) * mean_ux
    return dx, jnp.sum(dy * x * r, axis=0, keepdims=True)


def _qk_prep(pqkv, gq, gk, cos2, sin2, tm):
    s = pqkv.shape[0]
    bs512, sw512, eq, swh = _bf(_np_blocksum(512)), _bf(_np_swap32(512)), _bf(_np_expand_q()), _bf(_np_swap_halves())

    def body(q_ref, kv_ref, gq_ref, gk_ref, c_ref, s_ref, bs_ref, sw_ref, eq_ref, swh_ref, qe_ref, k_ref, v_ref, vs_ref):
        c2, s2 = c_ref[...], s_ref[...]
        c8, s8 = jnp.tile(c2, (1, 4)), jnp.tile(s2, (1, 4))
        bs, sw = bs_ref[...], sw_ref[...]
        zq, _ = _headnorm_rope(q_ref[...], gq_ref[...], c8, s8, bs, sw)
        qe_ref[...] = _mm(zq * (HEAD_DIM ** -0.5), eq_ref[...], NN).astype(CD)
        kv = kv_ref[...]
        zk, _ = _headnorm_rope(kv[:, :LANES], gk_ref[...], c2, s2, bs[:LANES, :LANES], sw[:LANES, :LANES])
        k_ref[...] = zk.astype(CD)
        v = kv[:, LANES:]
        v_ref[...] = v.astype(CD)
        vs_ref[...] = _mm(v, swh_ref[...], NN).astype(CD)

    full = lambda a: pl.BlockSpec(a.shape, lambda i: (0,) * a.ndim)
    tab = pl.BlockSpec((tm, LANES), lambda i: (i, 0))
    return pl.pallas_call(
        body, name="qk_prep", grid=(s // tm,),
        in_specs=[pl.BlockSpec((tm, 512), lambda i: (i, 0)), pl.BlockSpec((tm, 256), lambda i: (i, 2)),
                  full(gq), full(gk), tab, tab, full(bs512), full(sw512), full(eq), full(swh)],
        out_specs=[pl.BlockSpec((tm, 1024), lambda i: (i, 0)), tab, tab, tab],
        out_shape=[jax.ShapeDtypeStruct((s, 1024), CD)] + [jax.ShapeDtypeStruct((s, LANES), CD)] * 3,
        compiler_params=_params(1),
    )(pqkv, pqkv, gq, gk, cos2, sin2, bs512, sw512, eq, swh)


def _qk_prep_bwd(pqkv, dqe, dk, dv, gq, gk, cos2, sin2, tm):
    s = pqkv.shape[0]
    bs512, sw512, eqt = _bf(_np_blocksum(512)), _bf(_np_swap32(512)), _bf(_np_expand_q().T)

    def body(q_ref, kv_ref, dqe_ref, dk_ref, dv_ref, gq_ref, gk_ref, c_ref, s_ref, bs_ref, sw_ref, eqt_ref,
             dp_ref, dgq_ref, dgk_ref):
        @pl.when(pl.program_id(0) == 0)
        def _():
            dgq_ref[...] = jnp.zeros_like(dgq_ref)
            dgk_ref[...] = jnp.zeros_like(dgk_ref)

        c2, s2 = c_ref[...], s_ref[...]
        c8, s8 = jnp.tile(c2, (1, 4)), jnp.tile(s2, (1, 4))
        bs, sw = bs_ref[...], sw_ref[...]
        dzq = _xdot(dqe_ref[...], eqt_ref[...]) * (HEAD_DIM ** -0.5)
        dxq, dgq = _headnorm_rope_bwd(dzq, q_ref[...], gq_ref[...], c8, s8, bs, sw)
        kv = kv_ref[...]
        dxk, dgk = _headnorm_rope_bwd(dk_ref[...], kv[:, :LANES], gk_ref[...], c2, s2, bs[:LANES, :LANES], sw[:LANES, :LANES])
        dp_ref[...] = jnp.concatenate([dxq, dxk, dv_ref[...]], axis=1).astype(CD)
        dgq_ref[...] += dgq
        dgk_ref[...] += dgk

    full = lambda a: pl.BlockSpec(a.shape, lambda i: (0,) * a.ndim)
    tab = pl.BlockSpec((tm, LANES), lambda i: (i, 0))
    return pl.pallas_call(
        body, name="qk_prep_bwd", grid=(s // tm,),
        in_specs=[pl.BlockSpec((tm, 512), lambda i: (i, 0)), pl.BlockSpec((tm, 256), lambda i: (i, 2)),
                  pl.BlockSpec((tm, 1024), lambda i: (i, 0)), tab, tab, full(gq), full(gk), tab, tab,
                  full(bs512), full(sw512), full(eqt)],
        out_specs=[pl.BlockSpec((tm, 768), lambda i: (i, 0)), pl.BlockSpec((1, 512), lambda i: (0, 0)),
                   pl.BlockSpec((1, LANES), lambda i: (0, 0))],
        out_shape=[jax.ShapeDtypeStruct((s, 768), CD), jax.ShapeDtypeStruct((1, 512), F32),
                   jax.ShapeDtypeStruct((1, LANES), F32)],
        compiler_params=_params(1),
    )(pqkv, pqkv, dqe, dk, dv, gq, gk, cos2, sin2, bs512, sw512, eqt)


def _attn_fwd(qe, k, v, vs, tq):
    s = k.shape[0]

    def body(qa_ref, qb_ref, k_ref, v_ref, vs_ref, o_ref, lse_ref):
        m = pl.program_id(0)
        grp = m // 2
        kk = k_ref[...]
        outs = []
        for idx, q_ref in enumerate((qa_ref, qb_ref)):
            sc = _mm(q_ref[...], kk, NT)
            mx = jnp.max(sc, axis=-1, keepdims=True)
            e = jnp.exp(sc - mx)
            l = jnp.sum(e, axis=-1, keepdims=True)
            lse_ref[idx] = mx + jnp.log(l)
            p = e * (1.0 / l)
            vsel = jnp.where(grp != idx, vs_ref[...], v_ref[...])
            outs.append(_mm(p, vsel, NN))
        lane = lax.broadcasted_iota(jnp.int32, (1, LANES), 1)
        o_ref[...] = jnp.where(lane < HEAD_DIM, outs[0], outs[1])

    kv = pl.BlockSpec((s, LANES), lambda m, i: (0, 0))
    return pl.pallas_call(
        body, name="attn_fwd", grid=(4, s // tq),
        in_specs=[pl.BlockSpec((tq, LANES), lambda m, i: (i, 2 * m)), pl.BlockSpec((tq, LANES), lambda m, i: (i, 2 * m + 1)),
                  kv, kv, kv],
        out_specs=[pl.BlockSpec((tq, LANES), lambda m, i: (i, m)), pl.BlockSpec((2, tq, 1), lambda m, i: (m, i, 0))],
        out_shape=[jax.ShapeDtypeStruct((s, 512), F32), jax.ShapeDtypeStruct((8, s, 1), F32)],
        compiler_params=_params(2),
    )(qe, qe, k, v, vs)


def _attn_bwd(qe, k, v, doe, delta, lse, tq):
    s = k.shape[0]

    def body(q_ref, k_ref, v_ref, do_ref, dl_ref, lse_ref, dq_ref, dk_ref, dv_ref):
        @pl.when((pl.program_id(0) == 0) & (pl.program_id(1) == 0))
        def _():
            dk_ref[...] = jnp.zeros_like(dk_ref)
            dv_ref[...] = jnp.zeros_like(dv_ref)

        q, kk, do = q_ref[...], k_ref[...], do_ref[...]
        p = jnp.exp(_mm(q, kk, NT) - lse_ref[0])
        dp = _mm(do, v_ref[...], NT)
        ds = p * (dp - jnp.max(dl_ref[...], axis=-1, keepdims=True))
        dq_ref[...] = _mm(ds, kk, NN)
        dk_ref[...] += _mm(ds, q, TN)
        dv_ref[...] += _mm(p, do, TN)

    kv = pl.BlockSpec((s, LANES), lambda h, i: (0, 0))
    blk = pl.BlockSpec((tq, LANES), lambda h, i: (i, h))
    return pl.pallas_call(
        body, name="attn_bwd", grid=(8, s // tq),
        in_specs=[blk, kv, kv, blk, blk, pl.BlockSpec((1, tq, 1), lambda h, i: (h, i, 0))],
        out_specs=[blk, kv, kv],
        out_shape=[jax.ShapeDtypeStruct((s, 1024), F32), jax.ShapeDtypeStruct((s, LANES), F32),
                   jax.ShapeDtypeStruct((s, LANES), F32)],
        compiler_params=_params(2),
    )(qe, k, v, doe, delta, lse)


@jax.custom_vjp
def _mm_nn(a, b):
    return _mm(a, b, NN)


_mm_nn.defvjp(lambda a, b: (_mm(a, b, NN), (a, b)),
              lambda res, g: (_mm(g, res[1], NT), _mm(res[0], g, TN)))


@jax.custom_vjp
def _mm_nt(a, b):
    return _mm(a, b, NT)


_mm_nt.defvjp(lambda a, b: (_mm(a, b, NT), (a, b)),
              lambda res, g: (_mm(g, res[1], NN), _mm(g, res[0], TN)))


@jax.custom_vjp
def _mm_tn(a, b):
    return _mm(a, b, TN)


_mm_tn.defvjp(lambda a, b: (_mm(a, b, TN), (a, b)),
              lambda res, g: (_mm(res[1], g, NT), _mm(res[0], g, NN)))


@jax.custom_vjp
def _cmm(m, mt, x):
    return _xdot_l(m, x)


_cmm.defvjp(lambda m, mt, x: (_xdot_l(m, x), (m, mt)),
            lambda res, g: (jnp.zeros_like(res[0]), jnp.zeros_like(res[1]), _xdot_l(res[1], g)))


def _hgrn_masks(t, rev):
    n_ch = t // CHUNK
    r = jnp.bitwise_and(lax.broadcasted_iota(jnp.int32, (2 * t, t), 0), t - 1)
    c = lax.broadcasted_iota(jnp.int32, (2 * t, t), 1)
    same = jnp.right_shift(r, 5) == jnp.right_shift(c, 5)
    tri2 = same & ((c >= r) if rev else (c <= r))
    pr = lax.broadcasted_iota(jnp.int32, (LANES, LANES), 0)
    pc = lax.broadcasted_iota(jnp.int32, (LANES, LANES), 1)
    diag = jnp.right_shift(pr, 6) == jnp.right_shift(pc, 6)
    qr = lax.broadcasted_iota(jnp.int32, (t, n_ch * LANES), 0)
    qc = lax.broadcasted_iota(jnp.int32, (t, n_ch * LANES), 1)
    rows_chunk = jnp.right_shift(qc, 7) == jnp.right_shift(qr, 5)
    vr = lax.broadcasted_iota(jnp.int32, (n_ch * LANES, t), 0)
    vc = lax.broadcasted_iota(jnp.int32, (n_ch * LANES, t), 1)
    cols_chunk = jnp.right_shift(vr, 7) == jnp.right_shift(vc, 5)
    return dict(tri2=tri2, diag=diag, rows_chunk=rows_chunk, cols_chunk=cols_chunk)


def _hgrn_gates(xf, lb):
    f = lb + (1.0 - lb) * _sigmoid(xf)
    return 1.0 - f, jnp.log(f)


def _hgrn_dir(xq, xf, v, lb, state, cm, cmt, mk, rev):
    t = xq.shape[0]
    n_ch = t // CHUNK
    lo = lax.broadcasted_iota(jnp.int32, (1, LANES), 1) < HEAD_DIM
    q = xq * _sigmoid(xq)
    k, lf = _hgrn_gates(xf, lb)
    cs = _cmm(cm, cmt, lf)
    b, bm, bl = cs[:t], cs[t:2 * t], cs[2 * t:]
    qd = q * jnp.exp(b - bm)
    kd = k * jnp.exp(bm - b)
    kc = k * jnp.exp(bl - b)
    qe = q * jnp.exp(b)
    qd2 = jnp.concatenate([jnp.where(lo, qd, 0.0), jnp.where(lo, 0.0, qd)], axis=0)
    o2 = _mm_nn(jnp.where(mk["tri2"], _mm_nt(qd2, kd), 0.0), v)
    o = jnp.where(lo, o2[:t], o2[t:])
    vexp = jnp.where(mk["cols_chunk"], jnp.concatenate([jnp.transpose(v)] * n_ch, axis=0), 0.0)
    adds = _mm_nn(vexp, kc)
    dec = jnp.exp(bl)
    entering = [None] * n_ch
    for c in (range(n_ch - 1, -1, -1) if rev else range(n_ch)):
        entering[c] = state
        d = jnp.concatenate([dec[c * CHUNK:(c + 1) * CHUNK]] * (LANES // CHUNK), axis=0)
        state = d * state + jnp.where(mk["diag"], adds[c * LANES:(c + 1) * LANES], 0.0)
    qexp = jnp.where(mk["rows_chunk"], jnp.concatenate([qe] * n_ch, axis=1), 0.0)
    return o + _mm_nt(qexp, jnp.concatenate(entering, axis=1)), state


def _hgrn_lower_bounds(l):
    out = []
    for d in (0, 1):
        l0, l1 = l[2 * d:2 * d + 1, :], l[2 * d + 1:2 * d + 2, :]
        mx = jnp.maximum(l0, l1)
        e0, e1 = jnp.exp(l0 - mx), jnp.exp(l1 - mx)
        out.append(e0 / (e0 + e1))
    return out


def _hgrn_consts(t):
    cf, cb = _np_hgrn_cums(t, False), _np_hgrn_cums(t, True)
    return (_bf(cf), _bf(cf.T), _bf(cb), _bf(cb.T), _bf(_np_blocksum(LANES)))


def _hgrn_fwd(ph, lbl, ng):
    s = ph.shape[0]
    t = min(HG_TILE, s)
    nt = s // t
    consts = _hgrn_consts(t)

    def body(xq_ref, xff_ref, xfb_ref, xi_ref, xg_ref, lbl_ref, ng_ref, cf_ref, cft_ref, cb_ref, cbt_ref, bs_ref,
             o_ref, pre_ref, st_ref, acc):
        lbf, lbb = _hgrn_lower_bounds(lbl_ref)
        mk_f, mk_b = _hgrn_masks(t, False), _hgrn_masks(t, True)
        zero = jnp.zeros((LANES, LANES), F32)

        def rows_of(i):
            return pl.ds(pl.multiple_of(i * t, t), t)

        acc[...] = jnp.zeros_like(acc)

        def step(i, states):
            tb = nt - 1 - i
            rf, rb = rows_of(i), rows_of(tb)
            st_ref[0, 0, i] = states[0]
            st_ref[0, 1, tb] = states[1]
            of, sf = _hgrn_dir(xq_ref[rf, :], xff_ref[rf, :], xi_ref[rf, :], lbf, states[0],
                               cf_ref[...], cft_ref[...], mk_f, False)
            ob, sb = _hgrn_dir(xq_ref[rb, :], xfb_ref[rb, :], xi_ref[rb, :], lbb, states[1],
                               cb_ref[...], cbt_ref[...], mk_b, True)
            acc[rf, :] += of
            acc[rb, :] += ob
            return sf, sb

        lax.fori_loop(0, nt, step, (zero, zero))

        def step_n(i, carry):
            rows = rows_of(i)
            o = acc[rows, :]
            ss = _xdot(o * o, bs_ref[...])
            r = lax.rsqrt(ss * (1.0 / HEAD_DIM) + EPS)
            xg = xg_ref[rows, :]
            pre_ref[rows, :] = o
            o_ref[rows, :] = (o * r * ng_ref[...]) * (xg * _sigmoid(xg))
            return carry

        lax.fori_loop(0, nt, step_n, 0)

    col = lambda off: pl.BlockSpec((s, LANES), lambda m: (0, off + m))
    full = lambda a: pl.BlockSpec(a.shape, lambda m: (0,) * a.ndim)
    return pl.pallas_call(
        body, name="hgrn_fwd", grid=(4,),
        in_specs=[col(0), col(4), col(8), col(12), col(16), pl.BlockSpec((4, LANES), lambda m: (0, m)),
                  pl.BlockSpec((1, LANES), lambda m: (0, m))] + [full(c) for c in consts],
        out_specs=[col(0), col(0), pl.BlockSpec((1, 2, nt, LANES, LANES), lambda m: (m, 0, 0, 0, 0))],
        out_shape=[jax.ShapeDtypeStruct((s, 512), F32), jax.ShapeDtypeStruct((s, 512), F32),
                   jax.ShapeDtypeStruct((4, 2, nt, LANES, LANES), F32)],
        scratch_shapes=[pltpu.VMEM((s, LANES), F32)],
        compiler_params=_params(1),
    )(ph, ph, ph, ph, ph, lbl, ng, *consts)


def _hgrn_bwd(ph, pre, dout, states, lbl, ng):
    s = ph.shape[0]
    t = min(HG_TILE, s)
    nt = s // t
    consts = _hgrn_consts(t)

    def body(xq_ref, xff_ref, xfb_ref, xi_ref, xg_ref, pre_ref, do_ref, st_ref, lbl_ref, ng_ref,
             cf_ref, cft_ref, cb_ref, cbt_ref, bs_ref,
             dq_ref, dff_ref, dfb_ref, di_ref, dg_ref, dlb_ref, dng_ref, dpre, dq_acc, dv_acc):
        lbf, lbb = _hgrn_lower_bounds(lbl_ref)
        mk_f, mk_b = _hgrn_masks(t, False), _hgrn_masks(t, True)
        zero = jnp.zeros((LANES, LANES), F32)
        zrow = jnp.zeros((1, LANES), F32)

        def rows_of(i):
            return pl.ds(pl.multiple_of(i * t, t), t)

        def step_n(i, dng):
            rows = rows_of(i)
            o, xg, do = pre_ref[rows, :], xg_ref[rows, :], do_ref[rows, :]
            bs = bs_ref[...]
            r = lax.rsqrt(_xdot(o * o, bs) * (1.0 / HEAD_DIM) + EPS)
            sg = _sigmoid(xg)
            gate = xg * sg
            don = do * gate
            dg_ref[rows, :] = (do * (o * r * ng_ref[...]) * (sg * (1.0 + xg * (1.0 - sg)))).astype(CD)
            u = don * ng_ref[...]
            dpre[rows, :] = r * u - o * (r * r * r) * (_xdot(u * o, bs) * (1.0 / HEAD_DIM))
            return dng + jnp.sum(don * o * r, axis=0, keepdims=True)

        dng_ref[...] = lax.fori_loop(0, nt, step_n, zrow)

        dq_acc[...] = jnp.zeros_like(dq_acc)
        dv_acc[...] = jnp.zeros_like(dv_acc)

        def grad_tile(ti, xf_ref, df_ref, lb, cm, cmt, mk, rev, st, dstate):
            rows = rows_of(ti)
            fn = lambda xq, xf, v, lbv, s_in: _hgrn_dir(xq, xf, v, lbv, s_in, cm, cmt, mk, rev)
            _, vjp = jax.vjp(fn, xq_ref[rows, :], xf_ref[rows, :], xi_ref[rows, :], lb, st)
            dxq, dxf, dv, dlb_t, dstate = vjp((dpre[rows, :], dstate))
            df_ref[rows, :] = dxf.astype(CD)
            dq_acc[rows, :] += dxq
            dv_acc[rows, :] += dv
            return dstate, dlb_t

        def step_g(i, carry):
            dsf, dsb, dlbf, dlbb = carry
            tf, tb = nt - 1 - i, i
            dsf, gf = grad_tile(tf, xff_ref, dff_ref, lbf, cf_ref[...], cft_ref[...], mk_f, False, st_ref[0, 0, tf], dsf)
            dsb, gb = grad_tile(tb, xfb_ref, dfb_ref, lbb, cb_ref[...], cbt_ref[...], mk_b, True, st_ref[0, 1, tb], dsb)
            return dsf, dsb, dlbf + gf, dlbb + gb

        _, _, dlbf, dlbb = lax.fori_loop(0, nt, step_g, (zero, zero, zrow, zrow))
        dlb_ref[0:1, :] = dlbf
        dlb_ref[1:2, :] = dlbb
        dq_ref[...] = dq_acc[...].astype(CD)
        di_ref[...] = dv_acc[...].astype(CD)

    col = lambda off: pl.BlockSpec((s, LANES), lambda m: (0, off + m))
    full = lambda a: pl.BlockSpec(a.shape, lambda m: (0,) * a.ndim)
    stream = jax.ShapeDtypeStruct((s, 512), CD)
    return pl.pallas_call(
        body, name="hgrn_bwd", grid=(4,),
        in_specs=[col(0), col(4), col(8), col(12), col(16), col(0), col(0),
                  pl.BlockSpec((1, 2, nt, LANES, LANES), lambda m: (m, 0, 0, 0, 0)),
                  pl.BlockSpec((4, LANES), lambda m: (0, m)),
                  pl.BlockSpec((1, LANES), lambda m: (0, m))] + [full(c) for c in consts],
        out_specs=[col(0)] * 5 + [pl.BlockSpec((2, LANES), lambda m: (0, m)), pl.BlockSpec((1, LANES), lambda m: (0, m))],
        out_shape=[stream] * 5 + [jax.ShapeDtypeStruct((2, 512), F32), jax.ShapeDtypeStruct((1, 512), F32)],
        scratch_shapes=[pltpu.VMEM((s, LANES), F32), pltpu.VMEM((s, LANES), F32), pltpu.VMEM((s, LANES), F32)],
        compiler_params=_params(1),
    )(ph, ph, ph, ph, ph, pre, dout, states, lbl, ng, *consts)


def _branch_out(o, w4):
    return jnp.concatenate([_mm(o, w4[j], NN) for j in range(N_SHARD)], axis=1)


def _mix_out_fwd(x, oa, ob, pg, wa, wb, wo, tm):
    s, d = x.shape

    def body(x_ref, oa_ref, ob_ref, ga_ref, gb_ref, wa_ref, wb_ref, wo_ref, xo_ref):
        ya = _branch_out(oa_ref[...], wa_ref)
        yb = _branch_out(ob_ref[...], wb_ref)
        merged = _sigmoid(ga_ref[...]) * ya + _sigmoid(gb_ref[...]) * yb
        xo_ref[...] = x_ref[...] + _mm(merged, wo_ref[...], NN)

    row = pl.BlockSpec((tm, d), lambda i: (i, 0))
    half = pl.BlockSpec((tm, 512), lambda i: (i, 0))
    full = lambda a: pl.BlockSpec(a.shape, lambda i: (0,) * a.ndim)
    return pl.pallas_call(
        body, name="mix_out_fwd", grid=(s // tm,),
        in_specs=[row, half, half, row, pl.BlockSpec((tm, d), lambda i: (i, 1)), full(wa), full(wb), full(wo)],
        out_specs=row, out_shape=jax.ShapeDtypeStruct((s, d), F32),
        compiler_params=_params(1),
    )(x, oa, ob, pg, pg, wa, wb, wo)


def _mix_out_bwd(dx, oa, ob, pg, wa, wb, wo, tm):
    s, d = dx.shape
    bs512, eq, ebc = _bf(_np_blocksum(512)), _bf(_np_expand_q()), _bf(_np_bcast_head())

    def body(dx_ref, oa_ref, ob_ref, ga_ref, gb_ref, wa_ref, wb_ref, wo_ref, bs_ref, eq_ref, ebc_ref,
             dpg_ref, mg_ref, dya_ref, dyb_ref, doe_ref, dl_ref, dob_ref):
        oa = oa_ref[...]
        ya = _branch_out(oa, wa_ref)
        yb = _branch_out(ob_ref[...], wb_ref)
        sa, sb = _sigmoid(ga_ref[...]), _sigmoid(gb_ref[...])
        mg_ref[...] = (sa * ya + sb * yb).astype(CD)
        dm = _mm(dx_ref[...], wo_ref[...], NT)
        dpg_ref[...] = jnp.concatenate([dm * ya * sa * (1.0 - sa), dm * yb * sb * (1.0 - sb)], axis=1).astype(CD)
        dya, dyb = dm * sa, dm * sb
        dya_ref[...] = dya.astype(CD)
        dyb_ref[...] = dyb.astype(CD)
        doa = jnp.zeros(oa.shape, F32)
        dob = jnp.zeros(oa.shape, F32)
        for j in range(N_SHARD):
            doa = doa + _mm(dya[:, 256 * j:256 * j + 256], wa_ref[j], NT)
            dob = dob + _mm(dyb[:, 256 * j:256 * j + 256], wb_ref[j], NT)
        dob_ref[...] = dob
        doe_ref[...] = _mm(doa, eq_ref[...], NN).astype(CD)
        dl_ref[...] = _xdot(_xdot(doa * oa, bs_ref[...]), ebc_ref[...])

    row = pl.BlockSpec((tm, d), lambda i: (i, 0))
    half = pl.BlockSpec((tm, 512), lambda i: (i, 0))
    full = lambda a: pl.BlockSpec(a.shape, lambda i: (0,) * a.ndim)
    wide = jax.ShapeDtypeStruct((s, d), CD)
    return pl.pallas_call(
        body, name="mix_out_bwd", grid=(s // tm,),
        in_specs=[row, half, half, row, pl.BlockSpec((tm, d), lambda i: (i, 1)), full(wa), full(wb), full(wo),
                  full(bs512), full(eq), full(ebc)],
        out_specs=[pl.BlockSpec((tm, 2048), lambda i: (i, 0)), row, row, row, row, row, half],
        out_shape=[jax.ShapeDtypeStruct((s, 2048), CD), wide, wide, wide, wide, jax.ShapeDtypeStruct((s, d), F32),
                   jax.ShapeDtypeStruct((s, 512), F32)],
        compiler_params=_params(1),
    )(dx, oa, ob, pg, pg, wa, wb, wo, bs512, eq, ebc)


def _loss_head(x, g, target, tm):
    s, d = x.shape

    def body(x_ref, g_ref, t_ref, dx_ref, loss_ref, dg_ref):
        @pl.when(pl.program_id(0) == 0)
        def _():
            loss_ref[...] = jnp.zeros_like(loss_ref)
            dg_ref[...] = jnp.zeros_like(dg_ref)

        xv = x_ref[...]
        r = lax.rsqrt(jnp.mean(xv * xv, axis=-1, keepdims=True) + EPS)
        err = xv * r * g_ref[...] - t_ref[...]
        loss_ref[...] += 0.5 * jnp.sum(jnp.mean(err * err, axis=-1, keepdims=True))
        dy = err * (1.0 / d)
        u = dy * g_ref[...]
        dx_ref[...] = r * u - xv * (r * r * r) * jnp.mean(u * xv, axis=-1, keepdims=True)
        dg_ref[...] += jnp.sum(dy * xv * r, axis=0, keepdims=True)

    row = pl.BlockSpec((tm, d), lambda i: (i, 0))
    vec = pl.BlockSpec((1, d), lambda i: (0, 0))
    return pl.pallas_call(
        body, name="loss_head", grid=(s // tm,),
        in_specs=[row, vec, row], out_specs=[row, pl.BlockSpec((8, LANES), lambda i: (0, 0)), vec],
        out_shape=[jax.ShapeDtypeStruct((s, d), F32), jax.ShapeDtypeStruct((8, LANES), F32),
                   jax.ShapeDtypeStruct((1, d), F32)],
        compiler_params=_params(1),
    )(x, g, target)


def _position():
    x, y, c = lax.axis_index("x"), lax.axis_index("y"), lax.axis_index("c")
    return x, y, c, [(1 - x, y), (x, 1 - y), (1 - x, 1 - y)]


def _row_tile(rows, cap=256):
    best = rows
    for cand in range(8, min(rows, cap) + 1, 8):
        if rows % cand == 0:
            best = cand
    return best


def _cast_into_slot(shard, me_idx, dtype):
    rows, cols = shard.shape
    tr = _row_tile(rows)

    def body(me_ref, src_ref, out_ref):
        out_ref[0] = src_ref[...].astype(dtype)

    return pl.pallas_call(
        body, name="cast_into_slot",
        grid_spec=pltpu.PrefetchScalarGridSpec(
            num_scalar_prefetch=1, grid=(rows // tr,),
            in_specs=[pl.BlockSpec((tr, cols), lambda i, me: (i, 0))],
            out_specs=pl.BlockSpec((1, tr, cols), lambda i, me: (me[0], i, 0))),
        out_shape=jax.ShapeDtypeStruct((N_SHARD, rows, cols), dtype),
        compiler_params=_params(1),
    )(me_idx, shard)


HBM_SPEC = pl.BlockSpec(memory_space=pltpu.HBM)
SEM_SPEC = pl.BlockSpec(memory_space=pltpu.SEMAPHORE)
DATAFLOW = pltpu.SideEffectType.DATAFLOW_SIDE_EFFECTING


def _exchange_copies(srcs, lands, send, recv, gather):
    x, y, c, chips = _position()
    me = 2 * x + y
    out = []
    for a in range(len(lands)):
        for k, (px, py) in enumerate(chips):
            src = lands[a].at[me] if gather else srcs[a].at[2 * px + py]
            out.append(pltpu.make_async_remote_copy(src_ref=src, dst_ref=lands[a].at[me], send_sem=send.at[3 * a + k],
                                                    recv_sem=recv.at[3 * a + k], device_id=(px, py, c), device_id_type=MESH))
    return out


def _exchange_start(name, srcs, lands, after):
    ns, nl, na = len(srcs), len(lands), len(after)
    gather = ns == 0

    def body(*refs):
        src_refs, land_refs = refs[:ns], refs[ns:ns + nl]
        send, recv = refs[ns + nl + na], refs[ns + nl + na + 1]
        token = refs[-1]
        for cp in _exchange_copies(src_refs, land_refs, send, recv, gather):
            cp.start()
        token[...] = jnp.zeros_like(token)

    arrays = [pltpu.with_memory_space_constraint(a, pltpu.HBM) for a in list(srcs) + list(lands)]
    outs = pl.pallas_call(
        body, name=name,
        out_shape=(pltpu.SemaphoreType.DMA((3 * nl,)), pltpu.SemaphoreType.DMA((3 * nl,)),
                   *[pltpu.HBM(a.shape, a.dtype) for a in arrays], jax.ShapeDtypeStruct((8, LANES), F32)),
        in_specs=[HBM_SPEC] * (ns + nl) + [ANY] * na,
        out_specs=(SEM_SPEC, SEM_SPEC, *[HBM_SPEC] * (ns + nl), pl.BlockSpec(memory_space=pltpu.VMEM)),
        input_output_aliases={i: 2 + i for i in range(ns + nl)},
        compiler_params=pltpu.CompilerParams(has_side_effects=DATAFLOW),
    )(*arrays, *after)
    return outs[0], outs[1], list(outs[2:2 + ns]), list(outs[2 + ns:2 + ns + nl]), outs[-1]


def _exchange_wait(name, send, recv, srcs, lands, after):
    ns, nl, na = len(srcs), len(lands), len(after)
    gather = ns == 0

    def body(*refs):
        src_refs, land_refs = refs[:ns], refs[ns:ns + nl]
        send_ref, recv_ref = refs[ns + nl], refs[ns + nl + 1]
        for cp in _exchange_copies(src_refs, land_refs, send_ref, recv_ref, gather):
            cp.wait_send()
            cp.wait_recv()

    outs = pl.pallas_call(
        body, name=name,
        out_shape=tuple(pltpu.HBM(a.shape, a.dtype) for a in list(srcs) + list(lands)),
        in_specs=[HBM_SPEC] * (ns + nl) + [SEM_SPEC, SEM_SPEC] + [ANY] * na,
        out_specs=tuple([HBM_SPEC] * (ns + nl)),
        input_output_aliases={i: i for i in range(ns + nl)},
        compiler_params=pltpu.CompilerParams(has_side_effects=DATAFLOW),
    )(*srcs, *lands, send, recv, *after)
    return list(outs[ns:])


def _pair_exchange(grads):
    n = len(grads)

    def body(*refs):
        src, dst = refs[:n], refs[n:2 * n]
        send, recv = refs[2 * n:]
        x, y, c, _ = _position()
        copies = []
        for a in range(n):
            half = src[a].shape[1] // 2
            cp = pltpu.make_async_remote_copy(
                src_ref=src[a].at[:, pl.ds((1 - c) * half, half), :], dst_ref=dst[a], send_sem=send.at[a],
                recv_sem=recv.at[a], device_id=(x, y, 1 - c), device_id_type=MESH)
            cp.start()
            copies.append(cp)
        for cp in copies:
            cp.wait()

    return pl.pallas_call(
        body, name="grad_pair_exchange", in_specs=[ANY] * n, out_specs=[ANY] * n,
        out_shape=[jax.ShapeDtypeStruct((g.shape[0], g.shape[1] // 2, g.shape[2]), g.dtype) for g in grads],
        scratch_shapes=[pltpu.SemaphoreType.DMA((n,)), pltpu.SemaphoreType.DMA((n,))],
    )(*grads)


def _pair_sum(g, got, c_idx, me_idx):
    nsh, rows, cols = g.shape
    half = rows // 2

    def body(c_ref, me_ref, g_ref, got_ref, s_ref, own_ref):
        sm = g_ref[...] + got_ref[...]
        s_ref[...] = sm.astype(CD)

        @pl.when(pl.program_id(0) == me_ref[0])
        def _():
            own_ref[...] = sm[0]

    return pl.pallas_call(
        body, name="grad_pair_sum",
        grid_spec=pltpu.PrefetchScalarGridSpec(
            num_scalar_prefetch=2, grid=(nsh,),
            in_specs=[pl.BlockSpec((1, half, cols), lambda j, c, me: (j, c[0], 0)),
                      pl.BlockSpec((1, half, cols), lambda j, c, me: (j, 0, 0))],
            out_specs=[pl.BlockSpec((1, half, cols), lambda j, c, me: (j, 0, 0)),
                       pl.BlockSpec((half, cols), lambda j, c, me: (0, 0))]),
        out_shape=[jax.ShapeDtypeStruct((nsh, half, cols), CD), jax.ShapeDtypeStruct((half, cols), F32)],
        compiler_params=_params(1),
    )(c_idx, me_idx, g, got)


def _chip_sum(own, got, me_idx):
    nsh, half, cols = got.shape

    def body(me_ref, own_ref, got_ref, out_ref):
        j = pl.program_id(0)
        term = jnp.where(j == me_ref[0], own_ref[...], got_ref[0].astype(F32))

        @pl.when(j == 0)
        def _():
            out_ref[...] = term

        @pl.when(j > 0)
        def _():
            out_ref[...] += term

    return pl.pallas_call(
        body, name="grad_chip_sum",
        grid_spec=pltpu.PrefetchScalarGridSpec(
            num_scalar_prefetch=1, grid=(nsh,),
            in_specs=[pl.BlockSpec((half, cols), lambda j, me: (0, 0)), pl.BlockSpec((1, half, cols), lambda j, me: (j, 0, 0))],
            out_specs=pl.BlockSpec((half, cols), lambda j, me: (0, 0))),
        out_shape=jax.ShapeDtypeStruct((half, cols), F32),
        compiler_params=_params(1),
    )(me_idx, own, got)


def _pair_share(halves):
    n = len(halves)

    def body(*refs):
        src, dst = refs[:n], refs[n:2 * n]
        send, recv = refs[2 * n:]
        x, y, c, _ = _position()
        copies = []
        for a in range(n):
            cp = pltpu.make_async_remote_copy(src_ref=src[a], dst_ref=dst[a], send_sem=send.at[a],
                                              recv_sem=recv.at[a], device_id=(x, y, 1 - c), device_id_type=MESH)
            cp.start()
            copies.append(cp)
        for cp in copies:
            cp.wait()

    return pl.pallas_call(
        body, name="grad_pair_share", in_specs=[ANY] * n, out_specs=[ANY] * n,
        out_shape=[jax.ShapeDtypeStruct(h.shape, h.dtype) for h in halves],
        scratch_shapes=[pltpu.SemaphoreType.DMA((n,)), pltpu.SemaphoreType.DMA((n,))],
    )(*halves)


def _small_allreduce(buf):
    rows, cols = buf.shape

    def body(src_ref, out_ref, slots, send, recv):
        x, y, c, _ = _position()
        me = 4 * x + 2 * y + c
        slots[me] = src_ref[...]
        copies = []
        k = 0
        for dx in (0, 1):
            for dy in (0, 1):
                for dc in (0, 1):
                    if (dx, dy, dc) == (0, 0, 0):
                        continue
                    peer = (jnp.where(dx, 1 - x, x), jnp.where(dy, 1 - y, y), jnp.where(dc, 1 - c, c))
                    cp = pltpu.make_async_remote_copy(src_ref=src_ref, dst_ref=slots.at[me], send_sem=send.at[k],
                                                      recv_sem=recv.at[k], device_id=peer, device_id_type=MESH)
                    cp.start()
                    copies.append(cp)
                    k += 1
        for cp in copies:
            cp.wait()
        total = slots[0]
        for dev in range(1, N_DEV):
            total = total + slots[dev]
        out_ref[...] = total

    vm = pl.BlockSpec(memory_space=pltpu.VMEM)
    return pl.pallas_call(
        body, name="small_allreduce", in_specs=[vm], out_specs=vm,
        out_shape=jax.ShapeDtypeStruct((rows, cols), F32),
        scratch_shapes=[pltpu.VMEM((N_DEV, rows, cols), F32), pltpu.SemaphoreType.DMA((N_DEV - 1,)),
                        pltpu.SemaphoreType.DMA((N_DEV - 1,))],
    )(buf)


def _adamw_math(w, gv, m, v):
    mn = ADAM_B1 * m + (1.0 - ADAM_B1) * gv
    vn = ADAM_B2 * v + (1.0 - ADAM_B2) * (gv * gv)
    m_hat = mn / (1.0 - ADAM_B1 ** ADAM_STEP)
    v_hat = vn / (1.0 - ADAM_B2 ** ADAM_STEP)
    return -ADAM_LR * (m_hat / (jnp.sqrt(v_hat) + ADAM_EPS) + ADAM_WD * w), mn, vn


def _adamw(w, g, m, v):
    rows, cols = w.shape
    tr = _row_tile(rows)

    def body(w_ref, g_ref, m_ref, v_ref, d_ref, mo_ref, vo_ref):
        d_ref[...], mo_ref[...], vo_ref[...] = _adamw_math(w_ref[...], g_ref[...], m_ref[...], v_ref[...])

    blk = pl.BlockSpec((tr, cols), lambda i: (i, 0))
    shp = jax.ShapeDtypeStruct((rows, cols), F32)
    return pl.pallas_call(
        body, name="adamw", grid=(rows // tr,), in_specs=[blk] * 4, out_specs=[blk] * 3, out_shape=[shp] * 3,
        compiler_params=_params(1),
    )(w, g, m, v)


def _adamw_halves(w, own, got, m, v, c_idx):
    rows, cols = w.shape
    tr = _row_tile(rows // 2)
    per_half = rows // 2 // tr

    def body(c_ref, w_ref, own_ref, got_ref, m_ref, v_ref, d_ref, mo_ref, vo_ref, g_ref):
        mine = (pl.program_id(0) // per_half) == c_ref[0]
        gv = jnp.where(mine, own_ref[...], got_ref[...])
        g_ref[...] = gv
        d_ref[...], mo_ref[...], vo_ref[...] = _adamw_math(w_ref[...], gv, m_ref[...], v_ref[...])

    blk = pl.BlockSpec((tr, cols), lambda i, c: (i, 0))
    hblk = pl.BlockSpec((tr, cols), lambda i, c: (i % per_half, 0))
    shp = jax.ShapeDtypeStruct((rows, cols), F32)
    return pl.pallas_call(
        body, name="adamw_halves",
        grid_spec=pltpu.PrefetchScalarGridSpec(num_scalar_prefetch=1, grid=(rows // tr,),
                                               in_specs=[blk, hblk, hblk, blk, blk], out_specs=[blk] * 4),
        out_shape=[shp] * 4, compiler_params=_params(1),
    )(c_idx, w, own, got, m, v)


def _local_step(x, target, norm_gains, q_g, k_g, ng, weights_of, grads_done):
    s = x.shape[0]
    tm = min(512, s)
    tq = min(256, s)
    g1, gm, g2, gf = norm_gains
    cos2, sin2 = _rope_tables(s)
    gq8 = jnp.tile(q_g, (1, 8))
    gk2 = jnp.tile(k_g, (1, 2))

    tn = min(256, s)
    tk = min(1024, s)
    w1 = weights_of(1, ())
    x1, a1, b1, h1 = _ffn_fwd(x, g1, w1["g1"], w1["u1"], w1["d1"], tm)
    w2 = weights_of(2, (x1,))
    lbl = w2["lbl"]
    pqkv, ph, pg, hm = _mix_in_fwd(x1, gm, w2["in"], tn)
    qe, kr, vr, vs = _qk_prep(pqkv, gq8, gk2, cos2, sin2, tm)
    oa, lse = _attn_fwd(qe, kr, vr, vs, tq)
    ob, pre, hstates = _hgrn_fwd(ph, lbl, ng)
    x2 = _mix_out_fwd(x1, oa, ob, pg, w2["a"], w2["b"], w2["o"], tm)
    w3 = weights_of(3, (x2,))
    x3, a2, b2, h2 = _ffn_fwd(x2, g2, w3["g2"], w3["u2"], w3["d2"], tm)
    dx3, loss, dgf = _loss_head(x3, gf, target, tm)

    dx2, da2, db2, f2, dg2 = _ffn_bwd(dx3, x2, g2, a2, b2, w3["g2"], w3["u2"], w3["d2"], tm)
    grads_done(3, dict(g2=_dw_shared_b("dw_gate", da2, h2, tk, 1.0), u2=_dw_shared_b("dw_gate", db2, h2, tk, 1.0),
                       d2=_dw_shared_b("dw_down", f2, dx3, tk, 0.5)))

    dpg, mg, dya, dyb, doe, delta, dob = _mix_out_bwd(dx2, oa, ob, pg, w2["a"], w2["b"], w2["o"], tm)
    g_o = _dw_colblocks("dw_out", mg, dx2, 1, tk).reshape(N_SHARD, D_MODEL // N_SHARD, D_MODEL)
    g_a = _dw_colblocks("dw_branch", oa, dya, N_SHARD, tk)
    g_b = _dw_colblocks("dw_branch", ob, dyb, N_SHARD, tk)
    dqe, dk, dv = _attn_bwd(qe, kr, vr, doe, delta, lse, tq)
    dqkv, dgq, dgk = _qk_prep_bwd(pqkv, dqe, dk, dv, gq8, gk2, cos2, sin2, tm)
    dhq, dhff, dhfb, dhi, dhg, dlb, dng = _hgrn_bwd(ph, pre, dob, hstates, lbl, ng)
    dps = (dqkv, dhq, dhff, dhfb, dhi, dhg, dpg)
    g_in = _dw_in(dps, hm, min(2048, s)).reshape(N_SHARD, -1, D_MODEL)
    grads_done(2, {"in": g_in, "a": g_a, "b": g_b, "o": g_o})
    dx1, dgm = _mix_in_bwd(dps, w2["in"], x1, dx2, gm, tn)

    dx0, da1, db1, f1, dg1 = _ffn_bwd(dx1, x, g1, a1, b1, w1["g1"], w1["u1"], w1["d1"], tm)
    grads_done(1, dict(g1=_dw_shared_b("dw_gate", da1, h1, tk, 1.0), u1=_dw_shared_b("dw_gate", db1, h1, tk, 1.0),
                       d1=_dw_shared_b("dw_down", f1, dx1, tk, 0.5)))
    small = dict(g1=dg1, gm=dgm, g2=dg2, gf=dgf, gq=dgq, gk=dgk, lb=dlb, ng=dng)
    return loss, dx0, small, lbl


GROUPS = {1: ("g1", "u1", "d1"), 2: ("in", "a", "b", "o"), 3: ("g2", "u2", "d2")}
BIG = GROUPS[1] + GROUPS[2] + GROUPS[3]
TRANSPOSED = ("g1", "u1", "in", "g2", "u2")


def _pack_rows(vectors, width):
    rows = []
    for vct in vectors:
        flat = vct.reshape(-1)
        pad = (-flat.shape[0]) % width
        rows.append(jnp.pad(flat, (0, pad)).reshape(-1, width))
    return jnp.concatenate(rows, axis=0)


def kernel(x, ffn1_norm_g, ffn1_w_gate, ffn1_w_up, ffn1_w_down, mix_norm_g, w_in, q_norm_g, k_norm_g, hgrn_lb_logits, hgrn_out_norm_g, w_branch_attn, w_branch_hgrn, w_out, ffn2_norm_g, ffn2_w_gate, ffn2_w_up, ffn2_w_down, final_norm_g, loss_target, m_ffn1_norm_g, m_ffn1_w_gate, m_ffn1_w_up, m_ffn1_w_down, m_mix_norm_g, m_w_in, m_q_norm_g, m_k_norm_g, m_hgrn_lb_logits, m_hgrn_out_norm_g, m_w_branch_attn, m_w_branch_hgrn, m_w_out, m_ffn2_norm_g, m_ffn2_w_gate, m_ffn2_w_up, m_ffn2_w_down, m_final_norm_g, v_ffn1_norm_g, v_ffn1_w_gate, v_ffn1_w_up, v_ffn1_w_down, v_mix_norm_g, v_w_in, v_q_norm_g, v_k_norm_g, v_hgrn_lb_logits, v_hgrn_out_norm_g, v_w_branch_attn, v_w_branch_hgrn, v_w_out, v_ffn2_norm_g, v_ffn2_w_gate, v_ffn2_w_up, v_ffn2_w_down, v_final_norm_g):
    xi, yi, ci = lax.axis_index("x"), lax.axis_index("y"), lax.axis_index("c")
    me = 2 * xi + yi
    c_idx = jnp.reshape(ci, (1,)).astype(jnp.int32)
    me_idx = jnp.reshape(me, (1,)).astype(jnp.int32)

    big_w = dict(g1=ffn1_w_gate[0], u1=ffn1_w_up[0], d1=ffn1_w_down[0], a=w_branch_attn[0], b=w_branch_hgrn[0],
                 o=w_out[0], g2=ffn2_w_gate[0], u2=ffn2_w_up[0], d2=ffn2_w_down[0])
    big_w["in"] = w_in[0]
    big_m = dict(g1=m_ffn1_w_gate[0], u1=m_ffn1_w_up[0], d1=m_ffn1_w_down[0], a=m_w_branch_attn[0], b=m_w_branch_hgrn[0],
                 o=m_w_out[0], g2=m_ffn2_w_gate[0], u2=m_ffn2_w_up[0], d2=m_ffn2_w_down[0])
    big_m["in"] = m_w_in[0]
    big_v = dict(g1=v_ffn1_w_gate[0], u1=v_ffn1_w_up[0], d1=v_ffn1_w_down[0], a=v_w_branch_attn[0], b=v_w_branch_hgrn[0],
                 o=v_w_out[0], g2=v_ffn2_w_gate[0], u2=v_ffn2_w_up[0], d2=v_ffn2_w_down[0])
    big_v["in"] = v_w_in[0]
    for table in (big_w, big_m, big_v):
        for n in TRANSPOSED:
            table[n] = table[n].T

    slots = {n: _cast_into_slot(big_w[n], me_idx, CD) for n in BIG}
    lbl_slot = _cast_into_slot(hgrn_lb_logits.reshape(4, LANES), me_idx, F32)
    started, token = {}, ()
    for grp in (1, 2, 3):
        lands = [slots[n] for n in GROUPS[grp]] + ([lbl_slot] if grp == 2 else [])
        send, recv, _, lands, tok = _exchange_start("gather%d_start" % grp, [], lands, token)
        started[grp], token = (send, recv, lands), (tok,)

    def weights_of(grp, after):
        send, recv, lands = started[grp]
        got = _exchange_wait("gather%d_wait" % grp, send, recv, [], lands, tuple(after) + (token if grp == 1 else ()))
        w = dict(zip(GROUPS[grp], got))
        if grp == 2:
            w["in"] = w["in"].reshape(-1, D_MODEL)
            w["o"] = w["o"].reshape(D_MODEL, D_MODEL)
            w["lbl"] = jnp.transpose(got[-1], (1, 0, 2)).reshape(4, N_SHARD * LANES)
        return w

    pending = {}

    def grads_done(grp, grads):
        names = list(grads)
        glist = [grads[n] for n in names]
        got = _pair_exchange(glist)
        sums, owns = zip(*[_pair_sum(g, r, c_idx, me_idx) for g, r in zip(glist, got)])
        lands = [lax.empty(s_.shape, s_.dtype) for s_ in sums]
        send, recv, srcs, lands, tok = _exchange_start("reduce%d_start" % grp, list(sums), lands, ())
        pending[grp] = (names, send, recv, srcs, lands, owns, tok)

    def reduced_halves(grp, after):
        names, send, recv, srcs, lands, owns, _ = pending[grp]
        parts = _exchange_wait("reduce%d_wait" % grp, send, recv, srcs, lands, after)
        return names, [_chip_sum(o, p, me_idx) for o, p in zip(owns, parts)]

    loss, dx, small, lbl = _local_step(
        x[0], loss_target[0], (ffn1_norm_g, mix_norm_g, ffn2_norm_g, final_norm_g.reshape(1, -1)),
        q_norm_g, k_norm_g, hgrn_out_norm_g, weights_of, grads_done)

    dgq = small["gq"].reshape(8, HEAD_DIM).sum(axis=0)
    dgk = small["gk"].reshape(2, HEAD_DIM).sum(axis=0)
    lb_full = _hgrn_lower_bounds(lbl)
    dlog = []
    for d in (0, 1):
        t = small["lb"][d:d + 1] * lb_full[d] * (1.0 - lb_full[d])
        dlog += [t, -t]
    small_list = [small["g1"], small["gm"], small["g2"], small["gf"], small["ng"], dgq, dgk, jnp.concatenate(dlog, axis=0), loss[0, 0]]
    packed = _pack_rows(small_list, D_MODEL)
    n_rows = packed.shape[0]
    packed = jnp.pad(packed, ((0, (-n_rows) % 8), (0, 0)))
    red = _small_allreduce(packed)
    loss_out = red[n_rows - 1, 0]
    sg = dict(g1=red[0:1], gm=red[1:2], g2=red[2:3], gf=red[3], ng=red[4:5, :512], gq=red[5:6, :HEAD_DIM],
              gk=red[6:7, :HEAD_DIM])
    dlog_full = red[7:9].reshape(2, 2, 512)
    sg["lb"] = lax.dynamic_slice_in_dim(dlog_full, me * LANES, LANES, axis=2)

    small_w = dict(g1=ffn1_norm_g, gm=mix_norm_g, g2=ffn2_norm_g, gf=final_norm_g, ng=hgrn_out_norm_g, gq=q_norm_g,
                   gk=k_norm_g, lb=hgrn_lb_logits)
    small_m = dict(g1=m_ffn1_norm_g, gm=m_mix_norm_g, g2=m_ffn2_norm_g, gf=m_final_norm_g, ng=m_hgrn_out_norm_g,
                   gq=m_q_norm_g, gk=m_k_norm_g, lb=m_hgrn_lb_logits)
    small_v = dict(g1=v_ffn1_norm_g, gm=v_mix_norm_g, g2=v_ffn2_norm_g, gf=v_final_norm_g, ng=v_hgrn_out_norm_g,
                   gq=v_q_norm_g, gk=v_k_norm_g, lb=v_hgrn_lb_logits)
    small_names = ("g1", "gm", "g2", "gf", "ng", "gq", "gk", "lb")
    pack = lambda dct: _pack_rows([dct[n] for n in small_names], D_MODEL)
    pw, pgr, pm, pv = pack(small_w), pack(sg), pack(small_m), pack(small_v)
    pad8 = lambda a: jnp.pad(a, ((0, (-a.shape[0]) % 8), (0, 0)))
    sd, sm_, sv_ = _adamw(pad8(pw), pad8(pgr), pad8(pm), pad8(pv))

    def unpack(buf):
        out, r = {}, 0
        for n in small_names:
            size = small_w[n].size
            nr = -(-size // D_MODEL)
            out[n] = buf[r:r + nr].reshape(-1)[:size].reshape(small_w[n].shape)
            r += nr
        return out

    sdelta, snew_m, snew_v = unpack(sd), unpack(sm_), unpack(sv_)
    sgrad = {n: sg[n].reshape(small_w[n].shape) for n in small_names}

    bdelta, bnew_m, bnew_v, bgrad = {}, {}, {}, {}

    def update(names, halves):
        for n, own, got in zip(names, halves, _pair_share(halves)):
            res = _adamw_halves(big_w[n], own, got, big_m[n], big_v[n], c_idx)
            if n in TRANSPOSED:
                res = [r.T for r in res]
            bdelta[n], bnew_m[n], bnew_v[n], bgrad[n] = [r[None] for r in res]

    names3, halves3 = reduced_halves(3, (pending[1][-1],))
    names2, halves2 = reduced_halves(2, (halves3[0],))
    update(names3 + names2, halves3 + halves2)
    names1, halves1 = reduced_halves(1, (bdelta[names2[-1]],))
    update(names1, halves1)

    order = [("s", "g1"), ("b", "g1"), ("b", "u1"), ("b", "d1"), ("s", "gm"), ("b", "in"), ("s", "gq"), ("s", "gk"),
             ("s", "lb"), ("s", "ng"), ("b", "a"), ("b", "b"), ("b", "o"), ("s", "g2"), ("b", "g2"), ("b", "u2"),
             ("b", "d2"), ("s", "gf")]
    outs = [loss_out, dx[None]]
    for table_s, table_b in ((sgrad, bgrad), (sdelta, bdelta), (snew_m, bnew_m), (snew_v, bnew_v)):
        outs += [(table_s if kind == "s" else table_b)[n] for kind, n in order]
    return tuple(outs)
```

```python
import functools

import numpy as np
import jax
import jax.numpy as jnp
from jax import lax
from jax.experimental import pallas as pl
from jax.experimental.pallas import tpu as pltpu

F32 = jnp.float32
BF16 = jnp.bfloat16
CD = jnp.bfloat16

EPS = 1e-6
D_MODEL = 1024
HEAD_DIM = 64
GRID_W = 64
ROPE_THETA = 10000.0
CHUNK = 32
N_SHARD = 4
N_DEV = 8
VMEM_LIMIT = 56 * 1024 * 1024
LANES = 128
HG_TILE = 256

ADAM_LR = 0.001
ADAM_B1 = 0.9
ADAM_B2 = 0.999
ADAM_EPS = 1e-08
ADAM_WD = 0.01
ADAM_STEP = 10

NN = (((1,), (0,)), ((), ()))
NT = (((1,), (1,)), ((), ()))
TN = (((0,), (0,)), ((), ()))
MESH = pl.DeviceIdType.MESH
ANY = pl.BlockSpec(memory_space=pl.ANY)


def _mm(a, b, dn):
    return lax.dot_general(a.astype(CD), b.astype(CD), dn, preferred_element_type=F32)


def _split3(x):
    hi = x.astype(BF16)
    r = x - hi.astype(F32)
    mid = r.astype(BF16)
    lo = (r - mid.astype(F32)).astype(BF16)
    return hi, mid, lo


def _xdot(x, m):
    rows = x.shape[0]
    r = lax.dot_general(jnp.concatenate(_split3(x), axis=0), m, NN, preferred_element_type=F32)
    return r[:rows] + r[rows:2 * rows] + r[2 * rows:]


def _xdot_l(m, x):
    cols = x.shape[1]
    r = lax.dot_general(m, jnp.concatenate(_split3(x), axis=1), NN, preferred_element_type=F32)
    return r[:, :cols] + r[:, cols:2 * cols] + r[:, 2 * cols:]


def _params(n_grid):
    return pltpu.CompilerParams(dimension_semantics=("arbitrary",) * n_grid, vmem_limit_bytes=VMEM_LIMIT)


def _sigmoid(x):
    return jax.nn.sigmoid(x)


def _np_blocksum(n):
    i = np.arange(n)
    return (i[:, None] // HEAD_DIM == i[None, :] // HEAD_DIM).astype(np.float32)


def _np_swap32(n):
    i = np.arange(n)
    partner = np.where(i % HEAD_DIM < HEAD_DIM // 2, i + HEAD_DIM // 2, i - HEAD_DIM // 2)
    m = np.zeros((n, n), np.float32)
    m[i, partner] = 1.0
    return m


def _np_expand_q():
    m = np.zeros((512, 1024), np.float32)
    for h in range(8):
        g = h // 4
        for d in range(HEAD_DIM):
            m[64 * h + d, 128 * h + 64 * g + d] = 1.0
    return m


def _np_bcast_head():
    m = np.zeros((512, 1024), np.float32)
    for h in range(8):
        m[64 * h, 128 * h:128 * h + 128] = 1.0
    return m


def _np_swap_halves():
    m = np.zeros((128, 128), np.float32)
    i = np.arange(128)
    m[i, (i + 64) % 128] = 1.0
    return m


def _np_hgrn_cums(t, rev):
    r = np.arange(t)[:, None]
    c = np.arange(t)[None, :]
    same = (r // CHUNK) == (c // CHUNK)
    if not rev:
        cum = same & (c <= r)
        mid = same & (c % CHUNK <= CHUNK // 2 - 1)
    else:
        cum = same & (c >= r)
        mid = same & (c % CHUNK >= CHUNK // 2)
    return np.concatenate([cum, mid, same], axis=0).astype(np.float32)


def _bf(a):
    return jnp.asarray(a, dtype=BF16)


def _rope_tables(seq_len):
    rows = seq_len // GRID_W
    row = jnp.repeat(jnp.arange(rows, dtype=F32), GRID_W)
    col = jnp.tile(jnp.arange(GRID_W, dtype=F32), rows)
    n_freq = HEAD_DIM // 4
    inv = ROPE_THETA ** (-jnp.arange(n_freq, dtype=F32) / n_freq)
    ang = jnp.concatenate([row[:, None] * inv, col[:, None] * inv], axis=-1)
    cos, sin = jnp.cos(ang), jnp.sin(ang)
    c64 = jnp.concatenate([cos, cos], axis=-1)
    s64 = jnp.concatenate([-sin, sin], axis=-1)
    return jnp.tile(c64, (1, 2)), jnp.tile(s64, (1, 2))


def _ffn_fwd(x, g, wg, wu, wd, tm):
    s, d = x.shape
    nsh, fs, _ = wg.shape

    def body(x_ref, g_ref, wg_ref, wu_ref, wd_ref, xo_ref, a_ref, b_ref, hb_ref, acc, hs):
        j = pl.program_id(1)

        @pl.when(j == 0)
        def _():
            xv = x_ref[...]
            r = lax.rsqrt(jnp.mean(xv * xv, axis=-1, keepdims=True) + EPS)
            h = (xv * r * g_ref[...]).astype(CD)
            hs[...] = h
            hb_ref[...] = h
            acc[...] = jnp.zeros_like(acc)

        h = hs[...]
        a = _mm(h, wg_ref[0], NT)
        b = _mm(h, wu_ref[0], NT)
        f = a * _sigmoid(a) * b
        acc[...] += _mm(f, wd_ref[0], NN)
        a_ref[0] = a.astype(CD)
        b_ref[0] = b.astype(CD)

        @pl.when(j == nsh - 1)
        def _():
            xo_ref[...] = x_ref[...] + 0.5 * acc[...]

    return pl.pallas_call(
        body, name="ffn_fwd", grid=(s // tm, nsh),
        in_specs=[pl.BlockSpec((tm, d), lambda i, j: (i, 0)), pl.BlockSpec((1, d), lambda i, j: (0, 0))]
        + [pl.BlockSpec((1, fs, d), lambda i, j: (j, 0, 0))] * 3,
        out_specs=[pl.BlockSpec((tm, d), lambda i, j: (i, 0)), pl.BlockSpec((1, tm, fs), lambda i, j: (j, i, 0)),
                   pl.BlockSpec((1, tm, fs), lambda i, j: (j, i, 0)), pl.BlockSpec((tm, d), lambda i, j: (i, 0))],
        out_shape=[jax.ShapeDtypeStruct((s, d), F32), jax.ShapeDtypeStruct((nsh, s, fs), CD),
                   jax.ShapeDtypeStruct((nsh, s, fs), CD), jax.ShapeDtypeStruct((s, d), CD)],
        scratch_shapes=[pltpu.VMEM((tm, d), F32), pltpu.VMEM((tm, d), CD)],
        compiler_params=_params(2),
    )(x, g, wg, wu, wd)


def _ffn_bwd(dout, x, g, a, b, wg, wu, wd, tm):
    s, d = x.shape
    nsh, fs, _ = wg.shape

    def body(do_ref, x_ref, g_ref, a_ref, b_ref, wg_ref, wu_ref, wd_ref, dx_ref, da_ref, db_ref, f_ref, dg_ref, dh):
        i = pl.program_id(0)
        j = pl.program_id(1)

        @pl.when(j == 0)
        def _():
            dh[...] = jnp.zeros_like(dh)

        @pl.when((i == 0) & (j == 0))
        def _():
            dg_ref[...] = jnp.zeros_like(dg_ref)

        av = a_ref[0].astype(F32)
        bv = b_ref[0].astype(F32)
        sg = _sigmoid(av)
        sl = av * sg
        df = 0.5 * _mm(do_ref[...], wd_ref[0], NT)
        da = df * bv * (sg * (1.0 + av * (1.0 - sg)))
        db = df * sl
        dh[...] += _mm(da, wg_ref[0], NN) + _mm(db, wu_ref[0], NN)
        da_ref[0] = da.astype(CD)
        db_ref[0] = db.astype(CD)
        f_ref[0] = (sl * bv).astype(CD)

        @pl.when(j == nsh - 1)
        def _():
            xv = x_ref[...]
            r = lax.rsqrt(jnp.mean(xv * xv, axis=-1, keepdims=True) + EPS)
            dhv = dh[...]
            u = dhv * g_ref[...]
            dx_ref[...] = do_ref[...] + r * u - xv * (r * r * r) * jnp.mean(u * xv, axis=-1, keepdims=True)
            dg_ref[...] += jnp.sum(dhv * xv * r, axis=0, keepdims=True)

    act = pl.BlockSpec((1, tm, fs), lambda i, j: (j, i, 0))
    row = pl.BlockSpec((tm, d), lambda i, j: (i, 0))
    return pl.pallas_call(
        body, name="ffn_bwd", grid=(s // tm, nsh),
        in_specs=[row, row, pl.BlockSpec((1, d), lambda i, j: (0, 0)), act, act]
        + [pl.BlockSpec((1, fs, d), lambda i, j: (j, 0, 0))] * 3,
        out_specs=[row, act, act, act, pl.BlockSpec((1, d), lambda i, j: (0, 0))],
        out_shape=[jax.ShapeDtypeStruct((s, d), F32), jax.ShapeDtypeStruct((nsh, s, fs), CD),
                   jax.ShapeDtypeStruct((nsh, s, fs), CD), jax.ShapeDtypeStruct((nsh, s, fs), CD),
                   jax.ShapeDtypeStruct((1, d), F32)],
        scratch_shapes=[pltpu.VMEM((tm, d), F32)],
        compiler_params=_params(2),
    )(dout, x, g, a, b, wg, wu, wd)


def _tn_call(name, operands, in_specs, out_shape, out_spec, grid, acc_shape, pick, scale=1.0):
    nk = grid[-1]
    n_in = len(operands)

    def body(*refs):
        out_ref, acc = refs[n_in], refs[n_in + 1]
        k = pl.program_id(len(grid) - 1)

        @pl.when(k == 0)
        def _():
            acc[...] = jnp.zeros_like(acc)

        pick(refs[:n_in], acc)

        @pl.when(k == nk - 1)
        def _():
            res = acc[...] if scale == 1.0 else acc[...] * scale
            out_ref[...] = res.reshape(out_ref.shape)

    return pl.pallas_call(
        body, name=name, grid=grid, in_specs=in_specs, out_specs=out_spec, out_shape=out_shape,
        scratch_shapes=[pltpu.VMEM(acc_shape, F32)], compiler_params=_params(len(grid)),
    )(*operands)


def _dw_shared_b(name, a3, b, tk, scale):
    nj, s, m = a3.shape
    n = b.shape[1]

    def pick(refs, acc):
        acc[...] += _mm(refs[0][0], refs[1][...], TN)

    return _tn_call(name, (a3, b),
                    [pl.BlockSpec((1, tk, m), lambda j, k: (j, k, 0)), pl.BlockSpec((tk, n), lambda j, k: (k, 0))],
                    jax.ShapeDtypeStruct((nj, m, n), F32), pl.BlockSpec((1, m, n), lambda j, k: (j, 0, 0)),
                    (nj, s // tk), (m, n), pick, scale)


def _dw_colblocks(name, a, b, nj, tk):
    s, m = a.shape
    n = b.shape[1] // nj

    def pick(refs, acc):
        acc[...] += _mm(refs[0][...], refs[1][...], TN)

    return _tn_call(name, (a, b),
                    [pl.BlockSpec((tk, m), lambda j, k: (k, 0)), pl.BlockSpec((tk, n), lambda j, k: (k, j))],
                    jax.ShapeDtypeStruct((nj, m, n), F32), pl.BlockSpec((1, m, n), lambda j, k: (j, 0, 0)),
                    (nj, s // tk), (m, n), pick)


DP_WIDTHS = (768, 512, 512, 512, 512, 512, 2048)
DP_CHUNK = 256


def _dp_chunk_maps():
    starts, counts, off = [], [], 0
    for w in DP_WIDTHS:
        starts.append(off // DP_CHUNK)
        counts.append(w // DP_CHUNK)
        off += w
    return starts, counts


def _dp_specs(tm, row_axis, chunk_axis):
    starts, counts = _dp_chunk_maps()
    specs = []
    for st, cnt in zip(starts, counts):
        def imap(*ids, st=st, cnt=cnt):
            return (ids[row_axis], jnp.clip(ids[chunk_axis] - st, 0, cnt - 1))
        specs.append(pl.BlockSpec((tm, DP_CHUNK), imap))
    return specs


def _dp_select(n, refs, fn):
    starts, counts = _dp_chunk_maps()
    for ref, st, cnt in zip(refs, starts, counts):
        @pl.when((n >= st) & (n < st + cnt))
        def _(ref=ref):
            fn(ref)


def _dw_in(dps, hb, tk):
    s, d = hb.shape
    n_chunks = sum(DP_WIDTHS) // DP_CHUNK

    def pick(refs, acc):
        def add(ref):
            acc[...] += _mm(ref[...], refs[7][...], TN)

        _dp_select(pl.program_id(0), refs[:7], add)

    return _tn_call("dw_in", (*dps, hb),
                    _dp_specs(tk, 1, 0) + [pl.BlockSpec((tk, d), lambda n, k: (k, 0))],
                    jax.ShapeDtypeStruct((n_chunks * DP_CHUNK, d), F32), pl.BlockSpec((DP_CHUNK, d), lambda n, k: (n, 0)),
                    (n_chunks, s // tk), (DP_CHUNK, d), pick)


def _mix_in_fwd(x, g, w_t, tm):
    s, d = x.shape
    n_in = w_t.shape[0]

    def body(x_ref, g_ref, w_ref, qkv_ref, hg_ref, gt_ref, hb_ref):
        xv = x_ref[...]
        r = lax.rsqrt(jnp.mean(xv * xv, axis=-1, keepdims=True) + EPS)
        h = (xv * r * g_ref[...]).astype(CD)
        hb_ref[...] = h
        qkv_ref[...] = _mm(h, w_ref[0:768, :], NT)
        for c in range(5):
            hg_ref[:, 512 * c:512 * c + 512] = _mm(h, w_ref[768 + 512 * c:768 + 512 * c + 512, :], NT)
        for c in range(2):
            gt_ref[:, 1024 * c:1024 * c + 1024] = _mm(h, w_ref[3328 + 1024 * c:3328 + 1024 * c + 1024, :], NT)

    row = lambda w: pl.BlockSpec((tm, w), lambda i: (i, 0))
    return pl.pallas_call(
        body, name="mix_in_fwd", grid=(s // tm,),
        in_specs=[row(d), pl.BlockSpec((1, d), lambda i: (0, 0)), pl.BlockSpec((n_in, d), lambda i: (0, 0))],
        out_specs=[row(768), row(2560), row(2048), row(d)],
        out_shape=[jax.ShapeDtypeStruct((s, 768), F32), jax.ShapeDtypeStruct((s, 2560), F32),
                   jax.ShapeDtypeStruct((s, 2048), F32), jax.ShapeDtypeStruct((s, d), CD)],
        compiler_params=_params(1),
    )(x, g, w_t)


def _mix_in_bwd(dps, w_t, x, dres, g, tm, after=()):
    s, d = x.shape
    n_in = w_t.shape[0]

    def body(*refs):
        refs = refs[len(after):]
        dp_refs = refs[:7]
        w_ref, x_ref, dr_ref, g_ref, dx_ref, dg_ref = refs[7:]

        @pl.when(pl.program_id(0) == 0)
        def _():
            dg_ref[...] = jnp.zeros_like(dg_ref)

        dhv = jnp.zeros((tm, d), F32)
        off = 0
        for ref, width in zip(dp_refs, DP_WIDTHS):
            dhv = dhv + _mm(ref[...], w_ref[off:off + width, :], NN)
            off += width
        xv = x_ref[...]
        r = lax.rsqrt(jnp.mean(xv * xv, axis=-1, keepdims=True) + EPS)
        u = dhv * g_ref[...]
        dx_ref[...] = dr_ref[...] + r * u - xv * (r * r * r) * jnp.mean(u * xv, axis=-1, keepdims=True)
        dg_ref[...] += jnp.sum(dhv * xv * r, axis=0, keepdims=True)

    row = pl.BlockSpec((tm, d), lambda i: (i, 0))
    vec = pl.BlockSpec((1, d), lambda i: (0, 0))
    return pl.pallas_call(
        body, name="mix_in_bwd", grid=(s // tm,),
        in_specs=[ANY] * len(after) + [pl.BlockSpec((tm, w), lambda i: (i, 0)) for w in DP_WIDTHS]
        + [pl.BlockSpec((n_in, d), lambda i: (0, 0)), row, row, vec],
        out_specs=[row, vec],
        out_shape=[jax.ShapeDtypeStruct((s, d), F32), jax.ShapeDtypeStruct((1, d), F32)],
        compiler_params=_params(1),
    )(*after, *dps, w_t, x, dres, g)


def _headnorm_rope(x, gain, cos, sin, blocksum, swap):
    ss = _xdot(x * x, blocksum)
    r = lax.rsqrt(ss * (1.0 / HEAD_DIM) + EPS)
    y = x * r * gain
    return y * cos + _xdot(y, swap) * sin, r


def _headnorm_rope_bwd(dz, x, gain, cos, sin, blocksum, swap):
    ss = _xdot(x * x, blocksum)
    r = lax.rsqrt(ss * (1.0 / HEAD_DIM) + EPS)
    dy = dz * cos + _xdot(dz * sin, swap)
    u = dy * gain
    mean_ux = _xdot(u * x, blocksum) * (1.0 / HEAD_DIM)
    dx = r * u - x * (r * r * r) * mean_ux
    return dx, jnp.sum(dy * x * r, axis=0, keepdims=True)


def _qk_prep(pqkv, gq, gk, cos2, sin2, tm):
    s = pqkv.shape[0]
    bs512, sw512, eq, swh = _bf(_np_blocksum(512)), _bf(_np_swap32(512)), _bf(_np_expand_q()), _bf(_np_swap_halves())

    def body(q_ref, kv_ref, gq_ref, gk_ref, c_ref, s_ref, bs_ref, sw_ref, eq_ref, swh_ref, qe_ref, k_ref, v_ref, vs_ref):
        c2, s2 = c_ref[...], s_ref[...]
        c8, s8 = jnp.tile(c2, (1, 4)), jnp.tile(s2, (1, 4))
        bs, sw = bs_ref[...], sw_ref[...]
        zq, _ = _headnorm_rope(q_ref[...], gq_ref[...], c8, s8, bs, sw)
        qe_ref[...] = _mm(zq * (HEAD_DIM ** -0.5), eq_ref[...], NN).astype(CD)
        kv = kv_ref[...]
        zk, _ = _headnorm_rope(kv[:, :LANES], gk_ref[...], c2, s2, bs[:LANES, :LANES], sw[:LANES, :LANES])
        k_ref[...] = zk.astype(CD)
        v = kv[:, LANES:]
        v_ref[...] = v.astype(CD)
        vs_ref[...] = _mm(v, swh_ref[...], NN).astype(CD)

    full = lambda a: pl.BlockSpec(a.shape, lambda i: (0,) * a.ndim)
    tab = pl.BlockSpec((tm, LANES), lambda i: (i, 0))
    return pl.pallas_call(
        body, name="qk_prep", grid=(s // tm,),
        in_specs=[pl.BlockSpec((tm, 512), lambda i: (i, 0)), pl.BlockSpec((tm, 256), lambda i: (i, 2)),
                  full(gq), full(gk), tab, tab, full(bs512), full(sw512), full(eq), full(swh)],
        out_specs=[pl.BlockSpec((tm, 1024), lambda i: (i, 0)), tab, tab, tab],
        out_shape=[jax.ShapeDtypeStruct((s, 1024), CD)] + [jax.ShapeDtypeStruct((s, LANES), CD)] * 3,
        compiler_params=_params(1),
    )(pqkv, pqkv, gq, gk, cos2, sin2, bs512, sw512, eq, swh)


def _qk_prep_bwd(pqkv, dqe, dk, dv, gq, gk, cos2, sin2, tm):
    s = pqkv.shape[0]
    bs512, sw512, eqt = _bf(_np_blocksum(512)), _bf(_np_swap32(512)), _bf(_np_expand_q().T)

    def body(q_ref, kv_ref, dqe_ref, dk_ref, dv_ref, gq_ref, gk_ref, c_ref, s_ref, bs_ref, sw_ref, eqt_ref,
             dp_ref, dgq_ref, dgk_ref):
        @pl.when(pl.program_id(0) == 0)
        def _():
            dgq_ref[...] = jnp.zeros_like(dgq_ref)
            dgk_ref[...] = jnp.zeros_like(dgk_ref)

        c2, s2 = c_ref[...], s_ref[...]
        c8, s8 = jnp.tile(c2, (1, 4)), jnp.tile(s2, (1, 4))
        bs, sw = bs_ref[...], sw_ref[...]
        dzq = _xdot(dqe_ref[...], eqt_ref[...]) * (HEAD_DIM ** -0.5)
        dxq, dgq = _headnorm_rope_bwd(dzq, q_ref[...], gq_ref[...], c8, s8, bs, sw)
        kv = kv_ref[...]
        dxk, dgk = _headnorm_rope_bwd(dk_ref[...], kv[:, :LANES], gk_ref[...], c2, s2, bs[:LANES, :LANES], sw[:LANES, :LANES])
        dp_ref[...] = jnp.concatenate([dxq, dxk, dv_ref[...]], axis=1).astype(CD)
        dgq_ref[...] += dgq
        dgk_ref[...] += dgk

    full = lambda a: pl.BlockSpec(a.shape, lambda i: (0,) * a.ndim)
    tab = pl.BlockSpec((tm, LANES), lambda i: (i, 0))
    return pl.pallas_call(
        body, name="qk_prep_bwd", grid=(s // tm,),
        in_specs=[pl.BlockSpec((tm, 512), lambda i: (i, 0)), pl.BlockSpec((tm, 256), lambda i: (i, 2)),
                  pl.BlockSpec((tm, 1024), lambda i: (i, 0)), tab, tab, full(gq), full(gk), tab, tab,
                  full(bs512), full(sw512), full(eqt)],
        out_specs=[pl.BlockSpec((tm, 768), lambda i: (i, 0)), pl.BlockSpec((1, 512), lambda i: (0, 0)),
                   pl.BlockSpec((1, LANES), lambda i: (0, 0))],
        out_shape=[jax.ShapeDtypeStruct((s, 768), CD), jax.ShapeDtypeStruct((1, 512), F32),
                   jax.ShapeDtypeStruct((1, LANES), F32)],
        compiler_params=_params(1),
    )(pqkv, pqkv, dqe, dk, dv, gq, gk, cos2, sin2, bs512, sw512, eqt)


def _attn_fwd(qe, k, v, vs, tq):
    s = k.shape[0]

    def body(qa_ref, qb_ref, k_ref, v_ref, vs_ref, o_ref, lse_ref):
        m = pl.program_id(0)
        grp = m // 2
        kk = k_ref[...]
        outs = []
        for idx, q_ref in enumerate((qa_ref, qb_ref)):
            sc = _mm(q_ref[...], kk, NT)
            mx = jnp.max(sc, axis=-1, keepdims=True)
            e = jnp.exp(sc - mx)
            l = jnp.sum(e, axis=-1, keepdims=True)
            lse_ref[idx] = mx + jnp.log(l)
            p = e * (1.0 / l)
            vsel = jnp.where(grp != idx, vs_ref[...], v_ref[...])
            outs.append(_mm(p, vsel, NN))
        lane = lax.broadcasted_iota(jnp.int32, (1, LANES), 1)
        o_ref[...] = jnp.where(lane < HEAD_DIM, outs[0], outs[1])

    kv = pl.BlockSpec((s, LANES), lambda m, i: (0, 0))
    return pl.pallas_call(
        body, name="attn_fwd", grid=(4, s // tq),
        in_specs=[pl.BlockSpec((tq, LANES), lambda m, i: (i, 2 * m)), pl.BlockSpec((tq, LANES), lambda m, i: (i, 2 * m + 1)),
                  kv, kv, kv],
        out_specs=[pl.BlockSpec((tq, LANES), lambda m, i: (i, m)), pl.BlockSpec((2, tq, 1), lambda m, i: (m, i, 0))],
        out_shape=[jax.ShapeDtypeStruct((s, 512), F32), jax.ShapeDtypeStruct((8, s, 1), F32)],
        compiler_params=_params(2),
    )(qe, qe, k, v, vs)


def _attn_bwd(qe, k, v, doe, delta, lse, tq):
    s = k.shape[0]

    def body(q_ref, k_ref, v_ref, do_ref, dl_ref, lse_ref, dq_ref, dk_ref, dv_ref):
        @pl.when((pl.program_id(0) == 0) & (pl.program_id(1) == 0))
        def _():
            dk_ref[...] = jnp.zeros_like(dk_ref)
            dv_ref[...] = jnp.zeros_like(dv_ref)

        q, kk, do = q_ref[...], k_ref[...], do_ref[...]
        p = jnp.exp(_mm(q, kk, NT) - lse_ref[0])
        dp = _mm(do, v_ref[...], NT)
        ds = p * (dp - jnp.max(dl_ref[...], axis=-1, keepdims=True))
        dq_ref[...] = _mm(ds, kk, NN)
        dk_ref[...] += _mm(ds, q, TN)
        dv_ref[...] += _mm(p, do, TN)

    kv = pl.BlockSpec((s, LANES), lambda h, i: (0, 0))
    blk = pl.BlockSpec((tq, LANES), lambda h, i: (i, h))
    return pl.pallas_call(
        body, name="attn_bwd", grid=(8, s // tq),
        in_specs=[blk, kv, kv, blk, blk, pl.BlockSpec((1, tq, 1), lambda h, i: (h, i, 0))],
        out_specs=[blk, kv, kv],
        out_shape=[jax.ShapeDtypeStruct((s, 1024), F32), jax.ShapeDtypeStruct((s, LANES), F32),
                   jax.ShapeDtypeStruct((s, LANES), F32)],
        compiler_params=_params(2),
    )(qe, k, v, doe, delta, lse)


@jax.custom_vjp
def _mm_nn(a, b):
    return _mm(a, b, NN)


_mm_nn.defvjp(lambda a, b: (_mm(a, b, NN), (a, b)),
              lambda res, g: (_mm(g, res[1], NT), _mm(res[0], g, TN)))


@jax.custom_vjp
def _mm_nt(a, b):
    return _mm(a, b, NT)


_mm_nt.defvjp(lambda a, b: (_mm(a, b, NT), (a, b)),
              lambda res, g: (_mm(g, res[1], NN), _mm(g, res[0], TN)))


@jax.custom_vjp
def _mm_tn(a, b):
    return _mm(a, b, TN)


_mm_tn.defvjp(lambda a, b: (_mm(a, b, TN), (a, b)),
              lambda res, g: (_mm(res[1], g, NT), _mm(res[0], g, NN)))


@jax.custom_vjp
def _cmm(m, mt, x):
    return _xdot_l(m, x)


_cmm.defvjp(lambda m, mt, x: (_xdot_l(m, x), (m, mt)),
            lambda res, g: (jnp.zeros_like(res[0]), jnp.zeros_like(res[1]), _xdot_l(res[1], g)))


def _hgrn_masks(t, rev):
    n_ch = t // CHUNK
    r = jnp.bitwise_and(lax.broadcasted_iota(jnp.int32, (2 * t, t), 0), t - 1)
    c = lax.broadcasted_iota(jnp.int32, (2 * t, t), 1)
    same = jnp.right_shift(r, 5) == jnp.right_shift(c, 5)
    tri2 = same & ((c >= r) if rev else (c <= r))
    pr = lax.broadcasted_iota(jnp.int32, (LANES, LANES), 0)
    pc = lax.broadcasted_iota(jnp.int32, (LANES, LANES), 1)
    diag = jnp.right_shift(pr, 6) == jnp.right_shift(pc, 6)
    qr = lax.broadcasted_iota(jnp.int32, (t, n_ch * LANES), 0)
    qc = lax.broadcasted_iota(jnp.int32, (t, n_ch * LANES), 1)
    rows_chunk = jnp.right_shift(qc, 7) == jnp.right_shift(qr, 5)
    vr = lax.broadcasted_iota(jnp.int32, (n_ch * LANES, t), 0)
    vc = lax.broadcasted_iota(jnp.int32, (n_ch * LANES, t), 1)
    cols_chunk = jnp.right_shift(vr, 7) == jnp.right_shift(vc, 5)
    return dict(tri2=tri2, diag=diag, rows_chunk=rows_chunk, cols_chunk=cols_chunk)


def _hgrn_gates(xf, lb):
    f = lb + (1.0 - lb) * _sigmoid(xf)
    return 1.0 - f, jnp.log(f)


def _hgrn_dir(xq, xf, v, lb, state, cm, cmt, mk, rev):
    t = xq.shape[0]
    n_ch = t // CHUNK
    lo = lax.broadcasted_iota(jnp.int32, (1, LANES), 1) < HEAD_DIM
    q = xq * _sigmoid(xq)
    k, lf = _hgrn_gates(xf, lb)
    cs = _cmm(cm, cmt, lf)
    b, bm, bl = cs[:t], cs[t:2 * t], cs[2 * t:]
    qd = q * jnp.exp(b - bm)
    kd = k * jnp.exp(bm - b)
    kc = k * jnp.exp(bl - b)
    qe = q * jnp.exp(b)
    qd2 = jnp.concatenate([jnp.where(lo, qd, 0.0), jnp.where(lo, 0.0, qd)], axis=0)
    o2 = _mm_nn(jnp.where(mk["tri2"], _mm_nt(qd2, kd), 0.0), v)
    o = jnp.where(lo, o2[:t], o2[t:])
    vexp = jnp.where(mk["cols_chunk"], jnp.concatenate([jnp.transpose(v)] * n_ch, axis=0), 0.0)
    adds = _mm_nn(vexp, kc)
    dec = jnp.exp(bl)
    entering = [None] * n_ch
    for c in (range(n_ch - 1, -1, -1) if rev else range(n_ch)):
        entering[c] = state
        d = jnp.concatenate([dec[c * CHUNK:(c + 1) * CHUNK]] * (LANES // CHUNK), axis=0)
        state = d * state + jnp.where(mk["diag"], adds[c * LANES:(c + 1) * LANES], 0.0)
    qexp = jnp.where(mk["rows_chunk"], jnp.concatenate([qe] * n_ch, axis=1), 0.0)
    return o + _mm_nt(qexp, jnp.concatenate(entering, axis=1)), state


def _hgrn_lower_bounds(l):
    out = []
    for d in (0, 1):
        l0, l1 = l[2 * d:2 * d + 1, :], l[2 * d + 1:2 * d + 2, :]
        mx = jnp.maximum(l0, l1)
        e0, e1 = jnp.exp(l0 - mx), jnp.exp(l1 - mx)
        out.append(e0 / (e0 + e1))
    return out


def _hgrn_consts(t):
    cf, cb = _np_hgrn_cums(t, False), _np_hgrn_cums(t, True)
    return (_bf(cf), _bf(cf.T), _bf(cb), _bf(cb.T), _bf(_np_blocksum(LANES)))


def _hgrn_fwd(ph, lbl, ng):
    s = ph.shape[0]
    t = min(HG_TILE, s)
    nt = s // t
    consts = _hgrn_consts(t)

    def body(xq_ref, xff_ref, xfb_ref, xi_ref, xg_ref, lbl_ref, ng_ref, cf_ref, cft_ref, cb_ref, cbt_ref, bs_ref,
             o_ref, pre_ref, st_ref, acc):
        lbf, lbb = _hgrn_lower_bounds(lbl_ref)
        mk_f, mk_b = _hgrn_masks(t, False), _hgrn_masks(t, True)
        zero = jnp.zeros((LANES, LANES), F32)

        def rows_of(i):
            return pl.ds(pl.multiple_of(i * t, t), t)

        acc[...] = jnp.zeros_like(acc)

        def step(i, states):
            tb = nt - 1 - i
            rf, rb = rows_of(i), rows_of(tb)
            st_ref[0, 0, i] = states[0]
            st_ref[0, 1, tb] = states[1]
            of, sf = _hgrn_dir(xq_ref[rf, :], xff_ref[rf, :], xi_ref[rf, :], lbf, states[0],
                               cf_ref[...], cft_ref[...], mk_f, False)
            ob, sb = _hgrn_dir(xq_ref[rb, :], xfb_ref[rb, :], xi_ref[rb, :], lbb, states[1],
                               cb_ref[...], cbt_ref[...], mk_b, True)
            acc[rf, :] += of
            acc[rb, :] += ob
            return sf, sb

        lax.fori_loop(0, nt, step, (zero, zero))

        def step_n(i, carry):
            rows = rows_of(i)
            o = acc[rows, :]
            ss = _xdot(o * o, bs_ref[...])
            r = lax.rsqrt(ss * (1.0 / HEAD_DIM) + EPS)
            xg = xg_ref[rows, :]
            pre_ref[rows, :] = o
            o_ref[rows, :] = (o * r * ng_ref[...]) * (xg * _sigmoid(xg))
            return carry

        lax.fori_loop(0, nt, step_n, 0)

    col = lambda off: pl.BlockSpec((s, LANES), lambda m: (0, off + m))
    full = lambda a: pl.BlockSpec(a.shape, lambda m: (0,) * a.ndim)
    return pl.pallas_call(
        body, name="hgrn_fwd", grid=(4,),
        in_specs=[col(0), col(4), col(8), col(12), col(16), pl.BlockSpec((4, LANES), lambda m: (0, m)),
                  pl.BlockSpec((1, LANES), lambda m: (0, m))] + [full(c) for c in consts],
        out_specs=[col(0), col(0), pl.BlockSpec((1, 2, nt, LANES, LANES), lambda m: (m, 0, 0, 0, 0))],
        out_shape=[jax.ShapeDtypeStruct((s, 512), F32), jax.ShapeDtypeStruct((s, 512), F32),
                   jax.ShapeDtypeStruct((4, 2, nt, LANES, LANES), F32)],
        scratch_shapes=[pltpu.VMEM((s, LANES), F32)],
        compiler_params=_params(1),
    )(ph, ph, ph, ph, ph, lbl, ng, *consts)


def _hgrn_bwd(ph, pre, dout, states, lbl, ng):
    s = ph.shape[0]
    t = min(HG_TILE, s)
    nt = s // t
    consts = _hgrn_consts(t)

    def body(xq_ref, xff_ref, xfb_ref, xi_ref, xg_ref, pre_ref, do_ref, st_ref, lbl_ref, ng_ref,
             cf_ref, cft_ref, cb_ref, cbt_ref, bs_ref,
             dq_ref, dff_ref, dfb_ref, di_ref, dg_ref, dlb_ref, dng_ref, dpre, dq_acc, dv_acc):
        lbf, lbb = _hgrn_lower_bounds(lbl_ref)
        mk_f, mk_b = _hgrn_masks(t, False), _hgrn_masks(t, True)
        zero = jnp.zeros((LANES, LANES), F32)
        zrow = jnp.zeros((1, LANES), F32)

        def rows_of(i):
            return pl.ds(pl.multiple_of(i * t, t), t)

        def step_n(i, dng):
            rows = rows_of(i)
            o, xg, do = pre_ref[rows, :], xg_ref[rows, :], do_ref[rows, :]
            bs = bs_ref[...]
            r = lax.rsqrt(_xdot(o * o, bs) * (1.0 / HEAD_DIM) + EPS)
            sg = _sigmoid(xg)
            gate = xg * sg
            don = do * gate
            dg_ref[rows, :] = (do * (o * r * ng_ref[...]) * (sg * (1.0 + xg * (1.0 - sg)))).astype(CD)
            u = don * ng_ref[...]
            dpre[rows, :] = r * u - o * (r * r * r) * (_xdot(u * o, bs) * (1.0 / HEAD_DIM))
            return dng + jnp.sum(don * o * r, axis=0, keepdims=True)

        dng_ref[...] = lax.fori_loop(0, nt, step_n, zrow)

        dq_acc[...] = jnp.zeros_like(dq_acc)
        dv_acc[...] = jnp.zeros_like(dv_acc)

        def grad_tile(ti, xf_ref, df_ref, lb, cm, cmt, mk, rev, st, dstate):
            rows = rows_of(ti)
            fn = lambda xq, xf, v, lbv, s_in: _hgrn_dir(xq, xf, v, lbv, s_in, cm, cmt, mk, rev)
            _, vjp = jax.vjp(fn, xq_ref[rows, :], xf_ref[rows, :], xi_ref[rows, :], lb, st)
            dxq, dxf, dv, dlb_t, dstate = vjp((dpre[rows, :], dstate))
            df_ref[rows, :] = dxf.astype(CD)
            dq_acc[rows, :] += dxq
            dv_acc[rows, :] += dv
            return dstate, dlb_t

        def step_g(i, carry):
            dsf, dsb, dlbf, dlbb = carry
            tf, tb = nt - 1 - i, i
            dsf, gf = grad_tile(tf, xff_ref, dff_ref, lbf, cf_ref[...], cft_ref[...], mk_f, False, st_ref[0, 0, tf], dsf)
            dsb, gb = grad_tile(tb, xfb_ref, dfb_ref, lbb, cb_ref[...], cbt_ref[...], mk_b, True, st_ref[0, 1, tb], dsb)
            return dsf, dsb, dlbf + gf, dlbb + gb

        _, _, dlbf, dlbb = lax.fori_loop(0, nt, step_g, (zero, zero, zrow, zrow))
        dlb_ref[0:1, :] = dlbf
        dlb_ref[1:2, :] = dlbb
        dq_ref[...] = dq_acc[...].astype(CD)
        di_ref[...] = dv_acc[...].astype(CD)

    col = lambda off: pl.BlockSpec((s, LANES), lambda m: (0, off + m))
    full = lambda a: pl.BlockSpec(a.shape, lambda m: (0,) * a.ndim)
    stream = jax.ShapeDtypeStruct((s, 512), CD)
    return pl.pallas_call(
        body, name="hgrn_bwd", grid=(4,),
        in_specs=[col(0), col(4), col(8), col(12), col(16), col(0), col(0),
                  pl.BlockSpec((1, 2, nt, LANES, LANES), lambda m: (m, 0, 0, 0, 0)),
                  pl.BlockSpec((4, LANES), lambda m: (0, m)),
                  pl.BlockSpec((1, LANES), lambda m: (0, m))] + [full(c) for c in consts],
        out_specs=[col(0)] * 5 + [pl.BlockSpec((2, LANES), lambda m: (0, m)), pl.BlockSpec((1, LANES), lambda m: (0, m))],
        out_shape=[stream] * 5 + [jax.ShapeDtypeStruct((2, 512), F32), jax.ShapeDtypeStruct((1, 512), F32)],
        scratch_shapes=[pltpu.VMEM((s, LANES), F32), pltpu.VMEM((s, LANES), F32), pltpu.VMEM((s, LANES), F32)],
        compiler_params=_params(1),
    )(ph, ph, ph, ph, ph, pre, dout, states, lbl, ng, *consts)


def _branch_out(o, w4):
    return jnp.concatenate([_mm(o, w4[j], NN) for j in range(N_SHARD)], axis=1)


def _mix_out_fwd(x, oa, ob, pg, wa, wb, wo, tm):
    s, d = x.shape

    def body(x_ref, oa_ref, ob_ref, ga_ref, gb_ref, wa_ref, wb_ref, wo_ref, xo_ref):
        ya = _branch_out(oa_ref[...], wa_ref)
        yb = _branch_out(ob_ref[...], wb_ref)
        merged = _sigmoid(ga_ref[...]) * ya + _sigmoid(gb_ref[...]) * yb
        xo_ref[...] = x_ref[...] + _mm(merged, wo_ref[...], NN)

    row = pl.BlockSpec((tm, d), lambda i: (i, 0))
    half = pl.BlockSpec((tm, 512), lambda i: (i, 0))
    full = lambda a: pl.BlockSpec(a.shape, lambda i: (0,) * a.ndim)
    return pl.pallas_call(
        body, name="mix_out_fwd", grid=(s // tm,),
        in_specs=[row, half, half, row, pl.BlockSpec((tm, d), lambda i: (i, 1)), full(wa), full(wb), full(wo)],
        out_specs=row, out_shape=jax.ShapeDtypeStruct((s, d), F32),
        compiler_params=_params(1),
    )(x, oa, ob, pg, pg, wa, wb, wo)


def _mix_out_bwd(dx, oa, ob, pg, wa, wb, wo, tm, after=()):
    s, d = dx.shape
    bs512, eq, ebc = _bf(_np_blocksum(512)), _bf(_np_expand_q()), _bf(_np_bcast_head())

    def body(*refs):
        (dx_ref, oa_ref, ob_ref, ga_ref, gb_ref, wa_ref, wb_ref, wo_ref, bs_ref, eq_ref, ebc_ref,
         dpg_ref, mg_ref, dya_ref, dyb_ref, doe_ref, dl_ref, dob_ref) = refs[len(after):]
        oa = oa_ref[...]
        ya = _branch_out(oa, wa_ref)
        yb = _branch_out(ob_ref[...], wb_ref)
        sa, sb = _sigmoid(ga_ref[...]), _sigmoid(gb_ref[...])
        mg_ref[...] = (sa * ya + sb * yb).astype(CD)
        dm = _mm(dx_ref[...], wo_ref[...], NT)
        dpg_ref[...] = jnp.concatenate([dm * ya * sa * (1.0 - sa), dm * yb * sb * (1.0 - sb)], axis=1).astype(CD)
        dya, dyb = dm * sa, dm * sb
        dya_ref[...] = dya.astype(CD)
        dyb_ref[...] = dyb.astype(CD)
        doa = jnp.zeros(oa.shape, F32)
        dob = jnp.zeros(oa.shape, F32)
        for j in range(N_SHARD):
            doa = doa + _mm(dya[:, 256 * j:256 * j + 256], wa_ref[j], NT)
            dob = dob + _mm(dyb[:, 256 * j:256 * j + 256], wb_ref[j], NT)
        dob_ref[...] = dob
        doe_ref[...] = _mm(doa, eq_ref[...], NN).astype(CD)
        dl_ref[...] = _xdot(_xdot(doa * oa, bs_ref[...]), ebc_ref[...])

    row = pl.BlockSpec((tm, d), lambda i: (i, 0))
    half = pl.BlockSpec((tm, 512), lambda i: (i, 0))
    full = lambda a: pl.BlockSpec(a.shape, lambda i: (0,) * a.ndim)
    wide = jax.ShapeDtypeStruct((s, d), CD)
    return pl.pallas_call(
        body, name="mix_out_bwd", grid=(s // tm,),
        in_specs=[ANY] * len(after) + [row, half, half, row, pl.BlockSpec((tm, d), lambda i: (i, 1)), full(wa), full(wb),
                                       full(wo), full(bs512), full(eq), full(ebc)],
        out_specs=[pl.BlockSpec((tm, 2048), lambda i: (i, 0)), row, row, row, row, row, half],
        out_shape=[jax.ShapeDtypeStruct((s, 2048), CD), wide, wide, wide, wide, jax.ShapeDtypeStruct((s, d), F32),
                   jax.ShapeDtypeStruct((s, 512), F32)],
        compiler_params=_params(1),
    )(*after, dx, oa, ob, pg, pg, wa, wb, wo, bs512, eq, ebc)


def _loss_head(x, g, target, tm):
    s, d = x.shape

    def body(x_ref, g_ref, t_ref, dx_ref, loss_ref, dg_ref):
        @pl.when(pl.program_id(0) == 0)
        def _():
            loss_ref[...] = jnp.zeros_like(loss_ref)
            dg_ref[...] = jnp.zeros_like(dg_ref)

        xv = x_ref[...]
        r = lax.rsqrt(jnp.mean(xv * xv, axis=-1, keepdims=True) + EPS)
        err = xv * r * g_ref[...] - t_ref[...]
        loss_ref[...] += 0.5 * jnp.sum(jnp.mean(err * err, axis=-1, keepdims=True))
        dy = err * (1.0 / d)
        u = dy * g_ref[...]
        dx_ref[...] = r * u - xv * (r * r * r) * jnp.mean(u * xv, axis=-1, keepdims=True)
        dg_ref[...] += jnp.sum(dy * xv * r, axis=0, keepdims=True)

    row = pl.BlockSpec((tm, d), lambda i: (i, 0))
    vec = pl.BlockSpec((1, d), lambda i: (0, 0))
    return pl.pallas_call(
        body, name="loss_head", grid=(s // tm,),
        in_specs=[row, vec, row], out_specs=[row, pl.BlockSpec((8, LANES), lambda i: (0, 0)), vec],
        out_shape=[jax.ShapeDtypeStruct((s, d), F32), jax.ShapeDtypeStruct((8, LANES), F32),
                   jax.ShapeDtypeStruct((1, d), F32)],
        compiler_params=_params(1),
    )(x, g, target)


def _position():
    x, y, c = lax.axis_index("x"), lax.axis_index("y"), lax.axis_index("c")
    return x, y, c, [(1 - x, y), (x, 1 - y), (1 - x, 1 - y)]


def _row_tile(rows, cap=256):
    best = rows
    for cand in range(8, min(rows, cap) + 1, 8):
        if rows % cand == 0:
            best = cand
    return best


def _cast_into_slot(shard, me_idx, dtype):
    rows, cols = shard.shape
    tr = _row_tile(rows)

    def body(me_ref, src_ref, out_ref):
        out_ref[0] = src_ref[...].astype(dtype)

    return pl.pallas_call(
        body, name="cast_into_slot",
        grid_spec=pltpu.PrefetchScalarGridSpec(
            num_scalar_prefetch=1, grid=(rows // tr,),
            in_specs=[pl.BlockSpec((tr, cols), lambda i, me: (i, 0))],
            out_specs=pl.BlockSpec((1, tr, cols), lambda i, me: (me[0], i, 0))),
        out_shape=jax.ShapeDtypeStruct((N_SHARD, rows, cols), dtype),
        compiler_params=_params(1),
    )(me_idx, shard)


HBM_SPEC = pl.BlockSpec(memory_space=pltpu.HBM)
SEM_SPEC = pl.BlockSpec(memory_space=pltpu.SEMAPHORE)
DATAFLOW = pltpu.SideEffectType.DATAFLOW_SIDE_EFFECTING


def _exchange_copies(srcs, lands, send, recv, gather):
    x, y, c, chips = _position()
    me = 2 * x + y
    out = []
    for a in range(len(lands)):
        dst = lands[a].at[me]
        if gather and _halved(lands[a]):
            half = lands[a].shape[1] // 2
            dst = lands[a].at[me, pl.ds(c * half, half), :]
        for k, (px, py) in enumerate(chips):
            src = dst if gather else srcs[a].at[2 * px + py]
            out.append(pltpu.make_async_remote_copy(src_ref=src, dst_ref=dst, send_sem=send.at[3 * a + k],
                                                    recv_sem=recv.at[3 * a + k], device_id=(px, py, c), device_id_type=MESH))
    return out


def _halved(land):
    return land.shape[1] % 32 == 0


def _pair_fill(name, lands):
    n = len(lands)

    def body(*refs):
        src, dst = refs[:n], refs[n:2 * n]
        send, recv = refs[2 * n:]
        x, y, c, chips = _position()
        copies = []
        for a in range(n):
            half = src[a].shape[1] // 2
            for k, (px, py) in enumerate(chips):
                rows = (2 * px + py, pl.ds(c * half, half), slice(None))
                cp = pltpu.make_async_remote_copy(src_ref=src[a].at[rows], dst_ref=dst[a].at[rows], send_sem=send.at[a, k],
                                                  recv_sem=recv.at[a, k], device_id=(x, y, 1 - c), device_id_type=MESH)
                cp.start()
                copies.append(cp)
        for cp in copies:
            cp.wait()

    return pl.pallas_call(
        body, name=name, in_specs=[ANY] * n, out_specs=[ANY] * n,
        out_shape=[jax.ShapeDtypeStruct(l.shape, l.dtype) for l in lands],
        input_output_aliases={a: a for a in range(n)},
        scratch_shapes=[pltpu.SemaphoreType.DMA((n, 3)), pltpu.SemaphoreType.DMA((n, 3))],
    )(*lands)


def _exchange_start(name, srcs, lands, after):
    ns, nl, na = len(srcs), len(lands), len(after)
    gather = ns == 0

    def body(*refs):
        src_refs, land_refs = refs[:ns], refs[ns:ns + nl]
        send, recv = refs[ns + nl + na], refs[ns + nl + na + 1]
        token = refs[-1]
        for cp in _exchange_copies(src_refs, land_refs, send, recv, gather):
            cp.start()
        token[...] = jnp.zeros_like(token)

    arrays = [pltpu.with_memory_space_constraint(a, pltpu.HBM) for a in list(srcs) + list(lands)]
    outs = pl.pallas_call(
        body, name=name,
        out_shape=(pltpu.SemaphoreType.DMA((3 * nl,)), pltpu.SemaphoreType.DMA((3 * nl,)),
                   *[pltpu.HBM(a.shape, a.dtype) for a in arrays], jax.ShapeDtypeStruct((8, LANES), F32)),
        in_specs=[HBM_SPEC] * (ns + nl) + [ANY] * na,
        out_specs=(SEM_SPEC, SEM_SPEC, *[HBM_SPEC] * (ns + nl), pl.BlockSpec(memory_space=pltpu.VMEM)),
        input_output_aliases={i: 2 + i for i in range(ns + nl)},
        compiler_params=pltpu.CompilerParams(has_side_effects=DATAFLOW),
    )(*arrays, *after)
    return outs[0], outs[1], list(outs[2:2 + ns]), list(outs[2 + ns:2 + ns + nl]), outs[-1]


def _exchange_wait(name, send, recv, srcs, lands, after):
    ns, nl, na = len(srcs), len(lands), len(after)
    gather = ns == 0

    def body(*refs):
        src_refs, land_refs = refs[:ns], refs[ns:ns + nl]
        send_ref, recv_ref = refs[ns + nl], refs[ns + nl + 1]
        for cp in _exchange_copies(src_refs, land_refs, send_ref, recv_ref, gather):
            cp.wait_send()
            cp.wait_recv()

    outs = pl.pallas_call(
        body, name=name,
        out_shape=tuple(pltpu.HBM(a.shape, a.dtype) for a in list(srcs) + list(lands)),
        in_specs=[HBM_SPEC] * (ns + nl) + [SEM_SPEC, SEM_SPEC] + [ANY] * na,
        out_specs=tuple([HBM_SPEC] * (ns + nl)),
        input_output_aliases={i: i for i in range(ns + nl)},
        compiler_params=pltpu.CompilerParams(has_side_effects=DATAFLOW),
    )(*srcs, *lands, send, recv, *after)
    return list(outs[ns:])


def _pair_exchange(grads):
    n = len(grads)

    def body(*refs):
        src, dst = refs[:n], refs[n:2 * n]
        send, recv = refs[2 * n:]
        x, y, c, _ = _position()
        copies = []
        for a in range(n):
            half = src[a].shape[1] // 2
            cp = pltpu.make_async_remote_copy(
                src_ref=src[a].at[:, pl.ds((1 - c) * half, half), :], dst_ref=dst[a], send_sem=send.at[a],
                recv_sem=recv.at[a], device_id=(x, y, 1 - c), device_id_type=MESH)
            cp.start()
            copies.append(cp)
        for cp in copies:
            cp.wait()

    return pl.pallas_call(
        body, name="grad_pair_exchange", in_specs=[ANY] * n, out_specs=[ANY] * n,
        out_shape=[jax.ShapeDtypeStruct((g.shape[0], g.shape[1] // 2, g.shape[2]), g.dtype) for g in grads],
        scratch_shapes=[pltpu.SemaphoreType.DMA((n,)), pltpu.SemaphoreType.DMA((n,))],
    )(*grads)


def _pair_sum(g, got, c_idx, me_idx):
    nsh, rows, cols = g.shape
    half = rows // 2

    def body(c_ref, me_ref, g_ref, got_ref, s_ref, own_ref):
        sm = g_ref[...] + got_ref[...]
        s_ref[...] = sm.astype(CD)

        @pl.when(pl.program_id(0) == me_ref[0])
        def _():
            own_ref[...] = sm[0]

    return pl.pallas_call(
        body, name="grad_pair_sum",
        grid_spec=pltpu.PrefetchScalarGridSpec(
            num_scalar_prefetch=2, grid=(nsh,),
            in_specs=[pl.BlockSpec((1, half, cols), lambda j, c, me: (j, c[0], 0)),
                      pl.BlockSpec((1, half, cols), lambda j, c, me: (j, 0, 0))],
            out_specs=[pl.BlockSpec((1, half, cols), lambda j, c, me: (j, 0, 0)),
                       pl.BlockSpec((half, cols), lambda j, c, me: (0, 0))]),
        out_shape=[jax.ShapeDtypeStruct((nsh, half, cols), CD), jax.ShapeDtypeStruct((half, cols), F32)],
        compiler_params=_params(1),
    )(c_idx, me_idx, g, got)


def _chip_sum(own, got, me_idx):
    nsh, half, cols = got.shape

    def body(me_ref, own_ref, got_ref, out_ref):
        j = pl.program_id(0)
        term = jnp.where(j == me_ref[0], own_ref[...], got_ref[0].astype(F32))

        @pl.when(j == 0)
        def _():
            out_ref[...] = term

        @pl.when(j > 0)
        def _():
            out_ref[...] += term

    return pl.pallas_call(
        body, name="grad_chip_sum",
        grid_spec=pltpu.PrefetchScalarGridSpec(
            num_scalar_prefetch=1, grid=(nsh,),
            in_specs=[pl.BlockSpec((half, cols), lambda j, me: (0, 0)), pl.BlockSpec((1, half, cols), lambda j, me: (j, 0, 0))],
            out_specs=pl.BlockSpec((half, cols), lambda j, me: (0, 0))),
        out_shape=jax.ShapeDtypeStruct((half, cols), F32),
        compiler_params=_params(1),
    )(me_idx, own, got)


def _pair_share(halves):
    n = len(halves)

    def body(*refs):
        src, dst = refs[:n], refs[n:2 * n]
        send, recv = refs[2 * n:]
        x, y, c, _ = _position()
        copies = []
        for a in range(n):
            cp = pltpu.make_async_remote_copy(src_ref=src[a], dst_ref=dst[a], send_sem=send.at[a],
                                              recv_sem=recv.at[a], device_id=(x, y, 1 - c), device_id_type=MESH)
            cp.start()
            copies.append(cp)
        for cp in copies:
            cp.wait()

    return pl.pallas_call(
        body, name="grad_pair_share", in_specs=[ANY] * n, out_specs=[ANY] * n,
        out_shape=[jax.ShapeDtypeStruct(h.shape, h.dtype) for h in halves],
        scratch_shapes=[pltpu.SemaphoreType.DMA((n,)), pltpu.SemaphoreType.DMA((n,))],
    )(*halves)


def _small_allreduce(buf):
    rows, cols = buf.shape

    def body(src_ref, out_ref, slots, send, recv):
        x, y, c, _ = _position()
        me = 4 * x + 2 * y + c
        slots[me] = src_ref[...]
        copies = []
        k = 0
        for dx in (0, 1):
            for dy in (0, 1):
                for dc in (0, 1):
                    if (dx, dy, dc) == (0, 0, 0):
                        continue
                    peer = (jnp.where(dx, 1 - x, x), jnp.where(dy, 1 - y, y), jnp.where(dc, 1 - c, c))
                    cp = pltpu.make_async_remote_copy(src_ref=src_ref, dst_ref=slots.at[me], send_sem=send.at[k],
                                                      recv_sem=recv.at[k], device_id=peer, device_id_type=MESH)
                    cp.start()
                    copies.append(cp)
                    k += 1
        for cp in copies:
            cp.wait()
        total = slots[0]
        for dev in range(1, N_DEV):
            total = total + slots[dev]
        out_ref[...] = total

    vm = pl.BlockSpec(memory_space=pltpu.VMEM)
    return pl.pallas_call(
        body, name="small_allreduce", in_specs=[vm], out_specs=vm,
        out_shape=jax.ShapeDtypeStruct((rows, cols), F32),
        scratch_shapes=[pltpu.VMEM((N_DEV, rows, cols), F32), pltpu.SemaphoreType.DMA((N_DEV - 1,)),
                        pltpu.SemaphoreType.DMA((N_DEV - 1,))],
    )(buf)


def _adamw_math(w, gv, m, v):
    mn = ADAM_B1 * m + (1.0 - ADAM_B1) * gv
    vn = ADAM_B2 * v + (1.0 - ADAM_B2) * (gv * gv)
    m_hat = mn / (1.0 - ADAM_B1 ** ADAM_STEP)
    v_hat = vn / (1.0 - ADAM_B2 ** ADAM_STEP)
    return -ADAM_LR * (m_hat / (jnp.sqrt(v_hat) + ADAM_EPS) + ADAM_WD * w), mn, vn


def _adamw(w, g, m, v):
    rows, cols = w.shape
    tr = _row_tile(rows)

    def body(w_ref, g_ref, m_ref, v_ref, d_ref, mo_ref, vo_ref):
        d_ref[...], mo_ref[...], vo_ref[...] = _adamw_math(w_ref[...], g_ref[...], m_ref[...], v_ref[...])

    blk = pl.BlockSpec((tr, cols), lambda i: (i, 0))
    shp = jax.ShapeDtypeStruct((rows, cols), F32)
    return pl.pallas_call(
        body, name="adamw", grid=(rows // tr,), in_specs=[blk] * 4, out_specs=[blk] * 3, out_shape=[shp] * 3,
        compiler_params=_params(1),
    )(w, g, m, v)


def _adamw_halves(w, own, got, m, v, c_idx):
    rows, cols = w.shape
    tr = _row_tile(rows // 2)
    per_half = rows // 2 // tr

    def body(c_ref, w_ref, own_ref, got_ref, m_ref, v_ref, d_ref, mo_ref, vo_ref, g_ref):
        mine = (pl.program_id(0) // per_half) == c_ref[0]
        gv = jnp.where(mine, own_ref[...], got_ref[...])
        g_ref[...] = gv
        d_ref[...], mo_ref[...], vo_ref[...] = _adamw_math(w_ref[...], gv, m_ref[...], v_ref[...])

    blk = pl.BlockSpec((tr, cols), lambda i, c: (i, 0))
    hblk = pl.BlockSpec((tr, cols), lambda i, c: (i % per_half, 0))
    shp = jax.ShapeDtypeStruct((rows, cols), F32)
    return pl.pallas_call(
        body, name="adamw_halves",
        grid_spec=pltpu.PrefetchScalarGridSpec(num_scalar_prefetch=1, grid=(rows // tr,),
                                               in_specs=[blk, hblk, hblk, blk, blk], out_specs=[blk] * 4),
        out_shape=[shp] * 4, compiler_params=_params(1),
    )(c_idx, w, own, got, m, v)


def _local_step(x, target, norm_gains, q_g, k_g, ng, weights_of, grads_done):
    s = x.shape[0]
    tm = min(512, s)
    tq = min(256, s)
    g1, gm, g2, gf = norm_gains
    cos2, sin2 = _rope_tables(s)
    gq8 = jnp.tile(q_g, (1, 8))
    gk2 = jnp.tile(k_g, (1, 2))

    tn = min(256, s)
    tk = min(1024, s)
    w1 = weights_of(1, ())
    x1, a1, b1, h1 = _ffn_fwd(x, g1, w1["g1"], w1["u1"], w1["d1"], tm)
    w2 = weights_of(2, (x1,))
    lbl = w2["lbl"]
    pqkv, ph, pg, hm = _mix_in_fwd(x1, gm, w2["in"], tn)
    qe, kr, vr, vs = _qk_prep(pqkv, gq8, gk2, cos2, sin2, tm)
    oa, lse = _attn_fwd(qe, kr, vr, vs, tq)
    ob, pre, hstates = _hgrn_fwd(ph, lbl, ng)
    x2 = _mix_out_fwd(x1, oa, ob, pg, w2["a"], w2["b"], w2["o"], tm)
    w3 = weights_of(3, (x2,))
    x3, a2, b2, h2 = _ffn_fwd(x2, g2, w3["g2"], w3["u2"], w3["d2"], tm)
    dx3, loss, dgf = _loss_head(x3, gf, target, tm)

    dx2, da2, db2, f2, dg2 = _ffn_bwd(dx3, x2, g2, a2, b2, w3["g2"], w3["u2"], w3["d2"], tm)
    tok = grads_done(3, dict(g2=_dw_shared_b("dw_gate", da2, h2, tk, 1.0), u2=_dw_shared_b("dw_gate", db2, h2, tk, 1.0),
                             d2=_dw_shared_b("dw_down", f2, dx3, tk, 0.5)))

    dpg, mg, dya, dyb, doe, delta, dob = _mix_out_bwd(dx2, oa, ob, pg, w2["a"], w2["b"], w2["o"], tm, tok)
    g_o = _dw_colblocks("dw_out", mg, dx2, 1, tk).reshape(N_SHARD, D_MODEL // N_SHARD, D_MODEL)
    g_a = _dw_colblocks("dw_branch", oa, dya, N_SHARD, tk)
    g_b = _dw_colblocks("dw_branch", ob, dyb, N_SHARD, tk)
    dqe, dk, dv = _attn_bwd(qe, kr, vr, doe, delta, lse, tq)
    dqkv, dgq, dgk = _qk_prep_bwd(pqkv, dqe, dk, dv, gq8, gk2, cos2, sin2, tm)
    dhq, dhff, dhfb, dhi, dhg, dlb, dng = _hgrn_bwd(ph, pre, dob, hstates, lbl, ng)
    dps = (dqkv, dhq, dhff, dhfb, dhi, dhg, dpg)
    g_in = _dw_in(dps, hm, min(2048, s)).reshape(N_SHARD, -1, D_MODEL)
    tok = grads_done(2, {"in": g_in, "a": g_a, "b": g_b, "o": g_o})
    dx1, dgm = _mix_in_bwd(dps, w2["in"], x1, dx2, gm, tn, tok)

    dx0, da1, db1, f1, dg1 = _ffn_bwd(dx1, x, g1, a1, b1, w1["g1"], w1["u1"], w1["d1"], tm)
    grads_done(1, dict(g1=_dw_shared_b("dw_gate", da1, h1, tk, 1.0), u1=_dw_shared_b("dw_gate", db1, h1, tk, 1.0),
                       d1=_dw_shared_b("dw_down", f1, dx1, tk, 0.5)))
    small = dict(g1=dg1, gm=dgm, g2=dg2, gf=dgf, gq=dgq, gk=dgk, lb=dlb, ng=dng)
    return loss, dx0, small, lbl


GROUPS = {1: ("g1", "u1", "d1"), 2: ("in", "a", "b", "o"), 3: ("g2", "u2", "d2")}
BIG = GROUPS[1] + GROUPS[2] + GROUPS[3]
TRANSPOSED = ("g1", "u1", "in", "g2", "u2")


def _pack_rows(vectors, width):
    rows = []
    for vct in vectors:
        flat = vct.reshape(-1)
        pad = (-flat.shape[0]) % width
        rows.append(jnp.pad(flat, (0, pad)).reshape(-1, width))
    return jnp.concatenate(rows, axis=0)


def kernel(x, ffn1_norm_g, ffn1_w_gate, ffn1_w_up, ffn1_w_down, mix_norm_g, w_in, q_norm_g, k_norm_g, hgrn_lb_logits, hgrn_out_norm_g, w_branch_attn, w_branch_hgrn, w_out, ffn2_norm_g, ffn2_w_gate, ffn2_w_up, ffn2_w_down, final_norm_g, loss_target, m_ffn1_norm_g, m_ffn1_w_gate, m_ffn1_w_up, m_ffn1_w_down, m_mix_norm_g, m_w_in, m_q_norm_g, m_k_norm_g, m_hgrn_lb_logits, m_hgrn_out_norm_g, m_w_branch_attn, m_w_branch_hgrn, m_w_out, m_ffn2_norm_g, m_ffn2_w_gate, m_ffn2_w_up, m_ffn2_w_down, m_final_norm_g, v_ffn1_norm_g, v_ffn1_w_gate, v_ffn1_w_up, v_ffn1_w_down, v_mix_norm_g, v_w_in, v_q_norm_g, v_k_norm_g, v_hgrn_lb_logits, v_hgrn_out_norm_g, v_w_branch_attn, v_w_branch_hgrn, v_w_out, v_ffn2_norm_g, v_ffn2_w_gate, v_ffn2_w_up, v_ffn2_w_down, v_final_norm_g):
    xi, yi, ci = lax.axis_index("x"), lax.axis_index("y"), lax.axis_index("c")
    me = 2 * xi + yi
    c_idx = jnp.reshape(ci, (1,)).astype(jnp.int32)
    me_idx = jnp.reshape(me, (1,)).astype(jnp.int32)

    big_w = dict(g1=ffn1_w_gate[0], u1=ffn1_w_up[0], d1=ffn1_w_down[0], a=w_branch_attn[0], b=w_branch_hgrn[0],
                 o=w_out[0], g2=ffn2_w_gate[0], u2=ffn2_w_up[0], d2=ffn2_w_down[0])
    big_w["in"] = w_in[0]
    big_m = dict(g1=m_ffn1_w_gate[0], u1=m_ffn1_w_up[0], d1=m_ffn1_w_down[0], a=m_w_branch_attn[0], b=m_w_branch_hgrn[0],
                 o=m_w_out[0], g2=m_ffn2_w_gate[0], u2=m_ffn2_w_up[0], d2=m_ffn2_w_down[0])
    big_m["in"] = m_w_in[0]
    big_v = dict(g1=v_ffn1_w_gate[0], u1=v_ffn1_w_up[0], d1=v_ffn1_w_down[0], a=v_w_branch_attn[0], b=v_w_branch_hgrn[0],
                 o=v_w_out[0], g2=v_ffn2_w_gate[0], u2=v_ffn2_w_up[0], d2=v_ffn2_w_down[0])
    big_v["in"] = v_w_in[0]
    for table in (big_w, big_m, big_v):
        for n in TRANSPOSED:
            table[n] = table[n].T

    slots = {n: _cast_into_slot(big_w[n], me_idx, CD) for n in BIG}
    lbl_slot = _cast_into_slot(hgrn_lb_logits.reshape(4, LANES), me_idx, F32)
    started, token = {}, ()
    for grp in (1, 2, 3):
        lands = [slots[n] for n in GROUPS[grp]] + ([lbl_slot] if grp == 2 else [])
        send, recv, _, lands, tok = _exchange_start("gather%d_start" % grp, [], lands, token)
        started[grp], token = (send, recv, lands), (tok,)

    def weights_of(grp, after):
        send, recv, lands = started[grp]
        got = _exchange_wait("gather%d_wait" % grp, send, recv, [], lands, tuple(after) + (token if grp == 1 else ()))
        by_halves = [i for i, land in enumerate(got) if _halved(land)]
        for i, whole in zip(by_halves, _pair_fill("gather%d_fill" % grp, [got[i] for i in by_halves])):
            got[i] = whole
        w = dict(zip(GROUPS[grp], got))
        if grp == 2:
            w["in"] = w["in"].reshape(-1, D_MODEL)
            w["o"] = w["o"].reshape(D_MODEL, D_MODEL)
            w["lbl"] = jnp.transpose(got[-1], (1, 0, 2)).reshape(4, N_SHARD * LANES)
        return w

    pending = {}

    def grads_done(grp, grads):
        names = list(grads)
        glist = [grads[n] for n in names]
        got = _pair_exchange(glist)
        sums, owns = zip(*[_pair_sum(g, r, c_idx, me_idx) for g, r in zip(glist, got)])
        lands = [lax.empty(s_.shape, s_.dtype) for s_ in sums]
        send, recv, srcs, lands, tok = _exchange_start("reduce%d_start" % grp, list(sums), lands, ())
        pending[grp] = (names, send, recv, srcs, lands, owns, tok)
        return (tok,)

    def reduced_halves(grp, after):
        names, send, recv, srcs, lands, owns, _ = pending[grp]
        parts = _exchange_wait("reduce%d_wait" % grp, send, recv, srcs, lands, after)
        return names, [_chip_sum(o, p, me_idx) for o, p in zip(owns, parts)]

    loss, dx, small, lbl = _local_step(
        x[0], loss_target[0], (ffn1_norm_g, mix_norm_g, ffn2_norm_g, final_norm_g.reshape(1, -1)),
        q_norm_g, k_norm_g, hgrn_out_norm_g, weights_of, grads_done)

    dgq = small["gq"].reshape(8, HEAD_DIM).sum(axis=0)
    dgk = small["gk"].reshape(2, HEAD_DIM).sum(axis=0)
    lb_full = _hgrn_lower_bounds(lbl)
    dlog = []
    for d in (0, 1):
        t = small["lb"][d:d + 1] * lb_full[d] * (1.0 - lb_full[d])
        dlog += [t, -t]
    small_list = [small["g1"], small["gm"], small["g2"], small["gf"], small["ng"], dgq, dgk, jnp.concatenate(dlog, axis=0), loss[0, 0]]
    packed = _pack_rows(small_list, D_MODEL)
    n_rows = packed.shape[0]
    packed = jnp.pad(packed, ((0, (-n_rows) % 8), (0, 0)))
    red = _small_allreduce(packed)
    loss_out = red[n_rows - 1, 0]
    sg = dict(g1=red[0:1], gm=red[1:2], g2=red[2:3], gf=red[3], ng=red[4:5, :512], gq=red[5:6, :HEAD_DIM],
              gk=red[6:7, :HEAD_DIM])
    dlog_full = red[7:9].reshape(2, 2, 512)
    sg["lb"] = lax.dynamic_slice_in_dim(dlog_full, me * LANES, LANES, axis=2)

    small_w = dict(g1=ffn1_norm_g, gm=mix_norm_g, g2=ffn2_norm_g, gf=final_norm_g, ng=hgrn_out_norm_g, gq=q_norm_g,
                   gk=k_norm_g, lb=hgrn_lb_logits)
    small_m = dict(g1=m_ffn1_norm_g, gm=m_mix_norm_g, g2=m_ffn2_norm_g, gf=m_final_norm_g, ng=m_hgrn_out_norm_g,
                   gq=m_q_norm_g, gk=m_k_norm_g, lb=m_hgrn_lb_logits)
    small_v = dict(g1=v_ffn1_norm_g, gm=v_mix_norm_g, g2=v_ffn2_norm_g, gf=v_final_norm_g, ng=v_hgrn_out_norm_g,
                   gq=v_q_norm_g, gk=v_k_norm_g, lb=v_hgrn_lb_logits)
    small_names = ("g1", "gm", "g2", "gf", "ng", "gq", "gk", "lb")
    pack = lambda dct: _pack_rows([dct[n] for n in small_names], D_MODEL)
    pw, pgr, pm, pv = pack(small_w), pack(sg), pack(small_m), pack(small_v)
    pad8 = lambda a: jnp.pad(a, ((0, (-a.shape[0]) % 8), (0, 0)))
    sd, sm_, sv_ = _adamw(pad8(pw), pad8(pgr), pad8(pm), pad8(pv))

    def unpack(buf):
        out, r = {}, 0
        for n in small_names:
            size = small_w[n].size
            nr = -(-size // D_MODEL)
            out[n] = buf[r:r + nr].reshape(-1)[:size].reshape(small_w[n].shape)
            r += nr
        return out

    sdelta, snew_m, snew_v = unpack(sd), unpack(sm_), unpack(sv_)
    sgrad = {n: sg[n].reshape(small_w[n].shape) for n in small_names}

    bdelta, bnew_m, bnew_v, bgrad = {}, {}, {}, {}

    def update(names, halves):
        for n, own, got in zip(names, halves, _pair_share(halves)):
            res = _adamw_halves(big_w[n], own, got, big_m[n], big_v[n], c_idx)
            if n in TRANSPOSED:
                res = [r.T for r in res]
            bdelta[n], bnew_m[n], bnew_v[n], bgrad[n] = [r[None] for r in res]

    names3, halves3 = reduced_halves(3, (pending[1][-1],))
    names2, halves2 = reduced_halves(2, (halves3[0],))
    update(names3 + names2, halves3 + halves2)
    names1, halves1 = reduced_halves(1, (bdelta[names2[-1]],))
    update(names1, halves1)

    order = [("s", "g1"), ("b", "g1"), ("b", "u1"), ("b", "d1"), ("s", "gm"), ("b", "in"), ("s", "gq"), ("s", "gk"),
             ("s", "lb"), ("s", "ng"), ("b", "a"), ("b", "b"), ("b", "o"), ("s", "g2"), ("b", "g2"), ("b", "u2"),
             ("b", "d2"), ("s", "gf")]
    outs = [loss_out, dx[None]]
    for table_s, table_b in ((sgrad, bgrad), (sdelta, bdelta), (snew_m, bnew_m), (snew_v, bnew_v)):
        outs += [(table_s if kind == "s" else table_b)[n] for kind, n in order]
    return tuple(outs)
```

```python
import functools

import numpy as np
import jax
import jax.numpy as jnp
from jax import lax
from jax.experimental import pallas as pl
from jax.experimental.pallas import tpu as pltpu

F32 = jnp.float32
BF16 = jnp.bfloat16
CD = jnp.bfloat16

EPS = 1e-6
D_MODEL = 1024
HEAD_DIM = 64
GRID_W = 64
ROPE_THETA = 10000.0
CHUNK = 32
N_SHARD = 4
N_DEV = 8
VMEM_LIMIT = 56 * 1024 * 1024
LANES = 128
HG_TILE = 256

ADAM_LR = 0.001
ADAM_B1 = 0.9
ADAM_B2 = 0.999
ADAM_EPS = 1e-08
ADAM_WD = 0.01
ADAM_STEP = 10

NN = (((1,), (0,)), ((), ()))
NT = (((1,), (1,)), ((), ()))
TN = (((0,), (0,)), ((), ()))
MESH = pl.DeviceIdType.MESH
ANY = pl.BlockSpec(memory_space=pl.ANY)


def _mm(a, b, dn):
    return lax.dot_general(a.astype(CD), b.astype(CD), dn, preferred_element_type=F32)


def _split3(x):
    hi = x.astype(BF16)
    r = x - hi.astype(F32)
    mid = r.astype(BF16)
    lo = (r - mid.astype(F32)).astype(BF16)
    return hi, mid, lo


def _xdot(x, m):
    rows = x.shape[0]
    r = lax.dot_general(jnp.concatenate(_split3(x), axis=0), m, NN, preferred_element_type=F32)
    return r[:rows] + r[rows:2 * rows] + r[2 * rows:]


def _xdot_l(m, x):
    cols = x.shape[1]
    r = lax.dot_general(m, jnp.concatenate(_split3(x), axis=1), NN, preferred_element_type=F32)
    return r[:, :cols] + r[:, cols:2 * cols] + r[:, 2 * cols:]


def _params(n_grid):
    return pltpu.CompilerParams(dimension_semantics=("arbitrary",) * n_grid, vmem_limit_bytes=VMEM_LIMIT)


def _sigmoid(x):
    return jax.nn.sigmoid(x)


def _np_blocksum(n):
    i = np.arange(n)
    return (i[:, None] // HEAD_DIM == i[None, :] // HEAD_DIM).astype(np.float32)


def _np_swap32(n):
    i = np.arange(n)
    partner = np.where(i % HEAD_DIM < HEAD_DIM // 2, i + HEAD_DIM // 2, i - HEAD_DIM // 2)
    m = np.zeros((n, n), np.float32)
    m[i, partner] = 1.0
    return m


def _np_expand_q():
    m = np.zeros((512, 1024), np.float32)
    for h in range(8):
        g = h // 4
        for d in range(HEAD_DIM):
            m[64 * h + d, 128 * h + 64 * g + d] = 1.0
    return m


def _np_bcast_head():
    m = np.zeros((512, 1024), np.float32)
    for h in range(8):
        m[64 * h, 128 * h:128 * h + 128] = 1.0
    return m


def _np_swap_halves():
    m = np.zeros((128, 128), np.float32)
    i = np.arange(128)
    m[i, (i + 64) % 128] = 1.0
    return m


def _np_hgrn_cums(t, rev):
    r = np.arange(t)[:, None]
    c = np.arange(t)[None, :]
    same = (r // CHUNK) == (c // CHUNK)
    if not rev:
        cum = same & (c <= r)
        mid = same & (c % CHUNK <= CHUNK // 2 - 1)
    else:
        cum = same & (c >= r)
        mid = same & (c % CHUNK >= CHUNK // 2)
    return np.concatenate([cum, mid, same], axis=0).astype(np.float32)


def _bf(a):
    return jnp.asarray(a, dtype=BF16)


def _rope_tables(seq_len):
    rows = seq_len // GRID_W
    row = jnp.repeat(jnp.arange(rows, dtype=F32), GRID_W)
    col = jnp.tile(jnp.arange(GRID_W, dtype=F32), rows)
    n_freq = HEAD_DIM // 4
    inv = ROPE_THETA ** (-jnp.arange(n_freq, dtype=F32) / n_freq)
    ang = jnp.concatenate([row[:, None] * inv, col[:, None] * inv], axis=-1)
    cos, sin = jnp.cos(ang), jnp.sin(ang)
    c64 = jnp.concatenate([cos, cos], axis=-1)
    s64 = jnp.concatenate([-sin, sin], axis=-1)
    return jnp.tile(c64, (1, 2)), jnp.tile(s64, (1, 2))


def _ffn_fwd(x, g, wg, wu, wd, tm):
    s, d = x.shape
    nsh, fs, _ = wg.shape

    def body(x_ref, g_ref, wg_ref, wu_ref, wd_ref, xo_ref, a_ref, da_ref, b_ref, hb_ref, acc, hs):
        j = pl.program_id(1)

        @pl.when(j == 0)
        def _():
            xv = x_ref[...]
            r = lax.rsqrt(jnp.mean(xv * xv, axis=-1, keepdims=True) + EPS)
            h = (xv * r * g_ref[...]).astype(CD)
            hs[...] = h
            hb_ref[...] = h
            acc[...] = jnp.zeros_like(acc)

        h = hs[...]
        a = _mm(h, wg_ref[0], NT)
        b = _mm(h, wu_ref[0], NT)
        sg = _sigmoid(a)
        silu = a * sg
        acc[...] += _mm(silu * b, wd_ref[0], NN)
        a_ref[0] = silu.astype(CD)
        da_ref[0] = (sg * (1.0 + a * (1.0 - sg))).astype(CD)
        b_ref[0] = b.astype(CD)

        @pl.when(j == nsh - 1)
        def _():
            xo_ref[...] = x_ref[...] + 0.5 * acc[...]

    return pl.pallas_call(
        body, name="ffn_fwd", grid=(s // tm, nsh),
        in_specs=[pl.BlockSpec((tm, d), lambda i, j: (i, 0)), pl.BlockSpec((1, d), lambda i, j: (0, 0))]
        + [pl.BlockSpec((1, fs, d), lambda i, j: (j, 0, 0))] * 3,
        out_specs=[pl.BlockSpec((tm, d), lambda i, j: (i, 0))] + [pl.BlockSpec((1, tm, fs), lambda i, j: (j, i, 0))] * 3
        + [pl.BlockSpec((tm, d), lambda i, j: (i, 0))],
        out_shape=[jax.ShapeDtypeStruct((s, d), F32)] + [jax.ShapeDtypeStruct((nsh, s, fs), CD)] * 3
        + [jax.ShapeDtypeStruct((s, d), CD)],
        scratch_shapes=[pltpu.VMEM((tm, d), F32), pltpu.VMEM((tm, d), CD)],
        compiler_params=_params(2),
    )(x, g, wg, wu, wd)


def _ffn_bwd(dout, x, g, silu, dsilu, b, wg, wu, wd, tm):
    s, d = x.shape
    nsh, fs, _ = wg.shape

    def body(do_ref, x_ref, g_ref, sl_ref, ds_ref, b_ref, wg_ref, wu_ref, wd_ref,
             dx_ref, da_ref, db_ref, f_ref, dg_ref, do16_ref, dh):
        i = pl.program_id(0)
        j = pl.program_id(1)

        @pl.when(j == 0)
        def _():
            dh[...] = jnp.zeros_like(dh)
            do16_ref[...] = do_ref[...].astype(CD)

        @pl.when((i == 0) & (j == 0))
        def _():
            dg_ref[...] = jnp.zeros_like(dg_ref)

        sl = sl_ref[0].astype(F32)
        bv = b_ref[0].astype(F32)
        df = 0.5 * _mm(do_ref[...], wd_ref[0], NT)
        da = df * bv * ds_ref[0].astype(F32)
        db = df * sl
        dh[...] += _mm(da, wg_ref[0], NN) + _mm(db, wu_ref[0], NN)
        da_ref[0] = da.astype(CD)
        db_ref[0] = db.astype(CD)
        f_ref[0] = (sl * bv).astype(CD)

        @pl.when(j == nsh - 1)
        def _():
            xv = x_ref[...]
            r = lax.rsqrt(jnp.mean(xv * xv, axis=-1, keepdims=True) + EPS)
            dhv = dh[...]
            u = dhv * g_ref[...]
            dx_ref[...] = do_ref[...] + r * u - xv * (r * r * r) * jnp.mean(u * xv, axis=-1, keepdims=True)
            dg_ref[...] += jnp.sum(dhv * xv * r, axis=0, keepdims=True)

    act = pl.BlockSpec((1, tm, fs), lambda i, j: (j, i, 0))
    row = pl.BlockSpec((tm, d), lambda i, j: (i, 0))
    return pl.pallas_call(
        body, name="ffn_bwd", grid=(s // tm, nsh),
        in_specs=[row, row, pl.BlockSpec((1, d), lambda i, j: (0, 0)), act, act, act]
        + [pl.BlockSpec((1, fs, d), lambda i, j: (j, 0, 0))] * 3,
        out_specs=[row, act, act, act, pl.BlockSpec((1, d), lambda i, j: (0, 0)), row],
        out_shape=[jax.ShapeDtypeStruct((s, d), F32), jax.ShapeDtypeStruct((nsh, s, fs), CD),
                   jax.ShapeDtypeStruct((nsh, s, fs), CD), jax.ShapeDtypeStruct((nsh, s, fs), CD),
                   jax.ShapeDtypeStruct((1, d), F32), jax.ShapeDtypeStruct((s, d), CD)],
        scratch_shapes=[pltpu.VMEM((tm, d), F32)],
        compiler_params=_params(2),
    )(dout, x, g, silu, dsilu, b, wg, wu, wd)


def _tn_call(name, operands, in_specs, out_shape, out_spec, grid, acc_shape, pick, scale=1.0):
    nk = grid[-1]
    n_in = len(operands)

    def body(*refs):
        out_ref, acc = refs[n_in], refs[n_in + 1]
        k = pl.program_id(len(grid) - 1)

        @pl.when(k == 0)
        def _():
            acc[...] = jnp.zeros_like(acc)

        pick(refs[:n_in], acc)

        @pl.when(k == nk - 1)
        def _():
            res = acc[...] if scale == 1.0 else acc[...] * scale
            out_ref[...] = res.reshape(out_ref.shape)

    return pl.pallas_call(
        body, name=name, grid=grid, in_specs=in_specs, out_specs=out_spec, out_shape=out_shape,
        scratch_shapes=[pltpu.VMEM(acc_shape, F32)], compiler_params=_params(len(grid)),
    )(*operands)


def _dw_shared_b(name, a3, b, tk, scale):
    nj, s, m = a3.shape
    n = b.shape[1]

    def pick(refs, acc):
        rows = pl.ds(pl.multiple_of(pl.program_id(1) * tk, tk), tk)
        acc[...] += _mm(refs[0][0], refs[1][rows, :], TN)

    return _tn_call(name, (a3, b),
                    [pl.BlockSpec((1, tk, m), lambda j, k: (j, k, 0)), pl.BlockSpec((s, n), lambda j, k: (0, 0))],
                    jax.ShapeDtypeStruct((nj, m, n), F32), pl.BlockSpec((1, m, n), lambda j, k: (j, 0, 0)),
                    (nj, s // tk), (m, n), pick, scale)


def _dw_colblocks(name, a, b, nj, tk):
    s, m = a.shape
    n = b.shape[1] // nj

    def pick(refs, acc):
        acc[...] += _mm(refs[0][...], refs[1][...], TN)

    return _tn_call(name, (a, b),
                    [pl.BlockSpec((tk, m), lambda j, k: (k, 0)), pl.BlockSpec((tk, n), lambda j, k: (k, j))],
                    jax.ShapeDtypeStruct((nj, m, n), F32), pl.BlockSpec((1, m, n), lambda j, k: (j, 0, 0)),
                    (nj, s // tk), (m, n), pick)


DP_WIDTHS = (768, 512, 512, 512, 512, 512, 2048)
DP_CHUNK = 256


def _dp_chunk_maps():
    starts, counts, off = [], [], 0
    for w in DP_WIDTHS:
        starts.append(off // DP_CHUNK)
        counts.append(w // DP_CHUNK)
        off += w
    return starts, counts


def _dp_specs(tm, row_axis, chunk_axis):
    starts, counts = _dp_chunk_maps()
    specs = []
    for st, cnt in zip(starts, counts):
        def imap(*ids, st=st, cnt=cnt):
            return (ids[row_axis], jnp.clip(ids[chunk_axis] - st, 0, cnt - 1))
        specs.append(pl.BlockSpec((tm, DP_CHUNK), imap))
    return specs


def _dp_select(n, refs, fn):
    starts, counts = _dp_chunk_maps()
    for ref, st, cnt in zip(refs, starts, counts):
        @pl.when((n >= st) & (n < st + cnt))
        def _(ref=ref):
            fn(ref)


def _dw_in(dps, hb, tk):
    s, d = hb.shape
    n_chunks = sum(DP_WIDTHS) // DP_CHUNK

    def pick(refs, acc):
        rows = pl.ds(pl.multiple_of(pl.program_id(1) * tk, tk), tk)

        def add(ref):
            acc[...] += _mm(ref[...], refs[7][rows, :], TN)

        _dp_select(pl.program_id(0), refs[:7], add)

    return _tn_call("dw_in", (*dps, hb),
                    _dp_specs(tk, 1, 0) + [pl.BlockSpec((s, d), lambda n, k: (0, 0))],
                    jax.ShapeDtypeStruct((n_chunks * DP_CHUNK, d), F32), pl.BlockSpec((DP_CHUNK, d), lambda n, k: (n, 0)),
                    (n_chunks, s // tk), (DP_CHUNK, d), pick)


def _mix_in_fwd(x, g, w_t, tm):
    s, d = x.shape
    n_in = w_t.shape[0]

    def body(x_ref, g_ref, w_ref, qkv_ref, hg_ref, gt_ref, hb_ref):
        xv = x_ref[...]
        r = lax.rsqrt(jnp.mean(xv * xv, axis=-1, keepdims=True) + EPS)
        h = (xv * r * g_ref[...]).astype(CD)
        hb_ref[...] = h
        qkv_ref[...] = _mm(h, w_ref[0:768, :], NT)
        for c in range(5):
            hg_ref[:, 512 * c:512 * c + 512] = _mm(h, w_ref[768 + 512 * c:768 + 512 * c + 512, :], NT)
        for c in range(2):
            gt_ref[:, 1024 * c:1024 * c + 1024] = _mm(h, w_ref[3328 + 1024 * c:3328 + 1024 * c + 1024, :], NT)

    row = lambda w: pl.BlockSpec((tm, w), lambda i: (i, 0))
    return pl.pallas_call(
        body, name="mix_in_fwd", grid=(s // tm,),
        in_specs=[row(d), pl.BlockSpec((1, d), lambda i: (0, 0)), pl.BlockSpec((n_in, d), lambda i: (0, 0))],
        out_specs=[row(768), row(2560), row(2048), row(d)],
        out_shape=[jax.ShapeDtypeStruct((s, 768), F32), jax.ShapeDtypeStruct((s, 2560), F32),
                   jax.ShapeDtypeStruct((s, 2048), F32), jax.ShapeDtypeStruct((s, d), CD)],
        compiler_params=_params(1),
    )(x, g, w_t)


def _mix_in_bwd(dps, w_t, x, dres, g, tm, after=()):
    s, d = x.shape
    n_in = w_t.shape[0]

    def body(*refs):
        refs = refs[len(after):]
        dp_refs = refs[:7]
        w_ref, x_ref, dr_ref, g_ref, dx_ref, dg_ref = refs[7:]

        @pl.when(pl.program_id(0) == 0)
        def _():
            dg_ref[...] = jnp.zeros_like(dg_ref)

        dhv = jnp.zeros((tm, d), F32)
        off = 0
        for ref, width in zip(dp_refs, DP_WIDTHS):
            dhv = dhv + _mm(ref[...], w_ref[off:off + width, :], NN)
            off += width
        xv = x_ref[...]
        r = lax.rsqrt(jnp.mean(xv * xv, axis=-1, keepdims=True) + EPS)
        u = dhv * g_ref[...]
        dx_ref[...] = dr_ref[...] + r * u - xv * (r * r * r) * jnp.mean(u * xv, axis=-1, keepdims=True)
        dg_ref[...] += jnp.sum(dhv * xv * r, axis=0, keepdims=True)

    row = pl.BlockSpec((tm, d), lambda i: (i, 0))
    vec = pl.BlockSpec((1, d), lambda i: (0, 0))
    return pl.pallas_call(
        body, name="mix_in_bwd", grid=(s // tm,),
        in_specs=[ANY] * len(after) + [pl.BlockSpec((tm, w), lambda i: (i, 0)) for w in DP_WIDTHS]
        + [pl.BlockSpec((n_in, d), lambda i: (0, 0)), row, row, vec],
        out_specs=[row, vec],
        out_shape=[jax.ShapeDtypeStruct((s, d), F32), jax.ShapeDtypeStruct((1, d), F32)],
        compiler_params=_params(1),
    )(*after, *dps, w_t, x, dres, g)


def _headnorm_rope(x, gain, cos, sin, blocksum, swap):
    ss = _xdot(x * x, blocksum)
    r = lax.rsqrt(ss * (1.0 / HEAD_DIM) + EPS)
    y = x * r * gain
    return y * cos + _xdot(y, swap) * sin, r


def _headnorm_rope_bwd(dz, x, gain, cos, sin, blocksum, swap):
    ss = _xdot(x * x, blocksum)
    r = lax.rsqrt(ss * (1.0 / HEAD_DIM) + EPS)
    dy = dz * cos + _xdot(dz * sin, swap)
    u = dy * gain
    mean_ux = _xdot(u * x, blocksum) * (1.0 / HEAD_DIM)
    dx = r * u - x * (r * r * r) * mean_ux
    return dx, jnp.sum(dy * x * r, axis=0, keepdims=True)


def _qk_prep(pqkv, gq, gk, cos2, sin2, tm):
    s = pqkv.shape[0]
    bs512, sw512, eq, swh = _bf(_np_blocksum(512)), _bf(_np_swap32(512)), _bf(_np_expand_q()), _bf(_np_swap_halves())

    def body(q_ref, kv_ref, gq_ref, gk_ref, c_ref, s_ref, bs_ref, sw_ref, eq_ref, swh_ref, qe_ref, k_ref, v_ref, vs_ref):
        c2, s2 = c_ref[...], s_ref[...]
        c8, s8 = jnp.tile(c2, (1, 4)), jnp.tile(s2, (1, 4))
        bs, sw = bs_ref[...], sw_ref[...]
        zq, _ = _headnorm_rope(q_ref[...], gq_ref[...], c8, s8, bs, sw)
        qe_ref[...] = _mm(zq * (HEAD_DIM ** -0.5), eq_ref[...], NN).astype(CD)
        kv = kv_ref[...]
        zk, _ = _headnorm_rope(kv[:, :LANES], gk_ref[...], c2, s2, bs[:LANES, :LANES], sw[:LANES, :LANES])
        k_ref[...] = zk.astype(CD)
        v = kv[:, LANES:]
        v_ref[...] = v.astype(CD)
        vs_ref[...] = _mm(v, swh_ref[...], NN).astype(CD)

    full = lambda a: pl.BlockSpec(a.shape, lambda i: (0,) * a.ndim)
    tab = pl.BlockSpec((tm, LANES), lambda i: (i, 0))
    return pl.pallas_call(
        body, name="qk_prep", grid=(s // tm,),
        in_specs=[pl.BlockSpec((tm, 512), lambda i: (i, 0)), pl.BlockSpec((tm, 256), lambda i: (i, 2)),
                  full(gq), full(gk), tab, tab, full(bs512), full(sw512), full(eq), full(swh)],
        out_specs=[pl.BlockSpec((tm, 1024), lambda i: (i, 0)), tab, tab, tab],
        out_shape=[jax.ShapeDtypeStruct((s, 1024), CD)] + [jax.ShapeDtypeStruct((s, LANES), CD)] * 3,
        compiler_params=_params(1),
    )(pqkv, pqkv, gq, gk, cos2, sin2, bs512, sw512, eq, swh)


def _qk_prep_bwd(pqkv, dqe, dk, dv, gq, gk, cos2, sin2, tm):
    s = pqkv.shape[0]
    bs512, sw512, eqt = _bf(_np_blocksum(512)), _bf(_np_swap32(512)), _bf(_np_expand_q().T)

    def body(q_ref, kv_ref, dqe_ref, dk_ref, dv_ref, gq_ref, gk_ref, c_ref, s_ref, bs_ref, sw_ref, eqt_ref,
             dp_ref, dgq_ref, dgk_ref):
        @pl.when(pl.program_id(0) == 0)
        def _():
            dgq_ref[...] = jnp.zeros_like(dgq_ref)
            dgk_ref[...] = jnp.zeros_like(dgk_ref)

        c2, s2 = c_ref[...], s_ref[...]
        c8, s8 = jnp.tile(c2, (1, 4)), jnp.tile(s2, (1, 4))
        bs, sw = bs_ref[...], sw_ref[...]
        dzq = _xdot(dqe_ref[...], eqt_ref[...]) * (HEAD_DIM ** -0.5)
        dxq, dgq = _headnorm_rope_bwd(dzq, q_ref[...], gq_ref[...], c8, s8, bs, sw)
        kv = kv_ref[...]
        dxk, dgk = _headnorm_rope_bwd(dk_ref[...], kv[:, :LANES], gk_ref[...], c2, s2, bs[:LANES, :LANES], sw[:LANES, :LANES])
        dp_ref[...] = jnp.concatenate([dxq, dxk, dv_ref[...]], axis=1).astype(CD)
        dgq_ref[...] += dgq
        dgk_ref[...] += dgk

    full = lambda a: pl.BlockSpec(a.shape, lambda i: (0,) * a.ndim)
    tab = pl.BlockSpec((tm, LANES), lambda i: (i, 0))
    return pl.pallas_call(
        body, name="qk_prep_bwd", grid=(s // tm,),
        in_specs=[pl.BlockSpec((tm, 512), lambda i: (i, 0)), pl.BlockSpec((tm, 256), lambda i: (i, 2)),
                  pl.BlockSpec((tm, 1024), lambda i: (i, 0)), tab, tab, full(gq), full(gk), tab, tab,
                  full(bs512), full(sw512), full(eqt)],
        out_specs=[pl.BlockSpec((tm, 768), lambda i: (i, 0)), pl.BlockSpec((1, 512), lambda i: (0, 0)),
                   pl.BlockSpec((1, LANES), lambda i: (0, 0))],
        out_shape=[jax.ShapeDtypeStruct((s, 768), CD), jax.ShapeDtypeStruct((1, 512), F32),
                   jax.ShapeDtypeStruct((1, LANES), F32)],
        compiler_params=_params(1),
    )(pqkv, pqkv, dqe, dk, dv, gq, gk, cos2, sin2, bs512, sw512, eqt)


def _attn_fwd(qe, k, v, vs, tq):
    s = k.shape[0]

    def body(qa_ref, qb_ref, k_ref, v_ref, vs_ref, o_ref, lse_ref):
        m = pl.program_id(0)
        grp = m // 2
        kk = k_ref[...]
        outs = []
        for idx, q_ref in enumerate((qa_ref, qb_ref)):
            sc = _mm(q_ref[...], kk, NT)
            mx = jnp.max(sc, axis=-1, keepdims=True)
            e = jnp.exp(sc - mx)
            l = jnp.sum(e, axis=-1, keepdims=True)
            lse_ref[idx] = mx + jnp.log(l)
            p = e * (1.0 / l)
            vsel = jnp.where(grp != idx, vs_ref[...], v_ref[...])
            outs.append(_mm(p, vsel, NN))
        lane = lax.broadcasted_iota(jnp.int32, (1, LANES), 1)
        o_ref[...] = jnp.where(lane < HEAD_DIM, outs[0], outs[1])

    kv = pl.BlockSpec((s, LANES), lambda m, i: (0, 0))
    return pl.pallas_call(
        body, name="attn_fwd", grid=(4, s // tq),
        in_specs=[pl.BlockSpec((tq, LANES), lambda m, i: (i, 2 * m)), pl.BlockSpec((tq, LANES), lambda m, i: (i, 2 * m + 1)),
                  kv, kv, kv],
        out_specs=[pl.BlockSpec((tq, LANES), lambda m, i: (i, m)), pl.BlockSpec((2, tq, 1), lambda m, i: (m, i, 0))],
        out_shape=[jax.ShapeDtypeStruct((s, 512), F32), jax.ShapeDtypeStruct((8, s, 1), F32)],
        compiler_params=_params(2),
    )(qe, qe, k, v, vs)


def _attn_bwd(qe, k, v, doe, delta, lse, tq):
    s = k.shape[0]

    def body(q_ref, k_ref, v_ref, do_ref, dl_ref, lse_ref, dq_ref, dk_ref, dv_ref):
        @pl.when((pl.program_id(0) == 0) & (pl.program_id(1) == 0))
        def _():
            dk_ref[...] = jnp.zeros_like(dk_ref)
            dv_ref[...] = jnp.zeros_like(dv_ref)

        q, kk, do = q_ref[...], k_ref[...], do_ref[...]
        p = jnp.exp(_mm(q, kk, NT) - lse_ref[0])
        dp = _mm(do, v_ref[...], NT)
        ds = p * (dp - jnp.max(dl_ref[...], axis=-1, keepdims=True))
        dq_ref[...] = _mm(ds, kk, NN)
        dk_ref[...] += _mm(ds, q, TN)
        dv_ref[...] += _mm(p, do, TN)

    kv = pl.BlockSpec((s, LANES), lambda h, i: (0, 0))
    blk = pl.BlockSpec((tq, LANES), lambda h, i: (i, h))
    return pl.pallas_call(
        body, name="attn_bwd", grid=(8, s // tq),
        in_specs=[blk, kv, kv, blk, blk, pl.BlockSpec((1, tq, 1), lambda h, i: (h, i, 0))],
        out_specs=[blk, kv, kv],
        out_shape=[jax.ShapeDtypeStruct((s, 1024), F32), jax.ShapeDtypeStruct((s, LANES), F32),
                   jax.ShapeDtypeStruct((s, LANES), F32)],
        compiler_params=_params(2),
    )(qe, k, v, doe, delta, lse)


@jax.custom_vjp
def _mm_nn(a, b):
    return _mm(a, b, NN)


_mm_nn.defvjp(lambda a, b: (_mm(a, b, NN), (a, b)),
              lambda res, g: (_mm(g, res[1], NT), _mm(res[0], g, TN)))


@jax.custom_vjp
def _mm_nt(a, b):
    return _mm(a, b, NT)


_mm_nt.defvjp(lambda a, b: (_mm(a, b, NT), (a, b)),
              lambda res, g: (_mm(g, res[1], NN), _mm(g, res[0], TN)))


@jax.custom_vjp
def _mm_tn(a, b):
    return _mm(a, b, TN)


_mm_tn.defvjp(lambda a, b: (_mm(a, b, TN), (a, b)),
              lambda res, g: (_mm(res[1], g, NT), _mm(res[0], g, NN)))


@jax.custom_vjp
def _cmm(m, mt, x):
    return _xdot_l(m, x)


_cmm.defvjp(lambda m, mt, x: (_xdot_l(m, x), (m, mt)),
            lambda res, g: (jnp.zeros_like(res[0]), jnp.zeros_like(res[1]), _xdot_l(res[1], g)))


def _hgrn_masks(t, rev):
    n_ch = t // CHUNK
    r = jnp.bitwise_and(lax.broadcasted_iota(jnp.int32, (2 * t, t), 0), t - 1)
    c = lax.broadcasted_iota(jnp.int32, (2 * t, t), 1)
    same = jnp.right_shift(r, 5) == jnp.right_shift(c, 5)
    tri2 = same & ((c >= r) if rev else (c <= r))
    pr = lax.broadcasted_iota(jnp.int32, (LANES, LANES), 0)
    pc = lax.broadcasted_iota(jnp.int32, (LANES, LANES), 1)
    diag = jnp.right_shift(pr, 6) == jnp.right_shift(pc, 6)
    qr = lax.broadcasted_iota(jnp.int32, (t, n_ch * LANES), 0)
    qc = lax.broadcasted_iota(jnp.int32, (t, n_ch * LANES), 1)
    rows_chunk = jnp.right_shift(qc, 7) == jnp.right_shift(qr, 5)
    vr = lax.broadcasted_iota(jnp.int32, (n_ch * LANES, t), 0)
    vc = lax.broadcasted_iota(jnp.int32, (n_ch * LANES, t), 1)
    cols_chunk = jnp.right_shift(vr, 7) == jnp.right_shift(vc, 5)
    return dict(tri2=tri2, diag=diag, rows_chunk=rows_chunk, cols_chunk=cols_chunk)


def _hgrn_gates(xf, lb):
    f = lb + (1.0 - lb) * _sigmoid(xf)
    return 1.0 - f, jnp.log(f)


def _hgrn_dir(xq, xf, v, lb, state, cm, cmt, mk, rev):
    t = xq.shape[0]
    n_ch = t // CHUNK
    lo = lax.broadcasted_iota(jnp.int32, (1, LANES), 1) < HEAD_DIM
    q = xq * _sigmoid(xq)
    k, lf = _hgrn_gates(xf, lb)
    cs = _cmm(cm, cmt, lf)
    b, bm, bl = cs[:t], cs[t:2 * t], cs[2 * t:]
    qd = q * jnp.exp(b - bm)
    kd = k * jnp.exp(bm - b)
    kc = k * jnp.exp(bl - b)
    qe = q * jnp.exp(b)
    qd2 = jnp.concatenate([jnp.where(lo, qd, 0.0), jnp.where(lo, 0.0, qd)], axis=0)
    o2 = _mm_nn(jnp.where(mk["tri2"], _mm_nt(qd2, kd), 0.0), v)
    o = jnp.where(lo, o2[:t], o2[t:])
    vexp = jnp.where(mk["cols_chunk"], jnp.concatenate([jnp.transpose(v)] * n_ch, axis=0), 0.0)
    adds = _mm_nn(vexp, kc)
    dec = jnp.exp(bl)
    entering = [None] * n_ch
    for c in (range(n_ch - 1, -1, -1) if rev else range(n_ch)):
        entering[c] = state
        d = jnp.concatenate([dec[c * CHUNK:(c + 1) * CHUNK]] * (LANES // CHUNK), axis=0)
        state = d * state + jnp.where(mk["diag"], adds[c * LANES:(c + 1) * LANES], 0.0)
    qexp = jnp.where(mk["rows_chunk"], jnp.concatenate([qe] * n_ch, axis=1), 0.0)
    return o + _mm_nt(qexp, jnp.concatenate(entering, axis=1)), state


def _hgrn_lower_bounds(l):
    out = []
    for d in (0, 1):
        l0, l1 = l[2 * d:2 * d + 1, :], l[2 * d + 1:2 * d + 2, :]
        mx = jnp.maximum(l0, l1)
        e0, e1 = jnp.exp(l0 - mx), jnp.exp(l1 - mx)
        out.append(e0 / (e0 + e1))
    return out


def _hgrn_consts(t):
    cf, cb = _np_hgrn_cums(t, False), _np_hgrn_cums(t, True)
    return (_bf(cf), _bf(cf.T), _bf(cb), _bf(cb.T), _bf(_np_blocksum(LANES)))


def _hgrn_fwd(ph, lbl, ng):
    s = ph.shape[0]
    t = min(HG_TILE, s)
    nt = s // t
    consts = _hgrn_consts(t)

    def body(xq_ref, xff_ref, xfb_ref, xi_ref, xg_ref, lbl_ref, ng_ref, cf_ref, cft_ref, cb_ref, cbt_ref, bs_ref,
             o_ref, pre_ref, st_ref, acc):
        lbf, lbb = _hgrn_lower_bounds(lbl_ref)
        mk_f, mk_b = _hgrn_masks(t, False), _hgrn_masks(t, True)
        zero = jnp.zeros((LANES, LANES), F32)

        def rows_of(i):
            return pl.ds(pl.multiple_of(i * t, t), t)

        acc[...] = jnp.zeros_like(acc)

        def step(i, states):
            tb = nt - 1 - i
            rf, rb = rows_of(i), rows_of(tb)
            st_ref[0, 0, i] = states[0]
            st_ref[0, 1, tb] = states[1]
            of, sf = _hgrn_dir(xq_ref[rf, :], xff_ref[rf, :], xi_ref[rf, :], lbf, states[0],
                               cf_ref[...], cft_ref[...], mk_f, False)
            ob, sb = _hgrn_dir(xq_ref[rb, :], xfb_ref[rb, :], xi_ref[rb, :], lbb, states[1],
                               cb_ref[...], cbt_ref[...], mk_b, True)
            acc[rf, :] += of
            acc[rb, :] += ob
            return sf, sb

        lax.fori_loop(0, nt, step, (zero, zero))

        def step_n(i, carry):
            rows = rows_of(i)
            o = acc[rows, :]
            ss = _xdot(o * o, bs_ref[...])
            r = lax.rsqrt(ss * (1.0 / HEAD_DIM) + EPS)
            xg = xg_ref[rows, :]
            pre_ref[rows, :] = o
            o_ref[rows, :] = (o * r * ng_ref[...]) * (xg * _sigmoid(xg))
            return carry

        lax.fori_loop(0, nt, step_n, 0)

    col = lambda off: pl.BlockSpec((s, LANES), lambda m: (0, off + m))
    full = lambda a: pl.BlockSpec(a.shape, lambda m: (0,) * a.ndim)
    return pl.pallas_call(
        body, name="hgrn_fwd", grid=(4,),
        in_specs=[col(0), col(4), col(8), col(12), col(16), pl.BlockSpec((4, LANES), lambda m: (0, m)),
                  pl.BlockSpec((1, LANES), lambda m: (0, m))] + [full(c) for c in consts],
        out_specs=[col(0), col(0), pl.BlockSpec((1, 2, nt, LANES, LANES), lambda m: (m, 0, 0, 0, 0))],
        out_shape=[jax.ShapeDtypeStruct((s, 512), F32), jax.ShapeDtypeStruct((s, 512), F32),
                   jax.ShapeDtypeStruct((4, 2, nt, LANES, LANES), F32)],
        scratch_shapes=[pltpu.VMEM((s, LANES), F32)],
        compiler_params=_params(1),
    )(ph, ph, ph, ph, ph, lbl, ng, *consts)


def _hgrn_bwd(ph, pre, dout, states, lbl, ng):
    s = ph.shape[0]
    t = min(HG_TILE, s)
    nt = s // t
    consts = _hgrn_consts(t)

    def body(xq_ref, xff_ref, xfb_ref, xi_ref, xg_ref, pre_ref, do_ref, st_ref, lbl_ref, ng_ref,
             cf_ref, cft_ref, cb_ref, cbt_ref, bs_ref,
             dq_ref, dff_ref, dfb_ref, di_ref, dg_ref, dlb_ref, dng_ref, dpre, dq_acc, dv_acc):
        lbf, lbb = _hgrn_lower_bounds(lbl_ref)
        mk_f, mk_b = _hgrn_masks(t, False), _hgrn_masks(t, True)
        zero = jnp.zeros((LANES, LANES), F32)
        zrow = jnp.zeros((1, LANES), F32)

        def rows_of(i):
            return pl.ds(pl.multiple_of(i * t, t), t)

        def step_n(i, dng):
            rows = rows_of(i)
            o, xg, do = pre_ref[rows, :], xg_ref[rows, :], do_ref[rows, :]
            bs = bs_ref[...]
            r = lax.rsqrt(_xdot(o * o, bs) * (1.0 / HEAD_DIM) + EPS)
            sg = _sigmoid(xg)
            gate = xg * sg
            don = do * gate
            dg_ref[rows, :] = (do * (o * r * ng_ref[...]) * (sg * (1.0 + xg * (1.0 - sg)))).astype(CD)
            u = don * ng_ref[...]
            dpre[rows, :] = r * u - o * (r * r * r) * (_xdot(u * o, bs) * (1.0 / HEAD_DIM))
            return dng + jnp.sum(don * o * r, axis=0, keepdims=True)

        dng_ref[...] = lax.fori_loop(0, nt, step_n, zrow)

        dq_acc[...] = jnp.zeros_like(dq_acc)
        dv_acc[...] = jnp.zeros_like(dv_acc)

        def grad_tile(ti, xf_ref, df_ref, lb, cm, cmt, mk, rev, st, dstate):
            rows = rows_of(ti)
            fn = lambda xq, xf, v, lbv, s_in: _hgrn_dir(xq, xf, v, lbv, s_in, cm, cmt, mk, rev)
            _, vjp = jax.vjp(fn, xq_ref[rows, :], xf_ref[rows, :], xi_ref[rows, :], lb, st)
            dxq, dxf, dv, dlb_t, dstate = vjp((dpre[rows, :], dstate))
            df_ref[rows, :] = dxf.astype(CD)
            dq_acc[rows, :] += dxq
            dv_acc[rows, :] += dv
            return dstate, dlb_t

        def step_g(i, carry):
            dsf, dsb, dlbf, dlbb = carry
            tf, tb = nt - 1 - i, i
            dsf, gf = grad_tile(tf, xff_ref, dff_ref, lbf, cf_ref[...], cft_ref[...], mk_f, False, st_ref[0, 0, tf], dsf)
            dsb, gb = grad_tile(tb, xfb_ref, dfb_ref, lbb, cb_ref[...], cbt_ref[...], mk_b, True, st_ref[0, 1, tb], dsb)
            return dsf, dsb, dlbf + gf, dlbb + gb

        _, _, dlbf, dlbb = lax.fori_loop(0, nt, step_g, (zero, zero, zrow, zrow))
        dlb_ref[0:1, :] = dlbf
        dlb_ref[1:2, :] = dlbb
        dq_ref[...] = dq_acc[...].astype(CD)
        di_ref[...] = dv_acc[...].astype(CD)

    col = lambda off: pl.BlockSpec((s, LANES), lambda m: (0, off + m))
    full = lambda a: pl.BlockSpec(a.shape, lambda m: (0,) * a.ndim)
    stream = jax.ShapeDtypeStruct((s, 512), CD)
    return pl.pallas_call(
        body, name="hgrn_bwd", grid=(4,),
        in_specs=[col(0), col(4), col(8), col(12), col(16), col(0), col(0),
                  pl.BlockSpec((1, 2, nt, LANES, LANES), lambda m: (m, 0, 0, 0, 0)),
                  pl.BlockSpec((4, LANES), lambda m: (0, m)),
                  pl.BlockSpec((1, LANES), lambda m: (0, m))] + [full(c) for c in consts],
        out_specs=[col(0)] * 5 + [pl.BlockSpec((2, LANES), lambda m: (0, m)), pl.BlockSpec((1, LANES), lambda m: (0, m))],
        out_shape=[stream] * 5 + [jax.ShapeDtypeStruct((2, 512), F32), jax.ShapeDtypeStruct((1, 512), F32)],
        scratch_shapes=[pltpu.VMEM((s, LANES), F32), pltpu.VMEM((s, LANES), F32), pltpu.VMEM((s, LANES), F32)],
        compiler_params=_params(1),
    )(ph, ph, ph, ph, ph, pre, dout, states, lbl, ng, *consts)


def _branch_out(o, w4):
    return jnp.concatenate([_mm(o, w4[j], NN) for j in range(N_SHARD)], axis=1)


def _mix_out_fwd(x, oa, ob, pg, wa, wb, wo, tm):
    s, d = x.shape

    def body(x_ref, oa_ref, ob_ref, ga_ref, gb_ref, wa_ref, wb_ref, wo_ref, xo_ref):
        ya = _branch_out(oa_ref[...], wa_ref)
        yb = _branch_out(ob_ref[...], wb_ref)
        merged = _sigmoid(ga_ref[...]) * ya + _sigmoid(gb_ref[...]) * yb
        xo_ref[...] = x_ref[...] + _mm(merged, wo_ref[...], NN)

    row = pl.BlockSpec((tm, d), lambda i: (i, 0))
    half = pl.BlockSpec((tm, 512), lambda i: (i, 0))
    full = lambda a: pl.BlockSpec(a.shape, lambda i: (0,) * a.ndim)
    return pl.pallas_call(
        body, name="mix_out_fwd", grid=(s // tm,),
        in_specs=[row, half, half, row, pl.BlockSpec((tm, d), lambda i: (i, 1)), full(wa), full(wb), full(wo)],
        out_specs=row, out_shape=jax.ShapeDtypeStruct((s, d), F32),
        compiler_params=_params(1),
    )(x, oa, ob, pg, pg, wa, wb, wo)


def _mix_out_bwd(dx, oa, ob, pg, wa, wb, wo, tm, after=()):
    s, d = dx.shape
    bs512, eq, ebc = _bf(_np_blocksum(512)), _bf(_np_expand_q()), _bf(_np_bcast_head())

    def body(*refs):
        (dx_ref, oa_ref, ob_ref, ga_ref, gb_ref, wa_ref, wb_ref, wo_ref, bs_ref, eq_ref, ebc_ref,
         dpg_ref, mg_ref, dya_ref, dyb_ref, doe_ref, dl_ref, dob_ref) = refs[len(after):]
        oa = oa_ref[...]
        ya = _branch_out(oa, wa_ref)
        yb = _branch_out(ob_ref[...], wb_ref)
        sa, sb = _sigmoid(ga_ref[...]), _sigmoid(gb_ref[...])
        mg_ref[...] = (sa * ya + sb * yb).astype(CD)
        dm = _mm(dx_ref[...], wo_ref[...], NT)
        dpg_ref[...] = jnp.concatenate([dm * ya * sa * (1.0 - sa), dm * yb * sb * (1.0 - sb)], axis=1).astype(CD)
        dya, dyb = dm * sa, dm * sb
        dya_ref[...] = dya.astype(CD)
        dyb_ref[...] = dyb.astype(CD)
        doa = jnp.zeros(oa.shape, F32)
        dob = jnp.zeros(oa.shape, F32)
        for j in range(N_SHARD):
            doa = doa + _mm(dya[:, 256 * j:256 * j + 256], wa_ref[j], NT)
            dob = dob + _mm(dyb[:, 256 * j:256 * j + 256], wb_ref[j], NT)
        dob_ref[...] = dob
        doe_ref[...] = _mm(doa, eq_ref[...], NN).astype(CD)
        dl_ref[...] = _xdot(_xdot(doa * oa, bs_ref[...]), ebc_ref[...])

    row = pl.BlockSpec((tm, d), lambda i: (i, 0))
    half = pl.BlockSpec((tm, 512), lambda i: (i, 0))
    full = lambda a: pl.BlockSpec(a.shape, lambda i: (0,) * a.ndim)
    wide = jax.ShapeDtypeStruct((s, d), CD)
    return pl.pallas_call(
        body, name="mix_out_bwd", grid=(s // tm,),
        in_specs=[ANY] * len(after) + [row, half, half, row, pl.BlockSpec((tm, d), lambda i: (i, 1)), full(wa), full(wb),
                                       full(wo), full(bs512), full(eq), full(ebc)],
        out_specs=[pl.BlockSpec((tm, 2048), lambda i: (i, 0)), row, row, row, row, row, half],
        out_shape=[jax.ShapeDtypeStruct((s, 2048), CD), wide, wide, wide, wide, jax.ShapeDtypeStruct((s, d), F32),
                   jax.ShapeDtypeStruct((s, 512), F32)],
        compiler_params=_params(1),
    )(*after, dx, oa, ob, pg, pg, wa, wb, wo, bs512, eq, ebc)


def _loss_head(x, g, target, tm):
    s, d = x.shape

    def body(x_ref, g_ref, t_ref, dx_ref, loss_ref, dg_ref):
        @pl.when(pl.program_id(0) == 0)
        def _():
            loss_ref[...] = jnp.zeros_like(loss_ref)
            dg_ref[...] = jnp.zeros_like(dg_ref)

        xv = x_ref[...]
        r = lax.rsqrt(jnp.mean(xv * xv, axis=-1, keepdims=True) + EPS)
        err = xv * r * g_ref[...] - t_ref[...]
        loss_ref[...] += 0.5 * jnp.sum(jnp.mean(err * err, axis=-1, keepdims=True))
        dy = err * (1.0 / d)
        u = dy * g_ref[...]
        dx_ref[...] = r * u - xv * (r * r * r) * jnp.mean(u * xv, axis=-1, keepdims=True)
        dg_ref[...] += jnp.sum(dy * xv * r, axis=0, keepdims=True)

    row = pl.BlockSpec((tm, d), lambda i: (i, 0))
    vec = pl.BlockSpec((1, d), lambda i: (0, 0))
    return pl.pallas_call(
        body, name="loss_head", grid=(s // tm,),
        in_specs=[row, vec, row], out_specs=[row, pl.BlockSpec((8, LANES), lambda i: (0, 0)), vec],
        out_shape=[jax.ShapeDtypeStruct((s, d), F32), jax.ShapeDtypeStruct((8, LANES), F32),
                   jax.ShapeDtypeStruct((1, d), F32)],
        compiler_params=_params(1),
    )(x, g, target)


def _position():
    x, y, c = lax.axis_index("x"), lax.axis_index("y"), lax.axis_index("c")
    return x, y, c, [(1 - x, y), (x, 1 - y), (1 - x, 1 - y)]


def _row_tile(rows, cap=256):
    best = rows
    for cand in range(8, min(rows, cap) + 1, 8):
        if rows % cand == 0:
            best = cand
    return best


def _cast_into_slot(shard, me_idx, dtype):
    rows, cols = shard.shape
    tr = _row_tile(rows)

    def body(me_ref, src_ref, out_ref):
        out_ref[0] = src_ref[...].astype(dtype)

    return pl.pallas_call(
        body, name="cast_into_slot",
        grid_spec=pltpu.PrefetchScalarGridSpec(
            num_scalar_prefetch=1, grid=(rows // tr,),
            in_specs=[pl.BlockSpec((tr, cols), lambda i, me: (i, 0))],
            out_specs=pl.BlockSpec((1, tr, cols), lambda i, me: (me[0], i, 0))),
        out_shape=jax.ShapeDtypeStruct((N_SHARD, rows, cols), dtype),
        compiler_params=_params(1),
    )(me_idx, shard)


HBM_SPEC = pl.BlockSpec(memory_space=pltpu.HBM)
SEM_SPEC = pl.BlockSpec(memory_space=pltpu.SEMAPHORE)
DATAFLOW = pltpu.SideEffectType.DATAFLOW_SIDE_EFFECTING


def _exchange_copies(srcs, lands, send, recv, gather):
    x, y, c, chips = _position()
    me = 2 * x + y
    out = []
    for a in range(len(lands)):
        dst = lands[a].at[me]
        if gather and _halved(lands[a]):
            half = lands[a].shape[1] // 2
            dst = lands[a].at[me, pl.ds(c * half, half), :]
        for k, (px, py) in enumerate(chips):
            src = dst if gather else srcs[a].at[2 * px + py]
            out.append(pltpu.make_async_remote_copy(src_ref=src, dst_ref=dst, send_sem=send.at[3 * a + k],
                                                    recv_sem=recv.at[3 * a + k], device_id=(px, py, c), device_id_type=MESH))
    return out


def _halved(land):
    return land.shape[1] % 32 == 0


def _pair_fill(name, lands):
    n = len(lands)

    def body(*refs):
        src, dst = refs[:n], refs[n:2 * n]
        send, recv = refs[2 * n:]
        x, y, c, chips = _position()
        copies = []
        for a in range(n):
            half = src[a].shape[1] // 2
            for k, (px, py) in enumerate(chips):
                rows = (2 * px + py, pl.ds(c * half, half), slice(None))
                cp = pltpu.make_async_remote_copy(src_ref=src[a].at[rows], dst_ref=dst[a].at[rows], send_sem=send.at[a, k],
                                                  recv_sem=recv.at[a, k], device_id=(x, y, 1 - c), device_id_type=MESH)
                cp.start()
                copies.append(cp)
        for cp in copies:
            cp.wait()

    return pl.pallas_call(
        body, name=name, in_specs=[ANY] * n, out_specs=[ANY] * n,
        out_shape=[jax.ShapeDtypeStruct(l.shape, l.dtype) for l in lands],
        input_output_aliases={a: a for a in range(n)},
        scratch_shapes=[pltpu.SemaphoreType.DMA((n, 3)), pltpu.SemaphoreType.DMA((n, 3))],
    )(*lands)


def _exchange_start(name, srcs, lands, after):
    ns, nl, na = len(srcs), len(lands), len(after)
    gather = ns == 0

    def body(*refs):
        src_refs, land_refs = refs[:ns], refs[ns:ns + nl]
        send, recv = refs[ns + nl + na], refs[ns + nl + na + 1]
        token = refs[-1]
        for cp in _exchange_copies(src_refs, land_refs, send, recv, gather):
            cp.start()
        token[...] = jnp.zeros_like(token)

    arrays = [pltpu.with_memory_space_constraint(a, pltpu.HBM) for a in list(srcs) + list(lands)]
    outs = pl.pallas_call(
        body, name=name,
        out_shape=(pltpu.SemaphoreType.DMA((3 * nl,)), pltpu.SemaphoreType.DMA((3 * nl,)),
                   *[pltpu.HBM(a.shape, a.dtype) for a in arrays], jax.ShapeDtypeStruct((8, LANES), F32)),
        in_specs=[HBM_SPEC] * (ns + nl) + [ANY] * na,
        out_specs=(SEM_SPEC, SEM_SPEC, *[HBM_SPEC] * (ns + nl), pl.BlockSpec(memory_space=pltpu.VMEM)),
        input_output_aliases={i: 2 + i for i in range(ns + nl)},
        compiler_params=pltpu.CompilerParams(has_side_effects=DATAFLOW),
    )(*arrays, *after)
    return outs[0], outs[1], list(outs[2:2 + ns]), list(outs[2 + ns:2 + ns + nl]), outs[-1]


def _exchange_wait(name, send, recv, srcs, lands, after):
    ns, nl, na = len(srcs), len(lands), len(after)
    gather = ns == 0

    def body(*refs):
        src_refs, land_refs = refs[:ns], refs[ns:ns + nl]
        send_ref, recv_ref = refs[ns + nl], refs[ns + nl + 1]
        for cp in _exchange_copies(src_refs, land_refs, send_ref, recv_ref, gather):
            cp.wait_send()
            cp.wait_recv()

    outs = pl.pallas_call(
        body, name=name,
        out_shape=tuple(pltpu.HBM(a.shape, a.dtype) for a in list(srcs) + list(lands)),
        in_specs=[HBM_SPEC] * (ns + nl) + [SEM_SPEC, SEM_SPEC] + [ANY] * na,
        out_specs=tuple([HBM_SPEC] * (ns + nl)),
        input_output_aliases={i: i for i in range(ns + nl)},
        compiler_params=pltpu.CompilerParams(has_side_effects=DATAFLOW),
    )(*srcs, *lands, send, recv, *after)
    return list(outs[ns:])


def _pair_exchange(grads):
    n = len(grads)

    def body(*refs):
        src, dst = refs[:n], refs[n:2 * n]
        send, recv = refs[2 * n:]
        x, y, c, _ = _position()
        copies = []
        for a in range(n):
            half = src[a].shape[1] // 2
            cp = pltpu.make_async_remote_copy(
                src_ref=src[a].at[:, pl.ds((1 - c) * half, half), :], dst_ref=dst[a], send_sem=send.at[a],
                recv_sem=recv.at[a], device_id=(x, y, 1 - c), device_id_type=MESH)
            cp.start()
            copies.append(cp)
        for cp in copies:
            cp.wait()

    return pl.pallas_call(
        body, name="grad_pair_exchange", in_specs=[ANY] * n, out_specs=[ANY] * n,
        out_shape=[jax.ShapeDtypeStruct((g.shape[0], g.shape[1] // 2, g.shape[2]), g.dtype) for g in grads],
        scratch_shapes=[pltpu.SemaphoreType.DMA((n,)), pltpu.SemaphoreType.DMA((n,))],
    )(*grads)


def _pair_sum(g, got, c_idx, me_idx):
    nsh, rows, cols = g.shape
    half = rows // 2

    def body(c_ref, me_ref, g_ref, got_ref, s_ref, own_ref):
        sm = g_ref[...] + got_ref[...]
        s_ref[...] = sm.astype(CD)

        @pl.when(pl.program_id(0) == me_ref[0])
        def _():
            own_ref[...] = sm[0]

    return pl.pallas_call(
        body, name="grad_pair_sum",
        grid_spec=pltpu.PrefetchScalarGridSpec(
            num_scalar_prefetch=2, grid=(nsh,),
            in_specs=[pl.BlockSpec((1, half, cols), lambda j, c, me: (j, c[0], 0)),
                      pl.BlockSpec((1, half, cols), lambda j, c, me: (j, 0, 0))],
            out_specs=[pl.BlockSpec((1, half, cols), lambda j, c, me: (j, 0, 0)),
                       pl.BlockSpec((half, cols), lambda j, c, me: (0, 0))]),
        out_shape=[jax.ShapeDtypeStruct((nsh, half, cols), CD), jax.ShapeDtypeStruct((half, cols), F32)],
        compiler_params=_params(1),
    )(c_idx, me_idx, g, got)


def _chip_sum(own, got, me_idx):
    nsh, half, cols = got.shape

    def body(me_ref, own_ref, got_ref, out_ref):
        j = pl.program_id(0)
        term = jnp.where(j == me_ref[0], own_ref[...], got_ref[0].astype(F32))

        @pl.when(j == 0)
        def _():
            out_ref[...] = term

        @pl.when(j > 0)
        def _():
            out_ref[...] += term

    return pl.pallas_call(
        body, name="grad_chip_sum",
        grid_spec=pltpu.PrefetchScalarGridSpec(
            num_scalar_prefetch=1, grid=(nsh,),
            in_specs=[pl.BlockSpec((half, cols), lambda j, me: (0, 0)), pl.BlockSpec((1, half, cols), lambda j, me: (j, 0, 0))],
            out_specs=pl.BlockSpec((half, cols), lambda j, me: (0, 0))),
        out_shape=jax.ShapeDtypeStruct((half, cols), F32),
        compiler_params=_params(1),
    )(me_idx, own, got)


def _pair_share(halves):
    n = len(halves)

    def body(*refs):
        src, dst = refs[:n], refs[n:2 * n]
        send, recv = refs[2 * n:]
        x, y, c, _ = _position()
        copies = []
        for a in range(n):
            cp = pltpu.make_async_remote_copy(src_ref=src[a], dst_ref=dst[a], send_sem=send.at[a],
                                              recv_sem=recv.at[a], device_id=(x, y, 1 - c), device_id_type=MESH)
            cp.start()
            copies.append(cp)
        for cp in copies:
            cp.wait()

    return pl.pallas_call(
        body, name="grad_pair_share", in_specs=[ANY] * n, out_specs=[ANY] * n,
        out_shape=[jax.ShapeDtypeStruct(h.shape, h.dtype) for h in halves],
        scratch_shapes=[pltpu.SemaphoreType.DMA((n,)), pltpu.SemaphoreType.DMA((n,))],
    )(*halves)


def _small_allreduce(buf):
    rows, cols = buf.shape

    def body(src_ref, out_ref, slots, send, recv):
        x, y, c, _ = _position()
        me = 4 * x + 2 * y + c
        slots[me] = src_ref[...]
        copies = []
        k = 0
        for dx in (0, 1):
            for dy in (0, 1):
                for dc in (0, 1):
                    if (dx, dy, dc) == (0, 0, 0):
                        continue
                    peer = (jnp.where(dx, 1 - x, x), jnp.where(dy, 1 - y, y), jnp.where(dc, 1 - c, c))
                    cp = pltpu.make_async_remote_copy(src_ref=src_ref, dst_ref=slots.at[me], send_sem=send.at[k],
                                                      recv_sem=recv.at[k], device_id=peer, device_id_type=MESH)
                    cp.start()
                    copies.append(cp)
                    k += 1
        for cp in copies:
            cp.wait()
        total = slots[0]
        for dev in range(1, N_DEV):
            total = total + slots[dev]
        out_ref[...] = total

    vm = pl.BlockSpec(memory_space=pltpu.VMEM)
    return pl.pallas_call(
        body, name="small_allreduce", in_specs=[vm], out_specs=vm,
        out_shape=jax.ShapeDtypeStruct((rows, cols), F32),
        scratch_shapes=[pltpu.VMEM((N_DEV, rows, cols), F32), pltpu.SemaphoreType.DMA((N_DEV - 1,)),
                        pltpu.SemaphoreType.DMA((N_DEV - 1,))],
    )(buf)


def _adamw_math(w, gv, m, v):
    mn = ADAM_B1 * m + (1.0 - ADAM_B1) * gv
    vn = ADAM_B2 * v + (1.0 - ADAM_B2) * (gv * gv)
    m_hat = mn / (1.0 - ADAM_B1 ** ADAM_STEP)
    v_hat = vn / (1.0 - ADAM_B2 ** ADAM_STEP)
    return -ADAM_LR * (m_hat / (jnp.sqrt(v_hat) + ADAM_EPS) + ADAM_WD * w), mn, vn


def _adamw(w, g, m, v):
    rows, cols = w.shape
    tr = _row_tile(rows)

    def body(w_ref, g_ref, m_ref, v_ref, d_ref, mo_ref, vo_ref):
        d_ref[...], mo_ref[...], vo_ref[...] = _adamw_math(w_ref[...], g_ref[...], m_ref[...], v_ref[...])

    blk = pl.BlockSpec((tr, cols), lambda i: (i, 0))
    shp = jax.ShapeDtypeStruct((rows, cols), F32)
    return pl.pallas_call(
        body, name="adamw", grid=(rows // tr,), in_specs=[blk] * 4, out_specs=[blk] * 3, out_shape=[shp] * 3,
        compiler_params=_params(1),
    )(w, g, m, v)


def _adamw_halves(w, own, got, m, v, c_idx):
    rows, cols = w.shape
    tr = _row_tile(rows // 2)
    per_half = rows // 2 // tr

    def body(c_ref, w_ref, own_ref, got_ref, m_ref, v_ref, d_ref, mo_ref, vo_ref, g_ref):
        mine = (pl.program_id(0) // per_half) == c_ref[0]
        gv = jnp.where(mine, own_ref[...], got_ref[...])
        g_ref[...] = gv
        d_ref[...], mo_ref[...], vo_ref[...] = _adamw_math(w_ref[...], gv, m_ref[...], v_ref[...])

    blk = pl.BlockSpec((tr, cols), lambda i, c: (i, 0))
    hblk = pl.BlockSpec((tr, cols), lambda i, c: (i % per_half, 0))
    shp = jax.ShapeDtypeStruct((rows, cols), F32)
    return pl.pallas_call(
        body, name="adamw_halves",
        grid_spec=pltpu.PrefetchScalarGridSpec(num_scalar_prefetch=1, grid=(rows // tr,),
                                               in_specs=[blk, hblk, hblk, blk, blk], out_specs=[blk] * 4),
        out_shape=[shp] * 4, compiler_params=_params(1),
    )(c_idx, w, own, got, m, v)


def _local_step(x, target, norm_gains, q_g, k_g, ng, weights_of, grads_done):
    s = x.shape[0]
    tm = min(512, s)
    tq = min(256, s)
    g1, gm, g2, gf = norm_gains
    cos2, sin2 = _rope_tables(s)
    gq8 = jnp.tile(q_g, (1, 8))
    gk2 = jnp.tile(k_g, (1, 2))

    tn = min(256, s)
    tk = min(1024, s)
    w1 = weights_of(1, ())
    x1, s1, t1, b1, h1 = _ffn_fwd(x, g1, w1["g1"], w1["u1"], w1["d1"], tm)
    w2 = weights_of(2, (x1,))
    lbl = w2["lbl"]
    pqkv, ph, pg, hm = _mix_in_fwd(x1, gm, w2["in"], tn)
    qe, kr, vr, vs = _qk_prep(pqkv, gq8, gk2, cos2, sin2, tm)
    oa, lse = _attn_fwd(qe, kr, vr, vs, tq)
    ob, pre, hstates = _hgrn_fwd(ph, lbl, ng)
    x2 = _mix_out_fwd(x1, oa, ob, pg, w2["a"], w2["b"], w2["o"], tm)
    w3 = weights_of(3, (x2,))
    x3, s2, t2, b2, h2 = _ffn_fwd(x2, g2, w3["g2"], w3["u2"], w3["d2"], tm)
    dx3, loss, dgf = _loss_head(x3, gf, target, tm)

    dx2, da2, db2, f2, dg2, dx3c = _ffn_bwd(dx3, x2, g2, s2, t2, b2, w3["g2"], w3["u2"], w3["d2"], tm)
    tok = grads_done(3, dict(g2=_dw_shared_b("dw_gate", da2, h2, tk, 1.0), u2=_dw_shared_b("dw_gate", db2, h2, tk, 1.0),
                             d2=_dw_shared_b("dw_down", f2, dx3c, tk, 0.5)))

    dpg, mg, dya, dyb, doe, delta, dob = _mix_out_bwd(dx2, oa, ob, pg, w2["a"], w2["b"], w2["o"], tm, tok)
    g_o = _dw_colblocks("dw_out", mg, dx2, 1, tk).reshape(N_SHARD, D_MODEL // N_SHARD, D_MODEL)
    g_a = _dw_colblocks("dw_branch", oa, dya, N_SHARD, tk)
    g_b = _dw_colblocks("dw_branch", ob, dyb, N_SHARD, tk)
    dqe, dk, dv = _attn_bwd(qe, kr, vr, doe, delta, lse, tq)
    dqkv, dgq, dgk = _qk_prep_bwd(pqkv, dqe, dk, dv, gq8, gk2, cos2, sin2, tm)
    dhq, dhff, dhfb, dhi, dhg, dlb, dng = _hgrn_bwd(ph, pre, dob, hstates, lbl, ng)
    dps = (dqkv, dhq, dhff, dhfb, dhi, dhg, dpg)
    g_in = _dw_in(dps, hm, min(2048, s)).reshape(N_SHARD, -1, D_MODEL)
    tok = grads_done(2, {"in": g_in, "a": g_a, "b": g_b, "o": g_o})
    dx1, dgm = _mix_in_bwd(dps, w2["in"], x1, dx2, gm, tn, tok)

    dx0, da1, db1, f1, dg1, dx1c = _ffn_bwd(dx1, x, g1, s1, t1, b1, w1["g1"], w1["u1"], w1["d1"], tm)
    grads_done(1, dict(g1=_dw_shared_b("dw_gate", da1, h1, tk, 1.0), u1=_dw_shared_b("dw_gate", db1, h1, tk, 1.0),
                       d1=_dw_shared_b("dw_down", f1, dx1c, tk, 0.5)))
    small = dict(g1=dg1, gm=dgm, g2=dg2, gf=dgf, gq=dgq, gk=dgk, lb=dlb, ng=dng)
    return loss, dx0, small, lbl


GROUPS = {1: ("g1", "u1", "d1"), 2: ("in", "a", "b", "o"), 3: ("g2", "u2", "d2")}
BIG = GROUPS[1] + GROUPS[2] + GROUPS[3]
TRANSPOSED = ("g1", "u1", "in", "g2", "u2")


def _pack_rows(vectors, width):
    rows = []
    for vct in vectors:
        flat = vct.reshape(-1)
        pad = (-flat.shape[0]) % width
        rows.append(jnp.pad(flat, (0, pad)).reshape(-1, width))
    return jnp.concatenate(rows, axis=0)


def kernel(x, ffn1_norm_g, ffn1_w_gate, ffn1_w_up, ffn1_w_down, mix_norm_g, w_in, q_norm_g, k_norm_g, hgrn_lb_logits, hgrn_out_norm_g, w_branch_attn, w_branch_hgrn, w_out, ffn2_norm_g, ffn2_w_gate, ffn2_w_up, ffn2_w_down, final_norm_g, loss_target, m_ffn1_norm_g, m_ffn1_w_gate, m_ffn1_w_up, m_ffn1_w_down, m_mix_norm_g, m_w_in, m_q_norm_g, m_k_norm_g, m_hgrn_lb_logits, m_hgrn_out_norm_g, m_w_branch_attn, m_w_branch_hgrn, m_w_out, m_ffn2_norm_g, m_ffn2_w_gate, m_ffn2_w_up, m_ffn2_w_down, m_final_norm_g, v_ffn1_norm_g, v_ffn1_w_gate, v_ffn1_w_up, v_ffn1_w_down, v_mix_norm_g, v_w_in, v_q_norm_g, v_k_norm_g, v_hgrn_lb_logits, v_hgrn_out_norm_g, v_w_branch_attn, v_w_branch_hgrn, v_w_out, v_ffn2_norm_g, v_ffn2_w_gate, v_ffn2_w_up, v_ffn2_w_down, v_final_norm_g):
    xi, yi, ci = lax.axis_index("x"), lax.axis_index("y"), lax.axis_index("c")
    me = 2 * xi + yi
    c_idx = jnp.reshape(ci, (1,)).astype(jnp.int32)
    me_idx = jnp.reshape(me, (1,)).astype(jnp.int32)

    big_w = dict(g1=ffn1_w_gate[0], u1=ffn1_w_up[0], d1=ffn1_w_down[0], a=w_branch_attn[0], b=w_branch_hgrn[0],
                 o=w_out[0], g2=ffn2_w_gate[0], u2=ffn2_w_up[0], d2=ffn2_w_down[0])
    big_w["in"] = w_in[0]
    big_m = dict(g1=m_ffn1_w_gate[0], u1=m_ffn1_w_up[0], d1=m_ffn1_w_down[0], a=m_w_branch_attn[0], b=m_w_branch_hgrn[0],
                 o=m_w_out[0], g2=m_ffn2_w_gate[0], u2=m_ffn2_w_up[0], d2=m_ffn2_w_down[0])
    big_m["in"] = m_w_in[0]
    big_v = dict(g1=v_ffn1_w_gate[0], u1=v_ffn1_w_up[0], d1=v_ffn1_w_down[0], a=v_w_branch_attn[0], b=v_w_branch_hgrn[0],
                 o=v_w_out[0], g2=v_ffn2_w_gate[0], u2=v_ffn2_w_up[0], d2=v_ffn2_w_down[0])
    big_v["in"] = v_w_in[0]
    for table in (big_w, big_m, big_v):
        for n in TRANSPOSED:
            table[n] = table[n].T

    slots = {n: _cast_into_slot(big_w[n], me_idx, CD) for n in BIG}
    lbl_slot = _cast_into_slot(hgrn_lb_logits.reshape(4, LANES), me_idx, F32)
    started, token = {}, ()
    for grp in (1, 2, 3):
        lands = [slots[n] for n in GROUPS[grp]] + ([lbl_slot] if grp == 2 else [])
        send, recv, _, lands, tok = _exchange_start("gather%d_start" % grp, [], lands, token)
        started[grp], token = (send, recv, lands), (tok,)

    def weights_of(grp, after):
        send, recv, lands = started[grp]
        got = _exchange_wait("gather%d_wait" % grp, send, recv, [], lands, tuple(after) + (token if grp == 1 else ()))
        by_halves = [i for i, land in enumerate(got) if _halved(land)]
        for i, whole in zip(by_halves, _pair_fill("gather%d_fill" % grp, [got[i] for i in by_halves])):
            got[i] = whole
        w = dict(zip(GROUPS[grp], got))
        if grp == 2:
            w["in"] = w["in"].reshape(-1, D_MODEL)
            w["o"] = w["o"].reshape(D_MODEL, D_MODEL)
            w["lbl"] = jnp.transpose(got[-1], (1, 0, 2)).reshape(4, N_SHARD * LANES)
        return w

    pending = {}

    def grads_done(grp, grads):
        names = list(grads)
        glist = [grads[n] for n in names]
        got = _pair_exchange(glist)
        sums, owns = zip(*[_pair_sum(g, r, c_idx, me_idx) for g, r in zip(glist, got)])
        lands = [lax.empty(s_.shape, s_.dtype) for s_ in sums]
        send, recv, srcs, lands, tok = _exchange_start("reduce%d_start" % grp, list(sums), lands, ())
        pending[grp] = (names, send, recv, srcs, lands, owns, tok)
        return (tok,)

    def reduced_halves(grp, after):
        names, send, recv, srcs, lands, owns, _ = pending[grp]
        parts = _exchange_wait("reduce%d_wait" % grp, send, recv, srcs, lands, after)
        return names, [_chip_sum(o, p, me_idx) for o, p in zip(owns, parts)]

    loss, dx, small, lbl = _local_step(
        x[0], loss_target[0], (ffn1_norm_g, mix_norm_g, ffn2_norm_g, final_norm_g.reshape(1, -1)),
        q_norm_g, k_norm_g, hgrn_out_norm_g, weights_of, grads_done)

    dgq = small["gq"].reshape(8, HEAD_DIM).sum(axis=0)
    dgk = small["gk"].reshape(2, HEAD_DIM).sum(axis=0)
    lb_full = _hgrn_lower_bounds(lbl)
    dlog = []
    for d in (0, 1):
        t = small["lb"][d:d + 1] * lb_full[d] * (1.0 - lb_full[d])
        dlog += [t, -t]
    small_list = [small["g1"], small["gm"], small["g2"], small["gf"], small["ng"], dgq, dgk, jnp.concatenate(dlog, axis=0), loss[0, 0]]
    packed = _pack_rows(small_list, D_MODEL)
    n_rows = packed.shape[0]
    packed = jnp.pad(packed, ((0, (-n_rows) % 8), (0, 0)))
    red = _small_allreduce(packed)
    loss_out = red[n_rows - 1, 0]
    sg = dict(g1=red[0:1], gm=red[1:2], g2=red[2:3], gf=red[3], ng=red[4:5, :512], gq=red[5:6, :HEAD_DIM],
              gk=red[6:7, :HEAD_DIM])
    dlog_full = red[7:9].reshape(2, 2, 512)
    sg["lb"] = lax.dynamic_slice_in_dim(dlog_full, me * LANES, LANES, axis=2)

    small_w = dict(g1=ffn1_norm_g, gm=mix_norm_g, g2=ffn2_norm_g, gf=final_norm_g, ng=hgrn_out_norm_g, gq=q_norm_g,
                   gk=k_norm_g, lb=hgrn_lb_logits)
    small_m = dict(g1=m_ffn1_norm_g, gm=m_mix_norm_g, g2=m_ffn2_norm_g, gf=m_final_norm_g, ng=m_hgrn_out_norm_g,
                   gq=m_q_norm_g, gk=m_k_norm_g, lb=m_hgrn_lb_logits)
    small_v = dict(g1=v_ffn1_norm_g, gm=v_mix_norm_g, g2=v_ffn2_norm_g, gf=v_final_norm_g, ng=v_hgrn_out_norm_g,
                   gq=v_q_norm_g, gk=v_k_norm_g, lb=v_hgrn_lb_logits)
    small_names = ("g1", "gm", "g2", "gf", "ng", "gq", "gk", "lb")
    pack = lambda dct: _pack_rows([dct[n] for n in small_names], D_MODEL)
    pw, pgr, pm, pv = pack(small_w), pack(sg), pack(small_m), pack(small_v)
    pad8 = lambda a: jnp.pad(a, ((0, (-a.shape[0]) % 8), (0, 0)))
    sd, sm_, sv_ = _adamw(pad8(pw), pad8(pgr), pad8(pm), pad8(pv))

    def unpack(buf):
        out, r = {}, 0
        for n in small_names:
            size = small_w[n].size
            nr = -(-size // D_MODEL)
            out[n] = buf[r:r + nr].reshape(-1)[:size].reshape(small_w[n].shape)
            r += nr
        return out

    sdelta, snew_m, snew_v = unpack(sd), unpack(sm_), unpack(sv_)
    sgrad = {n: sg[n].reshape(small_w[n].shape) for n in small_names}

    bdelta, bnew_m, bnew_v, bgrad = {}, {}, {}, {}

    def update(names, halves):
        for n, own, got in zip(names, halves, _pair_share(halves)):
            res = _adamw_halves(big_w[n], own, got, big_m[n], big_v[n], c_idx)
            if n in TRANSPOSED:
                res = [r.T for r in res]
            bdelta[n], bnew_m[n], bnew_v[n], bgrad[n] = [r[None] for r in res]

    names3, halves3 = reduced_halves(3, (pending[1][-1],))
    names2, halves2 = reduced_halves(2, (halves3[0],))
    update(names3 + names2, halves3 + halves2)
    names1, halves1 = reduced_halves(1, (bdelta[names2[-1]],))
    update(names1, halves1)

    order = [("s", "g1"), ("b", "g1"), ("b", "u1"), ("b", "d1"), ("s", "gm"), ("b", "in"), ("s", "gq"), ("s", "gk"),
             ("s", "lb"), ("s", "ng"), ("b", "a"), ("b", "b"), ("b", "o"), ("s", "g2"), ("b", "g2"), ("b", "u2"),
             ("b", "d2"), ("s", "gf")]
    outs = [loss_out, dx[None]]
    for table_s, table_b in ((sgrad, bgrad), (sdelta, bdelta), (snew_m, bnew_m), (snew_v, bnew_v)):
        outs += [(table_s if kind == "s" else table_b)[n] for kind, n in order]
    return tuple(outs)
```

```python
import functools

import numpy as np
import jax
import jax.numpy as jnp
from jax import lax
from jax.experimental import pallas as pl
from jax.experimental.pallas import tpu as pltpu

F32 = jnp.float32
BF16 = jnp.bfloat16
CD = jnp.bfloat16

EPS = 1e-6
D_MODEL = 1024
HEAD_DIM = 64
GRID_W = 64
ROPE_THETA = 10000.0
CHUNK = 32
N_SHARD = 4
N_DEV = 8
VMEM_LIMIT = 56 * 1024 * 1024
LANES = 128
HG_TILE = 256
FFN_ROWS = 256

ADAM_LR = 0.001
ADAM_B1 = 0.9
ADAM_B2 = 0.999
ADAM_EPS = 1e-08
ADAM_WD = 0.01
ADAM_STEP = 10

NN = (((1,), (0,)), ((), ()))
NT = (((1,), (1,)), ((), ()))
TN = (((0,), (0,)), ((), ()))
MESH = pl.DeviceIdType.MESH
ANY = pl.BlockSpec(memory_space=pl.ANY)


def _mm(a, b, dn):
    return lax.dot_general(a.astype(CD), b.astype(CD), dn, preferred_element_type=F32)


def _split3(x):
    hi = x.astype(BF16)
    r = x - hi.astype(F32)
    mid = r.astype(BF16)
    lo = (r - mid.astype(F32)).astype(BF16)
    return hi, mid, lo


def _xdot(x, m):
    rows = x.shape[0]
    r = lax.dot_general(jnp.concatenate(_split3(x), axis=0), m, NN, preferred_element_type=F32)
    return r[:rows] + r[rows:2 * rows] + r[2 * rows:]


def _xdot_l(m, x):
    cols = x.shape[1]
    r = lax.dot_general(m, jnp.concatenate(_split3(x), axis=1), NN, preferred_element_type=F32)
    return r[:, :cols] + r[:, cols:2 * cols] + r[:, 2 * cols:]


def _params(n_grid):
    return pltpu.CompilerParams(dimension_semantics=("arbitrary",) * n_grid, vmem_limit_bytes=VMEM_LIMIT)


def _sigmoid(x):
    return jax.nn.sigmoid(x)


def _np_blocksum(n):
    i = np.arange(n)
    return (i[:, None] // HEAD_DIM == i[None, :] // HEAD_DIM).astype(np.float32)


def _np_swap32(n):
    i = np.arange(n)
    partner = np.where(i % HEAD_DIM < HEAD_DIM // 2, i + HEAD_DIM // 2, i - HEAD_DIM // 2)
    m = np.zeros((n, n), np.float32)
    m[i, partner] = 1.0
    return m


def _np_expand_q():
    m = np.zeros((512, 1024), np.float32)
    for h in range(8):
        g = h // 4
        for d in range(HEAD_DIM):
            m[64 * h + d, 128 * h + 64 * g + d] = 1.0
    return m


def _np_bcast_head():
    m = np.zeros((512, 1024), np.float32)
    for h in range(8):
        m[64 * h, 128 * h:128 * h + 128] = 1.0
    return m


def _np_swap_halves():
    m = np.zeros((128, 128), np.float32)
    i = np.arange(128)
    m[i, (i + 64) % 128] = 1.0
    return m


def _np_hgrn_cums(t, rev):
    r = np.arange(t)[:, None]
    c = np.arange(t)[None, :]
    same = (r // CHUNK) == (c // CHUNK)
    if not rev:
        cum = same & (c <= r)
        mid = same & (c % CHUNK <= CHUNK // 2 - 1)
    else:
        cum = same & (c >= r)
        mid = same & (c % CHUNK >= CHUNK // 2)
    return np.concatenate([cum, mid, same], axis=0).astype(np.float32)


def _bf(a):
    return jnp.asarray(a, dtype=BF16)


def _rope_tables(seq_len):
    rows = seq_len // GRID_W
    row = jnp.repeat(jnp.arange(rows, dtype=F32), GRID_W)
    col = jnp.tile(jnp.arange(GRID_W, dtype=F32), rows)
    n_freq = HEAD_DIM // 4
    inv = ROPE_THETA ** (-jnp.arange(n_freq, dtype=F32) / n_freq)
    ang = jnp.concatenate([row[:, None] * inv, col[:, None] * inv], axis=-1)
    cos, sin = jnp.cos(ang), jnp.sin(ang)
    c64 = jnp.concatenate([cos, cos], axis=-1)
    s64 = jnp.concatenate([-sin, sin], axis=-1)
    return jnp.tile(c64, (1, 2)), jnp.tile(s64, (1, 2))


def _ffn_fwd(x, g, wg, wu, wd, tm):
    s, d = x.shape
    nsh, fs, _ = wg.shape

    def body(x_ref, g_ref, wg_ref, wu_ref, wd_ref, xo_ref, a_ref, da_ref, b_ref, hb_ref, acc, hs):
        j = pl.program_id(1)

        @pl.when(j == 0)
        def _():
            xv = x_ref[...]
            r = lax.rsqrt(jnp.mean(xv * xv, axis=-1, keepdims=True) + EPS)
            h = (xv * r * g_ref[...]).astype(CD)
            hs[...] = h
            hb_ref[...] = h
            acc[...] = jnp.zeros_like(acc)

        for r0 in range(0, tm, FFN_ROWS):
            rows = slice(r0, min(r0 + FFN_ROWS, tm))
            h = hs[rows, :]
            a = _mm(h, wg_ref[0], NT)
            b = _mm(h, wu_ref[0], NT)
            sg = _sigmoid(a)
            silu = a * sg
            acc[rows, :] += _mm(silu * b, wd_ref[0], NN)
            a_ref[0, rows, :] = silu.astype(CD)
            da_ref[0, rows, :] = (sg * (1.0 + a * (1.0 - sg))).astype(CD)
            b_ref[0, rows, :] = b.astype(CD)

        @pl.when(j == nsh - 1)
        def _():
            xo_ref[...] = x_ref[...] + 0.5 * acc[...]

    return pl.pallas_call(
        body, name="ffn_fwd", grid=(s // tm, nsh),
        in_specs=[pl.BlockSpec((tm, d), lambda i, j: (i, 0)), pl.BlockSpec((1, d), lambda i, j: (0, 0))]
        + [pl.BlockSpec((1, fs, d), lambda i, j: (j, 0, 0))] * 3,
        out_specs=[pl.BlockSpec((tm, d), lambda i, j: (i, 0))] + [pl.BlockSpec((1, tm, fs), lambda i, j: (j, i, 0))] * 3
        + [pl.BlockSpec((tm, d), lambda i, j: (i, 0))],
        out_shape=[jax.ShapeDtypeStruct((s, d), F32)] + [jax.ShapeDtypeStruct((nsh, s, fs), CD)] * 3
        + [jax.ShapeDtypeStruct((s, d), CD)],
        scratch_shapes=[pltpu.VMEM((tm, d), F32), pltpu.VMEM((tm, d), CD)],
        compiler_params=_params(2),
    )(x, g, wg, wu, wd)


def _ffn_bwd(dout, x, g, silu, dsilu, b, wg, wu, wd, tm):
    s, d = x.shape
    nsh, fs, _ = wg.shape

    def body(do_ref, x_ref, g_ref, sl_ref, ds_ref, b_ref, wg_ref, wu_ref, wd_ref,
             dx_ref, da_ref, db_ref, f_ref, dg_ref, do16_ref, dh):
        i = pl.program_id(0)
        j = pl.program_id(1)

        @pl.when(j == 0)
        def _():
            dh[...] = jnp.zeros_like(dh)
            do16_ref[...] = do_ref[...].astype(CD)

        @pl.when((i == 0) & (j == 0))
        def _():
            dg_ref[...] = jnp.zeros_like(dg_ref)

        for r0 in range(0, tm, FFN_ROWS):
            rows = slice(r0, min(r0 + FFN_ROWS, tm))
            sl = sl_ref[0, rows, :].astype(F32)
            bv = b_ref[0, rows, :].astype(F32)
            df = 0.5 * _mm(do_ref[rows, :], wd_ref[0], NT)
            da = df * bv * ds_ref[0, rows, :].astype(F32)
            db = df * sl
            dh[rows, :] += _mm(da, wg_ref[0], NN) + _mm(db, wu_ref[0], NN)
            da_ref[0, rows, :] = da.astype(CD)
            db_ref[0, rows, :] = db.astype(CD)
            f_ref[0, rows, :] = (sl * bv).astype(CD)

        @pl.when(j == nsh - 1)
        def _():
            xv = x_ref[...]
            r = lax.rsqrt(jnp.mean(xv * xv, axis=-1, keepdims=True) + EPS)
            dhv = dh[...]
            u = dhv * g_ref[...]
            dx_ref[...] = do_ref[...] + r * u - xv * (r * r * r) * jnp.mean(u * xv, axis=-1, keepdims=True)
            dg_ref[...] += jnp.sum(dhv * xv * r, axis=0, keepdims=True)

    act = pl.BlockSpec((1, tm, fs), lambda i, j: (j, i, 0))
    row = pl.BlockSpec((tm, d), lambda i, j: (i, 0))
    return pl.pallas_call(
        body, name="ffn_bwd", grid=(s // tm, nsh),
        in_specs=[row, row, pl.BlockSpec((1, d), lambda i, j: (0, 0)), act, act, act]
        + [pl.BlockSpec((1, fs, d), lambda i, j: (j, 0, 0))] * 3,
        out_specs=[row, act, act, act, pl.BlockSpec((1, d), lambda i, j: (0, 0)), row],
        out_shape=[jax.ShapeDtypeStruct((s, d), F32), jax.ShapeDtypeStruct((nsh, s, fs), CD),
                   jax.ShapeDtypeStruct((nsh, s, fs), CD), jax.ShapeDtypeStruct((nsh, s, fs), CD),
                   jax.ShapeDtypeStruct((1, d), F32), jax.ShapeDtypeStruct((s, d), CD)],
        scratch_shapes=[pltpu.VMEM((tm, d), F32)],
        compiler_params=_params(2),
    )(dout, x, g, silu, dsilu, b, wg, wu, wd)


def _tn_call(name, operands, in_specs, out_shape, out_spec, grid, acc_shape, pick, scale=1.0):
    nk = grid[-1]
    n_in = len(operands)

    def body(*refs):
        out_ref, out16_ref, acc = refs[n_in], refs[n_in + 1], refs[n_in + 2]
        k = pl.program_id(len(grid) - 1)

        @pl.when(k == 0)
        def _():
            acc[...] = jnp.zeros_like(acc)

        pick(refs[:n_in], acc)

        @pl.when(k == nk - 1)
        def _():
            res = (acc[...] if scale == 1.0 else acc[...] * scale).reshape(out_ref.shape)
            out_ref[...] = res
            out16_ref[...] = res.astype(CD)

    return pl.pallas_call(
        body, name=name, grid=grid, in_specs=in_specs, out_specs=[out_spec, out_spec],
        out_shape=[out_shape, jax.ShapeDtypeStruct(out_shape.shape, CD)],
        scratch_shapes=[pltpu.VMEM(acc_shape, F32)], compiler_params=_params(len(grid)),
    )(*operands)


def _dw_shared_b(name, a3, b, tk, scale):
    nj, s, m = a3.shape
    n = b.shape[1]

    def pick(refs, acc):
        rows = pl.ds(pl.multiple_of(pl.program_id(1) * tk, tk), tk)
        acc[...] += _mm(refs[0][0], refs[1][rows, :], TN)

    return _tn_call(name, (a3, b),
                    [pl.BlockSpec((1, tk, m), lambda j, k: (j, k, 0)), pl.BlockSpec((s, n), lambda j, k: (0, 0))],
                    jax.ShapeDtypeStruct((nj, m, n), F32), pl.BlockSpec((1, m, n), lambda j, k: (j, 0, 0)),
                    (nj, s // tk), (m, n), pick, scale)


def _dw_colblocks(name, a, b, nj, tk):
    s, m = a.shape
    n = b.shape[1] // nj

    def pick(refs, acc):
        acc[...] += _mm(refs[0][...], refs[1][...], TN)

    return _tn_call(name, (a, b),
                    [pl.BlockSpec((tk, m), lambda j, k: (k, 0)), pl.BlockSpec((tk, n), lambda j, k: (k, j))],
                    jax.ShapeDtypeStruct((nj, m, n), F32), pl.BlockSpec((1, m, n), lambda j, k: (j, 0, 0)),
                    (nj, s // tk), (m, n), pick)


DP_WIDTHS = (768, 512, 512, 512, 512, 512, 2048)
DP_CHUNK = 256


def _dp_chunk_maps():
    starts, counts, off = [], [], 0
    for w in DP_WIDTHS:
        starts.append(off // DP_CHUNK)
        counts.append(w // DP_CHUNK)
        off += w
    return starts, counts


def _dp_specs(tm, row_axis, chunk_axis):
    starts, counts = _dp_chunk_maps()
    specs = []
    for st, cnt in zip(starts, counts):
        def imap(*ids, st=st, cnt=cnt):
            return (ids[row_axis], jnp.clip(ids[chunk_axis] - st, 0, cnt - 1))
        specs.append(pl.BlockSpec((tm, DP_CHUNK), imap))
    return specs


def _dp_select(n, refs, fn):
    starts, counts = _dp_chunk_maps()
    for ref, st, cnt in zip(refs, starts, counts):
        @pl.when((n >= st) & (n < st + cnt))
        def _(ref=ref):
            fn(ref)


def _dw_in(dps, hb, tk):
    s, d = hb.shape
    n_chunks = sum(DP_WIDTHS) // DP_CHUNK

    def pick(refs, acc):
        rows = pl.ds(pl.multiple_of(pl.program_id(1) * tk, tk), tk)

        def add(ref):
            acc[...] += _mm(ref[...], refs[7][rows, :], TN)

        _dp_select(pl.program_id(0), refs[:7], add)

    return _tn_call("dw_in", (*dps, hb),
                    _dp_specs(tk, 1, 0) + [pl.BlockSpec((s, d), lambda n, k: (0, 0))],
                    jax.ShapeDtypeStruct((n_chunks * DP_CHUNK, d), F32), pl.BlockSpec((DP_CHUNK, d), lambda n, k: (n, 0)),
                    (n_chunks, s // tk), (DP_CHUNK, d), pick)


def _mix_in_fwd(x, g, w_t, tm):
    s, d = x.shape
    n_in = w_t.shape[0]

    def body(x_ref, g_ref, w_ref, qkv_ref, hg_ref, gt_ref, hb_ref):
        xv = x_ref[...]
        r = lax.rsqrt(jnp.mean(xv * xv, axis=-1, keepdims=True) + EPS)
        h = (xv * r * g_ref[...]).astype(CD)
        hb_ref[...] = h
        qkv_ref[...] = _mm(h, w_ref[0:768, :], NT)
        for c in range(5):
            hg_ref[:, 512 * c:512 * c + 512] = _mm(h, w_ref[768 + 512 * c:768 + 512 * c + 512, :], NT)
        for c in range(2):
            gt_ref[:, 1024 * c:1024 * c + 1024] = _mm(h, w_ref[3328 + 1024 * c:3328 + 1024 * c + 1024, :], NT)

    row = lambda w: pl.BlockSpec((tm, w), lambda i: (i, 0))
    return pl.pallas_call(
        body, name="mix_in_fwd", grid=(s // tm,),
        in_specs=[row(d), pl.BlockSpec((1, d), lambda i: (0, 0)), pl.BlockSpec((n_in, d), lambda i: (0, 0))],
        out_specs=[row(768), row(2560), row(2048), row(d)],
        out_shape=[jax.ShapeDtypeStruct((s, 768), F32), jax.ShapeDtypeStruct((s, 2560), F32),
                   jax.ShapeDtypeStruct((s, 2048), F32), jax.ShapeDtypeStruct((s, d), CD)],
        compiler_params=_params(1),
    )(x, g, w_t)


def _mix_in_bwd(dps, w_t, x, dres, g, tm, after=()):
    s, d = x.shape
    n_in = w_t.shape[0]

    def body(*refs):
        refs = refs[len(after):]
        dp_refs = refs[:7]
        w_ref, x_ref, dr_ref, g_ref, dx_ref, dg_ref = refs[7:]

        @pl.when(pl.program_id(0) == 0)
        def _():
            dg_ref[...] = jnp.zeros_like(dg_ref)

        dhv = jnp.zeros((tm, d), F32)
        off = 0
        for ref, width in zip(dp_refs, DP_WIDTHS):
            dhv = dhv + _mm(ref[...], w_ref[off:off + width, :], NN)
            off += width
        xv = x_ref[...]
        r = lax.rsqrt(jnp.mean(xv * xv, axis=-1, keepdims=True) + EPS)
        u = dhv * g_ref[...]
        dx_ref[...] = dr_ref[...] + r * u - xv * (r * r * r) * jnp.mean(u * xv, axis=-1, keepdims=True)
        dg_ref[...] += jnp.sum(dhv * xv * r, axis=0, keepdims=True)

    row = pl.BlockSpec((tm, d), lambda i: (i, 0))
    vec = pl.BlockSpec((1, d), lambda i: (0, 0))
    return pl.pallas_call(
        body, name="mix_in_bwd", grid=(s // tm,),
        in_specs=[ANY] * len(after) + [pl.BlockSpec((tm, w), lambda i: (i, 0)) for w in DP_WIDTHS]
        + [pl.BlockSpec((n_in, d), lambda i: (0, 0)), row, row, vec],
        out_specs=[row, vec],
        out_shape=[jax.ShapeDtypeStruct((s, d), F32), jax.ShapeDtypeStruct((1, d), F32)],
        compiler_params=_params(1),
    )(*after, *dps, w_t, x, dres, g)


def _headnorm_rope(x, gain, cos, sin, blocksum, swap):
    ss = _xdot(x * x, blocksum)
    r = lax.rsqrt(ss * (1.0 / HEAD_DIM) + EPS)
    y = x * r * gain
    return y * cos + _xdot(y, swap) * sin, r


def _headnorm_rope_bwd(dz, x, gain, cos, sin, blocksum, swap):
    ss = _xdot(x * x, blocksum)
    r = lax.rsqrt(ss * (1.0 / HEAD_DIM) + EPS)
    dy = dz * cos + _xdot(dz * sin, swap)
    u = dy * gain
    mean_ux = _xdot(u * x, blocksum) * (1.0 / HEAD_DIM)
    dx = r * u - x * (r * r * r) * mean_ux
    return dx, jnp.sum(dy * x * r, axis=0, keepdims=True)


def _qk_prep(pqkv, gq, gk, cos2, sin2, tm):
    s = pqkv.shape[0]
    bs512, sw512, eq, swh = _bf(_np_blocksum(512)), _bf(_np_swap32(512)), _bf(_np_expand_q()), _bf(_np_swap_halves())

    def body(q_ref, kv_ref, gq_ref, gk_ref, c_ref, s_ref, bs_ref, sw_ref, eq_ref, swh_ref, qe_ref, k_ref, v_ref, vs_ref):
        c2, s2 = c_ref[...], s_ref[...]
        c8, s8 = jnp.tile(c2, (1, 4)), jnp.tile(s2, (1, 4))
        bs, sw = bs_ref[...], sw_ref[...]
        zq, _ = _headnorm_rope(q_ref[...], gq_ref[...], c8, s8, bs, sw)
        qe_ref[...] = _mm(zq * (HEAD_DIM ** -0.5), eq_ref[...], NN).astype(CD)
        kv = kv_ref[...]
        zk, _ = _headnorm_rope(kv[:, :LANES], gk_ref[...], c2, s2, bs[:LANES, :LANES], sw[:LANES, :LANES])
        k_ref[...] = zk.astype(CD)
        v = kv[:, LANES:]
        v_ref[...] = v.astype(CD)
        vs_ref[...] = _mm(v, swh_ref[...], NN).astype(CD)

    full = lambda a: pl.BlockSpec(a.shape, lambda i: (0,) * a.ndim)
    tab = pl.BlockSpec((tm, LANES), lambda i: (i, 0))
    return pl.pallas_call(
        body, name="qk_prep", grid=(s // tm,),
        in_specs=[pl.BlockSpec((tm, 512), lambda i: (i, 0)), pl.BlockSpec((tm, 256), lambda i: (i, 2)),
                  full(gq), full(gk), tab, tab, full(bs512), full(sw512), full(eq), full(swh)],
        out_specs=[pl.BlockSpec((tm, 1024), lambda i: (i, 0)), tab, tab, tab],
        out_shape=[jax.ShapeDtypeStruct((s, 1024), CD)] + [jax.ShapeDtypeStruct((s, LANES), CD)] * 3,
        compiler_params=_params(1),
    )(pqkv, pqkv, gq, gk, cos2, sin2, bs512, sw512, eq, swh)


def _qk_prep_bwd(pqkv, dqe, dk, dv, gq, gk, cos2, sin2, tm):
    s = pqkv.shape[0]
    bs512, sw512, eqt = _bf(_np_blocksum(512)), _bf(_np_swap32(512)), _bf(_np_expand_q().T)

    def body(q_ref, kv_ref, dqe_ref, dk_ref, dv_ref, gq_ref, gk_ref, c_ref, s_ref, bs_ref, sw_ref, eqt_ref,
             dp_ref, dgq_ref, dgk_ref):
        @pl.when(pl.program_id(0) == 0)
        def _():
            dgq_ref[...] = jnp.zeros_like(dgq_ref)
            dgk_ref[...] = jnp.zeros_like(dgk_ref)

        c2, s2 = c_ref[...], s_ref[...]
        c8, s8 = jnp.tile(c2, (1, 4)), jnp.tile(s2, (1, 4))
        bs, sw = bs_ref[...], sw_ref[...]
        dzq = _xdot(dqe_ref[...], eqt_ref[...]) * (HEAD_DIM ** -0.5)
        dxq, dgq = _headnorm_rope_bwd(dzq, q_ref[...], gq_ref[...], c8, s8, bs, sw)
        kv = kv_ref[...]
        dxk, dgk = _headnorm_rope_bwd(dk_ref[...], kv[:, :LANES], gk_ref[...], c2, s2, bs[:LANES, :LANES], sw[:LANES, :LANES])
        dp_ref[...] = jnp.concatenate([dxq, dxk, dv_ref[...]], axis=1).astype(CD)
        dgq_ref[...] += dgq
        dgk_ref[...] += dgk

    full = lambda a: pl.BlockSpec(a.shape, lambda i: (0,) * a.ndim)
    tab = pl.BlockSpec((tm, LANES), lambda i: (i, 0))
    return pl.pallas_call(
        body, name="qk_prep_bwd", grid=(s // tm,),
        in_specs=[pl.BlockSpec((tm, 512), lambda i: (i, 0)), pl.BlockSpec((tm, 256), lambda i: (i, 2)),
                  pl.BlockSpec((tm, 1024), lambda i: (i, 0)), tab, tab, full(gq), full(gk), tab, tab,
                  full(bs512), full(sw512), full(eqt)],
        out_specs=[pl.BlockSpec((tm, 768), lambda i: (i, 0)), pl.BlockSpec((1, 512), lambda i: (0, 0)),
                   pl.BlockSpec((1, LANES), lambda i: (0, 0))],
        out_shape=[jax.ShapeDtypeStruct((s, 768), CD), jax.ShapeDtypeStruct((1, 512), F32),
                   jax.ShapeDtypeStruct((1, LANES), F32)],
        compiler_params=_params(1),
    )(pqkv, pqkv, dqe, dk, dv, gq, gk, cos2, sin2, bs512, sw512, eqt)


def _attn_fwd(qe, k, v, vs, tq):
    s = k.shape[0]

    def body(qa_ref, qb_ref, k_ref, v_ref, vs_ref, o_ref, lse_ref):
        m = pl.program_id(0)
        grp = m // 2
        kk = k_ref[...]
        outs = []
        for idx, q_ref in enumerate((qa_ref, qb_ref)):
            sc = _mm(q_ref[...], kk, NT)
            mx = jnp.max(sc, axis=-1, keepdims=True)
            e = jnp.exp(sc - mx)
            l = jnp.sum(e, axis=-1, keepdims=True)
            lse_ref[idx] = mx + jnp.log(l)
            p = e * (1.0 / l)
            vsel = jnp.where(grp != idx, vs_ref[...], v_ref[...])
            outs.append(_mm(p, vsel, NN))
        lane = lax.broadcasted_iota(jnp.int32, (1, LANES), 1)
        o_ref[...] = jnp.where(lane < HEAD_DIM, outs[0], outs[1])

    kv = pl.BlockSpec((s, LANES), lambda m, i: (0, 0))
    return pl.pallas_call(
        body, name="attn_fwd", grid=(4, s // tq),
        in_specs=[pl.BlockSpec((tq, LANES), lambda m, i: (i, 2 * m)), pl.BlockSpec((tq, LANES), lambda m, i: (i, 2 * m + 1)),
                  kv, kv, kv],
        out_specs=[pl.BlockSpec((tq, LANES), lambda m, i: (i, m)), pl.BlockSpec((2, tq, 1), lambda m, i: (m, i, 0))],
        out_shape=[jax.ShapeDtypeStruct((s, 512), F32), jax.ShapeDtypeStruct((8, s, 1), F32)],
        compiler_params=_params(2),
    )(qe, qe, k, v, vs)


def _attn_bwd(qe, k, v, doe, delta, lse, tq):
    s = k.shape[0]

    def body(q_ref, k_ref, v_ref, do_ref, dl_ref, lse_ref, dq_ref, dk_ref, dv_ref):
        @pl.when((pl.program_id(0) == 0) & (pl.program_id(1) == 0))
        def _():
            dk_ref[...] = jnp.zeros_like(dk_ref)
            dv_ref[...] = jnp.zeros_like(dv_ref)

        q, kk, do = q_ref[...], k_ref[...], do_ref[...]
        p = jnp.exp(_mm(q, kk, NT) - lse_ref[0])
        dp = _mm(do, v_ref[...], NT)
        ds = p * (dp - jnp.max(dl_ref[...], axis=-1, keepdims=True))
        dq_ref[...] = _mm(ds, kk, NN)
        dk_ref[...] += _mm(ds, q, TN)
        dv_ref[...] += _mm(p, do, TN)

    kv = pl.BlockSpec((s, LANES), lambda h, i: (0, 0))
    blk = pl.BlockSpec((tq, LANES), lambda h, i: (i, h))
    return pl.pallas_call(
        body, name="attn_bwd", grid=(8, s // tq),
        in_specs=[blk, kv, kv, blk, blk, pl.BlockSpec((1, tq, 1), lambda h, i: (h, i, 0))],
        out_specs=[blk, kv, kv],
        out_shape=[jax.ShapeDtypeStruct((s, 1024), F32), jax.ShapeDtypeStruct((s, LANES), F32),
                   jax.ShapeDtypeStruct((s, LANES), F32)],
        compiler_params=_params(2),
    )(qe, k, v, doe, delta, lse)


@jax.custom_vjp
def _mm_nn(a, b):
    return _mm(a, b, NN)


_mm_nn.defvjp(lambda a, b: (_mm(a, b, NN), (a, b)),
              lambda res, g: (_mm(g, res[1], NT), _mm(res[0], g, TN)))


@jax.custom_vjp
def _mm_nt(a, b):
    return _mm(a, b, NT)


_mm_nt.defvjp(lambda a, b: (_mm(a, b, NT), (a, b)),
              lambda res, g: (_mm(g, res[1], NN), _mm(g, res[0], TN)))


@jax.custom_vjp
def _mm_tn(a, b):
    return _mm(a, b, TN)


_mm_tn.defvjp(lambda a, b: (_mm(a, b, TN), (a, b)),
              lambda res, g: (_mm(res[1], g, NT), _mm(res[0], g, NN)))


@jax.custom_vjp
def _cmm(m, mt, x):
    return _xdot_l(m, x)


_cmm.defvjp(lambda m, mt, x: (_xdot_l(m, x), (m, mt)),
            lambda res, g: (jnp.zeros_like(res[0]), jnp.zeros_like(res[1]), _xdot_l(res[1], g)))


def _hgrn_masks(t, rev):
    n_ch = t // CHUNK
    r = jnp.bitwise_and(lax.broadcasted_iota(jnp.int32, (2 * t, t), 0), t - 1)
    c = lax.broadcasted_iota(jnp.int32, (2 * t, t), 1)
    same = jnp.right_shift(r, 5) == jnp.right_shift(c, 5)
    tri2 = same & ((c >= r) if rev else (c <= r))
    pr = lax.broadcasted_iota(jnp.int32, (LANES, LANES), 0)
    pc = lax.broadcasted_iota(jnp.int32, (LANES, LANES), 1)
    diag = jnp.right_shift(pr, 6) == jnp.right_shift(pc, 6)
    qr = lax.broadcasted_iota(jnp.int32, (t, n_ch * LANES), 0)
    qc = lax.broadcasted_iota(jnp.int32, (t, n_ch * LANES), 1)
    rows_chunk = jnp.right_shift(qc, 7) == jnp.right_shift(qr, 5)
    vr = lax.broadcasted_iota(jnp.int32, (n_ch * LANES, t), 0)
    vc = lax.broadcasted_iota(jnp.int32, (n_ch * LANES, t), 1)
    cols_chunk = jnp.right_shift(vr, 7) == jnp.right_shift(vc, 5)
    return dict(tri2=tri2, diag=diag, rows_chunk=rows_chunk, cols_chunk=cols_chunk)


def _hgrn_gates(xf, lb):
    f = lb + (1.0 - lb) * _sigmoid(xf)
    return 1.0 - f, jnp.log(f)


def _hgrn_dir(xq, xf, v, lb, state, cm, cmt, mk, rev):
    t = xq.shape[0]
    n_ch = t // CHUNK
    lo = lax.broadcasted_iota(jnp.int32, (1, LANES), 1) < HEAD_DIM
    q = xq * _sigmoid(xq)
    k, lf = _hgrn_gates(xf, lb)
    cs = _cmm(cm, cmt, lf)
    b, bm, bl = cs[:t], cs[t:2 * t], cs[2 * t:]
    qd = q * jnp.exp(b - bm)
    kd = k * jnp.exp(bm - b)
    kc = k * jnp.exp(bl - b)
    qe = q * jnp.exp(b)
    qd2 = jnp.concatenate([jnp.where(lo, qd, 0.0), jnp.where(lo, 0.0, qd)], axis=0)
    o2 = _mm_nn(jnp.where(mk["tri2"], _mm_nt(qd2, kd), 0.0), v)
    o = jnp.where(lo, o2[:t], o2[t:])
    vexp = jnp.where(mk["cols_chunk"], jnp.concatenate([jnp.transpose(v)] * n_ch, axis=0), 0.0)
    adds = _mm_nn(vexp, kc)
    dec = jnp.exp(bl)
    entering = [None] * n_ch
    for c in (range(n_ch - 1, -1, -1) if rev else range(n_ch)):
        entering[c] = state
        d = jnp.concatenate([dec[c * CHUNK:(c + 1) * CHUNK]] * (LANES // CHUNK), axis=0)
        state = d * state + jnp.where(mk["diag"], adds[c * LANES:(c + 1) * LANES], 0.0)
    qexp = jnp.where(mk["rows_chunk"], jnp.concatenate([qe] * n_ch, axis=1), 0.0)
    return o + _mm_nt(qexp, jnp.concatenate(entering, axis=1)), state


def _hgrn_lower_bounds(l):
    out = []
    for d in (0, 1):
        l0, l1 = l[2 * d:2 * d + 1, :], l[2 * d + 1:2 * d + 2, :]
        mx = jnp.maximum(l0, l1)
        e0, e1 = jnp.exp(l0 - mx), jnp.exp(l1 - mx)
        out.append(e0 / (e0 + e1))
    return out


def _hgrn_consts(t):
    cf, cb = _np_hgrn_cums(t, False), _np_hgrn_cums(t, True)
    return (_bf(cf), _bf(cf.T), _bf(cb), _bf(cb.T), _bf(_np_blocksum(LANES)))


def _hgrn_fwd(ph, lbl, ng):
    s = ph.shape[0]
    t = min(HG_TILE, s)
    nt = s // t
    consts = _hgrn_consts(t)

    def body(xq_ref, xff_ref, xfb_ref, xi_ref, xg_ref, lbl_ref, ng_ref, cf_ref, cft_ref, cb_ref, cbt_ref, bs_ref,
             o_ref, pre_ref, st_ref, acc):
        lbf, lbb = _hgrn_lower_bounds(lbl_ref)
        mk_f, mk_b = _hgrn_masks(t, False), _hgrn_masks(t, True)
        zero = jnp.zeros((LANES, LANES), F32)

        def rows_of(i):
            return pl.ds(pl.multiple_of(i * t, t), t)

        acc[...] = jnp.zeros_like(acc)

        def step(i, states):
            tb = nt - 1 - i
            rf, rb = rows_of(i), rows_of(tb)
            st_ref[0, 0, i] = states[0]
            st_ref[0, 1, tb] = states[1]
            of, sf = _hgrn_dir(xq_ref[rf, :], xff_ref[rf, :], xi_ref[rf, :], lbf, states[0],
                               cf_ref[...], cft_ref[...], mk_f, False)
            ob, sb = _hgrn_dir(xq_ref[rb, :], xfb_ref[rb, :], xi_ref[rb, :], lbb, states[1],
                               cb_ref[...], cbt_ref[...], mk_b, True)
            acc[rf, :] += of
            acc[rb, :] += ob
            return sf, sb

        lax.fori_loop(0, nt, step, (zero, zero))

        def step_n(i, carry):
            rows = rows_of(i)
            o = acc[rows, :]
            ss = _xdot(o * o, bs_ref[...])
            r = lax.rsqrt(ss * (1.0 / HEAD_DIM) + EPS)
            xg = xg_ref[rows, :]
            pre_ref[rows, :] = o
            o_ref[rows, :] = (o * r * ng_ref[...]) * (xg * _sigmoid(xg))
            return carry

        lax.fori_loop(0, nt, step_n, 0)

    col = lambda off: pl.BlockSpec((s, LANES), lambda m: (0, off + m))
    full = lambda a: pl.BlockSpec(a.shape, lambda m: (0,) * a.ndim)
    return pl.pallas_call(
        body, name="hgrn_fwd", grid=(4,),
        in_specs=[col(0), col(4), col(8), col(12), col(16), pl.BlockSpec((4, LANES), lambda m: (0, m)),
                  pl.BlockSpec((1, LANES), lambda m: (0, m))] + [full(c) for c in consts],
        out_specs=[col(0), col(0), pl.BlockSpec((1, 2, nt, LANES, LANES), lambda m: (m, 0, 0, 0, 0))],
        out_shape=[jax.ShapeDtypeStruct((s, 512), F32), jax.ShapeDtypeStruct((s, 512), F32),
                   jax.ShapeDtypeStruct((4, 2, nt, LANES, LANES), F32)],
        scratch_shapes=[pltpu.VMEM((s, LANES), F32)],
        compiler_params=_params(1),
    )(ph, ph, ph, ph, ph, lbl, ng, *consts)


def _hgrn_bwd(ph, pre, dout, states, lbl, ng):
    s = ph.shape[0]
    t = min(HG_TILE, s)
    nt = s // t
    consts = _hgrn_consts(t)

    def body(xq_ref, xff_ref, xfb_ref, xi_ref, xg_ref, pre_ref, do_ref, st_ref, lbl_ref, ng_ref,
             cf_ref, cft_ref, cb_ref, cbt_ref, bs_ref,
             dq_ref, dff_ref, dfb_ref, di_ref, dg_ref, dlb_ref, dng_ref, dpre, dq_acc, dv_acc):
        lbf, lbb = _hgrn_lower_bounds(lbl_ref)
        mk_f, mk_b = _hgrn_masks(t, False), _hgrn_masks(t, True)
        zero = jnp.zeros((LANES, LANES), F32)
        zrow = jnp.zeros((1, LANES), F32)

        def rows_of(i):
            return pl.ds(pl.multiple_of(i * t, t), t)

        def step_n(i, dng):
            rows = rows_of(i)
            o, xg, do = pre_ref[rows, :], xg_ref[rows, :], do_ref[rows, :]
            bs = bs_ref[...]
            r = lax.rsqrt(_xdot(o * o, bs) * (1.0 / HEAD_DIM) + EPS)
            sg = _sigmoid(xg)
            gate = xg * sg
            don = do * gate
            dg_ref[rows, :] = (do * (o * r * ng_ref[...]) * (sg * (1.0 + xg * (1.0 - sg)))).astype(CD)
            u = don * ng_ref[...]
            dpre[rows, :] = r * u - o * (r * r * r) * (_xdot(u * o, bs) * (1.0 / HEAD_DIM))
            return dng + jnp.sum(don * o * r, axis=0, keepdims=True)

        dng_ref[...] = lax.fori_loop(0, nt, step_n, zrow)

        dq_acc[...] = jnp.zeros_like(dq_acc)
        dv_acc[...] = jnp.zeros_like(dv_acc)

        def grad_tile(ti, xf_ref, df_ref, lb, cm, cmt, mk, rev, st, dstate):
            rows = rows_of(ti)
            fn = lambda xq, xf, v, lbv, s_in: _hgrn_dir(xq, xf, v, lbv, s_in, cm, cmt, mk, rev)
            _, vjp = jax.vjp(fn, xq_ref[rows, :], xf_ref[rows, :], xi_ref[rows, :], lb, st)
            dxq, dxf, dv, dlb_t, dstate = vjp((dpre[rows, :], dstate))
            df_ref[rows, :] = dxf.astype(CD)
            dq_acc[rows, :] += dxq
            dv_acc[rows, :] += dv
            return dstate, dlb_t

        def step_g(i, carry):
            dsf, dsb, dlbf, dlbb = carry
            tf, tb = nt - 1 - i, i
            dsf, gf = grad_tile(tf, xff_ref, dff_ref, lbf, cf_ref[...], cft_ref[...], mk_f, False, st_ref[0, 0, tf], dsf)
            dsb, gb = grad_tile(tb, xfb_ref, dfb_ref, lbb, cb_ref[...], cbt_ref[...], mk_b, True, st_ref[0, 1, tb], dsb)
            return dsf, dsb, dlbf + gf, dlbb + gb

        _, _, dlbf, dlbb = lax.fori_loop(0, nt, step_g, (zero, zero, zrow, zrow))
        dlb_ref[0:1, :] = dlbf
        dlb_ref[1:2, :] = dlbb
        dq_ref[...] = dq_acc[...].astype(CD)
        di_ref[...] = dv_acc[...].astype(CD)

    col = lambda off: pl.BlockSpec((s, LANES), lambda m: (0, off + m))
    full = lambda a: pl.BlockSpec(a.shape, lambda m: (0,) * a.ndim)
    stream = jax.ShapeDtypeStruct((s, 512), CD)
    return pl.pallas_call(
        body, name="hgrn_bwd", grid=(4,),
        in_specs=[col(0), col(4), col(8), col(12), col(16), col(0), col(0),
                  pl.BlockSpec((1, 2, nt, LANES, LANES), lambda m: (m, 0, 0, 0, 0)),
                  pl.BlockSpec((4, LANES), lambda m: (0, m)),
                  pl.BlockSpec((1, LANES), lambda m: (0, m))] + [full(c) for c in consts],
        out_specs=[col(0)] * 5 + [pl.BlockSpec((2, LANES), lambda m: (0, m)), pl.BlockSpec((1, LANES), lambda m: (0, m))],
        out_shape=[stream] * 5 + [jax.ShapeDtypeStruct((2, 512), F32), jax.ShapeDtypeStruct((1, 512), F32)],
        scratch_shapes=[pltpu.VMEM((s, LANES), F32), pltpu.VMEM((s, LANES), F32), pltpu.VMEM((s, LANES), F32)],
        compiler_params=_params(1),
    )(ph, ph, ph, ph, ph, pre, dout, states, lbl, ng, *consts)


def _branch_out(o, w4):
    return jnp.concatenate([_mm(o, w4[j], NN) for j in range(N_SHARD)], axis=1)


def _mix_out_fwd(x, oa, ob, pg, wa, wb, wo, tm):
    s, d = x.shape

    def body(x_ref, oa_ref, ob_ref, ga_ref, gb_ref, wa_ref, wb_ref, wo_ref, xo_ref):
        ya = _branch_out(oa_ref[...], wa_ref)
        yb = _branch_out(ob_ref[...], wb_ref)
        merged = _sigmoid(ga_ref[...]) * ya + _sigmoid(gb_ref[...]) * yb
        xo_ref[...] = x_ref[...] + _mm(merged, wo_ref[...], NN)

    row = pl.BlockSpec((tm, d), lambda i: (i, 0))
    half = pl.BlockSpec((tm, 512), lambda i: (i, 0))
    full = lambda a: pl.BlockSpec(a.shape, lambda i: (0,) * a.ndim)
    return pl.pallas_call(
        body, name="mix_out_fwd", grid=(s // tm,),
        in_specs=[row, half, half, row, pl.BlockSpec((tm, d), lambda i: (i, 1)), full(wa), full(wb), full(wo)],
        out_specs=row, out_shape=jax.ShapeDtypeStruct((s, d), F32),
        compiler_params=_params(1),
    )(x, oa, ob, pg, pg, wa, wb, wo)


def _mix_out_bwd(dx, oa, ob, pg, wa, wb, wo, tm, after=()):
    s, d = dx.shape
    bs512, eq, ebc = _bf(_np_blocksum(512)), _bf(_np_expand_q()), _bf(_np_bcast_head())

    def body(*refs):
        (dx_ref, oa_ref, ob_ref, ga_ref, gb_ref, wa_ref, wb_ref, wo_ref, bs_ref, eq_ref, ebc_ref,
         dpg_ref, mg_ref, dya_ref, dyb_ref, doe_ref, dl_ref, dob_ref) = refs[len(after):]
        oa = oa_ref[...]
        ya = _branch_out(oa, wa_ref)
        yb = _branch_out(ob_ref[...], wb_ref)
        sa, sb = _sigmoid(ga_ref[...]), _sigmoid(gb_ref[...])
        mg_ref[...] = (sa * ya + sb * yb).astype(CD)
        dm = _mm(dx_ref[...], wo_ref[...], NT)
        dpg_ref[...] = jnp.concatenate([dm * ya * sa * (1.0 - sa), dm * yb * sb * (1.0 - sb)], axis=1).astype(CD)
        dya, dyb = dm * sa, dm * sb
        dya_ref[...] = dya.astype(CD)
        dyb_ref[...] = dyb.astype(CD)
        doa = jnp.zeros(oa.shape, F32)
        dob = jnp.zeros(oa.shape, F32)
        for j in range(N_SHARD):
            doa = doa + _mm(dya[:, 256 * j:256 * j + 256], wa_ref[j], NT)
            dob = dob + _mm(dyb[:, 256 * j:256 * j + 256], wb_ref[j], NT)
        dob_ref[...] = dob
        doe_ref[...] = _mm(doa, eq_ref[...], NN).astype(CD)
        dl_ref[...] = _xdot(_xdot(doa * oa, bs_ref[...]), ebc_ref[...])

    row = pl.BlockSpec((tm, d), lambda i: (i, 0))
    half = pl.BlockSpec((tm, 512), lambda i: (i, 0))
    full = lambda a: pl.BlockSpec(a.shape, lambda i: (0,) * a.ndim)
    wide = jax.ShapeDtypeStruct((s, d), CD)
    return pl.pallas_call(
        body, name="mix_out_bwd", grid=(s // tm,),
        in_specs=[ANY] * len(after) + [row, half, half, row, pl.BlockSpec((tm, d), lambda i: (i, 1)), full(wa), full(wb),
                                       full(wo), full(bs512), full(eq), full(ebc)],
        out_specs=[pl.BlockSpec((tm, 2048), lambda i: (i, 0)), row, row, row, row, row, half],
        out_shape=[jax.ShapeDtypeStruct((s, 2048), CD), wide, wide, wide, wide, jax.ShapeDtypeStruct((s, d), F32),
                   jax.ShapeDtypeStruct((s, 512), F32)],
        compiler_params=_params(1),
    )(*after, dx, oa, ob, pg, pg, wa, wb, wo, bs512, eq, ebc)


def _loss_head(x, g, target, tm):
    s, d = x.shape

    def body(x_ref, g_ref, t_ref, dx_ref, loss_ref, dg_ref):
        @pl.when(pl.program_id(0) == 0)
        def _():
            loss_ref[...] = jnp.zeros_like(loss_ref)
            dg_ref[...] = jnp.zeros_like(dg_ref)

        xv = x_ref[...]
        r = lax.rsqrt(jnp.mean(xv * xv, axis=-1, keepdims=True) + EPS)
        err = xv * r * g_ref[...] - t_ref[...]
        loss_ref[...] += 0.5 * jnp.sum(jnp.mean(err * err, axis=-1, keepdims=True))
        dy = err * (1.0 / d)
        u = dy * g_ref[...]
        dx_ref[...] = r * u - xv * (r * r * r) * jnp.mean(u * xv, axis=-1, keepdims=True)
        dg_ref[...] += jnp.sum(dy * xv * r, axis=0, keepdims=True)

    row = pl.BlockSpec((tm, d), lambda i: (i, 0))
    vec = pl.BlockSpec((1, d), lambda i: (0, 0))
    return pl.pallas_call(
        body, name="loss_head", grid=(s // tm,),
        in_specs=[row, vec, row], out_specs=[row, pl.BlockSpec((8, LANES), lambda i: (0, 0)), vec],
        out_shape=[jax.ShapeDtypeStruct((s, d), F32), jax.ShapeDtypeStruct((8, LANES), F32),
                   jax.ShapeDtypeStruct((1, d), F32)],
        compiler_params=_params(1),
    )(x, g, target)


def _position():
    x, y, c = lax.axis_index("x"), lax.axis_index("y"), lax.axis_index("c")
    return x, y, c, [(1 - x, y), (x, 1 - y), (1 - x, 1 - y)]


def _row_tile(rows, cap=256):
    best = rows
    for cand in range(8, min(rows, cap) + 1, 8):
        if rows % cand == 0:
            best = cand
    return best


def _cast_into_slot(shard, me_idx, dtype):
    rows, cols = shard.shape
    tr = _row_tile(rows)

    def body(me_ref, src_ref, out_ref):
        out_ref[0] = src_ref[...].astype(dtype)

    return pl.pallas_call(
        body, name="cast_into_slot",
        grid_spec=pltpu.PrefetchScalarGridSpec(
            num_scalar_prefetch=1, grid=(rows // tr,),
            in_specs=[pl.BlockSpec((tr, cols), lambda i, me: (i, 0))],
            out_specs=pl.BlockSpec((1, tr, cols), lambda i, me: (me[0], i, 0))),
        out_shape=jax.ShapeDtypeStruct((N_SHARD, rows, cols), dtype),
        compiler_params=_params(1),
    )(me_idx, shard)


HBM_SPEC = pl.BlockSpec(memory_space=pltpu.HBM)
SEM_SPEC = pl.BlockSpec(memory_space=pltpu.SEMAPHORE)
DATAFLOW = pltpu.SideEffectType.DATAFLOW_SIDE_EFFECTING


def _exchange_copies(srcs, lands, send, recv, gather):
    x, y, c, chips = _position()
    me = 2 * x + y
    out = []
    for a in range(len(lands)):
        dst = lands[a].at[me]
        if gather and _halved(lands[a]):
            half = lands[a].shape[1] // 2
            dst = lands[a].at[me, pl.ds(c * half, half), :]
        for k, (px, py) in enumerate(chips):
            src = dst if gather else srcs[a].at[2 * px + py]
            out.append(pltpu.make_async_remote_copy(src_ref=src, dst_ref=dst, send_sem=send.at[3 * a + k],
                                                    recv_sem=recv.at[3 * a + k], device_id=(px, py, c), device_id_type=MESH))
    return out


def _halved(land):
    return land.shape[1] % 32 == 0


def _pair_fill(name, lands):
    n = len(lands)

    def body(*refs):
        src, dst = refs[:n], refs[n:2 * n]
        send, recv = refs[2 * n:]
        x, y, c, chips = _position()
        copies = []
        for a in range(n):
            half = src[a].shape[1] // 2
            for k, (px, py) in enumerate(chips):
                rows = (2 * px + py, pl.ds(c * half, half), slice(None))
                cp = pltpu.make_async_remote_copy(src_ref=src[a].at[rows], dst_ref=dst[a].at[rows], send_sem=send.at[a, k],
                                                  recv_sem=recv.at[a, k], device_id=(x, y, 1 - c), device_id_type=MESH)
                cp.start()
                copies.append(cp)
        for cp in copies:
            cp.wait()

    return pl.pallas_call(
        body, name=name, in_specs=[ANY] * n, out_specs=[ANY] * n,
        out_shape=[jax.ShapeDtypeStruct(l.shape, l.dtype) for l in lands],
        input_output_aliases={a: a for a in range(n)},
        scratch_shapes=[pltpu.SemaphoreType.DMA((n, 3)), pltpu.SemaphoreType.DMA((n, 3))],
    )(*lands)


def _exchange_start(name, srcs, lands, after):
    ns, nl, na = len(srcs), len(lands), len(after)
    gather = ns == 0

    def body(*refs):
        src_refs, land_refs = refs[:ns], refs[ns:ns + nl]
        send, recv = refs[ns + nl + na], refs[ns + nl + na + 1]
        token = refs[-1]
        for cp in _exchange_copies(src_refs, land_refs, send, recv, gather):
            cp.start()
        token[...] = jnp.zeros_like(token)

    arrays = [pltpu.with_memory_space_constraint(a, pltpu.HBM) for a in list(srcs) + list(lands)]
    outs = pl.pallas_call(
        body, name=name,
        out_shape=(pltpu.SemaphoreType.DMA((3 * nl,)), pltpu.SemaphoreType.DMA((3 * nl,)),
                   *[pltpu.HBM(a.shape, a.dtype) for a in arrays], jax.ShapeDtypeStruct((8, LANES), F32)),
        in_specs=[HBM_SPEC] * (ns + nl) + [ANY] * na,
        out_specs=(SEM_SPEC, SEM_SPEC, *[HBM_SPEC] * (ns + nl), pl.BlockSpec(memory_space=pltpu.VMEM)),
        input_output_aliases={i: 2 + i for i in range(ns + nl)},
        compiler_params=pltpu.CompilerParams(has_side_effects=DATAFLOW),
    )(*arrays, *after)
    return outs[0], outs[1], list(outs[2:2 + ns]), list(outs[2 + ns:2 + ns + nl]), outs[-1]


def _exchange_wait(name, send, recv, srcs, lands, after):
    ns, nl, na = len(srcs), len(lands), len(after)
    gather = ns == 0

    def body(*refs):
        src_refs, land_refs = refs[:ns], refs[ns:ns + nl]
        send_ref, recv_ref = refs[ns + nl], refs[ns + nl + 1]
        for cp in _exchange_copies(src_refs, land_refs, send_ref, recv_ref, gather):
            cp.wait_send()
            cp.wait_recv()

    outs = pl.pallas_call(
        body, name=name,
        out_shape=tuple(pltpu.HBM(a.shape, a.dtype) for a in list(srcs) + list(lands)),
        in_specs=[HBM_SPEC] * (ns + nl) + [SEM_SPEC, SEM_SPEC] + [ANY] * na,
        out_specs=tuple([HBM_SPEC] * (ns + nl)),
        input_output_aliases={i: i for i in range(ns + nl)},
        compiler_params=pltpu.CompilerParams(has_side_effects=DATAFLOW),
    )(*srcs, *lands, send, recv, *after)
    return list(outs[ns:])


def _pair_exchange(grads):
    n = len(grads)

    def body(*refs):
        src, dst = refs[:n], refs[n:2 * n]
        send, recv = refs[2 * n:]
        x, y, c, _ = _position()
        copies = []
        for a in range(n):
            half = src[a].shape[1] // 2
            cp = pltpu.make_async_remote_copy(
                src_ref=src[a].at[:, pl.ds((1 - c) * half, half), :], dst_ref=dst[a], send_sem=send.at[a],
                recv_sem=recv.at[a], device_id=(x, y, 1 - c), device_id_type=MESH)
            cp.start()
            copies.append(cp)
        for cp in copies:
            cp.wait()

    return pl.pallas_call(
        body, name="grad_pair_exchange", in_specs=[ANY] * n, out_specs=[ANY] * n,
        out_shape=[jax.ShapeDtypeStruct((g.shape[0], g.shape[1] // 2, g.shape[2]), g.dtype) for g in grads],
        scratch_shapes=[pltpu.SemaphoreType.DMA((n,)), pltpu.SemaphoreType.DMA((n,))],
    )(*grads)


def _pair_sum(g, got, c_idx, me_idx):
    nsh, rows, cols = g.shape
    half = rows // 2

    def body(c_ref, me_ref, g_ref, got_ref, s_ref, own_ref):
        sm = g_ref[...] + got_ref[...].astype(F32)
        s_ref[...] = sm.astype(CD)

        @pl.when(pl.program_id(0) == me_ref[0])
        def _():
            own_ref[...] = sm[0]

    return pl.pallas_call(
        body, name="grad_pair_sum",
        grid_spec=pltpu.PrefetchScalarGridSpec(
            num_scalar_prefetch=2, grid=(nsh,),
            in_specs=[pl.BlockSpec((1, half, cols), lambda j, c, me: (j, c[0], 0)),
                      pl.BlockSpec((1, half, cols), lambda j, c, me: (j, 0, 0))],
            out_specs=[pl.BlockSpec((1, half, cols), lambda j, c, me: (j, 0, 0)),
                       pl.BlockSpec((half, cols), lambda j, c, me: (0, 0))]),
        out_shape=[jax.ShapeDtypeStruct((nsh, half, cols), CD), jax.ShapeDtypeStruct((half, cols), F32)],
        compiler_params=_params(1),
    )(c_idx, me_idx, g, got)


def _chip_sum(own, got, me_idx):
    nsh, half, cols = got.shape

    def body(me_ref, own_ref, got_ref, out_ref):
        j = pl.program_id(0)
        term = jnp.where(j == me_ref[0], own_ref[...], got_ref[0].astype(F32))

        @pl.when(j == 0)
        def _():
            out_ref[...] = term

        @pl.when(j > 0)
        def _():
            out_ref[...] += term

    return pl.pallas_call(
        body, name="grad_chip_sum",
        grid_spec=pltpu.PrefetchScalarGridSpec(
            num_scalar_prefetch=1, grid=(nsh,),
            in_specs=[pl.BlockSpec((half, cols), lambda j, me: (0, 0)), pl.BlockSpec((1, half, cols), lambda j, me: (j, 0, 0))],
            out_specs=pl.BlockSpec((half, cols), lambda j, me: (0, 0))),
        out_shape=jax.ShapeDtypeStruct((half, cols), F32),
        compiler_params=_params(1),
    )(me_idx, own, got)


def _pair_share(halves):
    n = len(halves)

    def body(*refs):
        src, dst = refs[:n], refs[n:2 * n]
        send, recv = refs[2 * n:]
        x, y, c, _ = _position()
        copies = []
        for a in range(n):
            cp = pltpu.make_async_remote_copy(src_ref=src[a], dst_ref=dst[a], send_sem=send.at[a],
                                              recv_sem=recv.at[a], device_id=(x, y, 1 - c), device_id_type=MESH)
            cp.start()
            copies.append(cp)
        for cp in copies:
            cp.wait()

    return pl.pallas_call(
        body, name="grad_pair_share", in_specs=[ANY] * n, out_specs=[ANY] * n,
        out_shape=[jax.ShapeDtypeStruct(h.shape, h.dtype) for h in halves],
        scratch_shapes=[pltpu.SemaphoreType.DMA((n,)), pltpu.SemaphoreType.DMA((n,))],
    )(*halves)


def _small_allreduce(buf):
    rows, cols = buf.shape

    def body(src_ref, out_ref, slots, send, recv):
        x, y, c, _ = _position()
        me = 4 * x + 2 * y + c
        slots[me] = src_ref[...]
        copies = []
        k = 0
        for dx in (0, 1):
            for dy in (0, 1):
                for dc in (0, 1):
                    if (dx, dy, dc) == (0, 0, 0):
                        continue
                    peer = (jnp.where(dx, 1 - x, x), jnp.where(dy, 1 - y, y), jnp.where(dc, 1 - c, c))
                    cp = pltpu.make_async_remote_copy(src_ref=src_ref, dst_ref=slots.at[me], send_sem=send.at[k],
                                                      recv_sem=recv.at[k], device_id=peer, device_id_type=MESH)
                    cp.start()
                    copies.append(cp)
                    k += 1
        for cp in copies:
            cp.wait()
        total = slots[0]
        for dev in range(1, N_DEV):
            total = total + slots[dev]
        out_ref[...] = total

    vm = pl.BlockSpec(memory_space=pltpu.VMEM)
    return pl.pallas_call(
        body, name="small_allreduce", in_specs=[vm], out_specs=vm,
        out_shape=jax.ShapeDtypeStruct((rows, cols), F32),
        scratch_shapes=[pltpu.VMEM((N_DEV, rows, cols), F32), pltpu.SemaphoreType.DMA((N_DEV - 1,)),
                        pltpu.SemaphoreType.DMA((N_DEV - 1,))],
    )(buf)


def _adamw_math(w, gv, m, v):
    mn = ADAM_B1 * m + (1.0 - ADAM_B1) * gv
    vn = ADAM_B2 * v + (1.0 - ADAM_B2) * (gv * gv)
    m_hat = mn / (1.0 - ADAM_B1 ** ADAM_STEP)
    v_hat = vn / (1.0 - ADAM_B2 ** ADAM_STEP)
    return -ADAM_LR * (m_hat / (jnp.sqrt(v_hat) + ADAM_EPS) + ADAM_WD * w), mn, vn


def _adamw(w, g, m, v):
    rows, cols = w.shape
    tr = _row_tile(rows)

    def body(w_ref, g_ref, m_ref, v_ref, d_ref, mo_ref, vo_ref):
        d_ref[...], mo_ref[...], vo_ref[...] = _adamw_math(w_ref[...], g_ref[...], m_ref[...], v_ref[...])

    blk = pl.BlockSpec((tr, cols), lambda i: (i, 0))
    shp = jax.ShapeDtypeStruct((rows, cols), F32)
    return pl.pallas_call(
        body, name="adamw", grid=(rows // tr,), in_specs=[blk] * 4, out_specs=[blk] * 3, out_shape=[shp] * 3,
        compiler_params=_params(1),
    )(w, g, m, v)


def _adamw_halves(w, own, got, m, v, c_idx):
    rows, cols = w.shape
    tr = _row_tile(rows // 2)
    per_half = rows // 2 // tr

    def body(c_ref, w_ref, own_ref, got_ref, m_ref, v_ref, d_ref, mo_ref, vo_ref, g_ref):
        mine = (pl.program_id(0) // per_half) == c_ref[0]
        gv = jnp.where(mine, own_ref[...], got_ref[...])
        g_ref[...] = gv
        d_ref[...], mo_ref[...], vo_ref[...] = _adamw_math(w_ref[...], gv, m_ref[...], v_ref[...])

    blk = pl.BlockSpec((tr, cols), lambda i, c: (i, 0))
    own_blk = pl.BlockSpec((tr, cols), lambda i, c: (jnp.where(i // per_half == c[0], i % per_half, 0), 0))
    got_blk = pl.BlockSpec((tr, cols), lambda i, c: (jnp.where(i // per_half == c[0], 0, i % per_half), 0))
    shp = jax.ShapeDtypeStruct((rows, cols), F32)
    return pl.pallas_call(
        body, name="adamw_halves",
        grid_spec=pltpu.PrefetchScalarGridSpec(num_scalar_prefetch=1, grid=(rows // tr,),
                                               in_specs=[blk, own_blk, got_blk, blk, blk], out_specs=[blk] * 4),
        out_shape=[shp] * 4, compiler_params=_params(1),
    )(c_idx, w, own, got, m, v)


def _local_step(x, target, norm_gains, q_g, k_g, ng, weights_of, grads_done):
    s = x.shape[0]
    tm = min(512, s)
    tq = min(256, s)
    g1, gm, g2, gf = norm_gains
    cos2, sin2 = _rope_tables(s)
    gq8 = jnp.tile(q_g, (1, 8))
    gk2 = jnp.tile(k_g, (1, 2))

    tn = min(256, s)
    tk = min(1024, s)
    w1 = weights_of(1, ())
    x1, s1, t1, b1, h1 = _ffn_fwd(x, g1, w1["g1"], w1["u1"], w1["d1"], tm)
    w2 = weights_of(2, (x1,))
    lbl = w2["lbl"]
    pqkv, ph, pg, hm = _mix_in_fwd(x1, gm, w2["in"], tn)
    qe, kr, vr, vs = _qk_prep(pqkv, gq8, gk2, cos2, sin2, tm)
    oa, lse = _attn_fwd(qe, kr, vr, vs, tq)
    ob, pre, hstates = _hgrn_fwd(ph, lbl, ng)
    x2 = _mix_out_fwd(x1, oa, ob, pg, w2["a"], w2["b"], w2["o"], tm)
    w3 = weights_of(3, (x2,))
    x3, s2, t2, b2, h2 = _ffn_fwd(x2, g2, w3["g2"], w3["u2"], w3["d2"], tm)
    dx3, loss, dgf = _loss_head(x3, gf, target, tm)

    dx2, da2, db2, f2, dg2, dx3c = _ffn_bwd(dx3, x2, g2, s2, t2, b2, w3["g2"], w3["u2"], w3["d2"], tm)
    tok = grads_done(3, dict(g2=_dw_shared_b("dw_gate", da2, h2, tk, 1.0), u2=_dw_shared_b("dw_gate", db2, h2, tk, 1.0),
                             d2=_dw_shared_b("dw_down", f2, dx3c, tk, 0.5)))

    dpg, mg, dya, dyb, doe, delta, dob = _mix_out_bwd(dx2, oa, ob, pg, w2["a"], w2["b"], w2["o"], tm, tok)
    g_o = [g.reshape(N_SHARD, D_MODEL // N_SHARD, D_MODEL) for g in _dw_colblocks("dw_out", mg, dx2, 1, tk)]
    g_a = _dw_colblocks("dw_branch", oa, dya, N_SHARD, tk)
    g_b = _dw_colblocks("dw_branch", ob, dyb, N_SHARD, tk)
    dqe, dk, dv = _attn_bwd(qe, kr, vr, doe, delta, lse, tq)
    dqkv, dgq, dgk = _qk_prep_bwd(pqkv, dqe, dk, dv, gq8, gk2, cos2, sin2, tm)
    dhq, dhff, dhfb, dhi, dhg, dlb, dng = _hgrn_bwd(ph, pre, dob, hstates, lbl, ng)
    dps = (dqkv, dhq, dhff, dhfb, dhi, dhg, dpg)
    g_in = [g.reshape(N_SHARD, -1, D_MODEL) for g in _dw_in(dps, hm, min(2048, s))]
    tok = grads_done(2, {"in": g_in, "a": g_a, "b": g_b, "o": g_o})
    dx1, dgm = _mix_in_bwd(dps, w2["in"], x1, dx2, gm, tn, tok)

    dx0, da1, db1, f1, dg1, dx1c = _ffn_bwd(dx1, x, g1, s1, t1, b1, w1["g1"], w1["u1"], w1["d1"], tm)
    grads_done(1, dict(g1=_dw_shared_b("dw_gate", da1, h1, tk, 1.0), u1=_dw_shared_b("dw_gate", db1, h1, tk, 1.0),
                       d1=_dw_shared_b("dw_down", f1, dx1c, tk, 0.5)))
    small = dict(g1=dg1, gm=dgm, g2=dg2, gf=dgf, gq=dgq, gk=dgk, lb=dlb, ng=dng)
    return loss, dx0, small, lbl


GROUPS = {1: ("g1", "u1", "d1"), 2: ("in", "a", "b", "o"), 3: ("g2", "u2", "d2")}
BIG = GROUPS[1] + GROUPS[2] + GROUPS[3]
TRANSPOSED = ("g1", "u1", "in", "g2", "u2")


def _pack_rows(vectors, width):
    rows = []
    for vct in vectors:
        flat = vct.reshape(-1)
        pad = (-flat.shape[0]) % width
        rows.append(jnp.pad(flat, (0, pad)).reshape(-1, width))
    return jnp.concatenate(rows, axis=0)


def kernel(x, ffn1_norm_g, ffn1_w_gate, ffn1_w_up, ffn1_w_down, mix_norm_g, w_in, q_norm_g, k_norm_g, hgrn_lb_logits, hgrn_out_norm_g, w_branch_attn, w_branch_hgrn, w_out, ffn2_norm_g, ffn2_w_gate, ffn2_w_up, ffn2_w_down, final_norm_g, loss_target, m_ffn1_norm_g, m_ffn1_w_gate, m_ffn1_w_up, m_ffn1_w_down, m_mix_norm_g, m_w_in, m_q_norm_g, m_k_norm_g, m_hgrn_lb_logits, m_hgrn_out_norm_g, m_w_branch_attn, m_w_branch_hgrn, m_w_out, m_ffn2_norm_g, m_ffn2_w_gate, m_ffn2_w_up, m_ffn2_w_down, m_final_norm_g, v_ffn1_norm_g, v_ffn1_w_gate, v_ffn1_w_up, v_ffn1_w_down, v_mix_norm_g, v_w_in, v_q_norm_g, v_k_norm_g, v_hgrn_lb_logits, v_hgrn_out_norm_g, v_w_branch_attn, v_w_branch_hgrn, v_w_out, v_ffn2_norm_g, v_ffn2_w_gate, v_ffn2_w_up, v_ffn2_w_down, v_final_norm_g):
    xi, yi, ci = lax.axis_index("x"), lax.axis_index("y"), lax.axis_index("c")
    me = 2 * xi + yi
    c_idx = jnp.reshape(ci, (1,)).astype(jnp.int32)
    me_idx = jnp.reshape(me, (1,)).astype(jnp.int32)

    big_w = dict(g1=ffn1_w_gate[0], u1=ffn1_w_up[0], d1=ffn1_w_down[0], a=w_branch_attn[0], b=w_branch_hgrn[0],
                 o=w_out[0], g2=ffn2_w_gate[0], u2=ffn2_w_up[0], d2=ffn2_w_down[0])
    big_w["in"] = w_in[0]
    big_m = dict(g1=m_ffn1_w_gate[0], u1=m_ffn1_w_up[0], d1=m_ffn1_w_down[0], a=m_w_branch_attn[0], b=m_w_branch_hgrn[0],
                 o=m_w_out[0], g2=m_ffn2_w_gate[0], u2=m_ffn2_w_up[0], d2=m_ffn2_w_down[0])
    big_m["in"] = m_w_in[0]
    big_v = dict(g1=v_ffn1_w_gate[0], u1=v_ffn1_w_up[0], d1=v_ffn1_w_down[0], a=v_w_branch_attn[0], b=v_w_branch_hgrn[0],
                 o=v_w_out[0], g2=v_ffn2_w_gate[0], u2=v_ffn2_w_up[0], d2=v_ffn2_w_down[0])
    big_v["in"] = v_w_in[0]
    for table in (big_w, big_m, big_v):
        for n in TRANSPOSED:
            table[n] = table[n].T

    slots = {n: _cast_into_slot(big_w[n], me_idx, CD) for n in BIG}
    lbl_slot = _cast_into_slot(hgrn_lb_logits.reshape(4, LANES), me_idx, F32)
    started, token = {}, ()
    for grp in (1, 2, 3):
        lands = [slots[n] for n in GROUPS[grp]] + ([lbl_slot] if grp == 2 else [])
        send, recv, _, lands, tok = _exchange_start("gather%d_start" % grp, [], lands, token)
        started[grp], token = (send, recv, lands), (tok,)

    def weights_of(grp, after):
        send, recv, lands = started[grp]
        got = _exchange_wait("gather%d_wait" % grp, send, recv, [], lands, tuple(after) + (token if grp == 1 else ()))
        by_halves = [i for i, land in enumerate(got) if _halved(land)]
        for i, whole in zip(by_halves, _pair_fill("gather%d_fill" % grp, [got[i] for i in by_halves])):
            got[i] = whole
        w = dict(zip(GROUPS[grp], got))
        if grp == 2:
            w["in"] = w["in"].reshape(-1, D_MODEL)
            w["o"] = w["o"].reshape(D_MODEL, D_MODEL)
            w["lbl"] = jnp.transpose(got[-1], (1, 0, 2)).reshape(4, N_SHARD * LANES)
        return w

    pending = {}

    def grads_done(grp, grads):
        names = list(grads)
        got = _pair_exchange([grads[n][1] for n in names])
        sums, owns = zip(*[_pair_sum(grads[n][0], r, c_idx, me_idx) for n, r in zip(names, got)])
        lands = [lax.empty(s_.shape, s_.dtype) for s_ in sums]
        send, recv, srcs, lands, tok = _exchange_start("reduce%d_start" % grp, list(sums), lands, ())
        pending[grp] = (names, send, recv, srcs, lands, owns, tok)
        return (tok,)

    def reduced_halves(grp, after):
        names, send, recv, srcs, lands, owns, _ = pending[grp]
        parts = _exchange_wait("reduce%d_wait" % grp, send, recv, srcs, lands, after)
        return names, [_chip_sum(o, p, me_idx) for o, p in zip(owns, parts)]

    loss, dx, small, lbl = _local_step(
        x[0], loss_target[0], (ffn1_norm_g, mix_norm_g, ffn2_norm_g, final_norm_g.reshape(1, -1)),
        q_norm_g, k_norm_g, hgrn_out_norm_g, weights_of, grads_done)

    dgq = small["gq"].reshape(8, HEAD_DIM).sum(axis=0)
    dgk = small["gk"].reshape(2, HEAD_DIM).sum(axis=0)
    lb_full = _hgrn_lower_bounds(lbl)
    dlog = []
    for d in (0, 1):
        t = small["lb"][d:d + 1] * lb_full[d] * (1.0 - lb_full[d])
        dlog += [t, -t]
    small_list = [small["g1"], small["gm"], small["g2"], small["gf"], small["ng"], dgq, dgk, jnp.concatenate(dlog, axis=0), loss[0, 0]]
    packed = _pack_rows(small_list, D_MODEL)
    n_rows = packed.shape[0]
    packed = jnp.pad(packed, ((0, (-n_rows) % 8), (0, 0)))
    red = _small_allreduce(packed)
    loss_out = red[n_rows - 1, 0]
    sg = dict(g1=red[0:1], gm=red[1:2], g2=red[2:3], gf=red[3], ng=red[4:5, :512], gq=red[5:6, :HEAD_DIM],
              gk=red[6:7, :HEAD_DIM])
    dlog_full = red[7:9].reshape(2, 2, 512)
    sg["lb"] = lax.dynamic_slice_in_dim(dlog_full, me * LANES, LANES, axis=2)

    small_w = dict(g1=ffn1_norm_g, gm=mix_norm_g, g2=ffn2_norm_g, gf=final_norm_g, ng=hgrn_out_norm_g, gq=q_norm_g,
                   gk=k_norm_g, lb=hgrn_lb_logits)
    small_m = dict(g1=m_ffn1_norm_g, gm=m_mix_norm_g, g2=m_ffn2_norm_g, gf=m_final_norm_g, ng=m_hgrn_out_norm_g,
                   gq=m_q_norm_g, gk=m_k_norm_g, lb=m_hgrn_lb_logits)
    small_v = dict(g1=v_ffn1_norm_g, gm=v_mix_norm_g, g2=v_ffn2_norm_g, gf=v_final_norm_g, ng=v_hgrn_out_norm_g,
                   gq=v_q_norm_g, gk=v_k_norm_g, lb=v_hgrn_lb_logits)
    small_names = ("g1", "gm", "g2", "gf", "ng", "gq", "gk", "lb")
    pack = lambda dct: _pack_rows([dct[n] for n in small_names], D_MODEL)
    pw, pgr, pm, pv = pack(small_w), pack(sg), pack(small_m), pack(small_v)
    pad8 = lambda a: jnp.pad(a, ((0, (-a.shape[0]) % 8), (0, 0)))
    sd, sm_, sv_ = _adamw(pad8(pw), pad8(pgr), pad8(pm), pad8(pv))

    def unpack(buf):
        out, r = {}, 0
        for n in small_names:
            size = small_w[n].size
            nr = -(-size // D_MODEL)
            out[n] = buf[r:r + nr].reshape(-1)[:size].reshape(small_w[n].shape)
            r += nr
        return out

    sdelta, snew_m, snew_v = unpack(sd), unpack(sm_), unpack(sv_)
    sgrad = {n: sg[n].reshape(small_w[n].shape) for n in small_names}

    bdelta, bnew_m, bnew_v, bgrad = {}, {}, {}, {}

    def update(names, halves):
        for n, own, got in zip(names, halves, _pair_share(halves)):
            res = _adamw_halves(big_w[n], own, got, big_m[n], big_v[n], c_idx)
            if n in TRANSPOSED:
                res = [r.T for r in res]
            bdelta[n], bnew_m[n], bnew_v[n], bgrad[n] = [r[None] for r in res]

    names3, halves3 = reduced_halves(3, (pending[1][-1],))
    names2, halves2 = reduced_halves(2, (halves3[0],))
    update(names3 + names2, halves3 + halves2)
    names1, halves1 = reduced_halves(1, (bdelta[names2[-1]],))
    update(names1, halves1)

    order = [("s", "g1"), ("b", "g1"), ("b", "u1"), ("b", "d1"), ("s", "gm"), ("b", "in"), ("s", "gq"), ("s", "gk"),
             ("s", "lb"), ("s", "ng"), ("b", "a"), ("b", "b"), ("b", "o"), ("s", "g2"), ("b", "g2"), ("b", "u2"),
             ("b", "d2"), ("s", "gf")]
    outs = [loss_out, dx[None]]
    for table_s, table_b in ((sgrad, bgrad), (sdelta, bdelta), (snew_m, bnew_m), (snew_v, bnew_v)):
        outs += [(table_s if kind == "s" else table_b)[n] for kind, n in order]
    return tuple(outs)
```

```python
import functools

import numpy as np
import jax
import jax.numpy as jnp
from jax import lax
from jax.experimental import pallas as pl
from jax.experimental.pallas import tpu as pltpu

F32 = jnp.float32
BF16 = jnp.bfloat16
CD = jnp.bfloat16

EPS = 1e-6
D_MODEL = 1024
HEAD_DIM = 64
GRID_W = 64
ROPE_THETA = 10000.0
CHUNK = 32
N_SHARD = 4
N_DEV = 8
VMEM_LIMIT = 56 * 1024 * 1024
LANES = 128
HG_TILE = 256
FFN_ROWS = 256

ADAM_LR = 0.001
ADAM_B1 = 0.9
ADAM_B2 = 0.999
ADAM_EPS = 1e-08
ADAM_WD = 0.01
ADAM_STEP = 10

NN = (((1,), (0,)), ((), ()))
NT = (((1,), (1,)), ((), ()))
TN = (((0,), (0,)), ((), ()))
MESH = pl.DeviceIdType.MESH
ANY = pl.BlockSpec(memory_space=pl.ANY)


def _mm(a, b, dn):
    return lax.dot_general(a.astype(CD), b.astype(CD), dn, preferred_element_type=F32)


def _split3(x):
    hi = x.astype(BF16)
    r = x - hi.astype(F32)
    mid = r.astype(BF16)
    lo = (r - mid.astype(F32)).astype(BF16)
    return hi, mid, lo


def _xdot(x, m):
    rows = x.shape[0]
    r = lax.dot_general(jnp.concatenate(_split3(x), axis=0), m, NN, preferred_element_type=F32)
    return r[:rows] + r[rows:2 * rows] + r[2 * rows:]


def _xdot_l(m, x):
    cols = x.shape[1]
    r = lax.dot_general(m, jnp.concatenate(_split3(x), axis=1), NN, preferred_element_type=F32)
    return r[:, :cols] + r[:, cols:2 * cols] + r[:, 2 * cols:]


def _params(n_grid):
    return pltpu.CompilerParams(dimension_semantics=("arbitrary",) * n_grid, vmem_limit_bytes=VMEM_LIMIT)


def _sigmoid(x):
    return jax.nn.sigmoid(x)


def _np_blocksum(n):
    i = np.arange(n)
    return (i[:, None] // HEAD_DIM == i[None, :] // HEAD_DIM).astype(np.float32)


def _np_swap32(n):
    i = np.arange(n)
    partner = np.where(i % HEAD_DIM < HEAD_DIM // 2, i + HEAD_DIM // 2, i - HEAD_DIM // 2)
    m = np.zeros((n, n), np.float32)
    m[i, partner] = 1.0
    return m


def _np_expand_q():
    m = np.zeros((512, 1024), np.float32)
    for h in range(8):
        g = h // 4
        for d in range(HEAD_DIM):
            m[64 * h + d, 128 * h + 64 * g + d] = 1.0
    return m


def _np_bcast_head():
    m = np.zeros((512, 1024), np.float32)
    for h in range(8):
        m[64 * h, 128 * h:128 * h + 128] = 1.0
    return m


def _np_swap_halves():
    m = np.zeros((128, 128), np.float32)
    i = np.arange(128)
    m[i, (i + 64) % 128] = 1.0
    return m


def _np_hgrn_cums(t, rev):
    r = np.arange(t)[:, None]
    c = np.arange(t)[None, :]
    same = (r // CHUNK) == (c // CHUNK)
    if not rev:
        cum = same & (c <= r)
        mid = same & (c % CHUNK <= CHUNK // 2 - 1)
    else:
        cum = same & (c >= r)
        mid = same & (c % CHUNK >= CHUNK // 2)
    return np.concatenate([cum, mid, same], axis=0).astype(np.float32)


def _bf(a):
    return jnp.asarray(a, dtype=BF16)


def _rope_tables(seq_len):
    rows = seq_len // GRID_W
    row = jnp.repeat(jnp.arange(rows, dtype=F32), GRID_W)
    col = jnp.tile(jnp.arange(GRID_W, dtype=F32), rows)
    n_freq = HEAD_DIM // 4
    inv = ROPE_THETA ** (-jnp.arange(n_freq, dtype=F32) / n_freq)
    ang = jnp.concatenate([row[:, None] * inv, col[:, None] * inv], axis=-1)
    cos, sin = jnp.cos(ang), jnp.sin(ang)
    c64 = jnp.concatenate([cos, cos], axis=-1)
    s64 = jnp.concatenate([-sin, sin], axis=-1)
    return jnp.tile(c64, (1, 2)), jnp.tile(s64, (1, 2))


def _ffn_fwd(x, g, wg, wu, wd, tm):
    s, d = x.shape
    nsh, fs, _ = wg.shape

    def body(x_ref, g_ref, wg_ref, wu_ref, wd_ref, xo_ref, a_ref, da_ref, b_ref, hb_ref, acc, hs):
        j = pl.program_id(1)

        @pl.when(j == 0)
        def _():
            xv = x_ref[...]
            r = lax.rsqrt(jnp.mean(xv * xv, axis=-1, keepdims=True) + EPS)
            h = (xv * r * g_ref[...]).astype(CD)
            hs[...] = h
            hb_ref[...] = h
            acc[...] = jnp.zeros_like(acc)

        for r0 in range(0, tm, FFN_ROWS):
            rows = slice(r0, min(r0 + FFN_ROWS, tm))
            h = hs[rows, :]
            a = _mm(h, wg_ref[0], NT)
            b = _mm(h, wu_ref[0], NT)
            sg = _sigmoid(a)
            silu = a * sg
            acc[rows, :] += _mm(silu * b, wd_ref[0], NN)
            a_ref[0, rows, :] = silu.astype(CD)
            da_ref[0, rows, :] = (sg * (1.0 + a * (1.0 - sg))).astype(CD)
            b_ref[0, rows, :] = b.astype(CD)

        @pl.when(j == nsh - 1)
        def _():
            xo_ref[...] = x_ref[...] + 0.5 * acc[...]

    return pl.pallas_call(
        body, name="ffn_fwd", grid=(s // tm, nsh),
        in_specs=[pl.BlockSpec((tm, d), lambda i, j: (i, 0)), pl.BlockSpec((1, d), lambda i, j: (0, 0))]
        + [pl.BlockSpec((1, fs, d), lambda i, j: (j, 0, 0))] * 3,
        out_specs=[pl.BlockSpec((tm, d), lambda i, j: (i, 0))] + [pl.BlockSpec((1, tm, fs), lambda i, j: (j, i, 0))] * 3
        + [pl.BlockSpec((tm, d), lambda i, j: (i, 0))],
        out_shape=[jax.ShapeDtypeStruct((s, d), F32)] + [jax.ShapeDtypeStruct((nsh, s, fs), CD)] * 3
        + [jax.ShapeDtypeStruct((s, d), CD)],
        scratch_shapes=[pltpu.VMEM((tm, d), F32), pltpu.VMEM((tm, d), CD)],
        compiler_params=_params(2),
    )(x, g, wg, wu, wd)


def _ffn_bwd(dout, x, g, silu, dsilu, b, wg, wu, wd, tm):
    s, d = x.shape
    nsh, fs, _ = wg.shape

    def body(do_ref, x_ref, g_ref, sl_ref, ds_ref, b_ref, wg_ref, wu_ref, wd_ref,
             dx_ref, da_ref, db_ref, f_ref, dg_ref, do16_ref, dh):
        i = pl.program_id(0)
        j = pl.program_id(1)

        @pl.when(j == 0)
        def _():
            dh[...] = jnp.zeros_like(dh)
            do16_ref[...] = do_ref[...].astype(CD)

        @pl.when((i == 0) & (j == 0))
        def _():
            dg_ref[...] = jnp.zeros_like(dg_ref)

        for r0 in range(0, tm, FFN_ROWS):
            rows = slice(r0, min(r0 + FFN_ROWS, tm))
            sl = sl_ref[0, rows, :].astype(F32)
            bv = b_ref[0, rows, :].astype(F32)
            df = 0.5 * _mm(do_ref[rows, :], wd_ref[0], NT)
            da = df * bv * ds_ref[0, rows, :].astype(F32)
            db = df * sl
            dh[rows, :] += _mm(da, wg_ref[0], NN) + _mm(db, wu_ref[0], NN)
            da_ref[0, rows, :] = da.astype(CD)
            db_ref[0, rows, :] = db.astype(CD)
            f_ref[0, rows, :] = (sl * bv).astype(CD)

        @pl.when(j == nsh - 1)
        def _():
            xv = x_ref[...]
            r = lax.rsqrt(jnp.mean(xv * xv, axis=-1, keepdims=True) + EPS)
            dhv = dh[...]
            u = dhv * g_ref[...]
            dx_ref[...] = do_ref[...] + r * u - xv * (r * r * r) * jnp.mean(u * xv, axis=-1, keepdims=True)
            dg_ref[...] += jnp.sum(dhv * xv * r, axis=0, keepdims=True)

    act = pl.BlockSpec((1, tm, fs), lambda i, j: (j, i, 0))
    row = pl.BlockSpec((tm, d), lambda i, j: (i, 0))
    return pl.pallas_call(
        body, name="ffn_bwd", grid=(s // tm, nsh),
        in_specs=[row, row, pl.BlockSpec((1, d), lambda i, j: (0, 0)), act, act, act]
        + [pl.BlockSpec((1, fs, d), lambda i, j: (j, 0, 0))] * 3,
        out_specs=[row, act, act, act, pl.BlockSpec((1, d), lambda i, j: (0, 0)), row],
        out_shape=[jax.ShapeDtypeStruct((s, d), F32), jax.ShapeDtypeStruct((nsh, s, fs), CD),
                   jax.ShapeDtypeStruct((nsh, s, fs), CD), jax.ShapeDtypeStruct((nsh, s, fs), CD),
                   jax.ShapeDtypeStruct((1, d), F32), jax.ShapeDtypeStruct((s, d), CD)],
        scratch_shapes=[pltpu.VMEM((tm, d), F32)],
        compiler_params=_params(2),
    )(dout, x, g, silu, dsilu, b, wg, wu, wd)


def _tn_call(name, operands, in_specs, out_shape, out_spec, grid, acc_shape, pick, scale=1.0):
    nk = grid[-1]
    n_in = len(operands)

    def body(*refs):
        out_ref, out16_ref, acc = refs[n_in], refs[n_in + 1], refs[n_in + 2]
        k = pl.program_id(len(grid) - 1)

        @pl.when(k == 0)
        def _():
            acc[...] = jnp.zeros_like(acc)

        pick(refs[:n_in], acc)

        @pl.when(k == nk - 1)
        def _():
            res = (acc[...] if scale == 1.0 else acc[...] * scale).reshape(out_ref.shape)
            out_ref[...] = res
            out16_ref[...] = res.astype(CD)

    return pl.pallas_call(
        body, name=name, grid=grid, in_specs=in_specs, out_specs=[out_spec, out_spec],
        out_shape=[out_shape, jax.ShapeDtypeStruct(out_shape.shape, CD)],
        scratch_shapes=[pltpu.VMEM(acc_shape, F32)], compiler_params=_params(len(grid)),
    )(*operands)


def _dw_shared_b(name, a3, b, tk, scale):
    nj, s, m = a3.shape
    n = b.shape[1]

    def pick(refs, acc):
        rows = pl.ds(pl.multiple_of(pl.program_id(1) * tk, tk), tk)
        acc[...] += _mm(refs[0][0], refs[1][rows, :], TN)

    return _tn_call(name, (a3, b),
                    [pl.BlockSpec((1, tk, m), lambda j, k: (j, k, 0)), pl.BlockSpec((s, n), lambda j, k: (0, 0))],
                    jax.ShapeDtypeStruct((nj, m, n), F32), pl.BlockSpec((1, m, n), lambda j, k: (j, 0, 0)),
                    (nj, s // tk), (m, n), pick, scale)


def _dw_colblocks(name, a, b, nj, tk):
    s, m = a.shape
    n = b.shape[1] // nj

    def pick(refs, acc):
        acc[...] += _mm(refs[0][...], refs[1][...], TN)

    return _tn_call(name, (a, b),
                    [pl.BlockSpec((tk, m), lambda j, k: (k, 0)), pl.BlockSpec((tk, n), lambda j, k: (k, j))],
                    jax.ShapeDtypeStruct((nj, m, n), F32), pl.BlockSpec((1, m, n), lambda j, k: (j, 0, 0)),
                    (nj, s // tk), (m, n), pick)


DP_WIDTHS = (768, 512, 512, 512, 512, 512, 2048)
DP_CHUNK = 256


def _dp_chunk_maps():
    starts, counts, off = [], [], 0
    for w in DP_WIDTHS:
        starts.append(off // DP_CHUNK)
        counts.append(w // DP_CHUNK)
        off += w
    return starts, counts


def _dp_specs(tm, row_axis, chunk_axis):
    starts, counts = _dp_chunk_maps()
    specs = []
    for st, cnt in zip(starts, counts):
        def imap(*ids, st=st, cnt=cnt):
            return (ids[row_axis], jnp.clip(ids[chunk_axis] - st, 0, cnt - 1))
        specs.append(pl.BlockSpec((tm, DP_CHUNK), imap))
    return specs


def _dp_select(n, refs, fn):
    starts, counts = _dp_chunk_maps()
    for ref, st, cnt in zip(refs, starts, counts):
        @pl.when((n >= st) & (n < st + cnt))
        def _(ref=ref):
            fn(ref)


def _dw_in(dps, hb, tk):
    s, d = hb.shape
    n_chunks = sum(DP_WIDTHS) // DP_CHUNK

    def pick(refs, acc):
        rows = pl.ds(pl.multiple_of(pl.program_id(1) * tk, tk), tk)

        def add(ref):
            acc[...] += _mm(ref[...], refs[7][rows, :], TN)

        _dp_select(pl.program_id(0), refs[:7], add)

    return _tn_call("dw_in", (*dps, hb),
                    _dp_specs(tk, 1, 0) + [pl.BlockSpec((s, d), lambda n, k: (0, 0))],
                    jax.ShapeDtypeStruct((n_chunks * DP_CHUNK, d), F32), pl.BlockSpec((DP_CHUNK, d), lambda n, k: (n, 0)),
                    (n_chunks, s // tk), (DP_CHUNK, d), pick)


def _mix_in_fwd(x, g, w_t, tm):
    s, d = x.shape
    n_in = w_t.shape[0]

    def body(x_ref, g_ref, w_ref, qkv_ref, hg_ref, gt_ref, hb_ref):
        xv = x_ref[...]
        r = lax.rsqrt(jnp.mean(xv * xv, axis=-1, keepdims=True) + EPS)
        h = (xv * r * g_ref[...]).astype(CD)
        hb_ref[...] = h
        qkv_ref[...] = _mm(h, w_ref[0:768, :], NT)
        for c in range(5):
            hg_ref[:, 512 * c:512 * c + 512] = _mm(h, w_ref[768 + 512 * c:768 + 512 * c + 512, :], NT)
        for c in range(2):
            gt_ref[:, 1024 * c:1024 * c + 1024] = _mm(h, w_ref[3328 + 1024 * c:3328 + 1024 * c + 1024, :], NT)

    row = lambda w: pl.BlockSpec((tm, w), lambda i: (i, 0))
    return pl.pallas_call(
        body, name="mix_in_fwd", grid=(s // tm,),
        in_specs=[row(d), pl.BlockSpec((1, d), lambda i: (0, 0)), pl.BlockSpec((n_in, d), lambda i: (0, 0))],
        out_specs=[row(768), row(2560), row(2048), row(d)],
        out_shape=[jax.ShapeDtypeStruct((s, 768), F32), jax.ShapeDtypeStruct((s, 2560), F32),
                   jax.ShapeDtypeStruct((s, 2048), F32), jax.ShapeDtypeStruct((s, d), CD)],
        compiler_params=_params(1),
    )(x, g, w_t)


def _mix_in_bwd(dps, w_t, x, dres, g, tm, after=()):
    s, d = x.shape
    n_in = w_t.shape[0]

    def body(*refs):
        refs = refs[len(after):]
        dp_refs = refs[:7]
        w_ref, x_ref, dr_ref, g_ref, dx_ref, dg_ref = refs[7:]

        @pl.when(pl.program_id(0) == 0)
        def _():
            dg_ref[...] = jnp.zeros_like(dg_ref)

        dhv = jnp.zeros((tm, d), F32)
        off = 0
        for ref, width in zip(dp_refs, DP_WIDTHS):
            dhv = dhv + _mm(ref[...], w_ref[off:off + width, :], NN)
            off += width
        xv = x_ref[...]
        r = lax.rsqrt(jnp.mean(xv * xv, axis=-1, keepdims=True) + EPS)
        u = dhv * g_ref[...]
        dx_ref[...] = dr_ref[...] + r * u - xv * (r * r * r) * jnp.mean(u * xv, axis=-1, keepdims=True)
        dg_ref[...] += jnp.sum(dhv * xv * r, axis=0, keepdims=True)

    row = pl.BlockSpec((tm, d), lambda i: (i, 0))
    vec = pl.BlockSpec((1, d), lambda i: (0, 0))
    return pl.pallas_call(
        body, name="mix_in_bwd", grid=(s // tm,),
        in_specs=[ANY] * len(after) + [pl.BlockSpec((tm, w), lambda i: (i, 0)) for w in DP_WIDTHS]
        + [pl.BlockSpec((n_in, d), lambda i: (0, 0)), row, row, vec],
        out_specs=[row, vec],
        out_shape=[jax.ShapeDtypeStruct((s, d), F32), jax.ShapeDtypeStruct((1, d), F32)],
        compiler_params=_params(1),
    )(*after, *dps, w_t, x, dres, g)


def _headnorm_rope(x, gain, cos, sin, blocksum, swap):
    ss = _xdot(x * x, blocksum)
    r = lax.rsqrt(ss * (1.0 / HEAD_DIM) + EPS)
    y = x * r * gain
    return y * cos + _xdot(y, swap) * sin, r


def _headnorm_rope_bwd(dz, x, gain, cos, sin, blocksum, swap):
    ss = _xdot(x * x, blocksum)
    r = lax.rsqrt(ss * (1.0 / HEAD_DIM) + EPS)
    dy = dz * cos + _xdot(dz * sin, swap)
    u = dy * gain
    mean_ux = _xdot(u * x, blocksum) * (1.0 / HEAD_DIM)
    dx = r * u - x * (r * r * r) * mean_ux
    return dx, jnp.sum(dy * x * r, axis=0, keepdims=True)


def _qk_prep(pqkv, gq, gk, cos2, sin2, tm):
    s = pqkv.shape[0]
    bs512, sw512, eq, swh = _bf(_np_blocksum(512)), _bf(_np_swap32(512)), _bf(_np_expand_q()), _bf(_np_swap_halves())

    def body(q_ref, kv_ref, gq_ref, gk_ref, c_ref, s_ref, bs_ref, sw_ref, eq_ref, swh_ref, qe_ref, k_ref, v_ref, vs_ref):
        c2, s2 = c_ref[...], s_ref[...]
        c8, s8 = jnp.tile(c2, (1, 4)), jnp.tile(s2, (1, 4))
        bs, sw = bs_ref[...], sw_ref[...]
        zq, _ = _headnorm_rope(q_ref[...], gq_ref[...], c8, s8, bs, sw)
        qe_ref[...] = _mm(zq * (HEAD_DIM ** -0.5), eq_ref[...], NN).astype(CD)
        kv = kv_ref[...]
        zk, _ = _headnorm_rope(kv[:, :LANES], gk_ref[...], c2, s2, bs[:LANES, :LANES], sw[:LANES, :LANES])
        k_ref[...] = zk.astype(CD)
        v = kv[:, LANES:]
        v_ref[...] = v.astype(CD)
        vs_ref[...] = _mm(v, swh_ref[...], NN).astype(CD)

    full = lambda a: pl.BlockSpec(a.shape, lambda i: (0,) * a.ndim)
    tab = pl.BlockSpec((tm, LANES), lambda i: (i, 0))
    return pl.pallas_call(
        body, name="qk_prep", grid=(s // tm,),
        in_specs=[pl.BlockSpec((tm, 512), lambda i: (i, 0)), pl.BlockSpec((tm, 256), lambda i: (i, 2)),
                  full(gq), full(gk), tab, tab, full(bs512), full(sw512), full(eq), full(swh)],
        out_specs=[pl.BlockSpec((tm, 1024), lambda i: (i, 0)), tab, tab, tab],
        out_shape=[jax.ShapeDtypeStruct((s, 1024), CD)] + [jax.ShapeDtypeStruct((s, LANES), CD)] * 3,
        compiler_params=_params(1),
    )(pqkv, pqkv, gq, gk, cos2, sin2, bs512, sw512, eq, swh)


def _qk_prep_bwd(pqkv, dq, dk, dv, gq, gk, cos2, sin2, tm):
    s = pqkv.shape[0]
    bs512, sw512 = _bf(_np_blocksum(512)), _bf(_np_swap32(512))

    def body(q_ref, kv_ref, dq_ref, dk_ref, dv_ref, gq_ref, gk_ref, c_ref, s_ref, bs_ref, sw_ref,
             dp_ref, dgq_ref, dgk_ref):
        @pl.when(pl.program_id(0) == 0)
        def _():
            dgq_ref[...] = jnp.zeros_like(dgq_ref)
            dgk_ref[...] = jnp.zeros_like(dgk_ref)

        c2, s2 = c_ref[...], s_ref[...]
        c8, s8 = jnp.tile(c2, (1, 4)), jnp.tile(s2, (1, 4))
        bs, sw = bs_ref[...], sw_ref[...]
        dzq = dq_ref[...] * (HEAD_DIM ** -0.5)
        dxq, dgq = _headnorm_rope_bwd(dzq, q_ref[...], gq_ref[...], c8, s8, bs, sw)
        kv = kv_ref[...]
        dxk, dgk = _headnorm_rope_bwd(dk_ref[...], kv[:, :LANES], gk_ref[...], c2, s2, bs[:LANES, :LANES], sw[:LANES, :LANES])
        dp_ref[...] = jnp.concatenate([dxq, dxk, dv_ref[...]], axis=1).astype(CD)
        dgq_ref[...] += dgq
        dgk_ref[...] += dgk

    full = lambda a: pl.BlockSpec(a.shape, lambda i: (0,) * a.ndim)
    tab = pl.BlockSpec((tm, LANES), lambda i: (i, 0))
    return pl.pallas_call(
        body, name="qk_prep_bwd", grid=(s // tm,),
        in_specs=[pl.BlockSpec((tm, 512), lambda i: (i, 0)), pl.BlockSpec((tm, 256), lambda i: (i, 2)),
                  pl.BlockSpec((tm, 512), lambda i: (i, 0)), tab, tab, full(gq), full(gk), tab, tab,
                  full(bs512), full(sw512)],
        out_specs=[pl.BlockSpec((tm, 768), lambda i: (i, 0)), pl.BlockSpec((1, 512), lambda i: (0, 0)),
                   pl.BlockSpec((1, LANES), lambda i: (0, 0))],
        out_shape=[jax.ShapeDtypeStruct((s, 768), CD), jax.ShapeDtypeStruct((1, 512), F32),
                   jax.ShapeDtypeStruct((1, LANES), F32)],
        compiler_params=_params(1),
    )(pqkv, pqkv, dq, dk, dv, gq, gk, cos2, sin2, bs512, sw512)


def _kv_rows(h):
    return pl.ds(pl.multiple_of((h // 4) * HEAD_DIM, HEAD_DIM), HEAD_DIM)


def _attn_fwd(qe, k, v, vs, tq):
    s = k.shape[0]

    def body(qa_ref, qb_ref, k_ref, v_ref, vs_ref, o_ref, lse_ref):
        m = pl.program_id(0)
        grp = m // 2
        kk = k_ref[...]
        outs = []
        for idx, q_ref in enumerate((qa_ref, qb_ref)):
            sc = _mm(q_ref[...], kk, NT)
            mx = jnp.max(sc, axis=-1, keepdims=True)
            e = jnp.exp(sc - mx)
            l = jnp.sum(e, axis=-1, keepdims=True)
            lse_ref[idx] = mx + jnp.log(l)
            p = e * (1.0 / l)
            vsel = jnp.where(grp != idx, vs_ref[...], v_ref[...])
            outs.append(_mm(p, vsel, NN))
        lane = lax.broadcasted_iota(jnp.int32, (1, LANES), 1)
        o_ref[...] = jnp.where(lane < HEAD_DIM, outs[0], outs[1])

    kv = pl.BlockSpec((s, LANES), lambda m, i: (0, 0))
    return pl.pallas_call(
        body, name="attn_fwd", grid=(4, s // tq),
        in_specs=[pl.BlockSpec((tq, LANES), lambda m, i: (i, 2 * m)), pl.BlockSpec((tq, LANES), lambda m, i: (i, 2 * m + 1)),
                  kv, kv, kv],
        out_specs=[pl.BlockSpec((tq, LANES), lambda m, i: (i, m)), pl.BlockSpec((2, tq, 1), lambda m, i: (m, i, 0))],
        out_shape=[jax.ShapeDtypeStruct((s, 512), F32), jax.ShapeDtypeStruct((8, s, 1), F32)],
        compiler_params=_params(2),
    )(qe, qe, k, v, vs)


def _attn_bwd(qe, k, kt, v, doe, delta, lse, tq):
    s = k.shape[0]

    def body(qa_ref, qb_ref, k_ref, kt_ref, v_ref, doa_ref, dob_ref, dla_ref, dlb_ref, lse_ref,
             dqt_ref, dkt_ref, dvt_ref, qt, dot):
        @pl.when((pl.program_id(0) == 0) & (pl.program_id(1) == 0))
        def _():
            dkt_ref[...] = jnp.zeros_like(dkt_ref)
            dvt_ref[...] = jnp.zeros_like(dvt_ref)

        rows = _kv_rows(2 * pl.program_id(0))
        kt = kt_ref[rows, :]
        dkt = jnp.zeros((HEAD_DIM, s), F32)
        dvt = jnp.zeros((HEAD_DIM, s), F32)
        for idx, (q_ref, do_ref, dl_ref) in enumerate(((qa_ref, doa_ref, dla_ref), (qb_ref, dob_ref, dlb_ref))):
            q, do = q_ref[...], do_ref[...]
            p = jnp.exp(_mm(q, k_ref[...], NT) - lse_ref[idx])
            dp = _mm(do, v_ref[...], NT)
            ds = p * (dp - jnp.max(dl_ref[...], axis=-1, keepdims=True))
            dqt_ref[idx * HEAD_DIM:(idx + 1) * HEAD_DIM, :] = _mm(kt, ds, NT)
            qt[idx] = jnp.transpose(q.astype(F32))
            dot[idx] = jnp.transpose(do.astype(F32))
            dkt = dkt + _mm(qt[idx, rows, :], ds, NN)
            dvt = dvt + _mm(dot[idx, rows, :], p, NN)
        dkt_ref[rows, :] += dkt
        dvt_ref[rows, :] += dvt

    kv = pl.BlockSpec((s, LANES), lambda m, i: (0, 0))
    kvt = pl.BlockSpec((LANES, s), lambda m, i: (0, 0))
    blk_a = pl.BlockSpec((tq, LANES), lambda m, i: (i, 2 * m))
    blk_b = pl.BlockSpec((tq, LANES), lambda m, i: (i, 2 * m + 1))
    return pl.pallas_call(
        body, name="attn_bwd", grid=(4, s // tq),
        in_specs=[blk_a, blk_b, kv, kvt, kv, blk_a, blk_b, blk_a, blk_b, pl.BlockSpec((2, tq, 1), lambda m, i: (m, i, 0))],
        out_specs=[pl.BlockSpec((LANES, tq), lambda m, i: (m, i)), kvt, kvt],
        out_shape=[jax.ShapeDtypeStruct((8 * HEAD_DIM, s), F32), jax.ShapeDtypeStruct((LANES, s), F32),
                   jax.ShapeDtypeStruct((LANES, s), F32)],
        scratch_shapes=[pltpu.VMEM((2, LANES, tq), F32), pltpu.VMEM((2, LANES, tq), F32)],
        compiler_params=_params(2),
    )(qe, qe, k, kt, v, doe, doe, delta, delta, lse)


@jax.custom_vjp
def _mm_nn(a, b):
    return _mm(a, b, NN)


_mm_nn.defvjp(lambda a, b: (_mm(a, b, NN), (a, b)),
              lambda res, g: (_mm(g, res[1], NT), _mm(res[0], g, TN)))


@jax.custom_vjp
def _mm_nt(a, b):
    return _mm(a, b, NT)


_mm_nt.defvjp(lambda a, b: (_mm(a, b, NT), (a, b)),
              lambda res, g: (_mm(g, res[1], NN), _mm(g, res[0], TN)))


@jax.custom_vjp
def _mm_tn(a, b):
    return _mm(a, b, TN)


_mm_tn.defvjp(lambda a, b: (_mm(a, b, TN), (a, b)),
              lambda res, g: (_mm(res[1], g, NT), _mm(res[0], g, NN)))


@jax.custom_vjp
def _cmm(m, mt, x):
    return _xdot_l(m, x)


_cmm.defvjp(lambda m, mt, x: (_xdot_l(m, x), (m, mt)),
            lambda res, g: (jnp.zeros_like(res[0]), jnp.zeros_like(res[1]), _xdot_l(res[1], g)))


def _hgrn_masks(t, rev):
    n_ch = t // CHUNK
    r = jnp.bitwise_and(lax.broadcasted_iota(jnp.int32, (2 * t, t), 0), t - 1)
    c = lax.broadcasted_iota(jnp.int32, (2 * t, t), 1)
    same = jnp.right_shift(r, 5) == jnp.right_shift(c, 5)
    tri2 = same & ((c >= r) if rev else (c <= r))
    pr = lax.broadcasted_iota(jnp.int32, (LANES, LANES), 0)
    pc = lax.broadcasted_iota(jnp.int32, (LANES, LANES), 1)
    diag = jnp.right_shift(pr, 6) == jnp.right_shift(pc, 6)
    qr = lax.broadcasted_iota(jnp.int32, (t, n_ch * LANES), 0)
    qc = lax.broadcasted_iota(jnp.int32, (t, n_ch * LANES), 1)
    rows_chunk = jnp.right_shift(qc, 7) == jnp.right_shift(qr, 5)
    vr = lax.broadcasted_iota(jnp.int32, (n_ch * LANES, t), 0)
    vc = lax.broadcasted_iota(jnp.int32, (n_ch * LANES, t), 1)
    cols_chunk = jnp.right_shift(vr, 7) == jnp.right_shift(vc, 5)
    return dict(tri2=tri2, diag=diag, rows_chunk=rows_chunk, cols_chunk=cols_chunk)


def _hgrn_gates(xf, lb):
    f = lb + (1.0 - lb) * _sigmoid(xf)
    return 1.0 - f, jnp.log(f)


def _hgrn_dir(xq, xf, v, lb, state, cm, cmt, mk, rev):
    t = xq.shape[0]
    n_ch = t // CHUNK
    lo = lax.broadcasted_iota(jnp.int32, (1, LANES), 1) < HEAD_DIM
    q = xq * _sigmoid(xq)
    k, lf = _hgrn_gates(xf, lb)
    cs = _cmm(cm, cmt, lf)
    b, bm, bl = cs[:t], cs[t:2 * t], cs[2 * t:]
    qd = q * jnp.exp(b - bm)
    kd = k * jnp.exp(bm - b)
    kc = k * jnp.exp(bl - b)
    qe = q * jnp.exp(b)
    qd2 = jnp.concatenate([jnp.where(lo, qd, 0.0), jnp.where(lo, 0.0, qd)], axis=0)
    o2 = _mm_nn(jnp.where(mk["tri2"], _mm_nt(qd2, kd), 0.0), v)
    o = jnp.where(lo, o2[:t], o2[t:])
    vexp = jnp.where(mk["cols_chunk"], jnp.concatenate([jnp.transpose(v)] * n_ch, axis=0), 0.0)
    adds = _mm_nn(vexp, kc)
    dec = jnp.exp(bl)
    entering = [None] * n_ch
    for c in (range(n_ch - 1, -1, -1) if rev else range(n_ch)):
        entering[c] = state
        d = jnp.concatenate([dec[c * CHUNK:(c + 1) * CHUNK]] * (LANES // CHUNK), axis=0)
        state = d * state + jnp.where(mk["diag"], adds[c * LANES:(c + 1) * LANES], 0.0)
    qexp = jnp.where(mk["rows_chunk"], jnp.concatenate([qe] * n_ch, axis=1), 0.0)
    return o + _mm_nt(qexp, jnp.concatenate(entering, axis=1)), state


def _hgrn_lower_bounds(l):
    out = []
    for d in (0, 1):
        l0, l1 = l[2 * d:2 * d + 1, :], l[2 * d + 1:2 * d + 2, :]
        mx = jnp.maximum(l0, l1)
        e0, e1 = jnp.exp(l0 - mx), jnp.exp(l1 - mx)
        out.append(e0 / (e0 + e1))
    return out


def _hgrn_consts(t):
    cf, cb = _np_hgrn_cums(t, False), _np_hgrn_cums(t, True)
    return (_bf(cf), _bf(cf.T), _bf(cb), _bf(cb.T), _bf(_np_blocksum(LANES)))


def _hgrn_fwd(ph, lbl, ng):
    s = ph.shape[0]
    t = min(HG_TILE, s)
    nt = s // t
    consts = _hgrn_consts(t)

    def body(xq_ref, xff_ref, xfb_ref, xi_ref, xg_ref, lbl_ref, ng_ref, cf_ref, cft_ref, cb_ref, cbt_ref, bs_ref,
             o_ref, pre_ref, st_ref, acc):
        lbf, lbb = _hgrn_lower_bounds(lbl_ref)
        mk_f, mk_b = _hgrn_masks(t, False), _hgrn_masks(t, True)
        zero = jnp.zeros((LANES, LANES), F32)

        def rows_of(i):
            return pl.ds(pl.multiple_of(i * t, t), t)

        acc[...] = jnp.zeros_like(acc)

        def step(i, states):
            tb = nt - 1 - i
            rf, rb = rows_of(i), rows_of(tb)
            st_ref[0, 0, i] = states[0]
            st_ref[0, 1, tb] = states[1]
            of, sf = _hgrn_dir(xq_ref[rf, :], xff_ref[rf, :], xi_ref[rf, :], lbf, states[0],
                               cf_ref[...], cft_ref[...], mk_f, False)
            ob, sb = _hgrn_dir(xq_ref[rb, :], xfb_ref[rb, :], xi_ref[rb, :], lbb, states[1],
                               cb_ref[...], cbt_ref[...], mk_b, True)
            acc[rf, :] += of
            acc[rb, :] += ob
            return sf, sb

        lax.fori_loop(0, nt, step, (zero, zero))

        def step_n(i, carry):
            rows = rows_of(i)
            o = acc[rows, :]
            ss = _xdot(o * o, bs_ref[...])
            r = lax.rsqrt(ss * (1.0 / HEAD_DIM) + EPS)
            xg = xg_ref[rows, :]
            pre_ref[rows, :] = o
            o_ref[rows, :] = (o * r * ng_ref[...]) * (xg * _sigmoid(xg))
            return carry

        lax.fori_loop(0, nt, step_n, 0)

    col = lambda off: pl.BlockSpec((s, LANES), lambda m: (0, off + m))
    full = lambda a: pl.BlockSpec(a.shape, lambda m: (0,) * a.ndim)
    return pl.pallas_call(
        body, name="hgrn_fwd", grid=(4,),
        in_specs=[col(0), col(4), col(8), col(12), col(16), pl.BlockSpec((4, LANES), lambda m: (0, m)),
                  pl.BlockSpec((1, LANES), lambda m: (0, m))] + [full(c) for c in consts],
        out_specs=[col(0), col(0), pl.BlockSpec((1, 2, nt, LANES, LANES), lambda m: (m, 0, 0, 0, 0))],
        out_shape=[jax.ShapeDtypeStruct((s, 512), F32), jax.ShapeDtypeStruct((s, 512), F32),
                   jax.ShapeDtypeStruct((4, 2, nt, LANES, LANES), F32)],
        scratch_shapes=[pltpu.VMEM((s, LANES), F32)],
        compiler_params=_params(1),
    )(ph, ph, ph, ph, ph, lbl, ng, *consts)


def _hgrn_bwd(ph, pre, dout, states, lbl, ng):
    s = ph.shape[0]
    t = min(HG_TILE, s)
    nt = s // t
    consts = _hgrn_consts(t)

    def body(xq_ref, xff_ref, xfb_ref, xi_ref, xg_ref, pre_ref, do_ref, st_ref, lbl_ref, ng_ref,
             cf_ref, cft_ref, cb_ref, cbt_ref, bs_ref,
             dq_ref, dff_ref, dfb_ref, di_ref, dg_ref, dlb_ref, dng_ref, dpre, dq_acc, dv_acc):
        lbf, lbb = _hgrn_lower_bounds(lbl_ref)
        mk_f, mk_b = _hgrn_masks(t, False), _hgrn_masks(t, True)
        zero = jnp.zeros((LANES, LANES), F32)
        zrow = jnp.zeros((1, LANES), F32)

        def rows_of(i):
            return pl.ds(pl.multiple_of(i * t, t), t)

        def step_n(i, dng):
            rows = rows_of(i)
            o, xg, do = pre_ref[rows, :], xg_ref[rows, :], do_ref[rows, :]
            bs = bs_ref[...]
            r = lax.rsqrt(_xdot(o * o, bs) * (1.0 / HEAD_DIM) + EPS)
            sg = _sigmoid(xg)
            gate = xg * sg
            don = do * gate
            dg_ref[rows, :] = (do * (o * r * ng_ref[...]) * (sg * (1.0 + xg * (1.0 - sg)))).astype(CD)
            u = don * ng_ref[...]
            dpre[rows, :] = r * u - o * (r * r * r) * (_xdot(u * o, bs) * (1.0 / HEAD_DIM))
            return dng + jnp.sum(don * o * r, axis=0, keepdims=True)

        dng_ref[...] = lax.fori_loop(0, nt, step_n, zrow)

        dq_acc[...] = jnp.zeros_like(dq_acc)
        dv_acc[...] = jnp.zeros_like(dv_acc)

        def grad_tile(ti, xf_ref, df_ref, lb, cm, cmt, mk, rev, st, dstate):
            rows = rows_of(ti)
            fn = lambda xq, xf, v, lbv, s_in: _hgrn_dir(xq, xf, v, lbv, s_in, cm, cmt, mk, rev)
            _, vjp = jax.vjp(fn, xq_ref[rows, :], xf_ref[rows, :], xi_ref[rows, :], lb, st)
            dxq, dxf, dv, dlb_t, dstate = vjp((dpre[rows, :], dstate))
            df_ref[rows, :] = dxf.astype(CD)
            dq_acc[rows, :] += dxq
            dv_acc[rows, :] += dv
            return dstate, dlb_t

        def step_g(i, carry):
            dsf, dsb, dlbf, dlbb = carry
            tf, tb = nt - 1 - i, i
            dsf, gf = grad_tile(tf, xff_ref, dff_ref, lbf, cf_ref[...], cft_ref[...], mk_f, False, st_ref[0, 0, tf], dsf)
            dsb, gb = grad_tile(tb, xfb_ref, dfb_ref, lbb, cb_ref[...], cbt_ref[...], mk_b, True, st_ref[0, 1, tb], dsb)
            return dsf, dsb, dlbf + gf, dlbb + gb

        _, _, dlbf, dlbb = lax.fori_loop(0, nt, step_g, (zero, zero, zrow, zrow))
        dlb_ref[0:1, :] = dlbf
        dlb_ref[1:2, :] = dlbb
        dq_ref[...] = dq_acc[...].astype(CD)
        di_ref[...] = dv_acc[...].astype(CD)

    col = lambda off: pl.BlockSpec((s, LANES), lambda m: (0, off + m))
    full = lambda a: pl.BlockSpec(a.shape, lambda m: (0,) * a.ndim)
    stream = jax.ShapeDtypeStruct((s, 512), CD)
    return pl.pallas_call(
        body, name="hgrn_bwd", grid=(4,),
        in_specs=[col(0), col(4), col(8), col(12), col(16), col(0), col(0),
                  pl.BlockSpec((1, 2, nt, LANES, LANES), lambda m: (m, 0, 0, 0, 0)),
                  pl.BlockSpec((4, LANES), lambda m: (0, m)),
                  pl.BlockSpec((1, LANES), lambda m: (0, m))] + [full(c) for c in consts],
        out_specs=[col(0)] * 5 + [pl.BlockSpec((2, LANES), lambda m: (0, m)), pl.BlockSpec((1, LANES), lambda m: (0, m))],
        out_shape=[stream] * 5 + [jax.ShapeDtypeStruct((2, 512), F32), jax.ShapeDtypeStruct((1, 512), F32)],
        scratch_shapes=[pltpu.VMEM((s, LANES), F32), pltpu.VMEM((s, LANES), F32), pltpu.VMEM((s, LANES), F32)],
        compiler_params=_params(1),
    )(ph, ph, ph, ph, ph, pre, dout, states, lbl, ng, *consts)


def _branch_out(o, w4):
    return jnp.concatenate([_mm(o, w4[j], NN) for j in range(N_SHARD)], axis=1)


def _mix_out_fwd(x, oa, ob, pg, wa, wb, wo, tm):
    s, d = x.shape

    def body(x_ref, oa_ref, ob_ref, ga_ref, gb_ref, wa_ref, wb_ref, wo_ref, xo_ref):
        ya = _branch_out(oa_ref[...], wa_ref)
        yb = _branch_out(ob_ref[...], wb_ref)
        merged = _sigmoid(ga_ref[...]) * ya + _sigmoid(gb_ref[...]) * yb
        xo_ref[...] = x_ref[...] + _mm(merged, wo_ref[...], NN)

    row = pl.BlockSpec((tm, d), lambda i: (i, 0))
    half = pl.BlockSpec((tm, 512), lambda i: (i, 0))
    full = lambda a: pl.BlockSpec(a.shape, lambda i: (0,) * a.ndim)
    return pl.pallas_call(
        body, name="mix_out_fwd", grid=(s // tm,),
        in_specs=[row, half, half, row, pl.BlockSpec((tm, d), lambda i: (i, 1)), full(wa), full(wb), full(wo)],
        out_specs=row, out_shape=jax.ShapeDtypeStruct((s, d), F32),
        compiler_params=_params(1),
    )(x, oa, ob, pg, pg, wa, wb, wo)


def _mix_out_bwd(dx, oa, ob, pg, wa, wb, wo, tm, after=()):
    s, d = dx.shape
    bs512, eq, ebc = _bf(_np_blocksum(512)), _bf(_np_expand_q()), _bf(_np_bcast_head())

    def body(*refs):
        (dx_ref, oa_ref, ob_ref, ga_ref, gb_ref, wa_ref, wb_ref, wo_ref, bs_ref, eq_ref, ebc_ref,
         dpg_ref, mg_ref, dya_ref, dyb_ref, doe_ref, dl_ref, dob_ref) = refs[len(after):]
        oa = oa_ref[...]
        ya = _branch_out(oa, wa_ref)
        yb = _branch_out(ob_ref[...], wb_ref)
        sa, sb = _sigmoid(ga_ref[...]), _sigmoid(gb_ref[...])
        mg_ref[...] = (sa * ya + sb * yb).astype(CD)
        dm = _mm(dx_ref[...], wo_ref[...], NT)
        dpg_ref[...] = jnp.concatenate([dm * ya * sa * (1.0 - sa), dm * yb * sb * (1.0 - sb)], axis=1).astype(CD)
        dya, dyb = dm * sa, dm * sb
        dya_ref[...] = dya.astype(CD)
        dyb_ref[...] = dyb.astype(CD)
        doa = jnp.zeros(oa.shape, F32)
        dob = jnp.zeros(oa.shape, F32)
        for j in range(N_SHARD):
            doa = doa + _mm(dya[:, 256 * j:256 * j + 256], wa_ref[j], NT)
            dob = dob + _mm(dyb[:, 256 * j:256 * j + 256], wb_ref[j], NT)
        dob_ref[...] = dob
        doe_ref[...] = _mm(doa, eq_ref[...], NN).astype(CD)
        dl_ref[...] = _xdot(_xdot(doa * oa, bs_ref[...]), ebc_ref[...])

    row = pl.BlockSpec((tm, d), lambda i: (i, 0))
    half = pl.BlockSpec((tm, 512), lambda i: (i, 0))
    full = lambda a: pl.BlockSpec(a.shape, lambda i: (0,) * a.ndim)
    wide = jax.ShapeDtypeStruct((s, d), CD)
    return pl.pallas_call(
        body, name="mix_out_bwd", grid=(s // tm,),
        in_specs=[ANY] * len(after) + [row, half, half, row, pl.BlockSpec((tm, d), lambda i: (i, 1)), full(wa), full(wb),
                                       full(wo), full(bs512), full(eq), full(ebc)],
        out_specs=[pl.BlockSpec((tm, 2048), lambda i: (i, 0)), row, row, row, row, row, half],
        out_shape=[jax.ShapeDtypeStruct((s, 2048), CD), wide, wide, wide, wide, jax.ShapeDtypeStruct((s, d), F32),
                   jax.ShapeDtypeStruct((s, 512), F32)],
        compiler_params=_params(1),
    )(*after, dx, oa, ob, pg, pg, wa, wb, wo, bs512, eq, ebc)


def _loss_head(x, g, target, tm):
    s, d = x.shape

    def body(x_ref, g_ref, t_ref, dx_ref, loss_ref, dg_ref):
        @pl.when(pl.program_id(0) == 0)
        def _():
            loss_ref[...] = jnp.zeros_like(loss_ref)
            dg_ref[...] = jnp.zeros_like(dg_ref)

        xv = x_ref[...]
        r = lax.rsqrt(jnp.mean(xv * xv, axis=-1, keepdims=True) + EPS)
        err = xv * r * g_ref[...] - t_ref[...]
        loss_ref[...] += 0.5 * jnp.sum(jnp.mean(err * err, axis=-1, keepdims=True))
        dy = err * (1.0 / d)
        u = dy * g_ref[...]
        dx_ref[...] = r * u - xv * (r * r * r) * jnp.mean(u * xv, axis=-1, keepdims=True)
        dg_ref[...] += jnp.sum(dy * xv * r, axis=0, keepdims=True)

    row = pl.BlockSpec((tm, d), lambda i: (i, 0))
    vec = pl.BlockSpec((1, d), lambda i: (0, 0))
    return pl.pallas_call(
        body, name="loss_head", grid=(s // tm,),
        in_specs=[row, vec, row], out_specs=[row, pl.BlockSpec((8, LANES), lambda i: (0, 0)), vec],
        out_shape=[jax.ShapeDtypeStruct((s, d), F32), jax.ShapeDtypeStruct((8, LANES), F32),
                   jax.ShapeDtypeStruct((1, d), F32)],
        compiler_params=_params(1),
    )(x, g, target)


def _position():
    x, y, c = lax.axis_index("x"), lax.axis_index("y"), lax.axis_index("c")
    return x, y, c, [(1 - x, y), (x, 1 - y), (1 - x, 1 - y)]


def _row_tile(rows, cap=256):
    best = rows
    for cand in range(8, min(rows, cap) + 1, 8):
        if rows % cand == 0:
            best = cand
    return best


def _cast_into_slot(shard, me_idx, dtype):
    rows, cols = shard.shape
    tr = _row_tile(rows)

    def body(me_ref, src_ref, out_ref):
        out_ref[0] = src_ref[...].astype(dtype)

    return pl.pallas_call(
        body, name="cast_into_slot",
        grid_spec=pltpu.PrefetchScalarGridSpec(
            num_scalar_prefetch=1, grid=(rows // tr,),
            in_specs=[pl.BlockSpec((tr, cols), lambda i, me: (i, 0))],
            out_specs=pl.BlockSpec((1, tr, cols), lambda i, me: (me[0], i, 0))),
        out_shape=jax.ShapeDtypeStruct((N_SHARD, rows, cols), dtype),
        compiler_params=_params(1),
    )(me_idx, shard)


HBM_SPEC = pl.BlockSpec(memory_space=pltpu.HBM)
SEM_SPEC = pl.BlockSpec(memory_space=pltpu.SEMAPHORE)
DATAFLOW = pltpu.SideEffectType.DATAFLOW_SIDE_EFFECTING


def _exchange_copies(srcs, lands, send, recv, gather):
    x, y, c, chips = _position()
    me = 2 * x + y
    out = []
    for a in range(len(lands)):
        dst = lands[a].at[me]
        if gather and _halved(lands[a]):
            half = lands[a].shape[1] // 2
            dst = lands[a].at[me, pl.ds(c * half, half), :]
        for k, (px, py) in enumerate(chips):
            src = dst if gather else srcs[a].at[2 * px + py]
            out.append(pltpu.make_async_remote_copy(src_ref=src, dst_ref=dst, send_sem=send.at[3 * a + k],
                                                    recv_sem=recv.at[3 * a + k], device_id=(px, py, c), device_id_type=MESH))
    return out


def _halved(land):
    return land.shape[1] % 32 == 0


def _pair_fill(name, lands):
    n = len(lands)

    def body(*refs):
        src, dst = refs[:n], refs[n:2 * n]
        send, recv = refs[2 * n:]
        x, y, c, chips = _position()
        copies = []
        for a in range(n):
            half = src[a].shape[1] // 2
            for k, (px, py) in enumerate(chips):
                rows = (2 * px + py, pl.ds(c * half, half), slice(None))
                cp = pltpu.make_async_remote_copy(src_ref=src[a].at[rows], dst_ref=dst[a].at[rows], send_sem=send.at[a, k],
                                                  recv_sem=recv.at[a, k], device_id=(x, y, 1 - c), device_id_type=MESH)
                cp.start()
                copies.append(cp)
        for cp in copies:
            cp.wait()

    return pl.pallas_call(
        body, name=name, in_specs=[ANY] * n, out_specs=[ANY] * n,
        out_shape=[jax.ShapeDtypeStruct(l.shape, l.dtype) for l in lands],
        input_output_aliases={a: a for a in range(n)},
        scratch_shapes=[pltpu.SemaphoreType.DMA((n, 3)), pltpu.SemaphoreType.DMA((n, 3))],
    )(*lands)


def _exchange_start(name, srcs, lands, after):
    ns, nl, na = len(srcs), len(lands), len(after)
    gather = ns == 0

    def body(*refs):
        src_refs, land_refs = refs[:ns], refs[ns:ns + nl]
        send, recv = refs[ns + nl + na], refs[ns + nl + na + 1]
        token = refs[-1]
        for cp in _exchange_copies(src_refs, land_refs, send, recv, gather):
            cp.start()
        token[...] = jnp.zeros_like(token)

    arrays = [pltpu.with_memory_space_constraint(a, pltpu.HBM) for a in list(srcs) + list(lands)]
    outs = pl.pallas_call(
        body, name=name,
        out_shape=(pltpu.SemaphoreType.DMA((3 * nl,)), pltpu.SemaphoreType.DMA((3 * nl,)),
                   *[pltpu.HBM(a.shape, a.dtype) for a in arrays], jax.ShapeDtypeStruct((8, LANES), F32)),
        in_specs=[HBM_SPEC] * (ns + nl) + [ANY] * na,
        out_specs=(SEM_SPEC, SEM_SPEC, *[HBM_SPEC] * (ns + nl), pl.BlockSpec(memory_space=pltpu.VMEM)),
        input_output_aliases={i: 2 + i for i in range(ns + nl)},
        compiler_params=pltpu.CompilerParams(has_side_effects=DATAFLOW),
    )(*arrays, *after)
    return outs[0], outs[1], list(outs[2:2 + ns]), list(outs[2 + ns:2 + ns + nl]), outs[-1]


def _exchange_wait(name, send, recv, srcs, lands, after):
    ns, nl, na = len(srcs), len(lands), len(after)
    gather = ns == 0

    def body(*refs):
        src_refs, land_refs = refs[:ns], refs[ns:ns + nl]
        send_ref, recv_ref = refs[ns + nl], refs[ns + nl + 1]
        for cp in _exchange_copies(src_refs, land_refs, send_ref, recv_ref, gather):
            cp.wait_send()
            cp.wait_recv()

    outs = pl.pallas_call(
        body, name=name,
        out_shape=tuple(pltpu.HBM(a.shape, a.dtype) for a in list(srcs) + list(lands)),
        in_specs=[HBM_SPEC] * (ns + nl) + [SEM_SPEC, SEM_SPEC] + [ANY] * na,
        out_specs=tuple([HBM_SPEC] * (ns + nl)),
        input_output_aliases={i: i for i in range(ns + nl)},
        compiler_params=pltpu.CompilerParams(has_side_effects=DATAFLOW),
    )(*srcs, *lands, send, recv, *after)
    return list(outs[ns:])


def _pair_exchange(grads):
    n = len(grads)

    def body(*refs):
        src, dst = refs[:n], refs[n:2 * n]
        send, recv = refs[2 * n:]
        x, y, c, _ = _position()
        copies = []
        for a in range(n):
            half = src[a].shape[1] // 2
            cp = pltpu.make_async_remote_copy(
                src_ref=src[a].at[:, pl.ds((1 - c) * half, half), :], dst_ref=dst[a], send_sem=send.at[a],
                recv_sem=recv.at[a], device_id=(x, y, 1 - c), device_id_type=MESH)
            cp.start()
            copies.append(cp)
        for cp in copies:
            cp.wait()

    return pl.pallas_call(
        body, name="grad_pair_exchange", in_specs=[ANY] * n, out_specs=[ANY] * n,
        out_shape=[jax.ShapeDtypeStruct((g.shape[0], g.shape[1] // 2, g.shape[2]), g.dtype) for g in grads],
        scratch_shapes=[pltpu.SemaphoreType.DMA((n,)), pltpu.SemaphoreType.DMA((n,))],
    )(*grads)


def _pair_sum(g, got, c_idx, me_idx):
    nsh, rows, cols = g.shape
    half = rows // 2

    def body(c_ref, me_ref, g_ref, got_ref, s_ref, own_ref):
        sm = g_ref[...] + got_ref[...].astype(F32)
        s_ref[...] = sm.astype(CD)

        @pl.when(pl.program_id(0) == me_ref[0])
        def _():
            own_ref[...] = sm[0]

    return pl.pallas_call(
        body, name="grad_pair_sum",
        grid_spec=pltpu.PrefetchScalarGridSpec(
            num_scalar_prefetch=2, grid=(nsh,),
            in_specs=[pl.BlockSpec((1, half, cols), lambda j, c, me: (j, c[0], 0)),
                      pl.BlockSpec((1, half, cols), lambda j, c, me: (j, 0, 0))],
            out_specs=[pl.BlockSpec((1, half, cols), lambda j, c, me: (j, 0, 0)),
                       pl.BlockSpec((half, cols), lambda j, c, me: (0, 0))]),
        out_shape=[jax.ShapeDtypeStruct((nsh, half, cols), CD), jax.ShapeDtypeStruct((half, cols), F32)],
        compiler_params=_params(1),
    )(c_idx, me_idx, g, got)


def _chip_sum(own, got, me_idx):
    nsh, half, cols = got.shape

    def body(me_ref, own_ref, got_ref, out_ref):
        j = pl.program_id(0)
        term = jnp.where(j == me_ref[0], own_ref[...], got_ref[0].astype(F32))

        @pl.when(j == 0)
        def _():
            out_ref[...] = term

        @pl.when(j > 0)
        def _():
            out_ref[...] += term

    return pl.pallas_call(
        body, name="grad_chip_sum",
        grid_spec=pltpu.PrefetchScalarGridSpec(
            num_scalar_prefetch=1, grid=(nsh,),
            in_specs=[pl.BlockSpec((half, cols), lambda j, me: (0, 0)), pl.BlockSpec((1, half, cols), lambda j, me: (j, 0, 0))],
            out_specs=pl.BlockSpec((half, cols), lambda j, me: (0, 0))),
        out_shape=jax.ShapeDtypeStruct((half, cols), F32),
        compiler_params=_params(1),
    )(me_idx, own, got)


def _pair_share(halves):
    n = len(halves)

    def body(*refs):
        src, dst = refs[:n], refs[n:2 * n]
        send, recv = refs[2 * n:]
        x, y, c, _ = _position()
        copies = []
        for a in range(n):
            cp = pltpu.make_async_remote_copy(src_ref=src[a], dst_ref=dst[a], send_sem=send.at[a],
                                              recv_sem=recv.at[a], device_id=(x, y, 1 - c), device_id_type=MESH)
            cp.start()
            copies.append(cp)
        for cp in copies:
            cp.wait()

    return pl.pallas_call(
        body, name="grad_pair_share", in_specs=[ANY] * n, out_specs=[ANY] * n,
        out_shape=[jax.ShapeDtypeStruct(h.shape, h.dtype) for h in halves],
        scratch_shapes=[pltpu.SemaphoreType.DMA((n,)), pltpu.SemaphoreType.DMA((n,))],
    )(*halves)


def _small_allreduce(buf):
    rows, cols = buf.shape

    def body(src_ref, out_ref, slots, send, recv):
        x, y, c, _ = _position()
        me = 4 * x + 2 * y + c
        slots[me] = src_ref[...]
        copies = []
        k = 0
        for dx in (0, 1):
            for dy in (0, 1):
                for dc in (0, 1):
                    if (dx, dy, dc) == (0, 0, 0):
                        continue
                    peer = (jnp.where(dx, 1 - x, x), jnp.where(dy, 1 - y, y), jnp.where(dc, 1 - c, c))
                    cp = pltpu.make_async_remote_copy(src_ref=src_ref, dst_ref=slots.at[me], send_sem=send.at[k],
                                                      recv_sem=recv.at[k], device_id=peer, device_id_type=MESH)
                    cp.start()
                    copies.append(cp)
                    k += 1
        for cp in copies:
            cp.wait()
        total = slots[0]
        for dev in range(1, N_DEV):
            total = total + slots[dev]
        out_ref[...] = total

    vm = pl.BlockSpec(memory_space=pltpu.VMEM)
    return pl.pallas_call(
        body, name="small_allreduce", in_specs=[vm], out_specs=vm,
        out_shape=jax.ShapeDtypeStruct((rows, cols), F32),
        scratch_shapes=[pltpu.VMEM((N_DEV, rows, cols), F32), pltpu.SemaphoreType.DMA((N_DEV - 1,)),
                        pltpu.SemaphoreType.DMA((N_DEV - 1,))],
    )(buf)


def _adamw_math(w, gv, m, v):
    mn = ADAM_B1 * m + (1.0 - ADAM_B1) * gv
    vn = ADAM_B2 * v + (1.0 - ADAM_B2) * (gv * gv)
    m_hat = mn / (1.0 - ADAM_B1 ** ADAM_STEP)
    v_hat = vn / (1.0 - ADAM_B2 ** ADAM_STEP)
    return -ADAM_LR * (m_hat / (jnp.sqrt(v_hat) + ADAM_EPS) + ADAM_WD * w), mn, vn


def _adamw(w, g, m, v):
    rows, cols = w.shape
    tr = _row_tile(rows)

    def body(w_ref, g_ref, m_ref, v_ref, d_ref, mo_ref, vo_ref):
        d_ref[...], mo_ref[...], vo_ref[...] = _adamw_math(w_ref[...], g_ref[...], m_ref[...], v_ref[...])

    blk = pl.BlockSpec((tr, cols), lambda i: (i, 0))
    shp = jax.ShapeDtypeStruct((rows, cols), F32)
    return pl.pallas_call(
        body, name="adamw", grid=(rows // tr,), in_specs=[blk] * 4, out_specs=[blk] * 3, out_shape=[shp] * 3,
        compiler_params=_params(1),
    )(w, g, m, v)


def _adamw_halves(w, own, got, m, v, c_idx):
    rows, cols = w.shape
    tr = _row_tile(rows // 2)
    per_half = rows // 2 // tr

    def body(c_ref, w_ref, own_ref, got_ref, m_ref, v_ref, d_ref, mo_ref, vo_ref, g_ref):
        mine = (pl.program_id(0) // per_half) == c_ref[0]
        gv = jnp.where(mine, own_ref[...], got_ref[...])
        g_ref[...] = gv
        d_ref[...], mo_ref[...], vo_ref[...] = _adamw_math(w_ref[...], gv, m_ref[...], v_ref[...])

    blk = pl.BlockSpec((tr, cols), lambda i, c: (i, 0))
    own_blk = pl.BlockSpec((tr, cols), lambda i, c: (jnp.where(i // per_half == c[0], i % per_half, 0), 0))
    got_blk = pl.BlockSpec((tr, cols), lambda i, c: (jnp.where(i // per_half == c[0], 0, i % per_half), 0))
    shp = jax.ShapeDtypeStruct((rows, cols), F32)
    return pl.pallas_call(
        body, name="adamw_halves",
        grid_spec=pltpu.PrefetchScalarGridSpec(num_scalar_prefetch=1, grid=(rows // tr,),
                                               in_specs=[blk, own_blk, got_blk, blk, blk], out_specs=[blk] * 4),
        out_shape=[shp] * 4, compiler_params=_params(1),
    )(c_idx, w, own, got, m, v)


def _local_step(x, target, norm_gains, q_g, k_g, ng, weights_of, grads_done):
    s = x.shape[0]
    tm = min(512, s)
    tq = min(256, s)
    g1, gm, g2, gf = norm_gains
    cos2, sin2 = _rope_tables(s)
    gq8 = jnp.tile(q_g, (1, 8))
    gk2 = jnp.tile(k_g, (1, 2))

    tn = min(256, s)
    tk = min(1024, s)
    w1 = weights_of(1, ())
    x1, s1, t1, b1, h1 = _ffn_fwd(x, g1, w1["g1"], w1["u1"], w1["d1"], tm)
    w2 = weights_of(2, (x1,))
    lbl = w2["lbl"]
    pqkv, ph, pg, hm = _mix_in_fwd(x1, gm, w2["in"], tn)
    qe, kr, vr, vs = _qk_prep(pqkv, gq8, gk2, cos2, sin2, tm)
    oa, lse = _attn_fwd(qe, kr, vr, vs, tq)
    ob, pre, hstates = _hgrn_fwd(ph, lbl, ng)
    x2 = _mix_out_fwd(x1, oa, ob, pg, w2["a"], w2["b"], w2["o"], tm)
    w3 = weights_of(3, (x2,))
    x3, s2, t2, b2, h2 = _ffn_fwd(x2, g2, w3["g2"], w3["u2"], w3["d2"], tm)
    dx3, loss, dgf = _loss_head(x3, gf, target, tm)

    dx2, da2, db2, f2, dg2, dx3c = _ffn_bwd(dx3, x2, g2, s2, t2, b2, w3["g2"], w3["u2"], w3["d2"], tm)
    tok = grads_done(3, dict(g2=_dw_shared_b("dw_gate", da2, h2, tk, 1.0), u2=_dw_shared_b("dw_gate", db2, h2, tk, 1.0),
                             d2=_dw_shared_b("dw_down", f2, dx3c, tk, 0.5)))

    dpg, mg, dya, dyb, doe, delta, dob = _mix_out_bwd(dx2, oa, ob, pg, w2["a"], w2["b"], w2["o"], tm, tok)
    g_o = [g.reshape(N_SHARD, D_MODEL // N_SHARD, D_MODEL) for g in _dw_colblocks("dw_out", mg, dx2, 1, tk)]
    g_a = _dw_colblocks("dw_branch", oa, dya, N_SHARD, tk)
    g_b = _dw_colblocks("dw_branch", ob, dyb, N_SHARD, tk)
    dqt, dkt, dvt = _attn_bwd(qe, kr, kr.T, vr, doe, delta, lse, tq)
    dqkv, dgq, dgk = _qk_prep_bwd(pqkv, dqt.T, dkt.T, dvt.T, gq8, gk2, cos2, sin2, tm)
    dhq, dhff, dhfb, dhi, dhg, dlb, dng = _hgrn_bwd(ph, pre, dob, hstates, lbl, ng)
    dps = (dqkv, dhq, dhff, dhfb, dhi, dhg, dpg)
    g_in = [g.reshape(N_SHARD, -1, D_MODEL) for g in _dw_in(dps, hm, min(2048, s))]
    tok = grads_done(2, {"in": g_in, "a": g_a, "b": g_b, "o": g_o})
    dx1, dgm = _mix_in_bwd(dps, w2["in"], x1, dx2, gm, tn, tok)

    dx0, da1, db1, f1, dg1, dx1c = _ffn_bwd(dx1, x, g1, s1, t1, b1, w1["g1"], w1["u1"], w1["d1"], tm)
    grads_done(1, dict(g1=_dw_shared_b("dw_gate", da1, h1, tk, 1.0), u1=_dw_shared_b("dw_gate", db1, h1, tk, 1.0),
                       d1=_dw_shared_b("dw_down", f1, dx1c, tk, 0.5)))
    small = dict(g1=dg1, gm=dgm, g2=dg2, gf=dgf, gq=dgq, gk=dgk, lb=dlb, ng=dng)
    return loss, dx0, small, lbl


GROUPS = {1: ("g1", "u1", "d1"), 2: ("in", "a", "b", "o"), 3: ("g2", "u2", "d2")}
BIG = GROUPS[1] + GROUPS[2] + GROUPS[3]
TRANSPOSED = ("g1", "u1", "in", "g2", "u2")


def _pack_rows(vectors, width):
    rows = []
    for vct in vectors:
        flat = vct.reshape(-1)
        pad = (-flat.shape[0]) % width
        rows.append(jnp.pad(flat, (0, pad)).reshape(-1, width))
    return jnp.concatenate(rows, axis=0)


def kernel(x, ffn1_norm_g, ffn1_w_gate, ffn1_w_up, ffn1_w_down, mix_norm_g, w_in, q_norm_g, k_norm_g, hgrn_lb_logits, hgrn_out_norm_g, w_branch_attn, w_branch_hgrn, w_out, ffn2_norm_g, ffn2_w_gate, ffn2_w_up, ffn2_w_down, final_norm_g, loss_target, m_ffn1_norm_g, m_ffn1_w_gate, m_ffn1_w_up, m_ffn1_w_down, m_mix_norm_g, m_w_in, m_q_norm_g, m_k_norm_g, m_hgrn_lb_logits, m_hgrn_out_norm_g, m_w_branch_attn, m_w_branch_hgrn, m_w_out, m_ffn2_norm_g, m_ffn2_w_gate, m_ffn2_w_up, m_ffn2_w_down, m_final_norm_g, v_ffn1_norm_g, v_ffn1_w_gate, v_ffn1_w_up, v_ffn1_w_down, v_mix_norm_g, v_w_in, v_q_norm_g, v_k_norm_g, v_hgrn_lb_logits, v_hgrn_out_norm_g, v_w_branch_attn, v_w_branch_hgrn, v_w_out, v_ffn2_norm_g, v_ffn2_w_gate, v_ffn2_w_up, v_ffn2_w_down, v_final_norm_g):
    xi, yi, ci = lax.axis_index("x"), lax.axis_index("y"), lax.axis_index("c")
    me = 2 * xi + yi
    c_idx = jnp.reshape(ci, (1,)).astype(jnp.int32)
    me_idx = jnp.reshape(me, (1,)).astype(jnp.int32)

    big_w = dict(g1=ffn1_w_gate[0], u1=ffn1_w_up[0], d1=ffn1_w_down[0], a=w_branch_attn[0], b=w_branch_hgrn[0],
                 o=w_out[0], g2=ffn2_w_gate[0], u2=ffn2_w_up[0], d2=ffn2_w_down[0])
    big_w["in"] = w_in[0]
    big_m = dict(g1=m_ffn1_w_gate[0], u1=m_ffn1_w_up[0], d1=m_ffn1_w_down[0], a=m_w_branch_attn[0], b=m_w_branch_hgrn[0],
                 o=m_w_out[0], g2=m_ffn2_w_gate[0], u2=m_ffn2_w_up[0], d2=m_ffn2_w_down[0])
    big_m["in"] = m_w_in[0]
    big_v = dict(g1=v_ffn1_w_gate[0], u1=v_ffn1_w_up[0], d1=v_ffn1_w_down[0], a=v_w_branch_attn[0], b=v_w_branch_hgrn[0],
                 o=v_w_out[0], g2=v_ffn2_w_gate[0], u2=v_ffn2_w_up[0], d2=v_ffn2_w_down[0])
    big_v["in"] = v_w_in[0]
    for table in (big_w, big_m, big_v):
        for n in TRANSPOSED:
            table[n] = table[n].T

    slots = {n: _cast_into_slot(big_w[n], me_idx, CD) for n in BIG}
    lbl_slot = _cast_into_slot(hgrn_lb_logits.reshape(4, LANES), me_idx, F32)
    started, token = {}, ()
    for grp in (1, 2, 3):
        lands = [slots[n] for n in GROUPS[grp]] + ([lbl_slot] if grp == 2 else [])
        send, recv, _, lands, tok = _exchange_start("gather%d_start" % grp, [], lands, token)
        started[grp], token = (send, recv, lands), (tok,)

    def weights_of(grp, after):
        send, recv, lands = started[grp]
        got = _exchange_wait("gather%d_wait" % grp, send, recv, [], lands, tuple(after) + (token if grp == 1 else ()))
        by_halves = [i for i, land in enumerate(got) if _halved(land)]
        for i, whole in zip(by_halves, _pair_fill("gather%d_fill" % grp, [got[i] for i in by_halves])):
            got[i] = whole
        w = dict(zip(GROUPS[grp], got))
        if grp == 2:
            w["in"] = w["in"].reshape(-1, D_MODEL)
            w["o"] = w["o"].reshape(D_MODEL, D_MODEL)
            w["lbl"] = jnp.transpose(got[-1], (1, 0, 2)).reshape(4, N_SHARD * LANES)
        return w

    pending = {}

    def grads_done(grp, grads):
        names = list(grads)
        got = _pair_exchange([grads[n][1] for n in names])
        sums, owns = zip(*[_pair_sum(grads[n][0], r, c_idx, me_idx) for n, r in zip(names, got)])
        lands = [lax.empty(s_.shape, s_.dtype) for s_ in sums]
        send, recv, srcs, lands, tok = _exchange_start("reduce%d_start" % grp, list(sums), lands, ())
        pending[grp] = (names, send, recv, srcs, lands, owns, tok)
        return (tok,)

    def reduced_halves(grp, after):
        names, send, recv, srcs, lands, owns, _ = pending[grp]
        parts = _exchange_wait("reduce%d_wait" % grp, send, recv, srcs, lands, after)
        return names, [_chip_sum(o, p, me_idx) for o, p in zip(owns, parts)]

    loss, dx, small, lbl = _local_step(
        x[0], loss_target[0], (ffn1_norm_g, mix_norm_g, ffn2_norm_g, final_norm_g.reshape(1, -1)),
        q_norm_g, k_norm_g, hgrn_out_norm_g, weights_of, grads_done)

    dgq = small["gq"].reshape(8, HEAD_DIM).sum(axis=0)
    dgk = small["gk"].reshape(2, HEAD_DIM).sum(axis=0)
    lb_full = _hgrn_lower_bounds(lbl)
    dlog = []
    for d in (0, 1):
        t = small["lb"][d:d + 1] * lb_full[d] * (1.0 - lb_full[d])
        dlog += [t, -t]
    small_list = [small["g1"], small["gm"], small["g2"], small["gf"], small["ng"], dgq, dgk, jnp.concatenate(dlog, axis=0), loss[0, 0]]
    packed = _pack_rows(small_list, D_MODEL)
    n_rows = packed.shape[0]
    packed = jnp.pad(packed, ((0, (-n_rows) % 8), (0, 0)))
    red = _small_allreduce(packed)
    loss_out = red[n_rows - 1, 0]
    sg = dict(g1=red[0:1], gm=red[1:2], g2=red[2:3], gf=red[3], ng=red[4:5, :512], gq=red[5:6, :HEAD_DIM],
              gk=red[6:7, :HEAD_DIM])
    dlog_full = red[7:9].reshape(2, 2, 512)
    sg["lb"] = lax.dynamic_slice_in_dim(dlog_full, me * LANES, LANES, axis=2)

    small_w = dict(g1=ffn1_norm_g, gm=mix_norm_g, g2=ffn2_norm_g, gf=final_norm_g, ng=hgrn_out_norm_g, gq=q_norm_g,
                   gk=k_norm_g, lb=hgrn_lb_logits)
    small_m = dict(g1=m_ffn1_norm_g, gm=m_mix_norm_g, g2=m_ffn2_norm_g, gf=m_final_norm_g, ng=m_hgrn_out_norm_g,
                   gq=m_q_norm_g, gk=m_k_norm_g, lb=m_hgrn_lb_logits)
    small_v = dict(g1=v_ffn1_norm_g, gm=v_mix_norm_g, g2=v_ffn2_norm_g, gf=v_final_norm_g, ng=v_hgrn_out_norm_g,
                   gq=v_q_norm_g, gk=v_k_norm_g, lb=v_hgrn_lb_logits)
    small_names = ("g1", "gm", "g2", "gf", "ng", "gq", "gk", "lb")
    pack = lambda dct: _pack_rows([dct[n] for n in small_names], D_MODEL)
    pw, pgr, pm, pv = pack(small_w), pack(sg), pack(small_m), pack(small_v)
    pad8 = lambda a: jnp.pad(a, ((0, (-a.shape[0]) % 8), (0, 0)))
    sd, sm_, sv_ = _adamw(pad8(pw), pad8(pgr), pad8(pm), pad8(pv))

    def unpack(buf):
        out, r = {}, 0
        for n in small_names:
            size = small_w[n].size
            nr = -(-size // D_MODEL)
            out[n] = buf[r:r + nr].reshape(-1)[:size].reshape(small_w[n].shape)
            r += nr
        return out

    sdelta, snew_m, snew_v = unpack(sd), unpack(sm_), unpack(sv_)
    sgrad = {n: sg[n].reshape(small_w[n].shape) for n in small_names}

    bdelta, bnew_m, bnew_v, bgrad = {}, {}, {}, {}

    def update(names, halves):
        for n, own, got in zip(names, halves, _pair_share(halves)):
            res = _adamw_halves(big_w[n], own, got, big_m[n], big_v[n], c_idx)
            if n in TRANSPOSED:
                res = [r.T for r in res]
            bdelta[n], bnew_m[n], bnew_v[n], bgrad[n] = [r[None] for r in res]

    names3, halves3 = reduced_halves(3, (pending[1][-1],))
    names2, halves2 = reduced_halves(2, (halves3[0],))
    update(names3 + names2, halves3 + halves2)
    names1, halves1 = reduced_halves(1, (bdelta[names2[-1]],))
    update(names1, halves1)

    order = [("s", "g1"), ("b", "g1"), ("b", "u1"), ("b", "d1"), ("s", "gm"), ("b", "in"), ("s", "gq"), ("s", "gk"),
             ("s", "lb"), ("s", "ng"), ("b", "a"), ("b", "b"), ("b", "o"), ("s", "g2"), ("b", "g2"), ("b", "u2"),
             ("b", "d2"), ("s", "gf")]
    outs = [loss_out, dx[None]]
    for table_s, table_b in ((sgrad, bgrad), (sdelta, bdelta), (snew_m, bnew_m), (snew_v, bnew_v)):
        outs += [(table_s if kind == "s" else table_b)[n] for kind, n in order]
    return tuple(outs)
```

```python
import functools

import numpy as np
import jax
import jax.numpy as jnp
from jax import lax
from jax.experimental import pallas as pl
from jax.experimental.pallas import tpu as pltpu

F32 = jnp.float32
BF16 = jnp.bfloat16
CD = jnp.bfloat16

EPS = 1e-6
D_MODEL = 1024
HEAD_DIM = 64
GRID_W = 64
ROPE_THETA = 10000.0
CHUNK = 32
N_SHARD = 4
N_DEV = 8
VMEM_LIMIT = 56 * 1024 * 1024
LANES = 128
HG_TILE = 256
FFN_ROWS = 256

ADAM_LR = 0.001
ADAM_B1 = 0.9
ADAM_B2 = 0.999
ADAM_EPS = 1e-08
ADAM_WD = 0.01
ADAM_STEP = 10

NN = (((1,), (0,)), ((), ()))
NT = (((1,), (1,)), ((), ()))
TN = (((0,), (0,)), ((), ()))
MESH = pl.DeviceIdType.MESH
ANY = pl.BlockSpec(memory_space=pl.ANY)


def _mm(a, b, dn):
    return lax.dot_general(a.astype(CD), b.astype(CD), dn, preferred_element_type=F32)


def _split3(x):
    hi = x.astype(BF16)
    r = x - hi.astype(F32)
    mid = r.astype(BF16)
    lo = (r - mid.astype(F32)).astype(BF16)
    return hi, mid, lo


def _xdot(x, m):
    rows = x.shape[0]
    r = lax.dot_general(jnp.concatenate(_split3(x), axis=0), m, NN, preferred_element_type=F32)
    return r[:rows] + r[rows:2 * rows] + r[2 * rows:]


def _xdot_l(m, x):
    cols = x.shape[1]
    hi, mid, _ = _split3(x)
    r = lax.dot_general(m, jnp.concatenate([hi, mid], axis=1), NN, preferred_element_type=F32)
    return r[:, :cols] + r[:, cols:]


def _params(n_grid):
    return pltpu.CompilerParams(dimension_semantics=("arbitrary",) * n_grid, vmem_limit_bytes=VMEM_LIMIT)


def _sigmoid(x):
    return jax.nn.sigmoid(x)


def _np_blocksum(n):
    i = np.arange(n)
    return (i[:, None] // HEAD_DIM == i[None, :] // HEAD_DIM).astype(np.float32)


def _np_swap32(n):
    i = np.arange(n)
    partner = np.where(i % HEAD_DIM < HEAD_DIM // 2, i + HEAD_DIM // 2, i - HEAD_DIM // 2)
    m = np.zeros((n, n), np.float32)
    m[i, partner] = 1.0
    return m


def _np_expand_q():
    m = np.zeros((512, 1024), np.float32)
    for h in range(8):
        g = h // 4
        for d in range(HEAD_DIM):
            m[64 * h + d, 128 * h + 64 * g + d] = 1.0
    return m


def _np_bcast_head():
    m = np.zeros((512, 1024), np.float32)
    for h in range(8):
        m[64 * h, 128 * h:128 * h + 128] = 1.0
    return m


def _np_swap_halves():
    m = np.zeros((128, 128), np.float32)
    i = np.arange(128)
    m[i, (i + 64) % 128] = 1.0
    return m


def _np_hgrn_cums(t, rev):
    r = np.arange(t)[:, None]
    c = np.arange(t)[None, :]
    same = (r // CHUNK) == (c // CHUNK)
    if not rev:
        cum = same & (c <= r)
        mid = same & (c % CHUNK <= CHUNK // 2 - 1)
    else:
        cum = same & (c >= r)
        mid = same & (c % CHUNK >= CHUNK // 2)
    return np.concatenate([cum, mid, same], axis=0).astype(np.float32)


def _bf(a):
    return jnp.asarray(a, dtype=BF16)


def _rope_tables(seq_len):
    rows = seq_len // GRID_W
    row = jnp.repeat(jnp.arange(rows, dtype=F32), GRID_W)
    col = jnp.tile(jnp.arange(GRID_W, dtype=F32), rows)
    n_freq = HEAD_DIM // 4
    inv = ROPE_THETA ** (-jnp.arange(n_freq, dtype=F32) / n_freq)
    ang = jnp.concatenate([row[:, None] * inv, col[:, None] * inv], axis=-1)
    cos, sin = jnp.cos(ang), jnp.sin(ang)
    c64 = jnp.concatenate([cos, cos], axis=-1)
    s64 = jnp.concatenate([-sin, sin], axis=-1)
    return jnp.tile(c64, (1, 2)), jnp.tile(s64, (1, 2))


def _ffn_fwd(x, g, wg, wu, wd, tm):
    s, d = x.shape
    nsh, fs, _ = wg.shape

    def body(x_ref, g_ref, wg_ref, wu_ref, wd_ref, xo_ref, a_ref, da_ref, b_ref, hb_ref, acc, hs):
        j = pl.program_id(1)

        @pl.when(j == 0)
        def _():
            xv = x_ref[...]
            r = lax.rsqrt(jnp.mean(xv * xv, axis=-1, keepdims=True) + EPS)
            h = (xv * r * g_ref[...]).astype(CD)
            hs[...] = h
            hb_ref[...] = h
            acc[...] = jnp.zeros_like(acc)

        for r0 in range(0, tm, FFN_ROWS):
            rows = slice(r0, min(r0 + FFN_ROWS, tm))
            h = hs[rows, :]
            a = _mm(h, wg_ref[0], NT)
            b = _mm(h, wu_ref[0], NT)
            sg = _sigmoid(a)
            silu = a * sg
            acc[rows, :] += _mm(silu * b, wd_ref[0], NN)
            a_ref[0, rows, :] = silu.astype(CD)
            da_ref[0, rows, :] = (sg * (1.0 + a * (1.0 - sg))).astype(CD)
            b_ref[0, rows, :] = b.astype(CD)

        @pl.when(j == nsh - 1)
        def _():
            xo_ref[...] = x_ref[...] + 0.5 * acc[...]

    return pl.pallas_call(
        body, name="ffn_fwd", grid=(s // tm, nsh),
        in_specs=[pl.BlockSpec((tm, d), lambda i, j: (i, 0)), pl.BlockSpec((1, d), lambda i, j: (0, 0))]
        + [pl.BlockSpec((1, fs, d), lambda i, j: (j, 0, 0))] * 3,
        out_specs=[pl.BlockSpec((tm, d), lambda i, j: (i, 0))] + [pl.BlockSpec((1, tm, fs), lambda i, j: (j, i, 0))] * 3
        + [pl.BlockSpec((tm, d), lambda i, j: (i, 0))],
        out_shape=[jax.ShapeDtypeStruct((s, d), F32)] + [jax.ShapeDtypeStruct((nsh, s, fs), CD)] * 3
        + [jax.ShapeDtypeStruct((s, d), CD)],
        scratch_shapes=[pltpu.VMEM((tm, d), F32), pltpu.VMEM((tm, d), CD)],
        compiler_params=_params(2),
    )(x, g, wg, wu, wd)


def _ffn_bwd(dout, x, g, silu, dsilu, b, wg, wu, wd, tm):
    s, d = x.shape
    nsh, fs, _ = wg.shape

    def body(do_ref, x_ref, g_ref, sl_ref, ds_ref, b_ref, wg_ref, wu_ref, wd_ref,
             dx_ref, da_ref, db_ref, f_ref, dg_ref, do16_ref, dh):
        i = pl.program_id(0)
        j = pl.program_id(1)

        @pl.when(j == 0)
        def _():
            dh[...] = jnp.zeros_like(dh)
            do16_ref[...] = do_ref[...].astype(CD)

        @pl.when((i == 0) & (j == 0))
        def _():
            dg_ref[...] = jnp.zeros_like(dg_ref)

        for r0 in range(0, tm, FFN_ROWS):
            rows = slice(r0, min(r0 + FFN_ROWS, tm))
            sl = sl_ref[0, rows, :].astype(F32)
            bv = b_ref[0, rows, :].astype(F32)
            df = 0.5 * _mm(do_ref[rows, :], wd_ref[0], NT)
            da = df * bv * ds_ref[0, rows, :].astype(F32)
            db = df * sl
            dh[rows, :] += _mm(da, wg_ref[0], NN) + _mm(db, wu_ref[0], NN)
            da_ref[0, rows, :] = da.astype(CD)
            db_ref[0, rows, :] = db.astype(CD)
            f_ref[0, rows, :] = (sl * bv).astype(CD)

        @pl.when(j == nsh - 1)
        def _():
            xv = x_ref[...]
            r = lax.rsqrt(jnp.mean(xv * xv, axis=-1, keepdims=True) + EPS)
            dhv = dh[...]
            u = dhv * g_ref[...]
            dx_ref[...] = do_ref[...] + r * u - xv * (r * r * r) * jnp.mean(u * xv, axis=-1, keepdims=True)
            dg_ref[...] += jnp.sum(dhv * xv * r, axis=0, keepdims=True)

    act = pl.BlockSpec((1, tm, fs), lambda i, j: (j, i, 0))
    row = pl.BlockSpec((tm, d), lambda i, j: (i, 0))
    return pl.pallas_call(
        body, name="ffn_bwd", grid=(s // tm, nsh),
        in_specs=[row, row, pl.BlockSpec((1, d), lambda i, j: (0, 0)), act, act, act]
        + [pl.BlockSpec((1, fs, d), lambda i, j: (j, 0, 0))] * 3,
        out_specs=[row, act, act, act, pl.BlockSpec((1, d), lambda i, j: (0, 0)), row],
        out_shape=[jax.ShapeDtypeStruct((s, d), F32), jax.ShapeDtypeStruct((nsh, s, fs), CD),
                   jax.ShapeDtypeStruct((nsh, s, fs), CD), jax.ShapeDtypeStruct((nsh, s, fs), CD),
                   jax.ShapeDtypeStruct((1, d), F32), jax.ShapeDtypeStruct((s, d), CD)],
        scratch_shapes=[pltpu.VMEM((tm, d), F32)],
        compiler_params=_params(2),
    )(dout, x, g, silu, dsilu, b, wg, wu, wd)


def _tn_call(name, operands, in_specs, out_shape, out_spec, grid, acc_shape, pick, scale=1.0):
    nk = grid[-1]
    n_in = len(operands)

    def body(*refs):
        out_ref, out16_ref, acc = refs[n_in], refs[n_in + 1], refs[n_in + 2]
        k = pl.program_id(len(grid) - 1)

        @pl.when(k == 0)
        def _():
            acc[...] = jnp.zeros_like(acc)

        pick(refs[:n_in], acc)

        @pl.when(k == nk - 1)
        def _():
            res = (acc[...] if scale == 1.0 else acc[...] * scale).reshape(out_ref.shape)
            out_ref[...] = res
            out16_ref[...] = res.astype(CD)

    return pl.pallas_call(
        body, name=name, grid=grid, in_specs=in_specs, out_specs=[out_spec, out_spec],
        out_shape=[out_shape, jax.ShapeDtypeStruct(out_shape.shape, CD)],
        scratch_shapes=[pltpu.VMEM(acc_shape, F32)], compiler_params=_params(len(grid)),
    )(*operands)


def _dw_shared_b(name, a3, b, tk, scale):
    nj, s, m = a3.shape
    n = b.shape[1]

    def pick(refs, acc):
        rows = pl.ds(pl.multiple_of(pl.program_id(1) * tk, tk), tk)
        acc[...] += _mm(refs[0][0], refs[1][rows, :], TN)

    return _tn_call(name, (a3, b),
                    [pl.BlockSpec((1, tk, m), lambda j, k: (j, k, 0)), pl.BlockSpec((s, n), lambda j, k: (0, 0))],
                    jax.ShapeDtypeStruct((nj, m, n), F32), pl.BlockSpec((1, m, n), lambda j, k: (j, 0, 0)),
                    (nj, s // tk), (m, n), pick, scale)


def _dw_colblocks(name, a, b, nj, tk):
    s, m = a.shape
    n = b.shape[1] // nj

    def pick(refs, acc):
        acc[...] += _mm(refs[0][...], refs[1][...], TN)

    return _tn_call(name, (a, b),
                    [pl.BlockSpec((tk, m), lambda j, k: (k, 0)), pl.BlockSpec((tk, n), lambda j, k: (k, j))],
                    jax.ShapeDtypeStruct((nj, m, n), F32), pl.BlockSpec((1, m, n), lambda j, k: (j, 0, 0)),
                    (nj, s // tk), (m, n), pick)


DP_WIDTHS = (768, 512, 512, 512, 512, 512, 2048)
DP_CHUNK = 256


def _dp_chunk_maps():
    starts, counts, off = [], [], 0
    for w in DP_WIDTHS:
        starts.append(off // DP_CHUNK)
        counts.append(w // DP_CHUNK)
        off += w
    return starts, counts


def _dp_specs(tm, row_axis, chunk_axis):
    starts, counts = _dp_chunk_maps()
    specs = []
    for st, cnt in zip(starts, counts):
        def imap(*ids, st=st, cnt=cnt):
            return (ids[row_axis], jnp.clip(ids[chunk_axis] - st, 0, cnt - 1))
        specs.append(pl.BlockSpec((tm, DP_CHUNK), imap))
    return specs


def _dp_select(n, refs, fn):
    starts, counts = _dp_chunk_maps()
    for ref, st, cnt in zip(refs, starts, counts):
        @pl.when((n >= st) & (n < st + cnt))
        def _(ref=ref):
            fn(ref)


def _dw_in(dps, hb, tk):
    s, d = hb.shape
    n_chunks = sum(DP_WIDTHS) // DP_CHUNK

    def pick(refs, acc):
        rows = pl.ds(pl.multiple_of(pl.program_id(1) * tk, tk), tk)

        def add(ref):
            acc[...] += _mm(ref[...], refs[7][rows, :], TN)

        _dp_select(pl.program_id(0), refs[:7], add)

    return _tn_call("dw_in", (*dps, hb),
                    _dp_specs(tk, 1, 0) + [pl.BlockSpec((s, d), lambda n, k: (0, 0))],
                    jax.ShapeDtypeStruct((n_chunks * DP_CHUNK, d), F32), pl.BlockSpec((DP_CHUNK, d), lambda n, k: (n, 0)),
                    (n_chunks, s // tk), (DP_CHUNK, d), pick)


def _mix_in_fwd(x, g, w_t, tm):
    s, d = x.shape
    n_in = w_t.shape[0]

    def body(x_ref, g_ref, w_ref, qkv_ref, hg_ref, gt_ref, hb_ref):
        xv = x_ref[...]
        r = lax.rsqrt(jnp.mean(xv * xv, axis=-1, keepdims=True) + EPS)
        h = (xv * r * g_ref[...]).astype(CD)
        hb_ref[...] = h
        qkv_ref[...] = _mm(h, w_ref[0:768, :], NT)
        for c in range(5):
            hg_ref[:, 512 * c:512 * c + 512] = _mm(h, w_ref[768 + 512 * c:768 + 512 * c + 512, :], NT)
        for c in range(2):
            gt_ref[:, 1024 * c:1024 * c + 1024] = _mm(h, w_ref[3328 + 1024 * c:3328 + 1024 * c + 1024, :], NT)

    row = lambda w: pl.BlockSpec((tm, w), lambda i: (i, 0))
    return pl.pallas_call(
        body, name="mix_in_fwd", grid=(s // tm,),
        in_specs=[row(d), pl.BlockSpec((1, d), lambda i: (0, 0)), pl.BlockSpec((n_in, d), lambda i: (0, 0))],
        out_specs=[row(768), row(2560), row(2048), row(d)],
        out_shape=[jax.ShapeDtypeStruct((s, 768), F32), jax.ShapeDtypeStruct((s, 2560), F32),
                   jax.ShapeDtypeStruct((s, 2048), F32), jax.ShapeDtypeStruct((s, d), CD)],
        compiler_params=_params(1),
    )(x, g, w_t)


def _mix_in_bwd(dps, w_t, x, dres, g, tm, after=()):
    s, d = x.shape
    n_in = w_t.shape[0]

    def body(*refs):
        refs = refs[len(after):]
        dp_refs = refs[:7]
        w_ref, x_ref, dr_ref, g_ref, dx_ref, dg_ref = refs[7:]

        @pl.when(pl.program_id(0) == 0)
        def _():
            dg_ref[...] = jnp.zeros_like(dg_ref)

        dhv = jnp.zeros((tm, d), F32)
        off = 0
        for ref, width in zip(dp_refs, DP_WIDTHS):
            dhv = dhv + _mm(ref[...], w_ref[off:off + width, :], NN)
            off += width
        xv = x_ref[...]
        r = lax.rsqrt(jnp.mean(xv * xv, axis=-1, keepdims=True) + EPS)
        u = dhv * g_ref[...]
        dx_ref[...] = dr_ref[...] + r * u - xv * (r * r * r) * jnp.mean(u * xv, axis=-1, keepdims=True)
        dg_ref[...] += jnp.sum(dhv * xv * r, axis=0, keepdims=True)

    row = pl.BlockSpec((tm, d), lambda i: (i, 0))
    vec = pl.BlockSpec((1, d), lambda i: (0, 0))
    return pl.pallas_call(
        body, name="mix_in_bwd", grid=(s // tm,),
        in_specs=[ANY] * len(after) + [pl.BlockSpec((tm, w), lambda i: (i, 0)) for w in DP_WIDTHS]
        + [pl.BlockSpec((n_in, d), lambda i: (0, 0)), row, row, vec],
        out_specs=[row, vec],
        out_shape=[jax.ShapeDtypeStruct((s, d), F32), jax.ShapeDtypeStruct((1, d), F32)],
        compiler_params=_params(1),
    )(*after, *dps, w_t, x, dres, g)


def _headnorm_rope(x, gain, cos, sin, blocksum, swap):
    ss = _xdot(x * x, blocksum)
    r = lax.rsqrt(ss * (1.0 / HEAD_DIM) + EPS)
    y = x * r * gain
    return y * cos + _xdot(y, swap) * sin, r


def _headnorm_rope_bwd(dz, x, gain, cos, sin, blocksum, swap):
    ss = _xdot(x * x, blocksum)
    r = lax.rsqrt(ss * (1.0 / HEAD_DIM) + EPS)
    dy = dz * cos + _xdot(dz * sin, swap)
    u = dy * gain
    mean_ux = _xdot(u * x, blocksum) * (1.0 / HEAD_DIM)
    dx = r * u - x * (r * r * r) * mean_ux
    return dx, jnp.sum(dy * x * r, axis=0, keepdims=True)


def _qk_prep(pqkv, gq, gk, cos2, sin2, tm):
    s = pqkv.shape[0]
    bs512, sw512, eq, swh = _bf(_np_blocksum(512)), _bf(_np_swap32(512)), _bf(_np_expand_q()), _bf(_np_swap_halves())

    def body(q_ref, kv_ref, gq_ref, gk_ref, c_ref, s_ref, bs_ref, sw_ref, eq_ref, swh_ref, qe_ref, k_ref, v_ref, vs_ref):
        c2, s2 = c_ref[...], s_ref[...]
        c8, s8 = jnp.tile(c2, (1, 4)), jnp.tile(s2, (1, 4))
        bs, sw = bs_ref[...], sw_ref[...]
        zq, _ = _headnorm_rope(q_ref[...], gq_ref[...], c8, s8, bs, sw)
        qe_ref[...] = _mm(zq * (HEAD_DIM ** -0.5), eq_ref[...], NN).astype(CD)
        kv = kv_ref[...]
        zk, _ = _headnorm_rope(kv[:, :LANES], gk_ref[...], c2, s2, bs[:LANES, :LANES], sw[:LANES, :LANES])
        k_ref[...] = zk.astype(CD)
        v = kv[:, LANES:]
        v_ref[...] = v.astype(CD)
        vs_ref[...] = _mm(v, swh_ref[...], NN).astype(CD)

    full = lambda a: pl.BlockSpec(a.shape, lambda i: (0,) * a.ndim)
    tab = pl.BlockSpec((tm, LANES), lambda i: (i, 0))
    return pl.pallas_call(
        body, name="qk_prep", grid=(s // tm,),
        in_specs=[pl.BlockSpec((tm, 512), lambda i: (i, 0)), pl.BlockSpec((tm, 256), lambda i: (i, 2)),
                  full(gq), full(gk), tab, tab, full(bs512), full(sw512), full(eq), full(swh)],
        out_specs=[pl.BlockSpec((tm, 1024), lambda i: (i, 0)), tab, tab, tab],
        out_shape=[jax.ShapeDtypeStruct((s, 1024), CD)] + [jax.ShapeDtypeStruct((s, LANES), CD)] * 3,
        compiler_params=_params(1),
    )(pqkv, pqkv, gq, gk, cos2, sin2, bs512, sw512, eq, swh)


def _qk_prep_bwd(pqkv, dq, dk, dv, gq, gk, cos2, sin2, tm):
    s = pqkv.shape[0]
    bs512, sw512 = _bf(_np_blocksum(512)), _bf(_np_swap32(512))

    def body(q_ref, kv_ref, dq_ref, dk_ref, dv_ref, gq_ref, gk_ref, c_ref, s_ref, bs_ref, sw_ref,
             dp_ref, dgq_ref, dgk_ref):
        @pl.when(pl.program_id(0) == 0)
        def _():
            dgq_ref[...] = jnp.zeros_like(dgq_ref)
            dgk_ref[...] = jnp.zeros_like(dgk_ref)

        c2, s2 = c_ref[...], s_ref[...]
        c8, s8 = jnp.tile(c2, (1, 4)), jnp.tile(s2, (1, 4))
        bs, sw = bs_ref[...], sw_ref[...]
        dzq = dq_ref[...] * (HEAD_DIM ** -0.5)
        dxq, dgq = _headnorm_rope_bwd(dzq, q_ref[...], gq_ref[...], c8, s8, bs, sw)
        kv = kv_ref[...]
        dxk, dgk = _headnorm_rope_bwd(dk_ref[...], kv[:, :LANES], gk_ref[...], c2, s2, bs[:LANES, :LANES], sw[:LANES, :LANES])
        dp_ref[...] = jnp.concatenate([dxq, dxk, dv_ref[...]], axis=1).astype(CD)
        dgq_ref[...] += dgq
        dgk_ref[...] += dgk

    full = lambda a: pl.BlockSpec(a.shape, lambda i: (0,) * a.ndim)
    tab = pl.BlockSpec((tm, LANES), lambda i: (i, 0))
    return pl.pallas_call(
        body, name="qk_prep_bwd", grid=(s // tm,),
        in_specs=[pl.BlockSpec((tm, 512), lambda i: (i, 0)), pl.BlockSpec((tm, 256), lambda i: (i, 2)),
                  pl.BlockSpec((tm, 512), lambda i: (i, 0)), tab, tab, full(gq), full(gk), tab, tab,
                  full(bs512), full(sw512)],
        out_specs=[pl.BlockSpec((tm, 768), lambda i: (i, 0)), pl.BlockSpec((1, 512), lambda i: (0, 0)),
                   pl.BlockSpec((1, LANES), lambda i: (0, 0))],
        out_shape=[jax.ShapeDtypeStruct((s, 768), CD), jax.ShapeDtypeStruct((1, 512), F32),
                   jax.ShapeDtypeStruct((1, LANES), F32)],
        compiler_params=_params(1),
    )(pqkv, pqkv, dq, dk, dv, gq, gk, cos2, sin2, bs512, sw512)


def _kv_rows(h):
    return pl.ds(pl.multiple_of((h // 4) * HEAD_DIM, HEAD_DIM), HEAD_DIM)


def _attn_fwd(qe, k, v, vs, tq):
    s = k.shape[0]

    def body(qa_ref, qb_ref, k_ref, v_ref, vs_ref, o_ref, lse_ref):
        m = pl.program_id(0)
        grp = m // 2
        kk = k_ref[...]
        outs = []
        for idx, q_ref in enumerate((qa_ref, qb_ref)):
            sc = _mm(q_ref[...], kk, NT)
            mx = jnp.max(sc, axis=-1, keepdims=True)
            e = jnp.exp(sc - mx)
            l = jnp.sum(e, axis=-1, keepdims=True)
            lse_ref[idx] = mx + jnp.log(l)
            vsel = jnp.where(grp != idx, vs_ref[...], v_ref[...])
            outs.append(_mm(e, vsel, NN) * (1.0 / l))
        lane = lax.broadcasted_iota(jnp.int32, (1, LANES), 1)
        o_ref[...] = jnp.where(lane < HEAD_DIM, outs[0], outs[1])

    kv = pl.BlockSpec((s, LANES), lambda m, i: (0, 0))
    return pl.pallas_call(
        body, name="attn_fwd", grid=(4, s // tq),
        in_specs=[pl.BlockSpec((tq, LANES), lambda m, i: (i, 2 * m)), pl.BlockSpec((tq, LANES), lambda m, i: (i, 2 * m + 1)),
                  kv, kv, kv],
        out_specs=[pl.BlockSpec((tq, LANES), lambda m, i: (i, m)), pl.BlockSpec((2, tq, 1), lambda m, i: (m, i, 0))],
        out_shape=[jax.ShapeDtypeStruct((s, 512), F32), jax.ShapeDtypeStruct((8, s, 1), F32)],
        compiler_params=_params(2),
    )(qe, qe, k, v, vs)


def _attn_bwd(qe, k, kt, v, doe, delta, lse, tq):
    s = k.shape[0]

    def body(qa_ref, qb_ref, k_ref, kt_ref, v_ref, doa_ref, dob_ref, dla_ref, dlb_ref, lse_ref,
             dqt_ref, dkt_ref, dvt_ref, qt, dot):
        @pl.when((pl.program_id(0) == 0) & (pl.program_id(1) == 0))
        def _():
            dkt_ref[...] = jnp.zeros_like(dkt_ref)
            dvt_ref[...] = jnp.zeros_like(dvt_ref)

        rows = _kv_rows(2 * pl.program_id(0))
        kt = kt_ref[rows, :]
        dkt = jnp.zeros((HEAD_DIM, s), F32)
        dvt = jnp.zeros((HEAD_DIM, s), F32)
        for idx, (q_ref, do_ref, dl_ref) in enumerate(((qa_ref, doa_ref, dla_ref), (qb_ref, dob_ref, dlb_ref))):
            q, do = q_ref[...], do_ref[...]
            p = jnp.exp(_mm(q, k_ref[...], NT) - lse_ref[idx])
            dp = _mm(do, v_ref[...], NT)
            ds = p * (dp - jnp.max(dl_ref[...], axis=-1, keepdims=True))
            dqt_ref[idx * HEAD_DIM:(idx + 1) * HEAD_DIM, :] = _mm(kt, ds, NT)
            qt[idx] = jnp.transpose(q.astype(F32))
            dot[idx] = jnp.transpose(do.astype(F32))
            dkt = dkt + _mm(qt[idx, rows, :], ds, NN)
            dvt = dvt + _mm(dot[idx, rows, :], p, NN)
        dkt_ref[rows, :] += dkt
        dvt_ref[rows, :] += dvt

    kv = pl.BlockSpec((s, LANES), lambda m, i: (0, 0))
    kvt = pl.BlockSpec((LANES, s), lambda m, i: (0, 0))
    blk_a = pl.BlockSpec((tq, LANES), lambda m, i: (i, 2 * m))
    blk_b = pl.BlockSpec((tq, LANES), lambda m, i: (i, 2 * m + 1))
    return pl.pallas_call(
        body, name="attn_bwd", grid=(4, s // tq),
        in_specs=[blk_a, blk_b, kv, kvt, kv, blk_a, blk_b, blk_a, blk_b, pl.BlockSpec((2, tq, 1), lambda m, i: (m, i, 0))],
        out_specs=[pl.BlockSpec((LANES, tq), lambda m, i: (m, i)), kvt, kvt],
        out_shape=[jax.ShapeDtypeStruct((8 * HEAD_DIM, s), F32), jax.ShapeDtypeStruct((LANES, s), F32),
                   jax.ShapeDtypeStruct((LANES, s), F32)],
        scratch_shapes=[pltpu.VMEM((2, LANES, tq), F32), pltpu.VMEM((2, LANES, tq), F32)],
        compiler_params=_params(2),
    )(qe, qe, k, kt, v, doe, doe, delta, delta, lse)


@jax.custom_vjp
def _mm_nn(a, b):
    return _mm(a, b, NN)


_mm_nn.defvjp(lambda a, b: (_mm(a, b, NN), (a, b)),
              lambda res, g: (_mm(g, res[1], NT), _mm(res[0], g, TN)))


@jax.custom_vjp
def _mm_nt(a, b):
    return _mm(a, b, NT)


_mm_nt.defvjp(lambda a, b: (_mm(a, b, NT), (a, b)),
              lambda res, g: (_mm(g, res[1], NN), _mm(g, res[0], TN)))


@jax.custom_vjp
def _mm_tn(a, b):
    return _mm(a, b, TN)


_mm_tn.defvjp(lambda a, b: (_mm(a, b, TN), (a, b)),
              lambda res, g: (_mm(res[1], g, NT), _mm(res[0], g, NN)))


@jax.custom_vjp
def _cmm(m, mt, x):
    return _xdot_l(m, x)


_cmm.defvjp(lambda m, mt, x: (_xdot_l(m, x), (m, mt)),
            lambda res, g: (jnp.zeros_like(res[0]), jnp.zeros_like(res[1]), _xdot_l(res[1], g)))


def _hgrn_masks(t, rev):
    n_ch = t // CHUNK
    r = jnp.bitwise_and(lax.broadcasted_iota(jnp.int32, (2 * t, t), 0), t - 1)
    c = lax.broadcasted_iota(jnp.int32, (2 * t, t), 1)
    same = jnp.right_shift(r, 5) == jnp.right_shift(c, 5)
    tri2 = same & ((c >= r) if rev else (c <= r))
    pr = lax.broadcasted_iota(jnp.int32, (LANES, LANES), 0)
    pc = lax.broadcasted_iota(jnp.int32, (LANES, LANES), 1)
    diag = jnp.right_shift(pr, 6) == jnp.right_shift(pc, 6)
    qr = lax.broadcasted_iota(jnp.int32, (t, n_ch * LANES), 0)
    qc = lax.broadcasted_iota(jnp.int32, (t, n_ch * LANES), 1)
    rows_chunk = jnp.right_shift(qc, 7) == jnp.right_shift(qr, 5)
    vr = lax.broadcasted_iota(jnp.int32, (n_ch * LANES, t), 0)
    vc = lax.broadcasted_iota(jnp.int32, (n_ch * LANES, t), 1)
    cols_chunk = jnp.right_shift(vr, 7) == jnp.right_shift(vc, 5)
    return dict(tri2=tri2, diag=diag, rows_chunk=rows_chunk, cols_chunk=cols_chunk)


def _hgrn_gates(xf, lb):
    f = lb + (1.0 - lb) * _sigmoid(xf)
    return 1.0 - f, jnp.log(f)


def _hgrn_dir(xq, xf, v, lb, state, cm, cmt, mk, rev):
    t = xq.shape[0]
    n_ch = t // CHUNK
    lo = lax.broadcasted_iota(jnp.int32, (1, LANES), 1) < HEAD_DIM
    q = xq * _sigmoid(xq)
    k, lf = _hgrn_gates(xf, lb)
    cs = _cmm(cm, cmt, lf)
    b, bm, bl = cs[:t], cs[t:2 * t], cs[2 * t:]
    qd = q * jnp.exp(b - bm)
    kd = k * jnp.exp(bm - b)
    kc = k * jnp.exp(bl - b)
    qe = q * jnp.exp(b)
    qd2 = jnp.concatenate([jnp.where(lo, qd, 0.0), jnp.where(lo, 0.0, qd)], axis=0)
    o2 = _mm_nn(jnp.where(mk["tri2"], _mm_nt(qd2, kd), 0.0), v)
    o = jnp.where(lo, o2[:t], o2[t:])
    vexp = jnp.where(mk["cols_chunk"], jnp.concatenate([jnp.transpose(v)] * n_ch, axis=0), 0.0)
    adds = _mm_nn(vexp, kc)
    dec = jnp.exp(bl)
    entering = [None] * n_ch
    for c in (range(n_ch - 1, -1, -1) if rev else range(n_ch)):
        entering[c] = state
        d = jnp.concatenate([dec[c * CHUNK:(c + 1) * CHUNK]] * (LANES // CHUNK), axis=0)
        state = d * state + jnp.where(mk["diag"], adds[c * LANES:(c + 1) * LANES], 0.0)
    qexp = jnp.where(mk["rows_chunk"], jnp.concatenate([qe] * n_ch, axis=1), 0.0)
    return o + _mm_nt(qexp, jnp.concatenate(entering, axis=1)), state


def _hgrn_lower_bounds(l):
    out = []
    for d in (0, 1):
        l0, l1 = l[2 * d:2 * d + 1, :], l[2 * d + 1:2 * d + 2, :]
        mx = jnp.maximum(l0, l1)
        e0, e1 = jnp.exp(l0 - mx), jnp.exp(l1 - mx)
        out.append(e0 / (e0 + e1))
    return out


def _hgrn_consts(t):
    cf, cb = _np_hgrn_cums(t, False), _np_hgrn_cums(t, True)
    return (_bf(cf), _bf(cf.T), _bf(cb), _bf(cb.T), _bf(_np_blocksum(LANES)))


def _hgrn_fwd(ph, lbl, ng):
    s = ph.shape[0]
    t = min(HG_TILE, s)
    nt = s // t
    consts = _hgrn_consts(t)

    def body(xq_ref, xff_ref, xfb_ref, xi_ref, xg_ref, lbl_ref, ng_ref, cf_ref, cft_ref, cb_ref, cbt_ref, bs_ref,
             o_ref, pre_ref, st_ref, acc):
        lbf, lbb = _hgrn_lower_bounds(lbl_ref)
        mk_f, mk_b = _hgrn_masks(t, False), _hgrn_masks(t, True)
        zero = jnp.zeros((LANES, LANES), F32)

        def rows_of(i):
            return pl.ds(pl.multiple_of(i * t, t), t)

        acc[...] = jnp.zeros_like(acc)

        def step(i, states):
            tb = nt - 1 - i
            rf, rb = rows_of(i), rows_of(tb)
            st_ref[0, 0, i] = states[0]
            st_ref[0, 1, tb] = states[1]
            of, sf = _hgrn_dir(xq_ref[rf, :], xff_ref[rf, :], xi_ref[rf, :], lbf, states[0],
                               cf_ref[...], cft_ref[...], mk_f, False)
            ob, sb = _hgrn_dir(xq_ref[rb, :], xfb_ref[rb, :], xi_ref[rb, :], lbb, states[1],
                               cb_ref[...], cbt_ref[...], mk_b, True)
            acc[rf, :] += of
            acc[rb, :] += ob
            return sf, sb

        lax.fori_loop(0, nt, step, (zero, zero))

        def step_n(i, carry):
            rows = rows_of(i)
            o = acc[rows, :]
            ss = _xdot(o * o, bs_ref[...])
            r = lax.rsqrt(ss * (1.0 / HEAD_DIM) + EPS)
            xg = xg_ref[rows, :]
            pre_ref[rows, :] = o
            o_ref[rows, :] = (o * r * ng_ref[...]) * (xg * _sigmoid(xg))
            return carry

        lax.fori_loop(0, nt, step_n, 0)

    col = lambda off: pl.BlockSpec((s, LANES), lambda m: (0, off + m))
    full = lambda a: pl.BlockSpec(a.shape, lambda m: (0,) * a.ndim)
    return pl.pallas_call(
        body, name="hgrn_fwd", grid=(4,),
        in_specs=[col(0), col(4), col(8), col(12), col(16), pl.BlockSpec((4, LANES), lambda m: (0, m)),
                  pl.BlockSpec((1, LANES), lambda m: (0, m))] + [full(c) for c in consts],
        out_specs=[col(0), col(0), pl.BlockSpec((1, 2, nt, LANES, LANES), lambda m: (m, 0, 0, 0, 0))],
        out_shape=[jax.ShapeDtypeStruct((s, 512), F32), jax.ShapeDtypeStruct((s, 512), F32),
                   jax.ShapeDtypeStruct((4, 2, nt, LANES, LANES), F32)],
        scratch_shapes=[pltpu.VMEM((s, LANES), F32)],
        compiler_params=_params(1),
    )(ph, ph, ph, ph, ph, lbl, ng, *consts)


def _hgrn_bwd(ph, pre, dout, states, lbl, ng):
    s = ph.shape[0]
    t = min(HG_TILE, s)
    nt = s // t
    consts = _hgrn_consts(t)

    def body(xq_ref, xff_ref, xfb_ref, xi_ref, xg_ref, pre_ref, do_ref, st_ref, lbl_ref, ng_ref,
             cf_ref, cft_ref, cb_ref, cbt_ref, bs_ref,
             dq_ref, dff_ref, dfb_ref, di_ref, dg_ref, dlb_ref, dng_ref, dpre, dq_acc, dv_acc):
        lbf, lbb = _hgrn_lower_bounds(lbl_ref)
        mk_f, mk_b = _hgrn_masks(t, False), _hgrn_masks(t, True)
        zero = jnp.zeros((LANES, LANES), F32)
        zrow = jnp.zeros((1, LANES), F32)

        def rows_of(i):
            return pl.ds(pl.multiple_of(i * t, t), t)

        def step_n(i, dng):
            rows = rows_of(i)
            o, xg, do = pre_ref[rows, :], xg_ref[rows, :], do_ref[rows, :]
            bs = bs_ref[...]
            r = lax.rsqrt(_xdot(o * o, bs) * (1.0 / HEAD_DIM) + EPS)
            sg = _sigmoid(xg)
            gate = xg * sg
            don = do * gate
            dg_ref[rows, :] = (do * (o * r * ng_ref[...]) * (sg * (1.0 + xg * (1.0 - sg)))).astype(CD)
            u = don * ng_ref[...]
            dpre[rows, :] = r * u - o * (r * r * r) * (_xdot(u * o, bs) * (1.0 / HEAD_DIM))
            return dng + jnp.sum(don * o * r, axis=0, keepdims=True)

        dng_ref[...] = lax.fori_loop(0, nt, step_n, zrow)

        dq_acc[...] = jnp.zeros_like(dq_acc)
        dv_acc[...] = jnp.zeros_like(dv_acc)

        def grad_tile(ti, xf_ref, df_ref, lb, cm, cmt, mk, rev, st, dstate):
            rows = rows_of(ti)
            fn = lambda xq, xf, v, lbv, s_in: _hgrn_dir(xq, xf, v, lbv, s_in, cm, cmt, mk, rev)
            _, vjp = jax.vjp(fn, xq_ref[rows, :], xf_ref[rows, :], xi_ref[rows, :], lb, st)
            dxq, dxf, dv, dlb_t, dstate = vjp((dpre[rows, :], dstate))
            df_ref[rows, :] = dxf.astype(CD)
            dq_acc[rows, :] += dxq
            dv_acc[rows, :] += dv
            return dstate, dlb_t

        def step_g(i, carry):
            dsf, dsb, dlbf, dlbb = carry
            tf, tb = nt - 1 - i, i
            dsf, gf = grad_tile(tf, xff_ref, dff_ref, lbf, cf_ref[...], cft_ref[...], mk_f, False, st_ref[0, 0, tf], dsf)
            dsb, gb = grad_tile(tb, xfb_ref, dfb_ref, lbb, cb_ref[...], cbt_ref[...], mk_b, True, st_ref[0, 1, tb], dsb)
            return dsf, dsb, dlbf + gf, dlbb + gb

        _, _, dlbf, dlbb = lax.fori_loop(0, nt, step_g, (zero, zero, zrow, zrow))
        dlb_ref[0:1, :] = dlbf
        dlb_ref[1:2, :] = dlbb
        dq_ref[...] = dq_acc[...].astype(CD)
        di_ref[...] = dv_acc[...].astype(CD)

    col = lambda off: pl.BlockSpec((s, LANES), lambda m: (0, off + m))
    full = lambda a: pl.BlockSpec(a.shape, lambda m: (0,) * a.ndim)
    stream = jax.ShapeDtypeStruct((s, 512), CD)
    return pl.pallas_call(
        body, name="hgrn_bwd", grid=(4,),
        in_specs=[col(0), col(4), col(8), col(12), col(16), col(0), col(0),
                  pl.BlockSpec((1, 2, nt, LANES, LANES), lambda m: (m, 0, 0, 0, 0)),
                  pl.BlockSpec((4, LANES), lambda m: (0, m)),
                  pl.BlockSpec((1, LANES), lambda m: (0, m))] + [full(c) for c in consts],
        out_specs=[col(0)] * 5 + [pl.BlockSpec((2, LANES), lambda m: (0, m)), pl.BlockSpec((1, LANES), lambda m: (0, m))],
        out_shape=[stream] * 5 + [jax.ShapeDtypeStruct((2, 512), F32), jax.ShapeDtypeStruct((1, 512), F32)],
        scratch_shapes=[pltpu.VMEM((s, LANES), F32), pltpu.VMEM((s, LANES), F32), pltpu.VMEM((s, LANES), F32)],
        compiler_params=_params(1),
    )(ph, ph, ph, ph, ph, pre, dout, states, lbl, ng, *consts)


def _branch_out(o, w4):
    return jnp.concatenate([_mm(o, w4[j], NN) for j in range(N_SHARD)], axis=1)


def _mix_out_fwd(x, oa, ob, pg, wa, wb, wo, tm):
    s, d = x.shape

    def body(x_ref, oa_ref, ob_ref, ga_ref, gb_ref, wa_ref, wb_ref, wo_ref, xo_ref):
        ya = _branch_out(oa_ref[...], wa_ref)
        yb = _branch_out(ob_ref[...], wb_ref)
        merged = _sigmoid(ga_ref[...]) * ya + _sigmoid(gb_ref[...]) * yb
        xo_ref[...] = x_ref[...] + _mm(merged, wo_ref[...], NN)

    row = pl.BlockSpec((tm, d), lambda i: (i, 0))
    half = pl.BlockSpec((tm, 512), lambda i: (i, 0))
    full = lambda a: pl.BlockSpec(a.shape, lambda i: (0,) * a.ndim)
    return pl.pallas_call(
        body, name="mix_out_fwd", grid=(s // tm,),
        in_specs=[row, half, half, row, pl.BlockSpec((tm, d), lambda i: (i, 1)), full(wa), full(wb), full(wo)],
        out_specs=row, out_shape=jax.ShapeDtypeStruct((s, d), F32),
        compiler_params=_params(1),
    )(x, oa, ob, pg, pg, wa, wb, wo)


def _mix_out_bwd(dx, oa, ob, pg, wa, wb, wo, tm, after=()):
    s, d = dx.shape
    bs512, eq, ebc = _bf(_np_blocksum(512)), _bf(_np_expand_q()), _bf(_np_bcast_head())

    def body(*refs):
        (dx_ref, oa_ref, ob_ref, ga_ref, gb_ref, wa_ref, wb_ref, wo_ref, bs_ref, eq_ref, ebc_ref,
         dpg_ref, mg_ref, dya_ref, dyb_ref, doe_ref, dl_ref, dob_ref) = refs[len(after):]
        oa = oa_ref[...]
        ya = _branch_out(oa, wa_ref)
        yb = _branch_out(ob_ref[...], wb_ref)
        sa, sb = _sigmoid(ga_ref[...]), _sigmoid(gb_ref[...])
        mg_ref[...] = (sa * ya + sb * yb).astype(CD)
        dm = _mm(dx_ref[...], wo_ref[...], NT)
        dpg_ref[...] = jnp.concatenate([dm * ya * sa * (1.0 - sa), dm * yb * sb * (1.0 - sb)], axis=1).astype(CD)
        dya, dyb = dm * sa, dm * sb
        dya_ref[...] = dya.astype(CD)
        dyb_ref[...] = dyb.astype(CD)
        doa = jnp.zeros(oa.shape, F32)
        dob = jnp.zeros(oa.shape, F32)
        for j in range(N_SHARD):
            doa = doa + _mm(dya[:, 256 * j:256 * j + 256], wa_ref[j], NT)
            dob = dob + _mm(dyb[:, 256 * j:256 * j + 256], wb_ref[j], NT)
        dob_ref[...] = dob
        doe_ref[...] = _mm(doa, eq_ref[...], NN).astype(CD)
        dl_ref[...] = _xdot(_xdot(doa * oa, bs_ref[...]), ebc_ref[...])

    row = pl.BlockSpec((tm, d), lambda i: (i, 0))
    half = pl.BlockSpec((tm, 512), lambda i: (i, 0))
    full = lambda a: pl.BlockSpec(a.shape, lambda i: (0,) * a.ndim)
    wide = jax.ShapeDtypeStruct((s, d), CD)
    return pl.pallas_call(
        body, name="mix_out_bwd", grid=(s // tm,),
        in_specs=[ANY] * len(after) + [row, half, half, row, pl.BlockSpec((tm, d), lambda i: (i, 1)), full(wa), full(wb),
                                       full(wo), full(bs512), full(eq), full(ebc)],
        out_specs=[pl.BlockSpec((tm, 2048), lambda i: (i, 0)), row, row, row, row, row, half],
        out_shape=[jax.ShapeDtypeStruct((s, 2048), CD), wide, wide, wide, wide, jax.ShapeDtypeStruct((s, d), F32),
                   jax.ShapeDtypeStruct((s, 512), F32)],
        compiler_params=_params(1),
    )(*after, dx, oa, ob, pg, pg, wa, wb, wo, bs512, eq, ebc)


def _loss_head(x, g, target, tm):
    s, d = x.shape

    def body(x_ref, g_ref, t_ref, dx_ref, loss_ref, dg_ref):
        @pl.when(pl.program_id(0) == 0)
        def _():
            loss_ref[...] = jnp.zeros_like(loss_ref)
            dg_ref[...] = jnp.zeros_like(dg_ref)

        xv = x_ref[...]
        r = lax.rsqrt(jnp.mean(xv * xv, axis=-1, keepdims=True) + EPS)
        err = xv * r * g_ref[...] - t_ref[...]
        loss_ref[...] += 0.5 * jnp.sum(jnp.mean(err * err, axis=-1, keepdims=True))
        dy = err * (1.0 / d)
        u = dy * g_ref[...]
        dx_ref[...] = r * u - xv * (r * r * r) * jnp.mean(u * xv, axis=-1, keepdims=True)
        dg_ref[...] += jnp.sum(dy * xv * r, axis=0, keepdims=True)

    row = pl.BlockSpec((tm, d), lambda i: (i, 0))
    vec = pl.BlockSpec((1, d), lambda i: (0, 0))
    return pl.pallas_call(
        body, name="loss_head", grid=(s // tm,),
        in_specs=[row, vec, row], out_specs=[row, pl.BlockSpec((8, LANES), lambda i: (0, 0)), vec],
        out_shape=[jax.ShapeDtypeStruct((s, d), F32), jax.ShapeDtypeStruct((8, LANES), F32),
                   jax.ShapeDtypeStruct((1, d), F32)],
        compiler_params=_params(1),
    )(x, g, target)


def _position():
    x, y, c = lax.axis_index("x"), lax.axis_index("y"), lax.axis_index("c")
    return x, y, c, [(1 - x, y), (x, 1 - y), (1 - x, 1 - y)]


def _row_tile(rows, cap=256):
    best = rows
    for cand in range(8, min(rows, cap) + 1, 8):
        if rows % cand == 0:
            best = cand
    return best


def _cast_into_slot(shard, me_idx, dtype):
    rows, cols = shard.shape
    tr = _row_tile(rows)

    def body(me_ref, src_ref, out_ref):
        out_ref[0] = src_ref[...].astype(dtype)

    return pl.pallas_call(
        body, name="cast_into_slot",
        grid_spec=pltpu.PrefetchScalarGridSpec(
            num_scalar_prefetch=1, grid=(rows // tr,),
            in_specs=[pl.BlockSpec((tr, cols), lambda i, me: (i, 0))],
            out_specs=pl.BlockSpec((1, tr, cols), lambda i, me: (me[0], i, 0))),
        out_shape=jax.ShapeDtypeStruct((N_SHARD, rows, cols), dtype),
        compiler_params=_params(1),
    )(me_idx, shard)


HBM_SPEC = pl.BlockSpec(memory_space=pltpu.HBM)
SEM_SPEC = pl.BlockSpec(memory_space=pltpu.SEMAPHORE)
DATAFLOW = pltpu.SideEffectType.DATAFLOW_SIDE_EFFECTING


def _exchange_copies(srcs, lands, send, recv, gather):
    x, y, c, chips = _position()
    me = 2 * x + y
    out = []
    for a in range(len(lands)):
        dst = lands[a].at[me]
        if gather and _halved(lands[a]):
            half = lands[a].shape[1] // 2
            dst = lands[a].at[me, pl.ds(c * half, half), :]
        for k, (px, py) in enumerate(chips):
            src = dst if gather else srcs[a].at[2 * px + py]
            out.append(pltpu.make_async_remote_copy(src_ref=src, dst_ref=dst, send_sem=send.at[3 * a + k],
                                                    recv_sem=recv.at[3 * a + k], device_id=(px, py, c), device_id_type=MESH))
    return out


def _halved(land):
    return land.shape[1] % 32 == 0


def _pair_fill(name, lands):
    n = len(lands)

    def body(*refs):
        src, dst = refs[:n], refs[n:2 * n]
        send, recv = refs[2 * n:]
        x, y, c, chips = _position()
        copies = []
        for a in range(n):
            half = src[a].shape[1] // 2
            for k, (px, py) in enumerate(chips):
                rows = (2 * px + py, pl.ds(c * half, half), slice(None))
                cp = pltpu.make_async_remote_copy(src_ref=src[a].at[rows], dst_ref=dst[a].at[rows], send_sem=send.at[a, k],
                                                  recv_sem=recv.at[a, k], device_id=(x, y, 1 - c), device_id_type=MESH)
                cp.start()
                copies.append(cp)
        for cp in copies:
            cp.wait()

    return pl.pallas_call(
        body, name=name, in_specs=[ANY] * n, out_specs=[ANY] * n,
        out_shape=[jax.ShapeDtypeStruct(l.shape, l.dtype) for l in lands],
        input_output_aliases={a: a for a in range(n)},
        scratch_shapes=[pltpu.SemaphoreType.DMA((n, 3)), pltpu.SemaphoreType.DMA((n, 3))],
    )(*lands)


def _exchange_start(name, srcs, lands, after):
    ns, nl, na = len(srcs), len(lands), len(after)
    gather = ns == 0

    def body(*refs):
        src_refs, land_refs = refs[:ns], refs[ns:ns + nl]
        send, recv = refs[ns + nl + na], refs[ns + nl + na + 1]
        token = refs[-1]
        for cp in _exchange_copies(src_refs, land_refs, send, recv, gather):
            cp.start()
        token[...] = jnp.zeros_like(token)

    arrays = [pltpu.with_memory_space_constraint(a, pltpu.HBM) for a in list(srcs) + list(lands)]
    outs = pl.pallas_call(
        body, name=name,
        out_shape=(pltpu.SemaphoreType.DMA((3 * nl,)), pltpu.SemaphoreType.DMA((3 * nl,)),
                   *[pltpu.HBM(a.shape, a.dtype) for a in arrays], jax.ShapeDtypeStruct((8, LANES), F32)),
        in_specs=[HBM_SPEC] * (ns + nl) + [ANY] * na,
        out_specs=(SEM_SPEC, SEM_SPEC, *[HBM_SPEC] * (ns + nl), pl.BlockSpec(memory_space=pltpu.VMEM)),
        input_output_aliases={i: 2 + i for i in range(ns + nl)},
        compiler_params=pltpu.CompilerParams(has_side_effects=DATAFLOW),
    )(*arrays, *after)
    return outs[0], outs[1], list(outs[2:2 + ns]), list(outs[2 + ns:2 + ns + nl]), outs[-1]


def _exchange_wait(name, send, recv, srcs, lands, after):
    ns, nl, na = len(srcs), len(lands), len(after)
    gather = ns == 0

    def body(*refs):
        src_refs, land_refs = refs[:ns], refs[ns:ns + nl]
        send_ref, recv_ref = refs[ns + nl], refs[ns + nl + 1]
        for cp in _exchange_copies(src_refs, land_refs, send_ref, recv_ref, gather):
            cp.wait_send()
            cp.wait_recv()

    outs = pl.pallas_call(
        body, name=name,
        out_shape=tuple(pltpu.HBM(a.shape, a.dtype) for a in list(srcs) + list(lands)),
        in_specs=[HBM_SPEC] * (ns + nl) + [SEM_SPEC, SEM_SPEC] + [ANY] * na,
        out_specs=tuple([HBM_SPEC] * (ns + nl)),
        input_output_aliases={i: i for i in range(ns + nl)},
        compiler_params=pltpu.CompilerParams(has_side_effects=DATAFLOW),
    )(*srcs, *lands, send, recv, *after)
    return list(outs[ns:])


def _pair_exchange(grads):
    n = len(grads)

    def body(*refs):
        src, dst = refs[:n], refs[n:2 * n]
        send, recv = refs[2 * n:]
        x, y, c, _ = _position()
        copies = []
        for a in range(n):
            half = src[a].shape[1] // 2
            cp = pltpu.make_async_remote_copy(
                src_ref=src[a].at[:, pl.ds((1 - c) * half, half), :], dst_ref=dst[a], send_sem=send.at[a],
                recv_sem=recv.at[a], device_id=(x, y, 1 - c), device_id_type=MESH)
            cp.start()
            copies.append(cp)
        for cp in copies:
            cp.wait()

    return pl.pallas_call(
        body, name="grad_pair_exchange", in_specs=[ANY] * n, out_specs=[ANY] * n,
        out_shape=[jax.ShapeDtypeStruct((g.shape[0], g.shape[1] // 2, g.shape[2]), g.dtype) for g in grads],
        scratch_shapes=[pltpu.SemaphoreType.DMA((n,)), pltpu.SemaphoreType.DMA((n,))],
    )(*grads)


def _pair_sum(g, got, c_idx, me_idx):
    nsh, rows, cols = g.shape
    half = rows // 2

    def body(c_ref, me_ref, g_ref, got_ref, s_ref, own_ref):
        sm = g_ref[...] + got_ref[...].astype(F32)
        s_ref[...] = sm.astype(CD)

        @pl.when(pl.program_id(0) == me_ref[0])
        def _():
            own_ref[...] = sm[0]

    return pl.pallas_call(
        body, name="grad_pair_sum",
        grid_spec=pltpu.PrefetchScalarGridSpec(
            num_scalar_prefetch=2, grid=(nsh,),
            in_specs=[pl.BlockSpec((1, half, cols), lambda j, c, me: (j, c[0], 0)),
                      pl.BlockSpec((1, half, cols), lambda j, c, me: (j, 0, 0))],
            out_specs=[pl.BlockSpec((1, half, cols), lambda j, c, me: (j, 0, 0)),
                       pl.BlockSpec((half, cols), lambda j, c, me: (0, 0))]),
        out_shape=[jax.ShapeDtypeStruct((nsh, half, cols), CD), jax.ShapeDtypeStruct((half, cols), F32)],
        compiler_params=_params(1),
    )(c_idx, me_idx, g, got)


def _chip_sum(own, got, me_idx):
    nsh, half, cols = got.shape

    def body(me_ref, own_ref, got_ref, out_ref):
        j = pl.program_id(0)
        term = jnp.where(j == me_ref[0], own_ref[...], got_ref[0].astype(F32))

        @pl.when(j == 0)
        def _():
            out_ref[...] = term

        @pl.when(j > 0)
        def _():
            out_ref[...] += term

    return pl.pallas_call(
        body, name="grad_chip_sum",
        grid_spec=pltpu.PrefetchScalarGridSpec(
            num_scalar_prefetch=1, grid=(nsh,),
            in_specs=[pl.BlockSpec((half, cols), lambda j, me: (0, 0)), pl.BlockSpec((1, half, cols), lambda j, me: (j, 0, 0))],
            out_specs=pl.BlockSpec((half, cols), lambda j, me: (0, 0))),
        out_shape=jax.ShapeDtypeStruct((half, cols), F32),
        compiler_params=_params(1),
    )(me_idx, own, got)


def _pair_share(halves):
    n = len(halves)

    def body(*refs):
        src, dst = refs[:n], refs[n:2 * n]
        send, recv = refs[2 * n:]
        x, y, c, _ = _position()
        copies = []
        for a in range(n):
            cp = pltpu.make_async_remote_copy(src_ref=src[a], dst_ref=dst[a], send_sem=send.at[a],
                                              recv_sem=recv.at[a], device_id=(x, y, 1 - c), device_id_type=MESH)
            cp.start()
            copies.append(cp)
        for cp in copies:
            cp.wait()

    return pl.pallas_call(
        body, name="grad_pair_share", in_specs=[ANY] * n, out_specs=[ANY] * n,
        out_shape=[jax.ShapeDtypeStruct(h.shape, h.dtype) for h in halves],
        scratch_shapes=[pltpu.SemaphoreType.DMA((n,)), pltpu.SemaphoreType.DMA((n,))],
    )(*halves)


def _small_allreduce(buf):
    rows, cols = buf.shape

    def body(src_ref, out_ref, slots, send, recv):
        x, y, c, _ = _position()
        me = 4 * x + 2 * y + c
        slots[me] = src_ref[...]
        copies = []
        k = 0
        for dx in (0, 1):
            for dy in (0, 1):
                for dc in (0, 1):
                    if (dx, dy, dc) == (0, 0, 0):
                        continue
                    peer = (jnp.where(dx, 1 - x, x), jnp.where(dy, 1 - y, y), jnp.where(dc, 1 - c, c))
                    cp = pltpu.make_async_remote_copy(src_ref=src_ref, dst_ref=slots.at[me], send_sem=send.at[k],
                                                      recv_sem=recv.at[k], device_id=peer, device_id_type=MESH)
                    cp.start()
                    copies.append(cp)
                    k += 1
        for cp in copies:
            cp.wait()
        total = slots[0]
        for dev in range(1, N_DEV):
            total = total + slots[dev]
        out_ref[...] = total

    vm = pl.BlockSpec(memory_space=pltpu.VMEM)
    return pl.pallas_call(
        body, name="small_allreduce", in_specs=[vm], out_specs=vm,
        out_shape=jax.ShapeDtypeStruct((rows, cols), F32),
        scratch_shapes=[pltpu.VMEM((N_DEV, rows, cols), F32), pltpu.SemaphoreType.DMA((N_DEV - 1,)),
                        pltpu.SemaphoreType.DMA((N_DEV - 1,))],
    )(buf)


def _adamw_math(w, gv, m, v):
    mn = ADAM_B1 * m + (1.0 - ADAM_B1) * gv
    vn = ADAM_B2 * v + (1.0 - ADAM_B2) * (gv * gv)
    m_hat = mn / (1.0 - ADAM_B1 ** ADAM_STEP)
    v_hat = vn / (1.0 - ADAM_B2 ** ADAM_STEP)
    return -ADAM_LR * (m_hat / (jnp.sqrt(v_hat) + ADAM_EPS) + ADAM_WD * w), mn, vn


def _adamw(w, g, m, v):
    rows, cols = w.shape
    tr = _row_tile(rows)

    def body(w_ref, g_ref, m_ref, v_ref, d_ref, mo_ref, vo_ref):
        d_ref[...], mo_ref[...], vo_ref[...] = _adamw_math(w_ref[...], g_ref[...], m_ref[...], v_ref[...])

    blk = pl.BlockSpec((tr, cols), lambda i: (i, 0))
    shp = jax.ShapeDtypeStruct((rows, cols), F32)
    return pl.pallas_call(
        body, name="adamw", grid=(rows // tr,), in_specs=[blk] * 4, out_specs=[blk] * 3, out_shape=[shp] * 3,
        compiler_params=_params(1),
    )(w, g, m, v)


def _adamw_halves(w, own, got, m, v, c_idx):
    rows, cols = w.shape
    tr = _row_tile(rows // 2)
    per_half = rows // 2 // tr

    def body(c_ref, w_ref, own_ref, got_ref, m_ref, v_ref, d_ref, mo_ref, vo_ref, g_ref):
        mine = (pl.program_id(0) // per_half) == c_ref[0]
        gv = jnp.where(mine, own_ref[...], got_ref[...])
        g_ref[...] = gv
        d_ref[...], mo_ref[...], vo_ref[...] = _adamw_math(w_ref[...], gv, m_ref[...], v_ref[...])

    blk = pl.BlockSpec((tr, cols), lambda i, c: (i, 0))
    own_blk = pl.BlockSpec((tr, cols), lambda i, c: (jnp.where(i // per_half == c[0], i % per_half, 0), 0))
    got_blk = pl.BlockSpec((tr, cols), lambda i, c: (jnp.where(i // per_half == c[0], 0, i % per_half), 0))
    shp = jax.ShapeDtypeStruct((rows, cols), F32)
    return pl.pallas_call(
        body, name="adamw_halves",
        grid_spec=pltpu.PrefetchScalarGridSpec(num_scalar_prefetch=1, grid=(rows // tr,),
                                               in_specs=[blk, own_blk, got_blk, blk, blk], out_specs=[blk] * 4),
        out_shape=[shp] * 4, compiler_params=_params(1),
    )(c_idx, w, own, got, m, v)


def _local_step(x, target, norm_gains, q_g, k_g, ng, weights_of, grads_done):
    s = x.shape[0]
    tm = min(512, s)
    tq = min(256, s)
    tf = min(1024, s)
    g1, gm, g2, gf = norm_gains
    cos2, sin2 = _rope_tables(s)
    gq8 = jnp.tile(q_g, (1, 8))
    gk2 = jnp.tile(k_g, (1, 2))

    tn = min(256, s)
    tk = min(1024, s)
    w1 = weights_of(1, ())
    x1, s1, t1, b1, h1 = _ffn_fwd(x, g1, w1["g1"], w1["u1"], w1["d1"], tf)
    w2 = weights_of(2, (x1,))
    lbl = w2["lbl"]
    pqkv, ph, pg, hm = _mix_in_fwd(x1, gm, w2["in"], tn)
    qe, kr, vr, vs = _qk_prep(pqkv, gq8, gk2, cos2, sin2, tm)
    oa, lse = _attn_fwd(qe, kr, vr, vs, tq)
    ob, pre, hstates = _hgrn_fwd(ph, lbl, ng)
    x2 = _mix_out_fwd(x1, oa, ob, pg, w2["a"], w2["b"], w2["o"], tm)
    w3 = weights_of(3, (x2,))
    x3, s2, t2, b2, h2 = _ffn_fwd(x2, g2, w3["g2"], w3["u2"], w3["d2"], tf)
    dx3, loss, dgf = _loss_head(x3, gf, target, tm)

    dx2, da2, db2, f2, dg2, dx3c = _ffn_bwd(dx3, x2, g2, s2, t2, b2, w3["g2"], w3["u2"], w3["d2"], tm)
    tok = grads_done(3, dict(g2=_dw_shared_b("dw_gate", da2, h2, tk, 1.0), u2=_dw_shared_b("dw_gate", db2, h2, tk, 1.0),
                             d2=_dw_shared_b("dw_down", f2, dx3c, tk, 0.5)))

    dpg, mg, dya, dyb, doe, delta, dob = _mix_out_bwd(dx2, oa, ob, pg, w2["a"], w2["b"], w2["o"], tm, tok)
    g_o = [g.reshape(N_SHARD, D_MODEL // N_SHARD, D_MODEL) for g in _dw_colblocks("dw_out", mg, dx2, 1, tk)]
    g_a = _dw_colblocks("dw_branch", oa, dya, N_SHARD, tk)
    g_b = _dw_colblocks("dw_branch", ob, dyb, N_SHARD, tk)
    dqt, dkt, dvt = _attn_bwd(qe, kr, kr.T, vr, doe, delta, lse, tq)
    dqkv, dgq, dgk = _qk_prep_bwd(pqkv, dqt.T, dkt.T, dvt.T, gq8, gk2, cos2, sin2, tm)
    dhq, dhff, dhfb, dhi, dhg, dlb, dng = _hgrn_bwd(ph, pre, dob, hstates, lbl, ng)
    dps = (dqkv, dhq, dhff, dhfb, dhi, dhg, dpg)
    g_in = [g.reshape(N_SHARD, -1, D_MODEL) for g in _dw_in(dps, hm, min(2048, s))]
    tok = grads_done(2, {"in": g_in, "a": g_a, "b": g_b, "o": g_o})
    dx1, dgm = _mix_in_bwd(dps, w2["in"], x1, dx2, gm, tn, tok)

    dx0, da1, db1, f1, dg1, dx1c = _ffn_bwd(dx1, x, g1, s1, t1, b1, w1["g1"], w1["u1"], w1["d1"], tm)
    grads_done(1, dict(g1=_dw_shared_b("dw_gate", da1, h1, tk, 1.0), u1=_dw_shared_b("dw_gate", db1, h1, tk, 1.0),
                       d1=_dw_shared_b("dw_down", f1, dx1c, tk, 0.5)))
    small = dict(g1=dg1, gm=dgm, g2=dg2, gf=dgf, gq=dgq, gk=dgk, lb=dlb, ng=dng)
    return loss, dx0, small, lbl


GROUPS = {1: ("g1", "u1", "d1"), 2: ("in", "a", "b", "o"), 3: ("g2", "u2", "d2")}
BIG = GROUPS[1] + GROUPS[2] + GROUPS[3]
TRANSPOSED = ("g1", "u1", "in", "g2", "u2")


def _pack_rows(vectors, width):
    rows = []
    for vct in vectors:
        flat = vct.reshape(-1)
        pad = (-flat.shape[0]) % width
        rows.append(jnp.pad(flat, (0, pad)).reshape(-1, width))
    return jnp.concatenate(rows, axis=0)


def kernel(x, ffn1_norm_g, ffn1_w_gate, ffn1_w_up, ffn1_w_down, mix_norm_g, w_in, q_norm_g, k_norm_g, hgrn_lb_logits, hgrn_out_norm_g, w_branch_attn, w_branch_hgrn, w_out, ffn2_norm_g, ffn2_w_gate, ffn2_w_up, ffn2_w_down, final_norm_g, loss_target, m_ffn1_norm_g, m_ffn1_w_gate, m_ffn1_w_up, m_ffn1_w_down, m_mix_norm_g, m_w_in, m_q_norm_g, m_k_norm_g, m_hgrn_lb_logits, m_hgrn_out_norm_g, m_w_branch_attn, m_w_branch_hgrn, m_w_out, m_ffn2_norm_g, m_ffn2_w_gate, m_ffn2_w_up, m_ffn2_w_down, m_final_norm_g, v_ffn1_norm_g, v_ffn1_w_gate, v_ffn1_w_up, v_ffn1_w_down, v_mix_norm_g, v_w_in, v_q_norm_g, v_k_norm_g, v_hgrn_lb_logits, v_hgrn_out_norm_g, v_w_branch_attn, v_w_branch_hgrn, v_w_out, v_ffn2_norm_g, v_ffn2_w_gate, v_ffn2_w_up, v_ffn2_w_down, v_final_norm_g):
    xi, yi, ci = lax.axis_index("x"), lax.axis_index("y"), lax.axis_index("c")
    me = 2 * xi + yi
    c_idx = jnp.reshape(ci, (1,)).astype(jnp.int32)
    me_idx = jnp.reshape(me, (1,)).astype(jnp.int32)

    big_w = dict(g1=ffn1_w_gate[0], u1=ffn1_w_up[0], d1=ffn1_w_down[0], a=w_branch_attn[0], b=w_branch_hgrn[0],
                 o=w_out[0], g2=ffn2_w_gate[0], u2=ffn2_w_up[0], d2=ffn2_w_down[0])
    big_w["in"] = w_in[0]
    big_m = dict(g1=m_ffn1_w_gate[0], u1=m_ffn1_w_up[0], d1=m_ffn1_w_down[0], a=m_w_branch_attn[0], b=m_w_branch_hgrn[0],
                 o=m_w_out[0], g2=m_ffn2_w_gate[0], u2=m_ffn2_w_up[0], d2=m_ffn2_w_down[0])
    big_m["in"] = m_w_in[0]
    big_v = dict(g1=v_ffn1_w_gate[0], u1=v_ffn1_w_up[0], d1=v_ffn1_w_down[0], a=v_w_branch_attn[0], b=v_w_branch_hgrn[0],
                 o=v_w_out[0], g2=v_ffn2_w_gate[0], u2=v_ffn2_w_up[0], d2=v_ffn2_w_down[0])
    big_v["in"] = v_w_in[0]
    for table in (big_w, big_m, big_v):
        for n in TRANSPOSED:
            table[n] = table[n].T

    slots = {n: _cast_into_slot(big_w[n], me_idx, CD) for n in BIG}
    lbl_slot = _cast_into_slot(hgrn_lb_logits.reshape(4, LANES), me_idx, F32)
    started, token = {}, ()
    for grp in (1, 2, 3):
        lands = [slots[n] for n in GROUPS[grp]] + ([lbl_slot] if grp == 2 else [])
        send, recv, _, lands, tok = _exchange_start("gather%d_start" % grp, [], lands, token)
        started[grp], token = (send, recv, lands), (tok,)

    def weights_of(grp, after):
        send, recv, lands = started[grp]
        got = _exchange_wait("gather%d_wait" % grp, send, recv, [], lands, tuple(after) + (token if grp == 1 else ()))
        by_halves = [i for i, land in enumerate(got) if _halved(land)]
        for i, whole in zip(by_halves, _pair_fill("gather%d_fill" % grp, [got[i] for i in by_halves])):
            got[i] = whole
        w = dict(zip(GROUPS[grp], got))
        if grp == 2:
            w["in"] = w["in"].reshape(-1, D_MODEL)
            w["o"] = w["o"].reshape(D_MODEL, D_MODEL)
            w["lbl"] = jnp.transpose(got[-1], (1, 0, 2)).reshape(4, N_SHARD * LANES)
        return w

    pending = {}

    def grads_done(grp, grads):
        names = list(grads)
        got = _pair_exchange([grads[n][1] for n in names])
        sums, owns = zip(*[_pair_sum(grads[n][0], r, c_idx, me_idx) for n, r in zip(names, got)])
        lands = [lax.empty(s_.shape, s_.dtype) for s_ in sums]
        send, recv, srcs, lands, tok = _exchange_start("reduce%d_start" % grp, list(sums), lands, ())
        pending[grp] = (names, send, recv, srcs, lands, owns, tok)
        return (tok,)

    def reduced_halves(grp, after):
        names, send, recv, srcs, lands, owns, _ = pending[grp]
        parts = _exchange_wait("reduce%d_wait" % grp, send, recv, srcs, lands, after)
        return names, [_chip_sum(o, p, me_idx) for o, p in zip(owns, parts)]

    loss, dx, small, lbl = _local_step(
        x[0], loss_target[0], (ffn1_norm_g, mix_norm_g, ffn2_norm_g, final_norm_g.reshape(1, -1)),
        q_norm_g, k_norm_g, hgrn_out_norm_g, weights_of, grads_done)

    dgq = small["gq"].reshape(8, HEAD_DIM).sum(axis=0)
    dgk = small["gk"].reshape(2, HEAD_DIM).sum(axis=0)
    lb_full = _hgrn_lower_bounds(lbl)
    dlog = []
    for d in (0, 1):
        t = small["lb"][d:d + 1] * lb_full[d] * (1.0 - lb_full[d])
        dlog += [t, -t]
    small_list = [small["g1"], small["gm"], small["g2"], small["gf"], small["ng"], dgq, dgk, jnp.concatenate(dlog, axis=0), loss[0, 0]]
    packed = _pack_rows(small_list, D_MODEL)
    n_rows = packed.shape[0]
    packed = jnp.pad(packed, ((0, (-n_rows) % 8), (0, 0)))
    red = _small_allreduce(packed)
    loss_out = red[n_rows - 1, 0]
    sg = dict(g1=red[0:1], gm=red[1:2], g2=red[2:3], gf=red[3], ng=red[4:5, :512], gq=red[5:6, :HEAD_DIM],
              gk=red[6:7, :HEAD_DIM])
    dlog_full = red[7:9].reshape(2, 2, 512)
    sg["lb"] = lax.dynamic_slice_in_dim(dlog_full, me * LANES, LANES, axis=2)

    small_w = dict(g1=ffn1_norm_g, gm=mix_norm_g, g2=ffn2_norm_g, gf=final_norm_g, ng=hgrn_out_norm_g, gq=q_norm_g,
                   gk=k_norm_g, lb=hgrn_lb_logits)
    small_m = dict(g1=m_ffn1_norm_g, gm=m_mix_norm_g, g2=m_ffn2_norm_g, gf=m_final_norm_g, ng=m_hgrn_out_norm_g,
                   gq=m_q_norm_g, gk=m_k_norm_g, lb=m_hgrn_lb_logits)
    small_v = dict(g1=v_ffn1_norm_g, gm=v_mix_norm_g, g2=v_ffn2_norm_g, gf=v_final_norm_g, ng=v_hgrn_out_norm_g,
                   gq=v_q_norm_g, gk=v_k_norm_g, lb=v_hgrn_lb_logits)
    small_names = ("g1", "gm", "g2", "gf", "ng", "gq", "gk", "lb")
    pack = lambda dct: _pack_rows([dct[n] for n in small_names], D_MODEL)
    pw, pgr, pm, pv = pack(small_w), pack(sg), pack(small_m), pack(small_v)
    pad8 = lambda a: jnp.pad(a, ((0, (-a.shape[0]) % 8), (0, 0)))
    sd, sm_, sv_ = _adamw(pad8(pw), pad8(pgr), pad8(pm), pad8(pv))

    def unpack(buf):
        out, r = {}, 0
        for n in small_names:
            size = small_w[n].size
            nr = -(-size // D_MODEL)
            out[n] = buf[r:r + nr].reshape(-1)[:size].reshape(small_w[n].shape)
            r += nr
        return out

    sdelta, snew_m, snew_v = unpack(sd), unpack(sm_), unpack(sv_)
    sgrad = {n: sg[n].reshape(small_w[n].shape) for n in small_names}

    bdelta, bnew_m, bnew_v, bgrad = {}, {}, {}, {}

    def update(names, halves):
        for n, own, got in zip(names, halves, _pair_share(halves)):
            res = _adamw_halves(big_w[n], own, got, big_m[n], big_v[n], c_idx)
            if n in TRANSPOSED:
                res = [r.T for r in res]
            bdelta[n], bnew_m[n], bnew_v[n], bgrad[n] = [r[None] for r in res]

    names3, halves3 = reduced_halves(3, (pending[1][-1],))
    names2, halves2 = reduced_halves(2, (halves3[0],))
    update(names3 + names2, halves3 + halves2)
    names1, halves1 = reduced_halves(1, (bdelta[names2[-1]],))
    update(names1, halves1)

    order = [("s", "g1"), ("b", "g1"), ("b", "u1"), ("b", "d1"), ("s", "gm"), ("b", "in"), ("s", "gq"), ("s", "gk"),
             ("s", "lb"), ("s", "ng"), ("b", "a"), ("b", "b"), ("b", "o"), ("s", "g2"), ("b", "g2"), ("b", "u2"),
             ("b", "d2"), ("s", "gf")]
    outs = [loss_out, dx[None]]
    for table_s, table_b in ((sgrad, bgrad), (sdelta, bdelta), (snew_m, bnew_m), (snew_v, bnew_v)):
        outs += [(table_s if kind == "s" else table_b)[n] for kind, n in order]
    return tuple(outs)
```

```python
import functools

import numpy as np
import jax
import jax.numpy as jnp
from jax import lax
from jax.experimental import pallas as pl
from jax.experimental.pallas import tpu as pltpu

F32 = jnp.float32
BF16 = jnp.bfloat16
CD = jnp.bfloat16

EPS = 1e-6
D_MODEL = 1024
HEAD_DIM = 64
GRID_W = 64
ROPE_THETA = 10000.0
CHUNK = 32
N_SHARD = 4
N_DEV = 8
VMEM_LIMIT = 56 * 1024 * 1024
LANES = 128
HG_TILE = 256
FFN_ROWS = 256

ADAM_LR = 0.001
ADAM_B1 = 0.9
ADAM_B2 = 0.999
ADAM_EPS = 1e-08
ADAM_WD = 0.01
ADAM_STEP = 10

NN = (((1,), (0,)), ((), ()))
NT = (((1,), (1,)), ((), ()))
TN = (((0,), (0,)), ((), ()))
MESH = pl.DeviceIdType.MESH
ANY = pl.BlockSpec(memory_space=pl.ANY)


def _mm(a, b, dn):
    return lax.dot_general(a.astype(CD), b.astype(CD), dn, preferred_element_type=F32)


def _split3(x):
    hi = x.astype(BF16)
    r = x - hi.astype(F32)
    mid = r.astype(BF16)
    lo = (r - mid.astype(F32)).astype(BF16)
    return hi, mid, lo


def _xdot(x, m):
    rows = x.shape[0]
    r = lax.dot_general(jnp.concatenate(_split3(x), axis=0), m, NN, preferred_element_type=F32)
    return r[:rows] + r[rows:2 * rows] + r[2 * rows:]


def _xdot_l(m, x):
    cols = x.shape[1]
    hi, mid, _ = _split3(x)
    r = lax.dot_general(m, jnp.concatenate([hi, mid], axis=1), NN, preferred_element_type=F32)
    return r[:, :cols] + r[:, cols:]


def _params(n_grid):
    return pltpu.CompilerParams(dimension_semantics=("arbitrary",) * n_grid, vmem_limit_bytes=VMEM_LIMIT)


def _sigmoid(x):
    return jax.nn.sigmoid(x)


def _np_blocksum(n):
    i = np.arange(n)
    return (i[:, None] // HEAD_DIM == i[None, :] // HEAD_DIM).astype(np.float32)


def _np_swap32(n):
    i = np.arange(n)
    partner = np.where(i % HEAD_DIM < HEAD_DIM // 2, i + HEAD_DIM // 2, i - HEAD_DIM // 2)
    m = np.zeros((n, n), np.float32)
    m[i, partner] = 1.0
    return m


def _np_expand_q():
    m = np.zeros((512, 1024), np.float32)
    for h in range(8):
        g = h // 4
        for d in range(HEAD_DIM):
            m[64 * h + d, 128 * h + 64 * g + d] = 1.0
    return m


def _np_bcast_head():
    m = np.zeros((512, 1024), np.float32)
    for h in range(8):
        m[64 * h, 128 * h:128 * h + 128] = 1.0
    return m


def _np_swap_halves():
    m = np.zeros((128, 128), np.float32)
    i = np.arange(128)
    m[i, (i + 64) % 128] = 1.0
    return m


def _np_hgrn_cums(t, rev):
    r = np.arange(t)[:, None]
    c = np.arange(t)[None, :]
    same = (r // CHUNK) == (c // CHUNK)
    if not rev:
        cum = same & (c <= r)
        mid = same & (c % CHUNK <= CHUNK // 2 - 1)
    else:
        cum = same & (c >= r)
        mid = same & (c % CHUNK >= CHUNK // 2)
    return np.concatenate([cum, mid, same], axis=0).astype(np.float32)


def _bf(a):
    return jnp.asarray(a, dtype=BF16)


def _rope_tables(seq_len):
    rows = seq_len // GRID_W
    row = jnp.repeat(jnp.arange(rows, dtype=F32), GRID_W)
    col = jnp.tile(jnp.arange(GRID_W, dtype=F32), rows)
    n_freq = HEAD_DIM // 4
    inv = ROPE_THETA ** (-jnp.arange(n_freq, dtype=F32) / n_freq)
    ang = jnp.concatenate([row[:, None] * inv, col[:, None] * inv], axis=-1)
    cos, sin = jnp.cos(ang), jnp.sin(ang)
    c64 = jnp.concatenate([cos, cos], axis=-1)
    s64 = jnp.concatenate([-sin, sin], axis=-1)
    return jnp.tile(c64, (1, 2)), jnp.tile(s64, (1, 2))


def _ffn_fwd(x, g, wg, wu, wd, tm):
    s, d = x.shape
    nsh, fs, _ = wg.shape

    def body(x_ref, g_ref, wg_ref, wu_ref, wd_ref, xo_ref, a_ref, da_ref, b_ref, hb_ref, acc, hs):
        j = pl.program_id(1)

        @pl.when(j == 0)
        def _():
            xv = x_ref[...]
            r = lax.rsqrt(jnp.mean(xv * xv, axis=-1, keepdims=True) + EPS)
            h = (xv * r * g_ref[...]).astype(CD)
            hs[...] = h
            hb_ref[...] = h
            acc[...] = jnp.zeros_like(acc)

        for r0 in range(0, tm, FFN_ROWS):
            rows = slice(r0, min(r0 + FFN_ROWS, tm))
            h = hs[rows, :]
            a = _mm(h, wg_ref[0], NT)
            b = _mm(h, wu_ref[0], NT)
            sg = _sigmoid(a)
            silu = a * sg
            acc[rows, :] += _mm(silu * b, wd_ref[0], NN)
            a_ref[0, rows, :] = silu.astype(CD)
            da_ref[0, rows, :] = (sg * (1.0 + a * (1.0 - sg))).astype(CD)
            b_ref[0, rows, :] = b.astype(CD)

        @pl.when(j == nsh - 1)
        def _():
            xo_ref[...] = x_ref[...] + 0.5 * acc[...]

    return pl.pallas_call(
        body, name="ffn_fwd", grid=(s // tm, nsh),
        in_specs=[pl.BlockSpec((tm, d), lambda i, j: (i, 0)), pl.BlockSpec((1, d), lambda i, j: (0, 0))]
        + [pl.BlockSpec((1, fs, d), lambda i, j: (j, 0, 0))] * 3,
        out_specs=[pl.BlockSpec((tm, d), lambda i, j: (i, 0))] + [pl.BlockSpec((1, tm, fs), lambda i, j: (j, i, 0))] * 3
        + [pl.BlockSpec((tm, d), lambda i, j: (i, 0))],
        out_shape=[jax.ShapeDtypeStruct((s, d), F32)] + [jax.ShapeDtypeStruct((nsh, s, fs), CD)] * 3
        + [jax.ShapeDtypeStruct((s, d), CD)],
        scratch_shapes=[pltpu.VMEM((tm, d), F32), pltpu.VMEM((tm, d), CD)],
        compiler_params=_params(2),
    )(x, g, wg, wu, wd)


def _ffn_bwd(dout, x, g, silu, dsilu, b, wg, wu, wd, tm):
    s, d = x.shape
    nsh, fs, _ = wg.shape

    def body(do_ref, x_ref, g_ref, sl_ref, ds_ref, b_ref, wg_ref, wu_ref, wd_ref,
             dx_ref, da_ref, db_ref, f_ref, dg_ref, do16_ref, dh):
        i = pl.program_id(0)
        j = pl.program_id(1)

        @pl.when(j == 0)
        def _():
            dh[...] = jnp.zeros_like(dh)
            do16_ref[...] = do_ref[...].astype(CD)

        @pl.when((i == 0) & (j == 0))
        def _():
            dg_ref[...] = jnp.zeros_like(dg_ref)

        for r0 in range(0, tm, FFN_ROWS):
            rows = slice(r0, min(r0 + FFN_ROWS, tm))
            sl = sl_ref[0, rows, :].astype(F32)
            bv = b_ref[0, rows, :].astype(F32)
            df = 0.5 * _mm(do_ref[rows, :], wd_ref[0], NT)
            da = df * bv * ds_ref[0, rows, :].astype(F32)
            db = df * sl
            dh[rows, :] += _mm(da, wg_ref[0], NN) + _mm(db, wu_ref[0], NN)
            da_ref[0, rows, :] = da.astype(CD)
            db_ref[0, rows, :] = db.astype(CD)
            f_ref[0, rows, :] = (sl * bv).astype(CD)

        @pl.when(j == nsh - 1)
        def _():
            xv = x_ref[...]
            r = lax.rsqrt(jnp.mean(xv * xv, axis=-1, keepdims=True) + EPS)
            dhv = dh[...]
            u = dhv * g_ref[...]
            dx_ref[...] = do_ref[...] + r * u - xv * (r * r * r) * jnp.mean(u * xv, axis=-1, keepdims=True)
            dg_ref[...] += jnp.sum(dhv * xv * r, axis=0, keepdims=True)

    act = pl.BlockSpec((1, tm, fs), lambda i, j: (j, i, 0))
    row = pl.BlockSpec((tm, d), lambda i, j: (i, 0))
    return pl.pallas_call(
        body, name="ffn_bwd", grid=(s // tm, nsh),
        in_specs=[row, row, pl.BlockSpec((1, d), lambda i, j: (0, 0)), act, act, act]
        + [pl.BlockSpec((1, fs, d), lambda i, j: (j, 0, 0))] * 3,
        out_specs=[row, act, act, act, pl.BlockSpec((1, d), lambda i, j: (0, 0)), row],
        out_shape=[jax.ShapeDtypeStruct((s, d), F32), jax.ShapeDtypeStruct((nsh, s, fs), CD),
                   jax.ShapeDtypeStruct((nsh, s, fs), CD), jax.ShapeDtypeStruct((nsh, s, fs), CD),
                   jax.ShapeDtypeStruct((1, d), F32), jax.ShapeDtypeStruct((s, d), CD)],
        scratch_shapes=[pltpu.VMEM((tm, d), F32)],
        compiler_params=_params(2),
    )(dout, x, g, silu, dsilu, b, wg, wu, wd)


def _tn_call(name, operands, in_specs, out_shape, out_spec, grid, acc_shape, pick, scale=1.0):
    nk = grid[-1]
    n_in = len(operands)

    def body(*refs):
        out_ref, out16_ref, acc = refs[n_in], refs[n_in + 1], refs[n_in + 2]
        k = pl.program_id(len(grid) - 1)

        @pl.when(k == 0)
        def _():
            acc[...] = jnp.zeros_like(acc)

        pick(refs[:n_in], acc)

        @pl.when(k == nk - 1)
        def _():
            res = (acc[...] if scale == 1.0 else acc[...] * scale).reshape(out_ref.shape)
            out_ref[...] = res
            out16_ref[...] = res.astype(CD)

    return pl.pallas_call(
        body, name=name, grid=grid, in_specs=in_specs, out_specs=[out_spec, out_spec],
        out_shape=[out_shape, jax.ShapeDtypeStruct(out_shape.shape, CD)],
        scratch_shapes=[pltpu.VMEM(acc_shape, F32)], compiler_params=_params(len(grid)),
    )(*operands)


def _dw_shared_b(name, a3, b, tk, scale):
    nj, s, m = a3.shape
    n = b.shape[1]

    def pick(refs, acc):
        rows = pl.ds(pl.multiple_of(pl.program_id(1) * tk, tk), tk)
        acc[...] += _mm(refs[0][0], refs[1][rows, :], TN)

    return _tn_call(name, (a3, b),
                    [pl.BlockSpec((1, tk, m), lambda j, k: (j, k, 0)), pl.BlockSpec((s, n), lambda j, k: (0, 0))],
                    jax.ShapeDtypeStruct((nj, m, n), F32), pl.BlockSpec((1, m, n), lambda j, k: (j, 0, 0)),
                    (nj, s // tk), (m, n), pick, scale)


def _dw_colblocks(name, a, b, nj, tk):
    s, m = a.shape
    n = b.shape[1] // nj

    def pick(refs, acc):
        acc[...] += _mm(refs[0][...], refs[1][...], TN)

    return _tn_call(name, (a, b),
                    [pl.BlockSpec((tk, m), lambda j, k: (k, 0)), pl.BlockSpec((tk, n), lambda j, k: (k, j))],
                    jax.ShapeDtypeStruct((nj, m, n), F32), pl.BlockSpec((1, m, n), lambda j, k: (j, 0, 0)),
                    (nj, s // tk), (m, n), pick)


DP_WIDTHS = (768, 512, 512, 512, 512, 512, 2048)
DP_CHUNK = 256


def _dp_chunk_maps():
    starts, counts, off = [], [], 0
    for w in DP_WIDTHS:
        starts.append(off // DP_CHUNK)
        counts.append(w // DP_CHUNK)
        off += w
    return starts, counts


def _dp_specs(tm, row_axis, chunk_axis):
    starts, counts = _dp_chunk_maps()
    specs = []
    for st, cnt in zip(starts, counts):
        def imap(*ids, st=st, cnt=cnt):
            return (ids[row_axis], jnp.clip(ids[chunk_axis] - st, 0, cnt - 1))
        specs.append(pl.BlockSpec((tm, DP_CHUNK), imap))
    return specs


def _dp_select(n, refs, fn):
    starts, counts = _dp_chunk_maps()
    for ref, st, cnt in zip(refs, starts, counts):
        @pl.when((n >= st) & (n < st + cnt))
        def _(ref=ref):
            fn(ref)


def _dw_in(dps, hb, tk):
    s, d = hb.shape
    n_chunks = sum(DP_WIDTHS) // DP_CHUNK

    def pick(refs, acc):
        rows = pl.ds(pl.multiple_of(pl.program_id(1) * tk, tk), tk)

        def add(ref):
            acc[...] += _mm(ref[...], refs[7][rows, :], TN)

        _dp_select(pl.program_id(0), refs[:7], add)

    return _tn_call("dw_in", (*dps, hb),
                    _dp_specs(tk, 1, 0) + [pl.BlockSpec((s, d), lambda n, k: (0, 0))],
                    jax.ShapeDtypeStruct((n_chunks * DP_CHUNK, d), F32), pl.BlockSpec((DP_CHUNK, d), lambda n, k: (n, 0)),
                    (n_chunks, s // tk), (DP_CHUNK, d), pick)


def _mix_in_fwd(x, g, w_t, tm):
    s, d = x.shape
    n_in = w_t.shape[0]

    def body(x_ref, g_ref, w_ref, qkv_ref, hg_ref, gt_ref, hb_ref):
        xv = x_ref[...]
        r = lax.rsqrt(jnp.mean(xv * xv, axis=-1, keepdims=True) + EPS)
        h = (xv * r * g_ref[...]).astype(CD)
        hb_ref[...] = h
        qkv_ref[...] = _mm(h, w_ref[0:768, :], NT)
        for c in range(5):
            hg_ref[:, 512 * c:512 * c + 512] = _mm(h, w_ref[768 + 512 * c:768 + 512 * c + 512, :], NT)
        for c in range(2):
            gt_ref[:, 1024 * c:1024 * c + 1024] = _mm(h, w_ref[3328 + 1024 * c:3328 + 1024 * c + 1024, :], NT)

    row = lambda w: pl.BlockSpec((tm, w), lambda i: (i, 0))
    return pl.pallas_call(
        body, name="mix_in_fwd", grid=(s // tm,),
        in_specs=[row(d), pl.BlockSpec((1, d), lambda i: (0, 0)), pl.BlockSpec((n_in, d), lambda i: (0, 0))],
        out_specs=[row(768), row(2560), row(2048), row(d)],
        out_shape=[jax.ShapeDtypeStruct((s, 768), F32), jax.ShapeDtypeStruct((s, 2560), F32),
                   jax.ShapeDtypeStruct((s, 2048), F32), jax.ShapeDtypeStruct((s, d), CD)],
        compiler_params=_params(1),
    )(x, g, w_t)


def _mix_in_bwd(dps, w_t, x, dres, g, tm, after=()):
    s, d = x.shape
    n_in = w_t.shape[0]

    def body(*refs):
        refs = refs[len(after):]
        dp_refs = refs[:7]
        w_ref, x_ref, dr_ref, g_ref, dx_ref, dg_ref = refs[7:]

        @pl.when(pl.program_id(0) == 0)
        def _():
            dg_ref[...] = jnp.zeros_like(dg_ref)

        dhv = jnp.zeros((tm, d), F32)
        off = 0
        for ref, width in zip(dp_refs, DP_WIDTHS):
            dhv = dhv + _mm(ref[...], w_ref[off:off + width, :], NN)
            off += width
        xv = x_ref[...]
        r = lax.rsqrt(jnp.mean(xv * xv, axis=-1, keepdims=True) + EPS)
        u = dhv * g_ref[...]
        dx_ref[...] = dr_ref[...] + r * u - xv * (r * r * r) * jnp.mean(u * xv, axis=-1, keepdims=True)
        dg_ref[...] += jnp.sum(dhv * xv * r, axis=0, keepdims=True)

    row = pl.BlockSpec((tm, d), lambda i: (i, 0))
    vec = pl.BlockSpec((1, d), lambda i: (0, 0))
    return pl.pallas_call(
        body, name="mix_in_bwd", grid=(s // tm,),
        in_specs=[ANY] * len(after) + [pl.BlockSpec((tm, w), lambda i: (i, 0)) for w in DP_WIDTHS]
        + [pl.BlockSpec((n_in, d), lambda i: (0, 0)), row, row, vec],
        out_specs=[row, vec],
        out_shape=[jax.ShapeDtypeStruct((s, d), F32), jax.ShapeDtypeStruct((1, d), F32)],
        compiler_params=_params(1),
    )(*after, *dps, w_t, x, dres, g)


def _headnorm_rope(x, gain, cos, sin, blocksum, swap):
    ss = _xdot(x * x, blocksum)
    r = lax.rsqrt(ss * (1.0 / HEAD_DIM) + EPS)
    y = x * r * gain
    return y * cos + _xdot(y, swap) * sin, r


def _headnorm_rope_bwd(dz, x, gain, cos, sin, blocksum, swap):
    ss = _xdot(x * x, blocksum)
    r = lax.rsqrt(ss * (1.0 / HEAD_DIM) + EPS)
    dy = dz * cos + _xdot(dz * sin, swap)
    u = dy * gain
    mean_ux = _xdot(u * x, blocksum) * (1.0 / HEAD_DIM)
    dx = r * u - x * (r * r * r) * mean_ux
    return dx, jnp.sum(dy * x * r, axis=0, keepdims=True)


def _qk_prep(pqkv, gq, gk, cos2, sin2, tm):
    s = pqkv.shape[0]
    bs512, sw512, eq, swh = _bf(_np_blocksum(512)), _bf(_np_swap32(512)), _bf(_np_expand_q()), _bf(_np_swap_halves())

    def body(q_ref, kv_ref, gq_ref, gk_ref, c_ref, s_ref, bs_ref, sw_ref, eq_ref, swh_ref, qe_ref, k_ref, v_ref, vs_ref):
        c2, s2 = c_ref[...], s_ref[...]
        c8, s8 = jnp.tile(c2, (1, 4)), jnp.tile(s2, (1, 4))
        bs, sw = bs_ref[...], sw_ref[...]
        zq, _ = _headnorm_rope(q_ref[...], gq_ref[...], c8, s8, bs, sw)
        qe_ref[...] = _mm(zq * (HEAD_DIM ** -0.5), eq_ref[...], NN).astype(CD)
        kv = kv_ref[...]
        zk, _ = _headnorm_rope(kv[:, :LANES], gk_ref[...], c2, s2, bs[:LANES, :LANES], sw[:LANES, :LANES])
        k_ref[...] = zk.astype(CD)
        v = kv[:, LANES:]
        v_ref[...] = v.astype(CD)
        vs_ref[...] = _mm(v, swh_ref[...], NN).astype(CD)

    full = lambda a: pl.BlockSpec(a.shape, lambda i: (0,) * a.ndim)
    tab = pl.BlockSpec((tm, LANES), lambda i: (i, 0))
    return pl.pallas_call(
        body, name="qk_prep", grid=(s // tm,),
        in_specs=[pl.BlockSpec((tm, 512), lambda i: (i, 0)), pl.BlockSpec((tm, 256), lambda i: (i, 2)),
                  full(gq), full(gk), tab, tab, full(bs512), full(sw512), full(eq), full(swh)],
        out_specs=[pl.BlockSpec((tm, 1024), lambda i: (i, 0)), tab, tab, tab],
        out_shape=[jax.ShapeDtypeStruct((s, 1024), CD)] + [jax.ShapeDtypeStruct((s, LANES), CD)] * 3,
        compiler_params=_params(1),
    )(pqkv, pqkv, gq, gk, cos2, sin2, bs512, sw512, eq, swh)


def _qk_prep_bwd(pqkv, dq, dk, dv, gq, gk, cos2, sin2, tm):
    s = pqkv.shape[0]
    bs512, sw512 = _bf(_np_blocksum(512)), _bf(_np_swap32(512))

    def body(q_ref, kv_ref, dq_ref, dk_ref, dv_ref, gq_ref, gk_ref, c_ref, s_ref, bs_ref, sw_ref,
             dp_ref, dgq_ref, dgk_ref):
        @pl.when(pl.program_id(0) == 0)
        def _():
            dgq_ref[...] = jnp.zeros_like(dgq_ref)
            dgk_ref[...] = jnp.zeros_like(dgk_ref)

        c2, s2 = c_ref[...], s_ref[...]
        c8, s8 = jnp.tile(c2, (1, 4)), jnp.tile(s2, (1, 4))
        bs, sw = bs_ref[...], sw_ref[...]
        dzq = dq_ref[...] * (HEAD_DIM ** -0.5)
        dxq, dgq = _headnorm_rope_bwd(dzq, q_ref[...], gq_ref[...], c8, s8, bs, sw)
        kv = kv_ref[...]
        dxk, dgk = _headnorm_rope_bwd(dk_ref[...], kv[:, :LANES], gk_ref[...], c2, s2, bs[:LANES, :LANES], sw[:LANES, :LANES])
        dp_ref[...] = jnp.concatenate([dxq, dxk, dv_ref[...]], axis=1).astype(CD)
        dgq_ref[...] += dgq
        dgk_ref[...] += dgk

    full = lambda a: pl.BlockSpec(a.shape, lambda i: (0,) * a.ndim)
    tab = pl.BlockSpec((tm, LANES), lambda i: (i, 0))
    return pl.pallas_call(
        body, name="qk_prep_bwd", grid=(s // tm,),
        in_specs=[pl.BlockSpec((tm, 512), lambda i: (i, 0)), pl.BlockSpec((tm, 256), lambda i: (i, 2)),
                  pl.BlockSpec((tm, 512), lambda i: (i, 0)), tab, tab, full(gq), full(gk), tab, tab,
                  full(bs512), full(sw512)],
        out_specs=[pl.BlockSpec((tm, 768), lambda i: (i, 0)), pl.BlockSpec((1, 512), lambda i: (0, 0)),
                   pl.BlockSpec((1, LANES), lambda i: (0, 0))],
        out_shape=[jax.ShapeDtypeStruct((s, 768), CD), jax.ShapeDtypeStruct((1, 512), F32),
                   jax.ShapeDtypeStruct((1, LANES), F32)],
        compiler_params=_params(1),
    )(pqkv, pqkv, dq, dk, dv, gq, gk, cos2, sin2, bs512, sw512)


def _kv_rows(h):
    return pl.ds(pl.multiple_of((h // 4) * HEAD_DIM, HEAD_DIM), HEAD_DIM)


def _attn_fwd(qe, k, v, vs, tq):
    s = k.shape[0]

    def body(qa_ref, qb_ref, k_ref, v_ref, vs_ref, o_ref, lse_ref):
        m = pl.program_id(0)
        grp = m // 2
        kk = k_ref[...]
        outs = []
        for idx, q_ref in enumerate((qa_ref, qb_ref)):
            sc = _mm(q_ref[...], kk, NT)
            mx = jnp.max(sc, axis=-1, keepdims=True)
            e = jnp.exp(sc - mx)
            l = jnp.sum(e, axis=-1, keepdims=True)
            lse_ref[idx] = mx + jnp.log(l)
            vsel = jnp.where(grp != idx, vs_ref[...], v_ref[...])
            outs.append(_mm(e, vsel, NN) * (1.0 / l))
        lane = lax.broadcasted_iota(jnp.int32, (1, LANES), 1)
        o_ref[...] = jnp.where(lane < HEAD_DIM, outs[0], outs[1])

    kv = pl.BlockSpec((s, LANES), lambda m, i: (0, 0))
    return pl.pallas_call(
        body, name="attn_fwd", grid=(4, s // tq),
        in_specs=[pl.BlockSpec((tq, LANES), lambda m, i: (i, 2 * m)), pl.BlockSpec((tq, LANES), lambda m, i: (i, 2 * m + 1)),
                  kv, kv, kv],
        out_specs=[pl.BlockSpec((tq, LANES), lambda m, i: (i, m)), pl.BlockSpec((2, tq, 1), lambda m, i: (m, i, 0))],
        out_shape=[jax.ShapeDtypeStruct((s, 512), F32), jax.ShapeDtypeStruct((8, s, 1), F32)],
        compiler_params=_params(2),
    )(qe, qe, k, v, vs)


def _attn_bwd(qe, k, kt, v, doe, delta, lse, tq):
    s = k.shape[0]

    def body(qa_ref, qb_ref, k_ref, kt_ref, v_ref, doa_ref, dob_ref, dla_ref, dlb_ref, lse_ref,
             dqt_ref, dkt_ref, dvt_ref, qt, dot):
        @pl.when((pl.program_id(0) == 0) & (pl.program_id(1) == 0))
        def _():
            dkt_ref[...] = jnp.zeros_like(dkt_ref)
            dvt_ref[...] = jnp.zeros_like(dvt_ref)

        rows = _kv_rows(2 * pl.program_id(0))
        kt = kt_ref[rows, :]
        dkt = jnp.zeros((HEAD_DIM, s), F32)
        dvt = jnp.zeros((HEAD_DIM, s), F32)
        for idx, (q_ref, do_ref, dl_ref) in enumerate(((qa_ref, doa_ref, dla_ref), (qb_ref, dob_ref, dlb_ref))):
            q, do = q_ref[...], do_ref[...]
            p = jnp.exp(_mm(q, k_ref[...], NT) - lse_ref[idx])
            dp = _mm(do, v_ref[...], NT)
            ds = (p * (dp - jnp.max(dl_ref[...], axis=-1, keepdims=True))).astype(CD)
            p = p.astype(CD)
            dqt_ref[idx * HEAD_DIM:(idx + 1) * HEAD_DIM, :] = _mm(kt, ds, NT)
            qt[idx] = jnp.transpose(q.astype(F32))
            dot[idx] = jnp.transpose(do.astype(F32))
            dkt = dkt + _mm(qt[idx, rows, :], ds, NN)
            dvt = dvt + _mm(dot[idx, rows, :], p, NN)
        dkt_ref[rows, :] += dkt
        dvt_ref[rows, :] += dvt

    kv = pl.BlockSpec((s, LANES), lambda m, i: (0, 0))
    kvt = pl.BlockSpec((LANES, s), lambda m, i: (0, 0))
    blk_a = pl.BlockSpec((tq, LANES), lambda m, i: (i, 2 * m))
    blk_b = pl.BlockSpec((tq, LANES), lambda m, i: (i, 2 * m + 1))
    return pl.pallas_call(
        body, name="attn_bwd", grid=(4, s // tq),
        in_specs=[blk_a, blk_b, kv, kvt, kv, blk_a, blk_b, blk_a, blk_b, pl.BlockSpec((2, tq, 1), lambda m, i: (m, i, 0))],
        out_specs=[pl.BlockSpec((LANES, tq), lambda m, i: (m, i)), kvt, kvt],
        out_shape=[jax.ShapeDtypeStruct((8 * HEAD_DIM, s), F32), jax.ShapeDtypeStruct((LANES, s), F32),
                   jax.ShapeDtypeStruct((LANES, s), F32)],
        scratch_shapes=[pltpu.VMEM((2, LANES, tq), F32), pltpu.VMEM((2, LANES, tq), F32)],
        compiler_params=_params(2),
    )(qe, qe, k, kt, v, doe, doe, delta, delta, lse)


@jax.custom_vjp
def _mm_nn(a, b):
    return _mm(a, b, NN)


_mm_nn.defvjp(lambda a, b: (_mm(a, b, NN), (a, b)),
              lambda res, g: (_mm(g, res[1], NT), _mm(res[0], g, TN)))


@jax.custom_vjp
def _mm_nt(a, b):
    return _mm(a, b, NT)


_mm_nt.defvjp(lambda a, b: (_mm(a, b, NT), (a, b)),
              lambda res, g: (_mm(g, res[1], NN), _mm(g, res[0], TN)))


@jax.custom_vjp
def _mm_tn(a, b):
    return _mm(a, b, TN)


_mm_tn.defvjp(lambda a, b: (_mm(a, b, TN), (a, b)),
              lambda res, g: (_mm(res[1], g, NT), _mm(res[0], g, NN)))


@jax.custom_vjp
def _cmm(m, mt, x):
    return _xdot_l(m, x)


_cmm.defvjp(lambda m, mt, x: (_xdot_l(m, x), (m, mt)),
            lambda res, g: (jnp.zeros_like(res[0]), jnp.zeros_like(res[1]), _xdot_l(res[1], g)))


def _hgrn_masks(t, rev):
    n_ch = t // CHUNK
    r = jnp.bitwise_and(lax.broadcasted_iota(jnp.int32, (2 * t, t), 0), t - 1)
    c = lax.broadcasted_iota(jnp.int32, (2 * t, t), 1)
    same = jnp.right_shift(r, 5) == jnp.right_shift(c, 5)
    tri2 = same & ((c >= r) if rev else (c <= r))
    pr = lax.broadcasted_iota(jnp.int32, (LANES, LANES), 0)
    pc = lax.broadcasted_iota(jnp.int32, (LANES, LANES), 1)
    diag = jnp.right_shift(pr, 6) == jnp.right_shift(pc, 6)
    qr = lax.broadcasted_iota(jnp.int32, (t, n_ch * LANES), 0)
    qc = lax.broadcasted_iota(jnp.int32, (t, n_ch * LANES), 1)
    rows_chunk = jnp.right_shift(qc, 7) == jnp.right_shift(qr, 5)
    vr = lax.broadcasted_iota(jnp.int32, (n_ch * LANES, t), 0)
    vc = lax.broadcasted_iota(jnp.int32, (n_ch * LANES, t), 1)
    cols_chunk = jnp.right_shift(vr, 7) == jnp.right_shift(vc, 5)
    return dict(tri2=tri2, diag=diag, rows_chunk=rows_chunk, cols_chunk=cols_chunk)


def _hgrn_gates(xf, lb):
    f = lb + (1.0 - lb) * _sigmoid(xf)
    return 1.0 - f, jnp.log(f)


def _hgrn_dir(xq, xf, v, lb, state, cm, cmt, mk, rev):
    t = xq.shape[0]
    n_ch = t // CHUNK
    lo = lax.broadcasted_iota(jnp.int32, (1, LANES), 1) < HEAD_DIM
    q = xq * _sigmoid(xq)
    k, lf = _hgrn_gates(xf, lb)
    cs = _cmm(cm, cmt, lf)
    b, bm, bl = cs[:t], cs[t:2 * t], cs[2 * t:]
    qd = q * jnp.exp(b - bm)
    kd = k * jnp.exp(bm - b)
    kc = k * jnp.exp(bl - b)
    qe = q * jnp.exp(b)
    qd2 = jnp.concatenate([jnp.where(lo, qd, 0.0), jnp.where(lo, 0.0, qd)], axis=0)
    o2 = _mm_nn(jnp.where(mk["tri2"], _mm_nt(qd2, kd), 0.0), v)
    o = jnp.where(lo, o2[:t], o2[t:])
    vexp = jnp.where(mk["cols_chunk"], jnp.concatenate([jnp.transpose(v)] * n_ch, axis=0), 0.0)
    adds = _mm_nn(vexp, kc)
    dec = jnp.exp(bl)
    entering = [None] * n_ch
    for c in (range(n_ch - 1, -1, -1) if rev else range(n_ch)):
        entering[c] = state
        d = jnp.concatenate([dec[c * CHUNK:(c + 1) * CHUNK]] * (LANES // CHUNK), axis=0)
        state = d * state + jnp.where(mk["diag"], adds[c * LANES:(c + 1) * LANES], 0.0)
    qexp = jnp.where(mk["rows_chunk"], jnp.concatenate([qe] * n_ch, axis=1), 0.0)
    return o + _mm_nt(qexp, jnp.concatenate(entering, axis=1)), state


def _hgrn_lower_bounds(l):
    out = []
    for d in (0, 1):
        l0, l1 = l[2 * d:2 * d + 1, :], l[2 * d + 1:2 * d + 2, :]
        mx = jnp.maximum(l0, l1)
        e0, e1 = jnp.exp(l0 - mx), jnp.exp(l1 - mx)
        out.append(e0 / (e0 + e1))
    return out


def _hgrn_consts(t):
    cf, cb = _np_hgrn_cums(t, False), _np_hgrn_cums(t, True)
    return (_bf(cf), _bf(cf.T), _bf(cb), _bf(cb.T), _bf(_np_blocksum(LANES)))


def _hgrn_fwd(ph, lbl, ng):
    s = ph.shape[0]
    t = min(HG_TILE, s)
    nt = s // t
    consts = _hgrn_consts(t)

    def body(xq_ref, xff_ref, xfb_ref, xi_ref, xg_ref, lbl_ref, ng_ref, cf_ref, cft_ref, cb_ref, cbt_ref, bs_ref,
             o_ref, pre_ref, st_ref, acc):
        lbf, lbb = _hgrn_lower_bounds(lbl_ref)
        mk_f, mk_b = _hgrn_masks(t, False), _hgrn_masks(t, True)
        zero = jnp.zeros((LANES, LANES), F32)

        def rows_of(i):
            return pl.ds(pl.multiple_of(i * t, t), t)

        acc[...] = jnp.zeros_like(acc)

        def step(i, states):
            tb = nt - 1 - i
            rf, rb = rows_of(i), rows_of(tb)
            st_ref[0, 0, i] = states[0]
            st_ref[0, 1, tb] = states[1]
            of, sf = _hgrn_dir(xq_ref[rf, :], xff_ref[rf, :], xi_ref[rf, :], lbf, states[0],
                               cf_ref[...], cft_ref[...], mk_f, False)
            ob, sb = _hgrn_dir(xq_ref[rb, :], xfb_ref[rb, :], xi_ref[rb, :], lbb, states[1],
                               cb_ref[...], cbt_ref[...], mk_b, True)
            acc[rf, :] += of
            acc[rb, :] += ob
            return sf, sb

        lax.fori_loop(0, nt, step, (zero, zero))

        def step_n(i, carry):
            rows = rows_of(i)
            o = acc[rows, :]
            ss = _xdot(o * o, bs_ref[...])
            r = lax.rsqrt(ss * (1.0 / HEAD_DIM) + EPS)
            xg = xg_ref[rows, :]
            pre_ref[rows, :] = o
            o_ref[rows, :] = (o * r * ng_ref[...]) * (xg * _sigmoid(xg))
            return carry

        lax.fori_loop(0, nt, step_n, 0)

    col = lambda off: pl.BlockSpec((s, LANES), lambda m: (0, off + m))
    full = lambda a: pl.BlockSpec(a.shape, lambda m: (0,) * a.ndim)
    return pl.pallas_call(
        body, name="hgrn_fwd", grid=(4,),
        in_specs=[col(0), col(4), col(8), col(12), col(16), pl.BlockSpec((4, LANES), lambda m: (0, m)),
                  pl.BlockSpec((1, LANES), lambda m: (0, m))] + [full(c) for c in consts],
        out_specs=[col(0), col(0), pl.BlockSpec((1, 2, nt, LANES, LANES), lambda m: (m, 0, 0, 0, 0))],
        out_shape=[jax.ShapeDtypeStruct((s, 512), F32), jax.ShapeDtypeStruct((s, 512), F32),
                   jax.ShapeDtypeStruct((4, 2, nt, LANES, LANES), F32)],
        scratch_shapes=[pltpu.VMEM((s, LANES), F32)],
        compiler_params=_params(1),
    )(ph, ph, ph, ph, ph, lbl, ng, *consts)


def _hgrn_bwd(ph, pre, dout, states, lbl, ng):
    s = ph.shape[0]
    t = min(HG_TILE, s)
    nt = s // t
    consts = _hgrn_consts(t)

    def body(xq_ref, xff_ref, xfb_ref, xi_ref, xg_ref, pre_ref, do_ref, st_ref, lbl_ref, ng_ref,
             cf_ref, cft_ref, cb_ref, cbt_ref, bs_ref,
             dq_ref, dff_ref, dfb_ref, di_ref, dg_ref, dlb_ref, dng_ref, dpre, dq_acc, dv_acc):
        lbf, lbb = _hgrn_lower_bounds(lbl_ref)
        mk_f, mk_b = _hgrn_masks(t, False), _hgrn_masks(t, True)
        zero = jnp.zeros((LANES, LANES), F32)
        zrow = jnp.zeros((1, LANES), F32)

        def rows_of(i):
            return pl.ds(pl.multiple_of(i * t, t), t)

        def step_n(i, dng):
            rows = rows_of(i)
            o, xg, do = pre_ref[rows, :], xg_ref[rows, :], do_ref[rows, :]
            bs = bs_ref[...]
            r = lax.rsqrt(_xdot(o * o, bs) * (1.0 / HEAD_DIM) + EPS)
            sg = _sigmoid(xg)
            gate = xg * sg
            don = do * gate
            dg_ref[rows, :] = (do * (o * r * ng_ref[...]) * (sg * (1.0 + xg * (1.0 - sg)))).astype(CD)
            u = don * ng_ref[...]
            dpre[rows, :] = r * u - o * (r * r * r) * (_xdot(u * o, bs) * (1.0 / HEAD_DIM))
            return dng + jnp.sum(don * o * r, axis=0, keepdims=True)

        dng_ref[...] = lax.fori_loop(0, nt, step_n, zrow)

        dq_acc[...] = jnp.zeros_like(dq_acc)
        dv_acc[...] = jnp.zeros_like(dv_acc)

        def grad_tile(ti, xf_ref, df_ref, lb, cm, cmt, mk, rev, st, dstate):
            rows = rows_of(ti)
            fn = lambda xq, xf, v, lbv, s_in: _hgrn_dir(xq, xf, v, lbv, s_in, cm, cmt, mk, rev)
            _, vjp = jax.vjp(fn, xq_ref[rows, :], xf_ref[rows, :], xi_ref[rows, :], lb, st)
            dxq, dxf, dv, dlb_t, dstate = vjp((dpre[rows, :], dstate))
            df_ref[rows, :] = dxf.astype(CD)
            dq_acc[rows, :] += dxq
            dv_acc[rows, :] += dv
            return dstate, dlb_t

        def step_g(i, carry):
            dsf, dsb, dlbf, dlbb = carry
            tf, tb = nt - 1 - i, i
            dsf, gf = grad_tile(tf, xff_ref, dff_ref, lbf, cf_ref[...], cft_ref[...], mk_f, False, st_ref[0, 0, tf], dsf)
            dsb, gb = grad_tile(tb, xfb_ref, dfb_ref, lbb, cb_ref[...], cbt_ref[...], mk_b, True, st_ref[0, 1, tb], dsb)
            return dsf, dsb, dlbf + gf, dlbb + gb

        _, _, dlbf, dlbb = lax.fori_loop(0, nt, step_g, (zero, zero, zrow, zrow))
        dlb_ref[0:1, :] = dlbf
        dlb_ref[1:2, :] = dlbb
        dq_ref[...] = dq_acc[...].astype(CD)
        di_ref[...] = dv_acc[...].astype(CD)

    col = lambda off: pl.BlockSpec((s, LANES), lambda m: (0, off + m))
    full = lambda a: pl.BlockSpec(a.shape, lambda m: (0,) * a.ndim)
    stream = jax.ShapeDtypeStruct((s, 512), CD)
    return pl.pallas_call(
        body, name="hgrn_bwd", grid=(4,),
        in_specs=[col(0), col(4), col(8), col(12), col(16), col(0), col(0),
                  pl.BlockSpec((1, 2, nt, LANES, LANES), lambda m: (m, 0, 0, 0, 0)),
                  pl.BlockSpec((4, LANES), lambda m: (0, m)),
                  pl.BlockSpec((1, LANES), lambda m: (0, m))] + [full(c) for c in consts],
        out_specs=[col(0)] * 5 + [pl.BlockSpec((2, LANES), lambda m: (0, m)), pl.BlockSpec((1, LANES), lambda m: (0, m))],
        out_shape=[stream] * 5 + [jax.ShapeDtypeStruct((2, 512), F32), jax.ShapeDtypeStruct((1, 512), F32)],
        scratch_shapes=[pltpu.VMEM((s, LANES), F32), pltpu.VMEM((s, LANES), F32), pltpu.VMEM((s, LANES), F32)],
        compiler_params=_params(1),
    )(ph, ph, ph, ph, ph, pre, dout, states, lbl, ng, *consts)


def _branch_out(o, w4):
    return jnp.concatenate([_mm(o, w4[j], NN) for j in range(N_SHARD)], axis=1)


def _mix_out_fwd(x, oa, ob, pg, wa, wb, wo, tm):
    s, d = x.shape

    def body(x_ref, oa_ref, ob_ref, ga_ref, gb_ref, wa_ref, wb_ref, wo_ref, xo_ref):
        ya = _branch_out(oa_ref[...], wa_ref)
        yb = _branch_out(ob_ref[...], wb_ref)
        merged = _sigmoid(ga_ref[...]) * ya + _sigmoid(gb_ref[...]) * yb
        xo_ref[...] = x_ref[...] + _mm(merged, wo_ref[...], NN)

    row = pl.BlockSpec((tm, d), lambda i: (i, 0))
    half = pl.BlockSpec((tm, 512), lambda i: (i, 0))
    full = lambda a: pl.BlockSpec(a.shape, lambda i: (0,) * a.ndim)
    return pl.pallas_call(
        body, name="mix_out_fwd", grid=(s // tm,),
        in_specs=[row, half, half, row, pl.BlockSpec((tm, d), lambda i: (i, 1)), full(wa), full(wb), full(wo)],
        out_specs=row, out_shape=jax.ShapeDtypeStruct((s, d), F32),
        compiler_params=_params(1),
    )(x, oa, ob, pg, pg, wa, wb, wo)


def _mix_out_bwd(dx, oa, ob, pg, wa, wb, wo, tm, after=()):
    s, d = dx.shape
    bs512, eq, ebc = _bf(_np_blocksum(512)), _bf(_np_expand_q()), _bf(_np_bcast_head())

    def body(*refs):
        (dx_ref, oa_ref, ob_ref, ga_ref, gb_ref, wa_ref, wb_ref, wo_ref, bs_ref, eq_ref, ebc_ref,
         dpg_ref, mg_ref, dya_ref, dyb_ref, doe_ref, dl_ref, dob_ref) = refs[len(after):]
        oa = oa_ref[...]
        ya = _branch_out(oa, wa_ref)
        yb = _branch_out(ob_ref[...], wb_ref)
        sa, sb = _sigmoid(ga_ref[...]), _sigmoid(gb_ref[...])
        mg_ref[...] = (sa * ya + sb * yb).astype(CD)
        dm = _mm(dx_ref[...], wo_ref[...], NT)
        dpg_ref[...] = jnp.concatenate([dm * ya * sa * (1.0 - sa), dm * yb * sb * (1.0 - sb)], axis=1).astype(CD)
        dya, dyb = dm * sa, dm * sb
        dya_ref[...] = dya.astype(CD)
        dyb_ref[...] = dyb.astype(CD)
        doa = jnp.zeros(oa.shape, F32)
        dob = jnp.zeros(oa.shape, F32)
        for j in range(N_SHARD):
            doa = doa + _mm(dya[:, 256 * j:256 * j + 256], wa_ref[j], NT)
            dob = dob + _mm(dyb[:, 256 * j:256 * j + 256], wb_ref[j], NT)
        dob_ref[...] = dob
        doe_ref[...] = _mm(doa, eq_ref[...], NN).astype(CD)
        dl_ref[...] = _xdot(_xdot(doa * oa, bs_ref[...]), ebc_ref[...])

    row = pl.BlockSpec((tm, d), lambda i: (i, 0))
    half = pl.BlockSpec((tm, 512), lambda i: (i, 0))
    full = lambda a: pl.BlockSpec(a.shape, lambda i: (0,) * a.ndim)
    wide = jax.ShapeDtypeStruct((s, d), CD)
    return pl.pallas_call(
        body, name="mix_out_bwd", grid=(s // tm,),
        in_specs=[ANY] * len(after) + [row, half, half, row, pl.BlockSpec((tm, d), lambda i: (i, 1)), full(wa), full(wb),
                                       full(wo), full(bs512), full(eq), full(ebc)],
        out_specs=[pl.BlockSpec((tm, 2048), lambda i: (i, 0)), row, row, row, row, row, half],
        out_shape=[jax.ShapeDtypeStruct((s, 2048), CD), wide, wide, wide, wide, jax.ShapeDtypeStruct((s, d), F32),
                   jax.ShapeDtypeStruct((s, 512), F32)],
        compiler_params=_params(1),
    )(*after, dx, oa, ob, pg, pg, wa, wb, wo, bs512, eq, ebc)


def _loss_head(x, g, target, tm):
    s, d = x.shape

    def body(x_ref, g_ref, t_ref, dx_ref, loss_ref, dg_ref):
        @pl.when(pl.program_id(0) == 0)
        def _():
            loss_ref[...] = jnp.zeros_like(loss_ref)
            dg_ref[...] = jnp.zeros_like(dg_ref)

        xv = x_ref[...]
        r = lax.rsqrt(jnp.mean(xv * xv, axis=-1, keepdims=True) + EPS)
        err = xv * r * g_ref[...] - t_ref[...]
        loss_ref[...] += 0.5 * jnp.sum(jnp.mean(err * err, axis=-1, keepdims=True))
        dy = err * (1.0 / d)
        u = dy * g_ref[...]
        dx_ref[...] = r * u - xv * (r * r * r) * jnp.mean(u * xv, axis=-1, keepdims=True)
        dg_ref[...] += jnp.sum(dy * xv * r, axis=0, keepdims=True)

    row = pl.BlockSpec((tm, d), lambda i: (i, 0))
    vec = pl.BlockSpec((1, d), lambda i: (0, 0))
    return pl.pallas_call(
        body, name="loss_head", grid=(s // tm,),
        in_specs=[row, vec, row], out_specs=[row, pl.BlockSpec((8, LANES), lambda i: (0, 0)), vec],
        out_shape=[jax.ShapeDtypeStruct((s, d), F32), jax.ShapeDtypeStruct((8, LANES), F32),
                   jax.ShapeDtypeStruct((1, d), F32)],
        compiler_params=_params(1),
    )(x, g, target)


def _position():
    x, y, c = lax.axis_index("x"), lax.axis_index("y"), lax.axis_index("c")
    return x, y, c, [(1 - x, y), (x, 1 - y), (1 - x, 1 - y)]


def _row_tile(rows, cap=256):
    best = rows
    for cand in range(8, min(rows, cap) + 1, 8):
        if rows % cand == 0:
            best = cand
    return best


def _cast_into_slots(shards, dtypes, me_idx):
    n = len(shards)
    tiles = [_row_tile(s.shape[0]) for s in shards]
    counts = [s.shape[0] // t for s, t in zip(shards, tiles)]
    starts = [sum(counts[:a]) for a in range(n)]

    def body(me_ref, *refs):
        i = pl.program_id(0)
        for a in range(n):
            @pl.when((i >= starts[a]) & (i < starts[a] + counts[a]))
            def _(a=a):
                refs[n + a][0] = refs[a][...].astype(dtypes[a])

    tile_of = [lambda i, a=a: jnp.clip(i - starts[a], 0, counts[a] - 1) for a in range(n)]
    return pl.pallas_call(
        body, name="cast_into_slots",
        grid_spec=pltpu.PrefetchScalarGridSpec(
            num_scalar_prefetch=1, grid=(sum(counts),),
            in_specs=[pl.BlockSpec((tiles[a], shards[a].shape[1]), lambda i, me, a=a: (tile_of[a](i), 0)) for a in range(n)],
            out_specs=[pl.BlockSpec((1, tiles[a], shards[a].shape[1]), lambda i, me, a=a: (me[0], tile_of[a](i), 0))
                       for a in range(n)]),
        out_shape=[jax.ShapeDtypeStruct((N_SHARD,) + s.shape, dt) for s, dt in zip(shards, dtypes)],
        compiler_params=_params(1),
    )(me_idx, *shards)


HBM_SPEC = pl.BlockSpec(memory_space=pltpu.HBM)
SEM_SPEC = pl.BlockSpec(memory_space=pltpu.SEMAPHORE)
DATAFLOW = pltpu.SideEffectType.DATAFLOW_SIDE_EFFECTING


def _exchange_copies(srcs, lands, send, recv, gather):
    x, y, c, chips = _position()
    me = 2 * x + y
    out = []
    for a in range(len(lands)):
        dst = lands[a].at[me]
        if gather and _halved(lands[a]):
            half = lands[a].shape[1] // 2
            dst = lands[a].at[me, pl.ds(c * half, half), :]
        for k, (px, py) in enumerate(chips):
            src = dst if gather else srcs[a].at[2 * px + py]
            out.append(pltpu.make_async_remote_copy(src_ref=src, dst_ref=dst, send_sem=send.at[3 * a + k],
                                                    recv_sem=recv.at[3 * a + k], device_id=(px, py, c), device_id_type=MESH))
    return out


def _halved(land):
    return land.shape[1] % 32 == 0


def _pair_fill(name, lands):
    n = len(lands)

    def body(*refs):
        src, dst = refs[:n], refs[n:2 * n]
        send, recv = refs[2 * n:]
        x, y, c, chips = _position()
        copies = []
        for a in range(n):
            half = src[a].shape[1] // 2
            for k, (px, py) in enumerate(chips):
                rows = (2 * px + py, pl.ds(c * half, half), slice(None))
                cp = pltpu.make_async_remote_copy(src_ref=src[a].at[rows], dst_ref=dst[a].at[rows], send_sem=send.at[a, k],
                                                  recv_sem=recv.at[a, k], device_id=(x, y, 1 - c), device_id_type=MESH)
                cp.start()
                copies.append(cp)
        for cp in copies:
            cp.wait()

    return pl.pallas_call(
        body, name=name, in_specs=[ANY] * n, out_specs=[ANY] * n,
        out_shape=[jax.ShapeDtypeStruct(l.shape, l.dtype) for l in lands],
        input_output_aliases={a: a for a in range(n)},
        scratch_shapes=[pltpu.SemaphoreType.DMA((n, 3)), pltpu.SemaphoreType.DMA((n, 3))],
    )(*lands)


def _exchange_start(name, srcs, lands, after):
    ns, nl, na = len(srcs), len(lands), len(after)
    gather = ns == 0

    def body(*refs):
        src_refs, land_refs = refs[:ns], refs[ns:ns + nl]
        send, recv = refs[ns + nl + na], refs[ns + nl + na + 1]
        token = refs[-1]
        for cp in _exchange_copies(src_refs, land_refs, send, recv, gather):
            cp.start()
        token[...] = jnp.zeros_like(token)

    arrays = [pltpu.with_memory_space_constraint(a, pltpu.HBM) for a in list(srcs) + list(lands)]
    outs = pl.pallas_call(
        body, name=name,
        out_shape=(pltpu.SemaphoreType.DMA((3 * nl,)), pltpu.SemaphoreType.DMA((3 * nl,)),
                   *[pltpu.HBM(a.shape, a.dtype) for a in arrays], jax.ShapeDtypeStruct((8, LANES), F32)),
        in_specs=[HBM_SPEC] * (ns + nl) + [ANY] * na,
        out_specs=(SEM_SPEC, SEM_SPEC, *[HBM_SPEC] * (ns + nl), pl.BlockSpec(memory_space=pltpu.VMEM)),
        input_output_aliases={i: 2 + i for i in range(ns + nl)},
        compiler_params=pltpu.CompilerParams(has_side_effects=DATAFLOW),
    )(*arrays, *after)
    return outs[0], outs[1], list(outs[2:2 + ns]), list(outs[2 + ns:2 + ns + nl]), outs[-1]


def _exchange_wait(name, send, recv, srcs, lands, after):
    ns, nl, na = len(srcs), len(lands), len(after)
    gather = ns == 0

    def body(*refs):
        src_refs, land_refs = refs[:ns], refs[ns:ns + nl]
        send_ref, recv_ref = refs[ns + nl], refs[ns + nl + 1]
        for cp in _exchange_copies(src_refs, land_refs, send_ref, recv_ref, gather):
            cp.wait_send()
            cp.wait_recv()

    outs = pl.pallas_call(
        body, name=name,
        out_shape=tuple(pltpu.HBM(a.shape, a.dtype) for a in list(srcs) + list(lands)),
        in_specs=[HBM_SPEC] * (ns + nl) + [SEM_SPEC, SEM_SPEC] + [ANY] * na,
        out_specs=tuple([HBM_SPEC] * (ns + nl)),
        input_output_aliases={i: i for i in range(ns + nl)},
        compiler_params=pltpu.CompilerParams(has_side_effects=DATAFLOW),
    )(*srcs, *lands, send, recv, *after)
    return list(outs[ns:])


def _pair_exchange(grads):
    n = len(grads)

    def body(*refs):
        src, dst = refs[:n], refs[n:2 * n]
        send, recv = refs[2 * n:]
        x, y, c, _ = _position()
        copies = []
        for a in range(n):
            half = src[a].shape[1] // 2
            cp = pltpu.make_async_remote_copy(
                src_ref=src[a].at[:, pl.ds((1 - c) * half, half), :], dst_ref=dst[a], send_sem=send.at[a],
                recv_sem=recv.at[a], device_id=(x, y, 1 - c), device_id_type=MESH)
            cp.start()
            copies.append(cp)
        for cp in copies:
            cp.wait()

    return pl.pallas_call(
        body, name="grad_pair_exchange", in_specs=[ANY] * n, out_specs=[ANY] * n,
        out_shape=[jax.ShapeDtypeStruct((g.shape[0], g.shape[1] // 2, g.shape[2]), g.dtype) for g in grads],
        scratch_shapes=[pltpu.SemaphoreType.DMA((n,)), pltpu.SemaphoreType.DMA((n,))],
    )(*grads)


def _shard_of(a):
    return lambda i: jnp.clip(i - a * N_SHARD, 0, N_SHARD - 1)


def _pair_sum(gs, gots, c_idx, me_idx):
    n = len(gs)
    halves = [(g.shape[1] // 2, g.shape[2]) for g in gs]

    def body(c_ref, me_ref, *refs):
        g_refs, got_refs, s_refs, own_refs = (refs[k * n:(k + 1) * n] for k in range(4))
        i = pl.program_id(0)
        for a in range(n):
            @pl.when(i // N_SHARD == a)
            def _(a=a):
                sm = g_refs[a][...] + got_refs[a][...].astype(F32)
                s_refs[a][...] = sm.astype(CD)

                @pl.when(i % N_SHARD == me_ref[0])
                def _():
                    own_refs[a][...] = sm[0]

    shard = [_shard_of(a) for a in range(n)]
    return pl.pallas_call(
        body, name="grad_pair_sum",
        grid_spec=pltpu.PrefetchScalarGridSpec(
            num_scalar_prefetch=2, grid=(n * N_SHARD,),
            in_specs=[pl.BlockSpec((1, h, c_), lambda i, c, me, a=a: (shard[a](i), c[0], 0)) for a, (h, c_) in enumerate(halves)]
            + [pl.BlockSpec((1, h, c_), lambda i, c, me, a=a: (shard[a](i), 0, 0)) for a, (h, c_) in enumerate(halves)],
            out_specs=[pl.BlockSpec((1, h, c_), lambda i, c, me, a=a: (shard[a](i), 0, 0)) for a, (h, c_) in enumerate(halves)]
            + [pl.BlockSpec((h, c_), lambda i, c, me: (0, 0)) for h, c_ in halves]),
        out_shape=[jax.ShapeDtypeStruct((N_SHARD, h, c_), CD) for h, c_ in halves]
        + [jax.ShapeDtypeStruct((h, c_), F32) for h, c_ in halves],
        compiler_params=_params(1),
    )(c_idx, me_idx, *gs, *gots)


def _chip_sum(owns, gots, me_idx):
    n = len(owns)

    def body(me_ref, *refs):
        own_refs, got_refs, out_refs = (refs[k * n:(k + 1) * n] for k in range(3))
        i = pl.program_id(0)
        j = i % N_SHARD
        for a in range(n):
            @pl.when(i // N_SHARD == a)
            def _(a=a):
                term = jnp.where(j == me_ref[0], own_refs[a][...], got_refs[a][0].astype(F32))

                @pl.when(j == 0)
                def _():
                    out_refs[a][...] = term

                @pl.when(j > 0)
                def _():
                    out_refs[a][...] += term

    shard = [_shard_of(a) for a in range(n)]
    whole = [pl.BlockSpec(o.shape, lambda i, me: (0, 0)) for o in owns]
    return pl.pallas_call(
        body, name="grad_chip_sum",
        grid_spec=pltpu.PrefetchScalarGridSpec(
            num_scalar_prefetch=1, grid=(n * N_SHARD,),
            in_specs=whole + [pl.BlockSpec((1,) + o.shape, lambda i, me, a=a: (shard[a](i), 0, 0)) for a, o in enumerate(owns)],
            out_specs=whole),
        out_shape=[jax.ShapeDtypeStruct(o.shape, F32) for o in owns],
        compiler_params=_params(1),
    )(me_idx, *owns, *gots)


def _pair_share(halves):
    n = len(halves)

    def body(*refs):
        src, dst = refs[:n], refs[n:2 * n]
        send, recv = refs[2 * n:]
        x, y, c, _ = _position()
        copies = []
        for a in range(n):
            cp = pltpu.make_async_remote_copy(src_ref=src[a], dst_ref=dst[a], send_sem=send.at[a],
                                              recv_sem=recv.at[a], device_id=(x, y, 1 - c), device_id_type=MESH)
            cp.start()
            copies.append(cp)
        for cp in copies:
            cp.wait()

    return pl.pallas_call(
        body, name="grad_pair_share", in_specs=[ANY] * n, out_specs=[ANY] * n,
        out_shape=[jax.ShapeDtypeStruct(h.shape, h.dtype) for h in halves],
        scratch_shapes=[pltpu.SemaphoreType.DMA((n,)), pltpu.SemaphoreType.DMA((n,))],
    )(*halves)


def _small_allreduce(buf):
    rows, cols = buf.shape

    def body(src_ref, out_ref, slots, send, recv):
        x, y, c, _ = _position()
        me = 4 * x + 2 * y + c
        slots[me] = src_ref[...]
        copies = []
        k = 0
        for dx in (0, 1):
            for dy in (0, 1):
                for dc in (0, 1):
                    if (dx, dy, dc) == (0, 0, 0):
                        continue
                    peer = (jnp.where(dx, 1 - x, x), jnp.where(dy, 1 - y, y), jnp.where(dc, 1 - c, c))
                    cp = pltpu.make_async_remote_copy(src_ref=src_ref, dst_ref=slots.at[me], send_sem=send.at[k],
                                                      recv_sem=recv.at[k], device_id=peer, device_id_type=MESH)
                    cp.start()
                    copies.append(cp)
                    k += 1
        for cp in copies:
            cp.wait()
        total = slots[0]
        for dev in range(1, N_DEV):
            total = total + slots[dev]
        out_ref[...] = total

    vm = pl.BlockSpec(memory_space=pltpu.VMEM)
    return pl.pallas_call(
        body, name="small_allreduce", in_specs=[vm], out_specs=vm,
        out_shape=jax.ShapeDtypeStruct((rows, cols), F32),
        scratch_shapes=[pltpu.VMEM((N_DEV, rows, cols), F32), pltpu.SemaphoreType.DMA((N_DEV - 1,)),
                        pltpu.SemaphoreType.DMA((N_DEV - 1,))],
    )(buf)


def _adamw_math(w, gv, m, v):
    mn = ADAM_B1 * m + (1.0 - ADAM_B1) * gv
    vn = ADAM_B2 * v + (1.0 - ADAM_B2) * (gv * gv)
    m_hat = mn / (1.0 - ADAM_B1 ** ADAM_STEP)
    v_hat = vn / (1.0 - ADAM_B2 ** ADAM_STEP)
    return -ADAM_LR * (m_hat / (jnp.sqrt(v_hat) + ADAM_EPS) + ADAM_WD * w), mn, vn


def _adamw(w, g, m, v):
    rows, cols = w.shape
    tr = _row_tile(rows)

    def body(w_ref, g_ref, m_ref, v_ref, d_ref, mo_ref, vo_ref):
        d_ref[...], mo_ref[...], vo_ref[...] = _adamw_math(w_ref[...], g_ref[...], m_ref[...], v_ref[...])

    blk = pl.BlockSpec((tr, cols), lambda i: (i, 0))
    shp = jax.ShapeDtypeStruct((rows, cols), F32)
    return pl.pallas_call(
        body, name="adamw", grid=(rows // tr,), in_specs=[blk] * 4, out_specs=[blk] * 3, out_shape=[shp] * 3,
        compiler_params=_params(1),
    )(w, g, m, v)


def _adamw_halves(w, own, got, m, v, c_idx):
    rows, cols = w.shape
    tr = _row_tile(rows // 2)
    per_half = rows // 2 // tr

    def body(c_ref, w_ref, own_ref, got_ref, m_ref, v_ref, d_ref, mo_ref, vo_ref, g_ref):
        mine = (pl.program_id(0) // per_half) == c_ref[0]
        gv = jnp.where(mine, own_ref[...], got_ref[...])
        g_ref[...] = gv
        d_ref[...], mo_ref[...], vo_ref[...] = _adamw_math(w_ref[...], gv, m_ref[...], v_ref[...])

    blk = pl.BlockSpec((tr, cols), lambda i, c: (i, 0))
    own_blk = pl.BlockSpec((tr, cols), lambda i, c: (jnp.where(i // per_half == c[0], i % per_half, 0), 0))
    got_blk = pl.BlockSpec((tr, cols), lambda i, c: (jnp.where(i // per_half == c[0], 0, i % per_half), 0))
    shp = jax.ShapeDtypeStruct((rows, cols), F32)
    return pl.pallas_call(
        body, name="adamw_halves",
        grid_spec=pltpu.PrefetchScalarGridSpec(num_scalar_prefetch=1, grid=(rows // tr,),
                                               in_specs=[blk, own_blk, got_blk, blk, blk], out_specs=[blk] * 4),
        out_shape=[shp] * 4, compiler_params=_params(1),
    )(c_idx, w, own, got, m, v)


def _local_step(x, target, norm_gains, q_g, k_g, ng, weights_of, grads_done):
    s = x.shape[0]
    tm = min(512, s)
    tq = min(256, s)
    tf = min(1024, s)
    g1, gm, g2, gf = norm_gains
    cos2, sin2 = _rope_tables(s)
    gq8 = jnp.tile(q_g, (1, 8))
    gk2 = jnp.tile(k_g, (1, 2))

    tn = min(256, s)
    tk = min(1024, s)
    w1 = weights_of(1, ())
    x1, s1, t1, b1, h1 = _ffn_fwd(x, g1, w1["g1"], w1["u1"], w1["d1"], tf)
    w2 = weights_of(2, (x1,))
    lbl = w2["lbl"]
    pqkv, ph, pg, hm = _mix_in_fwd(x1, gm, w2["in"], tn)
    qe, kr, vr, vs = _qk_prep(pqkv, gq8, gk2, cos2, sin2, tm)
    oa, lse = _attn_fwd(qe, kr, vr, vs, tq)
    ob, pre, hstates = _hgrn_fwd(ph, lbl, ng)
    x2 = _mix_out_fwd(x1, oa, ob, pg, w2["a"], w2["b"], w2["o"], tm)
    w3 = weights_of(3, (x2,))
    x3, s2, t2, b2, h2 = _ffn_fwd(x2, g2, w3["g2"], w3["u2"], w3["d2"], tf)
    dx3, loss, dgf = _loss_head(x3, gf, target, tm)

    dx2, da2, db2, f2, dg2, dx3c = _ffn_bwd(dx3, x2, g2, s2, t2, b2, w3["g2"], w3["u2"], w3["d2"], tm)
    tok = grads_done(3, dict(g2=_dw_shared_b("dw_gate", da2, h2, tk, 1.0), u2=_dw_shared_b("dw_gate", db2, h2, tk, 1.0),
                             d2=_dw_shared_b("dw_down", f2, dx3c, tk, 0.5)))

    dpg, mg, dya, dyb, doe, delta, dob = _mix_out_bwd(dx2, oa, ob, pg, w2["a"], w2["b"], w2["o"], tm, tok)
    g_o = [g.reshape(N_SHARD, D_MODEL // N_SHARD, D_MODEL) for g in _dw_colblocks("dw_out", mg, dx2, 1, tk)]
    g_a = _dw_colblocks("dw_branch", oa, dya, N_SHARD, tk)
    g_b = _dw_colblocks("dw_branch", ob, dyb, N_SHARD, tk)
    dqt, dkt, dvt = _attn_bwd(qe, kr, kr.T, vr, doe, delta, lse, tq)
    dqkv, dgq, dgk = _qk_prep_bwd(pqkv, dqt.T, dkt.T, dvt.T, gq8, gk2, cos2, sin2, tm)
    dhq, dhff, dhfb, dhi, dhg, dlb, dng = _hgrn_bwd(ph, pre, dob, hstates, lbl, ng)
    dps = (dqkv, dhq, dhff, dhfb, dhi, dhg, dpg)
    g_in = [g.reshape(N_SHARD, -1, D_MODEL) for g in _dw_in(dps, hm, min(2048, s))]
    tok = grads_done(2, {"in": g_in, "a": g_a, "b": g_b, "o": g_o})
    dx1, dgm = _mix_in_bwd(dps, w2["in"], x1, dx2, gm, tn, tok)

    dx0, da1, db1, f1, dg1, dx1c = _ffn_bwd(dx1, x, g1, s1, t1, b1, w1["g1"], w1["u1"], w1["d1"], tm)
    grads_done(1, dict(g1=_dw_shared_b("dw_gate", da1, h1, tk, 1.0), u1=_dw_shared_b("dw_gate", db1, h1, tk, 1.0),
                       d1=_dw_shared_b("dw_down", f1, dx1c, tk, 0.5)))
    small = dict(g1=dg1, gm=dgm, g2=dg2, gf=dgf, gq=dgq, gk=dgk, lb=dlb, ng=dng)
    return loss, dx0, small, lbl


GROUPS = {1: ("g1", "u1", "d1"), 2: ("in", "a", "b", "o"), 3: ("g2", "u2", "d2")}
BIG = GROUPS[1] + GROUPS[2] + GROUPS[3]
TRANSPOSED = ("g1", "u1", "in", "g2", "u2")


def _pack_rows(vectors, width):
    rows = []
    for vct in vectors:
        flat = vct.reshape(-1)
        pad = (-flat.shape[0]) % width
        rows.append(jnp.pad(flat, (0, pad)).reshape(-1, width))
    return jnp.concatenate(rows, axis=0)


def kernel(x, ffn1_norm_g, ffn1_w_gate, ffn1_w_up, ffn1_w_down, mix_norm_g, w_in, q_norm_g, k_norm_g, hgrn_lb_logits, hgrn_out_norm_g, w_branch_attn, w_branch_hgrn, w_out, ffn2_norm_g, ffn2_w_gate, ffn2_w_up, ffn2_w_down, final_norm_g, loss_target, m_ffn1_norm_g, m_ffn1_w_gate, m_ffn1_w_up, m_ffn1_w_down, m_mix_norm_g, m_w_in, m_q_norm_g, m_k_norm_g, m_hgrn_lb_logits, m_hgrn_out_norm_g, m_w_branch_attn, m_w_branch_hgrn, m_w_out, m_ffn2_norm_g, m_ffn2_w_gate, m_ffn2_w_up, m_ffn2_w_down, m_final_norm_g, v_ffn1_norm_g, v_ffn1_w_gate, v_ffn1_w_up, v_ffn1_w_down, v_mix_norm_g, v_w_in, v_q_norm_g, v_k_norm_g, v_hgrn_lb_logits, v_hgrn_out_norm_g, v_w_branch_attn, v_w_branch_hgrn, v_w_out, v_ffn2_norm_g, v_ffn2_w_gate, v_ffn2_w_up, v_ffn2_w_down, v_final_norm_g):
    xi, yi, ci = lax.axis_index("x"), lax.axis_index("y"), lax.axis_index("c")
    me = 2 * xi + yi
    c_idx = jnp.reshape(ci, (1,)).astype(jnp.int32)
    me_idx = jnp.reshape(me, (1,)).astype(jnp.int32)

    big_w = dict(g1=ffn1_w_gate[0], u1=ffn1_w_up[0], d1=ffn1_w_down[0], a=w_branch_attn[0], b=w_branch_hgrn[0],
                 o=w_out[0], g2=ffn2_w_gate[0], u2=ffn2_w_up[0], d2=ffn2_w_down[0])
    big_w["in"] = w_in[0]
    big_m = dict(g1=m_ffn1_w_gate[0], u1=m_ffn1_w_up[0], d1=m_ffn1_w_down[0], a=m_w_branch_attn[0], b=m_w_branch_hgrn[0],
                 o=m_w_out[0], g2=m_ffn2_w_gate[0], u2=m_ffn2_w_up[0], d2=m_ffn2_w_down[0])
    big_m["in"] = m_w_in[0]
    big_v = dict(g1=v_ffn1_w_gate[0], u1=v_ffn1_w_up[0], d1=v_ffn1_w_down[0], a=v_w_branch_attn[0], b=v_w_branch_hgrn[0],
                 o=v_w_out[0], g2=v_ffn2_w_gate[0], u2=v_ffn2_w_up[0], d2=v_ffn2_w_down[0])
    big_v["in"] = v_w_in[0]
    for table in (big_w, big_m, big_v):
        for n in TRANSPOSED:
            table[n] = table[n].T

    started, token = {}, ()
    for grp in (1, 2, 3):
        shards = [big_w[n] for n in GROUPS[grp]] + ([hgrn_lb_logits.reshape(4, LANES)] if grp == 2 else [])
        dtypes = [CD] * len(GROUPS[grp]) + ([F32] if grp == 2 else [])
        lands = _cast_into_slots(shards, dtypes, me_idx)
        send, recv, _, lands, tok = _exchange_start("gather%d_start" % grp, [], lands, token)
        started[grp], token = (send, recv, lands), (tok,)

    def weights_of(grp, after):
        send, recv, lands = started[grp]
        got = _exchange_wait("gather%d_wait" % grp, send, recv, [], lands, tuple(after) + (token if grp == 1 else ()))
        by_halves = [i for i, land in enumerate(got) if _halved(land)]
        for i, whole in zip(by_halves, _pair_fill("gather%d_fill" % grp, [got[i] for i in by_halves])):
            got[i] = whole
        w = dict(zip(GROUPS[grp], got))
        if grp == 2:
            w["in"] = w["in"].reshape(-1, D_MODEL)
            w["o"] = w["o"].reshape(D_MODEL, D_MODEL)
            w["lbl"] = jnp.transpose(got[-1], (1, 0, 2)).reshape(4, N_SHARD * LANES)
        return w

    pending = {}

    def grads_done(grp, grads):
        names = list(grads)
        got = _pair_exchange([grads[n][1] for n in names])
        res = _pair_sum([grads[n][0] for n in names], got, c_idx, me_idx)
        sums, owns = res[:len(names)], res[len(names):]
        lands = [lax.empty(s_.shape, s_.dtype) for s_ in sums]
        send, recv, srcs, lands, tok = _exchange_start("reduce%d_start" % grp, list(sums), lands, ())
        pending[grp] = (names, send, recv, srcs, lands, owns, tok)
        return (tok,)

    def reduced_halves(grp, after):
        names, send, recv, srcs, lands, owns, _ = pending[grp]
        parts = _exchange_wait("reduce%d_wait" % grp, send, recv, srcs, lands, after)
        return names, list(_chip_sum(list(owns), parts, me_idx))

    loss, dx, small, lbl = _local_step(
        x[0], loss_target[0], (ffn1_norm_g, mix_norm_g, ffn2_norm_g, final_norm_g.reshape(1, -1)),
        q_norm_g, k_norm_g, hgrn_out_norm_g, weights_of, grads_done)

    dgq = small["gq"].reshape(8, HEAD_DIM).sum(axis=0)
    dgk = small["gk"].reshape(2, HEAD_DIM).sum(axis=0)
    lb_full = _hgrn_lower_bounds(lbl)
    dlog = []
    for d in (0, 1):
        t = small["lb"][d:d + 1] * lb_full[d] * (1.0 - lb_full[d])
        dlog += [t, -t]
    small_list = [small["g1"], small["gm"], small["g2"], small["gf"], small["ng"], dgq, dgk, jnp.concatenate(dlog, axis=0), loss[0, 0]]
    packed = _pack_rows(small_list, D_MODEL)
    n_rows = packed.shape[0]
    packed = jnp.pad(packed, ((0, (-n_rows) % 8), (0, 0)))
    red = _small_allreduce(packed)
    loss_out = red[n_rows - 1, 0]
    sg = dict(g1=red[0:1], gm=red[1:2], g2=red[2:3], gf=red[3], ng=red[4:5, :512], gq=red[5:6, :HEAD_DIM],
              gk=red[6:7, :HEAD_DIM])
    dlog_full = red[7:9].reshape(2, 2, 512)
    sg["lb"] = lax.dynamic_slice_in_dim(dlog_full, me * LANES, LANES, axis=2)

    small_w = dict(g1=ffn1_norm_g, gm=mix_norm_g, g2=ffn2_norm_g, gf=final_norm_g, ng=hgrn_out_norm_g, gq=q_norm_g,
                   gk=k_norm_g, lb=hgrn_lb_logits)
    small_m = dict(g1=m_ffn1_norm_g, gm=m_mix_norm_g, g2=m_ffn2_norm_g, gf=m_final_norm_g, ng=m_hgrn_out_norm_g,
                   gq=m_q_norm_g, gk=m_k_norm_g, lb=m_hgrn_lb_logits)
    small_v = dict(g1=v_ffn1_norm_g, gm=v_mix_norm_g, g2=v_ffn2_norm_g, gf=v_final_norm_g, ng=v_hgrn_out_norm_g,
                   gq=v_q_norm_g, gk=v_k_norm_g, lb=v_hgrn_lb_logits)
    small_names = ("g1", "gm", "g2", "gf", "ng", "gq", "gk", "lb")
    pack = lambda dct: _pack_rows([dct[n] for n in small_names], D_MODEL)
    pw, pgr, pm, pv = pack(small_w), pack(sg), pack(small_m), pack(small_v)
    pad8 = lambda a: jnp.pad(a, ((0, (-a.shape[0]) % 8), (0, 0)))
    sd, sm_, sv_ = _adamw(pad8(pw), pad8(pgr), pad8(pm), pad8(pv))

    def unpack(buf):
        out, r = {}, 0
        for n in small_names:
            size = small_w[n].size
            nr = -(-size // D_MODEL)
            out[n] = buf[r:r + nr].reshape(-1)[:size].reshape(small_w[n].shape)
            r += nr
        return out

    sdelta, snew_m, snew_v = unpack(sd), unpack(sm_), unpack(sv_)
    sgrad = {n: sg[n].reshape(small_w[n].shape) for n in small_names}

    bdelta, bnew_m, bnew_v, bgrad = {}, {}, {}, {}

    def update(names, halves):
        for n, own, got in zip(names, halves, _pair_share(halves)):
            res = _adamw_halves(big_w[n], own, got, big_m[n], big_v[n], c_idx)
            if n in TRANSPOSED:
                res = [r.T for r in res]
            bdelta[n], bnew_m[n], bnew_v[n], bgrad[n] = [r[None] for r in res]

    names3, halves3 = reduced_halves(3, (pending[1][-1],))
    names2, halves2 = reduced_halves(2, (halves3[0],))
    update(names3 + names2, halves3 + halves2)
    names1, halves1 = reduced_halves(1, (bdelta[names2[-1]],))
    update(names1, halves1)

    order = [("s", "g1"), ("b", "g1"), ("b", "u1"), ("b", "d1"), ("s", "gm"), ("b", "in"), ("s", "gq"), ("s", "gk"),
             ("s", "lb"), ("s", "ng"), ("b", "a"), ("b", "b"), ("b", "o"), ("s", "g2"), ("b", "g2"), ("b", "u2"),
             ("b", "d2"), ("s", "gf")]
    outs = [loss_out, dx[None]]
    for table_s, table_b in ((sgrad, bgrad), (sdelta, bdelta), (snew_m, bnew_m), (snew_v, bnew_v)):
        outs += [(table_s if kind == "s" else table_b)[n] for kind, n in order]
    return tuple(outs)
```

```python
import functools

import numpy as np
import jax
import jax.numpy as jnp
from jax import lax
from jax.experimental import pallas as pl
from jax.experimental.pallas import tpu as pltpu

F32 = jnp.float32
BF16 = jnp.bfloat16
CD = jnp.bfloat16

EPS = 1e-6
D_MODEL = 1024
HEAD_DIM = 64
GRID_W = 64
ROPE_THETA = 10000.0
CHUNK = 32
N_SHARD = 4
N_DEV = 8
VMEM_LIMIT = 56 * 1024 * 1024
LANES = 128
HG_TILE = 256
FFN_ROWS = 256

ADAM_LR = 0.001
ADAM_B1 = 0.9
ADAM_B2 = 0.999
ADAM_EPS = 1e-08
ADAM_WD = 0.01
ADAM_STEP = 10

NN = (((1,), (0,)), ((), ()))
NT = (((1,), (1,)), ((), ()))
TN = (((0,), (0,)), ((), ()))
MESH = pl.DeviceIdType.MESH
ANY = pl.BlockSpec(memory_space=pl.ANY)


def _mm(a, b, dn):
    return lax.dot_general(a.astype(CD), b.astype(CD), dn, preferred_element_type=F32)


def _split3(x):
    hi = x.astype(BF16)
    r = x - hi.astype(F32)
    mid = r.astype(BF16)
    lo = (r - mid.astype(F32)).astype(BF16)
    return hi, mid, lo


def _xdot(x, m):
    rows = x.shape[0]
    hi, mid, _ = _split3(x)
    r = lax.dot_general(jnp.concatenate([hi, mid], axis=0), m, NN, preferred_element_type=F32)
    return r[:rows] + r[rows:]


def _xdot_l(m, x):
    cols = x.shape[1]
    hi, mid, _ = _split3(x)
    r = lax.dot_general(m, jnp.concatenate([hi, mid], axis=1), NN, preferred_element_type=F32)
    return r[:, :cols] + r[:, cols:]


def _params(n_grid):
    return pltpu.CompilerParams(dimension_semantics=("arbitrary",) * n_grid, vmem_limit_bytes=VMEM_LIMIT)


def _sigmoid(x):
    return jax.nn.sigmoid(x)


def _np_blocksum(n):
    i = np.arange(n)
    return (i[:, None] // HEAD_DIM == i[None, :] // HEAD_DIM).astype(np.float32)


def _np_swap32(n):
    i = np.arange(n)
    partner = np.where(i % HEAD_DIM < HEAD_DIM // 2, i + HEAD_DIM // 2, i - HEAD_DIM // 2)
    m = np.zeros((n, n), np.float32)
    m[i, partner] = 1.0
    return m


def _np_expand_q():
    m = np.zeros((512, 1024), np.float32)
    for h in range(8):
        g = h // 4
        for d in range(HEAD_DIM):
            m[64 * h + d, 128 * h + 64 * g + d] = 1.0
    return m


def _np_headsum_spread():
    m = np.zeros((512, 1024), np.float32)
    for h in range(8):
        m[64 * h:64 * h + 64, 128 * h:128 * h + 128] = 1.0
    return m


def _np_swap_halves():
    m = np.zeros((128, 128), np.float32)
    i = np.arange(128)
    m[i, (i + 64) % 128] = 1.0
    return m


def _np_hgrn_cums(t, rev):
    r = np.arange(t)[:, None]
    c = np.arange(t)[None, :]
    same = (r // CHUNK) == (c // CHUNK)
    if not rev:
        cum = same & (c <= r)
        mid = same & (c % CHUNK <= CHUNK // 2 - 1)
    else:
        cum = same & (c >= r)
        mid = same & (c % CHUNK >= CHUNK // 2)
    return np.concatenate([cum, mid, same], axis=0).astype(np.float32)


def _bf(a):
    return jnp.asarray(a, dtype=BF16)


def _rope_tables(seq_len):
    rows = seq_len // GRID_W
    row = jnp.repeat(jnp.arange(rows, dtype=F32), GRID_W)
    col = jnp.tile(jnp.arange(GRID_W, dtype=F32), rows)
    n_freq = HEAD_DIM // 4
    inv = ROPE_THETA ** (-jnp.arange(n_freq, dtype=F32) / n_freq)
    ang = jnp.concatenate([row[:, None] * inv, col[:, None] * inv], axis=-1)
    cos, sin = jnp.cos(ang), jnp.sin(ang)
    c64 = jnp.concatenate([cos, cos], axis=-1)
    s64 = jnp.concatenate([-sin, sin], axis=-1)
    return jnp.tile(c64, (1, 2)), jnp.tile(s64, (1, 2))


def _ffn_fwd(x, g, wg, wu, wd, tm):
    s, d = x.shape
    nsh, fs, _ = wg.shape

    def body(x_ref, g_ref, wg_ref, wu_ref, wd_ref, xo_ref, a_ref, da_ref, b_ref, hb_ref, acc, hs):
        j = pl.program_id(1)

        @pl.when(j == 0)
        def _():
            xv = x_ref[...]
            r = lax.rsqrt(jnp.mean(xv * xv, axis=-1, keepdims=True) + EPS)
            h = (xv * r * g_ref[...]).astype(CD)
            hs[...] = h
            hb_ref[...] = h
            acc[...] = jnp.zeros_like(acc)

        for r0 in range(0, tm, FFN_ROWS):
            rows = slice(r0, min(r0 + FFN_ROWS, tm))
            h = hs[rows, :]
            a = _mm(h, wg_ref[0], NT)
            b = _mm(h, wu_ref[0], NT)
            sg = _sigmoid(a)
            silu = a * sg
            acc[rows, :] += _mm(silu * b, wd_ref[0], NN)
            a_ref[0, rows, :] = silu.astype(CD)
            da_ref[0, rows, :] = (sg * (1.0 + a * (1.0 - sg))).astype(CD)
            b_ref[0, rows, :] = b.astype(CD)

        @pl.when(j == nsh - 1)
        def _():
            xo_ref[...] = x_ref[...] + 0.5 * acc[...]

    return pl.pallas_call(
        body, name="ffn_fwd", grid=(s // tm, nsh),
        in_specs=[pl.BlockSpec((tm, d), lambda i, j: (i, 0)), pl.BlockSpec((1, d), lambda i, j: (0, 0))]
        + [pl.BlockSpec((1, fs, d), lambda i, j: (j, 0, 0))] * 3,
        out_specs=[pl.BlockSpec((tm, d), lambda i, j: (i, 0))] + [pl.BlockSpec((1, tm, fs), lambda i, j: (j, i, 0))] * 3
        + [pl.BlockSpec((tm, d), lambda i, j: (i, 0))],
        out_shape=[jax.ShapeDtypeStruct((s, d), F32)] + [jax.ShapeDtypeStruct((nsh, s, fs), CD)] * 3
        + [jax.ShapeDtypeStruct((s, d), CD)],
        scratch_shapes=[pltpu.VMEM((tm, d), F32), pltpu.VMEM((tm, d), CD)],
        compiler_params=_params(2),
    )(x, g, wg, wu, wd)


def _ffn_bwd(dout, x, g, silu, dsilu, b, wg, wu, wd, tm):
    s, d = x.shape
    nsh, fs, _ = wg.shape

    def body(do_ref, x_ref, g_ref, sl_ref, ds_ref, b_ref, wg_ref, wu_ref, wd_ref,
             dx_ref, da_ref, db_ref, f_ref, dg_ref, do16_ref, dh):
        i = pl.program_id(0)
        j = pl.program_id(1)

        @pl.when(j == 0)
        def _():
            dh[...] = jnp.zeros_like(dh)
            do16_ref[...] = do_ref[...].astype(CD)

        @pl.when((i == 0) & (j == 0))
        def _():
            dg_ref[...] = jnp.zeros_like(dg_ref)

        for r0 in range(0, tm, FFN_ROWS):
            rows = slice(r0, min(r0 + FFN_ROWS, tm))
            sl = sl_ref[0, rows, :].astype(F32)
            bv = b_ref[0, rows, :].astype(F32)
            df = 0.5 * _mm(do_ref[rows, :], wd_ref[0], NT)
            da = df * bv * ds_ref[0, rows, :].astype(F32)
            db = df * sl
            dh[rows, :] += _mm(da, wg_ref[0], NN) + _mm(db, wu_ref[0], NN)
            da_ref[0, rows, :] = da.astype(CD)
            db_ref[0, rows, :] = db.astype(CD)
            f_ref[0, rows, :] = (sl * bv).astype(CD)

        @pl.when(j == nsh - 1)
        def _():
            xv = x_ref[...]
            r = lax.rsqrt(jnp.mean(xv * xv, axis=-1, keepdims=True) + EPS)
            dhv = dh[...]
            u = dhv * g_ref[...]
            dx_ref[...] = do_ref[...] + r * u - xv * (r * r * r) * jnp.mean(u * xv, axis=-1, keepdims=True)
            dg_ref[...] += jnp.sum(dhv * xv * r, axis=0, keepdims=True)

    act = pl.BlockSpec((1, tm, fs), lambda i, j: (j, i, 0))
    row = pl.BlockSpec((tm, d), lambda i, j: (i, 0))
    return pl.pallas_call(
        body, name="ffn_bwd", grid=(s // tm, nsh),
        in_specs=[row, row, pl.BlockSpec((1, d), lambda i, j: (0, 0)), act, act, act]
        + [pl.BlockSpec((1, fs, d), lambda i, j: (j, 0, 0))] * 3,
        out_specs=[row, act, act, act, pl.BlockSpec((1, d), lambda i, j: (0, 0)), row],
        out_shape=[jax.ShapeDtypeStruct((s, d), F32), jax.ShapeDtypeStruct((nsh, s, fs), CD),
                   jax.ShapeDtypeStruct((nsh, s, fs), CD), jax.ShapeDtypeStruct((nsh, s, fs), CD),
                   jax.ShapeDtypeStruct((1, d), F32), jax.ShapeDtypeStruct((s, d), CD)],
        scratch_shapes=[pltpu.VMEM((tm, d), F32)],
        compiler_params=_params(2),
    )(dout, x, g, silu, dsilu, b, wg, wu, wd)


def _tn_call(name, operands, in_specs, out_shape, out_spec, grid, acc_shape, pick, scale=1.0):
    nk = grid[-1]
    n_in = len(operands)

    def body(*refs):
        out_ref, out16_ref, acc = refs[n_in], refs[n_in + 1], refs[n_in + 2]
        k = pl.program_id(len(grid) - 1)

        @pl.when(k == 0)
        def _():
            acc[...] = jnp.zeros_like(acc)

        pick(refs[:n_in], acc)

        @pl.when(k == nk - 1)
        def _():
            res = (acc[...] if scale == 1.0 else acc[...] * scale).reshape(out_ref.shape)
            out_ref[...] = res
            out16_ref[...] = res.astype(CD)

    return pl.pallas_call(
        body, name=name, grid=grid, in_specs=in_specs, out_specs=[out_spec, out_spec],
        out_shape=[out_shape, jax.ShapeDtypeStruct(out_shape.shape, CD)],
        scratch_shapes=[pltpu.VMEM(acc_shape, F32)], compiler_params=_params(len(grid)),
    )(*operands)


def _dw_shared_b(name, a3, b, tk, scale):
    nj, s, m = a3.shape
    n = b.shape[1]

    def pick(refs, acc):
        rows = pl.ds(pl.multiple_of(pl.program_id(1) * tk, tk), tk)
        acc[...] += _mm(refs[0][0], refs[1][rows, :], TN)

    return _tn_call(name, (a3, b),
                    [pl.BlockSpec((1, tk, m), lambda j, k: (j, k, 0)), pl.BlockSpec((s, n), lambda j, k: (0, 0))],
                    jax.ShapeDtypeStruct((nj, m, n), F32), pl.BlockSpec((1, m, n), lambda j, k: (j, 0, 0)),
                    (nj, s // tk), (m, n), pick, scale)


def _dw_colblocks(name, a, b, nj, tk):
    s, m = a.shape
    n = b.shape[1] // nj

    def pick(refs, acc):
        acc[...] += _mm(refs[0][...], refs[1][...], TN)

    return _tn_call(name, (a, b),
                    [pl.BlockSpec((tk, m), lambda j, k: (k, 0)), pl.BlockSpec((tk, n), lambda j, k: (k, j))],
                    jax.ShapeDtypeStruct((nj, m, n), F32), pl.BlockSpec((1, m, n), lambda j, k: (j, 0, 0)),
                    (nj, s // tk), (m, n), pick)


DP_WIDTHS = (768, 512, 512, 512, 512, 512, 2048)
DP_CHUNK = 256


def _dp_chunk_maps():
    starts, counts, off = [], [], 0
    for w in DP_WIDTHS:
        starts.append(off // DP_CHUNK)
        counts.append(w // DP_CHUNK)
        off += w
    return starts, counts


def _dp_specs(tm, row_axis, chunk_axis):
    starts, counts = _dp_chunk_maps()
    specs = []
    for st, cnt in zip(starts, counts):
        def imap(*ids, st=st, cnt=cnt):
            return (ids[row_axis], jnp.clip(ids[chunk_axis] - st, 0, cnt - 1))
        specs.append(pl.BlockSpec((tm, DP_CHUNK), imap))
    return specs


def _dp_select(n, refs, fn):
    starts, counts = _dp_chunk_maps()
    for ref, st, cnt in zip(refs, starts, counts):
        @pl.when((n >= st) & (n < st + cnt))
        def _(ref=ref):
            fn(ref)


def _dw_in(dps, hb, tk):
    s, d = hb.shape
    n_chunks = sum(DP_WIDTHS) // DP_CHUNK

    def pick(refs, acc):
        rows = pl.ds(pl.multiple_of(pl.program_id(1) * tk, tk), tk)

        def add(ref):
            acc[...] += _mm(ref[...], refs[7][rows, :], TN)

        _dp_select(pl.program_id(0), refs[:7], add)

    return _tn_call("dw_in", (*dps, hb),
                    _dp_specs(tk, 1, 0) + [pl.BlockSpec((s, d), lambda n, k: (0, 0))],
                    jax.ShapeDtypeStruct((n_chunks * DP_CHUNK, d), F32), pl.BlockSpec((DP_CHUNK, d), lambda n, k: (n, 0)),
                    (n_chunks, s // tk), (DP_CHUNK, d), pick)


def _mix_in_fwd(x, g, w_t, tm):
    s, d = x.shape
    n_in = w_t.shape[0]

    def body(x_ref, g_ref, w_ref, qkv_ref, hg_ref, gt_ref, hb_ref):
        xv = x_ref[...]
        r = lax.rsqrt(jnp.mean(xv * xv, axis=-1, keepdims=True) + EPS)
        h = (xv * r * g_ref[...]).astype(CD)
        hb_ref[...] = h
        off = DP_WIDTHS[0]
        qkv_ref[...] = _mm(h, w_ref[0:off, :], NT)
        for c, width in enumerate(DP_WIDTHS[1:6]):
            hg_ref[:, c * width:(c + 1) * width] = _mm(h, w_ref[off:off + width, :], NT)
            off += width
        gate = DP_WIDTHS[6] // 2
        for c in range(2):
            gt_ref[:, c * gate:(c + 1) * gate] = _mm(h, w_ref[off:off + gate, :], NT)
            off += gate

    row = lambda w: pl.BlockSpec((tm, w), lambda i: (i, 0))
    return pl.pallas_call(
        body, name="mix_in_fwd", grid=(s // tm,),
        in_specs=[row(d), pl.BlockSpec((1, d), lambda i: (0, 0)), pl.BlockSpec((n_in, d), lambda i: (0, 0))],
        out_specs=[row(768), row(2560), row(2048), row(d)],
        out_shape=[jax.ShapeDtypeStruct((s, 768), F32), jax.ShapeDtypeStruct((s, 2560), F32),
                   jax.ShapeDtypeStruct((s, 2048), F32), jax.ShapeDtypeStruct((s, d), CD)],
        compiler_params=_params(1),
    )(x, g, w_t)


def _mix_in_bwd(dps, w_t, x, dres, g, tm, after=()):
    s, d = x.shape
    n_in = w_t.shape[0]

    def body(*refs):
        refs = refs[len(after):]
        dp_refs = refs[:7]
        w_ref, x_ref, dr_ref, g_ref, dx_ref, dg_ref = refs[7:]

        @pl.when(pl.program_id(0) == 0)
        def _():
            dg_ref[...] = jnp.zeros_like(dg_ref)

        dhv = jnp.zeros((tm, d), F32)
        off = 0
        for ref, width in zip(dp_refs, DP_WIDTHS):
            dhv = dhv + _mm(ref[...], w_ref[off:off + width, :], NN)
            off += width
        xv = x_ref[...]
        r = lax.rsqrt(jnp.mean(xv * xv, axis=-1, keepdims=True) + EPS)
        u = dhv * g_ref[...]
        dx_ref[...] = dr_ref[...] + r * u - xv * (r * r * r) * jnp.mean(u * xv, axis=-1, keepdims=True)
        dg_ref[...] += jnp.sum(dhv * xv * r, axis=0, keepdims=True)

    row = pl.BlockSpec((tm, d), lambda i: (i, 0))
    vec = pl.BlockSpec((1, d), lambda i: (0, 0))
    return pl.pallas_call(
        body, name="mix_in_bwd", grid=(s // tm,),
        in_specs=[ANY] * len(after) + [pl.BlockSpec((tm, w), lambda i: (i, 0)) for w in DP_WIDTHS]
        + [pl.BlockSpec((n_in, d), lambda i: (0, 0)), row, row, vec],
        out_specs=[row, vec],
        out_shape=[jax.ShapeDtypeStruct((s, d), F32), jax.ShapeDtypeStruct((1, d), F32)],
        compiler_params=_params(1),
    )(*after, *dps, w_t, x, dres, g)


def _headnorm_rope(x, gain, cos, sin, blocksum, swap):
    ss = _xdot(x * x, blocksum)
    r = lax.rsqrt(ss * (1.0 / HEAD_DIM) + EPS)
    y = x * r * gain
    return y * cos + _xdot(y, swap) * sin, r


def _headnorm_rope_bwd(dz, x, gain, cos, sin, blocksum, swap):
    ss = _xdot(x * x, blocksum)
    r = lax.rsqrt(ss * (1.0 / HEAD_DIM) + EPS)
    dy = dz * cos + _xdot(dz * sin, swap)
    u = dy * gain
    mean_ux = _xdot(u * x, blocksum) * (1.0 / HEAD_DIM)
    dx = r * u - x * (r * r * r) * mean_ux
    return dx, jnp.sum(dy * x * r, axis=0, keepdims=True)


def _qk_prep(pqkv, gq, gk, cos2, sin2, tm):
    s = pqkv.shape[0]
    bs512, sw512, eq, swh = _bf(_np_blocksum(512)), _bf(_np_swap32(512)), _bf(_np_expand_q()), _bf(_np_swap_halves())

    def body(q_ref, kv_ref, gq_ref, gk_ref, c_ref, s_ref, bs_ref, sw_ref, eq_ref, swh_ref, qe_ref, k_ref, v_ref, vs_ref):
        c2, s2 = c_ref[...], s_ref[...]
        c8, s8 = jnp.tile(c2, (1, 4)), jnp.tile(s2, (1, 4))
        bs, sw = bs_ref[...], sw_ref[...]
        zq, _ = _headnorm_rope(q_ref[...], gq_ref[...], c8, s8, bs, sw)
        qe_ref[...] = _mm(zq * (HEAD_DIM ** -0.5), eq_ref[...], NN).astype(CD)
        kv = kv_ref[...]
        zk, _ = _headnorm_rope(kv[:, :LANES], gk_ref[...], c2, s2, bs[:LANES, :LANES], sw[:LANES, :LANES])
        k_ref[...] = zk.astype(CD)
        v = kv[:, LANES:]
        v_ref[...] = v.astype(CD)
        vs_ref[...] = _mm(v, swh_ref[...], NN).astype(CD)

    full = lambda a: pl.BlockSpec(a.shape, lambda i: (0,) * a.ndim)
    tab = pl.BlockSpec((tm, LANES), lambda i: (i, 0))
    return pl.pallas_call(
        body, name="qk_prep", grid=(s // tm,),
        in_specs=[pl.BlockSpec((tm, 512), lambda i: (i, 0)), pl.BlockSpec((tm, 256), lambda i: (i, 2)),
                  full(gq), full(gk), tab, tab, full(bs512), full(sw512), full(eq), full(swh)],
        out_specs=[pl.BlockSpec((tm, 1024), lambda i: (i, 0)), tab, tab, tab],
        out_shape=[jax.ShapeDtypeStruct((s, 1024), CD)] + [jax.ShapeDtypeStruct((s, LANES), CD)] * 3,
        compiler_params=_params(1),
    )(pqkv, pqkv, gq, gk, cos2, sin2, bs512, sw512, eq, swh)


def _qk_prep_bwd(pqkv, dq, dk, dv, gq, gk, cos2, sin2, tm):
    s = pqkv.shape[0]
    bs512, sw512 = _bf(_np_blocksum(512)), _bf(_np_swap32(512))

    def body(q_ref, kv_ref, dq_ref, dk_ref, dv_ref, gq_ref, gk_ref, c_ref, s_ref, bs_ref, sw_ref,
             dp_ref, dgq_ref, dgk_ref):
        @pl.when(pl.program_id(0) == 0)
        def _():
            dgq_ref[...] = jnp.zeros_like(dgq_ref)
            dgk_ref[...] = jnp.zeros_like(dgk_ref)

        c2, s2 = c_ref[...], s_ref[...]
        c8, s8 = jnp.tile(c2, (1, 4)), jnp.tile(s2, (1, 4))
        bs, sw = bs_ref[...], sw_ref[...]
        dzq = dq_ref[...] * (HEAD_DIM ** -0.5)
        dxq, dgq = _headnorm_rope_bwd(dzq, q_ref[...], gq_ref[...], c8, s8, bs, sw)
        kv = kv_ref[...]
        dxk, dgk = _headnorm_rope_bwd(dk_ref[...], kv[:, :LANES], gk_ref[...], c2, s2, bs[:LANES, :LANES], sw[:LANES, :LANES])
        dp_ref[...] = jnp.concatenate([dxq, dxk, dv_ref[...]], axis=1).astype(CD)
        dgq_ref[...] += dgq
        dgk_ref[...] += dgk

    full = lambda a: pl.BlockSpec(a.shape, lambda i: (0,) * a.ndim)
    tab = pl.BlockSpec((tm, LANES), lambda i: (i, 0))
    return pl.pallas_call(
        body, name="qk_prep_bwd", grid=(s // tm,),
        in_specs=[pl.BlockSpec((tm, 512), lambda i: (i, 0)), pl.BlockSpec((tm, 256), lambda i: (i, 2)),
                  pl.BlockSpec((tm, 512), lambda i: (i, 0)), tab, tab, full(gq), full(gk), tab, tab,
                  full(bs512), full(sw512)],
        out_specs=[pl.BlockSpec((tm, 768), lambda i: (i, 0)), pl.BlockSpec((1, 512), lambda i: (0, 0)),
                   pl.BlockSpec((1, LANES), lambda i: (0, 0))],
        out_shape=[jax.ShapeDtypeStruct((s, 768), CD), jax.ShapeDtypeStruct((1, 512), F32),
                   jax.ShapeDtypeStruct((1, LANES), F32)],
        compiler_params=_params(1),
    )(pqkv, pqkv, dq, dk, dv, gq, gk, cos2, sin2, bs512, sw512)


def _kv_rows(h):
    return pl.ds(pl.multiple_of((h // 4) * HEAD_DIM, HEAD_DIM), HEAD_DIM)


def _attn_fwd(qe, k, v, vs, tq):
    s = k.shape[0]

    def body(qa_ref, qb_ref, k_ref, v_ref, vs_ref, o_ref, lse_ref):
        m = pl.program_id(0)
        grp = m // 2
        kk = k_ref[...]
        outs = []
        for idx, q_ref in enumerate((qa_ref, qb_ref)):
            sc = _mm(q_ref[...], kk, NT)
            mx = jnp.max(sc, axis=-1, keepdims=True)
            e = jnp.exp(sc - mx)
            l = jnp.sum(e, axis=-1, keepdims=True)
            lse_ref[idx] = mx + jnp.log(l)
            vsel = jnp.where(grp != idx, vs_ref[...], v_ref[...])
            outs.append(_mm(e, vsel, NN) * (1.0 / l))
        lane = lax.broadcasted_iota(jnp.int32, (1, LANES), 1)
        o_ref[...] = jnp.where(lane < HEAD_DIM, outs[0], outs[1])

    kv = pl.BlockSpec((s, LANES), lambda m, i: (0, 0))
    return pl.pallas_call(
        body, name="attn_fwd", grid=(4, s // tq),
        in_specs=[pl.BlockSpec((tq, LANES), lambda m, i: (i, 2 * m)), pl.BlockSpec((tq, LANES), lambda m, i: (i, 2 * m + 1)),
                  kv, kv, kv],
        out_specs=[pl.BlockSpec((tq, LANES), lambda m, i: (i, m)), pl.BlockSpec((2, tq, 1), lambda m, i: (m, i, 0))],
        out_shape=[jax.ShapeDtypeStruct((s, 512), F32), jax.ShapeDtypeStruct((8, s, 1), F32)],
        compiler_params=_params(2),
    )(qe, qe, k, v, vs)


def _attn_bwd(qe, k, kt, v, doe, delta, lse, tq):
    s = k.shape[0]

    def body(qa_ref, qb_ref, k_ref, kt_ref, v_ref, doa_ref, dob_ref, dla_ref, dlb_ref, lse_ref,
             dqt_ref, dkt_ref, dvt_ref, qt, dot):
        @pl.when((pl.program_id(0) == 0) & (pl.program_id(1) == 0))
        def _():
            dkt_ref[...] = jnp.zeros_like(dkt_ref)
            dvt_ref[...] = jnp.zeros_like(dvt_ref)

        rows = _kv_rows(2 * pl.program_id(0))
        kt = kt_ref[rows, :]
        dkt = jnp.zeros((HEAD_DIM, s), F32)
        dvt = jnp.zeros((HEAD_DIM, s), F32)
        for idx, (q_ref, do_ref, dl_ref) in enumerate(((qa_ref, doa_ref, dla_ref), (qb_ref, dob_ref, dlb_ref))):
            q, do = q_ref[...], do_ref[...]
            p = jnp.exp(_mm(q, k_ref[...], NT) - lse_ref[idx])
            dp = _mm(do, v_ref[...], NT)
            ds = (p * (dp - jnp.max(dl_ref[...], axis=-1, keepdims=True))).astype(CD)
            p = p.astype(CD)
            dqt_ref[idx * HEAD_DIM:(idx + 1) * HEAD_DIM, :] = _mm(kt, ds, NT)
            qt[idx] = jnp.transpose(q.astype(F32))
            dot[idx] = jnp.transpose(do.astype(F32))
            dkt = dkt + _mm(qt[idx, rows, :], ds, NN)
            dvt = dvt + _mm(dot[idx, rows, :], p, NN)
        dkt_ref[rows, :] += dkt
        dvt_ref[rows, :] += dvt

    kv = pl.BlockSpec((s, LANES), lambda m, i: (0, 0))
    kvt = pl.BlockSpec((LANES, s), lambda m, i: (0, 0))
    blk_a = pl.BlockSpec((tq, LANES), lambda m, i: (i, 2 * m))
    blk_b = pl.BlockSpec((tq, LANES), lambda m, i: (i, 2 * m + 1))
    return pl.pallas_call(
        body, name="attn_bwd", grid=(4, s // tq),
        in_specs=[blk_a, blk_b, kv, kvt, kv, blk_a, blk_b, blk_a, blk_b, pl.BlockSpec((2, tq, 1), lambda m, i: (m, i, 0))],
        out_specs=[pl.BlockSpec((LANES, tq), lambda m, i: (m, i)), kvt, kvt],
        out_shape=[jax.ShapeDtypeStruct((8 * HEAD_DIM, s), F32), jax.ShapeDtypeStruct((LANES, s), F32),
                   jax.ShapeDtypeStruct((LANES, s), F32)],
        scratch_shapes=[pltpu.VMEM((2, LANES, tq), F32), pltpu.VMEM((2, LANES, tq), F32)],
        compiler_params=_params(2),
    )(qe, qe, k, kt, v, doe, doe, delta, delta, lse)


@jax.custom_vjp
def _mm_nn(a, b):
    return _mm(a, b, NN)


_mm_nn.defvjp(lambda a, b: (_mm(a, b, NN), (a, b)),
              lambda res, g: (_mm(g, res[1], NT), _mm(res[0], g, TN)))


@jax.custom_vjp
def _mm_nt(a, b):
    return _mm(a, b, NT)


_mm_nt.defvjp(lambda a, b: (_mm(a, b, NT), (a, b)),
              lambda res, g: (_mm(g, res[1], NN), _mm(g, res[0], TN)))


@jax.custom_vjp
def _mm_tn(a, b):
    return _mm(a, b, TN)


_mm_tn.defvjp(lambda a, b: (_mm(a, b, TN), (a, b)),
              lambda res, g: (_mm(res[1], g, NT), _mm(res[0], g, NN)))


@jax.custom_vjp
def _cmm(m, mt, x):
    return _xdot_l(m, x)


_cmm.defvjp(lambda m, mt, x: (_xdot_l(m, x), (m, mt)),
            lambda res, g: (jnp.zeros_like(res[0]), jnp.zeros_like(res[1]), _xdot_l(res[1], g)))


def _hgrn_masks(t, rev):
    n_ch = t // CHUNK
    r = jnp.bitwise_and(lax.broadcasted_iota(jnp.int32, (2 * t, t), 0), t - 1)
    c = lax.broadcasted_iota(jnp.int32, (2 * t, t), 1)
    same = jnp.right_shift(r, 5) == jnp.right_shift(c, 5)
    tri2 = same & ((c >= r) if rev else (c <= r))
    pr = lax.broadcasted_iota(jnp.int32, (LANES, LANES), 0)
    pc = lax.broadcasted_iota(jnp.int32, (LANES, LANES), 1)
    diag = jnp.right_shift(pr, 6) == jnp.right_shift(pc, 6)
    qr = lax.broadcasted_iota(jnp.int32, (t, n_ch * LANES), 0)
    qc = lax.broadcasted_iota(jnp.int32, (t, n_ch * LANES), 1)
    rows_chunk = jnp.right_shift(qc, 7) == jnp.right_shift(qr, 5)
    vr = lax.broadcasted_iota(jnp.int32, (n_ch * LANES, t), 0)
    vc = lax.broadcasted_iota(jnp.int32, (n_ch * LANES, t), 1)
    cols_chunk = jnp.right_shift(vr, 7) == jnp.right_shift(vc, 5)
    return dict(tri2=tri2, diag=diag, rows_chunk=rows_chunk, cols_chunk=cols_chunk)


def _hgrn_gates(xf, lb):
    f = lb + (1.0 - lb) * _sigmoid(xf)
    return 1.0 - f, jnp.log(f)


def _hgrn_dir(xq, xf, v, lb, state, cm, cmt, mk, rev):
    t = xq.shape[0]
    n_ch = t // CHUNK
    lo = lax.broadcasted_iota(jnp.int32, (1, LANES), 1) < HEAD_DIM
    q = xq * _sigmoid(xq)
    k, lf = _hgrn_gates(xf, lb)
    cs = _cmm(cm, cmt, lf)
    b, bm, bl = cs[:t], cs[t:2 * t], cs[2 * t:]
    qd = q * jnp.exp(b - bm)
    kd = k * jnp.exp(bm - b)
    kc = k * jnp.exp(bl - b)
    qe = q * jnp.exp(b)
    qd2 = jnp.concatenate([jnp.where(lo, qd, 0.0), jnp.where(lo, 0.0, qd)], axis=0)
    o2 = _mm_nn(jnp.where(mk["tri2"], _mm_nt(qd2, kd), 0.0), v)
    o = jnp.where(lo, o2[:t], o2[t:])
    vexp = jnp.where(mk["cols_chunk"], jnp.concatenate([jnp.transpose(v)] * n_ch, axis=0), 0.0)
    adds = _mm_nn(vexp, kc)
    dec = jnp.exp(bl)
    entering = [None] * n_ch
    for c in (range(n_ch - 1, -1, -1) if rev else range(n_ch)):
        entering[c] = state
        d = jnp.concatenate([dec[c * CHUNK:(c + 1) * CHUNK]] * (LANES // CHUNK), axis=0)
        state = d * state + jnp.where(mk["diag"], adds[c * LANES:(c + 1) * LANES], 0.0)
    qexp = jnp.where(mk["rows_chunk"], jnp.concatenate([qe] * n_ch, axis=1), 0.0)
    return o + _mm_nt(qexp, jnp.concatenate(entering, axis=1)), state


def _hgrn_lower_bounds(l):
    out = []
    for d in (0, 1):
        l0, l1 = l[2 * d:2 * d + 1, :], l[2 * d + 1:2 * d + 2, :]
        mx = jnp.maximum(l0, l1)
        e0, e1 = jnp.exp(l0 - mx), jnp.exp(l1 - mx)
        out.append(e0 / (e0 + e1))
    return out


def _hgrn_consts(t):
    cf, cb = _np_hgrn_cums(t, False), _np_hgrn_cums(t, True)
    return (_bf(cf), _bf(cf.T), _bf(cb), _bf(cb.T), _bf(_np_blocksum(LANES)))


def _hgrn_fwd(ph, lbl, ng):
    s = ph.shape[0]
    t = min(HG_TILE, s)
    nt = s // t
    consts = _hgrn_consts(t)

    def body(xq_ref, xff_ref, xfb_ref, xi_ref, xg_ref, lbl_ref, ng_ref, cf_ref, cft_ref, cb_ref, cbt_ref, bs_ref,
             o_ref, pre_ref, st_ref, acc):
        lbf, lbb = _hgrn_lower_bounds(lbl_ref)
        mk_f, mk_b = _hgrn_masks(t, False), _hgrn_masks(t, True)
        zero = jnp.zeros((LANES, LANES), F32)

        def rows_of(i):
            return pl.ds(pl.multiple_of(i * t, t), t)

        acc[...] = jnp.zeros_like(acc)

        def step(i, states):
            tb = nt - 1 - i
            rf, rb = rows_of(i), rows_of(tb)
            st_ref[0, 0, i] = states[0]
            st_ref[0, 1, tb] = states[1]
            of, sf = _hgrn_dir(xq_ref[rf, :], xff_ref[rf, :], xi_ref[rf, :], lbf, states[0],
                               cf_ref[...], cft_ref[...], mk_f, False)
            ob, sb = _hgrn_dir(xq_ref[rb, :], xfb_ref[rb, :], xi_ref[rb, :], lbb, states[1],
                               cb_ref[...], cbt_ref[...], mk_b, True)
            acc[rf, :] += of
            acc[rb, :] += ob
            return sf, sb

        lax.fori_loop(0, nt, step, (zero, zero))

        def step_n(i, carry):
            rows = rows_of(i)
            o = acc[rows, :]
            ss = _xdot(o * o, bs_ref[...])
            r = lax.rsqrt(ss * (1.0 / HEAD_DIM) + EPS)
            xg = xg_ref[rows, :]
            pre_ref[rows, :] = o
            o_ref[rows, :] = (o * r * ng_ref[...]) * (xg * _sigmoid(xg))
            return carry

        lax.fori_loop(0, nt, step_n, 0)

    col = lambda off: pl.BlockSpec((s, LANES), lambda m: (0, off + m))
    full = lambda a: pl.BlockSpec(a.shape, lambda m: (0,) * a.ndim)
    return pl.pallas_call(
        body, name="hgrn_fwd", grid=(4,),
        in_specs=[col(0), col(4), col(8), col(12), col(16), pl.BlockSpec((4, LANES), lambda m: (0, m)),
                  pl.BlockSpec((1, LANES), lambda m: (0, m))] + [full(c) for c in consts],
        out_specs=[col(0), col(0), pl.BlockSpec((1, 2, nt, LANES, LANES), lambda m: (m, 0, 0, 0, 0))],
        out_shape=[jax.ShapeDtypeStruct((s, 512), F32), jax.ShapeDtypeStruct((s, 512), F32),
                   jax.ShapeDtypeStruct((4, 2, nt, LANES, LANES), F32)],
        scratch_shapes=[pltpu.VMEM((s, LANES), F32)],
        compiler_params=_params(1),
    )(ph, ph, ph, ph, ph, lbl, ng, *consts)


def _hgrn_bwd(ph, pre, dout, states, lbl, ng):
    s = ph.shape[0]
    t = min(HG_TILE, s)
    nt = s // t
    consts = _hgrn_consts(t)

    def body(xq_ref, xff_ref, xfb_ref, xi_ref, xg_ref, pre_ref, do_ref, st_ref, lbl_ref, ng_ref,
             cf_ref, cft_ref, cb_ref, cbt_ref, bs_ref,
             dq_ref, dff_ref, dfb_ref, di_ref, dg_ref, dlb_ref, dng_ref, dpre, dq_acc, dv_acc):
        lbf, lbb = _hgrn_lower_bounds(lbl_ref)
        mk_f, mk_b = _hgrn_masks(t, False), _hgrn_masks(t, True)
        zero = jnp.zeros((LANES, LANES), F32)
        zrow = jnp.zeros((1, LANES), F32)

        def rows_of(i):
            return pl.ds(pl.multiple_of(i * t, t), t)

        def step_n(i, dng):
            rows = rows_of(i)
            o, xg, do = pre_ref[rows, :], xg_ref[rows, :], do_ref[rows, :]
            bs = bs_ref[...]
            r = lax.rsqrt(_xdot(o * o, bs) * (1.0 / HEAD_DIM) + EPS)
            sg = _sigmoid(xg)
            gate = xg * sg
            don = do * gate
            dg_ref[rows, :] = (do * (o * r * ng_ref[...]) * (sg * (1.0 + xg * (1.0 - sg)))).astype(CD)
            u = don * ng_ref[...]
            dpre[rows, :] = r * u - o * (r * r * r) * (_xdot(u * o, bs) * (1.0 / HEAD_DIM))
            return dng + jnp.sum(don * o * r, axis=0, keepdims=True)

        dng_ref[...] = lax.fori_loop(0, nt, step_n, zrow)

        dq_acc[...] = jnp.zeros_like(dq_acc)
        dv_acc[...] = jnp.zeros_like(dv_acc)

        def grad_tile(ti, xf_ref, df_ref, lb, cm, cmt, mk, rev, st, dstate):
            rows = rows_of(ti)
            fn = lambda xq, xf, v, lbv, s_in: _hgrn_dir(xq, xf, v, lbv, s_in, cm, cmt, mk, rev)
            _, vjp = jax.vjp(fn, xq_ref[rows, :], xf_ref[rows, :], xi_ref[rows, :], lb, st)
            dxq, dxf, dv, dlb_t, dstate = vjp((dpre[rows, :], dstate))
            df_ref[rows, :] = dxf.astype(CD)
            dq_acc[rows, :] += dxq
            dv_acc[rows, :] += dv
            return dstate, dlb_t

        def step_g(i, carry):
            dsf, dsb, dlbf, dlbb = carry
            tf, tb = nt - 1 - i, i
            dsf, gf = grad_tile(tf, xff_ref, dff_ref, lbf, cf_ref[...], cft_ref[...], mk_f, False, st_ref[0, 0, tf], dsf)
            dsb, gb = grad_tile(tb, xfb_ref, dfb_ref, lbb, cb_ref[...], cbt_ref[...], mk_b, True, st_ref[0, 1, tb], dsb)
            return dsf, dsb, dlbf + gf, dlbb + gb

        _, _, dlbf, dlbb = lax.fori_loop(0, nt, step_g, (zero, zero, zrow, zrow))
        dlb_ref[0:1, :] = dlbf
        dlb_ref[1:2, :] = dlbb
        dq_ref[...] = dq_acc[...].astype(CD)
        di_ref[...] = dv_acc[...].astype(CD)

    col = lambda off: pl.BlockSpec((s, LANES), lambda m: (0, off + m))
    full = lambda a: pl.BlockSpec(a.shape, lambda m: (0,) * a.ndim)
    stream = jax.ShapeDtypeStruct((s, 512), CD)
    return pl.pallas_call(
        body, name="hgrn_bwd", grid=(4,),
        in_specs=[col(0), col(4), col(8), col(12), col(16), col(0), col(0),
                  pl.BlockSpec((1, 2, nt, LANES, LANES), lambda m: (m, 0, 0, 0, 0)),
                  pl.BlockSpec((4, LANES), lambda m: (0, m)),
                  pl.BlockSpec((1, LANES), lambda m: (0, m))] + [full(c) for c in consts],
        out_specs=[col(0)] * 5 + [pl.BlockSpec((2, LANES), lambda m: (0, m)), pl.BlockSpec((1, LANES), lambda m: (0, m))],
        out_shape=[stream] * 5 + [jax.ShapeDtypeStruct((2, 512), F32), jax.ShapeDtypeStruct((1, 512), F32)],
        scratch_shapes=[pltpu.VMEM((s, LANES), F32), pltpu.VMEM((s, LANES), F32), pltpu.VMEM((s, LANES), F32)],
        compiler_params=_params(1),
    )(ph, ph, ph, ph, ph, pre, dout, states, lbl, ng, *consts)


def _branch_out(o, w4):
    return jnp.concatenate([_mm(o, w4[j], NN) for j in range(N_SHARD)], axis=1)


def _mix_out_fwd(x, oa, ob, pg, wa, wb, wo, tm):
    s, d = x.shape

    def body(x_ref, oa_ref, ob_ref, ga_ref, gb_ref, wa_ref, wb_ref, wo_ref, xo_ref):
        ya = _branch_out(oa_ref[...], wa_ref)
        yb = _branch_out(ob_ref[...], wb_ref)
        merged = _sigmoid(ga_ref[...]) * ya + _sigmoid(gb_ref[...]) * yb
        xo_ref[...] = x_ref[...] + _mm(merged, wo_ref[...], NN)

    row = pl.BlockSpec((tm, d), lambda i: (i, 0))
    half = pl.BlockSpec((tm, 512), lambda i: (i, 0))
    full = lambda a: pl.BlockSpec(a.shape, lambda i: (0,) * a.ndim)
    return pl.pallas_call(
        body, name="mix_out_fwd", grid=(s // tm,),
        in_specs=[row, half, half, row, pl.BlockSpec((tm, d), lambda i: (i, 1)), full(wa), full(wb), full(wo)],
        out_specs=row, out_shape=jax.ShapeDtypeStruct((s, d), F32),
        compiler_params=_params(1),
    )(x, oa, ob, pg, pg, wa, wb, wo)


def _mix_out_bwd(dx, oa, ob, pg, wa, wb, wo, tm, after=()):
    s, d = dx.shape
    eq, ebc = _bf(_np_expand_q()), _bf(_np_headsum_spread())

    def body(*refs):
        (dx_ref, oa_ref, ob_ref, ga_ref, gb_ref, wa_ref, wb_ref, wo_ref, eq_ref, ebc_ref,
         dpg_ref, mg_ref, dya_ref, dyb_ref, doe_ref, dl_ref, dob_ref) = refs[len(after):]
        oa = oa_ref[...]
        ya = _branch_out(oa, wa_ref)
        yb = _branch_out(ob_ref[...], wb_ref)
        sa, sb = _sigmoid(ga_ref[...]), _sigmoid(gb_ref[...])
        mg_ref[...] = (sa * ya + sb * yb).astype(CD)
        dm = _mm(dx_ref[...], wo_ref[...], NT)
        dpg_ref[...] = jnp.concatenate([dm * ya * sa * (1.0 - sa), dm * yb * sb * (1.0 - sb)], axis=1).astype(CD)
        dya, dyb = dm * sa, dm * sb
        dya_ref[...] = dya.astype(CD)
        dyb_ref[...] = dyb.astype(CD)
        doa = jnp.zeros(oa.shape, F32)
        dob = jnp.zeros(oa.shape, F32)
        for j in range(N_SHARD):
            doa = doa + _mm(dya[:, 256 * j:256 * j + 256], wa_ref[j], NT)
            dob = dob + _mm(dyb[:, 256 * j:256 * j + 256], wb_ref[j], NT)
        dob_ref[...] = dob
        doe_ref[...] = _mm(doa, eq_ref[...], NN).astype(CD)
        dl_ref[...] = _xdot(doa * oa, ebc_ref[...])

    row = pl.BlockSpec((tm, d), lambda i: (i, 0))
    half = pl.BlockSpec((tm, 512), lambda i: (i, 0))
    full = lambda a: pl.BlockSpec(a.shape, lambda i: (0,) * a.ndim)
    wide = jax.ShapeDtypeStruct((s, d), CD)
    return pl.pallas_call(
        body, name="mix_out_bwd", grid=(s // tm,),
        in_specs=[ANY] * len(after) + [row, half, half, row, pl.BlockSpec((tm, d), lambda i: (i, 1)), full(wa), full(wb),
                                       full(wo), full(eq), full(ebc)],
        out_specs=[pl.BlockSpec((tm, 2048), lambda i: (i, 0)), row, row, row, row, row, half],
        out_shape=[jax.ShapeDtypeStruct((s, 2048), CD), wide, wide, wide, wide, jax.ShapeDtypeStruct((s, d), F32),
                   jax.ShapeDtypeStruct((s, 512), F32)],
        compiler_params=_params(1),
    )(*after, dx, oa, ob, pg, pg, wa, wb, wo, eq, ebc)


def _loss_head(x, g, target, tm):
    s, d = x.shape

    def body(x_ref, g_ref, t_ref, dx_ref, loss_ref, dg_ref):
        @pl.when(pl.program_id(0) == 0)
        def _():
            loss_ref[...] = jnp.zeros_like(loss_ref)
            dg_ref[...] = jnp.zeros_like(dg_ref)

        xv = x_ref[...]
        r = lax.rsqrt(jnp.mean(xv * xv, axis=-1, keepdims=True) + EPS)
        err = xv * r * g_ref[...] - t_ref[...]
        loss_ref[...] += 0.5 * jnp.sum(jnp.mean(err * err, axis=-1, keepdims=True))
        dy = err * (1.0 / d)
        u = dy * g_ref[...]
        dx_ref[...] = r * u - xv * (r * r * r) * jnp.mean(u * xv, axis=-1, keepdims=True)
        dg_ref[...] += jnp.sum(dy * xv * r, axis=0, keepdims=True)

    row = pl.BlockSpec((tm, d), lambda i: (i, 0))
    vec = pl.BlockSpec((1, d), lambda i: (0, 0))
    return pl.pallas_call(
        body, name="loss_head", grid=(s // tm,),
        in_specs=[row, vec, row], out_specs=[row, pl.BlockSpec((8, LANES), lambda i: (0, 0)), vec],
        out_shape=[jax.ShapeDtypeStruct((s, d), F32), jax.ShapeDtypeStruct((8, LANES), F32),
                   jax.ShapeDtypeStruct((1, d), F32)],
        compiler_params=_params(1),
    )(x, g, target)


def _position():
    x, y, c = lax.axis_index("x"), lax.axis_index("y"), lax.axis_index("c")
    return x, y, c, [(1 - x, y), (x, 1 - y), (1 - x, 1 - y)]


def _row_tile(rows, cap=256):
    best = rows
    for cand in range(8, min(rows, cap) + 1, 8):
        if rows % cand == 0:
            best = cand
    return best


def _cast_into_slots(shards, dtypes, me_idx):
    n = len(shards)
    tiles = [_row_tile(s.shape[0]) for s in shards]
    counts = [s.shape[0] // t for s, t in zip(shards, tiles)]
    starts = [sum(counts[:a]) for a in range(n)]

    def body(me_ref, *refs):
        i = pl.program_id(0)
        for a in range(n):
            @pl.when((i >= starts[a]) & (i < starts[a] + counts[a]))
            def _(a=a):
                refs[n + a][0] = refs[a][...].astype(dtypes[a])

    tile_of = [lambda i, a=a: jnp.clip(i - starts[a], 0, counts[a] - 1) for a in range(n)]
    return pl.pallas_call(
        body, name="cast_into_slots",
        grid_spec=pltpu.PrefetchScalarGridSpec(
            num_scalar_prefetch=1, grid=(sum(counts),),
            in_specs=[pl.BlockSpec((tiles[a], shards[a].shape[1]), lambda i, me, a=a: (tile_of[a](i), 0)) for a in range(n)],
            out_specs=[pl.BlockSpec((1, tiles[a], shards[a].shape[1]), lambda i, me, a=a: (me[0], tile_of[a](i), 0))
                       for a in range(n)]),
        out_shape=[jax.ShapeDtypeStruct((N_SHARD,) + s.shape, dt) for s, dt in zip(shards, dtypes)],
        compiler_params=_params(1),
    )(me_idx, *shards)


HBM_SPEC = pl.BlockSpec(memory_space=pltpu.HBM)
SEM_SPEC = pl.BlockSpec(memory_space=pltpu.SEMAPHORE)
DATAFLOW = pltpu.SideEffectType.DATAFLOW_SIDE_EFFECTING


def _exchange_copies(srcs, lands, send, recv, gather):
    x, y, c, chips = _position()
    me = 2 * x + y
    out = []
    for a in range(len(lands)):
        dst = lands[a].at[me]
        if gather and _halved(lands[a]):
            half = lands[a].shape[1] // 2
            dst = lands[a].at[me, pl.ds(c * half, half), :]
        for k, (px, py) in enumerate(chips):
            src = dst if gather else srcs[a].at[2 * px + py]
            out.append(pltpu.make_async_remote_copy(src_ref=src, dst_ref=dst, send_sem=send.at[3 * a + k],
                                                    recv_sem=recv.at[3 * a + k], device_id=(px, py, c), device_id_type=MESH))
    return out


def _halved(land):
    return land.shape[1] % 32 == 0


def _pair_fill(name, lands):
    n = len(lands)

    def body(*refs):
        src, dst = refs[:n], refs[n:2 * n]
        send, recv = refs[2 * n:]
        x, y, c, chips = _position()
        copies = []
        for a in range(n):
            half = src[a].shape[1] // 2
            for k, (px, py) in enumerate(chips):
                rows = (2 * px + py, pl.ds(c * half, half), slice(None))
                cp = pltpu.make_async_remote_copy(src_ref=src[a].at[rows], dst_ref=dst[a].at[rows], send_sem=send.at[a, k],
                                                  recv_sem=recv.at[a, k], device_id=(x, y, 1 - c), device_id_type=MESH)
                cp.start()
                copies.append(cp)
        for cp in copies:
            cp.wait()

    return pl.pallas_call(
        body, name=name, in_specs=[ANY] * n, out_specs=[ANY] * n,
        out_shape=[jax.ShapeDtypeStruct(l.shape, l.dtype) for l in lands],
        input_output_aliases={a: a for a in range(n)},
        scratch_shapes=[pltpu.SemaphoreType.DMA((n, 3)), pltpu.SemaphoreType.DMA((n, 3))],
    )(*lands)


def _exchange_start(name, srcs, lands, after):
    ns, nl, na = len(srcs), len(lands), len(after)
    gather = ns == 0

    def body(*refs):
        src_refs, land_refs = refs[:ns], refs[ns:ns + nl]
        send, recv = refs[ns + nl + na], refs[ns + nl + na + 1]
        token = refs[-1]
        for cp in _exchange_copies(src_refs, land_refs, send, recv, gather):
            cp.start()
        token[...] = jnp.zeros_like(token)

    arrays = [pltpu.with_memory_space_constraint(a, pltpu.HBM) for a in list(srcs) + list(lands)]
    outs = pl.pallas_call(
        body, name=name,
        out_shape=(pltpu.SemaphoreType.DMA((3 * nl,)), pltpu.SemaphoreType.DMA((3 * nl,)),
                   *[pltpu.HBM(a.shape, a.dtype) for a in arrays], jax.ShapeDtypeStruct((8, LANES), F32)),
        in_specs=[HBM_SPEC] * (ns + nl) + [ANY] * na,
        out_specs=(SEM_SPEC, SEM_SPEC, *[HBM_SPEC] * (ns + nl), pl.BlockSpec(memory_space=pltpu.VMEM)),
        input_output_aliases={i: 2 + i for i in range(ns + nl)},
        compiler_params=pltpu.CompilerParams(has_side_effects=DATAFLOW),
    )(*arrays, *after)
    return outs[0], outs[1], list(outs[2:2 + ns]), list(outs[2 + ns:2 + ns + nl]), outs[-1]


def _exchange_wait(name, send, recv, srcs, lands, after):
    ns, nl, na = len(srcs), len(lands), len(after)
    gather = ns == 0

    def body(*refs):
        src_refs, land_refs = refs[:ns], refs[ns:ns + nl]
        send_ref, recv_ref = refs[ns + nl], refs[ns + nl + 1]
        for cp in _exchange_copies(src_refs, land_refs, send_ref, recv_ref, gather):
            cp.wait_send()
            cp.wait_recv()

    outs = pl.pallas_call(
        body, name=name,
        out_shape=tuple(pltpu.HBM(a.shape, a.dtype) for a in list(srcs) + list(lands)),
        in_specs=[HBM_SPEC] * (ns + nl) + [SEM_SPEC, SEM_SPEC] + [ANY] * na,
        out_specs=tuple([HBM_SPEC] * (ns + nl)),
        input_output_aliases={i: i for i in range(ns + nl)},
        compiler_params=pltpu.CompilerParams(has_side_effects=DATAFLOW),
    )(*srcs, *lands, send, recv, *after)
    return list(outs[ns:])


def _pair_exchange(grads):
    n = len(grads)

    def body(*refs):
        src, dst = refs[:n], refs[n:2 * n]
        send, recv = refs[2 * n:]
        x, y, c, _ = _position()
        copies = []
        for a in range(n):
            half = src[a].shape[1] // 2
            cp = pltpu.make_async_remote_copy(
                src_ref=src[a].at[:, pl.ds((1 - c) * half, half), :], dst_ref=dst[a], send_sem=send.at[a],
                recv_sem=recv.at[a], device_id=(x, y, 1 - c), device_id_type=MESH)
            cp.start()
            copies.append(cp)
        for cp in copies:
            cp.wait()

    return pl.pallas_call(
        body, name="grad_pair_exchange", in_specs=[ANY] * n, out_specs=[ANY] * n,
        out_shape=[jax.ShapeDtypeStruct((g.shape[0], g.shape[1] // 2, g.shape[2]), g.dtype) for g in grads],
        scratch_shapes=[pltpu.SemaphoreType.DMA((n,)), pltpu.SemaphoreType.DMA((n,))],
    )(*grads)


def _shard_of(a):
    return lambda i: jnp.clip(i - a * N_SHARD, 0, N_SHARD - 1)


def _pair_sum(gs, gots, c_idx, me_idx):
    n = len(gs)
    halves = [(g.shape[1] // 2, g.shape[2]) for g in gs]

    def body(c_ref, me_ref, *refs):
        g_refs, got_refs, s_refs, own_refs = (refs[k * n:(k + 1) * n] for k in range(4))
        i = pl.program_id(0)
        for a in range(n):
            @pl.when(i // N_SHARD == a)
            def _(a=a):
                sm = g_refs[a][...] + got_refs[a][...].astype(F32)
                s_refs[a][...] = sm.astype(CD)

                @pl.when(i % N_SHARD == me_ref[0])
                def _():
                    own_refs[a][...] = sm[0]

    shard = [_shard_of(a) for a in range(n)]
    return pl.pallas_call(
        body, name="grad_pair_sum",
        grid_spec=pltpu.PrefetchScalarGridSpec(
            num_scalar_prefetch=2, grid=(n * N_SHARD,),
            in_specs=[pl.BlockSpec((1, h, c_), lambda i, c, me, a=a: (shard[a](i), c[0], 0)) for a, (h, c_) in enumerate(halves)]
            + [pl.BlockSpec((1, h, c_), lambda i, c, me, a=a: (shard[a](i), 0, 0)) for a, (h, c_) in enumerate(halves)],
            out_specs=[pl.BlockSpec((1, h, c_), lambda i, c, me, a=a: (shard[a](i), 0, 0)) for a, (h, c_) in enumerate(halves)]
            + [pl.BlockSpec((h, c_), lambda i, c, me: (0, 0)) for h, c_ in halves]),
        out_shape=[jax.ShapeDtypeStruct((N_SHARD, h, c_), CD) for h, c_ in halves]
        + [jax.ShapeDtypeStruct((h, c_), F32) for h, c_ in halves],
        compiler_params=_params(1),
    )(c_idx, me_idx, *gs, *gots)


def _chip_sum(owns, gots, me_idx):
    n = len(owns)

    def body(me_ref, *refs):
        own_refs, got_refs, out_refs = (refs[k * n:(k + 1) * n] for k in range(3))
        i = pl.program_id(0)
        j = i % N_SHARD
        for a in range(n):
            @pl.when(i // N_SHARD == a)
            def _(a=a):
                term = jnp.where(j == me_ref[0], own_refs[a][...], got_refs[a][0].astype(F32))

                @pl.when(j == 0)
                def _():
                    out_refs[a][...] = term

                @pl.when(j > 0)
                def _():
                    out_refs[a][...] += term

    shard = [_shard_of(a) for a in range(n)]
    whole = [pl.BlockSpec(o.shape, lambda i, me: (0, 0)) for o in owns]
    return pl.pallas_call(
        body, name="grad_chip_sum",
        grid_spec=pltpu.PrefetchScalarGridSpec(
            num_scalar_prefetch=1, grid=(n * N_SHARD,),
            in_specs=whole + [pl.BlockSpec((1,) + o.shape, lambda i, me, a=a: (shard[a](i), 0, 0)) for a, o in enumerate(owns)],
            out_specs=whole),
        out_shape=[jax.ShapeDtypeStruct(o.shape, F32) for o in owns],
        compiler_params=_params(1),
    )(me_idx, *owns, *gots)


def _pair_share(halves):
    n = len(halves)

    def body(*refs):
        src, dst = refs[:n], refs[n:2 * n]
        send, recv = refs[2 * n:]
        x, y, c, _ = _position()
        copies = []
        for a in range(n):
            cp = pltpu.make_async_remote_copy(src_ref=src[a], dst_ref=dst[a], send_sem=send.at[a],
                                              recv_sem=recv.at[a], device_id=(x, y, 1 - c), device_id_type=MESH)
            cp.start()
            copies.append(cp)
        for cp in copies:
            cp.wait()

    return pl.pallas_call(
        body, name="grad_pair_share", in_specs=[ANY] * n, out_specs=[ANY] * n,
        out_shape=[jax.ShapeDtypeStruct(h.shape, h.dtype) for h in halves],
        scratch_shapes=[pltpu.SemaphoreType.DMA((n,)), pltpu.SemaphoreType.DMA((n,))],
    )(*halves)


def _small_allreduce(buf):
    rows, cols = buf.shape

    def body(src_ref, out_ref, slots, send, recv):
        x, y, c, _ = _position()
        me = 4 * x + 2 * y + c
        slots[me] = src_ref[...]
        copies = []
        k = 0
        for dx in (0, 1):
            for dy in (0, 1):
                for dc in (0, 1):
                    if (dx, dy, dc) == (0, 0, 0):
                        continue
                    peer = (jnp.where(dx, 1 - x, x), jnp.where(dy, 1 - y, y), jnp.where(dc, 1 - c, c))
                    cp = pltpu.make_async_remote_copy(src_ref=src_ref, dst_ref=slots.at[me], send_sem=send.at[k],
                                                      recv_sem=recv.at[k], device_id=peer, device_id_type=MESH)
                    cp.start()
                    copies.append(cp)
                    k += 1
        for cp in copies:
            cp.wait()
        total = slots[0]
        for dev in range(1, N_DEV):
            total = total + slots[dev]
        out_ref[...] = total

    vm = pl.BlockSpec(memory_space=pltpu.VMEM)
    return pl.pallas_call(
        body, name="small_allreduce", in_specs=[vm], out_specs=vm,
        out_shape=jax.ShapeDtypeStruct((rows, cols), F32),
        scratch_shapes=[pltpu.VMEM((N_DEV, rows, cols), F32), pltpu.SemaphoreType.DMA((N_DEV - 1,)),
                        pltpu.SemaphoreType.DMA((N_DEV - 1,))],
    )(buf)


def _adamw_math(w, gv, m, v):
    mn = ADAM_B1 * m + (1.0 - ADAM_B1) * gv
    vn = ADAM_B2 * v + (1.0 - ADAM_B2) * (gv * gv)
    m_hat = mn / (1.0 - ADAM_B1 ** ADAM_STEP)
    v_hat = vn / (1.0 - ADAM_B2 ** ADAM_STEP)
    return -ADAM_LR * (m_hat / (jnp.sqrt(v_hat) + ADAM_EPS) + ADAM_WD * w), mn, vn


def _adamw(w, g, m, v):
    rows, cols = w.shape
    tr = _row_tile(rows)

    def body(w_ref, g_ref, m_ref, v_ref, d_ref, mo_ref, vo_ref):
        d_ref[...], mo_ref[...], vo_ref[...] = _adamw_math(w_ref[...], g_ref[...], m_ref[...], v_ref[...])

    blk = pl.BlockSpec((tr, cols), lambda i: (i, 0))
    shp = jax.ShapeDtypeStruct((rows, cols), F32)
    return pl.pallas_call(
        body, name="adamw", grid=(rows // tr,), in_specs=[blk] * 4, out_specs=[blk] * 3, out_shape=[shp] * 3,
        compiler_params=_params(1),
    )(w, g, m, v)


def _adamw_halves(w, own, got, m, v, c_idx):
    rows, cols = w.shape
    tr = _row_tile(rows // 2)
    per_half = rows // 2 // tr

    def body(c_ref, w_ref, own_ref, got_ref, m_ref, v_ref, d_ref, mo_ref, vo_ref, g_ref):
        mine = (pl.program_id(0) // per_half) == c_ref[0]
        gv = jnp.where(mine, own_ref[...], got_ref[...])
        g_ref[...] = gv
        d_ref[...], mo_ref[...], vo_ref[...] = _adamw_math(w_ref[...], gv, m_ref[...], v_ref[...])

    blk = pl.BlockSpec((tr, cols), lambda i, c: (i, 0))
    own_blk = pl.BlockSpec((tr, cols), lambda i, c: (jnp.where(i // per_half == c[0], i % per_half, 0), 0))
    got_blk = pl.BlockSpec((tr, cols), lambda i, c: (jnp.where(i // per_half == c[0], 0, i % per_half), 0))
    shp = jax.ShapeDtypeStruct((rows, cols), F32)
    return pl.pallas_call(
        body, name="adamw_halves",
        grid_spec=pltpu.PrefetchScalarGridSpec(num_scalar_prefetch=1, grid=(rows // tr,),
                                               in_specs=[blk, own_blk, got_blk, blk, blk], out_specs=[blk] * 4),
        out_shape=[shp] * 4, compiler_params=_params(1),
    )(c_idx, w, own, got, m, v)


def _local_step(x, target, norm_gains, q_g, k_g, ng, weights_of, grads_done):
    s = x.shape[0]
    tm = min(512, s)
    tq = min(256, s)
    tf = min(1024, s)
    g1, gm, g2, gf = norm_gains
    cos2, sin2 = _rope_tables(s)
    gq8 = jnp.tile(q_g, (1, 8))
    gk2 = jnp.tile(k_g, (1, 2))

    tn = min(256, s)
    tk = min(1024, s)
    w1 = weights_of(1, ())
    x1, s1, t1, b1, h1 = _ffn_fwd(x, g1, w1["g1"], w1["u1"], w1["d1"], tf)
    w2 = weights_of(2, (x1,))
    lbl = w2["lbl"]
    pqkv, ph, pg, hm = _mix_in_fwd(x1, gm, w2["in"], tn)
    qe, kr, vr, vs = _qk_prep(pqkv, gq8, gk2, cos2, sin2, tm)
    oa, lse = _attn_fwd(qe, kr, vr, vs, tq)
    ob, pre, hstates = _hgrn_fwd(ph, lbl, ng)
    x2 = _mix_out_fwd(x1, oa, ob, pg, w2["a"], w2["b"], w2["o"], tm)
    w3 = weights_of(3, (x2,))
    x3, s2, t2, b2, h2 = _ffn_fwd(x2, g2, w3["g2"], w3["u2"], w3["d2"], tf)
    dx3, loss, dgf = _loss_head(x3, gf, target, tm)

    dx2, da2, db2, f2, dg2, dx3c = _ffn_bwd(dx3, x2, g2, s2, t2, b2, w3["g2"], w3["u2"], w3["d2"], tm)
    tok = grads_done(3, dict(g2=_dw_shared_b("dw_gate", da2, h2, tk, 1.0), u2=_dw_shared_b("dw_gate", db2, h2, tk, 1.0),
                             d2=_dw_shared_b("dw_down", f2, dx3c, tk, 0.5)))

    dpg, mg, dya, dyb, doe, delta, dob = _mix_out_bwd(dx2, oa, ob, pg, w2["a"], w2["b"], w2["o"], tm, tok)
    g_o = [g.reshape(N_SHARD, D_MODEL // N_SHARD, D_MODEL) for g in _dw_colblocks("dw_out", mg, dx2, 1, tk)]
    g_a = _dw_colblocks("dw_branch", oa, dya, N_SHARD, tk)
    g_b = _dw_colblocks("dw_branch", ob, dyb, N_SHARD, tk)
    dqt, dkt, dvt = _attn_bwd(qe, kr, kr.T, vr, doe, delta, lse, tq)
    dqkv, dgq, dgk = _qk_prep_bwd(pqkv, dqt.T, dkt.T, dvt.T, gq8, gk2, cos2, sin2, tm)
    dhq, dhff, dhfb, dhi, dhg, dlb, dng = _hgrn_bwd(ph, pre, dob, hstates, lbl, ng)
    dps = (dqkv, dhq, dhff, dhfb, dhi, dhg, dpg)
    g_in = [g.reshape(N_SHARD, -1, D_MODEL) for g in _dw_in(dps, hm, min(2048, s))]
    tok = grads_done(2, {"in": g_in, "a": g_a, "b": g_b, "o": g_o})
    dx1, dgm = _mix_in_bwd(dps, w2["in"], x1, dx2, gm, tn, tok)

    dx0, da1, db1, f1, dg1, dx1c = _ffn_bwd(dx1, x, g1, s1, t1, b1, w1["g1"], w1["u1"], w1["d1"], tm)
    grads_done(1, dict(g1=_dw_shared_b("dw_gate", da1, h1, tk, 1.0), u1=_dw_shared_b("dw_gate", db1, h1, tk, 1.0),
                       d1=_dw_shared_b("dw_down", f1, dx1c, tk, 0.5)))
    small = dict(g1=dg1, gm=dgm, g2=dg2, gf=dgf, gq=dgq, gk=dgk, lb=dlb, ng=dng)
    return loss, dx0, small, lbl


GROUPS = {1: ("g1", "u1", "d1"), 2: ("in", "a", "b", "o"), 3: ("g2", "u2", "d2")}
BIG = GROUPS[1] + GROUPS[2] + GROUPS[3]
TRANSPOSED = ("g1", "u1", "in", "g2", "u2")


def _pack_rows(vectors, width):
    rows = []
    for vct in vectors:
        flat = vct.reshape(-1)
        pad = (-flat.shape[0]) % width
        rows.append(jnp.pad(flat, (0, pad)).reshape(-1, width))
    return jnp.concatenate(rows, axis=0)


def kernel(x, ffn1_norm_g, ffn1_w_gate, ffn1_w_up, ffn1_w_down, mix_norm_g, w_in, q_norm_g, k_norm_g, hgrn_lb_logits, hgrn_out_norm_g, w_branch_attn, w_branch_hgrn, w_out, ffn2_norm_g, ffn2_w_gate, ffn2_w_up, ffn2_w_down, final_norm_g, loss_target, m_ffn1_norm_g, m_ffn1_w_gate, m_ffn1_w_up, m_ffn1_w_down, m_mix_norm_g, m_w_in, m_q_norm_g, m_k_norm_g, m_hgrn_lb_logits, m_hgrn_out_norm_g, m_w_branch_attn, m_w_branch_hgrn, m_w_out, m_ffn2_norm_g, m_ffn2_w_gate, m_ffn2_w_up, m_ffn2_w_down, m_final_norm_g, v_ffn1_norm_g, v_ffn1_w_gate, v_ffn1_w_up, v_ffn1_w_down, v_mix_norm_g, v_w_in, v_q_norm_g, v_k_norm_g, v_hgrn_lb_logits, v_hgrn_out_norm_g, v_w_branch_attn, v_w_branch_hgrn, v_w_out, v_ffn2_norm_g, v_ffn2_w_gate, v_ffn2_w_up, v_ffn2_w_down, v_final_norm_g):
    xi, yi, ci = lax.axis_index("x"), lax.axis_index("y"), lax.axis_index("c")
    me = 2 * xi + yi
    c_idx = jnp.reshape(ci, (1,)).astype(jnp.int32)
    me_idx = jnp.reshape(me, (1,)).astype(jnp.int32)

    big_w = dict(g1=ffn1_w_gate[0], u1=ffn1_w_up[0], d1=ffn1_w_down[0], a=w_branch_attn[0], b=w_branch_hgrn[0],
                 o=w_out[0], g2=ffn2_w_gate[0], u2=ffn2_w_up[0], d2=ffn2_w_down[0])
    big_w["in"] = w_in[0]
    big_m = dict(g1=m_ffn1_w_gate[0], u1=m_ffn1_w_up[0], d1=m_ffn1_w_down[0], a=m_w_branch_attn[0], b=m_w_branch_hgrn[0],
                 o=m_w_out[0], g2=m_ffn2_w_gate[0], u2=m_ffn2_w_up[0], d2=m_ffn2_w_down[0])
    big_m["in"] = m_w_in[0]
    big_v = dict(g1=v_ffn1_w_gate[0], u1=v_ffn1_w_up[0], d1=v_ffn1_w_down[0], a=v_w_branch_attn[0], b=v_w_branch_hgrn[0],
                 o=v_w_out[0], g2=v_ffn2_w_gate[0], u2=v_ffn2_w_up[0], d2=v_ffn2_w_down[0])
    big_v["in"] = v_w_in[0]
    for table in (big_w, big_m, big_v):
        for n in TRANSPOSED:
            table[n] = table[n].T

    started, token = {}, ()
    for grp in (1, 2, 3):
        shards = [big_w[n] for n in GROUPS[grp]] + ([hgrn_lb_logits.reshape(4, LANES)] if grp == 2 else [])
        dtypes = [CD] * len(GROUPS[grp]) + ([F32] if grp == 2 else [])
        lands = _cast_into_slots(shards, dtypes, me_idx)
        send, recv, _, lands, tok = _exchange_start("gather%d_start" % grp, [], lands, token)
        started[grp], token = (send, recv, lands), (tok,)

    def weights_of(grp, after):
        send, recv, lands = started[grp]
        got = _exchange_wait("gather%d_wait" % grp, send, recv, [], lands, tuple(after) + (token if grp == 1 else ()))
        by_halves = [i for i, land in enumerate(got) if _halved(land)]
        for i, whole in zip(by_halves, _pair_fill("gather%d_fill" % grp, [got[i] for i in by_halves])):
            got[i] = whole
        w = dict(zip(GROUPS[grp], got))
        if grp == 2:
            w["in"] = w["in"].reshape(-1, D_MODEL)
            w["o"] = w["o"].reshape(D_MODEL, D_MODEL)
            w["lbl"] = jnp.transpose(got[-1], (1, 0, 2)).reshape(4, N_SHARD * LANES)
        return w

    pending = {}

    def grads_done(grp, grads):
        names = list(grads)
        got = _pair_exchange([grads[n][1] for n in names])
        res = _pair_sum([grads[n][0] for n in names], got, c_idx, me_idx)
        sums, owns = res[:len(names)], res[len(names):]
        lands = [lax.empty(s_.shape, s_.dtype) for s_ in sums]
        send, recv, srcs, lands, tok = _exchange_start("reduce%d_start" % grp, list(sums), lands, ())
        pending[grp] = (names, send, recv, srcs, lands, owns, tok)
        return (tok,)

    def reduced_halves(grp, after):
        names, send, recv, srcs, lands, owns, _ = pending[grp]
        parts = _exchange_wait("reduce%d_wait" % grp, send, recv, srcs, lands, after)
        return names, list(_chip_sum(list(owns), parts, me_idx))

    loss, dx, small, lbl = _local_step(
        x[0], loss_target[0], (ffn1_norm_g, mix_norm_g, ffn2_norm_g, final_norm_g.reshape(1, -1)),
        q_norm_g, k_norm_g, hgrn_out_norm_g, weights_of, grads_done)

    dgq = small["gq"].reshape(8, HEAD_DIM).sum(axis=0)
    dgk = small["gk"].reshape(2, HEAD_DIM).sum(axis=0)
    lb_full = _hgrn_lower_bounds(lbl)
    dlog = []
    for d in (0, 1):
        t = small["lb"][d:d + 1] * lb_full[d] * (1.0 - lb_full[d])
        dlog += [t, -t]
    small_list = [small["g1"], small["gm"], small["g2"], small["gf"], small["ng"], dgq, dgk, jnp.concatenate(dlog, axis=0), loss[0, 0]]
    packed = _pack_rows(small_list, D_MODEL)
    n_rows = packed.shape[0]
    packed = jnp.pad(packed, ((0, (-n_rows) % 8), (0, 0)))
    red = _small_allreduce(packed)
    loss_out = red[n_rows - 1, 0]
    sg = dict(g1=red[0:1], gm=red[1:2], g2=red[2:3], gf=red[3], ng=red[4:5, :512], gq=red[5:6, :HEAD_DIM],
              gk=red[6:7, :HEAD_DIM])
    dlog_full = red[7:9].reshape(2, 2, 512)
    sg["lb"] = lax.dynamic_slice_in_dim(dlog_full, me * LANES, LANES, axis=2)

    small_w = dict(g1=ffn1_norm_g, gm=mix_norm_g, g2=ffn2_norm_g, gf=final_norm_g, ng=hgrn_out_norm_g, gq=q_norm_g,
                   gk=k_norm_g, lb=hgrn_lb_logits)
    small_m = dict(g1=m_ffn1_norm_g, gm=m_mix_norm_g, g2=m_ffn2_norm_g, gf=m_final_norm_g, ng=m_hgrn_out_norm_g,
                   gq=m_q_norm_g, gk=m_k_norm_g, lb=m_hgrn_lb_logits)
    small_v = dict(g1=v_ffn1_norm_g, gm=v_mix_norm_g, g2=v_ffn2_norm_g, gf=v_final_norm_g, ng=v_hgrn_out_norm_g,
                   gq=v_q_norm_g, gk=v_k_norm_g, lb=v_hgrn_lb_logits)
    small_names = ("g1", "gm", "g2", "gf", "ng", "gq", "gk", "lb")
    pack = lambda dct: _pack_rows([dct[n] for n in small_names], D_MODEL)
    pw, pgr, pm, pv = pack(small_w), pack(sg), pack(small_m), pack(small_v)
    pad8 = lambda a: jnp.pad(a, ((0, (-a.shape[0]) % 8), (0, 0)))
    sd, sm_, sv_ = _adamw(pad8(pw), pad8(pgr), pad8(pm), pad8(pv))

    def unpack(buf):
        out, r = {}, 0
        for n in small_names:
            size = small_w[n].size
            nr = -(-size // D_MODEL)
            out[n] = buf[r:r + nr].reshape(-1)[:size].reshape(small_w[n].shape)
            r += nr
        return out

    sdelta, snew_m, snew_v = unpack(sd), unpack(sm_), unpack(sv_)
    sgrad = {n: sg[n].reshape(small_w[n].shape) for n in small_names}

    bdelta, bnew_m, bnew_v, bgrad = {}, {}, {}, {}

    def update(names, halves):
        for n, own, got in zip(names, halves, _pair_share(halves)):
            res = _adamw_halves(big_w[n], own, got, big_m[n], big_v[n], c_idx)
            if n in TRANSPOSED:
                res = [r.T for r in res]
            bdelta[n], bnew_m[n], bnew_v[n], bgrad[n] = [r[None] for r in res]

    names3, halves3 = reduced_halves(3, (pending[1][-1],))
    names2, halves2 = reduced_halves(2, (halves3[0],))
    update(names3 + names2, halves3 + halves2)
    names1, halves1 = reduced_halves(1, (bdelta[names2[-1]],))
    update(names1, halves1)

    order = [("s", "g1"), ("b", "g1"), ("b", "u1"), ("b", "d1"), ("s", "gm"), ("b", "in"), ("s", "gq"), ("s", "gk"),
             ("s", "lb"), ("s", "ng"), ("b", "a"), ("b", "b"), ("b", "o"), ("s", "g2"), ("b", "g2"), ("b", "u2"),
             ("b", "d2"), ("s", "gf")]
    outs = [loss_out, dx[None]]
    for table_s, table_b in ((sgrad, bgrad), (sdelta, bdelta), (snew_m, bnew_m), (snew_v, bnew_v)):
        outs += [(table_s if kind == "s" else table_b)[n] for kind, n in order]
    return tuple(outs)
```

```python
import functools

import numpy as np
import jax
import jax.numpy as jnp
from jax import lax
from jax.experimental import pallas as pl
from jax.experimental.pallas import tpu as pltpu

F32 = jnp.float32
BF16 = jnp.bfloat16
CD = jnp.bfloat16

EPS = 1e-6
D_MODEL = 1024
HEAD_DIM = 64
GRID_W = 64
ROPE_THETA = 10000.0
CHUNK = 32
N_SHARD = 4
N_DEV = 8
VMEM_LIMIT = 56 * 1024 * 1024
LANES = 128
HG_TILE = 256
ATTN_BWD_HEADS = 2
FFN_ROWS = 256

ADAM_LR = 0.001
ADAM_B1 = 0.9
ADAM_B2 = 0.999
ADAM_EPS = 1e-08
ADAM_WD = 0.01
ADAM_STEP = 10

NN = (((1,), (0,)), ((), ()))
NT = (((1,), (1,)), ((), ()))
TN = (((0,), (0,)), ((), ()))
MESH = pl.DeviceIdType.MESH
ANY = pl.BlockSpec(memory_space=pl.ANY)


def _mm(a, b, dn):
    return lax.dot_general(a.astype(CD), b.astype(CD), dn, preferred_element_type=F32)


def _split3(x):
    hi = x.astype(BF16)
    r = x - hi.astype(F32)
    mid = r.astype(BF16)
    lo = (r - mid.astype(F32)).astype(BF16)
    return hi, mid, lo


def _xdot(x, m):
    rows = x.shape[0]
    hi, mid, _ = _split3(x)
    r = lax.dot_general(jnp.concatenate([hi, mid], axis=0), m, NN, preferred_element_type=F32)
    return r[:rows] + r[rows:]


def _xdot_l(m, x):
    cols = x.shape[1]
    hi, mid, _ = _split3(x)
    r = lax.dot_general(m, jnp.concatenate([hi, mid], axis=1), NN, preferred_element_type=F32)
    return r[:, :cols] + r[:, cols:]


def _params(n_grid):
    return pltpu.CompilerParams(dimension_semantics=("arbitrary",) * n_grid, vmem_limit_bytes=VMEM_LIMIT)


def _sigmoid(x):
    return jax.nn.sigmoid(x)


def _np_blocksum(n):
    i = np.arange(n)
    return (i[:, None] // HEAD_DIM == i[None, :] // HEAD_DIM).astype(np.float32)


def _np_swap32(n):
    i = np.arange(n)
    partner = np.where(i % HEAD_DIM < HEAD_DIM // 2, i + HEAD_DIM // 2, i - HEAD_DIM // 2)
    m = np.zeros((n, n), np.float32)
    m[i, partner] = 1.0
    return m


def _np_expand_q():
    m = np.zeros((512, 1024), np.float32)
    for h in range(8):
        g = h // 4
        for d in range(HEAD_DIM):
            m[64 * h + d, 128 * h + 64 * g + d] = 1.0
    return m


def _np_headsum_spread():
    m = np.zeros((512, 1024), np.float32)
    for h in range(8):
        m[64 * h:64 * h + 64, 128 * h:128 * h + 128] = 1.0
    return m


def _np_swap_halves():
    m = np.zeros((128, 128), np.float32)
    i = np.arange(128)
    m[i, (i + 64) % 128] = 1.0
    return m


def _np_hgrn_cums(t, rev):
    r = np.arange(t)[:, None]
    c = np.arange(t)[None, :]
    same = (r // CHUNK) == (c // CHUNK)
    if not rev:
        cum = same & (c <= r)
        mid = same & (c % CHUNK <= CHUNK // 2 - 1)
    else:
        cum = same & (c >= r)
        mid = same & (c % CHUNK >= CHUNK // 2)
    return np.concatenate([cum, mid, same], axis=0).astype(np.float32)


def _bf(a):
    return jnp.asarray(a, dtype=BF16)


def _rope_tables(seq_len):
    rows = seq_len // GRID_W
    row = jnp.repeat(jnp.arange(rows, dtype=F32), GRID_W)
    col = jnp.tile(jnp.arange(GRID_W, dtype=F32), rows)
    n_freq = HEAD_DIM // 4
    inv = ROPE_THETA ** (-jnp.arange(n_freq, dtype=F32) / n_freq)
    ang = jnp.concatenate([row[:, None] * inv, col[:, None] * inv], axis=-1)
    cos, sin = jnp.cos(ang), jnp.sin(ang)
    c64 = jnp.concatenate([cos, cos], axis=-1)
    s64 = jnp.concatenate([-sin, sin], axis=-1)
    return jnp.tile(c64, (1, 2)), jnp.tile(s64, (1, 2))


def _ffn_fwd(x, g, wg, wu, wd, tm):
    s, d = x.shape
    nsh, fs, _ = wg.shape

    def body(x_ref, g_ref, wg_ref, wu_ref, wd_ref, xo_ref, a_ref, da_ref, b_ref, hb_ref, acc, hs):
        j = pl.program_id(1)

        @pl.when(j == 0)
        def _():
            xv = x_ref[...]
            r = lax.rsqrt(jnp.mean(xv * xv, axis=-1, keepdims=True) + EPS)
            h = (xv * r * g_ref[...]).astype(CD)
            hs[...] = h
            hb_ref[...] = h
            acc[...] = jnp.zeros_like(acc)

        for r0 in range(0, tm, FFN_ROWS):
            rows = slice(r0, min(r0 + FFN_ROWS, tm))
            h = hs[rows, :]
            a = _mm(h, wg_ref[0], NT)
            b = _mm(h, wu_ref[0], NT)
            sg = _sigmoid(a)
            silu = a * sg
            acc[rows, :] += _mm(silu * b, wd_ref[0], NN)
            a_ref[0, rows, :] = silu.astype(CD)
            da_ref[0, rows, :] = (sg * (1.0 + a * (1.0 - sg))).astype(CD)
            b_ref[0, rows, :] = b.astype(CD)

        @pl.when(j == nsh - 1)
        def _():
            xo_ref[...] = x_ref[...] + 0.5 * acc[...]

    return pl.pallas_call(
        body, name="ffn_fwd", grid=(s // tm, nsh),
        in_specs=[pl.BlockSpec((tm, d), lambda i, j: (i, 0)), pl.BlockSpec((1, d), lambda i, j: (0, 0))]
        + [pl.BlockSpec((1, fs, d), lambda i, j: (j, 0, 0))] * 3,
        out_specs=[pl.BlockSpec((tm, d), lambda i, j: (i, 0))] + [pl.BlockSpec((1, tm, fs), lambda i, j: (j, i, 0))] * 3
        + [pl.BlockSpec((tm, d), lambda i, j: (i, 0))],
        out_shape=[jax.ShapeDtypeStruct((s, d), F32)] + [jax.ShapeDtypeStruct((nsh, s, fs), CD)] * 3
        + [jax.ShapeDtypeStruct((s, d), CD)],
        scratch_shapes=[pltpu.VMEM((tm, d), F32), pltpu.VMEM((tm, d), CD)],
        compiler_params=_params(2),
    )(x, g, wg, wu, wd)


def _ffn_bwd(dout, x, g, silu, dsilu, b, wg, wu, wd, tm):
    s, d = x.shape
    nsh, fs, _ = wg.shape

    def body(do_ref, x_ref, g_ref, sl_ref, ds_ref, b_ref, wg_ref, wu_ref, wd_ref,
             dx_ref, da_ref, db_ref, f_ref, dg_ref, do16_ref, dh):
        i = pl.program_id(0)
        j = pl.program_id(1)

        @pl.when(j == 0)
        def _():
            dh[...] = jnp.zeros_like(dh)
            do16_ref[...] = do_ref[...].astype(CD)

        @pl.when((i == 0) & (j == 0))
        def _():
            dg_ref[...] = jnp.zeros_like(dg_ref)

        for r0 in range(0, tm, FFN_ROWS):
            rows = slice(r0, min(r0 + FFN_ROWS, tm))
            sl = sl_ref[0, rows, :].astype(F32)
            bv = b_ref[0, rows, :].astype(F32)
            df = 0.5 * _mm(do_ref[rows, :], wd_ref[0], NT)
            da = df * bv * ds_ref[0, rows, :].astype(F32)
            db = df * sl
            dh[rows, :] += _mm(da, wg_ref[0], NN) + _mm(db, wu_ref[0], NN)
            da_ref[0, rows, :] = da.astype(CD)
            db_ref[0, rows, :] = db.astype(CD)
            f_ref[0, rows, :] = (sl * bv).astype(CD)

        @pl.when(j == nsh - 1)
        def _():
            xv = x_ref[...]
            r = lax.rsqrt(jnp.mean(xv * xv, axis=-1, keepdims=True) + EPS)
            dhv = dh[...]
            u = dhv * g_ref[...]
            dx_ref[...] = do_ref[...] + r * u - xv * (r * r * r) * jnp.mean(u * xv, axis=-1, keepdims=True)
            dg_ref[...] += jnp.sum(dhv * xv * r, axis=0, keepdims=True)

    act = pl.BlockSpec((1, tm, fs), lambda i, j: (j, i, 0))
    row = pl.BlockSpec((tm, d), lambda i, j: (i, 0))
    return pl.pallas_call(
        body, name="ffn_bwd", grid=(s // tm, nsh),
        in_specs=[row, row, pl.BlockSpec((1, d), lambda i, j: (0, 0)), act, act, act]
        + [pl.BlockSpec((1, fs, d), lambda i, j: (j, 0, 0))] * 3,
        out_specs=[row, act, act, act, pl.BlockSpec((1, d), lambda i, j: (0, 0)), row],
        out_shape=[jax.ShapeDtypeStruct((s, d), F32), jax.ShapeDtypeStruct((nsh, s, fs), CD),
                   jax.ShapeDtypeStruct((nsh, s, fs), CD), jax.ShapeDtypeStruct((nsh, s, fs), CD),
                   jax.ShapeDtypeStruct((1, d), F32), jax.ShapeDtypeStruct((s, d), CD)],
        scratch_shapes=[pltpu.VMEM((tm, d), F32)],
        compiler_params=_params(2),
    )(dout, x, g, silu, dsilu, b, wg, wu, wd)


def _tn_call(name, operands, in_specs, out_shape, out_spec, grid, acc_shape, pick, scale=1.0):
    nk = grid[-1]
    n_in = len(operands)

    def body(*refs):
        out_ref, out16_ref, acc = refs[n_in], refs[n_in + 1], refs[n_in + 2]
        k = pl.program_id(len(grid) - 1)

        @pl.when(k == 0)
        def _():
            acc[...] = jnp.zeros_like(acc)

        pick(refs[:n_in], acc)

        @pl.when(k == nk - 1)
        def _():
            res = (acc[...] if scale == 1.0 else acc[...] * scale).reshape(out_ref.shape)
            out_ref[...] = res
            out16_ref[...] = res.astype(CD)

    return pl.pallas_call(
        body, name=name, grid=grid, in_specs=in_specs, out_specs=[out_spec, out_spec],
        out_shape=[out_shape, jax.ShapeDtypeStruct(out_shape.shape, CD)],
        scratch_shapes=[pltpu.VMEM(acc_shape, F32)], compiler_params=_params(len(grid)),
    )(*operands)


def _dw_shared_b(name, a3, b, tk, scale):
    nj, s, m = a3.shape
    n = b.shape[1]

    def pick(refs, acc):
        rows = pl.ds(pl.multiple_of(pl.program_id(1) * tk, tk), tk)
        acc[...] += _mm(refs[0][0], refs[1][rows, :], TN)

    return _tn_call(name, (a3, b),
                    [pl.BlockSpec((1, tk, m), lambda j, k: (j, k, 0)), pl.BlockSpec((s, n), lambda j, k: (0, 0))],
                    jax.ShapeDtypeStruct((nj, m, n), F32), pl.BlockSpec((1, m, n), lambda j, k: (j, 0, 0)),
                    (nj, s // tk), (m, n), pick, scale)


def _dw_colblocks(name, a, b, nj, tk):
    s, m = a.shape
    n = b.shape[1] // nj

    def pick(refs, acc):
        acc[...] += _mm(refs[0][...], refs[1][...], TN)

    return _tn_call(name, (a, b),
                    [pl.BlockSpec((tk, m), lambda j, k: (k, 0)), pl.BlockSpec((tk, n), lambda j, k: (k, j))],
                    jax.ShapeDtypeStruct((nj, m, n), F32), pl.BlockSpec((1, m, n), lambda j, k: (j, 0, 0)),
                    (nj, s // tk), (m, n), pick)


DP_WIDTHS = (768, 512, 512, 512, 512, 512, 2048)
DW_IN_COLS = 512


def _dw_in(dps, hb, tk):
    s, d = hb.shape
    nk = s // tk
    blocks, row = [], 0
    for p, width in enumerate(DP_WIDTHS):
        step = width if width <= 768 else DW_IN_COLS
        for c0 in range(0, width, step):
            blocks.append((p, c0, step, row))
            row += step
    nb, max_w = len(blocks), max(b[2] for b in blocks)
    first = [min(i for i, b in enumerate(blocks) if b[0] == p) for p in range(len(DP_WIDTHS))]
    count = [sum(1 for b in blocks if b[0] == p) for p in range(len(DP_WIDTHS))]

    def body(*refs):
        dp_refs, hb_ref, out_ref, out16_ref, acc, acc16, sems = refs[:7], refs[7], refs[8], refs[9], refs[10], refs[11], refs[12]
        b, k = pl.program_id(0), pl.program_id(1)
        rows = pl.ds(pl.multiple_of(k * tk, tk), tk)

        def writes(i):
            _, _, w, r0 = blocks[i]
            slot = i % 2
            return (pltpu.make_async_copy(acc.at[slot, 0:w], out_ref.at[r0:r0 + w], sems.at[slot, 0]),
                    pltpu.make_async_copy(acc16.at[slot, 0:w], out16_ref.at[r0:r0 + w], sems.at[slot, 1]))

        for i, (p, _, w, _) in enumerate(blocks):
            @pl.when(b == i)
            def _(i=i, p=p, w=w):
                slot = i % 2
                prod = _mm(dp_refs[p][...], hb_ref[rows, :], TN)

                @pl.when(k == 0)
                def _():
                    acc[slot, 0:w] = prod

                @pl.when(k > 0)
                def _():
                    acc[slot, 0:w] += prod

                @pl.when(k == nk - 1)
                def _():
                    if i >= 1:
                        for cp in writes(i - 1):
                            cp.wait()
                    acc16[slot, 0:w] = acc[slot, 0:w].astype(CD)
                    for cp in writes(i):
                        cp.start()
                    if i == nb - 1:
                        for cp in writes(i):
                            cp.wait()

    def piece_spec(p):
        width = DP_WIDTHS[p]
        cols = width if width <= 768 else DW_IN_COLS

        def imap(b, k):
            active = (b >= first[p]) & (b < first[p] + count[p])
            return (jnp.where(active, k, jnp.where(b < first[p], 0, nk - 1)), jnp.clip(b - first[p], 0, count[p] - 1))

        return pl.BlockSpec((tk, cols), imap)

    return pl.pallas_call(
        body, name="dw_in", grid=(nb, nk),
        in_specs=[piece_spec(p) for p in range(len(DP_WIDTHS))] + [pl.BlockSpec((s, d), lambda b, k: (0, 0))],
        out_specs=[ANY, ANY],
        out_shape=[jax.ShapeDtypeStruct((sum(DP_WIDTHS), d), F32), jax.ShapeDtypeStruct((sum(DP_WIDTHS), d), CD)],
        scratch_shapes=[pltpu.VMEM((2, max_w, d), F32), pltpu.VMEM((2, max_w, d), CD), pltpu.SemaphoreType.DMA((2, 2))],
        compiler_params=_params(2),
    )(*dps, hb)


def _mix_in_fwd(x, g, w_t, tm):
    s, d = x.shape
    n_in = w_t.shape[0]

    def body(x_ref, g_ref, w_ref, qkv_ref, hg_ref, gt_ref, hb_ref):
        xv = x_ref[...]
        r = lax.rsqrt(jnp.mean(xv * xv, axis=-1, keepdims=True) + EPS)
        h = (xv * r * g_ref[...]).astype(CD)
        hb_ref[...] = h
        off = DP_WIDTHS[0]
        qkv_ref[...] = _mm(h, w_ref[0:off, :], NT)
        for c, width in enumerate(DP_WIDTHS[1:6]):
            hg_ref[:, c * width:(c + 1) * width] = _mm(h, w_ref[off:off + width, :], NT)
            off += width
        gate = DP_WIDTHS[6] // 2
        for c in range(2):
            gt_ref[:, c * gate:(c + 1) * gate] = _mm(h, w_ref[off:off + gate, :], NT)
            off += gate

    row = lambda w: pl.BlockSpec((tm, w), lambda i: (i, 0))
    return pl.pallas_call(
        body, name="mix_in_fwd", grid=(s // tm,),
        in_specs=[row(d), pl.BlockSpec((1, d), lambda i: (0, 0)), pl.BlockSpec((n_in, d), lambda i: (0, 0))],
        out_specs=[row(768), row(2560), row(2048), row(d)],
        out_shape=[jax.ShapeDtypeStruct((s, 768), F32), jax.ShapeDtypeStruct((s, 2560), F32),
                   jax.ShapeDtypeStruct((s, 2048), F32), jax.ShapeDtypeStruct((s, d), CD)],
        compiler_params=_params(1),
    )(x, g, w_t)


def _mix_in_bwd(dps, w_t, x, dres, g, tm, after=()):
    s, d = x.shape
    n_in = w_t.shape[0]

    def body(*refs):
        refs = refs[len(after):]
        dp_refs = refs[:7]
        w_ref, x_ref, dr_ref, g_ref, dx_ref, dg_ref = refs[7:]

        @pl.when(pl.program_id(0) == 0)
        def _():
            dg_ref[...] = jnp.zeros_like(dg_ref)

        dhv = jnp.zeros((tm, d), F32)
        off = 0
        for ref, width in zip(dp_refs, DP_WIDTHS):
            dhv = dhv + _mm(ref[...], w_ref[off:off + width, :], NN)
            off += width
        xv = x_ref[...]
        r = lax.rsqrt(jnp.mean(xv * xv, axis=-1, keepdims=True) + EPS)
        u = dhv * g_ref[...]
        dx_ref[...] = dr_ref[...] + r * u - xv * (r * r * r) * jnp.mean(u * xv, axis=-1, keepdims=True)
        dg_ref[...] += jnp.sum(dhv * xv * r, axis=0, keepdims=True)

    row = pl.BlockSpec((tm, d), lambda i: (i, 0))
    vec = pl.BlockSpec((1, d), lambda i: (0, 0))
    return pl.pallas_call(
        body, name="mix_in_bwd", grid=(s // tm,),
        in_specs=[ANY] * len(after) + [pl.BlockSpec((tm, w), lambda i: (i, 0)) for w in DP_WIDTHS]
        + [pl.BlockSpec((n_in, d), lambda i: (0, 0)), row, row, vec],
        out_specs=[row, vec],
        out_shape=[jax.ShapeDtypeStruct((s, d), F32), jax.ShapeDtypeStruct((1, d), F32)],
        compiler_params=_params(1),
    )(*after, *dps, w_t, x, dres, g)


def _headnorm_rope(x, gain, cos, sin, blocksum, swap):
    ss = _xdot(x * x, blocksum)
    r = lax.rsqrt(ss * (1.0 / HEAD_DIM) + EPS)
    y = x * r * gain
    return y * cos + _xdot(y, swap) * sin, r


def _headnorm_rope_bwd(dz, x, gain, cos, sin, blocksum, swap):
    ss = _xdot(x * x, blocksum)
    r = lax.rsqrt(ss * (1.0 / HEAD_DIM) + EPS)
    dy = dz * cos + _xdot(dz * sin, swap)
    u = dy * gain
    mean_ux = _xdot(u * x, blocksum) * (1.0 / HEAD_DIM)
    dx = r * u - x * (r * r * r) * mean_ux
    return dx, jnp.sum(dy * x * r, axis=0, keepdims=True)


def _qk_prep(pqkv, gq, gk, cos2, sin2, tm):
    s = pqkv.shape[0]
    bs512, sw512, eq, swh = _bf(_np_blocksum(512)), _bf(_np_swap32(512)), _bf(_np_expand_q()), _bf(_np_swap_halves())

    def body(q_ref, kv_ref, gq_ref, gk_ref, c_ref, s_ref, bs_ref, sw_ref, eq_ref, swh_ref, qe_ref, k_ref, v_ref, vs_ref):
        c2, s2 = c_ref[...], s_ref[...]
        c8, s8 = jnp.tile(c2, (1, 4)), jnp.tile(s2, (1, 4))
        bs, sw = bs_ref[...], sw_ref[...]
        zq, _ = _headnorm_rope(q_ref[...], gq_ref[...], c8, s8, bs, sw)
        qe_ref[...] = _mm(zq * (HEAD_DIM ** -0.5), eq_ref[...], NN).astype(CD)
        kv = kv_ref[...]
        zk, _ = _headnorm_rope(kv[:, :LANES], gk_ref[...], c2, s2, bs[:LANES, :LANES], sw[:LANES, :LANES])
        k_ref[...] = zk.astype(CD)
        v = kv[:, LANES:]
        v_ref[...] = v.astype(CD)
        vs_ref[...] = _mm(v, swh_ref[...], NN).astype(CD)

    full = lambda a: pl.BlockSpec(a.shape, lambda i: (0,) * a.ndim)
    tab = pl.BlockSpec((tm, LANES), lambda i: (i, 0))
    return pl.pallas_call(
        body, name="qk_prep", grid=(s // tm,),
        in_specs=[pl.BlockSpec((tm, 512), lambda i: (i, 0)), pl.BlockSpec((tm, 256), lambda i: (i, 2)),
                  full(gq), full(gk), tab, tab, full(bs512), full(sw512), full(eq), full(swh)],
        out_specs=[pl.BlockSpec((tm, 1024), lambda i: (i, 0)), tab, tab, tab],
        out_shape=[jax.ShapeDtypeStruct((s, 1024), CD)] + [jax.ShapeDtypeStruct((s, LANES), CD)] * 3,
        compiler_params=_params(1),
    )(pqkv, pqkv, gq, gk, cos2, sin2, bs512, sw512, eq, swh)


def _qk_prep_bwd(pqkv, dq, dk, dv, gq, gk, cos2, sin2, tm):
    s = pqkv.shape[0]
    bs512, sw512 = _bf(_np_blocksum(512)), _bf(_np_swap32(512))

    def body(q_ref, kv_ref, dq_ref, dk_ref, dv_ref, gq_ref, gk_ref, c_ref, s_ref, bs_ref, sw_ref,
             dp_ref, dgq_ref, dgk_ref):
        @pl.when(pl.program_id(0) == 0)
        def _():
            dgq_ref[...] = jnp.zeros_like(dgq_ref)
            dgk_ref[...] = jnp.zeros_like(dgk_ref)

        c2, s2 = c_ref[...], s_ref[...]
        c8, s8 = jnp.tile(c2, (1, 4)), jnp.tile(s2, (1, 4))
        bs, sw = bs_ref[...], sw_ref[...]
        dzq = dq_ref[...] * (HEAD_DIM ** -0.5)
        dxq, dgq = _headnorm_rope_bwd(dzq, q_ref[...], gq_ref[...], c8, s8, bs, sw)
        kv = kv_ref[...]
        dxk, dgk = _headnorm_rope_bwd(dk_ref[...], kv[:, :LANES], gk_ref[...], c2, s2, bs[:LANES, :LANES], sw[:LANES, :LANES])
        dp_ref[...] = jnp.concatenate([dxq, dxk, dv_ref[...]], axis=1).astype(CD)
        dgq_ref[...] += dgq
        dgk_ref[...] += dgk

    full = lambda a: pl.BlockSpec(a.shape, lambda i: (0,) * a.ndim)
    tab = pl.BlockSpec((tm, LANES), lambda i: (i, 0))
    return pl.pallas_call(
        body, name="qk_prep_bwd", grid=(s // tm,),
        in_specs=[pl.BlockSpec((tm, 512), lambda i: (i, 0)), pl.BlockSpec((tm, 256), lambda i: (i, 2)),
                  pl.BlockSpec((tm, 512), lambda i: (i, 0)), tab, tab, full(gq), full(gk), tab, tab,
                  full(bs512), full(sw512)],
        out_specs=[pl.BlockSpec((tm, 768), lambda i: (i, 0)), pl.BlockSpec((1, 512), lambda i: (0, 0)),
                   pl.BlockSpec((1, LANES), lambda i: (0, 0))],
        out_shape=[jax.ShapeDtypeStruct((s, 768), CD), jax.ShapeDtypeStruct((1, 512), F32),
                   jax.ShapeDtypeStruct((1, LANES), F32)],
        compiler_params=_params(1),
    )(pqkv, pqkv, dq, dk, dv, gq, gk, cos2, sin2, bs512, sw512)


def _kv_rows(h):
    return pl.ds(pl.multiple_of((h // 4) * HEAD_DIM, HEAD_DIM), HEAD_DIM)


def _attn_fwd(qe, k, v, vs, tq):
    s = k.shape[0]

    def body(q0_ref, q1_ref, q2_ref, q3_ref, k_ref, v_ref, vs_ref, o_ref, lse_ref):
        grp = pl.program_id(0)
        kk = k_ref[...]
        outs = []
        for r, q_ref in enumerate((q0_ref, q1_ref, q2_ref, q3_ref)):
            sc = _mm(q_ref[...], kk, NT)
            mx = jnp.max(sc, axis=-1, keepdims=True)
            e = jnp.exp(sc - mx)
            l = jnp.sum(e, axis=-1, keepdims=True)
            lse_ref[r] = mx + jnp.log(l)
            vsel = jnp.where(grp != r % 2, vs_ref[...], v_ref[...])
            outs.append(_mm(e, vsel, NN) * (1.0 / l))
        low = lax.broadcasted_iota(jnp.int32, (1, LANES), 1) < HEAD_DIM
        o_ref[...] = jnp.concatenate([jnp.where(low, outs[0], outs[1]), jnp.where(low, outs[2], outs[3])], axis=1)

    kv = pl.BlockSpec((s, LANES), lambda g, i: (0, 0))
    qblk = lambda r: pl.BlockSpec((tq, LANES), lambda g, i: (i, 4 * g + r))
    return pl.pallas_call(
        body, name="attn_fwd", grid=(2, s // tq),
        in_specs=[qblk(0), qblk(1), qblk(2), qblk(3), kv, kv, kv],
        out_specs=[pl.BlockSpec((tq, 2 * LANES), lambda g, i: (i, g)), pl.BlockSpec((4, tq, 1), lambda g, i: (g, i, 0))],
        out_shape=[jax.ShapeDtypeStruct((s, 512), F32), jax.ShapeDtypeStruct((8, s, 1), F32)],
        compiler_params=_params(2),
    )(qe, qe, qe, qe, k, v, vs)


def _attn_bwd(qe, k, kt, v, doe, delta, lse, tq):
    s = k.shape[0]

    nh = ATTN_BWD_HEADS

    def body(*refs):
        q_refs, (k_ref, kt_ref, v_ref) = refs[:nh], refs[nh:nh + 3]
        do_refs, dl_refs = refs[nh + 3:2 * nh + 3], refs[2 * nh + 3:3 * nh + 3]
        lse_ref, dqt_ref, dkt_ref, dvt_ref, qt, dot = refs[3 * nh + 3:]

        @pl.when((pl.program_id(0) == 0) & (pl.program_id(1) == 0))
        def _():
            dkt_ref[...] = jnp.zeros_like(dkt_ref)
            dvt_ref[...] = jnp.zeros_like(dvt_ref)

        rows = _kv_rows(nh * pl.program_id(0))
        kt = kt_ref[rows, :]
        dkt = jnp.zeros((HEAD_DIM, s), F32)
        dvt = jnp.zeros((HEAD_DIM, s), F32)
        for idx, (q_ref, do_ref, dl_ref) in enumerate(zip(q_refs, do_refs, dl_refs)):
            q, do = q_ref[...], do_ref[...]
            p = jnp.exp(_mm(q, k_ref[...], NT) - lse_ref[idx])
            dp = _mm(do, v_ref[...], NT)
            ds = p * (dp - jnp.max(dl_ref[...], axis=-1, keepdims=True))
            dqt_ref[idx * HEAD_DIM:(idx + 1) * HEAD_DIM, :] = _mm(kt, ds, NT)
            qt[idx] = jnp.transpose(q.astype(F32))
            dot[idx] = jnp.transpose(do.astype(F32))
            dkt = dkt + _mm(qt[idx, rows, :], ds, NN)
            dvt = dvt + _mm(dot[idx, rows, :], p, NN)
        dkt_ref[rows, :] += dkt
        dvt_ref[rows, :] += dvt

    kv = pl.BlockSpec((s, LANES), lambda m, i: (0, 0))
    kvt = pl.BlockSpec((LANES, s), lambda m, i: (0, 0))
    blks = [pl.BlockSpec((tq, LANES), lambda m, i, r=r: (i, nh * m + r)) for r in range(nh)]
    return pl.pallas_call(
        body, name="attn_bwd", grid=(8 // nh, s // tq),
        in_specs=blks + [kv, kvt, kv] + blks + blks + [pl.BlockSpec((nh, tq, 1), lambda m, i: (m, i, 0))],
        out_specs=[pl.BlockSpec((nh * HEAD_DIM, tq), lambda m, i: (m, i)), kvt, kvt],
        out_shape=[jax.ShapeDtypeStruct((8 * HEAD_DIM, s), F32), jax.ShapeDtypeStruct((LANES, s), F32),
                   jax.ShapeDtypeStruct((LANES, s), F32)],
        scratch_shapes=[pltpu.VMEM((nh, LANES, tq), F32), pltpu.VMEM((nh, LANES, tq), F32)],
        compiler_params=_params(2),
    )(*[qe] * nh, k, kt, v, *[doe] * nh, *[delta] * nh, lse)


@jax.custom_vjp
def _mm_nn(a, b):
    return _mm(a, b, NN)


_mm_nn.defvjp(lambda a, b: (_mm(a, b, NN), (a, b)),
              lambda res, g: (_mm(g, res[1], NT), _mm(res[0], g, TN)))


@jax.custom_vjp
def _mm_nt(a, b):
    return _mm(a, b, NT)


_mm_nt.defvjp(lambda a, b: (_mm(a, b, NT), (a, b)),
              lambda res, g: (_mm(g, res[1], NN), _mm(g, res[0], TN)))


@jax.custom_vjp
def _mm_tn(a, b):
    return _mm(a, b, TN)


_mm_tn.defvjp(lambda a, b: (_mm(a, b, TN), (a, b)),
              lambda res, g: (_mm(res[1], g, NT), _mm(res[0], g, NN)))


@jax.custom_vjp
def _cmm(m, mt, x):
    return _xdot_l(m, x)


_cmm.defvjp(lambda m, mt, x: (_xdot_l(m, x), (m, mt)),
            lambda res, g: (jnp.zeros_like(res[0]), jnp.zeros_like(res[1]), _xdot_l(res[1], g)))


def _hgrn_masks(t, rev):
    n_ch = t // CHUNK
    r = jnp.bitwise_and(lax.broadcasted_iota(jnp.int32, (2 * t, t), 0), t - 1)
    c = lax.broadcasted_iota(jnp.int32, (2 * t, t), 1)
    same = jnp.right_shift(r, 5) == jnp.right_shift(c, 5)
    tri2 = same & ((c >= r) if rev else (c <= r))
    pr = lax.broadcasted_iota(jnp.int32, (LANES, LANES), 0)
    pc = lax.broadcasted_iota(jnp.int32, (LANES, LANES), 1)
    diag = jnp.right_shift(pr, 6) == jnp.right_shift(pc, 6)
    qr = lax.broadcasted_iota(jnp.int32, (t, n_ch * LANES), 0)
    qc = lax.broadcasted_iota(jnp.int32, (t, n_ch * LANES), 1)
    rows_chunk = jnp.right_shift(qc, 7) == jnp.right_shift(qr, 5)
    vr = lax.broadcasted_iota(jnp.int32, (n_ch * LANES, t), 0)
    vc = lax.broadcasted_iota(jnp.int32, (n_ch * LANES, t), 1)
    cols_chunk = jnp.right_shift(vr, 7) == jnp.right_shift(vc, 5)
    return dict(tri2=tri2, diag=diag, rows_chunk=rows_chunk, cols_chunk=cols_chunk)


def _hgrn_gates(xf, lb):
    f = lb + (1.0 - lb) * _sigmoid(xf)
    return 1.0 - f, jnp.log(f)


def _hgrn_dir(xq, xf, v, lb, state, cm, cmt, mk, rev):
    t = xq.shape[0]
    n_ch = t // CHUNK
    lo = lax.broadcasted_iota(jnp.int32, (1, LANES), 1) < HEAD_DIM
    q = xq * _sigmoid(xq)
    k, lf = _hgrn_gates(xf, lb)
    cs = _cmm(cm, cmt, lf)
    b, bm, bl = cs[:t], cs[t:2 * t], cs[2 * t:]
    qd = q * jnp.exp(b - bm)
    kd = k * jnp.exp(bm - b)
    kc = k * jnp.exp(bl - b)
    qe = q * jnp.exp(b)
    qd2 = jnp.concatenate([jnp.where(lo, qd, 0.0), jnp.where(lo, 0.0, qd)], axis=0)
    o2 = _mm_nn(jnp.where(mk["tri2"], _mm_nt(qd2, kd), 0.0), v)
    o = jnp.where(lo, o2[:t], o2[t:])
    vexp = jnp.where(mk["cols_chunk"], jnp.concatenate([jnp.transpose(v)] * n_ch, axis=0), 0.0)
    adds = _mm_nn(vexp, kc)
    dec = jnp.exp(bl)
    entering = [None] * n_ch
    for c in (range(n_ch - 1, -1, -1) if rev else range(n_ch)):
        entering[c] = state
        d = jnp.concatenate([dec[c * CHUNK:(c + 1) * CHUNK]] * (LANES // CHUNK), axis=0)
        state = d * state + jnp.where(mk["diag"], adds[c * LANES:(c + 1) * LANES], 0.0)
    qexp = jnp.where(mk["rows_chunk"], jnp.concatenate([qe] * n_ch, axis=1), 0.0)
    return o + _mm_nt(qexp, jnp.concatenate(entering, axis=1)), state


def _hgrn_lower_bounds(l):
    out = []
    for d in (0, 1):
        l0, l1 = l[2 * d:2 * d + 1, :], l[2 * d + 1:2 * d + 2, :]
        mx = jnp.maximum(l0, l1)
        e0, e1 = jnp.exp(l0 - mx), jnp.exp(l1 - mx)
        out.append(e0 / (e0 + e1))
    return out


def _hgrn_consts(t):
    cf, cb = _np_hgrn_cums(t, False), _np_hgrn_cums(t, True)
    return (_bf(cf), _bf(cf.T), _bf(cb), _bf(cb.T), _bf(_np_blocksum(LANES)))


def _hgrn_fwd(ph, lbl, ng):
    s = ph.shape[0]
    t = min(HG_TILE, s)
    nt = s // t
    consts = _hgrn_consts(t)

    def body(xq_ref, xff_ref, xfb_ref, xi_ref, xg_ref, lbl_ref, ng_ref, cf_ref, cft_ref, cb_ref, cbt_ref, bs_ref,
             o_ref, pre_ref, st_ref, acc):
        lbf, lbb = _hgrn_lower_bounds(lbl_ref)
        mk_f, mk_b = _hgrn_masks(t, False), _hgrn_masks(t, True)
        zero = jnp.zeros((LANES, LANES), F32)

        def rows_of(i):
            return pl.ds(pl.multiple_of(i * t, t), t)

        acc[...] = jnp.zeros_like(acc)

        def step(i, states):
            tb = nt - 1 - i
            rf, rb = rows_of(i), rows_of(tb)
            st_ref[0, 0, i] = states[0]
            st_ref[0, 1, tb] = states[1]
            of, sf = _hgrn_dir(xq_ref[rf, :], xff_ref[rf, :], xi_ref[rf, :], lbf, states[0],
                               cf_ref[...], cft_ref[...], mk_f, False)
            ob, sb = _hgrn_dir(xq_ref[rb, :], xfb_ref[rb, :], xi_ref[rb, :], lbb, states[1],
                               cb_ref[...], cbt_ref[...], mk_b, True)
            acc[rf, :] += of
            acc[rb, :] += ob
            return sf, sb

        lax.fori_loop(0, nt, step, (zero, zero))

        def step_n(i, carry):
            rows = rows_of(i)
            o = acc[rows, :]
            ss = _xdot(o * o, bs_ref[...])
            r = lax.rsqrt(ss * (1.0 / HEAD_DIM) + EPS)
            xg = xg_ref[rows, :]
            pre_ref[rows, :] = o
            o_ref[rows, :] = (o * r * ng_ref[...]) * (xg * _sigmoid(xg))
            return carry

        lax.fori_loop(0, nt, step_n, 0)

    col = lambda off: pl.BlockSpec((s, LANES), lambda m: (0, off + m))
    full = lambda a: pl.BlockSpec(a.shape, lambda m: (0,) * a.ndim)
    return pl.pallas_call(
        body, name="hgrn_fwd", grid=(4,),
        in_specs=[col(0), col(4), col(8), col(12), col(16), pl.BlockSpec((4, LANES), lambda m: (0, m)),
                  pl.BlockSpec((1, LANES), lambda m: (0, m))] + [full(c) for c in consts],
        out_specs=[col(0), col(0), pl.BlockSpec((1, 2, nt, LANES, LANES), lambda m: (m, 0, 0, 0, 0))],
        out_shape=[jax.ShapeDtypeStruct((s, 512), F32), jax.ShapeDtypeStruct((s, 512), F32),
                   jax.ShapeDtypeStruct((4, 2, nt, LANES, LANES), F32)],
        scratch_shapes=[pltpu.VMEM((s, LANES), F32)],
        compiler_params=_params(1),
    )(ph, ph, ph, ph, ph, lbl, ng, *consts)


def _hgrn_bwd(ph, pre, dout, states, lbl, ng):
    s = ph.shape[0]
    t = min(HG_TILE, s)
    nt = s // t
    consts = _hgrn_consts(t)

    def body(xq_ref, xff_ref, xfb_ref, xi_ref, xg_ref, pre_ref, do_ref, st_ref, lbl_ref, ng_ref,
             cf_ref, cft_ref, cb_ref, cbt_ref, bs_ref,
             dq_ref, dff_ref, dfb_ref, di_ref, dg_ref, dlb_ref, dng_ref, dpre, dq_acc, dv_acc):
        lbf, lbb = _hgrn_lower_bounds(lbl_ref)
        mk_f, mk_b = _hgrn_masks(t, False), _hgrn_masks(t, True)
        zero = jnp.zeros((LANES, LANES), F32)
        zrow = jnp.zeros((1, LANES), F32)

        def rows_of(i):
            return pl.ds(pl.multiple_of(i * t, t), t)

        def step_n(i, dng):
            rows = rows_of(i)
            o, xg, do = pre_ref[rows, :], xg_ref[rows, :], do_ref[rows, :]
            bs = bs_ref[...]
            r = lax.rsqrt(_xdot(o * o, bs) * (1.0 / HEAD_DIM) + EPS)
            sg = _sigmoid(xg)
            gate = xg * sg
            don = do * gate
            dg_ref[rows, :] = (do * (o * r * ng_ref[...]) * (sg * (1.0 + xg * (1.0 - sg)))).astype(CD)
            u = don * ng_ref[...]
            dpre[rows, :] = r * u - o * (r * r * r) * (_xdot(u * o, bs) * (1.0 / HEAD_DIM))
            return dng + jnp.sum(don * o * r, axis=0, keepdims=True)

        dng_ref[...] = lax.fori_loop(0, nt, step_n, zrow)

        dq_acc[...] = jnp.zeros_like(dq_acc)
        dv_acc[...] = jnp.zeros_like(dv_acc)

        def grad_tile(ti, xf_ref, df_ref, lb, cm, cmt, mk, rev, st, dstate):
            rows = rows_of(ti)
            fn = lambda xq, xf, v, lbv, s_in: _hgrn_dir(xq, xf, v, lbv, s_in, cm, cmt, mk, rev)
            _, vjp = jax.vjp(fn, xq_ref[rows, :], xf_ref[rows, :], xi_ref[rows, :], lb, st)
            dxq, dxf, dv, dlb_t, dstate = vjp((dpre[rows, :], dstate))
            df_ref[rows, :] = dxf.astype(CD)
            dq_acc[rows, :] += dxq
            dv_acc[rows, :] += dv
            return dstate, dlb_t

        def step_g(i, carry):
            dsf, dsb, dlbf, dlbb = carry
            tf, tb = nt - 1 - i, i
            dsf, gf = grad_tile(tf, xff_ref, dff_ref, lbf, cf_ref[...], cft_ref[...], mk_f, False, st_ref[0, 0, tf], dsf)
            dsb, gb = grad_tile(tb, xfb_ref, dfb_ref, lbb, cb_ref[...], cbt_ref[...], mk_b, True, st_ref[0, 1, tb], dsb)
            return dsf, dsb, dlbf + gf, dlbb + gb

        _, _, dlbf, dlbb = lax.fori_loop(0, nt, step_g, (zero, zero, zrow, zrow))
        dlb_ref[0:1, :] = dlbf
        dlb_ref[1:2, :] = dlbb
        dq_ref[...] = dq_acc[...].astype(CD)
        di_ref[...] = dv_acc[...].astype(CD)

    col = lambda off: pl.BlockSpec((s, LANES), lambda m: (0, off + m))
    full = lambda a: pl.BlockSpec(a.shape, lambda m: (0,) * a.ndim)
    stream = jax.ShapeDtypeStruct((s, 512), CD)
    return pl.pallas_call(
        body, name="hgrn_bwd", grid=(4,),
        in_specs=[col(0), col(4), col(8), col(12), col(16), col(0), col(0),
                  pl.BlockSpec((1, 2, nt, LANES, LANES), lambda m: (m, 0, 0, 0, 0)),
                  pl.BlockSpec((4, LANES), lambda m: (0, m)),
                  pl.BlockSpec((1, LANES), lambda m: (0, m))] + [full(c) for c in consts],
        out_specs=[col(0)] * 5 + [pl.BlockSpec((2, LANES), lambda m: (0, m)), pl.BlockSpec((1, LANES), lambda m: (0, m))],
        out_shape=[stream] * 5 + [jax.ShapeDtypeStruct((2, 512), F32), jax.ShapeDtypeStruct((1, 512), F32)],
        scratch_shapes=[pltpu.VMEM((s, LANES), F32), pltpu.VMEM((s, LANES), F32), pltpu.VMEM((s, LANES), F32)],
        compiler_params=_params(1),
    )(ph, ph, ph, ph, ph, pre, dout, states, lbl, ng, *consts)


def _branch_out(o, w4):
    return jnp.concatenate([_mm(o, w4[j], NN) for j in range(N_SHARD)], axis=1)


def _mix_out_fwd(x, oa, ob, pg, wa, wb, wo, tm):
    s, d = x.shape

    def body(x_ref, oa_ref, ob_ref, ga_ref, gb_ref, wa_ref, wb_ref, wo_ref, xo_ref):
        ya = _branch_out(oa_ref[...], wa_ref)
        yb = _branch_out(ob_ref[...], wb_ref)
        merged = _sigmoid(ga_ref[...]) * ya + _sigmoid(gb_ref[...]) * yb
        xo_ref[...] = x_ref[...] + _mm(merged, wo_ref[...], NN)

    row = pl.BlockSpec((tm, d), lambda i: (i, 0))
    half = pl.BlockSpec((tm, 512), lambda i: (i, 0))
    full = lambda a: pl.BlockSpec(a.shape, lambda i: (0,) * a.ndim)
    return pl.pallas_call(
        body, name="mix_out_fwd", grid=(s // tm,),
        in_specs=[row, half, half, row, pl.BlockSpec((tm, d), lambda i: (i, 1)), full(wa), full(wb), full(wo)],
        out_specs=row, out_shape=jax.ShapeDtypeStruct((s, d), F32),
        compiler_params=_params(1),
    )(x, oa, ob, pg, pg, wa, wb, wo)


def _mix_out_bwd(dx, oa, ob, pg, wa, wb, wo, tm, after=()):
    s, d = dx.shape
    eq, ebc = _bf(_np_expand_q()), _bf(_np_headsum_spread())

    def body(*refs):
        (dx_ref, oa_ref, ob_ref, ga_ref, gb_ref, wa_ref, wb_ref, wo_ref, eq_ref, ebc_ref,
         dpg_ref, mg_ref, dya_ref, dyb_ref, doe_ref, dl_ref, dob_ref) = refs[len(after):]
        oa = oa_ref[...]
        ya = _branch_out(oa, wa_ref)
        yb = _branch_out(ob_ref[...], wb_ref)
        sa, sb = _sigmoid(ga_ref[...]), _sigmoid(gb_ref[...])
        mg_ref[...] = (sa * ya + sb * yb).astype(CD)
        dm = _mm(dx_ref[...], wo_ref[...], NT)
        dpg_ref[...] = jnp.concatenate([dm * ya * sa * (1.0 - sa), dm * yb * sb * (1.0 - sb)], axis=1).astype(CD)
        dya, dyb = dm * sa, dm * sb
        dya_ref[...] = dya.astype(CD)
        dyb_ref[...] = dyb.astype(CD)
        doa = jnp.zeros(oa.shape, F32)
        dob = jnp.zeros(oa.shape, F32)
        for j in range(N_SHARD):
            doa = doa + _mm(dya[:, 256 * j:256 * j + 256], wa_ref[j], NT)
            dob = dob + _mm(dyb[:, 256 * j:256 * j + 256], wb_ref[j], NT)
        dob_ref[...] = dob
        doe_ref[...] = _mm(doa, eq_ref[...], NN).astype(CD)
        dl_ref[...] = _xdot(doa * oa, ebc_ref[...])

    row = pl.BlockSpec((tm, d), lambda i: (i, 0))
    half = pl.BlockSpec((tm, 512), lambda i: (i, 0))
    full = lambda a: pl.BlockSpec(a.shape, lambda i: (0,) * a.ndim)
    wide = jax.ShapeDtypeStruct((s, d), CD)
    return pl.pallas_call(
        body, name="mix_out_bwd", grid=(s // tm,),
        in_specs=[ANY] * len(after) + [row, half, half, row, pl.BlockSpec((tm, d), lambda i: (i, 1)), full(wa), full(wb),
                                       full(wo), full(eq), full(ebc)],
        out_specs=[pl.BlockSpec((tm, 2048), lambda i: (i, 0)), row, row, row, row, row, half],
        out_shape=[jax.ShapeDtypeStruct((s, 2048), CD), wide, wide, wide, wide, jax.ShapeDtypeStruct((s, d), F32),
                   jax.ShapeDtypeStruct((s, 512), F32)],
        compiler_params=_params(1),
    )(*after, dx, oa, ob, pg, pg, wa, wb, wo, eq, ebc)


def _loss_head(x, g, target, tm):
    s, d = x.shape

    def body(x_ref, g_ref, t_ref, dx_ref, loss_ref, dg_ref):
        @pl.when(pl.program_id(0) == 0)
        def _():
            loss_ref[...] = jnp.zeros_like(loss_ref)
            dg_ref[...] = jnp.zeros_like(dg_ref)

        xv = x_ref[...]
        r = lax.rsqrt(jnp.mean(xv * xv, axis=-1, keepdims=True) + EPS)
        err = xv * r * g_ref[...] - t_ref[...]
        loss_ref[...] += 0.5 * jnp.sum(jnp.mean(err * err, axis=-1, keepdims=True))
        dy = err * (1.0 / d)
        u = dy * g_ref[...]
        dx_ref[...] = r * u - xv * (r * r * r) * jnp.mean(u * xv, axis=-1, keepdims=True)
        dg_ref[...] += jnp.sum(dy * xv * r, axis=0, keepdims=True)

    row = pl.BlockSpec((tm, d), lambda i: (i, 0))
    vec = pl.BlockSpec((1, d), lambda i: (0, 0))
    return pl.pallas_call(
        body, name="loss_head", grid=(s // tm,),
        in_specs=[row, vec, row], out_specs=[row, pl.BlockSpec((8, LANES), lambda i: (0, 0)), vec],
        out_shape=[jax.ShapeDtypeStruct((s, d), F32), jax.ShapeDtypeStruct((8, LANES), F32),
                   jax.ShapeDtypeStruct((1, d), F32)],
        compiler_params=_params(1),
    )(x, g, target)


def _position():
    x, y, c = lax.axis_index("x"), lax.axis_index("y"), lax.axis_index("c")
    return x, y, c, [(1 - x, y), (x, 1 - y), (1 - x, 1 - y)]


def _row_tile(rows, cap=256):
    best = rows
    for cand in range(8, min(rows, cap) + 1, 8):
        if rows % cand == 0:
            best = cand
    return best


def _cast_into_slots(shards, dtypes, me_idx):
    n = len(shards)
    tiles = [_row_tile(s.shape[0]) for s in shards]
    counts = [s.shape[0] // t for s, t in zip(shards, tiles)]
    starts = [sum(counts[:a]) for a in range(n)]

    def body(me_ref, *refs):
        i = pl.program_id(0)
        for a in range(n):
            @pl.when((i >= starts[a]) & (i < starts[a] + counts[a]))
            def _(a=a):
                refs[n + a][0] = refs[a][...].astype(dtypes[a])

    tile_of = [lambda i, a=a: jnp.clip(i - starts[a], 0, counts[a] - 1) for a in range(n)]
    return pl.pallas_call(
        body, name="cast_into_slots",
        grid_spec=pltpu.PrefetchScalarGridSpec(
            num_scalar_prefetch=1, grid=(sum(counts),),
            in_specs=[pl.BlockSpec((tiles[a], shards[a].shape[1]), lambda i, me, a=a: (tile_of[a](i), 0)) for a in range(n)],
            out_specs=[pl.BlockSpec((1, tiles[a], shards[a].shape[1]), lambda i, me, a=a: (me[0], tile_of[a](i), 0))
                       for a in range(n)]),
        out_shape=[jax.ShapeDtypeStruct((N_SHARD,) + s.shape, dt) for s, dt in zip(shards, dtypes)],
        compiler_params=_params(1),
    )(me_idx, *shards)


HBM_SPEC = pl.BlockSpec(memory_space=pltpu.HBM)
SEM_SPEC = pl.BlockSpec(memory_space=pltpu.SEMAPHORE)
DATAFLOW = pltpu.SideEffectType.DATAFLOW_SIDE_EFFECTING


def _exchange_copies(srcs, lands, send, recv, gather):
    x, y, c, chips = _position()
    me = 2 * x + y
    out = []
    for a in range(len(lands)):
        dst = lands[a].at[me]
        if gather and _halved(lands[a]):
            half = lands[a].shape[1] // 2
            dst = lands[a].at[me, pl.ds(c * half, half), :]
        for k, (px, py) in enumerate(chips):
            src = dst if gather else srcs[a].at[2 * px + py]
            out.append(pltpu.make_async_remote_copy(src_ref=src, dst_ref=dst, send_sem=send.at[3 * a + k],
                                                    recv_sem=recv.at[3 * a + k], device_id=(px, py, c), device_id_type=MESH))
    return out


def _halved(land):
    return land.shape[1] % 32 == 0


def _pair_fill(name, lands):
    n = len(lands)

    def body(*refs):
        src, dst = refs[:n], refs[n:2 * n]
        send, recv = refs[2 * n:]
        x, y, c, chips = _position()
        copies = []
        for a in range(n):
            half = src[a].shape[1] // 2
            for k, (px, py) in enumerate(chips):
                rows = (2 * px + py, pl.ds(c * half, half), slice(None))
                cp = pltpu.make_async_remote_copy(src_ref=src[a].at[rows], dst_ref=dst[a].at[rows], send_sem=send.at[a, k],
                                                  recv_sem=recv.at[a, k], device_id=(x, y, 1 - c), device_id_type=MESH)
                cp.start()
                copies.append(cp)
        for cp in copies:
            cp.wait()

    return pl.pallas_call(
        body, name=name, in_specs=[ANY] * n, out_specs=[ANY] * n,
        out_shape=[jax.ShapeDtypeStruct(l.shape, l.dtype) for l in lands],
        input_output_aliases={a: a for a in range(n)},
        scratch_shapes=[pltpu.SemaphoreType.DMA((n, 3)), pltpu.SemaphoreType.DMA((n, 3))],
    )(*lands)


def _exchange_start(name, srcs, lands, after):
    ns, nl, na = len(srcs), len(lands), len(after)
    gather = ns == 0

    def body(*refs):
        src_refs, land_refs = refs[:ns], refs[ns:ns + nl]
        send, recv = refs[ns + nl + na], refs[ns + nl + na + 1]
        token = refs[-1]
        for cp in _exchange_copies(src_refs, land_refs, send, recv, gather):
            cp.start()
        token[...] = jnp.zeros_like(token)

    arrays = [pltpu.with_memory_space_constraint(a, pltpu.HBM) for a in list(srcs) + list(lands)]
    outs = pl.pallas_call(
        body, name=name,
        out_shape=(pltpu.SemaphoreType.DMA((3 * nl,)), pltpu.SemaphoreType.DMA((3 * nl,)),
                   *[pltpu.HBM(a.shape, a.dtype) for a in arrays], jax.ShapeDtypeStruct((8, LANES), F32)),
        in_specs=[HBM_SPEC] * (ns + nl) + [ANY] * na,
        out_specs=(SEM_SPEC, SEM_SPEC, *[HBM_SPEC] * (ns + nl), pl.BlockSpec(memory_space=pltpu.VMEM)),
        input_output_aliases={i: 2 + i for i in range(ns + nl)},
        compiler_params=pltpu.CompilerParams(has_side_effects=DATAFLOW),
    )(*arrays, *after)
    return outs[0], outs[1], list(outs[2:2 + ns]), list(outs[2 + ns:2 + ns + nl]), outs[-1]


def _exchange_wait(name, send, recv, srcs, lands, after):
    ns, nl, na = len(srcs), len(lands), len(after)
    gather = ns == 0

    def body(*refs):
        src_refs, land_refs = refs[:ns], refs[ns:ns + nl]
        send_ref, recv_ref = refs[ns + nl], refs[ns + nl + 1]
        for cp in _exchange_copies(src_refs, land_refs, send_ref, recv_ref, gather):
            cp.wait_send()
            cp.wait_recv()

    outs = pl.pallas_call(
        body, name=name,
        out_shape=tuple(pltpu.HBM(a.shape, a.dtype) for a in list(srcs) + list(lands)),
        in_specs=[HBM_SPEC] * (ns + nl) + [SEM_SPEC, SEM_SPEC] + [ANY] * na,
        out_specs=tuple([HBM_SPEC] * (ns + nl)),
        input_output_aliases={i: i for i in range(ns + nl)},
        compiler_params=pltpu.CompilerParams(has_side_effects=DATAFLOW),
    )(*srcs, *lands, send, recv, *after)
    return list(outs[ns:])


def _pair_exchange(grads):
    n = len(grads)

    def body(*refs):
        src, dst = refs[:n], refs[n:2 * n]
        send, recv = refs[2 * n:]
        x, y, c, _ = _position()
        copies = []
        for a in range(n):
            half = src[a].shape[1] // 2
            cp = pltpu.make_async_remote_copy(
                src_ref=src[a].at[:, pl.ds((1 - c) * half, half), :], dst_ref=dst[a], send_sem=send.at[a],
                recv_sem=recv.at[a], device_id=(x, y, 1 - c), device_id_type=MESH)
            cp.start()
            copies.append(cp)
        for cp in copies:
            cp.wait()

    return pl.pallas_call(
        body, name="grad_pair_exchange", in_specs=[ANY] * n, out_specs=[ANY] * n,
        out_shape=[jax.ShapeDtypeStruct((g.shape[0], g.shape[1] // 2, g.shape[2]), g.dtype) for g in grads],
        scratch_shapes=[pltpu.SemaphoreType.DMA((n,)), pltpu.SemaphoreType.DMA((n,))],
    )(*grads)


def _shard_of(a):
    return lambda i: jnp.clip(i - a * N_SHARD, 0, N_SHARD - 1)


def _pair_sum(gs, gots, c_idx, me_idx):
    n = len(gs)
    halves = [(g.shape[1] // 2, g.shape[2]) for g in gs]

    def body(c_ref, me_ref, *refs):
        g_refs, got_refs, s_refs, own_refs = (refs[k * n:(k + 1) * n] for k in range(4))
        i = pl.program_id(0)
        for a in range(n):
            @pl.when(i // N_SHARD == a)
            def _(a=a):
                sm = g_refs[a][...] + got_refs[a][...].astype(F32)
                s_refs[a][...] = sm.astype(CD)

                @pl.when(i % N_SHARD == me_ref[0])
                def _():
                    own_refs[a][...] = sm[0]

    shard = [_shard_of(a) for a in range(n)]
    return pl.pallas_call(
        body, name="grad_pair_sum",
        grid_spec=pltpu.PrefetchScalarGridSpec(
            num_scalar_prefetch=2, grid=(n * N_SHARD,),
            in_specs=[pl.BlockSpec((1, h, c_), lambda i, c, me, a=a: (shard[a](i), c[0], 0)) for a, (h, c_) in enumerate(halves)]
            + [pl.BlockSpec((1, h, c_), lambda i, c, me, a=a: (shard[a](i), 0, 0)) for a, (h, c_) in enumerate(halves)],
            out_specs=[pl.BlockSpec((1, h, c_), lambda i, c, me, a=a: (shard[a](i), 0, 0)) for a, (h, c_) in enumerate(halves)]
            + [pl.BlockSpec((h, c_), lambda i, c, me: (0, 0)) for h, c_ in halves]),
        out_shape=[jax.ShapeDtypeStruct((N_SHARD, h, c_), CD) for h, c_ in halves]
        + [jax.ShapeDtypeStruct((h, c_), F32) for h, c_ in halves],
        compiler_params=_params(1),
    )(c_idx, me_idx, *gs, *gots)


def _chip_sum(owns, gots, me_idx):
    n = len(owns)

    def body(me_ref, *refs):
        own_refs, got_refs, out_refs = (refs[k * n:(k + 1) * n] for k in range(3))
        i = pl.program_id(0)
        j = i % N_SHARD
        for a in range(n):
            @pl.when(i // N_SHARD == a)
            def _(a=a):
                term = jnp.where(j == me_ref[0], own_refs[a][...], got_refs[a][0].astype(F32))

                @pl.when(j == 0)
                def _():
                    out_refs[a][...] = term

                @pl.when(j > 0)
                def _():
                    out_refs[a][...] += term

    shard = [_shard_of(a) for a in range(n)]
    whole = [pl.BlockSpec(o.shape, lambda i, me: (0, 0)) for o in owns]
    return pl.pallas_call(
        body, name="grad_chip_sum",
        grid_spec=pltpu.PrefetchScalarGridSpec(
            num_scalar_prefetch=1, grid=(n * N_SHARD,),
            in_specs=whole + [pl.BlockSpec((1,) + o.shape, lambda i, me, a=a: (shard[a](i), 0, 0)) for a, o in enumerate(owns)],
            out_specs=whole),
        out_shape=[jax.ShapeDtypeStruct(o.shape, F32) for o in owns],
        compiler_params=_params(1),
    )(me_idx, *owns, *gots)


def _pair_share(halves):
    n = len(halves)

    def body(*refs):
        src, dst = refs[:n], refs[n:2 * n]
        send, recv = refs[2 * n:]
        x, y, c, _ = _position()
        copies = []
        for a in range(n):
            cp = pltpu.make_async_remote_copy(src_ref=src[a], dst_ref=dst[a], send_sem=send.at[a],
                                              recv_sem=recv.at[a], device_id=(x, y, 1 - c), device_id_type=MESH)
            cp.start()
            copies.append(cp)
        for cp in copies:
            cp.wait()

    return pl.pallas_call(
        body, name="grad_pair_share", in_specs=[ANY] * n, out_specs=[ANY] * n,
        out_shape=[jax.ShapeDtypeStruct(h.shape, h.dtype) for h in halves],
        scratch_shapes=[pltpu.SemaphoreType.DMA((n,)), pltpu.SemaphoreType.DMA((n,))],
    )(*halves)


def _small_allreduce(buf):
    rows, cols = buf.shape

    def body(src_ref, out_ref, slots, send, recv):
        x, y, c, _ = _position()
        me = 4 * x + 2 * y + c
        slots[me] = src_ref[...]
        copies = []
        k = 0
        for dx in (0, 1):
            for dy in (0, 1):
                for dc in (0, 1):
                    if (dx, dy, dc) == (0, 0, 0):
                        continue
                    peer = (jnp.where(dx, 1 - x, x), jnp.where(dy, 1 - y, y), jnp.where(dc, 1 - c, c))
                    cp = pltpu.make_async_remote_copy(src_ref=src_ref, dst_ref=slots.at[me], send_sem=send.at[k],
                                                      recv_sem=recv.at[k], device_id=peer, device_id_type=MESH)
                    cp.start()
                    copies.append(cp)
                    k += 1
        for cp in copies:
            cp.wait()
        total = slots[0]
        for dev in range(1, N_DEV):
            total = total + slots[dev]
        out_ref[...] = total

    vm = pl.BlockSpec(memory_space=pltpu.VMEM)
    return pl.pallas_call(
        body, name="small_allreduce", in_specs=[vm], out_specs=vm,
        out_shape=jax.ShapeDtypeStruct((rows, cols), F32),
        scratch_shapes=[pltpu.VMEM((N_DEV, rows, cols), F32), pltpu.SemaphoreType.DMA((N_DEV - 1,)),
                        pltpu.SemaphoreType.DMA((N_DEV - 1,))],
    )(buf)


def _adamw_math(w, gv, m, v):
    mn = ADAM_B1 * m + (1.0 - ADAM_B1) * gv
    vn = ADAM_B2 * v + (1.0 - ADAM_B2) * (gv * gv)
    m_hat = mn / (1.0 - ADAM_B1 ** ADAM_STEP)
    v_hat = vn / (1.0 - ADAM_B2 ** ADAM_STEP)
    return -ADAM_LR * (m_hat / (jnp.sqrt(v_hat) + ADAM_EPS) + ADAM_WD * w), mn, vn


def _adamw(w, g, m, v):
    rows, cols = w.shape
    tr = _row_tile(rows)

    def body(w_ref, g_ref, m_ref, v_ref, d_ref, mo_ref, vo_ref):
        d_ref[...], mo_ref[...], vo_ref[...] = _adamw_math(w_ref[...], g_ref[...], m_ref[...], v_ref[...])

    blk = pl.BlockSpec((tr, cols), lambda i: (i, 0))
    shp = jax.ShapeDtypeStruct((rows, cols), F32)
    return pl.pallas_call(
        body, name="adamw", grid=(rows // tr,), in_specs=[blk] * 4, out_specs=[blk] * 3, out_shape=[shp] * 3,
        compiler_params=_params(1),
    )(w, g, m, v)


def _adamw_halves(w, own, got, m, v, c_idx):
    rows, cols = w.shape
    tr = _row_tile(rows // 2)
    per_half = rows // 2 // tr

    def body(c_ref, w_ref, own_ref, got_ref, m_ref, v_ref, d_ref, mo_ref, vo_ref, g_ref):
        mine = (pl.program_id(0) // per_half) == c_ref[0]
        gv = jnp.where(mine, own_ref[...], got_ref[...])
        g_ref[...] = gv
        d_ref[...], mo_ref[...], vo_ref[...] = _adamw_math(w_ref[...], gv, m_ref[...], v_ref[...])

    blk = pl.BlockSpec((tr, cols), lambda i, c: (i, 0))
    own_blk = pl.BlockSpec((tr, cols), lambda i, c: (jnp.where(i // per_half == c[0], i % per_half, 0), 0))
    got_blk = pl.BlockSpec((tr, cols), lambda i, c: (jnp.where(i // per_half == c[0], 0, i % per_half), 0))
    shp = jax.ShapeDtypeStruct((rows, cols), F32)
    return pl.pallas_call(
        body, name="adamw_halves",
        grid_spec=pltpu.PrefetchScalarGridSpec(num_scalar_prefetch=1, grid=(rows // tr,),
                                               in_specs=[blk, own_blk, got_blk, blk, blk], out_specs=[blk] * 4),
        out_shape=[shp] * 4, compiler_params=_params(1),
    )(c_idx, w, own, got, m, v)


def _local_step(x, target, norm_gains, q_g, k_g, ng, weights_of, grads_done):
    s = x.shape[0]
    tm = min(512, s)
    tq = min(256, s)
    tf = min(1024, s)
    g1, gm, g2, gf = norm_gains
    cos2, sin2 = _rope_tables(s)
    gq8 = jnp.tile(q_g, (1, 8))
    gk2 = jnp.tile(k_g, (1, 2))

    tn = min(256, s)
    tk = min(1024, s)
    w1 = weights_of(1, ())
    x1, s1, t1, b1, h1 = _ffn_fwd(x, g1, w1["g1"], w1["u1"], w1["d1"], tf)
    w2 = weights_of(2, (x1,))
    lbl = w2["lbl"]
    pqkv, ph, pg, hm = _mix_in_fwd(x1, gm, w2["in"], tn)
    qe, kr, vr, vs = _qk_prep(pqkv, gq8, gk2, cos2, sin2, tm)
    oa, lse = _attn_fwd(qe, kr, vr, vs, tq)
    ob, pre, hstates = _hgrn_fwd(ph, lbl, ng)
    x2 = _mix_out_fwd(x1, oa, ob, pg, w2["a"], w2["b"], w2["o"], tm)
    w3 = weights_of(3, (x2,))
    x3, s2, t2, b2, h2 = _ffn_fwd(x2, g2, w3["g2"], w3["u2"], w3["d2"], tf)
    dx3, loss, dgf = _loss_head(x3, gf, target, tm)

    dx2, da2, db2, f2, dg2, dx3c = _ffn_bwd(dx3, x2, g2, s2, t2, b2, w3["g2"], w3["u2"], w3["d2"], tm)
    tok = grads_done(3, dict(g2=_dw_shared_b("dw_gate", da2, h2, tk, 1.0), u2=_dw_shared_b("dw_gate", db2, h2, tk, 1.0),
                             d2=_dw_shared_b("dw_down", f2, dx3c, tk, 0.5)))

    dpg, mg, dya, dyb, doe, delta, dob = _mix_out_bwd(dx2, oa, ob, pg, w2["a"], w2["b"], w2["o"], tm, tok)
    g_o = [g.reshape(N_SHARD, D_MODEL // N_SHARD, D_MODEL) for g in _dw_colblocks("dw_out", mg, dx2, 1, tk)]
    g_a = _dw_colblocks("dw_branch", oa, dya, N_SHARD, tk)
    g_b = _dw_colblocks("dw_branch", ob, dyb, N_SHARD, tk)
    dqt, dkt, dvt = _attn_bwd(qe, kr, kr.T, vr, doe, delta, lse, tq)
    dqkv, dgq, dgk = _qk_prep_bwd(pqkv, dqt.T, dkt.T, dvt.T, gq8, gk2, cos2, sin2, tm)
    dhq, dhff, dhfb, dhi, dhg, dlb, dng = _hgrn_bwd(ph, pre, dob, hstates, lbl, ng)
    dps = (dqkv, dhq, dhff, dhfb, dhi, dhg, dpg)
    g_in = [g.reshape(N_SHARD, -1, D_MODEL) for g in _dw_in(dps, hm, tk)]
    tok = grads_done(2, {"in": g_in, "a": g_a, "b": g_b, "o": g_o})
    dx1, dgm = _mix_in_bwd(dps, w2["in"], x1, dx2, gm, tn, tok)

    dx0, da1, db1, f1, dg1, dx1c = _ffn_bwd(dx1, x, g1, s1, t1, b1, w1["g1"], w1["u1"], w1["d1"], tm)
    grads_done(1, dict(g1=_dw_shared_b("dw_gate", da1, h1, tk, 1.0), u1=_dw_shared_b("dw_gate", db1, h1, tk, 1.0),
                       d1=_dw_shared_b("dw_down", f1, dx1c, tk, 0.5)))
    small = dict(g1=dg1, gm=dgm, g2=dg2, gf=dgf, gq=dgq, gk=dgk, lb=dlb, ng=dng)
    return loss, dx0, small, lbl


GROUPS = {1: ("g1", "u1", "d1"), 2: ("in", "a", "b", "o"), 3: ("g2", "u2", "d2")}
BIG = GROUPS[1] + GROUPS[2] + GROUPS[3]
TRANSPOSED = ("g1", "u1", "in", "g2", "u2")


def _pack_rows(vectors, width):
    rows = []
    for vct in vectors:
        flat = vct.reshape(-1)
        pad = (-flat.shape[0]) % width
        rows.append(jnp.pad(flat, (0, pad)).reshape(-1, width))
    return jnp.concatenate(rows, axis=0)


def kernel(x, ffn1_norm_g, ffn1_w_gate, ffn1_w_up, ffn1_w_down, mix_norm_g, w_in, q_norm_g, k_norm_g, hgrn_lb_logits, hgrn_out_norm_g, w_branch_attn, w_branch_hgrn, w_out, ffn2_norm_g, ffn2_w_gate, ffn2_w_up, ffn2_w_down, final_norm_g, loss_target, m_ffn1_norm_g, m_ffn1_w_gate, m_ffn1_w_up, m_ffn1_w_down, m_mix_norm_g, m_w_in, m_q_norm_g, m_k_norm_g, m_hgrn_lb_logits, m_hgrn_out_norm_g, m_w_branch_attn, m_w_branch_hgrn, m_w_out, m_ffn2_norm_g, m_ffn2_w_gate, m_ffn2_w_up, m_ffn2_w_down, m_final_norm_g, v_ffn1_norm_g, v_ffn1_w_gate, v_ffn1_w_up, v_ffn1_w_down, v_mix_norm_g, v_w_in, v_q_norm_g, v_k_norm_g, v_hgrn_lb_logits, v_hgrn_out_norm_g, v_w_branch_attn, v_w_branch_hgrn, v_w_out, v_ffn2_norm_g, v_ffn2_w_gate, v_ffn2_w_up, v_ffn2_w_down, v_final_norm_g):
    xi, yi, ci = lax.axis_index("x"), lax.axis_index("y"), lax.axis_index("c")
    me = 2 * xi + yi
    c_idx = jnp.reshape(ci, (1,)).astype(jnp.int32)
    me_idx = jnp.reshape(me, (1,)).astype(jnp.int32)

    big_w = dict(g1=ffn1_w_gate[0], u1=ffn1_w_up[0], d1=ffn1_w_down[0], a=w_branch_attn[0], b=w_branch_hgrn[0],
                 o=w_out[0], g2=ffn2_w_gate[0], u2=ffn2_w_up[0], d2=ffn2_w_down[0])
    big_w["in"] = w_in[0]
    big_m = dict(g1=m_ffn1_w_gate[0], u1=m_ffn1_w_up[0], d1=m_ffn1_w_down[0], a=m_w_branch_attn[0], b=m_w_branch_hgrn[0],
                 o=m_w_out[0], g2=m_ffn2_w_gate[0], u2=m_ffn2_w_up[0], d2=m_ffn2_w_down[0])
    big_m["in"] = m_w_in[0]
    big_v = dict(g1=v_ffn1_w_gate[0], u1=v_ffn1_w_up[0], d1=v_ffn1_w_down[0], a=v_w_branch_attn[0], b=v_w_branch_hgrn[0],
                 o=v_w_out[0], g2=v_ffn2_w_gate[0], u2=v_ffn2_w_up[0], d2=v_ffn2_w_down[0])
    big_v["in"] = v_w_in[0]
    for table in (big_w, big_m, big_v):
        for n in TRANSPOSED:
            table[n] = table[n].T

    started, token = {}, ()
    for grp in (1, 2, 3):
        shards = [big_w[n] for n in GROUPS[grp]] + ([hgrn_lb_logits.reshape(4, LANES)] if grp == 2 else [])
        dtypes = [CD] * len(GROUPS[grp]) + ([F32] if grp == 2 else [])
        lands = _cast_into_slots(shards, dtypes, me_idx)
        send, recv, _, lands, tok = _exchange_start("gather%d_start" % grp, [], lands, token)
        started[grp], token = (send, recv, lands), (tok,)

    def weights_of(grp, after):
        send, recv, lands = started[grp]
        got = _exchange_wait("gather%d_wait" % grp, send, recv, [], lands, tuple(after) + (token if grp == 1 else ()))
        by_halves = [i for i, land in enumerate(got) if _halved(land)]
        for i, whole in zip(by_halves, _pair_fill("gather%d_fill" % grp, [got[i] for i in by_halves])):
            got[i] = whole
        w = dict(zip(GROUPS[grp], got))
        if grp == 2:
            w["in"] = w["in"].reshape(-1, D_MODEL)
            w["o"] = w["o"].reshape(D_MODEL, D_MODEL)
            w["lbl"] = jnp.transpose(got[-1], (1, 0, 2)).reshape(4, N_SHARD * LANES)
        return w

    pending = {}

    def grads_done(grp, grads):
        names = list(grads)
        got = _pair_exchange([grads[n][1] for n in names])
        res = _pair_sum([grads[n][0] for n in names], got, c_idx, me_idx)
        sums, owns = res[:len(names)], res[len(names):]
        lands = [lax.empty(s_.shape, s_.dtype) for s_ in sums]
        send, recv, srcs, lands, tok = _exchange_start("reduce%d_start" % grp, list(sums), lands, ())
        pending[grp] = (names, send, recv, srcs, lands, owns, tok)
        return (tok,)

    def reduced_halves(grp, after):
        names, send, recv, srcs, lands, owns, _ = pending[grp]
        parts = _exchange_wait("reduce%d_wait" % grp, send, recv, srcs, lands, after)
        return names, list(_chip_sum(list(owns), parts, me_idx))

    loss, dx, small, lbl = _local_step(
        x[0], loss_target[0], (ffn1_norm_g, mix_norm_g, ffn2_norm_g, final_norm_g.reshape(1, -1)),
        q_norm_g, k_norm_g, hgrn_out_norm_g, weights_of, grads_done)

    dgq = small["gq"].reshape(8, HEAD_DIM).sum(axis=0)
    dgk = small["gk"].reshape(2, HEAD_DIM).sum(axis=0)
    lb_full = _hgrn_lower_bounds(lbl)
    dlog = []
    for d in (0, 1):
        t = small["lb"][d:d + 1] * lb_full[d] * (1.0 - lb_full[d])
        dlog += [t, -t]
    small_list = [small["g1"], small["gm"], small["g2"], small["gf"], small["ng"], dgq, dgk, jnp.concatenate(dlog, axis=0), loss[0, 0]]
    packed = _pack_rows(small_list, D_MODEL)
    n_rows = packed.shape[0]
    packed = jnp.pad(packed, ((0, (-n_rows) % 8), (0, 0)))
    red = _small_allreduce(packed)
    loss_out = red[n_rows - 1, 0]
    sg = dict(g1=red[0:1], gm=red[1:2], g2=red[2:3], gf=red[3], ng=red[4:5, :512], gq=red[5:6, :HEAD_DIM],
              gk=red[6:7, :HEAD_DIM])
    dlog_full = red[7:9].reshape(2, 2, 512)
    sg["lb"] = lax.dynamic_slice_in_dim(dlog_full, me * LANES, LANES, axis=2)

    small_w = dict(g1=ffn1_norm_g, gm=mix_norm_g, g2=ffn2_norm_g, gf=final_norm_g, ng=hgrn_out_norm_g, gq=q_norm_g,
                   gk=k_norm_g, lb=hgrn_lb_logits)
    small_m = dict(g1=m_ffn1_norm_g, gm=m_mix_norm_g, g2=m_ffn2_norm_g, gf=m_final_norm_g, ng=m_hgrn_out_norm_g,
                   gq=m_q_norm_g, gk=m_k_norm_g, lb=m_hgrn_lb_logits)
    small_v = dict(g1=v_ffn1_norm_g, gm=v_mix_norm_g, g2=v_ffn2_norm_g, gf=v_final_norm_g, ng=v_hgrn_out_norm_g,
                   gq=v_q_norm_g, gk=v_k_norm_g, lb=v_hgrn_lb_logits)
    small_names = ("g1", "gm", "g2", "gf", "ng", "gq", "gk", "lb")
    pack = lambda dct: _pack_rows([dct[n] for n in small_names], D_MODEL)
    pw, pgr, pm, pv = pack(small_w), pack(sg), pack(small_m), pack(small_v)
    pad8 = lambda a: jnp.pad(a, ((0, (-a.shape[0]) % 8), (0, 0)))
    sd, sm_, sv_ = _adamw(pad8(pw), pad8(pgr), pad8(pm), pad8(pv))

    def unpack(buf):
        out, r = {}, 0
        for n in small_names:
            size = small_w[n].size
            nr = -(-size // D_MODEL)
            out[n] = buf[r:r + nr].reshape(-1)[:size].reshape(small_w[n].shape)
            r += nr
        return out

    sdelta, snew_m, snew_v = unpack(sd), unpack(sm_), unpack(sv_)
    sgrad = {n: sg[n].reshape(small_w[n].shape) for n in small_names}

    bdelta, bnew_m, bnew_v, bgrad = {}, {}, {}, {}

    def update(names, halves):
        for n, own, got in zip(names, halves, _pair_share(halves)):
            res = _adamw_halves(big_w[n], own, got, big_m[n], big_v[n], c_idx)
            if n in TRANSPOSED:
                res = [r.T for r in res]
            bdelta[n], bnew_m[n], bnew_v[n], bgrad[n] = [r[None] for r in res]

    names3, halves3 = reduced_halves(3, (pending[1][-1],))
    names2, halves2 = reduced_halves(2, (halves3[0],))
    update(names3 + names2, halves3 + halves2)
    names1, halves1 = reduced_halves(1, (bdelta[names2[-1]],))
    update(names1, halves1)

    order = [("s", "g1"), ("b", "g1"), ("b", "u1"), ("b", "d1"), ("s", "gm"), ("b", "in"), ("s", "gq"), ("s", "gk"),
             ("s", "lb"), ("s", "ng"), ("b", "a"), ("b", "b"), ("b", "o"), ("s", "g2"), ("b", "g2"), ("b", "u2"),
             ("b", "d2"), ("s", "gf")]
    outs = [loss_out, dx[None]]
    for table_s, table_b in ((sgrad, bgrad), (sdelta, bdelta), (snew_m, bnew_m), (snew_v, bnew_v)):
        outs += [(table_s if kind == "s" else table_b)[n] for kind, n in order]
    return tuple(outs)
```

```python
import functools

import numpy as np
import jax
import jax.numpy as jnp
from jax import lax
from jax.experimental import pallas as pl
from jax.experimental.pallas import tpu as pltpu

F32 = jnp.float32
BF16 = jnp.bfloat16
CD = jnp.bfloat16

EPS = 1e-6
D_MODEL = 1024
HEAD_DIM = 64
GRID_W = 64
ROPE_THETA = 10000.0
CHUNK = 32
N_SHARD = 4
N_DEV = 8
VMEM_LIMIT = 56 * 1024 * 1024
LANES = 128
HG_TILE = 256
ATTN_BWD_HEADS = 2
FFN_ROWS = 256

ADAM_LR = 0.001
ADAM_B1 = 0.9
ADAM_B2 = 0.999
ADAM_EPS = 1e-08
ADAM_WD = 0.01
ADAM_STEP = 10

NN = (((1,), (0,)), ((), ()))
NT = (((1,), (1,)), ((), ()))
TN = (((0,), (0,)), ((), ()))
MESH = pl.DeviceIdType.MESH
ANY = pl.BlockSpec(memory_space=pl.ANY)


def _mm(a, b, dn):
    return lax.dot_general(a.astype(CD), b.astype(CD), dn, preferred_element_type=F32)


def _split3(x):
    hi = x.astype(BF16)
    r = x - hi.astype(F32)
    mid = r.astype(BF16)
    lo = (r - mid.astype(F32)).astype(BF16)
    return hi, mid, lo


def _xdot(x, m):
    rows = x.shape[0]
    hi, mid, _ = _split3(x)
    r = lax.dot_general(jnp.concatenate([hi, mid], axis=0), m, NN, preferred_element_type=F32)
    return r[:rows] + r[rows:]


def _xdot_l(m, x):
    cols = x.shape[1]
    hi, mid, _ = _split3(x)
    r = lax.dot_general(m, jnp.concatenate([hi, mid], axis=1), NN, preferred_element_type=F32)
    return r[:, :cols] + r[:, cols:]


def _params(n_grid):
    return pltpu.CompilerParams(dimension_semantics=("arbitrary",) * n_grid, vmem_limit_bytes=VMEM_LIMIT)


def _sigmoid(x):
    return jax.nn.sigmoid(x)


def _np_blocksum(n):
    i = np.arange(n)
    return (i[:, None] // HEAD_DIM == i[None, :] // HEAD_DIM).astype(np.float32)


def _np_swap32(n):
    i = np.arange(n)
    partner = np.where(i % HEAD_DIM < HEAD_DIM // 2, i + HEAD_DIM // 2, i - HEAD_DIM // 2)
    m = np.zeros((n, n), np.float32)
    m[i, partner] = 1.0
    return m


def _np_expand_q():
    m = np.zeros((512, 1024), np.float32)
    for h in range(8):
        g = h // 4
        for d in range(HEAD_DIM):
            m[64 * h + d, 128 * h + 64 * g + d] = 1.0
    return m


def _np_headsum_spread():
    m = np.zeros((512, 1024), np.float32)
    for h in range(8):
        m[64 * h:64 * h + 64, 128 * h:128 * h + 128] = 1.0
    return m


def _np_swap_halves():
    m = np.zeros((128, 128), np.float32)
    i = np.arange(128)
    m[i, (i + 64) % 128] = 1.0
    return m


def _np_hgrn_cums(t, rev):
    r = np.arange(t)[:, None]
    c = np.arange(t)[None, :]
    same = (r // CHUNK) == (c // CHUNK)
    if not rev:
        cum = same & (c <= r)
        mid = same & (c % CHUNK <= CHUNK // 2 - 1)
    else:
        cum = same & (c >= r)
        mid = same & (c % CHUNK >= CHUNK // 2)
    return np.concatenate([cum, mid, same], axis=0).astype(np.float32)


def _bf(a):
    return jnp.asarray(a, dtype=BF16)


def _rope_tables(seq_len):
    rows = seq_len // GRID_W
    row = jnp.repeat(jnp.arange(rows, dtype=F32), GRID_W)
    col = jnp.tile(jnp.arange(GRID_W, dtype=F32), rows)
    n_freq = HEAD_DIM // 4
    inv = ROPE_THETA ** (-jnp.arange(n_freq, dtype=F32) / n_freq)
    ang = jnp.concatenate([row[:, None] * inv, col[:, None] * inv], axis=-1)
    cos, sin = jnp.cos(ang), jnp.sin(ang)
    c64 = jnp.concatenate([cos, cos], axis=-1)
    s64 = jnp.concatenate([-sin, sin], axis=-1)
    return jnp.tile(c64, (1, 2)), jnp.tile(s64, (1, 2))


def _ffn_fwd(x, g, wg, wu, wd, tm):
    s, d = x.shape
    nsh, fs, _ = wg.shape

    def body(x_ref, g_ref, wg_ref, wu_ref, wd_ref, xo_ref, a_ref, da_ref, b_ref, hb_ref, acc, hs):
        j = pl.program_id(1)

        @pl.when(j == 0)
        def _():
            xv = x_ref[...]
            r = lax.rsqrt(jnp.mean(xv * xv, axis=-1, keepdims=True) + EPS)
            h = (xv * r * g_ref[...]).astype(CD)
            hs[...] = h
            hb_ref[...] = h
            acc[...] = jnp.zeros_like(acc)

        for r0 in range(0, tm, FFN_ROWS):
            rows = slice(r0, min(r0 + FFN_ROWS, tm))
            h = hs[rows, :]
            a = _mm(h, wg_ref[0], NT)
            b = _mm(h, wu_ref[0], NT)
            sg = _sigmoid(a)
            silu = a * sg
            acc[rows, :] += _mm(silu * b, wd_ref[0], NN)
            a_ref[0, rows, :] = silu.astype(CD)
            da_ref[0, rows, :] = (sg * (1.0 + a * (1.0 - sg))).astype(CD)
            b_ref[0, rows, :] = b.astype(CD)

        @pl.when(j == nsh - 1)
        def _():
            xo_ref[...] = x_ref[...] + 0.5 * acc[...]

    return pl.pallas_call(
        body, name="ffn_fwd", grid=(s // tm, nsh),
        in_specs=[pl.BlockSpec((tm, d), lambda i, j: (i, 0)), pl.BlockSpec((1, d), lambda i, j: (0, 0))]
        + [pl.BlockSpec((1, fs, d), lambda i, j: (j, 0, 0))] * 3,
        out_specs=[pl.BlockSpec((tm, d), lambda i, j: (i, 0))] + [pl.BlockSpec((1, tm, fs), lambda i, j: (j, i, 0))] * 3
        + [pl.BlockSpec((tm, d), lambda i, j: (i, 0))],
        out_shape=[jax.ShapeDtypeStruct((s, d), F32)] + [jax.ShapeDtypeStruct((nsh, s, fs), CD)] * 3
        + [jax.ShapeDtypeStruct((s, d), CD)],
        scratch_shapes=[pltpu.VMEM((tm, d), F32), pltpu.VMEM((tm, d), CD)],
        compiler_params=_params(2),
    )(x, g, wg, wu, wd)


def _ffn_bwd(dout, x, g, silu, dsilu, b, wg, wu, wd, tm):
    s, d = x.shape
    nsh, fs, _ = wg.shape

    def body(do_ref, x_ref, g_ref, sl_ref, ds_ref, b_ref, wg_ref, wu_ref, wd_ref,
             dx_ref, da_ref, db_ref, f_ref, dg_ref, do16_ref, dh):
        i = pl.program_id(0)
        j = pl.program_id(1)

        @pl.when(j == 0)
        def _():
            dh[...] = jnp.zeros_like(dh)
            do16_ref[...] = do_ref[...].astype(CD)

        @pl.when((i == 0) & (j == 0))
        def _():
            dg_ref[...] = jnp.zeros_like(dg_ref)

        for r0 in range(0, tm, FFN_ROWS):
            rows = slice(r0, min(r0 + FFN_ROWS, tm))
            sl = sl_ref[0, rows, :].astype(F32)
            bv = b_ref[0, rows, :].astype(F32)
            df = 0.5 * _mm(do_ref[rows, :], wd_ref[0], NT)
            da = df * bv * ds_ref[0, rows, :].astype(F32)
            db = df * sl
            dh[rows, :] += _mm(da, wg_ref[0], NN) + _mm(db, wu_ref[0], NN)
            da_ref[0, rows, :] = da.astype(CD)
            db_ref[0, rows, :] = db.astype(CD)
            f_ref[0, rows, :] = (sl * bv).astype(CD)

        @pl.when(j == nsh - 1)
        def _():
            xv = x_ref[...]
            r = lax.rsqrt(jnp.mean(xv * xv, axis=-1, keepdims=True) + EPS)
            dhv = dh[...]
            u = dhv * g_ref[...]
            dx_ref[...] = do_ref[...] + r * u - xv * (r * r * r) * jnp.mean(u * xv, axis=-1, keepdims=True)
            dg_ref[...] += jnp.sum(dhv * xv * r, axis=0, keepdims=True)

    act = pl.BlockSpec((1, tm, fs), lambda i, j: (j, i, 0))
    row = pl.BlockSpec((tm, d), lambda i, j: (i, 0))
    return pl.pallas_call(
        body, name="ffn_bwd", grid=(s // tm, nsh),
        in_specs=[row, row, pl.BlockSpec((1, d), lambda i, j: (0, 0)), act, act, act]
        + [pl.BlockSpec((1, fs, d), lambda i, j: (j, 0, 0))] * 3,
        out_specs=[row, act, act, act, pl.BlockSpec((1, d), lambda i, j: (0, 0)), row],
        out_shape=[jax.ShapeDtypeStruct((s, d), F32), jax.ShapeDtypeStruct((nsh, s, fs), CD),
                   jax.ShapeDtypeStruct((nsh, s, fs), CD), jax.ShapeDtypeStruct((nsh, s, fs), CD),
                   jax.ShapeDtypeStruct((1, d), F32), jax.ShapeDtypeStruct((s, d), CD)],
        scratch_shapes=[pltpu.VMEM((tm, d), F32)],
        compiler_params=_params(2),
    )(dout, x, g, silu, dsilu, b, wg, wu, wd)


def _tn_call(name, operands, in_specs, out_shape, out_spec, grid, acc_shape, pick, scale=1.0):
    nk = grid[-1]
    n_in = len(operands)

    def body(*refs):
        out_ref, out16_ref, acc = refs[n_in], refs[n_in + 1], refs[n_in + 2]
        k = pl.program_id(len(grid) - 1)

        @pl.when(k == 0)
        def _():
            acc[...] = jnp.zeros_like(acc)

        pick(refs[:n_in], acc)

        @pl.when(k == nk - 1)
        def _():
            res = (acc[...] if scale == 1.0 else acc[...] * scale).reshape(out_ref.shape)
            out_ref[...] = res
            out16_ref[...] = res.astype(CD)

    return pl.pallas_call(
        body, name=name, grid=grid, in_specs=in_specs, out_specs=[out_spec, out_spec],
        out_shape=[out_shape, jax.ShapeDtypeStruct(out_shape.shape, CD)],
        scratch_shapes=[pltpu.VMEM(acc_shape, F32)], compiler_params=_params(len(grid)),
    )(*operands)


def _dw_shared_b(name, a3, b, tk, scale):
    nj, s, m = a3.shape
    n = b.shape[1]

    def pick(refs, acc):
        rows = pl.ds(pl.multiple_of(pl.program_id(1) * tk, tk), tk)
        acc[...] += _mm(refs[0][0], refs[1][rows, :], TN)

    return _tn_call(name, (a3, b),
                    [pl.BlockSpec((1, tk, m), lambda j, k: (j, k, 0)), pl.BlockSpec((s, n), lambda j, k: (0, 0))],
                    jax.ShapeDtypeStruct((nj, m, n), F32), pl.BlockSpec((1, m, n), lambda j, k: (j, 0, 0)),
                    (nj, s // tk), (m, n), pick, scale)


def _dw_colblocks(name, a, b, nj, tk):
    s, m = a.shape
    n = b.shape[1] // nj
    nk = s // tk

    def body(a_ref, b_ref, out_ref, out16_ref, acc):
        k = pl.program_id(0)

        @pl.when(k == 0)
        def _():
            acc[...] = jnp.zeros_like(acc)

        acc[...] += _mm(a_ref[...], b_ref[...], TN)

        @pl.when(k == nk - 1)
        def _():
            for j in range(nj):
                res = acc[:, j * n:(j + 1) * n]
                out_ref[j] = res
                out16_ref[j] = res.astype(CD)

    whole = pl.BlockSpec((nj, m, n), lambda k: (0, 0, 0))
    return pl.pallas_call(
        body, name=name, grid=(nk,),
        in_specs=[pl.BlockSpec((tk, m), lambda k: (k, 0)), pl.BlockSpec((tk, nj * n), lambda k: (k, 0))],
        out_specs=[whole, whole],
        out_shape=[jax.ShapeDtypeStruct((nj, m, n), F32), jax.ShapeDtypeStruct((nj, m, n), CD)],
        scratch_shapes=[pltpu.VMEM((m, nj * n), F32)], compiler_params=_params(1),
    )(a, b)


DP_WIDTHS = (768, 512, 512, 512, 512, 512, 2048)
DW_IN_COLS = 512


def _dw_in(dps, hb, tk):
    s, d = hb.shape
    nk = s // tk
    blocks, row = [], 0
    for p, width in enumerate(DP_WIDTHS):
        step = width if width <= 768 else DW_IN_COLS
        for c0 in range(0, width, step):
            blocks.append((p, c0, step, row))
            row += step
    nb, max_w = len(blocks), max(b[2] for b in blocks)
    first = [min(i for i, b in enumerate(blocks) if b[0] == p) for p in range(len(DP_WIDTHS))]
    count = [sum(1 for b in blocks if b[0] == p) for p in range(len(DP_WIDTHS))]

    def body(*refs):
        dp_refs, hb_ref, out_ref, out16_ref, acc, acc16, sems = refs[:7], refs[7], refs[8], refs[9], refs[10], refs[11], refs[12]
        b, k = pl.program_id(0), pl.program_id(1)
        rows = pl.ds(pl.multiple_of(k * tk, tk), tk)

        def writes(i):
            _, _, w, r0 = blocks[i]
            slot = i % 2
            return (pltpu.make_async_copy(acc.at[slot, 0:w], out_ref.at[r0:r0 + w], sems.at[slot, 0]),
                    pltpu.make_async_copy(acc16.at[slot, 0:w], out16_ref.at[r0:r0 + w], sems.at[slot, 1]))

        for i, (p, _, w, _) in enumerate(blocks):
            @pl.when(b == i)
            def _(i=i, p=p, w=w):
                slot = i % 2
                prod = _mm(dp_refs[p][...], hb_ref[rows, :], TN)

                @pl.when(k == 0)
                def _():
                    acc[slot, 0:w] = prod

                @pl.when(k > 0)
                def _():
                    acc[slot, 0:w] += prod

                @pl.when(k == nk - 1)
                def _():
                    if i >= 1:
                        for cp in writes(i - 1):
                            cp.wait()
                    acc16[slot, 0:w] = acc[slot, 0:w].astype(CD)
                    for cp in writes(i):
                        cp.start()
                    if i == nb - 1:
                        for cp in writes(i):
                            cp.wait()

    def piece_spec(p):
        width = DP_WIDTHS[p]
        cols = width if width <= 768 else DW_IN_COLS

        def imap(b, k):
            active = (b >= first[p]) & (b < first[p] + count[p])
            return (jnp.where(active, k, jnp.where(b < first[p], 0, nk - 1)), jnp.clip(b - first[p], 0, count[p] - 1))

        return pl.BlockSpec((tk, cols), imap)

    return pl.pallas_call(
        body, name="dw_in", grid=(nb, nk),
        in_specs=[piece_spec(p) for p in range(len(DP_WIDTHS))] + [pl.BlockSpec((s, d), lambda b, k: (0, 0))],
        out_specs=[ANY, ANY],
        out_shape=[jax.ShapeDtypeStruct((sum(DP_WIDTHS), d), F32), jax.ShapeDtypeStruct((sum(DP_WIDTHS), d), CD)],
        scratch_shapes=[pltpu.VMEM((2, max_w, d), F32), pltpu.VMEM((2, max_w, d), CD), pltpu.SemaphoreType.DMA((2, 2))],
        compiler_params=_params(2),
    )(*dps, hb)


def _mix_in_fwd(x, g, w_t, tm):
    s, d = x.shape
    n_in = w_t.shape[0]

    def body(x_ref, g_ref, w_ref, qkv_ref, hg_ref, gt_ref, hb_ref):
        xv = x_ref[...]
        r = lax.rsqrt(jnp.mean(xv * xv, axis=-1, keepdims=True) + EPS)
        h = (xv * r * g_ref[...]).astype(CD)
        hb_ref[...] = h
        off = DP_WIDTHS[0]
        qkv_ref[...] = _mm(h, w_ref[0:off, :], NT)
        for c, width in enumerate(DP_WIDTHS[1:6]):
            hg_ref[:, c * width:(c + 1) * width] = _mm(h, w_ref[off:off + width, :], NT)
            off += width
        gate = DP_WIDTHS[6] // 2
        for c in range(2):
            gt_ref[:, c * gate:(c + 1) * gate] = _mm(h, w_ref[off:off + gate, :], NT)
            off += gate

    row = lambda w: pl.BlockSpec((tm, w), lambda i: (i, 0))
    return pl.pallas_call(
        body, name="mix_in_fwd", grid=(s // tm,),
        in_specs=[row(d), pl.BlockSpec((1, d), lambda i: (0, 0)), pl.BlockSpec((n_in, d), lambda i: (0, 0))],
        out_specs=[row(768), row(2560), row(2048), row(d)],
        out_shape=[jax.ShapeDtypeStruct((s, 768), F32), jax.ShapeDtypeStruct((s, 2560), F32),
                   jax.ShapeDtypeStruct((s, 2048), F32), jax.ShapeDtypeStruct((s, d), CD)],
        compiler_params=_params(1),
    )(x, g, w_t)


def _mix_in_bwd(dps, w_t, x, dres, g, tm, after=()):
    s, d = x.shape
    n_in = w_t.shape[0]

    def body(*refs):
        refs = refs[len(after):]
        dp_refs = refs[:7]
        w_ref, x_ref, dr_ref, g_ref, dx_ref, dg_ref = refs[7:]

        @pl.when(pl.program_id(0) == 0)
        def _():
            dg_ref[...] = jnp.zeros_like(dg_ref)

        dhv = jnp.zeros((tm, d), F32)
        off = 0
        for ref, width in zip(dp_refs, DP_WIDTHS):
            dhv = dhv + _mm(ref[...], w_ref[off:off + width, :], NN)
            off += width
        xv = x_ref[...]
        r = lax.rsqrt(jnp.mean(xv * xv, axis=-1, keepdims=True) + EPS)
        u = dhv * g_ref[...]
        dx_ref[...] = dr_ref[...] + r * u - xv * (r * r * r) * jnp.mean(u * xv, axis=-1, keepdims=True)
        dg_ref[...] += jnp.sum(dhv * xv * r, axis=0, keepdims=True)

    row = pl.BlockSpec((tm, d), lambda i: (i, 0))
    vec = pl.BlockSpec((1, d), lambda i: (0, 0))
    return pl.pallas_call(
        body, name="mix_in_bwd", grid=(s // tm,),
        in_specs=[ANY] * len(after) + [pl.BlockSpec((tm, w), lambda i: (i, 0)) for w in DP_WIDTHS]
        + [pl.BlockSpec((n_in, d), lambda i: (0, 0)), row, row, vec],
        out_specs=[row, vec],
        out_shape=[jax.ShapeDtypeStruct((s, d), F32), jax.ShapeDtypeStruct((1, d), F32)],
        compiler_params=_params(1),
    )(*after, *dps, w_t, x, dres, g)


def _headnorm_rope(x, gain, cos, sin, blocksum, swap):
    ss = _xdot(x * x, blocksum)
    r = lax.rsqrt(ss * (1.0 / HEAD_DIM) + EPS)
    y = x * r * gain
    return y * cos + _xdot(y, swap) * sin, r


def _headnorm_rope_bwd(dz, x, gain, cos, sin, blocksum, swap):
    ss = _xdot(x * x, blocksum)
    r = lax.rsqrt(ss * (1.0 / HEAD_DIM) + EPS)
    dy = dz * cos + _xdot(dz * sin, swap)
    u = dy * gain
    mean_ux = _xdot(u * x, blocksum) * (1.0 / HEAD_DIM)
    dx = r * u - x * (r * r * r) * mean_ux
    return dx, jnp.sum(dy * x * r, axis=0, keepdims=True)


def _qk_prep(pqkv, gq, gk, cos2, sin2, tm):
    s = pqkv.shape[0]
    bs512, sw512, eq, swh = _bf(_np_blocksum(512)), _bf(_np_swap32(512)), _bf(_np_expand_q()), _bf(_np_swap_halves())

    def body(q_ref, kv_ref, gq_ref, gk_ref, c_ref, s_ref, bs_ref, sw_ref, eq_ref, swh_ref, qe_ref, k_ref, v_ref, vs_ref):
        c2, s2 = c_ref[...], s_ref[...]
        c8, s8 = jnp.tile(c2, (1, 4)), jnp.tile(s2, (1, 4))
        bs, sw = bs_ref[...], sw_ref[...]
        zq, _ = _headnorm_rope(q_ref[...], gq_ref[...], c8, s8, bs, sw)
        qe_ref[...] = _mm(zq * (HEAD_DIM ** -0.5), eq_ref[...], NN).astype(CD)
        kv = kv_ref[...]
        zk, _ = _headnorm_rope(kv[:, :LANES], gk_ref[...], c2, s2, bs[:LANES, :LANES], sw[:LANES, :LANES])
        k_ref[...] = zk.astype(CD)
        v = kv[:, LANES:]
        v_ref[...] = v.astype(CD)
        vs_ref[...] = _mm(v, swh_ref[...], NN).astype(CD)

    full = lambda a: pl.BlockSpec(a.shape, lambda i: (0,) * a.ndim)
    tab = pl.BlockSpec((tm, LANES), lambda i: (i, 0))
    return pl.pallas_call(
        body, name="qk_prep", grid=(s // tm,),
        in_specs=[pl.BlockSpec((tm, 512), lambda i: (i, 0)), pl.BlockSpec((tm, 256), lambda i: (i, 2)),
                  full(gq), full(gk), tab, tab, full(bs512), full(sw512), full(eq), full(swh)],
        out_specs=[pl.BlockSpec((tm, 1024), lambda i: (i, 0)), tab, tab, tab],
        out_shape=[jax.ShapeDtypeStruct((s, 1024), CD)] + [jax.ShapeDtypeStruct((s, LANES), CD)] * 3,
        compiler_params=_params(1),
    )(pqkv, pqkv, gq, gk, cos2, sin2, bs512, sw512, eq, swh)


def _qk_prep_bwd(pqkv, dq, dk, dv, gq, gk, cos2, sin2, tm):
    s = pqkv.shape[0]
    bs512, sw512 = _bf(_np_blocksum(512)), _bf(_np_swap32(512))

    def body(q_ref, kv_ref, dq_ref, dk_ref, dv_ref, gq_ref, gk_ref, c_ref, s_ref, bs_ref, sw_ref,
             dp_ref, dgq_ref, dgk_ref):
        @pl.when(pl.program_id(0) == 0)
        def _():
            dgq_ref[...] = jnp.zeros_like(dgq_ref)
            dgk_ref[...] = jnp.zeros_like(dgk_ref)

        c2, s2 = c_ref[...], s_ref[...]
        c8, s8 = jnp.tile(c2, (1, 4)), jnp.tile(s2, (1, 4))
        bs, sw = bs_ref[...], sw_ref[...]
        dzq = dq_ref[...] * (HEAD_DIM ** -0.5)
        dxq, dgq = _headnorm_rope_bwd(dzq, q_ref[...], gq_ref[...], c8, s8, bs, sw)
        kv = kv_ref[...]
        dxk, dgk = _headnorm_rope_bwd(dk_ref[...], kv[:, :LANES], gk_ref[...], c2, s2, bs[:LANES, :LANES], sw[:LANES, :LANES])
        dp_ref[...] = jnp.concatenate([dxq, dxk, dv_ref[...]], axis=1).astype(CD)
        dgq_ref[...] += dgq
        dgk_ref[...] += dgk

    full = lambda a: pl.BlockSpec(a.shape, lambda i: (0,) * a.ndim)
    tab = pl.BlockSpec((tm, LANES), lambda i: (i, 0))
    return pl.pallas_call(
        body, name="qk_prep_bwd", grid=(s // tm,),
        in_specs=[pl.BlockSpec((tm, 512), lambda i: (i, 0)), pl.BlockSpec((tm, 256), lambda i: (i, 2)),
                  pl.BlockSpec((tm, 512), lambda i: (i, 0)), tab, tab, full(gq), full(gk), tab, tab,
                  full(bs512), full(sw512)],
        out_specs=[pl.BlockSpec((tm, 768), lambda i: (i, 0)), pl.BlockSpec((1, 512), lambda i: (0, 0)),
                   pl.BlockSpec((1, LANES), lambda i: (0, 0))],
        out_shape=[jax.ShapeDtypeStruct((s, 768), CD), jax.ShapeDtypeStruct((1, 512), F32),
                   jax.ShapeDtypeStruct((1, LANES), F32)],
        compiler_params=_params(1),
    )(pqkv, pqkv, dq, dk, dv, gq, gk, cos2, sin2, bs512, sw512)


def _kv_rows(h):
    return pl.ds(pl.multiple_of((h // 4) * HEAD_DIM, HEAD_DIM), HEAD_DIM)


def _attn_fwd(qe, k, v, vs, tq):
    s = k.shape[0]

    def body(q0_ref, q1_ref, q2_ref, q3_ref, k_ref, v_ref, vs_ref, o_ref, lse_ref):
        grp = pl.program_id(0)
        kk = k_ref[...]
        outs = []
        for r, q_ref in enumerate((q0_ref, q1_ref, q2_ref, q3_ref)):
            sc = _mm(q_ref[...], kk, NT)
            mx = jnp.max(sc, axis=-1, keepdims=True)
            e = jnp.exp(sc - mx)
            l = jnp.sum(e, axis=-1, keepdims=True)
            lse_ref[r] = mx + jnp.log(l)
            vsel = jnp.where(grp != r % 2, vs_ref[...], v_ref[...])
            outs.append(_mm(e, vsel, NN) * (1.0 / l))
        low = lax.broadcasted_iota(jnp.int32, (1, LANES), 1) < HEAD_DIM
        o_ref[...] = jnp.concatenate([jnp.where(low, outs[0], outs[1]), jnp.where(low, outs[2], outs[3])], axis=1)

    kv = pl.BlockSpec((s, LANES), lambda g, i: (0, 0))
    qblk = lambda r: pl.BlockSpec((tq, LANES), lambda g, i: (i, 4 * g + r))
    return pl.pallas_call(
        body, name="attn_fwd", grid=(2, s // tq),
        in_specs=[qblk(0), qblk(1), qblk(2), qblk(3), kv, kv, kv],
        out_specs=[pl.BlockSpec((tq, 2 * LANES), lambda g, i: (i, g)), pl.BlockSpec((4, tq, 1), lambda g, i: (g, i, 0))],
        out_shape=[jax.ShapeDtypeStruct((s, 512), F32), jax.ShapeDtypeStruct((8, s, 1), F32)],
        compiler_params=_params(2),
    )(qe, qe, qe, qe, k, v, vs)


def _attn_bwd(qe, k, kt, v, doe, delta, lse, tq):
    s = k.shape[0]

    nh = ATTN_BWD_HEADS

    def body(*refs):
        q_refs, (k_ref, kt_ref, v_ref) = refs[:nh], refs[nh:nh + 3]
        do_refs, dl_refs = refs[nh + 3:2 * nh + 3], refs[2 * nh + 3:3 * nh + 3]
        lse_ref, dqt_ref, dkt_ref, dvt_ref, qt, dot = refs[3 * nh + 3:]

        @pl.when((pl.program_id(0) == 0) & (pl.program_id(1) == 0))
        def _():
            dkt_ref[...] = jnp.zeros_like(dkt_ref)
            dvt_ref[...] = jnp.zeros_like(dvt_ref)

        rows = _kv_rows(nh * pl.program_id(0))
        kt = kt_ref[rows, :]
        dkt = jnp.zeros((HEAD_DIM, s), F32)
        dvt = jnp.zeros((HEAD_DIM, s), F32)
        for idx, (q_ref, do_ref, dl_ref) in enumerate(zip(q_refs, do_refs, dl_refs)):
            q, do = q_ref[...], do_ref[...]
            p = jnp.exp(_mm(q, k_ref[...], NT) - lse_ref[idx])
            dp = _mm(do, v_ref[...], NT)
            ds = p * (dp - jnp.max(dl_ref[...], axis=-1, keepdims=True))
            dqt_ref[idx * HEAD_DIM:(idx + 1) * HEAD_DIM, :] = _mm(kt, ds, NT)
            qt[idx] = jnp.transpose(q.astype(F32))
            dot[idx] = jnp.transpose(do.astype(F32))
            dkt = dkt + _mm(qt[idx, rows, :], ds, NN)
            dvt = dvt + _mm(dot[idx, rows, :], p, NN)
        dkt_ref[rows, :] += dkt
        dvt_ref[rows, :] += dvt

    kv = pl.BlockSpec((s, LANES), lambda m, i: (0, 0))
    kvt = pl.BlockSpec((LANES, s), lambda m, i: (0, 0))
    blks = [pl.BlockSpec((tq, LANES), lambda m, i, r=r: (i, nh * m + r)) for r in range(nh)]
    return pl.pallas_call(
        body, name="attn_bwd", grid=(8 // nh, s // tq),
        in_specs=blks + [kv, kvt, kv] + blks + blks + [pl.BlockSpec((nh, tq, 1), lambda m, i: (m, i, 0))],
        out_specs=[pl.BlockSpec((nh * HEAD_DIM, tq), lambda m, i: (m, i)), kvt, kvt],
        out_shape=[jax.ShapeDtypeStruct((8 * HEAD_DIM, s), F32), jax.ShapeDtypeStruct((LANES, s), F32),
                   jax.ShapeDtypeStruct((LANES, s), F32)],
        scratch_shapes=[pltpu.VMEM((nh, LANES, tq), F32), pltpu.VMEM((nh, LANES, tq), F32)],
        compiler_params=_params(2),
    )(*[qe] * nh, k, kt, v, *[doe] * nh, *[delta] * nh, lse)


@jax.custom_vjp
def _mm_nn(a, b):
    return _mm(a, b, NN)


_mm_nn.defvjp(lambda a, b: (_mm(a, b, NN), (a, b)),
              lambda res, g: (_mm(g, res[1], NT), _mm(res[0], g, TN)))


@jax.custom_vjp
def _mm_nt(a, b):
    return _mm(a, b, NT)


_mm_nt.defvjp(lambda a, b: (_mm(a, b, NT), (a, b)),
              lambda res, g: (_mm(g, res[1], NN), _mm(g, res[0], TN)))


@jax.custom_vjp
def _mm_tn(a, b):
    return _mm(a, b, TN)


_mm_tn.defvjp(lambda a, b: (_mm(a, b, TN), (a, b)),
              lambda res, g: (_mm(res[1], g, NT), _mm(res[0], g, NN)))


@jax.custom_vjp
def _cmm(m, mt, x):
    return _xdot_l(m, x)


_cmm.defvjp(lambda m, mt, x: (_xdot_l(m, x), (m, mt)),
            lambda res, g: (jnp.zeros_like(res[0]), jnp.zeros_like(res[1]), _xdot_l(res[1], g)))


def _hgrn_masks(t, rev):
    n_ch = t // CHUNK
    r = jnp.bitwise_and(lax.broadcasted_iota(jnp.int32, (2 * t, t), 0), t - 1)
    c = lax.broadcasted_iota(jnp.int32, (2 * t, t), 1)
    same = jnp.right_shift(r, 5) == jnp.right_shift(c, 5)
    tri2 = same & ((c >= r) if rev else (c <= r))
    pr = lax.broadcasted_iota(jnp.int32, (LANES, LANES), 0)
    pc = lax.broadcasted_iota(jnp.int32, (LANES, LANES), 1)
    diag = jnp.right_shift(pr, 6) == jnp.right_shift(pc, 6)
    qr = lax.broadcasted_iota(jnp.int32, (t, n_ch * LANES), 0)
    qc = lax.broadcasted_iota(jnp.int32, (t, n_ch * LANES), 1)
    rows_chunk = jnp.right_shift(qc, 7) == jnp.right_shift(qr, 5)
    vr = lax.broadcasted_iota(jnp.int32, (n_ch * LANES, t), 0)
    vc = lax.broadcasted_iota(jnp.int32, (n_ch * LANES, t), 1)
    cols_chunk = jnp.right_shift(vr, 7) == jnp.right_shift(vc, 5)
    return dict(tri2=tri2, diag=diag, rows_chunk=rows_chunk, cols_chunk=cols_chunk)


def _hgrn_gates(xf, lb):
    f = lb + (1.0 - lb) * _sigmoid(xf)
    return 1.0 - f, jnp.log(f)


def _hgrn_dir(xq, xf, v, lb, state, cm, cmt, mk, rev):
    t = xq.shape[0]
    n_ch = t // CHUNK
    lo = lax.broadcasted_iota(jnp.int32, (1, LANES), 1) < HEAD_DIM
    q = xq * _sigmoid(xq)
    k, lf = _hgrn_gates(xf, lb)
    cs = _cmm(cm, cmt, lf)
    b, bm, bl = cs[:t], cs[t:2 * t], cs[2 * t:]
    qd = q * jnp.exp(b - bm)
    kd = k * jnp.exp(bm - b)
    kc = k * jnp.exp(bl - b)
    qe = q * jnp.exp(b)
    qd2 = jnp.concatenate([jnp.where(lo, qd, 0.0), jnp.where(lo, 0.0, qd)], axis=0)
    o2 = _mm_nn(jnp.where(mk["tri2"], _mm_nt(qd2, kd), 0.0), v)
    o = jnp.where(lo, o2[:t], o2[t:])
    vexp = jnp.where(mk["cols_chunk"], jnp.concatenate([jnp.transpose(v)] * n_ch, axis=0), 0.0)
    adds = _mm_nn(vexp, kc)
    dec = jnp.exp(bl)
    entering = [None] * n_ch
    for c in (range(n_ch - 1, -1, -1) if rev else range(n_ch)):
        entering[c] = state
        d = jnp.concatenate([dec[c * CHUNK:(c + 1) * CHUNK]] * (LANES // CHUNK), axis=0)
        state = d * state + jnp.where(mk["diag"], adds[c * LANES:(c + 1) * LANES], 0.0)
    qexp = jnp.where(mk["rows_chunk"], jnp.concatenate([qe] * n_ch, axis=1), 0.0)
    return o + _mm_nt(qexp, jnp.concatenate(entering, axis=1)), state


def _hgrn_lower_bounds(l):
    out = []
    for d in (0, 1):
        l0, l1 = l[2 * d:2 * d + 1, :], l[2 * d + 1:2 * d + 2, :]
        mx = jnp.maximum(l0, l1)
        e0, e1 = jnp.exp(l0 - mx), jnp.exp(l1 - mx)
        out.append(e0 / (e0 + e1))
    return out


def _hgrn_consts(t):
    cf, cb = _np_hgrn_cums(t, False), _np_hgrn_cums(t, True)
    return (_bf(cf), _bf(cf.T), _bf(cb), _bf(cb.T), _bf(_np_blocksum(LANES)))


def _hgrn_fwd(ph, lbl, ng):
    s = ph.shape[0]
    t = min(HG_TILE, s)
    nt = s // t
    consts = _hgrn_consts(t)

    def body(xq_ref, xff_ref, xfb_ref, xi_ref, xg_ref, lbl_ref, ng_ref, cf_ref, cft_ref, cb_ref, cbt_ref, bs_ref,
             o_ref, pre_ref, st_ref, acc):
        lbf, lbb = _hgrn_lower_bounds(lbl_ref)
        mk_f, mk_b = _hgrn_masks(t, False), _hgrn_masks(t, True)
        zero = jnp.zeros((LANES, LANES), F32)

        def rows_of(i):
            return pl.ds(pl.multiple_of(i * t, t), t)

        acc[...] = jnp.zeros_like(acc)

        def step(i, states):
            tb = nt - 1 - i
            rf, rb = rows_of(i), rows_of(tb)
            st_ref[0, 0, i] = states[0]
            st_ref[0, 1, tb] = states[1]
            of, sf = _hgrn_dir(xq_ref[rf, :], xff_ref[rf, :], xi_ref[rf, :], lbf, states[0],
                               cf_ref[...], cft_ref[...], mk_f, False)
            ob, sb = _hgrn_dir(xq_ref[rb, :], xfb_ref[rb, :], xi_ref[rb, :], lbb, states[1],
                               cb_ref[...], cbt_ref[...], mk_b, True)
            acc[rf, :] += of
            acc[rb, :] += ob
            return sf, sb

        lax.fori_loop(0, nt, step, (zero, zero))

        def step_n(i, carry):
            rows = rows_of(i)
            o = acc[rows, :]
            ss = _xdot(o * o, bs_ref[...])
            r = lax.rsqrt(ss * (1.0 / HEAD_DIM) + EPS)
            xg = xg_ref[rows, :]
            pre_ref[rows, :] = o
            o_ref[rows, :] = (o * r * ng_ref[...]) * (xg * _sigmoid(xg))
            return carry

        lax.fori_loop(0, nt, step_n, 0)

    col = lambda off: pl.BlockSpec((s, LANES), lambda m: (0, off + m))
    full = lambda a: pl.BlockSpec(a.shape, lambda m: (0,) * a.ndim)
    return pl.pallas_call(
        body, name="hgrn_fwd", grid=(4,),
        in_specs=[col(0), col(4), col(8), col(12), col(16), pl.BlockSpec((4, LANES), lambda m: (0, m)),
                  pl.BlockSpec((1, LANES), lambda m: (0, m))] + [full(c) for c in consts],
        out_specs=[col(0), col(0), pl.BlockSpec((1, 2, nt, LANES, LANES), lambda m: (m, 0, 0, 0, 0))],
        out_shape=[jax.ShapeDtypeStruct((s, 512), F32), jax.ShapeDtypeStruct((s, 512), F32),
                   jax.ShapeDtypeStruct((4, 2, nt, LANES, LANES), F32)],
        scratch_shapes=[pltpu.VMEM((s, LANES), F32)],
        compiler_params=_params(1),
    )(ph, ph, ph, ph, ph, lbl, ng, *consts)


def _hgrn_bwd(ph, pre, dout, states, lbl, ng):
    s = ph.shape[0]
    t = min(HG_TILE, s)
    nt = s // t
    consts = _hgrn_consts(t)

    def body(xq_ref, xff_ref, xfb_ref, xi_ref, xg_ref, pre_ref, do_ref, st_ref, lbl_ref, ng_ref,
             cf_ref, cft_ref, cb_ref, cbt_ref, bs_ref,
             dq_ref, dff_ref, dfb_ref, di_ref, dg_ref, dlb_ref, dng_ref, dpre, dq_acc, dv_acc):
        lbf, lbb = _hgrn_lower_bounds(lbl_ref)
        mk_f, mk_b = _hgrn_masks(t, False), _hgrn_masks(t, True)
        zero = jnp.zeros((LANES, LANES), F32)
        zrow = jnp.zeros((1, LANES), F32)

        def rows_of(i):
            return pl.ds(pl.multiple_of(i * t, t), t)

        def step_n(i, dng):
            rows = rows_of(i)
            o, xg, do = pre_ref[rows, :], xg_ref[rows, :], do_ref[rows, :]
            bs = bs_ref[...]
            r = lax.rsqrt(_xdot(o * o, bs) * (1.0 / HEAD_DIM) + EPS)
            sg = _sigmoid(xg)
            gate = xg * sg
            don = do * gate
            dg_ref[rows, :] = (do * (o * r * ng_ref[...]) * (sg * (1.0 + xg * (1.0 - sg)))).astype(CD)
            u = don * ng_ref[...]
            dpre[rows, :] = r * u - o * (r * r * r) * (_xdot(u * o, bs) * (1.0 / HEAD_DIM))
            return dng + jnp.sum(don * o * r, axis=0, keepdims=True)

        dng_ref[...] = lax.fori_loop(0, nt, step_n, zrow)

        dq_acc[...] = jnp.zeros_like(dq_acc)
        dv_acc[...] = jnp.zeros_like(dv_acc)

        def grad_tile(ti, xf_ref, df_ref, lb, cm, cmt, mk, rev, st, dstate):
            rows = rows_of(ti)
            fn = lambda xq, xf, v, lbv, s_in: _hgrn_dir(xq, xf, v, lbv, s_in, cm, cmt, mk, rev)
            _, vjp = jax.vjp(fn, xq_ref[rows, :], xf_ref[rows, :], xi_ref[rows, :], lb, st)
            dxq, dxf, dv, dlb_t, dstate = vjp((dpre[rows, :], dstate))
            df_ref[rows, :] = dxf.astype(CD)
            dq_acc[rows, :] += dxq
            dv_acc[rows, :] += dv
            return dstate, dlb_t

        def step_g(i, carry):
            dsf, dsb, dlbf, dlbb = carry
            tf, tb = nt - 1 - i, i
            dsf, gf = grad_tile(tf, xff_ref, dff_ref, lbf, cf_ref[...], cft_ref[...], mk_f, False, st_ref[0, 0, tf], dsf)
            dsb, gb = grad_tile(tb, xfb_ref, dfb_ref, lbb, cb_ref[...], cbt_ref[...], mk_b, True, st_ref[0, 1, tb], dsb)
            return dsf, dsb, dlbf + gf, dlbb + gb

        _, _, dlbf, dlbb = lax.fori_loop(0, nt, step_g, (zero, zero, zrow, zrow))
        dlb_ref[0:1, :] = dlbf
        dlb_ref[1:2, :] = dlbb
        dq_ref[...] = dq_acc[...].astype(CD)
        di_ref[...] = dv_acc[...].astype(CD)

    col = lambda off: pl.BlockSpec((s, LANES), lambda m: (0, off + m))
    full = lambda a: pl.BlockSpec(a.shape, lambda m: (0,) * a.ndim)
    stream = jax.ShapeDtypeStruct((s, 512), CD)
    return pl.pallas_call(
        body, name="hgrn_bwd", grid=(4,),
        in_specs=[col(0), col(4), col(8), col(12), col(16), col(0), col(0),
                  pl.BlockSpec((1, 2, nt, LANES, LANES), lambda m: (m, 0, 0, 0, 0)),
                  pl.BlockSpec((4, LANES), lambda m: (0, m)),
                  pl.BlockSpec((1, LANES), lambda m: (0, m))] + [full(c) for c in consts],
        out_specs=[col(0)] * 5 + [pl.BlockSpec((2, LANES), lambda m: (0, m)), pl.BlockSpec((1, LANES), lambda m: (0, m))],
        out_shape=[stream] * 5 + [jax.ShapeDtypeStruct((2, 512), F32), jax.ShapeDtypeStruct((1, 512), F32)],
        scratch_shapes=[pltpu.VMEM((s, LANES), F32), pltpu.VMEM((s, LANES), F32), pltpu.VMEM((s, LANES), F32)],
        compiler_params=_params(1),
    )(ph, ph, ph, ph, ph, pre, dout, states, lbl, ng, *consts)


def _branch_out(o, w4):
    return jnp.concatenate([_mm(o, w4[j], NN) for j in range(N_SHARD)], axis=1)


def _mix_out_fwd(x, oa, ob, pg, wa, wb, wo, tm):
    s, d = x.shape

    def body(x_ref, oa_ref, ob_ref, ga_ref, gb_ref, wa_ref, wb_ref, wo_ref, xo_ref):
        ya = _branch_out(oa_ref[...], wa_ref)
        yb = _branch_out(ob_ref[...], wb_ref)
        merged = _sigmoid(ga_ref[...]) * ya + _sigmoid(gb_ref[...]) * yb
        xo_ref[...] = x_ref[...] + _mm(merged, wo_ref[...], NN)

    row = pl.BlockSpec((tm, d), lambda i: (i, 0))
    half = pl.BlockSpec((tm, 512), lambda i: (i, 0))
    full = lambda a: pl.BlockSpec(a.shape, lambda i: (0,) * a.ndim)
    return pl.pallas_call(
        body, name="mix_out_fwd", grid=(s // tm,),
        in_specs=[row, half, half, row, pl.BlockSpec((tm, d), lambda i: (i, 1)), full(wa), full(wb), full(wo)],
        out_specs=row, out_shape=jax.ShapeDtypeStruct((s, d), F32),
        compiler_params=_params(1),
    )(x, oa, ob, pg, pg, wa, wb, wo)


def _mix_out_bwd(dx, oa, ob, pg, wa, wb, wo, tm, after=()):
    s, d = dx.shape
    eq, ebc = _bf(_np_expand_q()), _bf(_np_headsum_spread())

    def body(*refs):
        (dx_ref, oa_ref, ob_ref, ga_ref, gb_ref, wa_ref, wb_ref, wo_ref, eq_ref, ebc_ref,
         dpg_ref, mg_ref, dya_ref, dyb_ref, doe_ref, dl_ref, dob_ref) = refs[len(after):]
        oa = oa_ref[...]
        ya = _branch_out(oa, wa_ref)
        yb = _branch_out(ob_ref[...], wb_ref)
        sa, sb = _sigmoid(ga_ref[...]), _sigmoid(gb_ref[...])
        mg_ref[...] = (sa * ya + sb * yb).astype(CD)
        dm = _mm(dx_ref[...], wo_ref[...], NT)
        dpg_ref[...] = jnp.concatenate([dm * ya * sa * (1.0 - sa), dm * yb * sb * (1.0 - sb)], axis=1).astype(CD)
        dya, dyb = dm * sa, dm * sb
        dya_ref[...] = dya.astype(CD)
        dyb_ref[...] = dyb.astype(CD)
        doa = jnp.zeros(oa.shape, F32)
        dob = jnp.zeros(oa.shape, F32)
        for j in range(N_SHARD):
            doa = doa + _mm(dya[:, 256 * j:256 * j + 256], wa_ref[j], NT)
            dob = dob + _mm(dyb[:, 256 * j:256 * j + 256], wb_ref[j], NT)
        dob_ref[...] = dob
        doe_ref[...] = _mm(doa, eq_ref[...], NN).astype(CD)
        dl_ref[...] = _xdot(doa * oa, ebc_ref[...])

    row = pl.BlockSpec((tm, d), lambda i: (i, 0))
    half = pl.BlockSpec((tm, 512), lambda i: (i, 0))
    full = lambda a: pl.BlockSpec(a.shape, lambda i: (0,) * a.ndim)
    wide = jax.ShapeDtypeStruct((s, d), CD)
    return pl.pallas_call(
        body, name="mix_out_bwd", grid=(s // tm,),
        in_specs=[ANY] * len(after) + [row, half, half, row, pl.BlockSpec((tm, d), lambda i: (i, 1)), full(wa), full(wb),
                                       full(wo), full(eq), full(ebc)],
        out_specs=[pl.BlockSpec((tm, 2048), lambda i: (i, 0)), row, row, row, row, row, half],
        out_shape=[jax.ShapeDtypeStruct((s, 2048), CD), wide, wide, wide, wide, jax.ShapeDtypeStruct((s, d), F32),
                   jax.ShapeDtypeStruct((s, 512), F32)],
        compiler_params=_params(1),
    )(*after, dx, oa, ob, pg, pg, wa, wb, wo, eq, ebc)


def _loss_head(x, g, target, tm):
    s, d = x.shape

    def body(x_ref, g_ref, t_ref, dx_ref, loss_ref, dg_ref):
        @pl.when(pl.program_id(0) == 0)
        def _():
            loss_ref[...] = jnp.zeros_like(loss_ref)
            dg_ref[...] = jnp.zeros_like(dg_ref)

        xv = x_ref[...]
        r = lax.rsqrt(jnp.mean(xv * xv, axis=-1, keepdims=True) + EPS)
        err = xv * r * g_ref[...] - t_ref[...]
        loss_ref[...] += 0.5 * jnp.sum(jnp.mean(err * err, axis=-1, keepdims=True))
        dy = err * (1.0 / d)
        u = dy * g_ref[...]
        dx_ref[...] = r * u - xv * (r * r * r) * jnp.mean(u * xv, axis=-1, keepdims=True)
        dg_ref[...] += jnp.sum(dy * xv * r, axis=0, keepdims=True)

    row = pl.BlockSpec((tm, d), lambda i: (i, 0))
    vec = pl.BlockSpec((1, d), lambda i: (0, 0))
    return pl.pallas_call(
        body, name="loss_head", grid=(s // tm,),
        in_specs=[row, vec, row], out_specs=[row, pl.BlockSpec((8, LANES), lambda i: (0, 0)), vec],
        out_shape=[jax.ShapeDtypeStruct((s, d), F32), jax.ShapeDtypeStruct((8, LANES), F32),
                   jax.ShapeDtypeStruct((1, d), F32)],
        compiler_params=_params(1),
    )(x, g, target)


def _position():
    x, y, c = lax.axis_index("x"), lax.axis_index("y"), lax.axis_index("c")
    return x, y, c, [(1 - x, y), (x, 1 - y), (1 - x, 1 - y)]


def _row_tile(rows, cap=256):
    best = rows
    for cand in range(8, min(rows, cap) + 1, 8):
        if rows % cand == 0:
            best = cand
    return best


def _cast_into_slots(shards, dtypes, me_idx):
    n = len(shards)
    tiles = [_row_tile(s.shape[0]) for s in shards]
    counts = [s.shape[0] // t for s, t in zip(shards, tiles)]
    starts = [sum(counts[:a]) for a in range(n)]

    def body(me_ref, *refs):
        i = pl.program_id(0)
        for a in range(n):
            @pl.when((i >= starts[a]) & (i < starts[a] + counts[a]))
            def _(a=a):
                refs[n + a][0] = refs[a][...].astype(dtypes[a])

    tile_of = [lambda i, a=a: jnp.clip(i - starts[a], 0, counts[a] - 1) for a in range(n)]
    return pl.pallas_call(
        body, name="cast_into_slots",
        grid_spec=pltpu.PrefetchScalarGridSpec(
            num_scalar_prefetch=1, grid=(sum(counts),),
            in_specs=[pl.BlockSpec((tiles[a], shards[a].shape[1]), lambda i, me, a=a: (tile_of[a](i), 0)) for a in range(n)],
            out_specs=[pl.BlockSpec((1, tiles[a], shards[a].shape[1]), lambda i, me, a=a: (me[0], tile_of[a](i), 0))
                       for a in range(n)]),
        out_shape=[jax.ShapeDtypeStruct((N_SHARD,) + s.shape, dt) for s, dt in zip(shards, dtypes)],
        compiler_params=_params(1),
    )(me_idx, *shards)


HBM_SPEC = pl.BlockSpec(memory_space=pltpu.HBM)
SEM_SPEC = pl.BlockSpec(memory_space=pltpu.SEMAPHORE)
DATAFLOW = pltpu.SideEffectType.DATAFLOW_SIDE_EFFECTING


def _exchange_copies(srcs, lands, send, recv, gather):
    x, y, c, chips = _position()
    me = 2 * x + y
    out = []
    for a in range(len(lands)):
        dst = lands[a].at[me]
        if gather and _halved(lands[a]):
            half = lands[a].shape[1] // 2
            dst = lands[a].at[me, pl.ds(c * half, half), :]
        for k, (px, py) in enumerate(chips):
            src = dst if gather else srcs[a].at[2 * px + py]
            out.append(pltpu.make_async_remote_copy(src_ref=src, dst_ref=dst, send_sem=send.at[3 * a + k],
                                                    recv_sem=recv.at[3 * a + k], device_id=(px, py, c), device_id_type=MESH))
    return out


def _halved(land):
    return land.shape[1] % 32 == 0


def _pair_fill(name, lands):
    n = len(lands)

    def body(*refs):
        src, dst = refs[:n], refs[n:2 * n]
        send, recv = refs[2 * n:]
        x, y, c, chips = _position()
        copies = []
        for a in range(n):
            half = src[a].shape[1] // 2
            for k, (px, py) in enumerate(chips):
                rows = (2 * px + py, pl.ds(c * half, half), slice(None))
                cp = pltpu.make_async_remote_copy(src_ref=src[a].at[rows], dst_ref=dst[a].at[rows], send_sem=send.at[a, k],
                                                  recv_sem=recv.at[a, k], device_id=(x, y, 1 - c), device_id_type=MESH)
                cp.start()
                copies.append(cp)
        for cp in copies:
            cp.wait()

    return pl.pallas_call(
        body, name=name, in_specs=[ANY] * n, out_specs=[ANY] * n,
        out_shape=[jax.ShapeDtypeStruct(l.shape, l.dtype) for l in lands],
        input_output_aliases={a: a for a in range(n)},
        scratch_shapes=[pltpu.SemaphoreType.DMA((n, 3)), pltpu.SemaphoreType.DMA((n, 3))],
    )(*lands)


def _exchange_start(name, srcs, lands, after):
    ns, nl, na = len(srcs), len(lands), len(after)
    gather = ns == 0

    def body(*refs):
        src_refs, land_refs = refs[:ns], refs[ns:ns + nl]
        send, recv = refs[ns + nl + na], refs[ns + nl + na + 1]
        token = refs[-1]
        for cp in _exchange_copies(src_refs, land_refs, send, recv, gather):
            cp.start()
        token[...] = jnp.zeros_like(token)

    arrays = [pltpu.with_memory_space_constraint(a, pltpu.HBM) for a in list(srcs) + list(lands)]
    outs = pl.pallas_call(
        body, name=name,
        out_shape=(pltpu.SemaphoreType.DMA((3 * nl,)), pltpu.SemaphoreType.DMA((3 * nl,)),
                   *[pltpu.HBM(a.shape, a.dtype) for a in arrays], jax.ShapeDtypeStruct((8, LANES), F32)),
        in_specs=[HBM_SPEC] * (ns + nl) + [ANY] * na,
        out_specs=(SEM_SPEC, SEM_SPEC, *[HBM_SPEC] * (ns + nl), pl.BlockSpec(memory_space=pltpu.VMEM)),
        input_output_aliases={i: 2 + i for i in range(ns + nl)},
        compiler_params=pltpu.CompilerParams(has_side_effects=DATAFLOW),
    )(*arrays, *after)
    return outs[0], outs[1], list(outs[2:2 + ns]), list(outs[2 + ns:2 + ns + nl]), outs[-1]


def _exchange_wait(name, send, recv, srcs, lands, after):
    ns, nl, na = len(srcs), len(lands), len(after)
    gather = ns == 0

    def body(*refs):
        src_refs, land_refs = refs[:ns], refs[ns:ns + nl]
        send_ref, recv_ref = refs[ns + nl], refs[ns + nl + 1]
        for cp in _exchange_copies(src_refs, land_refs, send_ref, recv_ref, gather):
            cp.wait_send()
            cp.wait_recv()

    outs = pl.pallas_call(
        body, name=name,
        out_shape=tuple(pltpu.HBM(a.shape, a.dtype) for a in list(srcs) + list(lands)),
        in_specs=[HBM_SPEC] * (ns + nl) + [SEM_SPEC, SEM_SPEC] + [ANY] * na,
        out_specs=tuple([HBM_SPEC] * (ns + nl)),
        input_output_aliases={i: i for i in range(ns + nl)},
        compiler_params=pltpu.CompilerParams(has_side_effects=DATAFLOW),
    )(*srcs, *lands, send, recv, *after)
    return list(outs[ns:])


def _pair_exchange(grads):
    n = len(grads)

    def body(*refs):
        src, dst = refs[:n], refs[n:2 * n]
        send, recv = refs[2 * n:]
        x, y, c, _ = _position()
        copies = []
        for a in range(n):
            half = src[a].shape[1] // 2
            cp = pltpu.make_async_remote_copy(
                src_ref=src[a].at[:, pl.ds((1 - c) * half, half), :], dst_ref=dst[a], send_sem=send.at[a],
                recv_sem=recv.at[a], device_id=(x, y, 1 - c), device_id_type=MESH)
            cp.start()
            copies.append(cp)
        for cp in copies:
            cp.wait()

    return pl.pallas_call(
        body, name="grad_pair_exchange", in_specs=[ANY] * n, out_specs=[ANY] * n,
        out_shape=[jax.ShapeDtypeStruct((g.shape[0], g.shape[1] // 2, g.shape[2]), g.dtype) for g in grads],
        scratch_shapes=[pltpu.SemaphoreType.DMA((n,)), pltpu.SemaphoreType.DMA((n,))],
    )(*grads)


def _shard_of(a):
    return lambda i: jnp.clip(i - a * N_SHARD, 0, N_SHARD - 1)


def _pair_sum(gs, gots, c_idx, me_idx):
    n = len(gs)
    halves = [(g.shape[1] // 2, g.shape[2]) for g in gs]

    def body(c_ref, me_ref, *refs):
        g_refs, got_refs, s_refs, own_refs = (refs[k * n:(k + 1) * n] for k in range(4))
        i = pl.program_id(0)
        for a in range(n):
            @pl.when(i // N_SHARD == a)
            def _(a=a):
                sm = g_refs[a][...] + got_refs[a][...].astype(F32)
                s_refs[a][...] = sm.astype(CD)

                @pl.when(i % N_SHARD == me_ref[0])
                def _():
                    own_refs[a][...] = sm[0]

    shard = [_shard_of(a) for a in range(n)]
    return pl.pallas_call(
        body, name="grad_pair_sum",
        grid_spec=pltpu.PrefetchScalarGridSpec(
            num_scalar_prefetch=2, grid=(n * N_SHARD,),
            in_specs=[pl.BlockSpec((1, h, c_), lambda i, c, me, a=a: (shard[a](i), c[0], 0)) for a, (h, c_) in enumerate(halves)]
            + [pl.BlockSpec((1, h, c_), lambda i, c, me, a=a: (shard[a](i), 0, 0)) for a, (h, c_) in enumerate(halves)],
            out_specs=[pl.BlockSpec((1, h, c_), lambda i, c, me, a=a: (shard[a](i), 0, 0)) for a, (h, c_) in enumerate(halves)]
            + [pl.BlockSpec((h, c_), lambda i, c, me: (0, 0)) for h, c_ in halves]),
        out_shape=[jax.ShapeDtypeStruct((N_SHARD, h, c_), CD) for h, c_ in halves]
        + [jax.ShapeDtypeStruct((h, c_), F32) for h, c_ in halves],
        compiler_params=_params(1),
    )(c_idx, me_idx, *gs, *gots)


def _chip_sum(owns, gots, me_idx):
    n = len(owns)
    tiles = [_row_tile(o.shape[0]) for o in owns]
    counts = [o.shape[0] // t for o, t in zip(owns, tiles)]
    starts = [sum(counts[:a]) for a in range(n)]

    def body(me_ref, *refs):
        own_refs, got_refs, out_refs = (refs[k * n:(k + 1) * n] for k in range(3))
        i = pl.program_id(0)
        for a in range(n):
            @pl.when((i >= starts[a]) & (i < starts[a] + counts[a]))
            def _(a=a):
                total = None
                for j in range(N_SHARD):
                    term = jnp.where(j == me_ref[0], own_refs[a][...], got_refs[a][j].astype(F32))
                    total = term if total is None else total + term
                out_refs[a][...] = total

    tile_of = [lambda i, a=a: jnp.clip(i - starts[a], 0, counts[a] - 1) for a in range(n)]
    own_specs = [pl.BlockSpec((tiles[a], owns[a].shape[1]), lambda i, me, a=a: (tile_of[a](i), 0)) for a in range(n)]
    return pl.pallas_call(
        body, name="grad_chip_sum",
        grid_spec=pltpu.PrefetchScalarGridSpec(
            num_scalar_prefetch=1, grid=(sum(counts),),
            in_specs=own_specs + [pl.BlockSpec((N_SHARD, tiles[a], owns[a].shape[1]), lambda i, me, a=a: (0, tile_of[a](i), 0))
                                  for a in range(n)],
            out_specs=own_specs),
        out_shape=[jax.ShapeDtypeStruct(o.shape, F32) for o in owns],
        compiler_params=_params(1),
    )(me_idx, *owns, *gots)


def _pair_share(halves):
    n = len(halves)

    def body(*refs):
        src, dst = refs[:n], refs[n:2 * n]
        send, recv = refs[2 * n:]
        x, y, c, _ = _position()
        copies = []
        for a in range(n):
            cp = pltpu.make_async_remote_copy(src_ref=src[a], dst_ref=dst[a], send_sem=send.at[a],
                                              recv_sem=recv.at[a], device_id=(x, y, 1 - c), device_id_type=MESH)
            cp.start()
            copies.append(cp)
        for cp in copies:
            cp.wait()

    return pl.pallas_call(
        body, name="grad_pair_share", in_specs=[ANY] * n, out_specs=[ANY] * n,
        out_shape=[jax.ShapeDtypeStruct(h.shape, h.dtype) for h in halves],
        scratch_shapes=[pltpu.SemaphoreType.DMA((n,)), pltpu.SemaphoreType.DMA((n,))],
    )(*halves)


def _small_allreduce(buf):
    rows, cols = buf.shape

    def body(src_ref, out_ref, slots, send, recv):
        x, y, c, _ = _position()
        me = 4 * x + 2 * y + c
        slots[me] = src_ref[...]
        copies = []
        k = 0
        for dx in (0, 1):
            for dy in (0, 1):
                for dc in (0, 1):
                    if (dx, dy, dc) == (0, 0, 0):
                        continue
                    peer = (jnp.where(dx, 1 - x, x), jnp.where(dy, 1 - y, y), jnp.where(dc, 1 - c, c))
                    cp = pltpu.make_async_remote_copy(src_ref=src_ref, dst_ref=slots.at[me], send_sem=send.at[k],
                                                      recv_sem=recv.at[k], device_id=peer, device_id_type=MESH)
                    cp.start()
                    copies.append(cp)
                    k += 1
        for cp in copies:
            cp.wait()
        total = slots[0]
        for dev in range(1, N_DEV):
            total = total + slots[dev]
        out_ref[...] = total

    vm = pl.BlockSpec(memory_space=pltpu.VMEM)
    return pl.pallas_call(
        body, name="small_allreduce", in_specs=[vm], out_specs=vm,
        out_shape=jax.ShapeDtypeStruct((rows, cols), F32),
        scratch_shapes=[pltpu.VMEM((N_DEV, rows, cols), F32), pltpu.SemaphoreType.DMA((N_DEV - 1,)),
                        pltpu.SemaphoreType.DMA((N_DEV - 1,))],
    )(buf)


def _adamw_math(w, gv, m, v):
    mn = ADAM_B1 * m + (1.0 - ADAM_B1) * gv
    vn = ADAM_B2 * v + (1.0 - ADAM_B2) * (gv * gv)
    m_hat = mn / (1.0 - ADAM_B1 ** ADAM_STEP)
    v_hat = vn / (1.0 - ADAM_B2 ** ADAM_STEP)
    return -ADAM_LR * (m_hat / (jnp.sqrt(v_hat) + ADAM_EPS) + ADAM_WD * w), mn, vn


def _adamw(w, g, m, v):
    rows, cols = w.shape
    tr = _row_tile(rows)

    def body(w_ref, g_ref, m_ref, v_ref, d_ref, mo_ref, vo_ref):
        d_ref[...], mo_ref[...], vo_ref[...] = _adamw_math(w_ref[...], g_ref[...], m_ref[...], v_ref[...])

    blk = pl.BlockSpec((tr, cols), lambda i: (i, 0))
    shp = jax.ShapeDtypeStruct((rows, cols), F32)
    return pl.pallas_call(
        body, name="adamw", grid=(rows // tr,), in_specs=[blk] * 4, out_specs=[blk] * 3, out_shape=[shp] * 3,
        compiler_params=_params(1),
    )(w, g, m, v)


def _adamw_halves(w, own, got, m, v, c_idx):
    rows, cols = w.shape
    tr = _row_tile(rows // 2)
    per_half = rows // 2 // tr

    def body(c_ref, w_ref, own_ref, got_ref, m_ref, v_ref, d_ref, mo_ref, vo_ref, g_ref):
        mine = (pl.program_id(0) // per_half) == c_ref[0]
        gv = jnp.where(mine, own_ref[...], got_ref[...])
        g_ref[...] = gv
        d_ref[...], mo_ref[...], vo_ref[...] = _adamw_math(w_ref[...], gv, m_ref[...], v_ref[...])

    blk = pl.BlockSpec((tr, cols), lambda i, c: (i, 0))
    own_blk = pl.BlockSpec((tr, cols), lambda i, c: (jnp.where(i // per_half == c[0], i % per_half, 0), 0))
    got_blk = pl.BlockSpec((tr, cols), lambda i, c: (jnp.where(i // per_half == c[0], 0, i % per_half), 0))
    shp = jax.ShapeDtypeStruct((rows, cols), F32)
    return pl.pallas_call(
        body, name="adamw_halves",
        grid_spec=pltpu.PrefetchScalarGridSpec(num_scalar_prefetch=1, grid=(rows // tr,),
                                               in_specs=[blk, own_blk, got_blk, blk, blk], out_specs=[blk] * 4),
        out_shape=[shp] * 4, compiler_params=_params(1),
    )(c_idx, w, own, got, m, v)


def _local_step(x, target, norm_gains, q_g, k_g, ng, weights_of, grads_done):
    s = x.shape[0]
    tm = min(512, s)
    tq = min(256, s)
    tf = min(1024, s)
    g1, gm, g2, gf = norm_gains
    cos2, sin2 = _rope_tables(s)
    gq8 = jnp.tile(q_g, (1, 8))
    gk2 = jnp.tile(k_g, (1, 2))

    tn = min(256, s)
    tk = min(1024, s)
    w1 = weights_of(1, ())
    x1, s1, t1, b1, h1 = _ffn_fwd(x, g1, w1["g1"], w1["u1"], w1["d1"], tf)
    w2 = weights_of(2, (x1,))
    lbl = w2["lbl"]
    pqkv, ph, pg, hm = _mix_in_fwd(x1, gm, w2["in"], tn)
    qe, kr, vr, vs = _qk_prep(pqkv, gq8, gk2, cos2, sin2, tm)
    oa, lse = _attn_fwd(qe, kr, vr, vs, tq)
    ob, pre, hstates = _hgrn_fwd(ph, lbl, ng)
    x2 = _mix_out_fwd(x1, oa, ob, pg, w2["a"], w2["b"], w2["o"], tm)
    w3 = weights_of(3, (x2,))
    x3, s2, t2, b2, h2 = _ffn_fwd(x2, g2, w3["g2"], w3["u2"], w3["d2"], tf)
    dx3, loss, dgf = _loss_head(x3, gf, target, tm)

    dx2, da2, db2, f2, dg2, dx3c = _ffn_bwd(dx3, x2, g2, s2, t2, b2, w3["g2"], w3["u2"], w3["d2"], tm)
    tok = grads_done(3, dict(g2=_dw_shared_b("dw_gate", da2, h2, tk, 1.0), u2=_dw_shared_b("dw_gate", db2, h2, tk, 1.0),
                             d2=_dw_shared_b("dw_down", f2, dx3c, tk, 0.5)))

    dpg, mg, dya, dyb, doe, delta, dob = _mix_out_bwd(dx2, oa, ob, pg, w2["a"], w2["b"], w2["o"], tm, tok)
    g_o = [g.reshape(N_SHARD, D_MODEL // N_SHARD, D_MODEL) for g in _dw_colblocks("dw_out", mg, dx2, 1, tk)]
    g_a = _dw_colblocks("dw_branch", oa, dya, N_SHARD, tk)
    g_b = _dw_colblocks("dw_branch", ob, dyb, N_SHARD, tk)
    dqt, dkt, dvt = _attn_bwd(qe, kr, kr.T, vr, doe, delta, lse, tq)
    dqkv, dgq, dgk = _qk_prep_bwd(pqkv, dqt.T, dkt.T, dvt.T, gq8, gk2, cos2, sin2, tm)
    dhq, dhff, dhfb, dhi, dhg, dlb, dng = _hgrn_bwd(ph, pre, dob, hstates, lbl, ng)
    dps = (dqkv, dhq, dhff, dhfb, dhi, dhg, dpg)
    g_in = [g.reshape(N_SHARD, -1, D_MODEL) for g in _dw_in(dps, hm, tk)]
    tok = grads_done(2, {"in": g_in, "a": g_a, "b": g_b, "o": g_o})
    dx1, dgm = _mix_in_bwd(dps, w2["in"], x1, dx2, gm, tn, tok)

    dx0, da1, db1, f1, dg1, dx1c = _ffn_bwd(dx1, x, g1, s1, t1, b1, w1["g1"], w1["u1"], w1["d1"], tm)
    grads_done(1, dict(g1=_dw_shared_b("dw_gate", da1, h1, tk, 1.0), u1=_dw_shared_b("dw_gate", db1, h1, tk, 1.0),
                       d1=_dw_shared_b("dw_down", f1, dx1c, tk, 0.5)))
    small = dict(g1=dg1, gm=dgm, g2=dg2, gf=dgf, gq=dgq, gk=dgk, lb=dlb, ng=dng)
    return loss, dx0, small, lbl


GROUPS = {1: ("g1", "u1", "d1"), 2: ("in", "a", "b", "o"), 3: ("g2", "u2", "d2")}
BIG = GROUPS[1] + GROUPS[2] + GROUPS[3]
TRANSPOSED = ("g1", "u1", "in", "g2", "u2")


def _pack_rows(vectors, width):
    rows = []
    for vct in vectors:
        flat = vct.reshape(-1)
        pad = (-flat.shape[0]) % width
        rows.append(jnp.pad(flat, (0, pad)).reshape(-1, width))
    return jnp.concatenate(rows, axis=0)


def kernel(x, ffn1_norm_g, ffn1_w_gate, ffn1_w_up, ffn1_w_down, mix_norm_g, w_in, q_norm_g, k_norm_g, hgrn_lb_logits, hgrn_out_norm_g, w_branch_attn, w_branch_hgrn, w_out, ffn2_norm_g, ffn2_w_gate, ffn2_w_up, ffn2_w_down, final_norm_g, loss_target, m_ffn1_norm_g, m_ffn1_w_gate, m_ffn1_w_up, m_ffn1_w_down, m_mix_norm_g, m_w_in, m_q_norm_g, m_k_norm_g, m_hgrn_lb_logits, m_hgrn_out_norm_g, m_w_branch_attn, m_w_branch_hgrn, m_w_out, m_ffn2_norm_g, m_ffn2_w_gate, m_ffn2_w_up, m_ffn2_w_down, m_final_norm_g, v_ffn1_norm_g, v_ffn1_w_gate, v_ffn1_w_up, v_ffn1_w_down, v_mix_norm_g, v_w_in, v_q_norm_g, v_k_norm_g, v_hgrn_lb_logits, v_hgrn_out_norm_g, v_w_branch_attn, v_w_branch_hgrn, v_w_out, v_ffn2_norm_g, v_ffn2_w_gate, v_ffn2_w_up, v_ffn2_w_down, v_final_norm_g):
    xi, yi, ci = lax.axis_index("x"), lax.axis_index("y"), lax.axis_index("c")
    me = 2 * xi + yi
    c_idx = jnp.reshape(ci, (1,)).astype(jnp.int32)
    me_idx = jnp.reshape(me, (1,)).astype(jnp.int32)

    big_w = dict(g1=ffn1_w_gate[0], u1=ffn1_w_up[0], d1=ffn1_w_down[0], a=w_branch_attn[0], b=w_branch_hgrn[0],
                 o=w_out[0], g2=ffn2_w_gate[0], u2=ffn2_w_up[0], d2=ffn2_w_down[0])
    big_w["in"] = w_in[0]
    big_m = dict(g1=m_ffn1_w_gate[0], u1=m_ffn1_w_up[0], d1=m_ffn1_w_down[0], a=m_w_branch_attn[0], b=m_w_branch_hgrn[0],
                 o=m_w_out[0], g2=m_ffn2_w_gate[0], u2=m_ffn2_w_up[0], d2=m_ffn2_w_down[0])
    big_m["in"] = m_w_in[0]
    big_v = dict(g1=v_ffn1_w_gate[0], u1=v_ffn1_w_up[0], d1=v_ffn1_w_down[0], a=v_w_branch_attn[0], b=v_w_branch_hgrn[0],
                 o=v_w_out[0], g2=v_ffn2_w_gate[0], u2=v_ffn2_w_up[0], d2=v_ffn2_w_down[0])
    big_v["in"] = v_w_in[0]
    for table in (big_w, big_m, big_v):
        for n in TRANSPOSED:
            table[n] = table[n].T

    started, token = {}, ()
    for grp in (1, 2, 3):
        shards = [big_w[n] for n in GROUPS[grp]] + ([hgrn_lb_logits.reshape(4, LANES)] if grp == 2 else [])
        dtypes = [CD] * len(GROUPS[grp]) + ([F32] if grp == 2 else [])
        lands = _cast_into_slots(shards, dtypes, me_idx)
        send, recv, _, lands, tok = _exchange_start("gather%d_start" % grp, [], lands, token)
        started[grp], token = (send, recv, lands), (tok,)

    def weights_of(grp, after):
        send, recv, lands = started[grp]
        got = _exchange_wait("gather%d_wait" % grp, send, recv, [], lands, tuple(after) + (token if grp == 1 else ()))
        by_halves = [i for i, land in enumerate(got) if _halved(land)]
        for i, whole in zip(by_halves, _pair_fill("gather%d_fill" % grp, [got[i] for i in by_halves])):
            got[i] = whole
        w = dict(zip(GROUPS[grp], got))
        if grp == 2:
            w["in"] = w["in"].reshape(-1, D_MODEL)
            w["o"] = w["o"].reshape(D_MODEL, D_MODEL)
            w["lbl"] = jnp.transpose(got[-1], (1, 0, 2)).reshape(4, N_SHARD * LANES)
        return w

    pending = {}

    def grads_done(grp, grads):
        names = list(grads)
        got = _pair_exchange([grads[n][1] for n in names])
        res = _pair_sum([grads[n][0] for n in names], got, c_idx, me_idx)
        sums, owns = res[:len(names)], res[len(names):]
        lands = [lax.empty(s_.shape, s_.dtype) for s_ in sums]
        send, recv, srcs, lands, tok = _exchange_start("reduce%d_start" % grp, list(sums), lands, ())
        pending[grp] = (names, send, recv, srcs, lands, owns, tok)
        return (tok,)

    def reduced_halves(grp, after):
        names, send, recv, srcs, lands, owns, _ = pending[grp]
        parts = _exchange_wait("reduce%d_wait" % grp, send, recv, srcs, lands, after)
        return names, list(_chip_sum(list(owns), parts, me_idx))

    loss, dx, small, lbl = _local_step(
        x[0], loss_target[0], (ffn1_norm_g, mix_norm_g, ffn2_norm_g, final_norm_g.reshape(1, -1)),
        q_norm_g, k_norm_g, hgrn_out_norm_g, weights_of, grads_done)

    dgq = small["gq"].reshape(8, HEAD_DIM).sum(axis=0)
    dgk = small["gk"].reshape(2, HEAD_DIM).sum(axis=0)
    lb_full = _hgrn_lower_bounds(lbl)
    dlog = []
    for d in (0, 1):
        t = small["lb"][d:d + 1] * lb_full[d] * (1.0 - lb_full[d])
        dlog += [t, -t]
    small_list = [small["g1"], small["gm"], small["g2"], small["gf"], small["ng"], dgq, dgk, jnp.concatenate(dlog, axis=0), loss[0, 0]]
    packed = _pack_rows(small_list, D_MODEL)
    n_rows = packed.shape[0]
    packed = jnp.pad(packed, ((0, (-n_rows) % 8), (0, 0)))
    red = _small_allreduce(packed)
    loss_out = red[n_rows - 1, 0]
    sg = dict(g1=red[0:1], gm=red[1:2], g2=red[2:3], gf=red[3], ng=red[4:5, :512], gq=red[5:6, :HEAD_DIM],
              gk=red[6:7, :HEAD_DIM])
    dlog_full = red[7:9].reshape(2, 2, 512)
    sg["lb"] = lax.dynamic_slice_in_dim(dlog_full, me * LANES, LANES, axis=2)

    small_w = dict(g1=ffn1_norm_g, gm=mix_norm_g, g2=ffn2_norm_g, gf=final_norm_g, ng=hgrn_out_norm_g, gq=q_norm_g,
                   gk=k_norm_g, lb=hgrn_lb_logits)
    small_m = dict(g1=m_ffn1_norm_g, gm=m_mix_norm_g, g2=m_ffn2_norm_g, gf=m_final_norm_g, ng=m_hgrn_out_norm_g,
                   gq=m_q_norm_g, gk=m_k_norm_g, lb=m_hgrn_lb_logits)
    small_v = dict(g1=v_ffn1_norm_g, gm=v_mix_norm_g, g2=v_ffn2_norm_g, gf=v_final_norm_g, ng=v_hgrn_out_norm_g,
                   gq=v_q_norm_g, gk=v_k_norm_g, lb=v_hgrn_lb_logits)
    small_names = ("g1", "gm", "g2", "gf", "ng", "gq", "gk", "lb")
    pack = lambda dct: _pack_rows([dct[n] for n in small_names], D_MODEL)
    pw, pgr, pm, pv = pack(small_w), pack(sg), pack(small_m), pack(small_v)
    pad8 = lambda a: jnp.pad(a, ((0, (-a.shape[0]) % 8), (0, 0)))
    sd, sm_, sv_ = _adamw(pad8(pw), pad8(pgr), pad8(pm), pad8(pv))

    def unpack(buf):
        out, r = {}, 0
        for n in small_names:
            size = small_w[n].size
            nr = -(-size // D_MODEL)
            out[n] = buf[r:r + nr].reshape(-1)[:size].reshape(small_w[n].shape)
            r += nr
        return out

    sdelta, snew_m, snew_v = unpack(sd), unpack(sm_), unpack(sv_)
    sgrad = {n: sg[n].reshape(small_w[n].shape) for n in small_names}

    bdelta, bnew_m, bnew_v, bgrad = {}, {}, {}, {}

    def update(names, halves):
        for n, own, got in zip(names, halves, _pair_share(halves)):
            res = _adamw_halves(big_w[n], own, got, big_m[n], big_v[n], c_idx)
            if n in TRANSPOSED:
                res = [r.T for r in res]
            bdelta[n], bnew_m[n], bnew_v[n], bgrad[n] = [r[None] for r in res]

    names3, halves3 = reduced_halves(3, (pending[1][-1],))
    names2, halves2 = reduced_halves(2, (halves3[0],))
    update(names3 + names2, halves3 + halves2)
    names1, halves1 = reduced_halves(1, (bdelta[names2[-1]],))
    update(names1, halves1)

    order = [("s", "g1"), ("b", "g1"), ("b", "u1"), ("b", "d1"), ("s", "gm"), ("b", "in"), ("s", "gq"), ("s", "gk"),
             ("s", "lb"), ("s", "ng"), ("b", "a"), ("b", "b"), ("b", "o"), ("s", "g2"), ("b", "g2"), ("b", "u2"),
             ("b", "d2"), ("s", "gf")]
    outs = [loss_out, dx[None]]
    for table_s, table_b in ((sgrad, bgrad), (sdelta, bdelta), (snew_m, bnew_m), (snew_v, bnew_v)):
        outs += [(table_s if kind == "s" else table_b)[n] for kind, n in order]
    return tuple(outs)
```

```python
import functools

import numpy as np
import jax
import jax.numpy as jnp
from jax import lax
from jax.experimental import pallas as pl
from jax.experimental.pallas import tpu as pltpu

F32 = jnp.float32
BF16 = jnp.bfloat16
CD = jnp.bfloat16

EPS = 1e-6
D_MODEL = 1024
HEAD_DIM = 64
GRID_W = 64
ROPE_THETA = 10000.0
CHUNK = 32
N_SHARD = 4
N_DEV = 8
VMEM_LIMIT = 56 * 1024 * 1024
LANES = 128
HG_TILE = 256
ATTN_BWD_HEADS = 2
FFN_ROWS = 256

ADAM_LR = 0.001
ADAM_B1 = 0.9
ADAM_B2 = 0.999
ADAM_EPS = 1e-08
ADAM_WD = 0.01
ADAM_STEP = 10

NN = (((1,), (0,)), ((), ()))
NT = (((1,), (1,)), ((), ()))
TN = (((0,), (0,)), ((), ()))
MESH = pl.DeviceIdType.MESH
ANY = pl.BlockSpec(memory_space=pl.ANY)


def _mm(a, b, dn):
    return lax.dot_general(a.astype(CD), b.astype(CD), dn, preferred_element_type=F32)


def _split3(x):
    hi = x.astype(BF16)
    r = x - hi.astype(F32)
    mid = r.astype(BF16)
    lo = (r - mid.astype(F32)).astype(BF16)
    return hi, mid, lo


def _xdot(x, m):
    rows = x.shape[0]
    hi, mid, _ = _split3(x)
    r = lax.dot_general(jnp.concatenate([hi, mid], axis=0), m, NN, preferred_element_type=F32)
    return r[:rows] + r[rows:]


def _xdot_l(m, x):
    cols = x.shape[1]
    hi, mid, _ = _split3(x)
    r = lax.dot_general(m, jnp.concatenate([hi, mid], axis=1), NN, preferred_element_type=F32)
    return r[:, :cols] + r[:, cols:]


def _params(n_grid):
    return pltpu.CompilerParams(dimension_semantics=("arbitrary",) * n_grid, vmem_limit_bytes=VMEM_LIMIT)


def _sigmoid(x):
    return jax.nn.sigmoid(x)


def _np_blocksum(n):
    i = np.arange(n)
    return (i[:, None] // HEAD_DIM == i[None, :] // HEAD_DIM).astype(np.float32)


def _np_swap32(n):
    i = np.arange(n)
    partner = np.where(i % HEAD_DIM < HEAD_DIM // 2, i + HEAD_DIM // 2, i - HEAD_DIM // 2)
    m = np.zeros((n, n), np.float32)
    m[i, partner] = 1.0
    return m


def _np_expand_q():
    m = np.zeros((512, 1024), np.float32)
    for h in range(8):
        g = h // 4
        for d in range(HEAD_DIM):
            m[64 * h + d, 128 * h + 64 * g + d] = 1.0
    return m


def _np_headsum_spread():
    m = np.zeros((512, 1024), np.float32)
    for h in range(8):
        m[64 * h:64 * h + 64, 128 * h:128 * h + 128] = 1.0
    return m


def _np_swap_halves():
    m = np.zeros((128, 128), np.float32)
    i = np.arange(128)
    m[i, (i + 64) % 128] = 1.0
    return m


def _np_hgrn_cums(t, rev):
    r = np.arange(t)[:, None]
    c = np.arange(t)[None, :]
    same = (r // CHUNK) == (c // CHUNK)
    if not rev:
        cum = same & (c <= r)
        mid = same & (c % CHUNK <= CHUNK // 2 - 1)
    else:
        cum = same & (c >= r)
        mid = same & (c % CHUNK >= CHUNK // 2)
    return np.concatenate([cum, mid, same], axis=0).astype(np.float32)


def _bf(a):
    return jnp.asarray(a, dtype=BF16)


def _rope_tables(seq_len):
    rows = seq_len // GRID_W
    row = jnp.repeat(jnp.arange(rows, dtype=F32), GRID_W)
    col = jnp.tile(jnp.arange(GRID_W, dtype=F32), rows)
    n_freq = HEAD_DIM // 4
    inv = ROPE_THETA ** (-jnp.arange(n_freq, dtype=F32) / n_freq)
    ang = jnp.concatenate([row[:, None] * inv, col[:, None] * inv], axis=-1)
    cos, sin = jnp.cos(ang), jnp.sin(ang)
    c64 = jnp.concatenate([cos, cos], axis=-1)
    s64 = jnp.concatenate([-sin, sin], axis=-1)
    return jnp.tile(c64, (1, 2)), jnp.tile(s64, (1, 2))


def _ffn_fwd(x, g, wg, wu, wd, tm):
    s, d = x.shape
    nsh, fs, _ = wg.shape

    def body(x_ref, g_ref, wg_ref, wu_ref, wd_ref, xo_ref, a_ref, da_ref, b_ref, hb_ref, acc, hs):
        j = pl.program_id(1)

        @pl.when(j == 0)
        def _():
            xv = x_ref[...]
            r = lax.rsqrt(jnp.mean(xv * xv, axis=-1, keepdims=True) + EPS)
            h = (xv * r * g_ref[...]).astype(CD)
            hs[...] = h
            hb_ref[...] = h
            acc[...] = jnp.zeros_like(acc)

        for r0 in range(0, tm, FFN_ROWS):
            rows = slice(r0, min(r0 + FFN_ROWS, tm))
            h = hs[rows, :]
            a = _mm(h, wg_ref[0], NT)
            b = _mm(h, wu_ref[0], NT)
            sg = _sigmoid(a)
            silu = a * sg
            acc[rows, :] += _mm(silu * b, wd_ref[0], NN)
            a_ref[0, rows, :] = silu.astype(CD)
            da_ref[0, rows, :] = (sg * (1.0 + a * (1.0 - sg))).astype(CD)
            b_ref[0, rows, :] = b.astype(CD)

        @pl.when(j == nsh - 1)
        def _():
            xo_ref[...] = x_ref[...] + 0.5 * acc[...]

    return pl.pallas_call(
        body, name="ffn_fwd", grid=(s // tm, nsh),
        in_specs=[pl.BlockSpec((tm, d), lambda i, j: (i, 0)), pl.BlockSpec((1, d), lambda i, j: (0, 0))]
        + [pl.BlockSpec((1, fs, d), lambda i, j: (j, 0, 0))] * 3,
        out_specs=[pl.BlockSpec((tm, d), lambda i, j: (i, 0))] + [pl.BlockSpec((1, tm, fs), lambda i, j: (j, i, 0))] * 3
        + [pl.BlockSpec((tm, d), lambda i, j: (i, 0))],
        out_shape=[jax.ShapeDtypeStruct((s, d), F32)] + [jax.ShapeDtypeStruct((nsh, s, fs), CD)] * 3
        + [jax.ShapeDtypeStruct((s, d), CD)],
        scratch_shapes=[pltpu.VMEM((tm, d), F32), pltpu.VMEM((tm, d), CD)],
        compiler_params=_params(2),
    )(x, g, wg, wu, wd)


def _ffn_bwd(dout, x, g, silu, dsilu, b, wg, wu, wd, tm):
    s, d = x.shape
    nsh, fs, _ = wg.shape

    def body(do_ref, x_ref, g_ref, sl_ref, ds_ref, b_ref, wg_ref, wu_ref, wd_ref,
             dx_ref, da_ref, db_ref, f_ref, dg_ref, do16_ref, dh):
        i = pl.program_id(0)
        j = pl.program_id(1)

        @pl.when(j == 0)
        def _():
            dh[...] = jnp.zeros_like(dh)
            do16_ref[...] = (0.5 * do_ref[...]).astype(CD)

        @pl.when((i == 0) & (j == 0))
        def _():
            dg_ref[...] = jnp.zeros_like(dg_ref)

        for r0 in range(0, tm, FFN_ROWS):
            rows = slice(r0, min(r0 + FFN_ROWS, tm))
            sl = sl_ref[0, rows, :].astype(F32)
            bv = b_ref[0, rows, :].astype(F32)
            df = _mm(do16_ref[rows, :], wd_ref[0], NT)
            da = df * bv * ds_ref[0, rows, :].astype(F32)
            db = df * sl
            dh[rows, :] += _mm(da, wg_ref[0], NN) + _mm(db, wu_ref[0], NN)
            da_ref[0, rows, :] = da.astype(CD)
            db_ref[0, rows, :] = db.astype(CD)
            f_ref[0, rows, :] = (sl * bv).astype(CD)

        @pl.when(j == nsh - 1)
        def _():
            xv = x_ref[...]
            r = lax.rsqrt(jnp.mean(xv * xv, axis=-1, keepdims=True) + EPS)
            dhv = dh[...]
            u = dhv * g_ref[...]
            dx_ref[...] = do_ref[...] + r * u - xv * (r * r * r) * jnp.mean(u * xv, axis=-1, keepdims=True)
            dg_ref[...] += jnp.sum(dhv * xv * r, axis=0, keepdims=True)

    act = pl.BlockSpec((1, tm, fs), lambda i, j: (j, i, 0))
    row = pl.BlockSpec((tm, d), lambda i, j: (i, 0))
    return pl.pallas_call(
        body, name="ffn_bwd", grid=(s // tm, nsh),
        in_specs=[row, row, pl.BlockSpec((1, d), lambda i, j: (0, 0)), act, act, act]
        + [pl.BlockSpec((1, fs, d), lambda i, j: (j, 0, 0))] * 3,
        out_specs=[row, act, act, act, pl.BlockSpec((1, d), lambda i, j: (0, 0)), row],
        out_shape=[jax.ShapeDtypeStruct((s, d), F32), jax.ShapeDtypeStruct((nsh, s, fs), CD),
                   jax.ShapeDtypeStruct((nsh, s, fs), CD), jax.ShapeDtypeStruct((nsh, s, fs), CD),
                   jax.ShapeDtypeStruct((1, d), F32), jax.ShapeDtypeStruct((s, d), CD)],
        scratch_shapes=[pltpu.VMEM((tm, d), F32)],
        compiler_params=_params(2),
    )(dout, x, g, silu, dsilu, b, wg, wu, wd)


def _tn_call(name, operands, in_specs, out_shape, out_spec, grid, acc_shape, pick, scale=1.0):
    nk = grid[-1]
    n_in = len(operands)

    def body(*refs):
        out_ref, out16_ref, acc = refs[n_in], refs[n_in + 1], refs[n_in + 2]
        k = pl.program_id(len(grid) - 1)

        @pl.when(k == 0)
        def _():
            acc[...] = jnp.zeros_like(acc)

        pick(refs[:n_in], acc)

        @pl.when(k == nk - 1)
        def _():
            res = (acc[...] if scale == 1.0 else acc[...] * scale).reshape(out_ref.shape)
            out_ref[...] = res
            out16_ref[...] = res.astype(CD)

    return pl.pallas_call(
        body, name=name, grid=grid, in_specs=in_specs, out_specs=[out_spec, out_spec],
        out_shape=[out_shape, jax.ShapeDtypeStruct(out_shape.shape, CD)],
        scratch_shapes=[pltpu.VMEM(acc_shape, F32)], compiler_params=_params(len(grid)),
    )(*operands)


def _dw_shared_b(name, a3, b, tk, scale):
    nj, s, m = a3.shape
    n = b.shape[1]

    def pick(refs, acc):
        rows = pl.ds(pl.multiple_of(pl.program_id(1) * tk, tk), tk)
        acc[...] += _mm(refs[0][0], refs[1][rows, :], TN)

    return _tn_call(name, (a3, b),
                    [pl.BlockSpec((1, tk, m), lambda j, k: (j, k, 0)), pl.BlockSpec((s, n), lambda j, k: (0, 0))],
                    jax.ShapeDtypeStruct((nj, m, n), F32), pl.BlockSpec((1, m, n), lambda j, k: (j, 0, 0)),
                    (nj, s // tk), (m, n), pick, scale)


def _dw_colblocks(name, a, b, nj, tk):
    s, m = a.shape
    n = b.shape[1] // nj
    nk = s // tk

    def body(a_ref, b_ref, out_ref, out16_ref, acc):
        k = pl.program_id(0)

        @pl.when(k == 0)
        def _():
            acc[...] = jnp.zeros_like(acc)

        acc[...] += _mm(a_ref[...], b_ref[...], TN)

        @pl.when(k == nk - 1)
        def _():
            for j in range(nj):
                res = acc[:, j * n:(j + 1) * n]
                out_ref[j] = res
                out16_ref[j] = res.astype(CD)

    whole = pl.BlockSpec((nj, m, n), lambda k: (0, 0, 0))
    return pl.pallas_call(
        body, name=name, grid=(nk,),
        in_specs=[pl.BlockSpec((tk, m), lambda k: (k, 0)), pl.BlockSpec((tk, nj * n), lambda k: (k, 0))],
        out_specs=[whole, whole],
        out_shape=[jax.ShapeDtypeStruct((nj, m, n), F32), jax.ShapeDtypeStruct((nj, m, n), CD)],
        scratch_shapes=[pltpu.VMEM((m, nj * n), F32)], compiler_params=_params(1),
    )(a, b)


DP_WIDTHS = (768, 512, 512, 512, 512, 512, 2048)
DW_IN_COLS = 512


def _dw_in(dps, hb, tk):
    s, d = hb.shape
    nk = s // tk
    blocks, row = [], 0
    for p, width in enumerate(DP_WIDTHS):
        step = width if width <= 768 else DW_IN_COLS
        for c0 in range(0, width, step):
            blocks.append((p, c0, step, row))
            row += step
    nb, max_w = len(blocks), max(b[2] for b in blocks)
    first = [min(i for i, b in enumerate(blocks) if b[0] == p) for p in range(len(DP_WIDTHS))]
    count = [sum(1 for b in blocks if b[0] == p) for p in range(len(DP_WIDTHS))]

    def body(*refs):
        dp_refs, hb_ref, out_ref, out16_ref, acc, acc16, sems = refs[:7], refs[7], refs[8], refs[9], refs[10], refs[11], refs[12]
        b, k = pl.program_id(0), pl.program_id(1)
        rows = pl.ds(pl.multiple_of(k * tk, tk), tk)

        def writes(i):
            _, _, w, r0 = blocks[i]
            slot = i % 2
            return (pltpu.make_async_copy(acc.at[slot, 0:w], out_ref.at[r0:r0 + w], sems.at[slot, 0]),
                    pltpu.make_async_copy(acc16.at[slot, 0:w], out16_ref.at[r0:r0 + w], sems.at[slot, 1]))

        for i, (p, _, w, _) in enumerate(blocks):
            @pl.when(b == i)
            def _(i=i, p=p, w=w):
                slot = i % 2
                prod = _mm(dp_refs[p][...], hb_ref[rows, :], TN)

                @pl.when(k == 0)
                def _():
                    acc[slot, 0:w] = prod

                @pl.when(k > 0)
                def _():
                    acc[slot, 0:w] += prod

                @pl.when(k == nk - 1)
                def _():
                    if i >= 1:
                        for cp in writes(i - 1):
                            cp.wait()
                    acc16[slot, 0:w] = acc[slot, 0:w].astype(CD)
                    for cp in writes(i):
                        cp.start()
                    if i == nb - 1:
                        for cp in writes(i):
                            cp.wait()

    def piece_spec(p):
        width = DP_WIDTHS[p]
        cols = width if width <= 768 else DW_IN_COLS

        def imap(b, k):
            active = (b >= first[p]) & (b < first[p] + count[p])
            return (jnp.where(active, k, jnp.where(b < first[p], 0, nk - 1)), jnp.clip(b - first[p], 0, count[p] - 1))

        return pl.BlockSpec((tk, cols), imap)

    return pl.pallas_call(
        body, name="dw_in", grid=(nb, nk),
        in_specs=[piece_spec(p) for p in range(len(DP_WIDTHS))] + [pl.BlockSpec((s, d), lambda b, k: (0, 0))],
        out_specs=[ANY, ANY],
        out_shape=[jax.ShapeDtypeStruct((sum(DP_WIDTHS), d), F32), jax.ShapeDtypeStruct((sum(DP_WIDTHS), d), CD)],
        scratch_shapes=[pltpu.VMEM((2, max_w, d), F32), pltpu.VMEM((2, max_w, d), CD), pltpu.SemaphoreType.DMA((2, 2))],
        compiler_params=_params(2),
    )(*dps, hb)


def _mix_in_fwd(x, g, w_t, tm):
    s, d = x.shape
    n_in = w_t.shape[0]

    def body(x_ref, g_ref, w_ref, qkv_ref, hg_ref, gt_ref, hb_ref):
        xv = x_ref[...]
        r = lax.rsqrt(jnp.mean(xv * xv, axis=-1, keepdims=True) + EPS)
        h = (xv * r * g_ref[...]).astype(CD)
        hb_ref[...] = h
        off = DP_WIDTHS[0]
        qkv_ref[...] = _mm(h, w_ref[0:off, :], NT)
        for c, width in enumerate(DP_WIDTHS[1:6]):
            hg_ref[:, c * width:(c + 1) * width] = _mm(h, w_ref[off:off + width, :], NT)
            off += width
        gate = DP_WIDTHS[6] // 2
        for c in range(2):
            gt_ref[:, c * gate:(c + 1) * gate] = _mm(h, w_ref[off:off + gate, :], NT).astype(CD)
            off += gate

    row = lambda w: pl.BlockSpec((tm, w), lambda i: (i, 0))
    return pl.pallas_call(
        body, name="mix_in_fwd", grid=(s // tm,),
        in_specs=[row(d), pl.BlockSpec((1, d), lambda i: (0, 0)), pl.BlockSpec((n_in, d), lambda i: (0, 0))],
        out_specs=[row(768), row(2560), row(2048), row(d)],
        out_shape=[jax.ShapeDtypeStruct((s, 768), F32), jax.ShapeDtypeStruct((s, 2560), F32),
                   jax.ShapeDtypeStruct((s, 2048), CD), jax.ShapeDtypeStruct((s, d), CD)],
        compiler_params=_params(1),
    )(x, g, w_t)


def _mix_in_bwd(dps, w_t, x, dres, g, tm, after=()):
    s, d = x.shape
    n_in = w_t.shape[0]

    def body(*refs):
        refs = refs[len(after):]
        dp_refs = refs[:7]
        w_ref, x_ref, dr_ref, g_ref, dx_ref, dg_ref = refs[7:]

        @pl.when(pl.program_id(0) == 0)
        def _():
            dg_ref[...] = jnp.zeros_like(dg_ref)

        dhv = jnp.zeros((tm, d), F32)
        off = 0
        for ref, width in zip(dp_refs, DP_WIDTHS):
            dhv = dhv + _mm(ref[...], w_ref[off:off + width, :], NN)
            off += width
        xv = x_ref[...]
        r = lax.rsqrt(jnp.mean(xv * xv, axis=-1, keepdims=True) + EPS)
        u = dhv * g_ref[...]
        dx_ref[...] = dr_ref[...] + r * u - xv * (r * r * r) * jnp.mean(u * xv, axis=-1, keepdims=True)
        dg_ref[...] += jnp.sum(dhv * xv * r, axis=0, keepdims=True)

    row = pl.BlockSpec((tm, d), lambda i: (i, 0))
    vec = pl.BlockSpec((1, d), lambda i: (0, 0))
    return pl.pallas_call(
        body, name="mix_in_bwd", grid=(s // tm,),
        in_specs=[ANY] * len(after) + [pl.BlockSpec((tm, w), lambda i: (i, 0)) for w in DP_WIDTHS]
        + [pl.BlockSpec((n_in, d), lambda i: (0, 0)), row, row, vec],
        out_specs=[row, vec],
        out_shape=[jax.ShapeDtypeStruct((s, d), F32), jax.ShapeDtypeStruct((1, d), F32)],
        compiler_params=_params(1),
    )(*after, *dps, w_t, x, dres, g)


def _headnorm_rope(x, gain, cos, sin, blocksum, swap):
    ss = _xdot(x * x, blocksum)
    r = lax.rsqrt(ss * (1.0 / HEAD_DIM) + EPS)
    y = x * r * gain
    return y * cos + _xdot(y, swap) * sin, r


def _headnorm_rope_bwd(dz, x, gain, cos, sin, blocksum, swap):
    ss = _xdot(x * x, blocksum)
    r = lax.rsqrt(ss * (1.0 / HEAD_DIM) + EPS)
    dy = dz * cos + _xdot(dz * sin, swap)
    u = dy * gain
    mean_ux = _xdot(u * x, blocksum) * (1.0 / HEAD_DIM)
    dx = r * u - x * (r * r * r) * mean_ux
    return dx, jnp.sum(dy * x * r, axis=0, keepdims=True)


def _qk_prep(pqkv, gq, gk, cos2, sin2, tm):
    s = pqkv.shape[0]
    bs512, sw512, eq, swh = _bf(_np_blocksum(512)), _bf(_np_swap32(512)), _bf(_np_expand_q()), _bf(_np_swap_halves())

    def body(q_ref, kv_ref, gq_ref, gk_ref, c_ref, s_ref, bs_ref, sw_ref, eq_ref, swh_ref, qe_ref, k_ref, v_ref, vs_ref):
        c2, s2 = c_ref[...], s_ref[...]
        c8, s8 = jnp.tile(c2, (1, 4)), jnp.tile(s2, (1, 4))
        bs, sw = bs_ref[...], sw_ref[...]
        zq, _ = _headnorm_rope(q_ref[...], gq_ref[...], c8, s8, bs, sw)
        qe_ref[...] = _mm(zq * (HEAD_DIM ** -0.5), eq_ref[...], NN).astype(CD)
        kv = kv_ref[...]
        zk, _ = _headnorm_rope(kv[:, :LANES], gk_ref[...], c2, s2, bs[:LANES, :LANES], sw[:LANES, :LANES])
        k_ref[...] = zk.astype(CD)
        v = kv[:, LANES:]
        v_ref[...] = v.astype(CD)
        vs_ref[...] = _mm(v, swh_ref[...], NN).astype(CD)

    full = lambda a: pl.BlockSpec(a.shape, lambda i: (0,) * a.ndim)
    tab = pl.BlockSpec((tm, LANES), lambda i: (i, 0))
    return pl.pallas_call(
        body, name="qk_prep", grid=(s // tm,),
        in_specs=[pl.BlockSpec((tm, 512), lambda i: (i, 0)), pl.BlockSpec((tm, 256), lambda i: (i, 2)),
                  full(gq), full(gk), tab, tab, full(bs512), full(sw512), full(eq), full(swh)],
        out_specs=[pl.BlockSpec((tm, 1024), lambda i: (i, 0)), tab, tab, tab],
        out_shape=[jax.ShapeDtypeStruct((s, 1024), CD)] + [jax.ShapeDtypeStruct((s, LANES), CD)] * 3,
        compiler_params=_params(1),
    )(pqkv, pqkv, gq, gk, cos2, sin2, bs512, sw512, eq, swh)


def _qk_prep_bwd(pqkv, dq, dk, dv, gq, gk, cos2, sin2, tm):
    s = pqkv.shape[0]
    bs512, sw512 = _bf(_np_blocksum(512)), _bf(_np_swap32(512))

    def body(q_ref, kv_ref, dq_ref, dk_ref, dv_ref, gq_ref, gk_ref, c_ref, s_ref, bs_ref, sw_ref,
             dp_ref, dgq_ref, dgk_ref):
        @pl.when(pl.program_id(0) == 0)
        def _():
            dgq_ref[...] = jnp.zeros_like(dgq_ref)
            dgk_ref[...] = jnp.zeros_like(dgk_ref)

        c2, s2 = c_ref[...], s_ref[...]
        c8, s8 = jnp.tile(c2, (1, 4)), jnp.tile(s2, (1, 4))
        bs, sw = bs_ref[...], sw_ref[...]
        dzq = dq_ref[...] * (HEAD_DIM ** -0.5)
        dxq, dgq = _headnorm_rope_bwd(dzq, q_ref[...], gq_ref[...], c8, s8, bs, sw)
        kv = kv_ref[...]
        dxk, dgk = _headnorm_rope_bwd(dk_ref[...], kv[:, :LANES], gk_ref[...], c2, s2, bs[:LANES, :LANES], sw[:LANES, :LANES])
        dp_ref[...] = jnp.concatenate([dxq, dxk, dv_ref[...]], axis=1).astype(CD)
        dgq_ref[...] += dgq
        dgk_ref[...] += dgk

    full = lambda a: pl.BlockSpec(a.shape, lambda i: (0,) * a.ndim)
    tab = pl.BlockSpec((tm, LANES), lambda i: (i, 0))
    return pl.pallas_call(
        body, name="qk_prep_bwd", grid=(s // tm,),
        in_specs=[pl.BlockSpec((tm, 512), lambda i: (i, 0)), pl.BlockSpec((tm, 256), lambda i: (i, 2)),
                  pl.BlockSpec((tm, 512), lambda i: (i, 0)), tab, tab, full(gq), full(gk), tab, tab,
                  full(bs512), full(sw512)],
        out_specs=[pl.BlockSpec((tm, 768), lambda i: (i, 0)), pl.BlockSpec((1, 512), lambda i: (0, 0)),
                   pl.BlockSpec((1, LANES), lambda i: (0, 0))],
        out_shape=[jax.ShapeDtypeStruct((s, 768), CD), jax.ShapeDtypeStruct((1, 512), F32),
                   jax.ShapeDtypeStruct((1, LANES), F32)],
        compiler_params=_params(1),
    )(pqkv, pqkv, dq, dk, dv, gq, gk, cos2, sin2, bs512, sw512)


def _kv_rows(h):
    return pl.ds(pl.multiple_of((h // 4) * HEAD_DIM, HEAD_DIM), HEAD_DIM)


def _attn_fwd(qe, k, v, vs, tq):
    s = k.shape[0]

    def body(q0_ref, q1_ref, q2_ref, q3_ref, k_ref, v_ref, vs_ref, o_ref, lse_ref):
        grp = pl.program_id(0)
        kk = k_ref[...]
        outs = []
        for r, q_ref in enumerate((q0_ref, q1_ref, q2_ref, q3_ref)):
            sc = _mm(q_ref[...], kk, NT)
            mx = jnp.max(sc, axis=-1, keepdims=True)
            e = jnp.exp(sc - mx)
            l = jnp.sum(e, axis=-1, keepdims=True)
            lse_ref[r] = mx + jnp.log(l)
            vsel = jnp.where(grp != r % 2, vs_ref[...], v_ref[...])
            outs.append(_mm(e, vsel, NN) * (1.0 / l))
        low = lax.broadcasted_iota(jnp.int32, (1, LANES), 1) < HEAD_DIM
        o_ref[...] = jnp.concatenate([jnp.where(low, outs[0], outs[1]), jnp.where(low, outs[2], outs[3])], axis=1)

    kv = pl.BlockSpec((s, LANES), lambda g, i: (0, 0))
    qblk = lambda r: pl.BlockSpec((tq, LANES), lambda g, i: (i, 4 * g + r))
    return pl.pallas_call(
        body, name="attn_fwd", grid=(2, s // tq),
        in_specs=[qblk(0), qblk(1), qblk(2), qblk(3), kv, kv, kv],
        out_specs=[pl.BlockSpec((tq, 2 * LANES), lambda g, i: (i, g)), pl.BlockSpec((4, tq, 1), lambda g, i: (g, i, 0))],
        out_shape=[jax.ShapeDtypeStruct((s, 512), F32), jax.ShapeDtypeStruct((8, s, 1), F32)],
        compiler_params=_params(2),
    )(qe, qe, qe, qe, k, v, vs)


def _attn_bwd(qe, k, kt, v, doe, delta, lse, tq):
    s = k.shape[0]

    nh = ATTN_BWD_HEADS

    def body(*refs):
        q_refs, (k_ref, kt_ref, v_ref) = refs[:nh], refs[nh:nh + 3]
        do_refs, dl_refs = refs[nh + 3:2 * nh + 3], refs[2 * nh + 3:3 * nh + 3]
        lse_ref, dqt_ref, dkt_ref, dvt_ref, qt, dot = refs[3 * nh + 3:]

        @pl.when((pl.program_id(0) == 0) & (pl.program_id(1) == 0))
        def _():
            dkt_ref[...] = jnp.zeros_like(dkt_ref)
            dvt_ref[...] = jnp.zeros_like(dvt_ref)

        rows = _kv_rows(nh * pl.program_id(0))
        kt = kt_ref[rows, :]
        dkt = jnp.zeros((HEAD_DIM, s), F32)
        dvt = jnp.zeros((HEAD_DIM, s), F32)
        for idx, (q_ref, do_ref, dl_ref) in enumerate(zip(q_refs, do_refs, dl_refs)):
            q, do = q_ref[...], do_ref[...]
            p = jnp.exp(_mm(q, k_ref[...], NT) - lse_ref[idx])
            dp = _mm(do, v_ref[...], NT)
            ds = p * (dp - jnp.max(dl_ref[...], axis=-1, keepdims=True))
            dqt_ref[idx * HEAD_DIM:(idx + 1) * HEAD_DIM, :] = _mm(kt, ds, NT)
            qt[idx] = jnp.transpose(q.astype(F32))
            dot[idx] = jnp.transpose(do.astype(F32))
            dkt = dkt + _mm(qt[idx, rows, :], ds, NN)
            dvt = dvt + _mm(dot[idx, rows, :], p, NN)
        dkt_ref[rows, :] += dkt
        dvt_ref[rows, :] += dvt

    kv = pl.BlockSpec((s, LANES), lambda m, i: (0, 0))
    kvt = pl.BlockSpec((LANES, s), lambda m, i: (0, 0))
    blks = [pl.BlockSpec((tq, LANES), lambda m, i, r=r: (i, nh * m + r)) for r in range(nh)]
    return pl.pallas_call(
        body, name="attn_bwd", grid=(8 // nh, s // tq),
        in_specs=blks + [kv, kvt, kv] + blks + blks + [pl.BlockSpec((nh, tq, 1), lambda m, i: (m, i, 0))],
        out_specs=[pl.BlockSpec((nh * HEAD_DIM, tq), lambda m, i: (m, i)), kvt, kvt],
        out_shape=[jax.ShapeDtypeStruct((8 * HEAD_DIM, s), F32), jax.ShapeDtypeStruct((LANES, s), F32),
                   jax.ShapeDtypeStruct((LANES, s), F32)],
        scratch_shapes=[pltpu.VMEM((nh, LANES, tq), F32), pltpu.VMEM((nh, LANES, tq), F32)],
        compiler_params=_params(2),
    )(*[qe] * nh, k, kt, v, *[doe] * nh, *[delta] * nh, lse)


@jax.custom_vjp
def _mm_nn(a, b):
    return _mm(a, b, NN)


_mm_nn.defvjp(lambda a, b: (_mm(a, b, NN), (a, b)),
              lambda res, g: (_mm(g, res[1], NT), _mm(res[0], g, TN)))


@jax.custom_vjp
def _mm_nt(a, b):
    return _mm(a, b, NT)


_mm_nt.defvjp(lambda a, b: (_mm(a, b, NT), (a, b)),
              lambda res, g: (_mm(g, res[1], NN), _mm(g, res[0], TN)))


@jax.custom_vjp
def _mm_tn(a, b):
    return _mm(a, b, TN)


_mm_tn.defvjp(lambda a, b: (_mm(a, b, TN), (a, b)),
              lambda res, g: (_mm(res[1], g, NT), _mm(res[0], g, NN)))


@jax.custom_vjp
def _cmm(m, mt, x):
    return _xdot_l(m, x)


_cmm.defvjp(lambda m, mt, x: (_xdot_l(m, x), (m, mt)),
            lambda res, g: (jnp.zeros_like(res[0]), jnp.zeros_like(res[1]), _xdot_l(res[1], g)))


def _hgrn_masks(t, rev):
    n_ch = t // CHUNK
    r = jnp.bitwise_and(lax.broadcasted_iota(jnp.int32, (2 * t, t), 0), t - 1)
    c = lax.broadcasted_iota(jnp.int32, (2 * t, t), 1)
    same = jnp.right_shift(r, 5) == jnp.right_shift(c, 5)
    tri2 = same & ((c >= r) if rev else (c <= r))
    pr = lax.broadcasted_iota(jnp.int32, (LANES, LANES), 0)
    pc = lax.broadcasted_iota(jnp.int32, (LANES, LANES), 1)
    diag = jnp.right_shift(pr, 6) == jnp.right_shift(pc, 6)
    qr = lax.broadcasted_iota(jnp.int32, (t, n_ch * LANES), 0)
    qc = lax.broadcasted_iota(jnp.int32, (t, n_ch * LANES), 1)
    rows_chunk = jnp.right_shift(qc, 7) == jnp.right_shift(qr, 5)
    vr = lax.broadcasted_iota(jnp.int32, (n_ch * LANES, t), 0)
    vc = lax.broadcasted_iota(jnp.int32, (n_ch * LANES, t), 1)
    cols_chunk = jnp.right_shift(vr, 7) == jnp.right_shift(vc, 5)
    return dict(tri2=tri2, diag=diag, rows_chunk=rows_chunk, cols_chunk=cols_chunk)


def _hgrn_gates(xf, lb):
    f = lb + (1.0 - lb) * _sigmoid(xf)
    return 1.0 - f, jnp.log(f)


def _hgrn_dir(xq, xf, v, lb, state, cm, cmt, mk, rev):
    t = xq.shape[0]
    n_ch = t // CHUNK
    lo = lax.broadcasted_iota(jnp.int32, (1, LANES), 1) < HEAD_DIM
    q = xq * _sigmoid(xq)
    k, lf = _hgrn_gates(xf, lb)
    cs = _cmm(cm, cmt, lf)
    b, bm, bl = cs[:t], cs[t:2 * t], cs[2 * t:]
    qd = q * jnp.exp(b - bm)
    kd = k * jnp.exp(bm - b)
    kc = k * jnp.exp(bl - b)
    qe = q * jnp.exp(b)
    qd2 = jnp.concatenate([jnp.where(lo, qd, 0.0), jnp.where(lo, 0.0, qd)], axis=0)
    o2 = _mm_nn(jnp.where(mk["tri2"], _mm_nt(qd2, kd), 0.0), v)
    o = jnp.where(lo, o2[:t], o2[t:])
    vexp = jnp.where(mk["cols_chunk"], jnp.concatenate([jnp.transpose(v)] * n_ch, axis=0), 0.0)
    adds = _mm_nn(vexp, kc)
    dec = jnp.exp(bl)
    entering = [None] * n_ch
    for c in (range(n_ch - 1, -1, -1) if rev else range(n_ch)):
        entering[c] = state
        d = jnp.concatenate([dec[c * CHUNK:(c + 1) * CHUNK]] * (LANES // CHUNK), axis=0)
        state = d * state + jnp.where(mk["diag"], adds[c * LANES:(c + 1) * LANES], 0.0)
    qexp = jnp.where(mk["rows_chunk"], jnp.concatenate([qe] * n_ch, axis=1), 0.0)
    return o + _mm_nt(qexp, jnp.concatenate(entering, axis=1)), state


def _hgrn_lower_bounds(l):
    out = []
    for d in (0, 1):
        l0, l1 = l[2 * d:2 * d + 1, :], l[2 * d + 1:2 * d + 2, :]
        mx = jnp.maximum(l0, l1)
        e0, e1 = jnp.exp(l0 - mx), jnp.exp(l1 - mx)
        out.append(e0 / (e0 + e1))
    return out


def _hgrn_consts(t):
    cf, cb = _np_hgrn_cums(t, False), _np_hgrn_cums(t, True)
    return (_bf(cf), _bf(cf.T), _bf(cb), _bf(cb.T), _bf(_np_blocksum(LANES)))


def _hgrn_fwd(ph, lbl, ng):
    s = ph.shape[0]
    t = min(HG_TILE, s)
    nt = s // t
    consts = _hgrn_consts(t)

    def body(xq_ref, xff_ref, xfb_ref, xi_ref, xg_ref, lbl_ref, ng_ref, cf_ref, cft_ref, cb_ref, cbt_ref, bs_ref,
             o_ref, pre_ref, st_ref, acc):
        lbf, lbb = _hgrn_lower_bounds(lbl_ref)
        mk_f, mk_b = _hgrn_masks(t, False), _hgrn_masks(t, True)
        zero = jnp.zeros((LANES, LANES), F32)

        def rows_of(i):
            return pl.ds(pl.multiple_of(i * t, t), t)

        acc[...] = jnp.zeros_like(acc)

        def step(i, states):
            tb = nt - 1 - i
            rf, rb = rows_of(i), rows_of(tb)
            st_ref[0, 0, i] = states[0]
            st_ref[0, 1, tb] = states[1]
            of, sf = _hgrn_dir(xq_ref[rf, :], xff_ref[rf, :], xi_ref[rf, :], lbf, states[0],
                               cf_ref[...], cft_ref[...], mk_f, False)
            ob, sb = _hgrn_dir(xq_ref[rb, :], xfb_ref[rb, :], xi_ref[rb, :], lbb, states[1],
                               cb_ref[...], cbt_ref[...], mk_b, True)
            acc[rf, :] += of
            acc[rb, :] += ob
            return sf, sb

        lax.fori_loop(0, nt, step, (zero, zero))

        def step_n(i, carry):
            rows = rows_of(i)
            o = acc[rows, :]
            ss = _xdot(o * o, bs_ref[...])
            r = lax.rsqrt(ss * (1.0 / HEAD_DIM) + EPS)
            xg = xg_ref[rows, :]
            pre_ref[rows, :] = o
            o_ref[rows, :] = ((o * r * ng_ref[...]) * (xg * _sigmoid(xg))).astype(CD)
            return carry

        lax.fori_loop(0, nt, step_n, 0)

    col = lambda off: pl.BlockSpec((s, LANES), lambda m: (0, off + m))
    full = lambda a: pl.BlockSpec(a.shape, lambda m: (0,) * a.ndim)
    return pl.pallas_call(
        body, name="hgrn_fwd", grid=(4,),
        in_specs=[col(0), col(4), col(8), col(12), col(16), pl.BlockSpec((4, LANES), lambda m: (0, m)),
                  pl.BlockSpec((1, LANES), lambda m: (0, m))] + [full(c) for c in consts],
        out_specs=[col(0), col(0), pl.BlockSpec((1, 2, nt, LANES, LANES), lambda m: (m, 0, 0, 0, 0))],
        out_shape=[jax.ShapeDtypeStruct((s, 512), CD), jax.ShapeDtypeStruct((s, 512), F32),
                   jax.ShapeDtypeStruct((4, 2, nt, LANES, LANES), F32)],
        scratch_shapes=[pltpu.VMEM((s, LANES), F32)],
        compiler_params=_params(1),
    )(ph, ph, ph, ph, ph, lbl, ng, *consts)


def _hgrn_bwd(ph, pre, dout, states, lbl, ng):
    s = ph.shape[0]
    t = min(HG_TILE, s)
    nt = s // t
    consts = _hgrn_consts(t)

    def body(xq_ref, xff_ref, xfb_ref, xi_ref, xg_ref, pre_ref, do_ref, st_ref, lbl_ref, ng_ref,
             cf_ref, cft_ref, cb_ref, cbt_ref, bs_ref,
             dq_ref, dff_ref, dfb_ref, di_ref, dg_ref, dlb_ref, dng_ref, dpre, dq_acc, dv_acc):
        lbf, lbb = _hgrn_lower_bounds(lbl_ref)
        mk_f, mk_b = _hgrn_masks(t, False), _hgrn_masks(t, True)
        zero = jnp.zeros((LANES, LANES), F32)
        zrow = jnp.zeros((1, LANES), F32)

        def rows_of(i):
            return pl.ds(pl.multiple_of(i * t, t), t)

        def step_n(i, dng):
            rows = rows_of(i)
            o, xg, do = pre_ref[rows, :], xg_ref[rows, :], do_ref[rows, :]
            bs = bs_ref[...]
            r = lax.rsqrt(_xdot(o * o, bs) * (1.0 / HEAD_DIM) + EPS)
            sg = _sigmoid(xg)
            gate = xg * sg
            don = do * gate
            dg_ref[rows, :] = (do * (o * r * ng_ref[...]) * (sg * (1.0 + xg * (1.0 - sg)))).astype(CD)
            u = don * ng_ref[...]
            dpre[rows, :] = r * u - o * (r * r * r) * (_xdot(u * o, bs) * (1.0 / HEAD_DIM))
            return dng + jnp.sum(don * o * r, axis=0, keepdims=True)

        dng_ref[...] = lax.fori_loop(0, nt, step_n, zrow)

        dq_acc[...] = jnp.zeros_like(dq_acc)
        dv_acc[...] = jnp.zeros_like(dv_acc)

        def grad_tile(ti, xf_ref, df_ref, lb, cm, cmt, mk, rev, st, dstate):
            rows = rows_of(ti)
            fn = lambda xq, xf, v, lbv, s_in: _hgrn_dir(xq, xf, v, lbv, s_in, cm, cmt, mk, rev)
            _, vjp = jax.vjp(fn, xq_ref[rows, :], xf_ref[rows, :], xi_ref[rows, :], lb, st)
            dxq, dxf, dv, dlb_t, dstate = vjp((dpre[rows, :], dstate))
            df_ref[rows, :] = dxf.astype(CD)
            dq_acc[rows, :] += dxq
            dv_acc[rows, :] += dv
            return dstate, dlb_t

        def step_g(i, carry):
            dsf, dsb, dlbf, dlbb = carry
            tf, tb = nt - 1 - i, i
            dsf, gf = grad_tile(tf, xff_ref, dff_ref, lbf, cf_ref[...], cft_ref[...], mk_f, False, st_ref[0, 0, tf], dsf)
            dsb, gb = grad_tile(tb, xfb_ref, dfb_ref, lbb, cb_ref[...], cbt_ref[...], mk_b, True, st_ref[0, 1, tb], dsb)
            return dsf, dsb, dlbf + gf, dlbb + gb

        _, _, dlbf, dlbb = lax.fori_loop(0, nt, step_g, (zero, zero, zrow, zrow))
        dlb_ref[0:1, :] = dlbf
        dlb_ref[1:2, :] = dlbb
        dq_ref[...] = dq_acc[...].astype(CD)
        di_ref[...] = dv_acc[...].astype(CD)

    col = lambda off: pl.BlockSpec((s, LANES), lambda m: (0, off + m))
    full = lambda a: pl.BlockSpec(a.shape, lambda m: (0,) * a.ndim)
    stream = jax.ShapeDtypeStruct((s, 512), CD)
    return pl.pallas_call(
        body, name="hgrn_bwd", grid=(4,),
        in_specs=[col(0), col(4), col(8), col(12), col(16), col(0), col(0),
                  pl.BlockSpec((1, 2, nt, LANES, LANES), lambda m: (m, 0, 0, 0, 0)),
                  pl.BlockSpec((4, LANES), lambda m: (0, m)),
                  pl.BlockSpec((1, LANES), lambda m: (0, m))] + [full(c) for c in consts],
        out_specs=[col(0)] * 5 + [pl.BlockSpec((2, LANES), lambda m: (0, m)), pl.BlockSpec((1, LANES), lambda m: (0, m))],
        out_shape=[stream] * 5 + [jax.ShapeDtypeStruct((2, 512), F32), jax.ShapeDtypeStruct((1, 512), F32)],
        scratch_shapes=[pltpu.VMEM((s, LANES), F32), pltpu.VMEM((s, LANES), F32), pltpu.VMEM((s, LANES), F32)],
        compiler_params=_params(1),
    )(ph, ph, ph, ph, ph, pre, dout, states, lbl, ng, *consts)


def _branch_out(o, w4):
    o = o.astype(CD)
    return jnp.concatenate([_mm(o, w4[j], NN) for j in range(N_SHARD)], axis=1)


def _mix_out_fwd(x, oa, ob, pg, wa, wb, wo, tm):
    s, d = x.shape

    def body(x_ref, oa_ref, ob_ref, ga_ref, gb_ref, wa_ref, wb_ref, wo_ref, xo_ref):
        ya = _branch_out(oa_ref[...], wa_ref)
        yb = _branch_out(ob_ref[...], wb_ref)
        merged = _sigmoid(ga_ref[...].astype(F32)) * ya + _sigmoid(gb_ref[...].astype(F32)) * yb
        xo_ref[...] = x_ref[...] + _mm(merged, wo_ref[...], NN)

    row = pl.BlockSpec((tm, d), lambda i: (i, 0))
    half = pl.BlockSpec((tm, 512), lambda i: (i, 0))
    full = lambda a: pl.BlockSpec(a.shape, lambda i: (0,) * a.ndim)
    return pl.pallas_call(
        body, name="mix_out_fwd", grid=(s // tm,),
        in_specs=[row, half, half, row, pl.BlockSpec((tm, d), lambda i: (i, 1)), full(wa), full(wb), full(wo)],
        out_specs=row, out_shape=jax.ShapeDtypeStruct((s, d), F32),
        compiler_params=_params(1),
    )(x, oa, ob, pg, pg, wa, wb, wo)


def _mix_out_bwd(dx, oa, ob, pg, wa, wb, wo, tm, after=()):
    s, d = dx.shape
    eq, ebc = _bf(_np_expand_q()), _bf(_np_headsum_spread())

    def body(*refs):
        (dx_ref, oa_ref, ob_ref, ga_ref, gb_ref, wa_ref, wb_ref, wo_ref, eq_ref, ebc_ref,
         dpg_ref, mg_ref, dya_ref, dyb_ref, doe_ref, dl_ref, dob_ref) = refs[len(after):]
        oa = oa_ref[...]
        ya = _branch_out(oa, wa_ref)
        yb = _branch_out(ob_ref[...], wb_ref)
        sa, sb = _sigmoid(ga_ref[...].astype(F32)), _sigmoid(gb_ref[...].astype(F32))
        mg_ref[...] = (sa * ya + sb * yb).astype(CD)
        dm = _mm(dx_ref[...], wo_ref[...], NT)
        dpg_ref[...] = jnp.concatenate([dm * ya * sa * (1.0 - sa), dm * yb * sb * (1.0 - sb)], axis=1).astype(CD)
        dya, dyb = dm * sa, dm * sb
        dya_ref[...] = dya.astype(CD)
        dyb_ref[...] = dyb.astype(CD)
        doa = jnp.zeros(oa.shape, F32)
        dob = jnp.zeros(oa.shape, F32)
        for j in range(N_SHARD):
            doa = doa + _mm(dya[:, 256 * j:256 * j + 256], wa_ref[j], NT)
            dob = dob + _mm(dyb[:, 256 * j:256 * j + 256], wb_ref[j], NT)
        dob_ref[...] = dob
        doe_ref[...] = _mm(doa, eq_ref[...], NN).astype(CD)
        dl_ref[...] = _xdot(doa * oa, ebc_ref[...])

    row = pl.BlockSpec((tm, d), lambda i: (i, 0))
    half = pl.BlockSpec((tm, 512), lambda i: (i, 0))
    full = lambda a: pl.BlockSpec(a.shape, lambda i: (0,) * a.ndim)
    wide = jax.ShapeDtypeStruct((s, d), CD)
    return pl.pallas_call(
        body, name="mix_out_bwd", grid=(s // tm,),
        in_specs=[ANY] * len(after) + [row, half, half, row, pl.BlockSpec((tm, d), lambda i: (i, 1)), full(wa), full(wb),
                                       full(wo), full(eq), full(ebc)],
        out_specs=[pl.BlockSpec((tm, 2048), lambda i: (i, 0)), row, row, row, row, row, half],
        out_shape=[jax.ShapeDtypeStruct((s, 2048), CD), wide, wide, wide, wide, jax.ShapeDtypeStruct((s, d), F32),
                   jax.ShapeDtypeStruct((s, 512), F32)],
        compiler_params=_params(1),
    )(*after, dx, oa, ob, pg, pg, wa, wb, wo, eq, ebc)


def _loss_head(x, g, target, tm):
    s, d = x.shape

    def body(x_ref, g_ref, t_ref, dx_ref, loss_ref, dg_ref):
        @pl.when(pl.program_id(0) == 0)
        def _():
            loss_ref[...] = jnp.zeros_like(loss_ref)
            dg_ref[...] = jnp.zeros_like(dg_ref)

        xv = x_ref[...]
        r = lax.rsqrt(jnp.mean(xv * xv, axis=-1, keepdims=True) + EPS)
        err = xv * r * g_ref[...] - t_ref[...]
        loss_ref[...] += 0.5 * jnp.sum(jnp.mean(err * err, axis=-1, keepdims=True))
        dy = err * (1.0 / d)
        u = dy * g_ref[...]
        dx_ref[...] = r * u - xv * (r * r * r) * jnp.mean(u * xv, axis=-1, keepdims=True)
        dg_ref[...] += jnp.sum(dy * xv * r, axis=0, keepdims=True)

    row = pl.BlockSpec((tm, d), lambda i: (i, 0))
    vec = pl.BlockSpec((1, d), lambda i: (0, 0))
    return pl.pallas_call(
        body, name="loss_head", grid=(s // tm,),
        in_specs=[row, vec, row], out_specs=[row, pl.BlockSpec((8, LANES), lambda i: (0, 0)), vec],
        out_shape=[jax.ShapeDtypeStruct((s, d), F32), jax.ShapeDtypeStruct((8, LANES), F32),
                   jax.ShapeDtypeStruct((1, d), F32)],
        compiler_params=_params(1),
    )(x, g, target)


def _position():
    x, y, c = lax.axis_index("x"), lax.axis_index("y"), lax.axis_index("c")
    return x, y, c, [(1 - x, y), (x, 1 - y), (1 - x, 1 - y)]


def _row_tile(rows, cap=256):
    best = rows
    for cand in range(8, min(rows, cap) + 1, 8):
        if rows % cand == 0:
            best = cand
    return best


def _cast_into_slots(shards, dtypes, me_idx):
    n = len(shards)
    tiles = [_row_tile(s.shape[0]) for s in shards]
    counts = [s.shape[0] // t for s, t in zip(shards, tiles)]
    starts = [sum(counts[:a]) for a in range(n)]

    def body(me_ref, *refs):
        i = pl.program_id(0)
        for a in range(n):
            @pl.when((i >= starts[a]) & (i < starts[a] + counts[a]))
            def _(a=a):
                refs[n + a][0] = refs[a][...].astype(dtypes[a])

    tile_of = [lambda i, a=a: jnp.clip(i - starts[a], 0, counts[a] - 1) for a in range(n)]
    return pl.pallas_call(
        body, name="cast_into_slots",
        grid_spec=pltpu.PrefetchScalarGridSpec(
            num_scalar_prefetch=1, grid=(sum(counts),),
            in_specs=[pl.BlockSpec((tiles[a], shards[a].shape[1]), lambda i, me, a=a: (tile_of[a](i), 0)) for a in range(n)],
            out_specs=[pl.BlockSpec((1, tiles[a], shards[a].shape[1]), lambda i, me, a=a: (me[0], tile_of[a](i), 0))
                       for a in range(n)]),
        out_shape=[jax.ShapeDtypeStruct((N_SHARD,) + s.shape, dt) for s, dt in zip(shards, dtypes)],
        compiler_params=_params(1),
    )(me_idx, *shards)


HBM_SPEC = pl.BlockSpec(memory_space=pltpu.HBM)
SEM_SPEC = pl.BlockSpec(memory_space=pltpu.SEMAPHORE)
DATAFLOW = pltpu.SideEffectType.DATAFLOW_SIDE_EFFECTING


def _exchange_copies(srcs, lands, send, recv, gather):
    x, y, c, chips = _position()
    me = 2 * x + y
    out = []
    for a in range(len(lands)):
        dst = lands[a].at[me]
        if gather and _halved(lands[a]):
            half = lands[a].shape[1] // 2
            dst = lands[a].at[me, pl.ds(c * half, half), :]
        for k, (px, py) in enumerate(chips):
            src = dst if gather else srcs[a].at[2 * px + py]
            out.append(pltpu.make_async_remote_copy(src_ref=src, dst_ref=dst, send_sem=send.at[3 * a + k],
                                                    recv_sem=recv.at[3 * a + k], device_id=(px, py, c), device_id_type=MESH))
    return out


def _halved(land):
    return land.shape[1] % 32 == 0


def _pair_fill(name, lands):
    n = len(lands)

    def body(*refs):
        src, dst = refs[:n], refs[n:2 * n]
        send, recv = refs[2 * n:]
        x, y, c, chips = _position()
        copies = []
        for a in range(n):
            half = src[a].shape[1] // 2
            for k, (px, py) in enumerate(chips):
                rows = (2 * px + py, pl.ds(c * half, half), slice(None))
                cp = pltpu.make_async_remote_copy(src_ref=src[a].at[rows], dst_ref=dst[a].at[rows], send_sem=send.at[a, k],
                                                  recv_sem=recv.at[a, k], device_id=(x, y, 1 - c), device_id_type=MESH)
                cp.start()
                copies.append(cp)
        for cp in copies:
            cp.wait()

    return pl.pallas_call(
        body, name=name, in_specs=[ANY] * n, out_specs=[ANY] * n,
        out_shape=[jax.ShapeDtypeStruct(l.shape, l.dtype) for l in lands],
        input_output_aliases={a: a for a in range(n)},
        scratch_shapes=[pltpu.SemaphoreType.DMA((n, 3)), pltpu.SemaphoreType.DMA((n, 3))],
    )(*lands)


def _exchange_start(name, srcs, lands, after):
    ns, nl, na = len(srcs), len(lands), len(after)
    gather = ns == 0

    def body(*refs):
        src_refs, land_refs = refs[:ns], refs[ns:ns + nl]
        send, recv = refs[ns + nl + na], refs[ns + nl + na + 1]
        token = refs[-1]
        for cp in _exchange_copies(src_refs, land_refs, send, recv, gather):
            cp.start()
        token[...] = jnp.zeros_like(token)

    arrays = [pltpu.with_memory_space_constraint(a, pltpu.HBM) for a in list(srcs) + list(lands)]
    outs = pl.pallas_call(
        body, name=name,
        out_shape=(pltpu.SemaphoreType.DMA((3 * nl,)), pltpu.SemaphoreType.DMA((3 * nl,)),
                   *[pltpu.HBM(a.shape, a.dtype) for a in arrays], jax.ShapeDtypeStruct((8, LANES), F32)),
        in_specs=[HBM_SPEC] * (ns + nl) + [ANY] * na,
        out_specs=(SEM_SPEC, SEM_SPEC, *[HBM_SPEC] * (ns + nl), pl.BlockSpec(memory_space=pltpu.VMEM)),
        input_output_aliases={i: 2 + i for i in range(ns + nl)},
        compiler_params=pltpu.CompilerParams(has_side_effects=DATAFLOW),
    )(*arrays, *after)
    return outs[0], outs[1], list(outs[2:2 + ns]), list(outs[2 + ns:2 + ns + nl]), outs[-1]


def _exchange_wait(name, send, recv, srcs, lands, after):
    ns, nl, na = len(srcs), len(lands), len(after)
    gather = ns == 0

    def body(*refs):
        src_refs, land_refs = refs[:ns], refs[ns:ns + nl]
        send_ref, recv_ref = refs[ns + nl], refs[ns + nl + 1]
        for cp in _exchange_copies(src_refs, land_refs, send_ref, recv_ref, gather):
            cp.wait_send()
            cp.wait_recv()

    outs = pl.pallas_call(
        body, name=name,
        out_shape=tuple(pltpu.HBM(a.shape, a.dtype) for a in list(srcs) + list(lands)),
        in_specs=[HBM_SPEC] * (ns + nl) + [SEM_SPEC, SEM_SPEC] + [ANY] * na,
        out_specs=tuple([HBM_SPEC] * (ns + nl)),
        input_output_aliases={i: i for i in range(ns + nl)},
        compiler_params=pltpu.CompilerParams(has_side_effects=DATAFLOW),
    )(*srcs, *lands, send, recv, *after)
    return list(outs[ns:])


def _pair_exchange(grads):
    n = len(grads)

    def body(*refs):
        src, dst = refs[:n], refs[n:2 * n]
        send, recv = refs[2 * n:]
        x, y, c, _ = _position()
        copies = []
        for a in range(n):
            half = src[a].shape[1] // 2
            cp = pltpu.make_async_remote_copy(
                src_ref=src[a].at[:, pl.ds((1 - c) * half, half), :], dst_ref=dst[a], send_sem=send.at[a],
                recv_sem=recv.at[a], device_id=(x, y, 1 - c), device_id_type=MESH)
            cp.start()
            copies.append(cp)
        for cp in copies:
            cp.wait()

    return pl.pallas_call(
        body, name="grad_pair_exchange", in_specs=[ANY] * n, out_specs=[ANY] * n,
        out_shape=[jax.ShapeDtypeStruct((g.shape[0], g.shape[1] // 2, g.shape[2]), g.dtype) for g in grads],
        scratch_shapes=[pltpu.SemaphoreType.DMA((n,)), pltpu.SemaphoreType.DMA((n,))],
    )(*grads)


def _shard_of(a):
    return lambda i: jnp.clip(i - a * N_SHARD, 0, N_SHARD - 1)


def _pair_sum(gs, gots, c_idx, me_idx):
    n = len(gs)
    halves = [(g.shape[1] // 2, g.shape[2]) for g in gs]

    def body(c_ref, me_ref, *refs):
        g_refs, got_refs, s_refs, own_refs = (refs[k * n:(k + 1) * n] for k in range(4))
        i = pl.program_id(0)
        for a in range(n):
            @pl.when(i // N_SHARD == a)
            def _(a=a):
                sm = g_refs[a][...] + got_refs[a][...].astype(F32)
                s_refs[a][...] = sm.astype(CD)

                @pl.when(i % N_SHARD == me_ref[0])
                def _():
                    own_refs[a][...] = sm[0]

    shard = [_shard_of(a) for a in range(n)]
    return pl.pallas_call(
        body, name="grad_pair_sum",
        grid_spec=pltpu.PrefetchScalarGridSpec(
            num_scalar_prefetch=2, grid=(n * N_SHARD,),
            in_specs=[pl.BlockSpec((1, h, c_), lambda i, c, me, a=a: (shard[a](i), c[0], 0)) for a, (h, c_) in enumerate(halves)]
            + [pl.BlockSpec((1, h, c_), lambda i, c, me, a=a: (shard[a](i), 0, 0)) for a, (h, c_) in enumerate(halves)],
            out_specs=[pl.BlockSpec((1, h, c_), lambda i, c, me, a=a: (shard[a](i), 0, 0)) for a, (h, c_) in enumerate(halves)]
            + [pl.BlockSpec((h, c_), lambda i, c, me: (0, 0)) for h, c_ in halves]),
        out_shape=[jax.ShapeDtypeStruct((N_SHARD, h, c_), CD) for h, c_ in halves]
        + [jax.ShapeDtypeStruct((h, c_), F32) for h, c_ in halves],
        compiler_params=_params(1),
    )(c_idx, me_idx, *gs, *gots)


def _chip_sum(owns, gots, me_idx):
    n = len(owns)
    tiles = [_row_tile(o.shape[0]) for o in owns]
    counts = [o.shape[0] // t for o, t in zip(owns, tiles)]
    starts = [sum(counts[:a]) for a in range(n)]

    def body(me_ref, *refs):
        own_refs, got_refs, out_refs = (refs[k * n:(k + 1) * n] for k in range(3))
        i = pl.program_id(0)
        for a in range(n):
            @pl.when((i >= starts[a]) & (i < starts[a] + counts[a]))
            def _(a=a):
                total = None
                for j in range(N_SHARD):
                    term = jnp.where(j == me_ref[0], own_refs[a][...], got_refs[a][j].astype(F32))
                    total = term if total is None else total + term
                out_refs[a][...] = total

    tile_of = [lambda i, a=a: jnp.clip(i - starts[a], 0, counts[a] - 1) for a in range(n)]
    own_specs = [pl.BlockSpec((tiles[a], owns[a].shape[1]), lambda i, me, a=a: (tile_of[a](i), 0)) for a in range(n)]
    return pl.pallas_call(
        body, name="grad_chip_sum",
        grid_spec=pltpu.PrefetchScalarGridSpec(
            num_scalar_prefetch=1, grid=(sum(counts),),
            in_specs=own_specs + [pl.BlockSpec((N_SHARD, tiles[a], owns[a].shape[1]), lambda i, me, a=a: (0, tile_of[a](i), 0))
                                  for a in range(n)],
            out_specs=own_specs),
        out_shape=[jax.ShapeDtypeStruct(o.shape, F32) for o in owns],
        compiler_params=_params(1),
    )(me_idx, *owns, *gots)


def _pair_share(halves):
    n = len(halves)

    def body(*refs):
        src, dst = refs[:n], refs[n:2 * n]
        send, recv = refs[2 * n:]
        x, y, c, _ = _position()
        copies = []
        for a in range(n):
            cp = pltpu.make_async_remote_copy(src_ref=src[a], dst_ref=dst[a], send_sem=send.at[a],
                                              recv_sem=recv.at[a], device_id=(x, y, 1 - c), device_id_type=MESH)
            cp.start()
            copies.append(cp)
        for cp in copies:
            cp.wait()

    return pl.pallas_call(
        body, name="grad_pair_share", in_specs=[ANY] * n, out_specs=[ANY] * n,
        out_shape=[jax.ShapeDtypeStruct(h.shape, h.dtype) for h in halves],
        scratch_shapes=[pltpu.SemaphoreType.DMA((n,)), pltpu.SemaphoreType.DMA((n,))],
    )(*halves)


def _small_allreduce(buf):
    rows, cols = buf.shape

    def body(src_ref, out_ref, slots, send, recv):
        x, y, c, _ = _position()
        me = 4 * x + 2 * y + c
        slots[me] = src_ref[...]
        copies = []
        k = 0
        for dx in (0, 1):
            for dy in (0, 1):
                for dc in (0, 1):
                    if (dx, dy, dc) == (0, 0, 0):
                        continue
                    peer = (jnp.where(dx, 1 - x, x), jnp.where(dy, 1 - y, y), jnp.where(dc, 1 - c, c))
                    cp = pltpu.make_async_remote_copy(src_ref=src_ref, dst_ref=slots.at[me], send_sem=send.at[k],
                                                      recv_sem=recv.at[k], device_id=peer, device_id_type=MESH)
                    cp.start()
                    copies.append(cp)
                    k += 1
        for cp in copies:
            cp.wait()
        total = slots[0]
        for dev in range(1, N_DEV):
            total = total + slots[dev]
        out_ref[...] = total

    vm = pl.BlockSpec(memory_space=pltpu.VMEM)
    return pl.pallas_call(
        body, name="small_allreduce", in_specs=[vm], out_specs=vm,
        out_shape=jax.ShapeDtypeStruct((rows, cols), F32),
        scratch_shapes=[pltpu.VMEM((N_DEV, rows, cols), F32), pltpu.SemaphoreType.DMA((N_DEV - 1,)),
                        pltpu.SemaphoreType.DMA((N_DEV - 1,))],
    )(buf)


def _adamw_math(w, gv, m, v):
    mn = ADAM_B1 * m + (1.0 - ADAM_B1) * gv
    vn = ADAM_B2 * v + (1.0 - ADAM_B2) * (gv * gv)
    m_hat = mn / (1.0 - ADAM_B1 ** ADAM_STEP)
    v_hat = vn / (1.0 - ADAM_B2 ** ADAM_STEP)
    return -ADAM_LR * (m_hat / (jnp.sqrt(v_hat) + ADAM_EPS) + ADAM_WD * w), mn, vn


def _adamw(w, g, m, v):
    rows, cols = w.shape
    tr = _row_tile(rows)

    def body(w_ref, g_ref, m_ref, v_ref, d_ref, mo_ref, vo_ref):
        d_ref[...], mo_ref[...], vo_ref[...] = _adamw_math(w_ref[...], g_ref[...], m_ref[...], v_ref[...])

    blk = pl.BlockSpec((tr, cols), lambda i: (i, 0))
    shp = jax.ShapeDtypeStruct((rows, cols), F32)
    return pl.pallas_call(
        body, name="adamw", grid=(rows // tr,), in_specs=[blk] * 4, out_specs=[blk] * 3, out_shape=[shp] * 3,
        compiler_params=_params(1),
    )(w, g, m, v)


def _adamw_halves(w, own, got, m, v, c_idx):
    rows, cols = w.shape
    tr = _row_tile(rows // 2)
    per_half = rows // 2 // tr

    def body(c_ref, w_ref, own_ref, got_ref, m_ref, v_ref, d_ref, mo_ref, vo_ref, g_ref):
        mine = (pl.program_id(0) // per_half) == c_ref[0]
        gv = jnp.where(mine, own_ref[...], got_ref[...])
        g_ref[...] = gv
        d_ref[...], mo_ref[...], vo_ref[...] = _adamw_math(w_ref[...], gv, m_ref[...], v_ref[...])

    blk = pl.BlockSpec((tr, cols), lambda i, c: (i, 0))
    own_blk = pl.BlockSpec((tr, cols), lambda i, c: (jnp.where(i // per_half == c[0], i % per_half, 0), 0))
    got_blk = pl.BlockSpec((tr, cols), lambda i, c: (jnp.where(i // per_half == c[0], 0, i % per_half), 0))
    shp = jax.ShapeDtypeStruct((rows, cols), F32)
    return pl.pallas_call(
        body, name="adamw_halves",
        grid_spec=pltpu.PrefetchScalarGridSpec(num_scalar_prefetch=1, grid=(rows // tr,),
                                               in_specs=[blk, own_blk, got_blk, blk, blk], out_specs=[blk] * 4),
        out_shape=[shp] * 4, compiler_params=_params(1),
    )(c_idx, w, own, got, m, v)


def _local_step(x, target, norm_gains, q_g, k_g, ng, weights_of, grads_done):
    s = x.shape[0]
    tm = min(512, s)
    tq = min(256, s)
    tf = min(1024, s)
    g1, gm, g2, gf = norm_gains
    cos2, sin2 = _rope_tables(s)
    gq8 = jnp.tile(q_g, (1, 8))
    gk2 = jnp.tile(k_g, (1, 2))

    tn = min(256, s)
    tk = min(1024, s)
    w1 = weights_of(1, ())
    x1, s1, t1, b1, h1 = _ffn_fwd(x, g1, w1["g1"], w1["u1"], w1["d1"], tf)
    w2 = weights_of(2, (x1,))
    lbl = w2["lbl"]
    pqkv, ph, pg, hm = _mix_in_fwd(x1, gm, w2["in"], tn)
    qe, kr, vr, vs = _qk_prep(pqkv, gq8, gk2, cos2, sin2, tm)
    oa, lse = _attn_fwd(qe, kr, vr, vs, tq)
    ob, pre, hstates = _hgrn_fwd(ph, lbl, ng)
    x2 = _mix_out_fwd(x1, oa, ob, pg, w2["a"], w2["b"], w2["o"], tm)
    w3 = weights_of(3, (x2,))
    x3, s2, t2, b2, h2 = _ffn_fwd(x2, g2, w3["g2"], w3["u2"], w3["d2"], tf)
    dx3, loss, dgf = _loss_head(x3, gf, target, tm)

    dx2, da2, db2, f2, dg2, dx3c = _ffn_bwd(dx3, x2, g2, s2, t2, b2, w3["g2"], w3["u2"], w3["d2"], tm)
    tok = grads_done(3, dict(g2=_dw_shared_b("dw_gate", da2, h2, tk, 1.0), u2=_dw_shared_b("dw_gate", db2, h2, tk, 1.0),
                             d2=_dw_shared_b("dw_down", f2, dx3c, tk, 1.0)))

    dpg, mg, dya, dyb, doe, delta, dob = _mix_out_bwd(dx2, oa, ob, pg, w2["a"], w2["b"], w2["o"], tm, tok)
    g_o = [g.reshape(N_SHARD, D_MODEL // N_SHARD, D_MODEL) for g in _dw_colblocks("dw_out", mg, dx2, 1, tk)]
    g_a = _dw_colblocks("dw_branch", oa, dya, N_SHARD, tk)
    g_b = _dw_colblocks("dw_branch", ob, dyb, N_SHARD, tk)
    dqt, dkt, dvt = _attn_bwd(qe, kr, kr.T, vr, doe, delta, lse, tq)
    dqkv, dgq, dgk = _qk_prep_bwd(pqkv, dqt.T, dkt.T, dvt.T, gq8, gk2, cos2, sin2, tm)
    dhq, dhff, dhfb, dhi, dhg, dlb, dng = _hgrn_bwd(ph, pre, dob, hstates, lbl, ng)
    dps = (dqkv, dhq, dhff, dhfb, dhi, dhg, dpg)
    g_in = [g.reshape(N_SHARD, -1, D_MODEL) for g in _dw_in(dps, hm, tk)]
    tok = grads_done(2, {"in": g_in, "a": g_a, "b": g_b, "o": g_o})
    dx1, dgm = _mix_in_bwd(dps, w2["in"], x1, dx2, gm, tn, tok)

    dx0, da1, db1, f1, dg1, dx1c = _ffn_bwd(dx1, x, g1, s1, t1, b1, w1["g1"], w1["u1"], w1["d1"], tm)
    grads_done(1, dict(g1=_dw_shared_b("dw_gate", da1, h1, tk, 1.0), u1=_dw_shared_b("dw_gate", db1, h1, tk, 1.0),
                       d1=_dw_shared_b("dw_down", f1, dx1c, tk, 1.0)))
    small = dict(g1=dg1, gm=dgm, g2=dg2, gf=dgf, gq=dgq, gk=dgk, lb=dlb, ng=dng)
    return loss, dx0, small, lbl


GROUPS = {1: ("g1", "u1", "d1"), 2: ("in", "a", "b", "o"), 3: ("g2", "u2", "d2")}
BIG = GROUPS[1] + GROUPS[2] + GROUPS[3]
TRANSPOSED = ("g1", "u1", "in", "g2", "u2")


def _pack_rows(vectors, width):
    rows = []
    for vct in vectors:
        flat = vct.reshape(-1)
        pad = (-flat.shape[0]) % width
        rows.append(jnp.pad(flat, (0, pad)).reshape(-1, width))
    return jnp.concatenate(rows, axis=0)


def kernel(x, ffn1_norm_g, ffn1_w_gate, ffn1_w_up, ffn1_w_down, mix_norm_g, w_in, q_norm_g, k_norm_g, hgrn_lb_logits, hgrn_out_norm_g, w_branch_attn, w_branch_hgrn, w_out, ffn2_norm_g, ffn2_w_gate, ffn2_w_up, ffn2_w_down, final_norm_g, loss_target, m_ffn1_norm_g, m_ffn1_w_gate, m_ffn1_w_up, m_ffn1_w_down, m_mix_norm_g, m_w_in, m_q_norm_g, m_k_norm_g, m_hgrn_lb_logits, m_hgrn_out_norm_g, m_w_branch_attn, m_w_branch_hgrn, m_w_out, m_ffn2_norm_g, m_ffn2_w_gate, m_ffn2_w_up, m_ffn2_w_down, m_final_norm_g, v_ffn1_norm_g, v_ffn1_w_gate, v_ffn1_w_up, v_ffn1_w_down, v_mix_norm_g, v_w_in, v_q_norm_g, v_k_norm_g, v_hgrn_lb_logits, v_hgrn_out_norm_g, v_w_branch_attn, v_w_branch_hgrn, v_w_out, v_ffn2_norm_g, v_ffn2_w_gate, v_ffn2_w_up, v_ffn2_w_down, v_final_norm_g):
    xi, yi, ci = lax.axis_index("x"), lax.axis_index("y"), lax.axis_index("c")
    me = 2 * xi + yi
    c_idx = jnp.reshape(ci, (1,)).astype(jnp.int32)
    me_idx = jnp.reshape(me, (1,)).astype(jnp.int32)

    big_w = dict(g1=ffn1_w_gate[0], u1=ffn1_w_up[0], d1=ffn1_w_down[0], a=w_branch_attn[0], b=w_branch_hgrn[0],
                 o=w_out[0], g2=ffn2_w_gate[0], u2=ffn2_w_up[0], d2=ffn2_w_down[0])
    big_w["in"] = w_in[0]
    big_m = dict(g1=m_ffn1_w_gate[0], u1=m_ffn1_w_up[0], d1=m_ffn1_w_down[0], a=m_w_branch_attn[0], b=m_w_branch_hgrn[0],
                 o=m_w_out[0], g2=m_ffn2_w_gate[0], u2=m_ffn2_w_up[0], d2=m_ffn2_w_down[0])
    big_m["in"] = m_w_in[0]
    big_v = dict(g1=v_ffn1_w_gate[0], u1=v_ffn1_w_up[0], d1=v_ffn1_w_down[0], a=v_w_branch_attn[0], b=v_w_branch_hgrn[0],
                 o=v_w_out[0], g2=v_ffn2_w_gate[0], u2=v_ffn2_w_up[0], d2=v_ffn2_w_down[0])
    big_v["in"] = v_w_in[0]
    for table in (big_w, big_m, big_v):
        for n in TRANSPOSED:
            table[n] = table[n].T

    started, token = {}, ()
    for grp in (1, 2, 3):
        shards = [big_w[n] for n in GROUPS[grp]] + ([hgrn_lb_logits.reshape(4, LANES)] if grp == 2 else [])
        dtypes = [CD] * len(GROUPS[grp]) + ([F32] if grp == 2 else [])
        lands = _cast_into_slots(shards, dtypes, me_idx)
        send, recv, _, lands, tok = _exchange_start("gather%d_start" % grp, [], lands, token)
        started[grp], token = (send, recv, lands), (tok,)

    def weights_of(grp, after):
        send, recv, lands = started[grp]
        got = _exchange_wait("gather%d_wait" % grp, send, recv, [], lands, tuple(after) + (token if grp == 1 else ()))
        by_halves = [i for i, land in enumerate(got) if _halved(land)]
        for i, whole in zip(by_halves, _pair_fill("gather%d_fill" % grp, [got[i] for i in by_halves])):
            got[i] = whole
        w = dict(zip(GROUPS[grp], got))
        if grp == 2:
            w["in"] = w["in"].reshape(-1, D_MODEL)
            w["o"] = w["o"].reshape(D_MODEL, D_MODEL)
            w["lbl"] = jnp.transpose(got[-1], (1, 0, 2)).reshape(4, N_SHARD * LANES)
        return w

    pending = {}

    def grads_done(grp, grads):
        names = list(grads)
        got = _pair_exchange([grads[n][1] for n in names])
        res = _pair_sum([grads[n][0] for n in names], got, c_idx, me_idx)
        sums, owns = res[:len(names)], res[len(names):]
        lands = [lax.empty(s_.shape, s_.dtype) for s_ in sums]
        send, recv, srcs, lands, tok = _exchange_start("reduce%d_start" % grp, list(sums), lands, ())
        pending[grp] = (names, send, recv, srcs, lands, owns, tok)
        return (tok,)

    def reduced_halves(grp, after):
        names, send, recv, srcs, lands, owns, _ = pending[grp]
        parts = _exchange_wait("reduce%d_wait" % grp, send, recv, srcs, lands, after)
        return names, list(_chip_sum(list(owns), parts, me_idx))

    loss, dx, small, lbl = _local_step(
        x[0], loss_target[0], (ffn1_norm_g, mix_norm_g, ffn2_norm_g, final_norm_g.reshape(1, -1)),
        q_norm_g, k_norm_g, hgrn_out_norm_g, weights_of, grads_done)

    dgq = small["gq"].reshape(8, HEAD_DIM).sum(axis=0)
    dgk = small["gk"].reshape(2, HEAD_DIM).sum(axis=0)
    lb_full = _hgrn_lower_bounds(lbl)
    dlog = []
    for d in (0, 1):
        t = small["lb"][d:d + 1] * lb_full[d] * (1.0 - lb_full[d])
        dlog += [t, -t]
    small_list = [small["g1"], small["gm"], small["g2"], small["gf"], small["ng"], dgq, dgk, jnp.concatenate(dlog, axis=0), loss[0, 0]]
    packed = _pack_rows(small_list, D_MODEL)
    n_rows = packed.shape[0]
    packed = jnp.pad(packed, ((0, (-n_rows) % 8), (0, 0)))
    red = _small_allreduce(packed)
    loss_out = red[n_rows - 1, 0]
    sg = dict(g1=red[0:1], gm=red[1:2], g2=red[2:3], gf=red[3], ng=red[4:5, :512], gq=red[5:6, :HEAD_DIM],
              gk=red[6:7, :HEAD_DIM])
    dlog_full = red[7:9].reshape(2, 2, 512)
    sg["lb"] = lax.dynamic_slice_in_dim(dlog_full, me * LANES, LANES, axis=2)

    small_w = dict(g1=ffn1_norm_g, gm=mix_norm_g, g2=ffn2_norm_g, gf=final_norm_g, ng=hgrn_out_norm_g, gq=q_norm_g,
                   gk=k_norm_g, lb=hgrn_lb_logits)
    small_m = dict(g1=m_ffn1_norm_g, gm=m_mix_norm_g, g2=m_ffn2_norm_g, gf=m_final_norm_g, ng=m_hgrn_out_norm_g,
                   gq=m_q_norm_g, gk=m_k_norm_g, lb=m_hgrn_lb_logits)
    small_v = dict(g1=v_ffn1_norm_g, gm=v_mix_norm_g, g2=v_ffn2_norm_g, gf=v_final_norm_g, ng=v_hgrn_out_norm_g,
                   gq=v_q_norm_g, gk=v_k_norm_g, lb=v_hgrn_lb_logits)
    small_names = ("g1", "gm", "g2", "gf", "ng", "gq", "gk", "lb")
    pack = lambda dct: _pack_rows([dct[n] for n in small_names], D_MODEL)
    pw, pgr, pm, pv = pack(small_w), pack(sg), pack(small_m), pack(small_v)
    pad8 = lambda a: jnp.pad(a, ((0, (-a.shape[0]) % 8), (0, 0)))
    sd, sm_, sv_ = _adamw(pad8(pw), pad8(pgr), pad8(pm), pad8(pv))

    def unpack(buf):
        out, r = {}, 0
        for n in small_names:
            size = small_w[n].size
            nr = -(-size // D_MODEL)
            out[n] = buf[r:r + nr].reshape(-1)[:size].reshape(small_w[n].shape)
            r += nr
        return out

    sdelta, snew_m, snew_v = unpack(sd), unpack(sm_), unpack(sv_)
    sgrad = {n: sg[n].reshape(small_w[n].shape) for n in small_names}

    bdelta, bnew_m, bnew_v, bgrad = {}, {}, {}, {}

    def update(names, halves):
        for n, own, got in zip(names, halves, _pair_share(halves)):
            res = _adamw_halves(big_w[n], own, got, big_m[n], big_v[n], c_idx)
            if n in TRANSPOSED:
                res = [r.T for r in res]
            bdelta[n], bnew_m[n], bnew_v[n], bgrad[n] = [r[None] for r in res]

    names3, halves3 = reduced_halves(3, (pending[1][-1],))
    names2, halves2 = reduced_halves(2, (halves3[0],))
    update(names3 + names2, halves3 + halves2)
    names1, halves1 = reduced_halves(1, (bdelta[names2[-1]],))
    update(names1, halves1)

    order = [("s", "g1"), ("b", "g1"), ("b", "u1"), ("b", "d1"), ("s", "gm"), ("b", "in"), ("s", "gq"), ("s", "gk"),
             ("s", "lb"), ("s", "ng"), ("b", "a"), ("b", "b"), ("b", "o"), ("s", "g2"), ("b", "g2"), ("b", "u2"),
             ("b", "d2"), ("s", "gf")]
    outs = [loss_out, dx[None]]
    for table_s, table_b in ((sgrad, bgrad), (sdelta, bdelta), (snew_m, bnew_m), (snew_v, bnew_v)):
        outs += [(table_s if kind == "s" else table_b)[n] for kind, n in order]
    return tuple(outs)
```

```python
import functools

import numpy as np
import jax
import jax.numpy as jnp
from jax import lax
from jax.experimental import pallas as pl
from jax.experimental.pallas import tpu as pltpu

F32 = jnp.float32
BF16 = jnp.bfloat16
CD = jnp.bfloat16

EPS = 1e-6
D_MODEL = 1024
HEAD_DIM = 64
GRID_W = 64
ROPE_THETA = 10000.0
CHUNK = 32
N_SHARD = 4
N_DEV = 8
VMEM_LIMIT = 56 * 1024 * 1024
HGRN_BWD_VMEM = 60 * 1024 * 1024
LANES = 128
HG_TILE = 256
ATTN_BWD_HEADS = 2
FFN_ROWS = 256

ADAM_LR = 0.001
ADAM_B1 = 0.9
ADAM_B2 = 0.999
ADAM_EPS = 1e-08
ADAM_WD = 0.01
ADAM_STEP = 10

NN = (((1,), (0,)), ((), ()))
NT = (((1,), (1,)), ((), ()))
TN = (((0,), (0,)), ((), ()))
MESH = pl.DeviceIdType.MESH
ANY = pl.BlockSpec(memory_space=pl.ANY)


def _mm(a, b, dn):
    return lax.dot_general(a.astype(CD), b.astype(CD), dn, preferred_element_type=F32)


def _split3(x):
    hi = x.astype(BF16)
    r = x - hi.astype(F32)
    mid = r.astype(BF16)
    lo = (r - mid.astype(F32)).astype(BF16)
    return hi, mid, lo


def _xdot(x, m):
    rows = x.shape[0]
    hi, mid, _ = _split3(x)
    r = lax.dot_general(jnp.concatenate([hi, mid], axis=0), m, NN, preferred_element_type=F32)
    return r[:rows] + r[rows:]


def _xdot_l(m, x):
    cols = x.shape[1]
    hi, mid, _ = _split3(x)
    r = lax.dot_general(m, jnp.concatenate([hi, mid], axis=1), NN, preferred_element_type=F32)
    return r[:, :cols] + r[:, cols:]


def _params(n_grid, vmem_limit=VMEM_LIMIT):
    return pltpu.CompilerParams(dimension_semantics=("arbitrary",) * n_grid, vmem_limit_bytes=vmem_limit)


def _sigmoid(x):
    return jax.nn.sigmoid(x)


def _np_blocksum(n):
    i = np.arange(n)
    return (i[:, None] // HEAD_DIM == i[None, :] // HEAD_DIM).astype(np.float32)


def _np_swap32(n):
    i = np.arange(n)
    partner = np.where(i % HEAD_DIM < HEAD_DIM // 2, i + HEAD_DIM // 2, i - HEAD_DIM // 2)
    m = np.zeros((n, n), np.float32)
    m[i, partner] = 1.0
    return m


def _np_expand_q():
    m = np.zeros((512, 1024), np.float32)
    for h in range(8):
        g = h // 4
        for d in range(HEAD_DIM):
            m[64 * h + d, 128 * h + 64 * g + d] = 1.0
    return m


def _np_headsum_spread():
    m = np.zeros((512, 1024), np.float32)
    for h in range(8):
        m[64 * h:64 * h + 64, 128 * h:128 * h + 128] = 1.0
    return m


def _np_swap_halves():
    m = np.zeros((128, 128), np.float32)
    i = np.arange(128)
    m[i, (i + 64) % 128] = 1.0
    return m


def _np_hgrn_cums(t, rev):
    r = np.arange(t)[:, None]
    c = np.arange(t)[None, :]
    same = (r // CHUNK) == (c // CHUNK)
    if not rev:
        cum = same & (c <= r)
        mid = same & (c % CHUNK <= CHUNK // 2 - 1)
    else:
        cum = same & (c >= r)
        mid = same & (c % CHUNK >= CHUNK // 2)
    return np.concatenate([cum, mid, same], axis=0).astype(np.float32)


def _bf(a):
    return jnp.asarray(a, dtype=BF16)


def _rope_tables(seq_len):
    rows = seq_len // GRID_W
    row = jnp.repeat(jnp.arange(rows, dtype=F32), GRID_W)
    col = jnp.tile(jnp.arange(GRID_W, dtype=F32), rows)
    n_freq = HEAD_DIM // 4
    inv = ROPE_THETA ** (-jnp.arange(n_freq, dtype=F32) / n_freq)
    ang = jnp.concatenate([row[:, None] * inv, col[:, None] * inv], axis=-1)
    cos, sin = jnp.cos(ang), jnp.sin(ang)
    c64 = jnp.concatenate([cos, cos], axis=-1)
    s64 = jnp.concatenate([-sin, sin], axis=-1)
    return jnp.tile(c64, (1, 2)), jnp.tile(s64, (1, 2))


def _ffn_fwd(x, g, wg, wu, wd, tm):
    s, d = x.shape
    nsh, fs, _ = wg.shape

    def body(x_ref, g_ref, wg_ref, wu_ref, wd_ref, xo_ref, a_ref, da_ref, b_ref, hb_ref, acc, hs):
        j = pl.program_id(1)

        @pl.when(j == 0)
        def _():
            xv = x_ref[...]
            r = lax.rsqrt(jnp.mean(xv * xv, axis=-1, keepdims=True) + EPS)
            h = (xv * r * g_ref[...]).astype(CD)
            hs[...] = h
            hb_ref[...] = h
            acc[...] = jnp.zeros_like(acc)

        blocks = [slice(r0, min(r0 + FFN_ROWS, tm)) for r0 in range(0, tm, FFN_ROWS)]
        firsts = [(_mm(hs[rows, :], wg_ref[0], NT), _mm(hs[rows, :], wu_ref[0], NT)) for rows in blocks]
        for rows, (a, b) in zip(blocks, firsts):
            sg = _sigmoid(a)
            silu = a * sg
            acc[rows, :] += _mm(silu * b, wd_ref[0], NN)
            a_ref[0, rows, :] = silu.astype(CD)
            da_ref[0, rows, :] = (sg * (1.0 + a * (1.0 - sg))).astype(CD)
            b_ref[0, rows, :] = b.astype(CD)

        @pl.when(j == nsh - 1)
        def _():
            xo_ref[...] = x_ref[...] + 0.5 * acc[...]

    return pl.pallas_call(
        body, name="ffn_fwd", grid=(s // tm, nsh),
        in_specs=[pl.BlockSpec((tm, d), lambda i, j: (i, 0)), pl.BlockSpec((1, d), lambda i, j: (0, 0))]
        + [pl.BlockSpec((1, fs, d), lambda i, j: (j, 0, 0))] * 3,
        out_specs=[pl.BlockSpec((tm, d), lambda i, j: (i, 0))] + [pl.BlockSpec((1, tm, fs), lambda i, j: (j, i, 0))] * 3
        + [pl.BlockSpec((tm, d), lambda i, j: (i, 0))],
        out_shape=[jax.ShapeDtypeStruct((s, d), F32)] + [jax.ShapeDtypeStruct((nsh, s, fs), CD)] * 3
        + [jax.ShapeDtypeStruct((s, d), CD)],
        scratch_shapes=[pltpu.VMEM((tm, d), F32), pltpu.VMEM((tm, d), CD)],
        compiler_params=_params(2),
    )(x, g, wg, wu, wd)


def _ffn_bwd(dout, x, g, silu, dsilu, b, wg, wu, wd, tm):
    s, d = x.shape
    nsh, fs, _ = wg.shape

    def body(do_ref, x_ref, g_ref, sl_ref, ds_ref, b_ref, wg_ref, wu_ref, wd_ref,
             dx_ref, da_ref, db_ref, f_ref, dg_ref, do16_ref, dh):
        i = pl.program_id(0)
        j = pl.program_id(1)

        @pl.when(j == 0)
        def _():
            dh[...] = jnp.zeros_like(dh)
            do16_ref[...] = (0.5 * do_ref[...]).astype(CD)

        @pl.when((i == 0) & (j == 0))
        def _():
            dg_ref[...] = jnp.zeros_like(dg_ref)

        blocks = [slice(r0, min(r0 + FFN_ROWS, tm)) for r0 in range(0, tm, FFN_ROWS)]
        dfs = [_mm(do16_ref[rows, :], wd_ref[0], NT) for rows in blocks]
        das, dbs = [], []
        for rows, df in zip(blocks, dfs):
            sl = sl_ref[0, rows, :].astype(F32)
            bv = b_ref[0, rows, :].astype(F32)
            da = (df * bv * ds_ref[0, rows, :].astype(F32)).astype(CD)
            db = (df * sl).astype(CD)
            da_ref[0, rows, :] = da
            db_ref[0, rows, :] = db
            f_ref[0, rows, :] = (sl * bv).astype(CD)
            das.append(da)
            dbs.append(db)
        for rows, da, db in zip(blocks, das, dbs):
            dh[rows, :] += _mm(da, wg_ref[0], NN) + _mm(db, wu_ref[0], NN)

        @pl.when(j == nsh - 1)
        def _():
            xv = x_ref[...]
            r = lax.rsqrt(jnp.mean(xv * xv, axis=-1, keepdims=True) + EPS)
            dhv = dh[...]
            u = dhv * g_ref[...]
            dx_ref[...] = do_ref[...] + r * u - xv * (r * r * r) * jnp.mean(u * xv, axis=-1, keepdims=True)
            dg_ref[...] += jnp.sum(dhv * xv * r, axis=0, keepdims=True)

    act = pl.BlockSpec((1, tm, fs), lambda i, j: (j, i, 0))
    row = pl.BlockSpec((tm, d), lambda i, j: (i, 0))
    return pl.pallas_call(
        body, name="ffn_bwd", grid=(s // tm, nsh),
        in_specs=[row, row, pl.BlockSpec((1, d), lambda i, j: (0, 0)), act, act, act]
        + [pl.BlockSpec((1, fs, d), lambda i, j: (j, 0, 0))] * 3,
        out_specs=[row, act, act, act, pl.BlockSpec((1, d), lambda i, j: (0, 0)), row],
        out_shape=[jax.ShapeDtypeStruct((s, d), F32), jax.ShapeDtypeStruct((nsh, s, fs), CD),
                   jax.ShapeDtypeStruct((nsh, s, fs), CD), jax.ShapeDtypeStruct((nsh, s, fs), CD),
                   jax.ShapeDtypeStruct((1, d), F32), jax.ShapeDtypeStruct((s, d), CD)],
        scratch_shapes=[pltpu.VMEM((tm, d), F32)],
        compiler_params=_params(2),
    )(dout, x, g, silu, dsilu, b, wg, wu, wd)


def _tn_call(name, operands, in_specs, out_shape, out_spec, grid, acc_shape, pick, scale=1.0):
    nk = grid[-1]
    n_in = len(operands)

    def body(*refs):
        out_ref, out16_ref, acc = refs[n_in], refs[n_in + 1], refs[n_in + 2]
        k = pl.program_id(len(grid) - 1)

        @pl.when(k == 0)
        def _():
            acc[...] = jnp.zeros_like(acc)

        pick(refs[:n_in], acc)

        @pl.when(k == nk - 1)
        def _():
            res = (acc[...] if scale == 1.0 else acc[...] * scale).reshape(out_ref.shape)
            out_ref[...] = res
            out16_ref[...] = res.astype(CD)

    return pl.pallas_call(
        body, name=name, grid=grid, in_specs=in_specs, out_specs=[out_spec, out_spec],
        out_shape=[out_shape, jax.ShapeDtypeStruct(out_shape.shape, CD)],
        scratch_shapes=[pltpu.VMEM(acc_shape, F32)], compiler_params=_params(len(grid)),
    )(*operands)


def _dw_shared_b(name, a3, b, tk, scale):
    nj, s, m = a3.shape
    n = b.shape[1]

    def pick(refs, acc):
        rows = pl.ds(pl.multiple_of(pl.program_id(1) * tk, tk), tk)
        acc[...] += _mm(refs[0][0], refs[1][rows, :], TN)

    return _tn_call(name, (a3, b),
                    [pl.BlockSpec((1, tk, m), lambda j, k: (j, k, 0)), pl.BlockSpec((s, n), lambda j, k: (0, 0))],
                    jax.ShapeDtypeStruct((nj, m, n), F32), pl.BlockSpec((1, m, n), lambda j, k: (j, 0, 0)),
                    (nj, s // tk), (m, n), pick, scale)


def _dw_colblocks(name, a, b, nj, tk):
    s, m = a.shape
    n = b.shape[1] // nj
    nk = s // tk

    def body(a_ref, b_ref, out_ref, out16_ref, acc):
        k = pl.program_id(0)

        @pl.when(k == 0)
        def _():
            acc[...] = jnp.zeros_like(acc)

        acc[...] += _mm(a_ref[...], b_ref[...], TN)

        @pl.when(k == nk - 1)
        def _():
            for j in range(nj):
                res = acc[:, j * n:(j + 1) * n]
                out_ref[j] = res
                out16_ref[j] = res.astype(CD)

    whole = pl.BlockSpec((nj, m, n), lambda k: (0, 0, 0))
    return pl.pallas_call(
        body, name=name, grid=(nk,),
        in_specs=[pl.BlockSpec((tk, m), lambda k: (k, 0)), pl.BlockSpec((tk, nj * n), lambda k: (k, 0))],
        out_specs=[whole, whole],
        out_shape=[jax.ShapeDtypeStruct((nj, m, n), F32), jax.ShapeDtypeStruct((nj, m, n), CD)],
        scratch_shapes=[pltpu.VMEM((m, nj * n), F32)], compiler_params=_params(1),
    )(a, b)


DP_WIDTHS = (768, 512, 512, 512, 512, 512, 2048)
DW_IN_COLS = 512


def _dw_in(dps, hb, tk):
    s, d = hb.shape
    nk = s // tk
    blocks, row = [], 0
    for p, width in enumerate(DP_WIDTHS):
        step = width if width <= 768 else DW_IN_COLS
        for c0 in range(0, width, step):
            blocks.append((p, c0, step, row))
            row += step
    nb, max_w = len(blocks), max(b[2] for b in blocks)
    first = [min(i for i, b in enumerate(blocks) if b[0] == p) for p in range(len(DP_WIDTHS))]
    count = [sum(1 for b in blocks if b[0] == p) for p in range(len(DP_WIDTHS))]

    def body(*refs):
        dp_refs, hb_ref, out_ref, out16_ref, acc, acc16, sems = refs[:7], refs[7], refs[8], refs[9], refs[10], refs[11], refs[12]
        b, k = pl.program_id(0), pl.program_id(1)
        rows = pl.ds(pl.multiple_of(k * tk, tk), tk)

        def writes(i):
            _, _, w, r0 = blocks[i]
            slot = i % 2
            return (pltpu.make_async_copy(acc.at[slot, 0:w], out_ref.at[r0:r0 + w], sems.at[slot, 0]),
                    pltpu.make_async_copy(acc16.at[slot, 0:w], out16_ref.at[r0:r0 + w], sems.at[slot, 1]))

        for i, (p, _, w, _) in enumerate(blocks):
            @pl.when(b == i)
            def _(i=i, p=p, w=w):
                slot = i % 2
                prod = _mm(dp_refs[p][...], hb_ref[rows, :], TN)

                @pl.when(k == 0)
                def _():
                    acc[slot, 0:w] = prod

                @pl.when(k > 0)
                def _():
                    acc[slot, 0:w] += prod

                @pl.when(k == nk - 1)
                def _():
                    if i >= 1:
                        for cp in writes(i - 1):
                            cp.wait()
                    acc16[slot, 0:w] = acc[slot, 0:w].astype(CD)
                    for cp in writes(i):
                        cp.start()
                    if i == nb - 1:
                        for cp in writes(i):
                            cp.wait()

    def piece_spec(p):
        width = DP_WIDTHS[p]
        cols = width if width <= 768 else DW_IN_COLS

        def imap(b, k):
            active = (b >= first[p]) & (b < first[p] + count[p])
            return (jnp.where(active, k, jnp.where(b < first[p], 0, nk - 1)), jnp.clip(b - first[p], 0, count[p] - 1))

        return pl.BlockSpec((tk, cols), imap)

    return pl.pallas_call(
        body, name="dw_in", grid=(nb, nk),
        in_specs=[piece_spec(p) for p in range(len(DP_WIDTHS))] + [pl.BlockSpec((s, d), lambda b, k: (0, 0))],
        out_specs=[ANY, ANY],
        out_shape=[jax.ShapeDtypeStruct((sum(DP_WIDTHS), d), F32), jax.ShapeDtypeStruct((sum(DP_WIDTHS), d), CD)],
        scratch_shapes=[pltpu.VMEM((2, max_w, d), F32), pltpu.VMEM((2, max_w, d), CD), pltpu.SemaphoreType.DMA((2, 2))],
        compiler_params=_params(2),
    )(*dps, hb)


def _mix_in_fwd(x, g, w_t, tm):
    s, d = x.shape
    n_in = w_t.shape[0]

    def body(x_ref, g_ref, w_ref, qkv_ref, hg_ref, gt_ref, hb_ref):
        xv = x_ref[...]
        r = lax.rsqrt(jnp.mean(xv * xv, axis=-1, keepdims=True) + EPS)
        h = (xv * r * g_ref[...]).astype(CD)
        hb_ref[...] = h
        off = DP_WIDTHS[0]
        qkv_ref[...] = _mm(h, w_ref[0:off, :], NT)
        for c, width in enumerate(DP_WIDTHS[1:6]):
            hg_ref[:, c * width:(c + 1) * width] = _mm(h, w_ref[off:off + width, :], NT)
            off += width
        gate = DP_WIDTHS[6] // 2
        for c in range(2):
            gt_ref[:, c * gate:(c + 1) * gate] = _mm(h, w_ref[off:off + gate, :], NT).astype(CD)
            off += gate

    row = lambda w: pl.BlockSpec((tm, w), lambda i: (i, 0))
    return pl.pallas_call(
        body, name="mix_in_fwd", grid=(s // tm,),
        in_specs=[row(d), pl.BlockSpec((1, d), lambda i: (0, 0)), pl.BlockSpec((n_in, d), lambda i: (0, 0))],
        out_specs=[row(768), row(2560), row(2048), row(d)],
        out_shape=[jax.ShapeDtypeStruct((s, 768), F32), jax.ShapeDtypeStruct((s, 2560), F32),
                   jax.ShapeDtypeStruct((s, 2048), CD), jax.ShapeDtypeStruct((s, d), CD)],
        compiler_params=_params(1),
    )(x, g, w_t)


def _mix_in_bwd(dps, w_t, x, dres, g, tm, after=()):
    s, d = x.shape
    n_in = w_t.shape[0]

    def body(*refs):
        refs = refs[len(after):]
        dp_refs = refs[:7]
        w_ref, x_ref, dr_ref, g_ref, dx_ref, dg_ref = refs[7:]

        @pl.when(pl.program_id(0) == 0)
        def _():
            dg_ref[...] = jnp.zeros_like(dg_ref)

        dhv = jnp.zeros((tm, d), F32)
        off = 0
        for ref, width in zip(dp_refs, DP_WIDTHS):
            dhv = dhv + _mm(ref[...], w_ref[off:off + width, :], NN)
            off += width
        xv = x_ref[...]
        r = lax.rsqrt(jnp.mean(xv * xv, axis=-1, keepdims=True) + EPS)
        u = dhv * g_ref[...]
        dx_ref[...] = dr_ref[...] + r * u - xv * (r * r * r) * jnp.mean(u * xv, axis=-1, keepdims=True)
        dg_ref[...] += jnp.sum(dhv * xv * r, axis=0, keepdims=True)

    row = pl.BlockSpec((tm, d), lambda i: (i, 0))
    vec = pl.BlockSpec((1, d), lambda i: (0, 0))
    return pl.pallas_call(
        body, name="mix_in_bwd", grid=(s // tm,),
        in_specs=[ANY] * len(after) + [pl.BlockSpec((tm, w), lambda i: (i, 0)) for w in DP_WIDTHS]
        + [pl.BlockSpec((n_in, d), lambda i: (0, 0)), row, row, vec],
        out_specs=[row, vec],
        out_shape=[jax.ShapeDtypeStruct((s, d), F32), jax.ShapeDtypeStruct((1, d), F32)],
        compiler_params=_params(1),
    )(*after, *dps, w_t, x, dres, g)


def _headnorm_rope(x, gain, cos, sin, blocksum, swap):
    ss = _xdot(x * x, blocksum)
    r = lax.rsqrt(ss * (1.0 / HEAD_DIM) + EPS)
    y = x * r * gain
    return y * cos + _xdot(y, swap) * sin, r


def _headnorm_rope_bwd(dz, x, gain, cos, sin, blocksum, swap):
    ss = _xdot(x * x, blocksum)
    r = lax.rsqrt(ss * (1.0 / HEAD_DIM) + EPS)
    dy = dz * cos + _xdot(dz * sin, swap)
    u = dy * gain
    mean_ux = _xdot(u * x, blocksum) * (1.0 / HEAD_DIM)
    dx = r * u - x * (r * r * r) * mean_ux
    return dx, jnp.sum(dy * x * r, axis=0, keepdims=True)


def _qk_prep(pqkv, gq, gk, cos2, sin2, tm):
    s = pqkv.shape[0]
    bs512, sw512, eq, swh = _bf(_np_blocksum(512)), _bf(_np_swap32(512)), _bf(_np_expand_q()), _bf(_np_swap_halves())

    def body(q_ref, kv_ref, gq_ref, gk_ref, c_ref, s_ref, bs_ref, sw_ref, eq_ref, swh_ref, qe_ref, k_ref, v_ref, vs_ref):
        c2, s2 = c_ref[...], s_ref[...]
        c8, s8 = jnp.tile(c2, (1, 4)), jnp.tile(s2, (1, 4))
        bs, sw = bs_ref[...], sw_ref[...]
        zq, _ = _headnorm_rope(q_ref[...], gq_ref[...], c8, s8, bs, sw)
        qe_ref[...] = _mm(zq * (HEAD_DIM ** -0.5), eq_ref[...], NN).astype(CD)
        kv = kv_ref[...]
        zk, _ = _headnorm_rope(kv[:, :LANES], gk_ref[...], c2, s2, bs[:LANES, :LANES], sw[:LANES, :LANES])
        k_ref[...] = zk.astype(CD)
        v = kv[:, LANES:]
        v_ref[...] = v.astype(CD)
        vs_ref[...] = _mm(v, swh_ref[...], NN).astype(CD)

    full = lambda a: pl.BlockSpec(a.shape, lambda i: (0,) * a.ndim)
    tab = pl.BlockSpec((tm, LANES), lambda i: (i, 0))
    return pl.pallas_call(
        body, name="qk_prep", grid=(s // tm,),
        in_specs=[pl.BlockSpec((tm, 512), lambda i: (i, 0)), pl.BlockSpec((tm, 256), lambda i: (i, 2)),
                  full(gq), full(gk), tab, tab, full(bs512), full(sw512), full(eq), full(swh)],
        out_specs=[pl.BlockSpec((tm, 1024), lambda i: (i, 0)), tab, tab, tab],
        out_shape=[jax.ShapeDtypeStruct((s, 1024), CD)] + [jax.ShapeDtypeStruct((s, LANES), CD)] * 3,
        compiler_params=_params(1),
    )(pqkv, pqkv, gq, gk, cos2, sin2, bs512, sw512, eq, swh)


def _qk_prep_bwd(pqkv, dq, dk, dv, gq, gk, cos2, sin2, tm):
    s = pqkv.shape[0]
    bs512, sw512 = _bf(_np_blocksum(512)), _bf(_np_swap32(512))

    def body(q_ref, kv_ref, dq_ref, dk_ref, dv_ref, gq_ref, gk_ref, c_ref, s_ref, bs_ref, sw_ref,
             dp_ref, dgq_ref, dgk_ref):
        @pl.when(pl.program_id(0) == 0)
        def _():
            dgq_ref[...] = jnp.zeros_like(dgq_ref)
            dgk_ref[...] = jnp.zeros_like(dgk_ref)

        c2, s2 = c_ref[...], s_ref[...]
        c8, s8 = jnp.tile(c2, (1, 4)), jnp.tile(s2, (1, 4))
        bs, sw = bs_ref[...], sw_ref[...]
        dzq = dq_ref[...] * (HEAD_DIM ** -0.5)
        dxq, dgq = _headnorm_rope_bwd(dzq, q_ref[...], gq_ref[...], c8, s8, bs, sw)
        kv = kv_ref[...]
        dxk, dgk = _headnorm_rope_bwd(dk_ref[...], kv[:, :LANES], gk_ref[...], c2, s2, bs[:LANES, :LANES], sw[:LANES, :LANES])
        dp_ref[...] = jnp.concatenate([dxq, dxk, dv_ref[...]], axis=1).astype(CD)
        dgq_ref[...] += dgq
        dgk_ref[...] += dgk

    full = lambda a: pl.BlockSpec(a.shape, lambda i: (0,) * a.ndim)
    tab = pl.BlockSpec((tm, LANES), lambda i: (i, 0))
    return pl.pallas_call(
        body, name="qk_prep_bwd", grid=(s // tm,),
        in_specs=[pl.BlockSpec((tm, 512), lambda i: (i, 0)), pl.BlockSpec((tm, 256), lambda i: (i, 2)),
                  pl.BlockSpec((tm, 512), lambda i: (i, 0)), tab, tab, full(gq), full(gk), tab, tab,
                  full(bs512), full(sw512)],
        out_specs=[pl.BlockSpec((tm, 768), lambda i: (i, 0)), pl.BlockSpec((1, 512), lambda i: (0, 0)),
                   pl.BlockSpec((1, LANES), lambda i: (0, 0))],
        out_shape=[jax.ShapeDtypeStruct((s, 768), CD), jax.ShapeDtypeStruct((1, 512), F32),
                   jax.ShapeDtypeStruct((1, LANES), F32)],
        compiler_params=_params(1),
    )(pqkv, pqkv, dq, dk, dv, gq, gk, cos2, sin2, bs512, sw512)


def _kv_rows(h):
    return pl.ds(pl.multiple_of((h // 4) * HEAD_DIM, HEAD_DIM), HEAD_DIM)


def _attn_fwd(qe, k, v, vs, tq):
    s = k.shape[0]

    def body(q0_ref, q1_ref, q2_ref, q3_ref, k_ref, v_ref, vs_ref, o_ref, lse_ref):
        grp = pl.program_id(0)
        kk = k_ref[...]
        heads = range(4)
        scores = [_mm(q_ref[...], kk, NT) for q_ref in (q0_ref, q1_ref, q2_ref, q3_ref)]
        mxs = [jnp.max(sc, axis=-1, keepdims=True) for sc in scores]
        es = [jnp.exp(scores[r] - mxs[r]) for r in heads]
        ls = [jnp.sum(e, axis=-1, keepdims=True) for e in es]
        for r in heads:
            lse_ref[r] = mxs[r] + jnp.log(ls[r])
        outs = [_mm(es[r], jnp.where(grp != r % 2, vs_ref[...], v_ref[...]), NN) * (1.0 / ls[r]) for r in heads]
        low = lax.broadcasted_iota(jnp.int32, (1, LANES), 1) < HEAD_DIM
        o_ref[...] = jnp.concatenate([jnp.where(low, outs[0], outs[1]), jnp.where(low, outs[2], outs[3])], axis=1)

    kv = pl.BlockSpec((s, LANES), lambda g, i: (0, 0))
    qblk = lambda r: pl.BlockSpec((tq, LANES), lambda g, i: (i, 4 * g + r))
    return pl.pallas_call(
        body, name="attn_fwd", grid=(2, s // tq),
        in_specs=[qblk(0), qblk(1), qblk(2), qblk(3), kv, kv, kv],
        out_specs=[pl.BlockSpec((tq, 2 * LANES), lambda g, i: (i, g)), pl.BlockSpec((4, tq, 1), lambda g, i: (g, i, 0))],
        out_shape=[jax.ShapeDtypeStruct((s, 512), F32), jax.ShapeDtypeStruct((8, s, 1), F32)],
        compiler_params=_params(2),
    )(qe, qe, qe, qe, k, v, vs)


def _attn_bwd(qe, k, kt, v, doe, delta, lse, tq):
    s = k.shape[0]

    nh = ATTN_BWD_HEADS

    def body(*refs):
        q_refs, (k_ref, kt_ref, v_ref) = refs[:nh], refs[nh:nh + 3]
        do_refs, dl_refs = refs[nh + 3:2 * nh + 3], refs[2 * nh + 3:3 * nh + 3]
        lse_ref, dqt_ref, dkt_ref, dvt_ref, qt, dot = refs[3 * nh + 3:]

        @pl.when((pl.program_id(0) == 0) & (pl.program_id(1) == 0))
        def _():
            dkt_ref[...] = jnp.zeros_like(dkt_ref)
            dvt_ref[...] = jnp.zeros_like(dvt_ref)

        rows = _kv_rows(nh * pl.program_id(0))
        kt = kt_ref[rows, :]
        dkt = jnp.zeros((HEAD_DIM, s), F32)
        dvt = jnp.zeros((HEAD_DIM, s), F32)
        firsts = [(_mm(q_ref[...], k_ref[...], NT), _mm(do_ref[...], v_ref[...], NT))
                  for q_ref, do_ref in zip(q_refs, do_refs)]
        for idx, (q_ref, do_ref, dl_ref) in enumerate(zip(q_refs, do_refs, dl_refs)):
            q, do = q_ref[...], do_ref[...]
            sc, dp = firsts[idx]
            p = jnp.exp(sc - lse_ref[idx])
            ds = p * (dp - jnp.max(dl_ref[...], axis=-1, keepdims=True))
            dqt_ref[idx * HEAD_DIM:(idx + 1) * HEAD_DIM, :] = _mm(kt, ds, NT)
            qt[idx] = jnp.transpose(q.astype(F32))
            dot[idx] = jnp.transpose(do.astype(F32))
            dkt = dkt + _mm(qt[idx, rows, :], ds, NN)
            dvt = dvt + _mm(dot[idx, rows, :], p, NN)
        dkt_ref[rows, :] += dkt
        dvt_ref[rows, :] += dvt

    kv = pl.BlockSpec((s, LANES), lambda m, i: (0, 0))
    kvt = pl.BlockSpec((LANES, s), lambda m, i: (0, 0))
    blks = [pl.BlockSpec((tq, LANES), lambda m, i, r=r: (i, nh * m + r)) for r in range(nh)]
    return pl.pallas_call(
        body, name="attn_bwd", grid=(8 // nh, s // tq),
        in_specs=blks + [kv, kvt, kv] + blks + blks + [pl.BlockSpec((nh, tq, 1), lambda m, i: (m, i, 0))],
        out_specs=[pl.BlockSpec((nh * HEAD_DIM, tq), lambda m, i: (m, i)), kvt, kvt],
        out_shape=[jax.ShapeDtypeStruct((8 * HEAD_DIM, s), F32), jax.ShapeDtypeStruct((LANES, s), F32),
                   jax.ShapeDtypeStruct((LANES, s), F32)],
        scratch_shapes=[pltpu.VMEM((nh, LANES, tq), F32), pltpu.VMEM((nh, LANES, tq), F32)],
        compiler_params=_params(2),
    )(*[qe] * nh, k, kt, v, *[doe] * nh, *[delta] * nh, lse)


@jax.custom_vjp
def _mm_nn(a, b):
    return _mm(a, b, NN)


_mm_nn.defvjp(lambda a, b: (_mm(a, b, NN), (a, b)),
              lambda res, g: (_mm(g, res[1], NT), _mm(res[0], g, TN)))


@jax.custom_vjp
def _mm_nt(a, b):
    return _mm(a, b, NT)


_mm_nt.defvjp(lambda a, b: (_mm(a, b, NT), (a, b)),
              lambda res, g: (_mm(g, res[1], NN), _mm(g, res[0], TN)))


@jax.custom_vjp
def _mm_tn(a, b):
    return _mm(a, b, TN)


_mm_tn.defvjp(lambda a, b: (_mm(a, b, TN), (a, b)),
              lambda res, g: (_mm(res[1], g, NT), _mm(res[0], g, NN)))


@jax.custom_vjp
def _cmm(m, mt, x):
    return _xdot_l(m, x)


_cmm.defvjp(lambda m, mt, x: (_xdot_l(m, x), (m, mt)),
            lambda res, g: (jnp.zeros_like(res[0]), jnp.zeros_like(res[1]), _xdot_l(res[1], g)))


def _hgrn_masks(t, rev):
    n_ch = t // CHUNK
    r = jnp.bitwise_and(lax.broadcasted_iota(jnp.int32, (2 * t, t), 0), t - 1)
    c = lax.broadcasted_iota(jnp.int32, (2 * t, t), 1)
    same = jnp.right_shift(r, 5) == jnp.right_shift(c, 5)
    tri2 = same & ((c >= r) if rev else (c <= r))
    pr = lax.broadcasted_iota(jnp.int32, (LANES, LANES), 0)
    pc = lax.broadcasted_iota(jnp.int32, (LANES, LANES), 1)
    diag = jnp.right_shift(pr, 6) == jnp.right_shift(pc, 6)
    qr = lax.broadcasted_iota(jnp.int32, (t, n_ch * LANES), 0)
    qc = lax.broadcasted_iota(jnp.int32, (t, n_ch * LANES), 1)
    rows_chunk = jnp.right_shift(qc, 7) == jnp.right_shift(qr, 5)
    vr = lax.broadcasted_iota(jnp.int32, (n_ch * LANES, t), 0)
    vc = lax.broadcasted_iota(jnp.int32, (n_ch * LANES, t), 1)
    cols_chunk = jnp.right_shift(vr, 7) == jnp.right_shift(vc, 5)
    return dict(tri2=tri2, diag=diag, rows_chunk=rows_chunk, cols_chunk=cols_chunk)


def _hgrn_gates(xf, lb):
    f = lb + (1.0 - lb) * _sigmoid(xf)
    return 1.0 - f, jnp.log(f)


def _hgrn_dir(*args):
    return _hgrn_dirs([args])[0]


def _hgrn_dirs(arg_sets):
    chains = [_hgrn_phases(*a) for a in arg_sets]
    results = [None] * len(chains)
    while any(r is None for r in results):
        for n, chain in enumerate(chains):
            if results[n] is None:
                try:
                    next(chain)
                except StopIteration as done:
                    results[n] = done.value
    return results


def _hgrn_phases(xq, xf, v, lb, state, cm, cmt, mk, rev):
    t = xq.shape[0]
    n_ch = t // CHUNK
    lo = lax.broadcasted_iota(jnp.int32, (1, LANES), 1) < HEAD_DIM
    q = xq * _sigmoid(xq)
    k, lf = _hgrn_gates(xf, lb)
    cs = _cmm(cm, cmt, lf)
    yield
    b, bm, bl = cs[:t], cs[t:2 * t], cs[2 * t:]
    qd = q * jnp.exp(b - bm)
    kd = k * jnp.exp(bm - b)
    kc = k * jnp.exp(bl - b)
    qe = q * jnp.exp(b)
    yield
    qd2 = jnp.concatenate([jnp.where(lo, qd, 0.0), jnp.where(lo, 0.0, qd)], axis=0)
    scores = _mm_nt(qd2, kd)
    vexp = jnp.where(mk["cols_chunk"], jnp.concatenate([jnp.transpose(v)] * n_ch, axis=0), 0.0)
    adds = _mm_nn(vexp, kc)
    yield
    o2 = _mm_nn(jnp.where(mk["tri2"], scores, 0.0), v)
    o = jnp.where(lo, o2[:t], o2[t:])
    dec = jnp.exp(bl)
    entering = [None] * n_ch
    for c in (range(n_ch - 1, -1, -1) if rev else range(n_ch)):
        entering[c] = state
        d = jnp.concatenate([dec[c * CHUNK:(c + 1) * CHUNK]] * (LANES // CHUNK), axis=0)
        state = d * state + jnp.where(mk["diag"], adds[c * LANES:(c + 1) * LANES], 0.0)
    yield
    qexp = jnp.where(mk["rows_chunk"], jnp.concatenate([qe] * n_ch, axis=1), 0.0)
    return o + _mm_nt(qexp, jnp.concatenate(entering, axis=1)), state


def _hgrn_lower_bounds(l):
    out = []
    for d in (0, 1):
        l0, l1 = l[2 * d:2 * d + 1, :], l[2 * d + 1:2 * d + 2, :]
        mx = jnp.maximum(l0, l1)
        e0, e1 = jnp.exp(l0 - mx), jnp.exp(l1 - mx)
        out.append(e0 / (e0 + e1))
    return out


def _hgrn_consts(t):
    cf, cb = _np_hgrn_cums(t, False), _np_hgrn_cums(t, True)
    return (_bf(cf), _bf(cf.T), _bf(cb), _bf(cb.T), _bf(_np_blocksum(LANES)))


def _hgrn_fwd(ph, lbl, ng):
    s = ph.shape[0]
    t = min(HG_TILE, s)
    nt = s // t
    consts = _hgrn_consts(t)

    def body(xq_ref, xff_ref, xfb_ref, xi_ref, xg_ref, lbl_ref, ng_ref, cf_ref, cft_ref, cb_ref, cbt_ref, bs_ref,
             o_ref, pre_ref, st_ref, acc):
        lbf, lbb = _hgrn_lower_bounds(lbl_ref)
        mk_f, mk_b = _hgrn_masks(t, False), _hgrn_masks(t, True)
        zero = jnp.zeros((LANES, LANES), F32)

        def rows_of(i):
            return pl.ds(pl.multiple_of(i * t, t), t)

        acc[...] = jnp.zeros_like(acc)

        def step(i, states):
            tb = nt - 1 - i
            rf, rb = rows_of(i), rows_of(tb)
            st_ref[0, 0, i] = states[0]
            st_ref[0, 1, tb] = states[1]
            (of, sf), (ob, sb) = _hgrn_dirs([
                (xq_ref[rf, :], xff_ref[rf, :], xi_ref[rf, :], lbf, states[0], cf_ref[...], cft_ref[...], mk_f, False),
                (xq_ref[rb, :], xfb_ref[rb, :], xi_ref[rb, :], lbb, states[1], cb_ref[...], cbt_ref[...], mk_b, True)])
            acc[rf, :] += of
            acc[rb, :] += ob
            return sf, sb

        lax.fori_loop(0, nt, step, (zero, zero))

        def step_n(i, carry):
            rows = rows_of(i)
            o = acc[rows, :]
            ss = _xdot(o * o, bs_ref[...])
            r = lax.rsqrt(ss * (1.0 / HEAD_DIM) + EPS)
            xg = xg_ref[rows, :]
            pre_ref[rows, :] = o
            o_ref[rows, :] = ((o * r * ng_ref[...]) * (xg * _sigmoid(xg))).astype(CD)
            return carry

        lax.fori_loop(0, nt, step_n, 0)

    col = lambda off: pl.BlockSpec((s, LANES), lambda m: (0, off + m))
    full = lambda a: pl.BlockSpec(a.shape, lambda m: (0,) * a.ndim)
    return pl.pallas_call(
        body, name="hgrn_fwd", grid=(4,),
        in_specs=[col(0), col(4), col(8), col(12), col(16), pl.BlockSpec((4, LANES), lambda m: (0, m)),
                  pl.BlockSpec((1, LANES), lambda m: (0, m))] + [full(c) for c in consts],
        out_specs=[col(0), col(0), pl.BlockSpec((1, 2, nt, LANES, LANES), lambda m: (m, 0, 0, 0, 0))],
        out_shape=[jax.ShapeDtypeStruct((s, 512), CD), jax.ShapeDtypeStruct((s, 512), F32),
                   jax.ShapeDtypeStruct((4, 2, nt, LANES, LANES), F32)],
        scratch_shapes=[pltpu.VMEM((s, LANES), F32)],
        compiler_params=_params(1),
    )(ph, ph, ph, ph, ph, lbl, ng, *consts)


def _hgrn_bwd(ph, pre, dout, states, lbl, ng):
    s = ph.shape[0]
    t = min(HG_TILE, s)
    nt = s // t
    consts = _hgrn_consts(t)

    def body(xq_ref, xff_ref, xfb_ref, xi_ref, xg_ref, pre_ref, do_ref, st_ref, lbl_ref, ng_ref,
             cf_ref, cft_ref, cb_ref, cbt_ref, bs_ref,
             dq_ref, dff_ref, dfb_ref, di_ref, dg_ref, dlb_ref, dng_ref, dpre, dq_acc, dv_acc):
        lbf, lbb = _hgrn_lower_bounds(lbl_ref)
        mk_f, mk_b = _hgrn_masks(t, False), _hgrn_masks(t, True)
        zero = jnp.zeros((LANES, LANES), F32)
        zrow = jnp.zeros((1, LANES), F32)

        def rows_of(i):
            return pl.ds(pl.multiple_of(i * t, t), t)

        def step_n(i, dng):
            rows = rows_of(i)
            o, xg, do = pre_ref[rows, :], xg_ref[rows, :], do_ref[rows, :]
            bs = bs_ref[...]
            r = lax.rsqrt(_xdot(o * o, bs) * (1.0 / HEAD_DIM) + EPS)
            sg = _sigmoid(xg)
            gate = xg * sg
            don = do * gate
            dg_ref[rows, :] = (do * (o * r * ng_ref[...]) * (sg * (1.0 + xg * (1.0 - sg)))).astype(CD)
            u = don * ng_ref[...]
            dpre[rows, :] = r * u - o * (r * r * r) * (_xdot(u * o, bs) * (1.0 / HEAD_DIM))
            return dng + jnp.sum(don * o * r, axis=0, keepdims=True)

        dng_ref[...] = lax.fori_loop(0, nt, step_n, zrow)

        dq_acc[...] = jnp.zeros_like(dq_acc)
        dv_acc[...] = jnp.zeros_like(dv_acc)

        def step_g(i, carry):
            dsf, dsb, dlbf, dlbb = carry
            tf, tb = nt - 1 - i, i
            rf, rb = rows_of(tf), rows_of(tb)
            cf, cft, cb, cbt = cf_ref[...], cft_ref[...], cb_ref[...], cbt_ref[...]

            def both(xq_f, xf_f, v_f, lb_f, s_f, xq_b, xf_b, v_b, lb_b, s_b):
                (of, sf), (ob, sb) = _hgrn_dirs([(xq_f, xf_f, v_f, lb_f, s_f, cf, cft, mk_f, False),
                                                 (xq_b, xf_b, v_b, lb_b, s_b, cb, cbt, mk_b, True)])
                return of, sf, ob, sb

            _, vjp = jax.vjp(both, xq_ref[rf, :], xff_ref[rf, :], xi_ref[rf, :], lbf, st_ref[0, 0, tf],
                             xq_ref[rb, :], xfb_ref[rb, :], xi_ref[rb, :], lbb, st_ref[0, 1, tb])
            dq_f, dx_f, dv_f, gf, dsf, dq_b, dx_b, dv_b, gb, dsb = vjp((dpre[rf, :], dsf, dpre[rb, :], dsb))
            dff_ref[rf, :] = dx_f.astype(CD)
            dfb_ref[rb, :] = dx_b.astype(CD)
            dq_acc[rf, :] += dq_f
            dv_acc[rf, :] += dv_f
            dq_acc[rb, :] += dq_b
            dv_acc[rb, :] += dv_b
            return dsf, dsb, dlbf + gf, dlbb + gb

        _, _, dlbf, dlbb = lax.fori_loop(0, nt, step_g, (zero, zero, zrow, zrow))
        dlb_ref[0:1, :] = dlbf
        dlb_ref[1:2, :] = dlbb
        dq_ref[...] = dq_acc[...].astype(CD)
        di_ref[...] = dv_acc[...].astype(CD)

    col = lambda off: pl.BlockSpec((s, LANES), lambda m: (0, off + m))
    full = lambda a: pl.BlockSpec(a.shape, lambda m: (0,) * a.ndim)
    stream = jax.ShapeDtypeStruct((s, 512), CD)
    return pl.pallas_call(
        body, name="hgrn_bwd", grid=(4,),
        in_specs=[col(0), col(4), col(8), col(12), col(16), col(0), col(0),
                  pl.BlockSpec((1, 2, nt, LANES, LANES), lambda m: (m, 0, 0, 0, 0)),
                  pl.BlockSpec((4, LANES), lambda m: (0, m)),
                  pl.BlockSpec((1, LANES), lambda m: (0, m))] + [full(c) for c in consts],
        out_specs=[col(0)] * 5 + [pl.BlockSpec((2, LANES), lambda m: (0, m)), pl.BlockSpec((1, LANES), lambda m: (0, m))],
        out_shape=[stream] * 5 + [jax.ShapeDtypeStruct((2, 512), F32), jax.ShapeDtypeStruct((1, 512), F32)],
        scratch_shapes=[pltpu.VMEM((s, LANES), F32), pltpu.VMEM((s, LANES), F32), pltpu.VMEM((s, LANES), F32)],
        compiler_params=_params(1, HGRN_BWD_VMEM),
    )(ph, ph, ph, ph, ph, pre, dout, states, lbl, ng, *consts)


def _branch_out(o, w4):
    o = o.astype(CD)
    return jnp.concatenate([_mm(o, w4[j], NN) for j in range(N_SHARD)], axis=1)


def _mix_out_fwd(x, oa, ob, pg, wa, wb, wo, tm):
    s, d = x.shape

    def body(x_ref, oa_ref, ob_ref, ga_ref, gb_ref, wa_ref, wb_ref, wo_ref, xo_ref):
        ya = _branch_out(oa_ref[...], wa_ref)
        yb = _branch_out(ob_ref[...], wb_ref)
        merged = _sigmoid(ga_ref[...].astype(F32)) * ya + _sigmoid(gb_ref[...].astype(F32)) * yb
        xo_ref[...] = x_ref[...] + _mm(merged, wo_ref[...], NN)

    row = pl.BlockSpec((tm, d), lambda i: (i, 0))
    half = pl.BlockSpec((tm, 512), lambda i: (i, 0))
    full = lambda a: pl.BlockSpec(a.shape, lambda i: (0,) * a.ndim)
    return pl.pallas_call(
        body, name="mix_out_fwd", grid=(s // tm,),
        in_specs=[row, half, half, row, pl.BlockSpec((tm, d), lambda i: (i, 1)), full(wa), full(wb), full(wo)],
        out_specs=row, out_shape=jax.ShapeDtypeStruct((s, d), F32),
        compiler_params=_params(1),
    )(x, oa, ob, pg, pg, wa, wb, wo)


def _mix_out_bwd(dx, oa, ob, pg, wa, wb, wo, tm, after=()):
    s, d = dx.shape
    eq, ebc = _bf(_np_expand_q()), _bf(_np_headsum_spread())

    def body(*refs):
        (dx_ref, oa_ref, ob_ref, ga_ref, gb_ref, wa_ref, wb_ref, wo_ref, eq_ref, ebc_ref,
         dpg_ref, mg_ref, dya_ref, dyb_ref, doe_ref, dl_ref, dob_ref) = refs[len(after):]
        oa = oa_ref[...]
        ya = _branch_out(oa, wa_ref)
        yb = _branch_out(ob_ref[...], wb_ref)
        sa, sb = _sigmoid(ga_ref[...].astype(F32)), _sigmoid(gb_ref[...].astype(F32))
        mg_ref[...] = (sa * ya + sb * yb).astype(CD)
        dm = _mm(dx_ref[...], wo_ref[...], NT)
        dpg_ref[...] = jnp.concatenate([dm * ya * sa * (1.0 - sa), dm * yb * sb * (1.0 - sb)], axis=1).astype(CD)
        dya, dyb = dm * sa, dm * sb
        dya_ref[...] = dya.astype(CD)
        dyb_ref[...] = dyb.astype(CD)
        doa = jnp.zeros(oa.shape, F32)
        dob = jnp.zeros(oa.shape, F32)
        for j in range(N_SHARD):
            doa = doa + _mm(dya[:, 256 * j:256 * j + 256], wa_ref[j], NT)
            dob = dob + _mm(dyb[:, 256 * j:256 * j + 256], wb_ref[j], NT)
        dob_ref[...] = dob
        doe_ref[...] = _mm(doa, eq_ref[...], NN).astype(CD)
        dl_ref[...] = _xdot(doa * oa, ebc_ref[...])

    row = pl.BlockSpec((tm, d), lambda i: (i, 0))
    half = pl.BlockSpec((tm, 512), lambda i: (i, 0))
    full = lambda a: pl.BlockSpec(a.shape, lambda i: (0,) * a.ndim)
    wide = jax.ShapeDtypeStruct((s, d), CD)
    return pl.pallas_call(
        body, name="mix_out_bwd", grid=(s // tm,),
        in_specs=[ANY] * len(after) + [row, half, half, row, pl.BlockSpec((tm, d), lambda i: (i, 1)), full(wa), full(wb),
                                       full(wo), full(eq), full(ebc)],
        out_specs=[pl.BlockSpec((tm, 2048), lambda i: (i, 0)), row, row, row, row, row, half],
        out_shape=[jax.ShapeDtypeStruct((s, 2048), CD), wide, wide, wide, wide, jax.ShapeDtypeStruct((s, d), F32),
                   jax.ShapeDtypeStruct((s, 512), F32)],
        compiler_params=_params(1),
    )(*after, dx, oa, ob, pg, pg, wa, wb, wo, eq, ebc)


def _loss_head(x, g, target, tm):
    s, d = x.shape

    def body(x_ref, g_ref, t_ref, dx_ref, loss_ref, dg_ref):
        @pl.when(pl.program_id(0) == 0)
        def _():
            loss_ref[...] = jnp.zeros_like(loss_ref)
            dg_ref[...] = jnp.zeros_like(dg_ref)

        xv = x_ref[...]
        r = lax.rsqrt(jnp.mean(xv * xv, axis=-1, keepdims=True) + EPS)
        err = xv * r * g_ref[...] - t_ref[...]
        loss_ref[...] += 0.5 * jnp.sum(jnp.mean(err * err, axis=-1, keepdims=True))
        dy = err * (1.0 / d)
        u = dy * g_ref[...]
        dx_ref[...] = r * u - xv * (r * r * r) * jnp.mean(u * xv, axis=-1, keepdims=True)
        dg_ref[...] += jnp.sum(dy * xv * r, axis=0, keepdims=True)

    row = pl.BlockSpec((tm, d), lambda i: (i, 0))
    vec = pl.BlockSpec((1, d), lambda i: (0, 0))
    return pl.pallas_call(
        body, name="loss_head", grid=(s // tm,),
        in_specs=[row, vec, row], out_specs=[row, pl.BlockSpec((8, LANES), lambda i: (0, 0)), vec],
        out_shape=[jax.ShapeDtypeStruct((s, d), F32), jax.ShapeDtypeStruct((8, LANES), F32),
                   jax.ShapeDtypeStruct((1, d), F32)],
        compiler_params=_params(1),
    )(x, g, target)


def _position():
    x, y, c = lax.axis_index("x"), lax.axis_index("y"), lax.axis_index("c")
    return x, y, c, [(1 - x, y), (x, 1 - y), (1 - x, 1 - y)]


def _row_tile(rows, cap=256):
    best = rows
    for cand in range(8, min(rows, cap) + 1, 8):
        if rows % cand == 0:
            best = cand
    return best


def _cast_into_slots(shards, dtypes, me_idx):
    n = len(shards)
    tiles = [_row_tile(s.shape[0]) for s in shards]
    counts = [s.shape[0] // t for s, t in zip(shards, tiles)]
    starts = [sum(counts[:a]) for a in range(n)]

    def body(me_ref, *refs):
        i = pl.program_id(0)
        for a in range(n):
            @pl.when((i >= starts[a]) & (i < starts[a] + counts[a]))
            def _(a=a):
                refs[n + a][0] = refs[a][...].astype(dtypes[a])

    tile_of = [lambda i, a=a: jnp.clip(i - starts[a], 0, counts[a] - 1) for a in range(n)]
    return pl.pallas_call(
        body, name="cast_into_slots",
        grid_spec=pltpu.PrefetchScalarGridSpec(
            num_scalar_prefetch=1, grid=(sum(counts),),
            in_specs=[pl.BlockSpec((tiles[a], shards[a].shape[1]), lambda i, me, a=a: (tile_of[a](i), 0)) for a in range(n)],
            out_specs=[pl.BlockSpec((1, tiles[a], shards[a].shape[1]), lambda i, me, a=a: (me[0], tile_of[a](i), 0))
                       for a in range(n)]),
        out_shape=[jax.ShapeDtypeStruct((N_SHARD,) + s.shape, dt) for s, dt in zip(shards, dtypes)],
        compiler_params=_params(1),
    )(me_idx, *shards)


HBM_SPEC = pl.BlockSpec(memory_space=pltpu.HBM)
SEM_SPEC = pl.BlockSpec(memory_space=pltpu.SEMAPHORE)
DATAFLOW = pltpu.SideEffectType.DATAFLOW_SIDE_EFFECTING


def _exchange_copies(srcs, lands, send, recv, gather):
    x, y, c, chips = _position()
    me = 2 * x + y
    out = []
    for a in range(len(lands)):
        dst = lands[a].at[me]
        if gather and _halved(lands[a]):
            half = lands[a].shape[1] // 2
            dst = lands[a].at[me, pl.ds(c * half, half), :]
        for k, (px, py) in enumerate(chips):
            src = dst if gather else srcs[a].at[2 * px + py]
            out.append(pltpu.make_async_remote_copy(src_ref=src, dst_ref=dst, send_sem=send.at[3 * a + k],
                                                    recv_sem=recv.at[3 * a + k], device_id=(px, py, c), device_id_type=MESH))
    return out


def _halved(land):
    return land.shape[1] % 32 == 0


def _pair_fill(name, lands):
    n = len(lands)

    def body(*refs):
        src, dst = refs[:n], refs[n:2 * n]
        send, recv = refs[2 * n:]
        x, y, c, chips = _position()
        copies = []
        for a in range(n):
            half = src[a].shape[1] // 2
            for k, (px, py) in enumerate(chips):
                rows = (2 * px + py, pl.ds(c * half, half), slice(None))
                cp = pltpu.make_async_remote_copy(src_ref=src[a].at[rows], dst_ref=dst[a].at[rows], send_sem=send.at[a, k],
                                                  recv_sem=recv.at[a, k], device_id=(x, y, 1 - c), device_id_type=MESH)
                cp.start()
                copies.append(cp)
        for cp in copies:
            cp.wait()

    return pl.pallas_call(
        body, name=name, in_specs=[ANY] * n, out_specs=[ANY] * n,
        out_shape=[jax.ShapeDtypeStruct(l.shape, l.dtype) for l in lands],
        input_output_aliases={a: a for a in range(n)},
        scratch_shapes=[pltpu.SemaphoreType.DMA((n, 3)), pltpu.SemaphoreType.DMA((n, 3))],
    )(*lands)


def _exchange_start(name, srcs, lands, after):
    ns, nl, na = len(srcs), len(lands), len(after)
    gather = ns == 0

    def body(*refs):
        src_refs, land_refs = refs[:ns], refs[ns:ns + nl]
        send, recv = refs[ns + nl + na], refs[ns + nl + na + 1]
        token = refs[-1]
        for cp in _exchange_copies(src_refs, land_refs, send, recv, gather):
            cp.start()
        token[...] = jnp.zeros_like(token)

    arrays = [pltpu.with_memory_space_constraint(a, pltpu.HBM) for a in list(srcs) + list(lands)]
    outs = pl.pallas_call(
        body, name=name,
        out_shape=(pltpu.SemaphoreType.DMA((3 * nl,)), pltpu.SemaphoreType.DMA((3 * nl,)),
                   *[pltpu.HBM(a.shape, a.dtype) for a in arrays], jax.ShapeDtypeStruct((8, LANES), F32)),
        in_specs=[HBM_SPEC] * (ns + nl) + [ANY] * na,
        out_specs=(SEM_SPEC, SEM_SPEC, *[HBM_SPEC] * (ns + nl), pl.BlockSpec(memory_space=pltpu.VMEM)),
        input_output_aliases={i: 2 + i for i in range(ns + nl)},
        compiler_params=pltpu.CompilerParams(has_side_effects=DATAFLOW),
    )(*arrays, *after)
    return outs[0], outs[1], list(outs[2:2 + ns]), list(outs[2 + ns:2 + ns + nl]), outs[-1]


def _exchange_wait(name, send, recv, srcs, lands, after):
    ns, nl, na = len(srcs), len(lands), len(after)
    gather = ns == 0

    def body(*refs):
        src_refs, land_refs = refs[:ns], refs[ns:ns + nl]
        send_ref, recv_ref = refs[ns + nl], refs[ns + nl + 1]
        for cp in _exchange_copies(src_refs, land_refs, send_ref, recv_ref, gather):
            cp.wait_send()
            cp.wait_recv()

    outs = pl.pallas_call(
        body, name=name,
        out_shape=tuple(pltpu.HBM(a.shape, a.dtype) for a in list(srcs) + list(lands)),
        in_specs=[HBM_SPEC] * (ns + nl) + [SEM_SPEC, SEM_SPEC] + [ANY] * na,
        out_specs=tuple([HBM_SPEC] * (ns + nl)),
        input_output_aliases={i: i for i in range(ns + nl)},
        compiler_params=pltpu.CompilerParams(has_side_effects=DATAFLOW),
    )(*srcs, *lands, send, recv, *after)
    return list(outs[ns:])


def _pair_exchange(grads):
    n = len(grads)

    def body(*refs):
        src, dst = refs[:n], refs[n:2 * n]
        send, recv = refs[2 * n:]
        x, y, c, _ = _position()
        copies = []
        for a in range(n):
            half = src[a].shape[1] // 2
            cp = pltpu.make_async_remote_copy(
                src_ref=src[a].at[:, pl.ds((1 - c) * half, half), :], dst_ref=dst[a], send_sem=send.at[a],
                recv_sem=recv.at[a], device_id=(x, y, 1 - c), device_id_type=MESH)
            cp.start()
            copies.append(cp)
        for cp in copies:
            cp.wait()

    return pl.pallas_call(
        body, name="grad_pair_exchange", in_specs=[ANY] * n, out_specs=[ANY] * n,
        out_shape=[jax.ShapeDtypeStruct((g.shape[0], g.shape[1] // 2, g.shape[2]), g.dtype) for g in grads],
        scratch_shapes=[pltpu.SemaphoreType.DMA((n,)), pltpu.SemaphoreType.DMA((n,))],
    )(*grads)


def _shard_of(a):
    return lambda i: jnp.clip(i - a * N_SHARD, 0, N_SHARD - 1)


def _pair_sum(gs, gots, c_idx, me_idx):
    n = len(gs)
    halves = [(g.shape[1] // 2, g.shape[2]) for g in gs]

    def body(c_ref, me_ref, *refs):
        g_refs, got_refs, s_refs, own_refs = (refs[k * n:(k + 1) * n] for k in range(4))
        i = pl.program_id(0)
        for a in range(n):
            @pl.when(i // N_SHARD == a)
            def _(a=a):
                sm = g_refs[a][...] + got_refs[a][...].astype(F32)
                s_refs[a][...] = sm.astype(CD)

                @pl.when(i % N_SHARD == me_ref[0])
                def _():
                    own_refs[a][...] = sm[0]

    shard = [_shard_of(a) for a in range(n)]
    return pl.pallas_call(
        body, name="grad_pair_sum",
        grid_spec=pltpu.PrefetchScalarGridSpec(
            num_scalar_prefetch=2, grid=(n * N_SHARD,),
            in_specs=[pl.BlockSpec((1, h, c_), lambda i, c, me, a=a: (shard[a](i), c[0], 0)) for a, (h, c_) in enumerate(halves)]
            + [pl.BlockSpec((1, h, c_), lambda i, c, me, a=a: (shard[a](i), 0, 0)) for a, (h, c_) in enumerate(halves)],
            out_specs=[pl.BlockSpec((1, h, c_), lambda i, c, me, a=a: (shard[a](i), 0, 0)) for a, (h, c_) in enumerate(halves)]
            + [pl.BlockSpec((h, c_), lambda i, c, me: (0, 0)) for h, c_ in halves]),
        out_shape=[jax.ShapeDtypeStruct((N_SHARD, h, c_), CD) for h, c_ in halves]
        + [jax.ShapeDtypeStruct((h, c_), F32) for h, c_ in halves],
        compiler_params=_params(1),
    )(c_idx, me_idx, *gs, *gots)


def _chip_sum(owns, gots, me_idx):
    n = len(owns)
    tiles = [_row_tile(o.shape[0]) for o in owns]
    counts = [o.shape[0] // t for o, t in zip(owns, tiles)]
    starts = [sum(counts[:a]) for a in range(n)]

    def body(me_ref, *refs):
        own_refs, got_refs, out_refs = (refs[k * n:(k + 1) * n] for k in range(3))
        i = pl.program_id(0)
        for a in range(n):
            @pl.when((i >= starts[a]) & (i < starts[a] + counts[a]))
            def _(a=a):
                total = None
                for j in range(N_SHARD):
                    term = jnp.where(j == me_ref[0], own_refs[a][...], got_refs[a][j].astype(F32))
                    total = term if total is None else total + term
                out_refs[a][...] = total

    tile_of = [lambda i, a=a: jnp.clip(i - starts[a], 0, counts[a] - 1) for a in range(n)]
    own_specs = [pl.BlockSpec((tiles[a], owns[a].shape[1]), lambda i, me, a=a: (tile_of[a](i), 0)) for a in range(n)]
    return pl.pallas_call(
        body, name="grad_chip_sum",
        grid_spec=pltpu.PrefetchScalarGridSpec(
            num_scalar_prefetch=1, grid=(sum(counts),),
            in_specs=own_specs + [pl.BlockSpec((N_SHARD, tiles[a], owns[a].shape[1]), lambda i, me, a=a: (0, tile_of[a](i), 0))
                                  for a in range(n)],
            out_specs=own_specs),
        out_shape=[jax.ShapeDtypeStruct(o.shape, F32) for o in owns],
        compiler_params=_params(1),
    )(me_idx, *owns, *gots)


def _pair_share(halves):
    n = len(halves)

    def body(*refs):
        src, dst = refs[:n], refs[n:2 * n]
        send, recv = refs[2 * n:]
        x, y, c, _ = _position()
        copies = []
        for a in range(n):
            cp = pltpu.make_async_remote_copy(src_ref=src[a], dst_ref=dst[a], send_sem=send.at[a],
                                              recv_sem=recv.at[a], device_id=(x, y, 1 - c), device_id_type=MESH)
            cp.start()
            copies.append(cp)
        for cp in copies:
            cp.wait()

    return pl.pallas_call(
        body, name="grad_pair_share", in_specs=[ANY] * n, out_specs=[ANY] * n,
        out_shape=[jax.ShapeDtypeStruct(h.shape, h.dtype) for h in halves],
        scratch_shapes=[pltpu.SemaphoreType.DMA((n,)), pltpu.SemaphoreType.DMA((n,))],
    )(*halves)


def _small_allreduce(buf):
    rows, cols = buf.shape

    def body(src_ref, out_ref, slots, send, recv):
        x, y, c, _ = _position()
        me = 4 * x + 2 * y + c
        slots[me] = src_ref[...]
        copies = []
        k = 0
        for dx in (0, 1):
            for dy in (0, 1):
                for dc in (0, 1):
                    if (dx, dy, dc) == (0, 0, 0):
                        continue
                    peer = (jnp.where(dx, 1 - x, x), jnp.where(dy, 1 - y, y), jnp.where(dc, 1 - c, c))
                    cp = pltpu.make_async_remote_copy(src_ref=src_ref, dst_ref=slots.at[me], send_sem=send.at[k],
                                                      recv_sem=recv.at[k], device_id=peer, device_id_type=MESH)
                    cp.start()
                    copies.append(cp)
                    k += 1
        for cp in copies:
            cp.wait()
        total = slots[0]
        for dev in range(1, N_DEV):
            total = total + slots[dev]
        out_ref[...] = total

    vm = pl.BlockSpec(memory_space=pltpu.VMEM)
    return pl.pallas_call(
        body, name="small_allreduce", in_specs=[vm], out_specs=vm,
        out_shape=jax.ShapeDtypeStruct((rows, cols), F32),
        scratch_shapes=[pltpu.VMEM((N_DEV, rows, cols), F32), pltpu.SemaphoreType.DMA((N_DEV - 1,)),
                        pltpu.SemaphoreType.DMA((N_DEV - 1,))],
    )(buf)


def _adamw_math(w, gv, m, v):
    mn = ADAM_B1 * m + (1.0 - ADAM_B1) * gv
    vn = ADAM_B2 * v + (1.0 - ADAM_B2) * (gv * gv)
    m_hat = mn / (1.0 - ADAM_B1 ** ADAM_STEP)
    v_hat = vn / (1.0 - ADAM_B2 ** ADAM_STEP)
    return -ADAM_LR * (m_hat / (jnp.sqrt(v_hat) + ADAM_EPS) + ADAM_WD * w), mn, vn


def _adamw(w, g, m, v):
    rows, cols = w.shape
    tr = _row_tile(rows)

    def body(w_ref, g_ref, m_ref, v_ref, d_ref, mo_ref, vo_ref):
        d_ref[...], mo_ref[...], vo_ref[...] = _adamw_math(w_ref[...], g_ref[...], m_ref[...], v_ref[...])

    blk = pl.BlockSpec((tr, cols), lambda i: (i, 0))
    shp = jax.ShapeDtypeStruct((rows, cols), F32)
    return pl.pallas_call(
        body, name="adamw", grid=(rows // tr,), in_specs=[blk] * 4, out_specs=[blk] * 3, out_shape=[shp] * 3,
        compiler_params=_params(1),
    )(w, g, m, v)


def _adamw_halves(w, own, got, m, v, c_idx):
    rows, cols = w.shape
    tr = _row_tile(rows // 2)
    per_half = rows // 2 // tr

    def body(c_ref, w_ref, own_ref, got_ref, m_ref, v_ref, d_ref, mo_ref, vo_ref, g_ref):
        mine = (pl.program_id(0) // per_half) == c_ref[0]
        gv = jnp.where(mine, own_ref[...], got_ref[...])
        g_ref[...] = gv
        d_ref[...], mo_ref[...], vo_ref[...] = _adamw_math(w_ref[...], gv, m_ref[...], v_ref[...])

    blk = pl.BlockSpec((tr, cols), lambda i, c: (i, 0))
    own_blk = pl.BlockSpec((tr, cols), lambda i, c: (jnp.where(i // per_half == c[0], i % per_half, 0), 0))
    got_blk = pl.BlockSpec((tr, cols), lambda i, c: (jnp.where(i // per_half == c[0], 0, i % per_half), 0))
    shp = jax.ShapeDtypeStruct((rows, cols), F32)
    return pl.pallas_call(
        body, name="adamw_halves",
        grid_spec=pltpu.PrefetchScalarGridSpec(num_scalar_prefetch=1, grid=(rows // tr,),
                                               in_specs=[blk, own_blk, got_blk, blk, blk], out_specs=[blk] * 4),
        out_shape=[shp] * 4, compiler_params=_params(1),
    )(c_idx, w, own, got, m, v)


def _local_step(x, target, norm_gains, q_g, k_g, ng, weights_of, grads_done):
    s = x.shape[0]
    tm = min(512, s)
    tq = min(256, s)
    tf = min(1024, s)
    g1, gm, g2, gf = norm_gains
    cos2, sin2 = _rope_tables(s)
    gq8 = jnp.tile(q_g, (1, 8))
    gk2 = jnp.tile(k_g, (1, 2))

    tn = min(256, s)
    tk = min(1024, s)
    w1 = weights_of(1, ())
    x1, s1, t1, b1, h1 = _ffn_fwd(x, g1, w1["g1"], w1["u1"], w1["d1"], tf)
    w2 = weights_of(2, (x1,))
    lbl = w2["lbl"]
    pqkv, ph, pg, hm = _mix_in_fwd(x1, gm, w2["in"], tn)
    qe, kr, vr, vs = _qk_prep(pqkv, gq8, gk2, cos2, sin2, tm)
    oa, lse = _attn_fwd(qe, kr, vr, vs, tq)
    ob, pre, hstates = _hgrn_fwd(ph, lbl, ng)
    x2 = _mix_out_fwd(x1, oa, ob, pg, w2["a"], w2["b"], w2["o"], tm)
    w3 = weights_of(3, (x2,))
    x3, s2, t2, b2, h2 = _ffn_fwd(x2, g2, w3["g2"], w3["u2"], w3["d2"], tf)
    dx3, loss, dgf = _loss_head(x3, gf, target, tm)

    dx2, da2, db2, f2, dg2, dx3c = _ffn_bwd(dx3, x2, g2, s2, t2, b2, w3["g2"], w3["u2"], w3["d2"], tm)
    tok = grads_done(3, dict(g2=_dw_shared_b("dw_gate", da2, h2, tk, 1.0), u2=_dw_shared_b("dw_gate", db2, h2, tk, 1.0),
                             d2=_dw_shared_b("dw_down", f2, dx3c, tk, 1.0)))

    dpg, mg, dya, dyb, doe, delta, dob = _mix_out_bwd(dx2, oa, ob, pg, w2["a"], w2["b"], w2["o"], tm, tok)
    g_o = [g.reshape(N_SHARD, D_MODEL // N_SHARD, D_MODEL) for g in _dw_colblocks("dw_out", mg, dx2, 1, tk)]
    g_a = _dw_colblocks("dw_branch", oa, dya, N_SHARD, tk)
    g_b = _dw_colblocks("dw_branch", ob, dyb, N_SHARD, tk)
    dqt, dkt, dvt = _attn_bwd(qe, kr, kr.T, vr, doe, delta, lse, tq)
    dqkv, dgq, dgk = _qk_prep_bwd(pqkv, dqt.T, dkt.T, dvt.T, gq8, gk2, cos2, sin2, tm)
    dhq, dhff, dhfb, dhi, dhg, dlb, dng = _hgrn_bwd(ph, pre, dob, hstates, lbl, ng)
    dps = (dqkv, dhq, dhff, dhfb, dhi, dhg, dpg)
    g_in = [g.reshape(N_SHARD, -1, D_MODEL) for g in _dw_in(dps, hm, tk)]
    tok = grads_done(2, {"in": g_in, "a": g_a, "b": g_b, "o": g_o})
    dx1, dgm = _mix_in_bwd(dps, w2["in"], x1, dx2, gm, tn, tok)

    dx0, da1, db1, f1, dg1, dx1c = _ffn_bwd(dx1, x, g1, s1, t1, b1, w1["g1"], w1["u1"], w1["d1"], tm)
    grads_done(1, dict(g1=_dw_shared_b("dw_gate", da1, h1, tk, 1.0), u1=_dw_shared_b("dw_gate", db1, h1, tk, 1.0),
                       d1=_dw_shared_b("dw_down", f1, dx1c, tk, 1.0)))
    small = dict(g1=dg1, gm=dgm, g2=dg2, gf=dgf, gq=dgq, gk=dgk, lb=dlb, ng=dng)
    return loss, dx0, small, lbl


GROUPS = {1: ("g1", "u1", "d1"), 2: ("in", "a", "b", "o"), 3: ("g2", "u2", "d2")}
BIG = GROUPS[1] + GROUPS[2] + GROUPS[3]
TRANSPOSED = ("g1", "u1", "in", "g2", "u2")


def _pack_rows(vectors, width):
    rows = []
    for vct in vectors:
        flat = vct.reshape(-1)
        pad = (-flat.shape[0]) % width
        rows.append(jnp.pad(flat, (0, pad)).reshape(-1, width))
    return jnp.concatenate(rows, axis=0)


def kernel(x, ffn1_norm_g, ffn1_w_gate, ffn1_w_up, ffn1_w_down, mix_norm_g, w_in, q_norm_g, k_norm_g, hgrn_lb_logits, hgrn_out_norm_g, w_branch_attn, w_branch_hgrn, w_out, ffn2_norm_g, ffn2_w_gate, ffn2_w_up, ffn2_w_down, final_norm_g, loss_target, m_ffn1_norm_g, m_ffn1_w_gate, m_ffn1_w_up, m_ffn1_w_down, m_mix_norm_g, m_w_in, m_q_norm_g, m_k_norm_g, m_hgrn_lb_logits, m_hgrn_out_norm_g, m_w_branch_attn, m_w_branch_hgrn, m_w_out, m_ffn2_norm_g, m_ffn2_w_gate, m_ffn2_w_up, m_ffn2_w_down, m_final_norm_g, v_ffn1_norm_g, v_ffn1_w_gate, v_ffn1_w_up, v_ffn1_w_down, v_mix_norm_g, v_w_in, v_q_norm_g, v_k_norm_g, v_hgrn_lb_logits, v_hgrn_out_norm_g, v_w_branch_attn, v_w_branch_hgrn, v_w_out, v_ffn2_norm_g, v_ffn2_w_gate, v_ffn2_w_up, v_ffn2_w_down, v_final_norm_g):
    xi, yi, ci = lax.axis_index("x"), lax.axis_index("y"), lax.axis_index("c")
    me = 2 * xi + yi
    c_idx = jnp.reshape(ci, (1,)).astype(jnp.int32)
    me_idx = jnp.reshape(me, (1,)).astype(jnp.int32)

    big_w = dict(g1=ffn1_w_gate[0], u1=ffn1_w_up[0], d1=ffn1_w_down[0], a=w_branch_attn[0], b=w_branch_hgrn[0],
                 o=w_out[0], g2=ffn2_w_gate[0], u2=ffn2_w_up[0], d2=ffn2_w_down[0])
    big_w["in"] = w_in[0]
    big_m = dict(g1=m_ffn1_w_gate[0], u1=m_ffn1_w_up[0], d1=m_ffn1_w_down[0], a=m_w_branch_attn[0], b=m_w_branch_hgrn[0],
                 o=m_w_out[0], g2=m_ffn2_w_gate[0], u2=m_ffn2_w_up[0], d2=m_ffn2_w_down[0])
    big_m["in"] = m_w_in[0]
    big_v = dict(g1=v_ffn1_w_gate[0], u1=v_ffn1_w_up[0], d1=v_ffn1_w_down[0], a=v_w_branch_attn[0], b=v_w_branch_hgrn[0],
                 o=v_w_out[0], g2=v_ffn2_w_gate[0], u2=v_ffn2_w_up[0], d2=v_ffn2_w_down[0])
    big_v["in"] = v_w_in[0]
    for table in (big_w, big_m, big_v):
        for n in TRANSPOSED:
            table[n] = table[n].T

    started, token = {}, ()
    for grp in (1, 2, 3):
        shards = [big_w[n] for n in GROUPS[grp]] + ([hgrn_lb_logits.reshape(4, LANES)] if grp == 2 else [])
        dtypes = [CD] * len(GROUPS[grp]) + ([F32] if grp == 2 else [])
        lands = _cast_into_slots(shards, dtypes, me_idx)
        send, recv, _, lands, tok = _exchange_start("gather%d_start" % grp, [], lands, token)
        started[grp], token = (send, recv, lands), (tok,)

    def weights_of(grp, after):
        send, recv, lands = started[grp]
        got = _exchange_wait("gather%d_wait" % grp, send, recv, [], lands, tuple(after) + (token if grp == 1 else ()))
        by_halves = [i for i, land in enumerate(got) if _halved(land)]
        for i, whole in zip(by_halves, _pair_fill("gather%d_fill" % grp, [got[i] for i in by_halves])):
            got[i] = whole
        w = dict(zip(GROUPS[grp], got))
        if grp == 2:
            w["in"] = w["in"].reshape(-1, D_MODEL)
            w["o"] = w["o"].reshape(D_MODEL, D_MODEL)
            w["lbl"] = jnp.transpose(got[-1], (1, 0, 2)).reshape(4, N_SHARD * LANES)
        return w

    pending = {}

    def grads_done(grp, grads):
        names = list(grads)
        got = _pair_exchange([grads[n][1] for n in names])
        res = _pair_sum([grads[n][0] for n in names], got, c_idx, me_idx)
        sums, owns = res[:len(names)], res[len(names):]
        lands = [lax.empty(s_.shape, s_.dtype) for s_ in sums]
        send, recv, srcs, lands, tok = _exchange_start("reduce%d_start" % grp, list(sums), lands, ())
        pending[grp] = (names, send, recv, srcs, lands, owns, tok)
        return (tok,)

    def reduced_halves(grp, after):
        names, send, recv, srcs, lands, owns, _ = pending[grp]
        parts = _exchange_wait("reduce%d_wait" % grp, send, recv, srcs, lands, after)
        return names, list(_chip_sum(list(owns), parts, me_idx))

    loss, dx, small, lbl = _local_step(
        x[0], loss_target[0], (ffn1_norm_g, mix_norm_g, ffn2_norm_g, final_norm_g.reshape(1, -1)),
        q_norm_g, k_norm_g, hgrn_out_norm_g, weights_of, grads_done)

    dgq = small["gq"].reshape(8, HEAD_DIM).sum(axis=0)
    dgk = small["gk"].reshape(2, HEAD_DIM).sum(axis=0)
    lb_full = _hgrn_lower_bounds(lbl)
    dlog = []
    for d in (0, 1):
        t = small["lb"][d:d + 1] * lb_full[d] * (1.0 - lb_full[d])
        dlog += [t, -t]
    small_list = [small["g1"], small["gm"], small["g2"], small["gf"], small["ng"], dgq, dgk, jnp.concatenate(dlog, axis=0), loss[0, 0]]
    packed = _pack_rows(small_list, D_MODEL)
    n_rows = packed.shape[0]
    packed = jnp.pad(packed, ((0, (-n_rows) % 8), (0, 0)))
    red = _small_allreduce(packed)
    loss_out = red[n_rows - 1, 0]
    sg = dict(g1=red[0:1], gm=red[1:2], g2=red[2:3], gf=red[3], ng=red[4:5, :512], gq=red[5:6, :HEAD_DIM],
              gk=red[6:7, :HEAD_DIM])
    dlog_full = red[7:9].reshape(2, 2, 512)
    sg["lb"] = lax.dynamic_slice_in_dim(dlog_full, me * LANES, LANES, axis=2)

    small_w = dict(g1=ffn1_norm_g, gm=mix_norm_g, g2=ffn2_norm_g, gf=final_norm_g, ng=hgrn_out_norm_g, gq=q_norm_g,
                   gk=k_norm_g, lb=hgrn_lb_logits)
    small_m = dict(g1=m_ffn1_norm_g, gm=m_mix_norm_g, g2=m_ffn2_norm_g, gf=m_final_norm_g, ng=m_hgrn_out_norm_g,
                   gq=m_q_norm_g, gk=m_k_norm_g, lb=m_hgrn_lb_logits)
    small_v = dict(g1=v_ffn1_norm_g, gm=v_mix_norm_g, g2=v_ffn2_norm_g, gf=v_final_norm_g, ng=v_hgrn_out_norm_g,
                   gq=v_q_norm_g, gk=v_k_norm_g, lb=v_hgrn_lb_logits)
    small_names = ("g1", "gm", "g2", "gf", "ng", "gq", "gk", "lb")
    pack = lambda dct: _pack_rows([dct[n] for n in small_names], D_MODEL)
    pw, pgr, pm, pv = pack(small_w), pack(sg), pack(small_m), pack(small_v)
    pad8 = lambda a: jnp.pad(a, ((0, (-a.shape[0]) % 8), (0, 0)))
    sd, sm_, sv_ = _adamw(pad8(pw), pad8(pgr), pad8(pm), pad8(pv))

    def unpack(buf):
        out, r = {}, 0
        for n in small_names:
            size = small_w[n].size
            nr = -(-size // D_MODEL)
            out[n] = buf[r:r + nr].reshape(-1)[:size].reshape(small_w[n].shape)
            r += nr
        return out

    sdelta, snew_m, snew_v = unpack(sd), unpack(sm_), unpack(sv_)
    sgrad = {n: sg[n].reshape(small_w[n].shape) for n in small_names}

    bdelta, bnew_m, bnew_v, bgrad = {}, {}, {}, {}

    def update(names, halves):
        for n, own, got in zip(names, halves, _pair_share(halves)):
            res = _adamw_halves(big_w[n], own, got, big_m[n], big_v[n], c_idx)
            if n in TRANSPOSED:
                res = [r.T for r in res]
            bdelta[n], bnew_m[n], bnew_v[n], bgrad[n] = [r[None] for r in res]

    names3, halves3 = reduced_halves(3, (pending[1][-1],))
    names2, halves2 = reduced_halves(2, (halves3[0],))
    update(names3 + names2, halves3 + halves2)
    names1, halves1 = reduced_halves(1, (bdelta[names2[-1]],))
    update(names1, halves1)

    order = [("s", "g1"), ("b", "g1"), ("b", "u1"), ("b", "d1"), ("s", "gm"), ("b", "in"), ("s", "gq"), ("s", "gk"),
             ("s", "lb"), ("s", "ng"), ("b", "a"), ("b", "b"), ("b", "o"), ("s", "g2"), ("b", "g2"), ("b", "u2"),
             ("b", "d2"), ("s", "gf")]
    outs = [loss_out, dx[None]]
    for table_s, table_b in ((sgrad, bgrad), (sdelta, bdelta), (snew_m, bnew_m), (snew_v, bnew_v)):
        outs += [(table_s if kind == "s" else table_b)[n] for kind, n in order]
    return tuple(outs)
```

```python
import functools

import numpy as np
import jax
import jax.numpy as jnp
from jax import lax
from jax.experimental import pallas as pl
from jax.experimental.pallas import tpu as pltpu

F32 = jnp.float32
BF16 = jnp.bfloat16
CD = jnp.bfloat16

EPS = 1e-6
D_MODEL = 1024
HEAD_DIM = 64
GRID_W = 64
ROPE_THETA = 10000.0
CHUNK = 32
N_SHARD = 4
N_DEV = 8
VMEM_LIMIT = 56 * 1024 * 1024
HGRN_BWD_VMEM = 60 * 1024 * 1024
LANES = 128
HG_TILE = 256
ATTN_BWD_HEADS = 2
FFN_ROWS = 256

ADAM_LR = 0.001
ADAM_B1 = 0.9
ADAM_B2 = 0.999
ADAM_EPS = 1e-08
ADAM_WD = 0.01
ADAM_STEP = 10

NN = (((1,), (0,)), ((), ()))
NT = (((1,), (1,)), ((), ()))
TN = (((0,), (0,)), ((), ()))
MESH = pl.DeviceIdType.MESH
ANY = pl.BlockSpec(memory_space=pl.ANY)


def _mm(a, b, dn):
    return lax.dot_general(a.astype(CD), b.astype(CD), dn, preferred_element_type=F32)


def _split3(x):
    hi = x.astype(BF16)
    r = x - hi.astype(F32)
    mid = r.astype(BF16)
    lo = (r - mid.astype(F32)).astype(BF16)
    return hi, mid, lo


def _xdot(x, m):
    rows = x.shape[0]
    hi, mid, _ = _split3(x)
    r = lax.dot_general(jnp.concatenate([hi, mid], axis=0), m, NN, preferred_element_type=F32)
    return r[:rows] + r[rows:]


def _xdot_l(m, x):
    cols = x.shape[1]
    hi, mid, _ = _split3(x)
    r = lax.dot_general(m, jnp.concatenate([hi, mid], axis=1), NN, preferred_element_type=F32)
    return r[:, :cols] + r[:, cols:]


def _params(n_grid, vmem_limit=VMEM_LIMIT):
    return pltpu.CompilerParams(dimension_semantics=("arbitrary",) * n_grid, vmem_limit_bytes=vmem_limit)


def _sigmoid(x):
    return jax.nn.sigmoid(x)


def _np_blocksum(n):
    i = np.arange(n)
    return (i[:, None] // HEAD_DIM == i[None, :] // HEAD_DIM).astype(np.float32)


def _np_swap32(n):
    i = np.arange(n)
    partner = np.where(i % HEAD_DIM < HEAD_DIM // 2, i + HEAD_DIM // 2, i - HEAD_DIM // 2)
    m = np.zeros((n, n), np.float32)
    m[i, partner] = 1.0
    return m


def _np_expand_q():
    m = np.zeros((512, 1024), np.float32)
    for h in range(8):
        g = h // 4
        for d in range(HEAD_DIM):
            m[64 * h + d, 128 * h + 64 * g + d] = 1.0
    return m


def _np_headsum_spread():
    m = np.zeros((512, 1024), np.float32)
    for h in range(8):
        m[64 * h:64 * h + 64, 128 * h:128 * h + 128] = 1.0
    return m


def _np_swap_halves():
    m = np.zeros((128, 128), np.float32)
    i = np.arange(128)
    m[i, (i + 64) % 128] = 1.0
    return m


def _np_hgrn_cums(t, rev):
    r = np.arange(t)[:, None]
    c = np.arange(t)[None, :]
    same = (r // CHUNK) == (c // CHUNK)
    if not rev:
        cum = same & (c <= r)
        mid = same & (c % CHUNK <= CHUNK // 2 - 1)
    else:
        cum = same & (c >= r)
        mid = same & (c % CHUNK >= CHUNK // 2)
    return np.concatenate([cum, mid, same], axis=0).astype(np.float32)


def _bf(a):
    return jnp.asarray(a, dtype=BF16)


def _rope_tables(seq_len):
    rows = seq_len // GRID_W
    row = jnp.repeat(jnp.arange(rows, dtype=F32), GRID_W)
    col = jnp.tile(jnp.arange(GRID_W, dtype=F32), rows)
    n_freq = HEAD_DIM // 4
    inv = ROPE_THETA ** (-jnp.arange(n_freq, dtype=F32) / n_freq)
    ang = jnp.concatenate([row[:, None] * inv, col[:, None] * inv], axis=-1)
    cos, sin = jnp.cos(ang), jnp.sin(ang)
    c64 = jnp.concatenate([cos, cos], axis=-1)
    s64 = jnp.concatenate([-sin, sin], axis=-1)
    return jnp.tile(c64, (1, 2)), jnp.tile(s64, (1, 2))


def _ffn_fwd(x, g, wg, wu, wd, tm):
    s, d = x.shape
    nsh, fs, _ = wg.shape

    def body(x_ref, g_ref, wg_ref, wu_ref, wd_ref, xo_ref, a_ref, da_ref, b_ref, hb_ref, acc, hs):
        j = pl.program_id(1)

        @pl.when(j == 0)
        def _():
            xv = x_ref[...]
            r = lax.rsqrt(jnp.mean(xv * xv, axis=-1, keepdims=True) + EPS)
            h = (xv * r * g_ref[...]).astype(CD)
            hs[...] = h
            hb_ref[...] = h
            acc[...] = jnp.zeros_like(acc)

        blocks = [slice(r0, min(r0 + FFN_ROWS, tm)) for r0 in range(0, tm, FFN_ROWS)]
        firsts = [(_mm(hs[rows, :], wg_ref[0], NT), _mm(hs[rows, :], wu_ref[0], NT)) for rows in blocks]
        for rows, (a, b) in zip(blocks, firsts):
            sg = _sigmoid(a)
            silu = a * sg
            acc[rows, :] += _mm(silu * b, wd_ref[0], NN)
            a_ref[0, rows, :] = silu.astype(CD)
            da_ref[0, rows, :] = (sg * (1.0 + a * (1.0 - sg))).astype(CD)
            b_ref[0, rows, :] = b.astype(CD)

        @pl.when(j == nsh - 1)
        def _():
            xo_ref[...] = x_ref[...] + 0.5 * acc[...]

    return pl.pallas_call(
        body, name="ffn_fwd", grid=(s // tm, nsh),
        in_specs=[pl.BlockSpec((tm, d), lambda i, j: (i, 0)), pl.BlockSpec((1, d), lambda i, j: (0, 0))]
        + [pl.BlockSpec((1, fs, d), lambda i, j: (j, 0, 0))] * 3,
        out_specs=[pl.BlockSpec((tm, d), lambda i, j: (i, 0))] + [pl.BlockSpec((1, tm, fs), lambda i, j: (j, i, 0))] * 3
        + [pl.BlockSpec((tm, d), lambda i, j: (i, 0))],
        out_shape=[jax.ShapeDtypeStruct((s, d), F32)] + [jax.ShapeDtypeStruct((nsh, s, fs), CD)] * 3
        + [jax.ShapeDtypeStruct((s, d), CD)],
        scratch_shapes=[pltpu.VMEM((tm, d), F32), pltpu.VMEM((tm, d), CD)],
        compiler_params=_params(2),
    )(x, g, wg, wu, wd)


def _ffn_bwd(dout, x, g, silu, dsilu, b, wg, wu, wd, tm):
    s, d = x.shape
    nsh, fs, _ = wg.shape

    def body(do_ref, x_ref, g_ref, sl_ref, ds_ref, b_ref, wg_ref, wu_ref, wd_ref,
             dx_ref, da_ref, db_ref, f_ref, dg_ref, do16_ref, dh):
        i = pl.program_id(0)
        j = pl.program_id(1)

        @pl.when(j == 0)
        def _():
            dh[...] = jnp.zeros_like(dh)
            do16_ref[...] = (0.5 * do_ref[...]).astype(CD)

        @pl.when((i == 0) & (j == 0))
        def _():
            dg_ref[...] = jnp.zeros_like(dg_ref)

        blocks = [slice(r0, min(r0 + FFN_ROWS, tm)) for r0 in range(0, tm, FFN_ROWS)]
        dfs = [_mm(do16_ref[rows, :], wd_ref[0], NT) for rows in blocks]
        das, dbs = [], []
        for rows, df in zip(blocks, dfs):
            sl = sl_ref[0, rows, :].astype(F32)
            bv = b_ref[0, rows, :].astype(F32)
            da = (df * bv * ds_ref[0, rows, :].astype(F32)).astype(CD)
            db = (df * sl).astype(CD)
            da_ref[0, rows, :] = da
            db_ref[0, rows, :] = db
            f_ref[0, rows, :] = (sl * bv).astype(CD)
            das.append(da)
            dbs.append(db)
        for rows, da, db in zip(blocks, das, dbs):
            dh[rows, :] += _mm(da, wg_ref[0], NN) + _mm(db, wu_ref[0], NN)

        @pl.when(j == nsh - 1)
        def _():
            xv = x_ref[...]
            r = lax.rsqrt(jnp.mean(xv * xv, axis=-1, keepdims=True) + EPS)
            dhv = dh[...]
            u = dhv * g_ref[...]
            dx_ref[...] = do_ref[...] + r * u - xv * (r * r * r) * jnp.mean(u * xv, axis=-1, keepdims=True)
            dg_ref[...] += jnp.sum(dhv * xv * r, axis=0, keepdims=True)

    act = pl.BlockSpec((1, tm, fs), lambda i, j: (j, i, 0))
    row = pl.BlockSpec((tm, d), lambda i, j: (i, 0))
    return pl.pallas_call(
        body, name="ffn_bwd", grid=(s // tm, nsh),
        in_specs=[row, row, pl.BlockSpec((1, d), lambda i, j: (0, 0)), act, act, act]
        + [pl.BlockSpec((1, fs, d), lambda i, j: (j, 0, 0))] * 3,
        out_specs=[row, act, act, act, pl.BlockSpec((1, d), lambda i, j: (0, 0)), row],
        out_shape=[jax.ShapeDtypeStruct((s, d), F32), jax.ShapeDtypeStruct((nsh, s, fs), CD),
                   jax.ShapeDtypeStruct((nsh, s, fs), CD), jax.ShapeDtypeStruct((nsh, s, fs), CD),
                   jax.ShapeDtypeStruct((1, d), F32), jax.ShapeDtypeStruct((s, d), CD)],
        scratch_shapes=[pltpu.VMEM((tm, d), F32)],
        compiler_params=_params(2),
    )(dout, x, g, silu, dsilu, b, wg, wu, wd)


def _tn_call(name, operands, in_specs, out_shape, out_spec, grid, acc_shape, pick, scale=1.0):
    nk = grid[-1]
    n_in = len(operands)

    def body(*refs):
        out_ref, out16_ref, acc = refs[n_in], refs[n_in + 1], refs[n_in + 2]
        k = pl.program_id(len(grid) - 1)

        @pl.when(k == 0)
        def _():
            acc[...] = jnp.zeros_like(acc)

        pick(refs[:n_in], acc)

        @pl.when(k == nk - 1)
        def _():
            res = (acc[...] if scale == 1.0 else acc[...] * scale).reshape(out_ref.shape)
            out_ref[...] = res
            out16_ref[...] = res.astype(CD)

    return pl.pallas_call(
        body, name=name, grid=grid, in_specs=in_specs, out_specs=[out_spec, out_spec],
        out_shape=[out_shape, jax.ShapeDtypeStruct(out_shape.shape, CD)],
        scratch_shapes=[pltpu.VMEM(acc_shape, F32)], compiler_params=_params(len(grid)),
    )(*operands)


def _dw_shared_b(name, a3, b, tk, scale):
    nj, s, m = a3.shape
    n = b.shape[1]

    def pick(refs, acc):
        rows = pl.ds(pl.multiple_of(pl.program_id(1) * tk, tk), tk)
        acc[...] += _mm(refs[0][0], refs[1][rows, :], TN)

    return _tn_call(name, (a3, b),
                    [pl.BlockSpec((1, tk, m), lambda j, k: (j, k, 0)), pl.BlockSpec((s, n), lambda j, k: (0, 0))],
                    jax.ShapeDtypeStruct((nj, m, n), F32), pl.BlockSpec((1, m, n), lambda j, k: (j, 0, 0)),
                    (nj, s // tk), (m, n), pick, scale)


def _dw_colblocks(name, a, b, nj, tk):
    s, m = a.shape
    n = b.shape[1] // nj
    nk = s // tk

    def body(a_ref, b_ref, out_ref, out16_ref, acc):
        k = pl.program_id(0)

        @pl.when(k == 0)
        def _():
            acc[...] = jnp.zeros_like(acc)

        acc[...] += _mm(a_ref[...], b_ref[...], TN)

        @pl.when(k == nk - 1)
        def _():
            for j in range(nj):
                res = acc[:, j * n:(j + 1) * n]
                out_ref[j] = res
                out16_ref[j] = res.astype(CD)

    whole = pl.BlockSpec((nj, m, n), lambda k: (0, 0, 0))
    return pl.pallas_call(
        body, name=name, grid=(nk,),
        in_specs=[pl.BlockSpec((tk, m), lambda k: (k, 0)), pl.BlockSpec((tk, nj * n), lambda k: (k, 0))],
        out_specs=[whole, whole],
        out_shape=[jax.ShapeDtypeStruct((nj, m, n), F32), jax.ShapeDtypeStruct((nj, m, n), CD)],
        scratch_shapes=[pltpu.VMEM((m, nj * n), F32)], compiler_params=_params(1),
    )(a, b)


DP_WIDTHS = (768, 512, 512, 512, 512, 512, 2048)
DW_IN_COLS = 512


def _dw_in(dps, hb, tk):
    s, d = hb.shape
    nk = s // tk
    blocks, row = [], 0
    for p, width in enumerate(DP_WIDTHS):
        step = width if width <= 768 else DW_IN_COLS
        for c0 in range(0, width, step):
            blocks.append((p, c0, step, row))
            row += step
    nb, max_w = len(blocks), max(b[2] for b in blocks)
    first = [min(i for i, b in enumerate(blocks) if b[0] == p) for p in range(len(DP_WIDTHS))]
    count = [sum(1 for b in blocks if b[0] == p) for p in range(len(DP_WIDTHS))]

    def body(*refs):
        dp_refs, hb_ref, out_ref, out16_ref, acc, acc16, sems = refs[:7], refs[7], refs[8], refs[9], refs[10], refs[11], refs[12]
        b, k = pl.program_id(0), pl.program_id(1)
        rows = pl.ds(pl.multiple_of(k * tk, tk), tk)

        def writes(i):
            _, _, w, r0 = blocks[i]
            slot = i % 2
            return (pltpu.make_async_copy(acc.at[slot, 0:w], out_ref.at[r0:r0 + w], sems.at[slot, 0]),
                    pltpu.make_async_copy(acc16.at[slot, 0:w], out16_ref.at[r0:r0 + w], sems.at[slot, 1]))

        for i, (p, _, w, _) in enumerate(blocks):
            @pl.when(b == i)
            def _(i=i, p=p, w=w):
                slot = i % 2
                prod = _mm(dp_refs[p][...], hb_ref[rows, :], TN)

                @pl.when(k == 0)
                def _():
                    acc[slot, 0:w] = prod

                @pl.when(k > 0)
                def _():
                    acc[slot, 0:w] += prod

                @pl.when(k == nk - 1)
                def _():
                    if i >= 1:
                        for cp in writes(i - 1):
                            cp.wait()
                    acc16[slot, 0:w] = acc[slot, 0:w].astype(CD)
                    for cp in writes(i):
                        cp.start()
                    if i == nb - 1:
                        for cp in writes(i):
                            cp.wait()

    def piece_spec(p):
        width = DP_WIDTHS[p]
        cols = width if width <= 768 else DW_IN_COLS

        def imap(b, k):
            active = (b >= first[p]) & (b < first[p] + count[p])
            return (jnp.where(active, k, jnp.where(b < first[p], 0, nk - 1)), jnp.clip(b - first[p], 0, count[p] - 1))

        return pl.BlockSpec((tk, cols), imap)

    return pl.pallas_call(
        body, name="dw_in", grid=(nb, nk),
        in_specs=[piece_spec(p) for p in range(len(DP_WIDTHS))] + [pl.BlockSpec((s, d), lambda b, k: (0, 0))],
        out_specs=[ANY, ANY],
        out_shape=[jax.ShapeDtypeStruct((sum(DP_WIDTHS), d), F32), jax.ShapeDtypeStruct((sum(DP_WIDTHS), d), CD)],
        scratch_shapes=[pltpu.VMEM((2, max_w, d), F32), pltpu.VMEM((2, max_w, d), CD), pltpu.SemaphoreType.DMA((2, 2))],
        compiler_params=_params(2),
    )(*dps, hb)


def _mix_in_fwd(x, g, w_t, tm):
    s, d = x.shape
    n_in = w_t.shape[0]

    def body(x_ref, g_ref, w_ref, qkv_ref, hg_ref, gt_ref, hb_ref):
        xv = x_ref[...]
        r = lax.rsqrt(jnp.mean(xv * xv, axis=-1, keepdims=True) + EPS)
        h = (xv * r * g_ref[...]).astype(CD)
        hb_ref[...] = h
        off = DP_WIDTHS[0]
        qkv_ref[...] = _mm(h, w_ref[0:off, :], NT)
        for c, width in enumerate(DP_WIDTHS[1:6]):
            hg_ref[:, c * width:(c + 1) * width] = _mm(h, w_ref[off:off + width, :], NT)
            off += width
        gate = DP_WIDTHS[6] // 2
        for c in range(2):
            gt_ref[:, c * gate:(c + 1) * gate] = _mm(h, w_ref[off:off + gate, :], NT).astype(CD)
            off += gate

    row = lambda w: pl.BlockSpec((tm, w), lambda i: (i, 0))
    return pl.pallas_call(
        body, name="mix_in_fwd", grid=(s // tm,),
        in_specs=[row(d), pl.BlockSpec((1, d), lambda i: (0, 0)), pl.BlockSpec((n_in, d), lambda i: (0, 0))],
        out_specs=[row(768), row(2560), row(2048), row(d)],
        out_shape=[jax.ShapeDtypeStruct((s, 768), F32), jax.ShapeDtypeStruct((s, 2560), F32),
                   jax.ShapeDtypeStruct((s, 2048), CD), jax.ShapeDtypeStruct((s, d), CD)],
        compiler_params=_params(1),
    )(x, g, w_t)


def _mix_in_bwd(dps, w_t, x, dres, g, tm, after=()):
    s, d = x.shape
    n_in = w_t.shape[0]

    def body(*refs):
        refs = refs[len(after):]
        dp_refs = refs[:7]
        w_ref, x_ref, dr_ref, g_ref, dx_ref, dg_ref = refs[7:]

        @pl.when(pl.program_id(0) == 0)
        def _():
            dg_ref[...] = jnp.zeros_like(dg_ref)

        dhv = jnp.zeros((tm, d), F32)
        off = 0
        for ref, width in zip(dp_refs, DP_WIDTHS):
            dhv = dhv + _mm(ref[...], w_ref[off:off + width, :], NN)
            off += width
        xv = x_ref[...]
        r = lax.rsqrt(jnp.mean(xv * xv, axis=-1, keepdims=True) + EPS)
        u = dhv * g_ref[...]
        dx_ref[...] = dr_ref[...] + r * u - xv * (r * r * r) * jnp.mean(u * xv, axis=-1, keepdims=True)
        dg_ref[...] += jnp.sum(dhv * xv * r, axis=0, keepdims=True)

    row = pl.BlockSpec((tm, d), lambda i: (i, 0))
    vec = pl.BlockSpec((1, d), lambda i: (0, 0))
    return pl.pallas_call(
        body, name="mix_in_bwd", grid=(s // tm,),
        in_specs=[ANY] * len(after) + [pl.BlockSpec((tm, w), lambda i: (i, 0)) for w in DP_WIDTHS]
        + [pl.BlockSpec((n_in, d), lambda i: (0, 0)), row, row, vec],
        out_specs=[row, vec],
        out_shape=[jax.ShapeDtypeStruct((s, d), F32), jax.ShapeDtypeStruct((1, d), F32)],
        compiler_params=_params(1),
    )(*after, *dps, w_t, x, dres, g)


def _headnorm_rope(x, gain, cos, sin, blocksum, swap):
    ss = _xdot(x * x, blocksum)
    r = lax.rsqrt(ss * (1.0 / HEAD_DIM) + EPS)
    y = x * r * gain
    return y * cos + _xdot(y, swap) * sin, r


def _headnorm_rope_bwd(dz, x, gain, cos, sin, blocksum, swap):
    ss = _xdot(x * x, blocksum)
    r = lax.rsqrt(ss * (1.0 / HEAD_DIM) + EPS)
    dy = dz * cos + _xdot(dz * sin, swap)
    u = dy * gain
    mean_ux = _xdot(u * x, blocksum) * (1.0 / HEAD_DIM)
    dx = r * u - x * (r * r * r) * mean_ux
    return dx, jnp.sum(dy * x * r, axis=0, keepdims=True)


def _qk_prep(pqkv, gq, gk, cos2, sin2, tm):
    s = pqkv.shape[0]
    bs512, sw512, eq, swh = _bf(_np_blocksum(512)), _bf(_np_swap32(512)), _bf(_np_expand_q()), _bf(_np_swap_halves())

    def body(q_ref, kv_ref, gq_ref, gk_ref, c_ref, s_ref, bs_ref, sw_ref, eq_ref, swh_ref, qe_ref, k_ref, v_ref, vs_ref):
        c2, s2 = c_ref[...], s_ref[...]
        c8, s8 = jnp.tile(c2, (1, 4)), jnp.tile(s2, (1, 4))
        bs, sw = bs_ref[...], sw_ref[...]
        zq, _ = _headnorm_rope(q_ref[...], gq_ref[...], c8, s8, bs, sw)
        qe_ref[...] = _mm(zq * (HEAD_DIM ** -0.5), eq_ref[...], NN).astype(CD)
        kv = kv_ref[...]
        zk, _ = _headnorm_rope(kv[:, :LANES], gk_ref[...], c2, s2, bs[:LANES, :LANES], sw[:LANES, :LANES])
        k_ref[...] = zk.astype(CD)
        v = kv[:, LANES:]
        v_ref[...] = v.astype(CD)
        vs_ref[...] = _mm(v, swh_ref[...], NN).astype(CD)

    full = lambda a: pl.BlockSpec(a.shape, lambda i: (0,) * a.ndim)
    tab = pl.BlockSpec((tm, LANES), lambda i: (i, 0))
    return pl.pallas_call(
        body, name="qk_prep", grid=(s // tm,),
        in_specs=[pl.BlockSpec((tm, 512), lambda i: (i, 0)), pl.BlockSpec((tm, 256), lambda i: (i, 2)),
                  full(gq), full(gk), tab, tab, full(bs512), full(sw512), full(eq), full(swh)],
        out_specs=[pl.BlockSpec((tm, 1024), lambda i: (i, 0)), tab, tab, tab],
        out_shape=[jax.ShapeDtypeStruct((s, 1024), CD)] + [jax.ShapeDtypeStruct((s, LANES), CD)] * 3,
        compiler_params=_params(1),
    )(pqkv, pqkv, gq, gk, cos2, sin2, bs512, sw512, eq, swh)


def _qk_prep_bwd(pqkv, dq, dk, dv, gq, gk, cos2, sin2, tm):
    s = pqkv.shape[0]
    bs512, sw512 = _bf(_np_blocksum(512)), _bf(_np_swap32(512))

    def body(q_ref, kv_ref, dq_ref, dk_ref, dv_ref, gq_ref, gk_ref, c_ref, s_ref, bs_ref, sw_ref,
             dp_ref, dgq_ref, dgk_ref):
        @pl.when(pl.program_id(0) == 0)
        def _():
            dgq_ref[...] = jnp.zeros_like(dgq_ref)
            dgk_ref[...] = jnp.zeros_like(dgk_ref)

        c2, s2 = c_ref[...], s_ref[...]
        c8, s8 = jnp.tile(c2, (1, 4)), jnp.tile(s2, (1, 4))
        bs, sw = bs_ref[...], sw_ref[...]
        dzq = dq_ref[...] * (HEAD_DIM ** -0.5)
        dxq, dgq = _headnorm_rope_bwd(dzq, q_ref[...], gq_ref[...], c8, s8, bs, sw)
        kv = kv_ref[...]
        dxk, dgk = _headnorm_rope_bwd(dk_ref[...], kv[:, :LANES], gk_ref[...], c2, s2, bs[:LANES, :LANES], sw[:LANES, :LANES])
        dp_ref[...] = jnp.concatenate([dxq, dxk, dv_ref[...]], axis=1).astype(CD)
        dgq_ref[...] += dgq
        dgk_ref[...] += dgk

    full = lambda a: pl.BlockSpec(a.shape, lambda i: (0,) * a.ndim)
    tab = pl.BlockSpec((tm, LANES), lambda i: (i, 0))
    return pl.pallas_call(
        body, name="qk_prep_bwd", grid=(s // tm,),
        in_specs=[pl.BlockSpec((tm, 512), lambda i: (i, 0)), pl.BlockSpec((tm, 256), lambda i: (i, 2)),
                  pl.BlockSpec((tm, 512), lambda i: (i, 0)), tab, tab, full(gq), full(gk), tab, tab,
                  full(bs512), full(sw512)],
        out_specs=[pl.BlockSpec((tm, 768), lambda i: (i, 0)), pl.BlockSpec((1, 512), lambda i: (0, 0)),
                   pl.BlockSpec((1, LANES), lambda i: (0, 0))],
        out_shape=[jax.ShapeDtypeStruct((s, 768), CD), jax.ShapeDtypeStruct((1, 512), F32),
                   jax.ShapeDtypeStruct((1, LANES), F32)],
        compiler_params=_params(1),
    )(pqkv, pqkv, dq, dk, dv, gq, gk, cos2, sin2, bs512, sw512)


def _kv_rows(h):
    return pl.ds(pl.multiple_of((h // 4) * HEAD_DIM, HEAD_DIM), HEAD_DIM)


def _attn_fwd(qe, k, v, vs, tq):
    s = k.shape[0]

    def body(q0_ref, q1_ref, q2_ref, q3_ref, k_ref, v_ref, vs_ref, o_ref, lse_ref):
        grp = pl.program_id(0)
        kk = k_ref[...]
        heads = range(4)
        scores = [_mm(q_ref[...], kk, NT) for q_ref in (q0_ref, q1_ref, q2_ref, q3_ref)]
        mxs = [jnp.max(sc, axis=-1, keepdims=True) for sc in scores]
        es = [jnp.exp(scores[r] - mxs[r]) for r in heads]
        ls = [jnp.sum(e, axis=-1, keepdims=True) for e in es]
        for r in heads:
            lse_ref[r] = mxs[r] + jnp.log(ls[r])
        outs = [_mm(es[r], jnp.where(grp != r % 2, vs_ref[...], v_ref[...]), NN) * (1.0 / ls[r]) for r in heads]
        low = lax.broadcasted_iota(jnp.int32, (1, LANES), 1) < HEAD_DIM
        o_ref[...] = jnp.concatenate([jnp.where(low, outs[0], outs[1]), jnp.where(low, outs[2], outs[3])], axis=1)

    kv = pl.BlockSpec((s, LANES), lambda g, i: (0, 0))
    qblk = lambda r: pl.BlockSpec((tq, LANES), lambda g, i: (i, 4 * g + r))
    return pl.pallas_call(
        body, name="attn_fwd", grid=(2, s // tq),
        in_specs=[qblk(0), qblk(1), qblk(2), qblk(3), kv, kv, kv],
        out_specs=[pl.BlockSpec((tq, 2 * LANES), lambda g, i: (i, g)), pl.BlockSpec((4, tq, 1), lambda g, i: (g, i, 0))],
        out_shape=[jax.ShapeDtypeStruct((s, 512), F32), jax.ShapeDtypeStruct((8, s, 1), F32)],
        compiler_params=_params(2),
    )(qe, qe, qe, qe, k, v, vs)


def _attn_bwd(qe, k, kt, v, doe, delta, lse, tq):
    s = k.shape[0]

    nh = ATTN_BWD_HEADS

    def body(*refs):
        q_refs, (k_ref, kt_ref, v_ref) = refs[:nh], refs[nh:nh + 3]
        do_refs, dl_refs = refs[nh + 3:2 * nh + 3], refs[2 * nh + 3:3 * nh + 3]
        lse_ref, dqt_ref, dkt_ref, dvt_ref, qt, dot = refs[3 * nh + 3:]

        @pl.when((pl.program_id(0) == 0) & (pl.program_id(1) == 0))
        def _():
            dkt_ref[...] = jnp.zeros_like(dkt_ref)
            dvt_ref[...] = jnp.zeros_like(dvt_ref)

        rows = _kv_rows(nh * pl.program_id(0))
        kt = kt_ref[rows, :]
        dkt = jnp.zeros((HEAD_DIM, s), F32)
        dvt = jnp.zeros((HEAD_DIM, s), F32)
        firsts = [(_mm(q_ref[...], k_ref[...], NT), _mm(do_ref[...], v_ref[...], NT))
                  for q_ref, do_ref in zip(q_refs, do_refs)]
        for idx, (q_ref, do_ref, dl_ref) in enumerate(zip(q_refs, do_refs, dl_refs)):
            q, do = q_ref[...], do_ref[...]
            sc, dp = firsts[idx]
            p = jnp.exp(sc - lse_ref[idx])
            ds = p * (dp - jnp.max(dl_ref[...], axis=-1, keepdims=True))
            dqt_ref[idx * HEAD_DIM:(idx + 1) * HEAD_DIM, :] = _mm(kt, ds, NT)
            qt[idx] = jnp.transpose(q.astype(F32))
            dot[idx] = jnp.transpose(do.astype(F32))
            dkt = dkt + _mm(qt[idx, rows, :], ds, NN)
            dvt = dvt + _mm(dot[idx, rows, :], p, NN)
        dkt_ref[rows, :] += dkt
        dvt_ref[rows, :] += dvt

    kv = pl.BlockSpec((s, LANES), lambda m, i: (0, 0))
    kvt = pl.BlockSpec((LANES, s), lambda m, i: (0, 0))
    blks = [pl.BlockSpec((tq, LANES), lambda m, i, r=r: (i, nh * m + r)) for r in range(nh)]
    return pl.pallas_call(
        body, name="attn_bwd", grid=(8 // nh, s // tq),
        in_specs=blks + [kv, kvt, kv] + blks + blks + [pl.BlockSpec((nh, tq, 1), lambda m, i: (m, i, 0))],
        out_specs=[pl.BlockSpec((nh * HEAD_DIM, tq), lambda m, i: (m, i)), kvt, kvt],
        out_shape=[jax.ShapeDtypeStruct((8 * HEAD_DIM, s), F32), jax.ShapeDtypeStruct((LANES, s), F32),
                   jax.ShapeDtypeStruct((LANES, s), F32)],
        scratch_shapes=[pltpu.VMEM((nh, LANES, tq), F32), pltpu.VMEM((nh, LANES, tq), F32)],
        compiler_params=_params(2),
    )(*[qe] * nh, k, kt, v, *[doe] * nh, *[delta] * nh, lse)


@jax.custom_vjp
def _mm_nn(a, b):
    return _mm(a, b, NN)


_mm_nn.defvjp(lambda a, b: (_mm(a, b, NN), (a, b)),
              lambda res, g: (_mm(g, res[1], NT), _mm(res[0], g, TN)))


@jax.custom_vjp
def _mm_nt(a, b):
    return _mm(a, b, NT)


_mm_nt.defvjp(lambda a, b: (_mm(a, b, NT), (a, b)),
              lambda res, g: (_mm(g, res[1], NN), _mm(g, res[0], TN)))


@jax.custom_vjp
def _mm_tn(a, b):
    return _mm(a, b, TN)


_mm_tn.defvjp(lambda a, b: (_mm(a, b, TN), (a, b)),
              lambda res, g: (_mm(res[1], g, NT), _mm(res[0], g, NN)))


@jax.custom_vjp
def _cmm(m, mt, x):
    return _xdot_l(m, x)


_cmm.defvjp(lambda m, mt, x: (_xdot_l(m, x), (m, mt)),
            lambda res, g: (jnp.zeros_like(res[0]), jnp.zeros_like(res[1]), _xdot_l(res[1], g)))


def _hgrn_masks(t, rev):
    n_ch = t // CHUNK
    r = jnp.bitwise_and(lax.broadcasted_iota(jnp.int32, (2 * t, t), 0), t - 1)
    c = lax.broadcasted_iota(jnp.int32, (2 * t, t), 1)
    same = jnp.right_shift(r, 5) == jnp.right_shift(c, 5)
    tri2 = same & ((c >= r) if rev else (c <= r))
    pr = lax.broadcasted_iota(jnp.int32, (LANES, LANES), 0)
    pc = lax.broadcasted_iota(jnp.int32, (LANES, LANES), 1)
    diag = jnp.right_shift(pr, 6) == jnp.right_shift(pc, 6)
    qr = lax.broadcasted_iota(jnp.int32, (t, n_ch * LANES), 0)
    qc = lax.broadcasted_iota(jnp.int32, (t, n_ch * LANES), 1)
    rows_chunk = jnp.right_shift(qc, 7) == jnp.right_shift(qr, 5)
    vr = lax.broadcasted_iota(jnp.int32, (n_ch * LANES, t), 0)
    vc = lax.broadcasted_iota(jnp.int32, (n_ch * LANES, t), 1)
    cols_chunk = jnp.right_shift(vr, 7) == jnp.right_shift(vc, 5)
    return dict(tri2=tri2, diag=diag, rows_chunk=rows_chunk, cols_chunk=cols_chunk)


def _hgrn_gates(xf, lb):
    f = lb + (1.0 - lb) * _sigmoid(xf)
    return 1.0 - f, jnp.log(f)


def _hgrn_dir(*args):
    return _hgrn_dirs([args])[0]


def _hgrn_dirs(arg_sets):
    chains = [_hgrn_phases(*a) for a in arg_sets]
    results = [None] * len(chains)
    while any(r is None for r in results):
        for n, chain in enumerate(chains):
            if results[n] is None:
                try:
                    next(chain)
                except StopIteration as done:
                    results[n] = done.value
    return results


def _hgrn_phases(xq, xf, v, lb, state, cm, cmt, mk, rev):
    t = xq.shape[0]
    n_ch = t // CHUNK
    lo = lax.broadcasted_iota(jnp.int32, (1, LANES), 1) < HEAD_DIM
    k, lf = _hgrn_gates(xf, lb)
    cs = _cmm(cm, cmt, lf)
    yield
    q = xq * _sigmoid(xq)
    b, bm, bl = cs[:t], cs[t:2 * t], cs[2 * t:]
    qd = q * jnp.exp(b - bm)
    kd = k * jnp.exp(bm - b)
    yield
    qd2 = jnp.concatenate([jnp.where(lo, qd, 0.0), jnp.where(lo, 0.0, qd)], axis=0)
    scores = _mm_nt(qd2, kd)
    yield
    kc = k * jnp.exp(bl - b)
    qe = q * jnp.exp(b)
    vexp = jnp.where(mk["cols_chunk"], jnp.concatenate([jnp.transpose(v)] * n_ch, axis=0), 0.0)
    adds = _mm_nn(vexp, kc)
    yield
    o2 = _mm_nn(jnp.where(mk["tri2"], scores, 0.0), v)
    o = jnp.where(lo, o2[:t], o2[t:])
    yield
    dec = jnp.exp(bl)
    entering = [None] * n_ch
    for c in (range(n_ch - 1, -1, -1) if rev else range(n_ch)):
        entering[c] = state
        d = jnp.concatenate([dec[c * CHUNK:(c + 1) * CHUNK]] * (LANES // CHUNK), axis=0)
        state = d * state + jnp.where(mk["diag"], adds[c * LANES:(c + 1) * LANES], 0.0)
    yield
    qexp = jnp.where(mk["rows_chunk"], jnp.concatenate([qe] * n_ch, axis=1), 0.0)
    return o + _mm_nt(qexp, jnp.concatenate(entering, axis=1)), state


def _hgrn_lower_bounds(l):
    out = []
    for d in (0, 1):
        l0, l1 = l[2 * d:2 * d + 1, :], l[2 * d + 1:2 * d + 2, :]
        mx = jnp.maximum(l0, l1)
        e0, e1 = jnp.exp(l0 - mx), jnp.exp(l1 - mx)
        out.append(e0 / (e0 + e1))
    return out


def _hgrn_consts(t):
    cf, cb = _np_hgrn_cums(t, False), _np_hgrn_cums(t, True)
    return (_bf(cf), _bf(cf.T), _bf(cb), _bf(cb.T), _bf(_np_blocksum(LANES)))


def _hgrn_fwd(ph, lbl, ng):
    s = ph.shape[0]
    t = min(HG_TILE, s)
    nt = s // t
    consts = _hgrn_consts(t)

    def body(xq_ref, xff_ref, xfb_ref, xi_ref, xg_ref, lbl_ref, ng_ref, cf_ref, cft_ref, cb_ref, cbt_ref, bs_ref,
             o_ref, pre_ref, st_ref, acc):
        lbf, lbb = _hgrn_lower_bounds(lbl_ref)
        mk_f, mk_b = _hgrn_masks(t, False), _hgrn_masks(t, True)
        zero = jnp.zeros((LANES, LANES), F32)

        def rows_of(i):
            return pl.ds(pl.multiple_of(i * t, t), t)

        acc[...] = jnp.zeros_like(acc)

        def step(i, states):
            tb = nt - 1 - i
            rf, rb = rows_of(i), rows_of(tb)
            st_ref[0, 0, i] = states[0]
            st_ref[0, 1, tb] = states[1]
            (of, sf), (ob, sb) = _hgrn_dirs([
                (xq_ref[rf, :], xff_ref[rf, :], xi_ref[rf, :], lbf, states[0], cf_ref[...], cft_ref[...], mk_f, False),
                (xq_ref[rb, :], xfb_ref[rb, :], xi_ref[rb, :], lbb, states[1], cb_ref[...], cbt_ref[...], mk_b, True)])
            acc[rf, :] += of
            acc[rb, :] += ob
            return sf, sb

        lax.fori_loop(0, nt, step, (zero, zero))

        def step_n(i, carry):
            rows = rows_of(i)
            o = acc[rows, :]
            ss = _xdot(o * o, bs_ref[...])
            r = lax.rsqrt(ss * (1.0 / HEAD_DIM) + EPS)
            xg = xg_ref[rows, :]
            pre_ref[rows, :] = o
            o_ref[rows, :] = ((o * r * ng_ref[...]) * (xg * _sigmoid(xg))).astype(CD)
            return carry

        lax.fori_loop(0, nt, step_n, 0)

    col = lambda off: pl.BlockSpec((s, LANES), lambda m: (0, off + m))
    full = lambda a: pl.BlockSpec(a.shape, lambda m: (0,) * a.ndim)
    return pl.pallas_call(
        body, name="hgrn_fwd", grid=(4,),
        in_specs=[col(0), col(4), col(8), col(12), col(16), pl.BlockSpec((4, LANES), lambda m: (0, m)),
                  pl.BlockSpec((1, LANES), lambda m: (0, m))] + [full(c) for c in consts],
        out_specs=[col(0), col(0), pl.BlockSpec((1, 2, nt, LANES, LANES), lambda m: (m, 0, 0, 0, 0))],
        out_shape=[jax.ShapeDtypeStruct((s, 512), CD), jax.ShapeDtypeStruct((s, 512), F32),
                   jax.ShapeDtypeStruct((4, 2, nt, LANES, LANES), F32)],
        scratch_shapes=[pltpu.VMEM((s, LANES), F32)],
        compiler_params=_params(1),
    )(ph, ph, ph, ph, ph, lbl, ng, *consts)


def _hgrn_bwd(ph, pre, dout, states, lbl, ng):
    s = ph.shape[0]
    t = min(HG_TILE, s)
    nt = s // t
    consts = _hgrn_consts(t)

    def body(xq_ref, xff_ref, xfb_ref, xi_ref, xg_ref, pre_ref, do_ref, st_ref, lbl_ref, ng_ref,
             cf_ref, cft_ref, cb_ref, cbt_ref, bs_ref,
             dq_ref, dff_ref, dfb_ref, di_ref, dg_ref, dlb_ref, dng_ref, dpre, dq_acc, dv_acc):
        lbf, lbb = _hgrn_lower_bounds(lbl_ref)
        mk_f, mk_b = _hgrn_masks(t, False), _hgrn_masks(t, True)
        zero = jnp.zeros((LANES, LANES), F32)
        zrow = jnp.zeros((1, LANES), F32)

        def rows_of(i):
            return pl.ds(pl.multiple_of(i * t, t), t)

        def step_n(i, dng):
            rows = rows_of(i)
            o, xg, do = pre_ref[rows, :], xg_ref[rows, :], do_ref[rows, :]
            bs = bs_ref[...]
            r = lax.rsqrt(_xdot(o * o, bs) * (1.0 / HEAD_DIM) + EPS)
            sg = _sigmoid(xg)
            gate = xg * sg
            don = do * gate
            dg_ref[rows, :] = (do * (o * r * ng_ref[...]) * (sg * (1.0 + xg * (1.0 - sg)))).astype(CD)
            u = don * ng_ref[...]
            dpre[rows, :] = r * u - o * (r * r * r) * (_xdot(u * o, bs) * (1.0 / HEAD_DIM))
            return dng + jnp.sum(don * o * r, axis=0, keepdims=True)

        dng_ref[...] = lax.fori_loop(0, nt, step_n, zrow)

        dq_acc[...] = jnp.zeros_like(dq_acc)
        dv_acc[...] = jnp.zeros_like(dv_acc)

        def step_g(i, carry):
            dsf, dsb, dlbf, dlbb = carry
            tf, tb = nt - 1 - i, i
            rf, rb = rows_of(tf), rows_of(tb)
            cf, cft, cb, cbt = cf_ref[...], cft_ref[...], cb_ref[...], cbt_ref[...]

            def both(xq_f, xf_f, v_f, lb_f, s_f, xq_b, xf_b, v_b, lb_b, s_b):
                (of, sf), (ob, sb) = _hgrn_dirs([(xq_f, xf_f, v_f, lb_f, s_f, cf, cft, mk_f, False),
                                                 (xq_b, xf_b, v_b, lb_b, s_b, cb, cbt, mk_b, True)])
                return of, sf, ob, sb

            _, vjp = jax.vjp(both, xq_ref[rf, :], xff_ref[rf, :], xi_ref[rf, :], lbf, st_ref[0, 0, tf],
                             xq_ref[rb, :], xfb_ref[rb, :], xi_ref[rb, :], lbb, st_ref[0, 1, tb])
            dq_f, dx_f, dv_f, gf, dsf, dq_b, dx_b, dv_b, gb, dsb = vjp((dpre[rf, :], dsf, dpre[rb, :], dsb))
            dff_ref[rf, :] = dx_f.astype(CD)
            dfb_ref[rb, :] = dx_b.astype(CD)
            dq_acc[rf, :] += dq_f
            dv_acc[rf, :] += dv_f
            dq_acc[rb, :] += dq_b
            dv_acc[rb, :] += dv_b
            return dsf, dsb, dlbf + gf, dlbb + gb

        _, _, dlbf, dlbb = lax.fori_loop(0, nt, step_g, (zero, zero, zrow, zrow))
        dlb_ref[0:1, :] = dlbf
        dlb_ref[1:2, :] = dlbb
        dq_ref[...] = dq_acc[...].astype(CD)
        di_ref[...] = dv_acc[...].astype(CD)

    col = lambda off: pl.BlockSpec((s, LANES), lambda m: (0, off + m))
    full = lambda a: pl.BlockSpec(a.shape, lambda m: (0,) * a.ndim)
    stream = jax.ShapeDtypeStruct((s, 512), CD)
    return pl.pallas_call(
        body, name="hgrn_bwd", grid=(4,),
        in_specs=[col(0), col(4), col(8), col(12), col(16), col(0), col(0),
                  pl.BlockSpec((1, 2, nt, LANES, LANES), lambda m: (m, 0, 0, 0, 0)),
                  pl.BlockSpec((4, LANES), lambda m: (0, m)),
                  pl.BlockSpec((1, LANES), lambda m: (0, m))] + [full(c) for c in consts],
        out_specs=[col(0)] * 5 + [pl.BlockSpec((2, LANES), lambda m: (0, m)), pl.BlockSpec((1, LANES), lambda m: (0, m))],
        out_shape=[stream] * 5 + [jax.ShapeDtypeStruct((2, 512), F32), jax.ShapeDtypeStruct((1, 512), F32)],
        scratch_shapes=[pltpu.VMEM((s, LANES), F32), pltpu.VMEM((s, LANES), F32), pltpu.VMEM((s, LANES), F32)],
        compiler_params=_params(1, HGRN_BWD_VMEM),
    )(ph, ph, ph, ph, ph, pre, dout, states, lbl, ng, *consts)


def _branch_out(o, w4):
    o = o.astype(CD)
    return jnp.concatenate([_mm(o, w4[j], NN) for j in range(N_SHARD)], axis=1)


def _mix_out_fwd(x, oa, ob, pg, wa, wb, wo, tm):
    s, d = x.shape

    def body(x_ref, oa_ref, ob_ref, ga_ref, gb_ref, wa_ref, wb_ref, wo_ref, xo_ref):
        ya = _branch_out(oa_ref[...], wa_ref)
        yb = _branch_out(ob_ref[...], wb_ref)
        merged = _sigmoid(ga_ref[...].astype(F32)) * ya + _sigmoid(gb_ref[...].astype(F32)) * yb
        xo_ref[...] = x_ref[...] + _mm(merged, wo_ref[...], NN)

    row = pl.BlockSpec((tm, d), lambda i: (i, 0))
    half = pl.BlockSpec((tm, 512), lambda i: (i, 0))
    full = lambda a: pl.BlockSpec(a.shape, lambda i: (0,) * a.ndim)
    return pl.pallas_call(
        body, name="mix_out_fwd", grid=(s // tm,),
        in_specs=[row, half, half, row, pl.BlockSpec((tm, d), lambda i: (i, 1)), full(wa), full(wb), full(wo)],
        out_specs=row, out_shape=jax.ShapeDtypeStruct((s, d), F32),
        compiler_params=_params(1),
    )(x, oa, ob, pg, pg, wa, wb, wo)


def _mix_out_bwd(dx, oa, ob, pg, wa, wb, wo, tm, after=()):
    s, d = dx.shape
    eq, ebc = _bf(_np_expand_q()), _bf(_np_headsum_spread())

    def body(*refs):
        (dx_ref, oa_ref, ob_ref, ga_ref, gb_ref, wa_ref, wb_ref, wo_ref, eq_ref, ebc_ref,
         dpg_ref, mg_ref, dya_ref, dyb_ref, doe_ref, dl_ref, dob_ref) = refs[len(after):]
        oa = oa_ref[...]
        ya = _branch_out(oa, wa_ref)
        yb = _branch_out(ob_ref[...], wb_ref)
        sa, sb = _sigmoid(ga_ref[...].astype(F32)), _sigmoid(gb_ref[...].astype(F32))
        mg_ref[...] = (sa * ya + sb * yb).astype(CD)
        dm = _mm(dx_ref[...], wo_ref[...], NT)
        dpg_ref[...] = jnp.concatenate([dm * ya * sa * (1.0 - sa), dm * yb * sb * (1.0 - sb)], axis=1).astype(CD)
        dya, dyb = dm * sa, dm * sb
        dya_ref[...] = dya.astype(CD)
        dyb_ref[...] = dyb.astype(CD)
        doa = jnp.zeros(oa.shape, F32)
        dob = jnp.zeros(oa.shape, F32)
        for j in range(N_SHARD):
            doa = doa + _mm(dya[:, 256 * j:256 * j + 256], wa_ref[j], NT)
            dob = dob + _mm(dyb[:, 256 * j:256 * j + 256], wb_ref[j], NT)
        dob_ref[...] = dob
        doe_ref[...] = _mm(doa, eq_ref[...], NN).astype(CD)
        dl_ref[...] = _xdot(doa * oa, ebc_ref[...])

    row = pl.BlockSpec((tm, d), lambda i: (i, 0))
    half = pl.BlockSpec((tm, 512), lambda i: (i, 0))
    full = lambda a: pl.BlockSpec(a.shape, lambda i: (0,) * a.ndim)
    wide = jax.ShapeDtypeStruct((s, d), CD)
    return pl.pallas_call(
        body, name="mix_out_bwd", grid=(s // tm,),
        in_specs=[ANY] * len(after) + [row, half, half, row, pl.BlockSpec((tm, d), lambda i: (i, 1)), full(wa), full(wb),
                                       full(wo), full(eq), full(ebc)],
        out_specs=[pl.BlockSpec((tm, 2048), lambda i: (i, 0)), row, row, row, row, row, half],
        out_shape=[jax.ShapeDtypeStruct((s, 2048), CD), wide, wide, wide, wide, jax.ShapeDtypeStruct((s, d), F32),
                   jax.ShapeDtypeStruct((s, 512), F32)],
        compiler_params=_params(1),
    )(*after, dx, oa, ob, pg, pg, wa, wb, wo, eq, ebc)


def _loss_head(x, g, target, tm):
    s, d = x.shape

    def body(x_ref, g_ref, t_ref, dx_ref, loss_ref, dg_ref):
        @pl.when(pl.program_id(0) == 0)
        def _():
            loss_ref[...] = jnp.zeros_like(loss_ref)
            dg_ref[...] = jnp.zeros_like(dg_ref)

        xv = x_ref[...]
        r = lax.rsqrt(jnp.mean(xv * xv, axis=-1, keepdims=True) + EPS)
        err = xv * r * g_ref[...] - t_ref[...]
        loss_ref[...] += 0.5 * jnp.sum(jnp.mean(err * err, axis=-1, keepdims=True))
        dy = err * (1.0 / d)
        u = dy * g_ref[...]
        dx_ref[...] = r * u - xv * (r * r * r) * jnp.mean(u * xv, axis=-1, keepdims=True)
        dg_ref[...] += jnp.sum(dy * xv * r, axis=0, keepdims=True)

    row = pl.BlockSpec((tm, d), lambda i: (i, 0))
    vec = pl.BlockSpec((1, d), lambda i: (0, 0))
    return pl.pallas_call(
        body, name="loss_head", grid=(s // tm,),
        in_specs=[row, vec, row], out_specs=[row, pl.BlockSpec((8, LANES), lambda i: (0, 0)), vec],
        out_shape=[jax.ShapeDtypeStruct((s, d), F32), jax.ShapeDtypeStruct((8, LANES), F32),
                   jax.ShapeDtypeStruct((1, d), F32)],
        compiler_params=_params(1),
    )(x, g, target)


def _position():
    x, y, c = lax.axis_index("x"), lax.axis_index("y"), lax.axis_index("c")
    return x, y, c, [(1 - x, y), (x, 1 - y), (1 - x, 1 - y)]


def _row_tile(rows, cap=256):
    best = rows
    for cand in range(8, min(rows, cap) + 1, 8):
        if rows % cand == 0:
            best = cand
    return best


def _cast_into_slots(shards, dtypes, me_idx):
    n = len(shards)
    tiles = [_row_tile(s.shape[0]) for s in shards]
    counts = [s.shape[0] // t for s, t in zip(shards, tiles)]
    starts = [sum(counts[:a]) for a in range(n)]

    def body(me_ref, *refs):
        i = pl.program_id(0)
        for a in range(n):
            @pl.when((i >= starts[a]) & (i < starts[a] + counts[a]))
            def _(a=a):
                refs[n + a][0] = refs[a][...].astype(dtypes[a])

    tile_of = [lambda i, a=a: jnp.clip(i - starts[a], 0, counts[a] - 1) for a in range(n)]
    return pl.pallas_call(
        body, name="cast_into_slots",
        grid_spec=pltpu.PrefetchScalarGridSpec(
            num_scalar_prefetch=1, grid=(sum(counts),),
            in_specs=[pl.BlockSpec((tiles[a], shards[a].shape[1]), lambda i, me, a=a: (tile_of[a](i), 0)) for a in range(n)],
            out_specs=[pl.BlockSpec((1, tiles[a], shards[a].shape[1]), lambda i, me, a=a: (me[0], tile_of[a](i), 0))
                       for a in range(n)]),
        out_shape=[jax.ShapeDtypeStruct((N_SHARD,) + s.shape, dt) for s, dt in zip(shards, dtypes)],
        compiler_params=_params(1),
    )(me_idx, *shards)


HBM_SPEC = pl.BlockSpec(memory_space=pltpu.HBM)
SEM_SPEC = pl.BlockSpec(memory_space=pltpu.SEMAPHORE)
DATAFLOW = pltpu.SideEffectType.DATAFLOW_SIDE_EFFECTING


def _exchange_copies(srcs, lands, send, recv, gather):
    x, y, c, chips = _position()
    me = 2 * x + y
    out = []
    for a in range(len(lands)):
        dst = lands[a].at[me]
        if gather and _halved(lands[a]):
            half = lands[a].shape[1] // 2
            dst = lands[a].at[me, pl.ds(c * half, half), :]
        for k, (px, py) in enumerate(chips):
            src = dst if gather else srcs[a].at[2 * px + py]
            out.append(pltpu.make_async_remote_copy(src_ref=src, dst_ref=dst, send_sem=send.at[3 * a + k],
                                                    recv_sem=recv.at[3 * a + k], device_id=(px, py, c), device_id_type=MESH))
    return out


def _halved(land):
    return land.shape[1] % 32 == 0


def _pair_fill(name, lands):
    n = len(lands)

    def body(*refs):
        src, dst = refs[:n], refs[n:2 * n]
        send, recv = refs[2 * n:]
        x, y, c, chips = _position()
        copies = []
        for a in range(n):
            half = src[a].shape[1] // 2
            for k, (px, py) in enumerate(chips):
                rows = (2 * px + py, pl.ds(c * half, half), slice(None))
                cp = pltpu.make_async_remote_copy(src_ref=src[a].at[rows], dst_ref=dst[a].at[rows], send_sem=send.at[a, k],
                                                  recv_sem=recv.at[a, k], device_id=(x, y, 1 - c), device_id_type=MESH)
                cp.start()
                copies.append(cp)
        for cp in copies:
            cp.wait()

    return pl.pallas_call(
        body, name=name, in_specs=[ANY] * n, out_specs=[ANY] * n,
        out_shape=[jax.ShapeDtypeStruct(l.shape, l.dtype) for l in lands],
        input_output_aliases={a: a for a in range(n)},
        scratch_shapes=[pltpu.SemaphoreType.DMA((n, 3)), pltpu.SemaphoreType.DMA((n, 3))],
    )(*lands)


def _exchange_start(name, srcs, lands, after):
    ns, nl, na = len(srcs), len(lands), len(after)
    gather = ns == 0

    def body(*refs):
        src_refs, land_refs = refs[:ns], refs[ns:ns + nl]
        send, recv = refs[ns + nl + na], refs[ns + nl + na + 1]
        token = refs[-1]
        for cp in _exchange_copies(src_refs, land_refs, send, recv, gather):
            cp.start()
        token[...] = jnp.zeros_like(token)

    arrays = [pltpu.with_memory_space_constraint(a, pltpu.HBM) for a in list(srcs) + list(lands)]
    outs = pl.pallas_call(
        body, name=name,
        out_shape=(pltpu.SemaphoreType.DMA((3 * nl,)), pltpu.SemaphoreType.DMA((3 * nl,)),
                   *[pltpu.HBM(a.shape, a.dtype) for a in arrays], jax.ShapeDtypeStruct((8, LANES), F32)),
        in_specs=[HBM_SPEC] * (ns + nl) + [ANY] * na,
        out_specs=(SEM_SPEC, SEM_SPEC, *[HBM_SPEC] * (ns + nl), pl.BlockSpec(memory_space=pltpu.VMEM)),
        input_output_aliases={i: 2 + i for i in range(ns + nl)},
        compiler_params=pltpu.CompilerParams(has_side_effects=DATAFLOW),
    )(*arrays, *after)
    return outs[0], outs[1], list(outs[2:2 + ns]), list(outs[2 + ns:2 + ns + nl]), outs[-1]


def _exchange_wait(name, send, recv, srcs, lands, after):
    ns, nl, na = len(srcs), len(lands), len(after)
    gather = ns == 0

    def body(*refs):
        src_refs, land_refs = refs[:ns], refs[ns:ns + nl]
        send_ref, recv_ref = refs[ns + nl], refs[ns + nl + 1]
        for cp in _exchange_copies(src_refs, land_refs, send_ref, recv_ref, gather):
            cp.wait_send()
            cp.wait_recv()

    outs = pl.pallas_call(
        body, name=name,
        out_shape=tuple(pltpu.HBM(a.shape, a.dtype) for a in list(srcs) + list(lands)),
        in_specs=[HBM_SPEC] * (ns + nl) + [SEM_SPEC, SEM_SPEC] + [ANY] * na,
        out_specs=tuple([HBM_SPEC] * (ns + nl)),
        input_output_aliases={i: i for i in range(ns + nl)},
        compiler_params=pltpu.CompilerParams(has_side_effects=DATAFLOW),
    )(*srcs, *lands, send, recv, *after)
    return list(outs[ns:])


def _pair_exchange(grads):
    n = len(grads)

    def body(*refs):
        src, dst = refs[:n], refs[n:2 * n]
        send, recv = refs[2 * n:]
        x, y, c, _ = _position()
        copies = []
        for a in range(n):
            half = src[a].shape[1] // 2
            cp = pltpu.make_async_remote_copy(
                src_ref=src[a].at[:, pl.ds((1 - c) * half, half), :], dst_ref=dst[a], send_sem=send.at[a],
                recv_sem=recv.at[a], device_id=(x, y, 1 - c), device_id_type=MESH)
            cp.start()
            copies.append(cp)
        for cp in copies:
            cp.wait()

    return pl.pallas_call(
        body, name="grad_pair_exchange", in_specs=[ANY] * n, out_specs=[ANY] * n,
        out_shape=[jax.ShapeDtypeStruct((g.shape[0], g.shape[1] // 2, g.shape[2]), g.dtype) for g in grads],
        scratch_shapes=[pltpu.SemaphoreType.DMA((n,)), pltpu.SemaphoreType.DMA((n,))],
    )(*grads)


def _shard_of(a):
    return lambda i: jnp.clip(i - a * N_SHARD, 0, N_SHARD - 1)


def _pair_sum(gs, gots, c_idx, me_idx):
    n = len(gs)
    halves = [(g.shape[1] // 2, g.shape[2]) for g in gs]

    def body(c_ref, me_ref, *refs):
        g_refs, got_refs, s_refs, own_refs = (refs[k * n:(k + 1) * n] for k in range(4))
        i = pl.program_id(0)
        for a in range(n):
            @pl.when(i // N_SHARD == a)
            def _(a=a):
                sm = g_refs[a][...] + got_refs[a][...].astype(F32)
                s_refs[a][...] = sm.astype(CD)

                @pl.when(i % N_SHARD == me_ref[0])
                def _():
                    own_refs[a][...] = sm[0]

    shard = [_shard_of(a) for a in range(n)]
    return pl.pallas_call(
        body, name="grad_pair_sum",
        grid_spec=pltpu.PrefetchScalarGridSpec(
            num_scalar_prefetch=2, grid=(n * N_SHARD,),
            in_specs=[pl.BlockSpec((1, h, c_), lambda i, c, me, a=a: (shard[a](i), c[0], 0)) for a, (h, c_) in enumerate(halves)]
            + [pl.BlockSpec((1, h, c_), lambda i, c, me, a=a: (shard[a](i), 0, 0)) for a, (h, c_) in enumerate(halves)],
            out_specs=[pl.BlockSpec((1, h, c_), lambda i, c, me, a=a: (shard[a](i), 0, 0)) for a, (h, c_) in enumerate(halves)]
            + [pl.BlockSpec((h, c_), lambda i, c, me: (0, 0)) for h, c_ in halves]),
        out_shape=[jax.ShapeDtypeStruct((N_SHARD, h, c_), CD) for h, c_ in halves]
        + [jax.ShapeDtypeStruct((h, c_), F32) for h, c_ in halves],
        compiler_params=_params(1),
    )(c_idx, me_idx, *gs, *gots)


def _chip_sum(owns, gots, me_idx):
    n = len(owns)
    tiles = [_row_tile(o.shape[0]) for o in owns]
    counts = [o.shape[0] // t for o, t in zip(owns, tiles)]
    starts = [sum(counts[:a]) for a in range(n)]

    def body(me_ref, *refs):
        own_refs, got_refs, out_refs = (refs[k * n:(k + 1) * n] for k in range(3))
        i = pl.program_id(0)
        for a in range(n):
            @pl.when((i >= starts[a]) & (i < starts[a] + counts[a]))
            def _(a=a):
                total = None
                for j in range(N_SHARD):
                    term = jnp.where(j == me_ref[0], own_refs[a][...], got_refs[a][j].astype(F32))
                    total = term if total is None else total + term
                out_refs[a][...] = total

    tile_of = [lambda i, a=a: jnp.clip(i - starts[a], 0, counts[a] - 1) for a in range(n)]
    own_specs = [pl.BlockSpec((tiles[a], owns[a].shape[1]), lambda i, me, a=a: (tile_of[a](i), 0)) for a in range(n)]
    return pl.pallas_call(
        body, name="grad_chip_sum",
        grid_spec=pltpu.PrefetchScalarGridSpec(
            num_scalar_prefetch=1, grid=(sum(counts),),
            in_specs=own_specs + [pl.BlockSpec((N_SHARD, tiles[a], owns[a].shape[1]), lambda i, me, a=a: (0, tile_of[a](i), 0))
                                  for a in range(n)],
            out_specs=own_specs),
        out_shape=[jax.ShapeDtypeStruct(o.shape, F32) for o in owns],
        compiler_params=_params(1),
    )(me_idx, *owns, *gots)


def _pair_share(halves):
    n = len(halves)

    def body(*refs):
        src, dst = refs[:n], refs[n:2 * n]
        send, recv = refs[2 * n:]
        x, y, c, _ = _position()
        copies = []
        for a in range(n):
            cp = pltpu.make_async_remote_copy(src_ref=src[a], dst_ref=dst[a], send_sem=send.at[a],
                                              recv_sem=recv.at[a], device_id=(x, y, 1 - c), device_id_type=MESH)
            cp.start()
            copies.append(cp)
        for cp in copies:
            cp.wait()

    return pl.pallas_call(
        body, name="grad_pair_share", in_specs=[ANY] * n, out_specs=[ANY] * n,
        out_shape=[jax.ShapeDtypeStruct(h.shape, h.dtype) for h in halves],
        scratch_shapes=[pltpu.SemaphoreType.DMA((n,)), pltpu.SemaphoreType.DMA((n,))],
    )(*halves)


def _small_allreduce(buf):
    rows, cols = buf.shape

    def body(src_ref, out_ref, slots, send, recv):
        x, y, c, _ = _position()
        me = 4 * x + 2 * y + c
        slots[me] = src_ref[...]
        copies = []
        k = 0
        for dx in (0, 1):
            for dy in (0, 1):
                for dc in (0, 1):
                    if (dx, dy, dc) == (0, 0, 0):
                        continue
                    peer = (jnp.where(dx, 1 - x, x), jnp.where(dy, 1 - y, y), jnp.where(dc, 1 - c, c))
                    cp = pltpu.make_async_remote_copy(src_ref=src_ref, dst_ref=slots.at[me], send_sem=send.at[k],
                                                      recv_sem=recv.at[k], device_id=peer, device_id_type=MESH)
                    cp.start()
                    copies.append(cp)
                    k += 1
        for cp in copies:
            cp.wait()
        total = slots[0]
        for dev in range(1, N_DEV):
            total = total + slots[dev]
        out_ref[...] = total

    vm = pl.BlockSpec(memory_space=pltpu.VMEM)
    return pl.pallas_call(
        body, name="small_allreduce", in_specs=[vm], out_specs=vm,
        out_shape=jax.ShapeDtypeStruct((rows, cols), F32),
        scratch_shapes=[pltpu.VMEM((N_DEV, rows, cols), F32), pltpu.SemaphoreType.DMA((N_DEV - 1,)),
                        pltpu.SemaphoreType.DMA((N_DEV - 1,))],
    )(buf)


def _adamw_math(w, gv, m, v):
    mn = ADAM_B1 * m + (1.0 - ADAM_B1) * gv
    vn = ADAM_B2 * v + (1.0 - ADAM_B2) * (gv * gv)
    m_hat = mn / (1.0 - ADAM_B1 ** ADAM_STEP)
    v_hat = vn / (1.0 - ADAM_B2 ** ADAM_STEP)
    return -ADAM_LR * (m_hat / (jnp.sqrt(v_hat) + ADAM_EPS) + ADAM_WD * w), mn, vn


def _adamw(w, g, m, v):
    rows, cols = w.shape
    tr = _row_tile(rows)

    def body(w_ref, g_ref, m_ref, v_ref, d_ref, mo_ref, vo_ref):
        d_ref[...], mo_ref[...], vo_ref[...] = _adamw_math(w_ref[...], g_ref[...], m_ref[...], v_ref[...])

    blk = pl.BlockSpec((tr, cols), lambda i: (i, 0))
    shp = jax.ShapeDtypeStruct((rows, cols), F32)
    return pl.pallas_call(
        body, name="adamw", grid=(rows // tr,), in_specs=[blk] * 4, out_specs=[blk] * 3, out_shape=[shp] * 3,
        compiler_params=_params(1),
    )(w, g, m, v)


def _adamw_halves(w, own, got, m, v, c_idx):
    rows, cols = w.shape
    tr = _row_tile(rows // 2)
    per_half = rows // 2 // tr

    def body(c_ref, w_ref, own_ref, got_ref, m_ref, v_ref, d_ref, mo_ref, vo_ref, g_ref):
        mine = (pl.program_id(0) // per_half) == c_ref[0]
        gv = jnp.where(mine, own_ref[...], got_ref[...])
        g_ref[...] = gv
        d_ref[...], mo_ref[...], vo_ref[...] = _adamw_math(w_ref[...], gv, m_ref[...], v_ref[...])

    blk = pl.BlockSpec((tr, cols), lambda i, c: (i, 0))
    own_blk = pl.BlockSpec((tr, cols), lambda i, c: (jnp.where(i // per_half == c[0], i % per_half, 0), 0))
    got_blk = pl.BlockSpec((tr, cols), lambda i, c: (jnp.where(i // per_half == c[0], 0, i % per_half), 0))
    shp = jax.ShapeDtypeStruct((rows, cols), F32)
    return pl.pallas_call(
        body, name="adamw_halves",
        grid_spec=pltpu.PrefetchScalarGridSpec(num_scalar_prefetch=1, grid=(rows // tr,),
                                               in_specs=[blk, own_blk, got_blk, blk, blk], out_specs=[blk] * 4),
        out_shape=[shp] * 4, compiler_params=_params(1),
    )(c_idx, w, own, got, m, v)


def _local_step(x, target, norm_gains, q_g, k_g, ng, weights_of, grads_done):
    s = x.shape[0]
    tm = min(512, s)
    tq = min(256, s)
    tf = min(1024, s)
    g1, gm, g2, gf = norm_gains
    cos2, sin2 = _rope_tables(s)
    gq8 = jnp.tile(q_g, (1, 8))
    gk2 = jnp.tile(k_g, (1, 2))

    tn = min(256, s)
    tk = min(1024, s)
    w1 = weights_of(1, ())
    x1, s1, t1, b1, h1 = _ffn_fwd(x, g1, w1["g1"], w1["u1"], w1["d1"], tf)
    w2 = weights_of(2, (x1,))
    lbl = w2["lbl"]
    pqkv, ph, pg, hm = _mix_in_fwd(x1, gm, w2["in"], tn)
    qe, kr, vr, vs = _qk_prep(pqkv, gq8, gk2, cos2, sin2, tm)
    oa, lse = _attn_fwd(qe, kr, vr, vs, tq)
    ob, pre, hstates = _hgrn_fwd(ph, lbl, ng)
    x2 = _mix_out_fwd(x1, oa, ob, pg, w2["a"], w2["b"], w2["o"], tm)
    w3 = weights_of(3, (x2,))
    x3, s2, t2, b2, h2 = _ffn_fwd(x2, g2, w3["g2"], w3["u2"], w3["d2"], tf)
    dx3, loss, dgf = _loss_head(x3, gf, target, tm)

    dx2, da2, db2, f2, dg2, dx3c = _ffn_bwd(dx3, x2, g2, s2, t2, b2, w3["g2"], w3["u2"], w3["d2"], tm)
    tok = grads_done(3, dict(g2=_dw_shared_b("dw_gate", da2, h2, tk, 1.0), u2=_dw_shared_b("dw_gate", db2, h2, tk, 1.0),
                             d2=_dw_shared_b("dw_down", f2, dx3c, tk, 1.0)))

    dpg, mg, dya, dyb, doe, delta, dob = _mix_out_bwd(dx2, oa, ob, pg, w2["a"], w2["b"], w2["o"], tm, tok)
    g_o = [g.reshape(N_SHARD, D_MODEL // N_SHARD, D_MODEL) for g in _dw_colblocks("dw_out", mg, dx2, 1, tk)]
    g_a = _dw_colblocks("dw_branch", oa, dya, N_SHARD, tk)
    g_b = _dw_colblocks("dw_branch", ob, dyb, N_SHARD, tk)
    dqt, dkt, dvt = _attn_bwd(qe, kr, kr.T, vr, doe, delta, lse, tq)
    dqkv, dgq, dgk = _qk_prep_bwd(pqkv, dqt.T, dkt.T, dvt.T, gq8, gk2, cos2, sin2, tm)
    dhq, dhff, dhfb, dhi, dhg, dlb, dng = _hgrn_bwd(ph, pre, dob, hstates, lbl, ng)
    dps = (dqkv, dhq, dhff, dhfb, dhi, dhg, dpg)
    g_in = [g.reshape(N_SHARD, -1, D_MODEL) for g in _dw_in(dps, hm, tk)]
    tok = grads_done(2, {"in": g_in, "a": g_a, "b": g_b, "o": g_o})
    dx1, dgm = _mix_in_bwd(dps, w2["in"], x1, dx2, gm, tn, tok)

    dx0, da1, db1, f1, dg1, dx1c = _ffn_bwd(dx1, x, g1, s1, t1, b1, w1["g1"], w1["u1"], w1["d1"], tm)
    grads_done(1, dict(g1=_dw_shared_b("dw_gate", da1, h1, tk, 1.0), u1=_dw_shared_b("dw_gate", db1, h1, tk, 1.0),
                       d1=_dw_shared_b("dw_down", f1, dx1c, tk, 1.0)))
    small = dict(g1=dg1, gm=dgm, g2=dg2, gf=dgf, gq=dgq, gk=dgk, lb=dlb, ng=dng)
    return loss, dx0, small, lbl


GROUPS = {1: ("g1", "u1", "d1"), 2: ("in", "a", "b", "o"), 3: ("g2", "u2", "d2")}
BIG = GROUPS[1] + GROUPS[2] + GROUPS[3]
TRANSPOSED = ("g1", "u1", "in", "g2", "u2")


def _pack_rows(vectors, width):
    rows = []
    for vct in vectors:
        flat = vct.reshape(-1)
        pad = (-flat.shape[0]) % width
        rows.append(jnp.pad(flat, (0, pad)).reshape(-1, width))
    return jnp.concatenate(rows, axis=0)


def kernel(x, ffn1_norm_g, ffn1_w_gate, ffn1_w_up, ffn1_w_down, mix_norm_g, w_in, q_norm_g, k_norm_g, hgrn_lb_logits, hgrn_out_norm_g, w_branch_attn, w_branch_hgrn, w_out, ffn2_norm_g, ffn2_w_gate, ffn2_w_up, ffn2_w_down, final_norm_g, loss_target, m_ffn1_norm_g, m_ffn1_w_gate, m_ffn1_w_up, m_ffn1_w_down, m_mix_norm_g, m_w_in, m_q_norm_g, m_k_norm_g, m_hgrn_lb_logits, m_hgrn_out_norm_g, m_w_branch_attn, m_w_branch_hgrn, m_w_out, m_ffn2_norm_g, m_ffn2_w_gate, m_ffn2_w_up, m_ffn2_w_down, m_final_norm_g, v_ffn1_norm_g, v_ffn1_w_gate, v_ffn1_w_up, v_ffn1_w_down, v_mix_norm_g, v_w_in, v_q_norm_g, v_k_norm_g, v_hgrn_lb_logits, v_hgrn_out_norm_g, v_w_branch_attn, v_w_branch_hgrn, v_w_out, v_ffn2_norm_g, v_ffn2_w_gate, v_ffn2_w_up, v_ffn2_w_down, v_final_norm_g):
    xi, yi, ci = lax.axis_index("x"), lax.axis_index("y"), lax.axis_index("c")
    me = 2 * xi + yi
    c_idx = jnp.reshape(ci, (1,)).astype(jnp.int32)
    me_idx = jnp.reshape(me, (1,)).astype(jnp.int32)

    big_w = dict(g1=ffn1_w_gate[0], u1=ffn1_w_up[0], d1=ffn1_w_down[0], a=w_branch_attn[0], b=w_branch_hgrn[0],
                 o=w_out[0], g2=ffn2_w_gate[0], u2=ffn2_w_up[0], d2=ffn2_w_down[0])
    big_w["in"] = w_in[0]
    big_m = dict(g1=m_ffn1_w_gate[0], u1=m_ffn1_w_up[0], d1=m_ffn1_w_down[0], a=m_w_branch_attn[0], b=m_w_branch_hgrn[0],
                 o=m_w_out[0], g2=m_ffn2_w_gate[0], u2=m_ffn2_w_up[0], d2=m_ffn2_w_down[0])
    big_m["in"] = m_w_in[0]
    big_v = dict(g1=v_ffn1_w_gate[0], u1=v_ffn1_w_up[0], d1=v_ffn1_w_down[0], a=v_w_branch_attn[0], b=v_w_branch_hgrn[0],
                 o=v_w_out[0], g2=v_ffn2_w_gate[0], u2=v_ffn2_w_up[0], d2=v_ffn2_w_down[0])
    big_v["in"] = v_w_in[0]
    for table in (big_w, big_m, big_v):
        for n in TRANSPOSED:
            table[n] = table[n].T

    started, token = {}, ()
    for grp in (1, 2, 3):
        shards = [big_w[n] for n in GROUPS[grp]] + ([hgrn_lb_logits.reshape(4, LANES)] if grp == 2 else [])
        dtypes = [CD] * len(GROUPS[grp]) + ([F32] if grp == 2 else [])
        lands = _cast_into_slots(shards, dtypes, me_idx)
        send, recv, _, lands, tok = _exchange_start("gather%d_start" % grp, [], lands, token)
        started[grp], token = (send, recv, lands), (tok,)

    def weights_of(grp, after):
        send, recv, lands = started[grp]
        got = _exchange_wait("gather%d_wait" % grp, send, recv, [], lands, tuple(after) + (token if grp == 1 else ()))
        by_halves = [i for i, land in enumerate(got) if _halved(land)]
        for i, whole in zip(by_halves, _pair_fill("gather%d_fill" % grp, [got[i] for i in by_halves])):
            got[i] = whole
        w = dict(zip(GROUPS[grp], got))
        if grp == 2:
            w["in"] = w["in"].reshape(-1, D_MODEL)
            w["o"] = w["o"].reshape(D_MODEL, D_MODEL)
            w["lbl"] = jnp.transpose(got[-1], (1, 0, 2)).reshape(4, N_SHARD * LANES)
        return w

    pending = {}

    def grads_done(grp, grads):
        names = list(grads)
        got = _pair_exchange([grads[n][1] for n in names])
        res = _pair_sum([grads[n][0] for n in names], got, c_idx, me_idx)
        sums, owns = res[:len(names)], res[len(names):]
        lands = [lax.empty(s_.shape, s_.dtype) for s_ in sums]
        send, recv, srcs, lands, tok = _exchange_start("reduce%d_start" % grp, list(sums), lands, ())
        pending[grp] = (names, send, recv, srcs, lands, owns, tok)
        return (tok,)

    def reduced_halves(grp, after):
        names, send, recv, srcs, lands, owns, _ = pending[grp]
        parts = _exchange_wait("reduce%d_wait" % grp, send, recv, srcs, lands, after)
        return names, list(_chip_sum(list(owns), parts, me_idx))

    loss, dx, small, lbl = _local_step(
        x[0], loss_target[0], (ffn1_norm_g, mix_norm_g, ffn2_norm_g, final_norm_g.reshape(1, -1)),
        q_norm_g, k_norm_g, hgrn_out_norm_g, weights_of, grads_done)

    dgq = small["gq"].reshape(8, HEAD_DIM).sum(axis=0)
    dgk = small["gk"].reshape(2, HEAD_DIM).sum(axis=0)
    lb_full = _hgrn_lower_bounds(lbl)
    dlog = []
    for d in (0, 1):
        t = small["lb"][d:d + 1] * lb_full[d] * (1.0 - lb_full[d])
        dlog += [t, -t]
    small_list = [small["g1"], small["gm"], small["g2"], small["gf"], small["ng"], dgq, dgk, jnp.concatenate(dlog, axis=0), loss[0, 0]]
    packed = _pack_rows(small_list, D_MODEL)
    n_rows = packed.shape[0]
    packed = jnp.pad(packed, ((0, (-n_rows) % 8), (0, 0)))
    red = _small_allreduce(packed)
    loss_out = red[n_rows - 1, 0]
    sg = dict(g1=red[0:1], gm=red[1:2], g2=red[2:3], gf=red[3], ng=red[4:5, :512], gq=red[5:6, :HEAD_DIM],
              gk=red[6:7, :HEAD_DIM])
    dlog_full = red[7:9].reshape(2, 2, 512)
    sg["lb"] = lax.dynamic_slice_in_dim(dlog_full, me * LANES, LANES, axis=2)

    small_w = dict(g1=ffn1_norm_g, gm=mix_norm_g, g2=ffn2_norm_g, gf=final_norm_g, ng=hgrn_out_norm_g, gq=q_norm_g,
                   gk=k_norm_g, lb=hgrn_lb_logits)
    small_m = dict(g1=m_ffn1_norm_g, gm=m_mix_norm_g, g2=m_ffn2_norm_g, gf=m_final_norm_g, ng=m_hgrn_out_norm_g,
                   gq=m_q_norm_g, gk=m_k_norm_g, lb=m_hgrn_lb_logits)
    small_v = dict(g1=v_ffn1_norm_g, gm=v_mix_norm_g, g2=v_ffn2_norm_g, gf=v_final_norm_g, ng=v_hgrn_out_norm_g,
                   gq=v_q_norm_g, gk=v_k_norm_g, lb=v_hgrn_lb_logits)
    small_names = ("g1", "gm", "g2", "gf", "ng", "gq", "gk", "lb")
    pack = lambda dct: _pack_rows([dct[n] for n in small_names], D_MODEL)
    pw, pgr, pm, pv = pack(small_w), pack(sg), pack(small_m), pack(small_v)
    pad8 = lambda a: jnp.pad(a, ((0, (-a.shape[0]) % 8), (0, 0)))
    sd, sm_, sv_ = _adamw(pad8(pw), pad8(pgr), pad8(pm), pad8(pv))

    def unpack(buf):
        out, r = {}, 0
        for n in small_names:
            size = small_w[n].size
            nr = -(-size // D_MODEL)
            out[n] = buf[r:r + nr].reshape(-1)[:size].reshape(small_w[n].shape)
            r += nr
        return out

    sdelta, snew_m, snew_v = unpack(sd), unpack(sm_), unpack(sv_)
    sgrad = {n: sg[n].reshape(small_w[n].shape) for n in small_names}

    bdelta, bnew_m, bnew_v, bgrad = {}, {}, {}, {}

    def update(names, halves):
        for n, own, got in zip(names, halves, _pair_share(halves)):
            res = _adamw_halves(big_w[n], own, got, big_m[n], big_v[n], c_idx)
            if n in TRANSPOSED:
                res = [r.T for r in res]
            bdelta[n], bnew_m[n], bnew_v[n], bgrad[n] = [r[None] for r in res]

    names3, halves3 = reduced_halves(3, (pending[1][-1],))
    names2, halves2 = reduced_halves(2, (halves3[0],))
    update(names3 + names2, halves3 + halves2)
    names1, halves1 = reduced_halves(1, (bdelta[names2[-1]],))
    update(names1, halves1)

    order = [("s", "g1"), ("b", "g1"), ("b", "u1"), ("b", "d1"), ("s", "gm"), ("b", "in"), ("s", "gq"), ("s", "gk"),
             ("s", "lb"), ("s", "ng"), ("b", "a"), ("b", "b"), ("b", "o"), ("s", "g2"), ("b", "g2"), ("b", "u2"),
             ("b", "d2"), ("s", "gf")]
    outs = [loss_out, dx[None]]
    for table_s, table_b in ((sgrad, bgrad), (sdelta, bdelta), (snew_m, bnew_m), (snew_v, bnew_v)):
        outs += [(table_s if kind == "s" else table_b)[n] for kind, n in order]
    return tuple(outs)
```

```python
import functools

import numpy as np
import jax
import jax.numpy as jnp
from jax import lax
from jax.experimental import pallas as pl
from jax.experimental.pallas import tpu as pltpu

F32 = jnp.float32
BF16 = jnp.bfloat16
CD = jnp.bfloat16

EPS = 1e-6
D_MODEL = 1024
HEAD_DIM = 64
GRID_W = 64
ROPE_THETA = 10000.0
CHUNK = 32
N_SHARD = 4
N_DEV = 8
VMEM_LIMIT = 56 * 1024 * 1024
HGRN_BWD_VMEM = 60 * 1024 * 1024
LANES = 128
HG_TILE = 128
ATTN_BWD_HEADS = 2
FFN_ROWS = 256

ADAM_LR = 0.001
ADAM_B1 = 0.9
ADAM_B2 = 0.999
ADAM_EPS = 1e-08
ADAM_WD = 0.01
ADAM_STEP = 10

NN = (((1,), (0,)), ((), ()))
NT = (((1,), (1,)), ((), ()))
TN = (((0,), (0,)), ((), ()))
MESH = pl.DeviceIdType.MESH
ANY = pl.BlockSpec(memory_space=pl.ANY)


def _mm(a, b, dn):
    return lax.dot_general(a.astype(CD), b.astype(CD), dn, preferred_element_type=F32)


def _split3(x):
    hi = x.astype(BF16)
    r = x - hi.astype(F32)
    mid = r.astype(BF16)
    lo = (r - mid.astype(F32)).astype(BF16)
    return hi, mid, lo


def _xdot(x, m):
    rows = x.shape[0]
    hi, mid, _ = _split3(x)
    r = lax.dot_general(jnp.concatenate([hi, mid], axis=0), m, NN, preferred_element_type=F32)
    return r[:rows] + r[rows:]


def _xdot_l(m, x):
    cols = x.shape[1]
    hi, mid, _ = _split3(x)
    r = lax.dot_general(m, jnp.concatenate([hi, mid], axis=1), NN, preferred_element_type=F32)
    return r[:, :cols] + r[:, cols:]


def _params(n_grid, vmem_limit=VMEM_LIMIT):
    return pltpu.CompilerParams(dimension_semantics=("arbitrary",) * n_grid, vmem_limit_bytes=vmem_limit)


def _sigmoid(x):
    return jax.nn.sigmoid(x)


def _np_blocksum(n):
    i = np.arange(n)
    return (i[:, None] // HEAD_DIM == i[None, :] // HEAD_DIM).astype(np.float32)


def _np_swap32(n):
    i = np.arange(n)
    partner = np.where(i % HEAD_DIM < HEAD_DIM // 2, i + HEAD_DIM // 2, i - HEAD_DIM // 2)
    m = np.zeros((n, n), np.float32)
    m[i, partner] = 1.0
    return m


def _np_expand_q():
    m = np.zeros((512, 1024), np.float32)
    for h in range(8):
        g = h // 4
        for d in range(HEAD_DIM):
            m[64 * h + d, 128 * h + 64 * g + d] = 1.0
    return m


def _np_headsum_spread():
    m = np.zeros((512, 1024), np.float32)
    for h in range(8):
        m[64 * h:64 * h + 64, 128 * h:128 * h + 128] = 1.0
    return m


def _np_swap_halves():
    m = np.zeros((128, 128), np.float32)
    i = np.arange(128)
    m[i, (i + 64) % 128] = 1.0
    return m


def _np_hgrn_cums(t, rev):
    r = np.arange(t)[:, None]
    c = np.arange(t)[None, :]
    same = (r // CHUNK) == (c // CHUNK)
    if not rev:
        cum = same & (c <= r)
        mid = same & (c % CHUNK <= CHUNK // 2 - 1)
    else:
        cum = same & (c >= r)
        mid = same & (c % CHUNK >= CHUNK // 2)
    return np.concatenate([cum, mid, same], axis=0).astype(np.float32)


def _bf(a):
    return jnp.asarray(a, dtype=BF16)


def _rope_tables(seq_len):
    rows = seq_len // GRID_W
    row = jnp.repeat(jnp.arange(rows, dtype=F32), GRID_W)
    col = jnp.tile(jnp.arange(GRID_W, dtype=F32), rows)
    n_freq = HEAD_DIM // 4
    inv = ROPE_THETA ** (-jnp.arange(n_freq, dtype=F32) / n_freq)
    ang = jnp.concatenate([row[:, None] * inv, col[:, None] * inv], axis=-1)
    cos, sin = jnp.cos(ang), jnp.sin(ang)
    c64 = jnp.concatenate([cos, cos], axis=-1)
    s64 = jnp.concatenate([-sin, sin], axis=-1)
    return jnp.tile(c64, (1, 2)), jnp.tile(s64, (1, 2))


def _ffn_fwd(x, g, wg, wu, wd, tm):
    s, d = x.shape
    nsh, fs, _ = wg.shape

    def body(x_ref, g_ref, wg_ref, wu_ref, wd_ref, xo_ref, a_ref, da_ref, b_ref, hb_ref, acc, hs):
        j = pl.program_id(1)

        @pl.when(j == 0)
        def _():
            xv = x_ref[...]
            r = lax.rsqrt(jnp.mean(xv * xv, axis=-1, keepdims=True) + EPS)
            h = (xv * r * g_ref[...]).astype(CD)
            hs[...] = h
            hb_ref[...] = h
            acc[...] = jnp.zeros_like(acc)

        blocks = [slice(r0, min(r0 + FFN_ROWS, tm)) for r0 in range(0, tm, FFN_ROWS)]
        firsts = [(_mm(hs[rows, :], wg_ref[0], NT), _mm(hs[rows, :], wu_ref[0], NT)) for rows in blocks]
        for rows, (a, b) in zip(blocks, firsts):
            sg = _sigmoid(a)
            silu = a * sg
            acc[rows, :] += _mm(silu * b, wd_ref[0], NN)
            a_ref[0, rows, :] = silu.astype(CD)
            da_ref[0, rows, :] = (sg * (1.0 + a * (1.0 - sg))).astype(CD)
            b_ref[0, rows, :] = b.astype(CD)

        @pl.when(j == nsh - 1)
        def _():
            xo_ref[...] = x_ref[...] + 0.5 * acc[...]

    return pl.pallas_call(
        body, name="ffn_fwd", grid=(s // tm, nsh),
        in_specs=[pl.BlockSpec((tm, d), lambda i, j: (i, 0)), pl.BlockSpec((1, d), lambda i, j: (0, 0))]
        + [pl.BlockSpec((1, fs, d), lambda i, j: (j, 0, 0))] * 3,
        out_specs=[pl.BlockSpec((tm, d), lambda i, j: (i, 0))] + [pl.BlockSpec((1, tm, fs), lambda i, j: (j, i, 0))] * 3
        + [pl.BlockSpec((tm, d), lambda i, j: (i, 0))],
        out_shape=[jax.ShapeDtypeStruct((s, d), F32)] + [jax.ShapeDtypeStruct((nsh, s, fs), CD)] * 3
        + [jax.ShapeDtypeStruct((s, d), CD)],
        scratch_shapes=[pltpu.VMEM((tm, d), F32), pltpu.VMEM((tm, d), CD)],
        compiler_params=_params(2),
    )(x, g, wg, wu, wd)


def _ffn_bwd(dout, x, g, silu, dsilu, b, wg, wu, wd, tm):
    s, d = x.shape
    nsh, fs, _ = wg.shape

    def body(do_ref, x_ref, g_ref, sl_ref, ds_ref, b_ref, wg_ref, wu_ref, wd_ref,
             dx_ref, da_ref, db_ref, f_ref, dg_ref, do16_ref, dh):
        i = pl.program_id(0)
        j = pl.program_id(1)

        @pl.when(j == 0)
        def _():
            dh[...] = jnp.zeros_like(dh)
            do16_ref[...] = (0.5 * do_ref[...]).astype(CD)

        @pl.when((i == 0) & (j == 0))
        def _():
            dg_ref[...] = jnp.zeros_like(dg_ref)

        blocks = [slice(r0, min(r0 + FFN_ROWS, tm)) for r0 in range(0, tm, FFN_ROWS)]
        dfs = [_mm(do16_ref[rows, :], wd_ref[0], NT) for rows in blocks]
        das, dbs = [], []
        for rows, df in zip(blocks, dfs):
            sl = sl_ref[0, rows, :].astype(F32)
            bv = b_ref[0, rows, :].astype(F32)
            da = (df * bv * ds_ref[0, rows, :].astype(F32)).astype(CD)
            db = (df * sl).astype(CD)
            da_ref[0, rows, :] = da
            db_ref[0, rows, :] = db
            f_ref[0, rows, :] = (sl * bv).astype(CD)
            das.append(da)
            dbs.append(db)
        for rows, da, db in zip(blocks, das, dbs):
            dh[rows, :] += _mm(da, wg_ref[0], NN) + _mm(db, wu_ref[0], NN)

        @pl.when(j == nsh - 1)
        def _():
            xv = x_ref[...]
            r = lax.rsqrt(jnp.mean(xv * xv, axis=-1, keepdims=True) + EPS)
            dhv = dh[...]
            u = dhv * g_ref[...]
            dx_ref[...] = do_ref[...] + r * u - xv * (r * r * r) * jnp.mean(u * xv, axis=-1, keepdims=True)
            dg_ref[...] += jnp.sum(dhv * xv * r, axis=0, keepdims=True)

    act = pl.BlockSpec((1, tm, fs), lambda i, j: (j, i, 0))
    row = pl.BlockSpec((tm, d), lambda i, j: (i, 0))
    return pl.pallas_call(
        body, name="ffn_bwd", grid=(s // tm, nsh),
        in_specs=[row, row, pl.BlockSpec((1, d), lambda i, j: (0, 0)), act, act, act]
        + [pl.BlockSpec((1, fs, d), lambda i, j: (j, 0, 0))] * 3,
        out_specs=[row, act, act, act, pl.BlockSpec((1, d), lambda i, j: (0, 0)), row],
        out_shape=[jax.ShapeDtypeStruct((s, d), F32), jax.ShapeDtypeStruct((nsh, s, fs), CD),
                   jax.ShapeDtypeStruct((nsh, s, fs), CD), jax.ShapeDtypeStruct((nsh, s, fs), CD),
                   jax.ShapeDtypeStruct((1, d), F32), jax.ShapeDtypeStruct((s, d), CD)],
        scratch_shapes=[pltpu.VMEM((tm, d), F32)],
        compiler_params=_params(2),
    )(dout, x, g, silu, dsilu, b, wg, wu, wd)


def _tn_call(name, operands, in_specs, out_shape, out_spec, grid, acc_shape, pick, scale=1.0):
    nk = grid[-1]
    n_in = len(operands)

    def body(*refs):
        out_ref, out16_ref, acc = refs[n_in], refs[n_in + 1], refs[n_in + 2]
        k = pl.program_id(len(grid) - 1)

        @pl.when(k == 0)
        def _():
            acc[...] = jnp.zeros_like(acc)

        pick(refs[:n_in], acc)

        @pl.when(k == nk - 1)
        def _():
            res = (acc[...] if scale == 1.0 else acc[...] * scale).reshape(out_ref.shape)
            out_ref[...] = res
            out16_ref[...] = res.astype(CD)

    return pl.pallas_call(
        body, name=name, grid=grid, in_specs=in_specs, out_specs=[out_spec, out_spec],
        out_shape=[out_shape, jax.ShapeDtypeStruct(out_shape.shape, CD)],
        scratch_shapes=[pltpu.VMEM(acc_shape, F32)], compiler_params=_params(len(grid)),
    )(*operands)


def _dw_shared_b(name, a3, b, tk, scale):
    nj, s, m = a3.shape
    n = b.shape[1]

    def pick(refs, acc):
        rows = pl.ds(pl.multiple_of(pl.program_id(1) * tk, tk), tk)
        acc[...] += _mm(refs[0][0], refs[1][rows, :], TN)

    return _tn_call(name, (a3, b),
                    [pl.BlockSpec((1, tk, m), lambda j, k: (j, k, 0)), pl.BlockSpec((s, n), lambda j, k: (0, 0))],
                    jax.ShapeDtypeStruct((nj, m, n), F32), pl.BlockSpec((1, m, n), lambda j, k: (j, 0, 0)),
                    (nj, s // tk), (m, n), pick, scale)


def _dw_colblocks(name, a, b, nj, tk):
    s, m = a.shape
    n = b.shape[1] // nj
    nk = s // tk

    def body(a_ref, b_ref, out_ref, out16_ref, acc):
        k = pl.program_id(0)

        @pl.when(k == 0)
        def _():
            acc[...] = jnp.zeros_like(acc)

        acc[...] += _mm(a_ref[...], b_ref[...], TN)

        @pl.when(k == nk - 1)
        def _():
            for j in range(nj):
                res = acc[:, j * n:(j + 1) * n]
                out_ref[j] = res
                out16_ref[j] = res.astype(CD)

    whole = pl.BlockSpec((nj, m, n), lambda k: (0, 0, 0))
    return pl.pallas_call(
        body, name=name, grid=(nk,),
        in_specs=[pl.BlockSpec((tk, m), lambda k: (k, 0)), pl.BlockSpec((tk, nj * n), lambda k: (k, 0))],
        out_specs=[whole, whole],
        out_shape=[jax.ShapeDtypeStruct((nj, m, n), F32), jax.ShapeDtypeStruct((nj, m, n), CD)],
        scratch_shapes=[pltpu.VMEM((m, nj * n), F32)], compiler_params=_params(1),
    )(a, b)


DP_WIDTHS = (768, 512, 512, 512, 512, 512, 2048)
DW_IN_COLS = 512


def _dw_in(dps, hb, tk):
    s, d = hb.shape
    nk = s // tk
    blocks, row = [], 0
    for p, width in enumerate(DP_WIDTHS):
        step = width if width <= 768 else DW_IN_COLS
        for c0 in range(0, width, step):
            blocks.append((p, c0, step, row))
            row += step
    nb, max_w = len(blocks), max(b[2] for b in blocks)
    first = [min(i for i, b in enumerate(blocks) if b[0] == p) for p in range(len(DP_WIDTHS))]
    count = [sum(1 for b in blocks if b[0] == p) for p in range(len(DP_WIDTHS))]

    def body(*refs):
        dp_refs, hb_ref, out_ref, out16_ref, acc, acc16, sems = refs[:7], refs[7], refs[8], refs[9], refs[10], refs[11], refs[12]
        b, k = pl.program_id(0), pl.program_id(1)
        rows = pl.ds(pl.multiple_of(k * tk, tk), tk)

        def writes(i):
            _, _, w, r0 = blocks[i]
            slot = i % 2
            return (pltpu.make_async_copy(acc.at[slot, 0:w], out_ref.at[r0:r0 + w], sems.at[slot, 0]),
                    pltpu.make_async_copy(acc16.at[slot, 0:w], out16_ref.at[r0:r0 + w], sems.at[slot, 1]))

        for i, (p, _, w, _) in enumerate(blocks):
            @pl.when(b == i)
            def _(i=i, p=p, w=w):
                slot = i % 2
                prod = _mm(dp_refs[p][...], hb_ref[rows, :], TN)

                @pl.when(k == 0)
                def _():
                    acc[slot, 0:w] = prod

                @pl.when(k > 0)
                def _():
                    acc[slot, 0:w] += prod

                @pl.when(k == nk - 1)
                def _():
                    if i >= 1:
                        for cp in writes(i - 1):
                            cp.wait()
                    acc16[slot, 0:w] = acc[slot, 0:w].astype(CD)
                    for cp in writes(i):
                        cp.start()
                    if i == nb - 1:
                        for cp in writes(i):
                            cp.wait()

    def piece_spec(p):
        width = DP_WIDTHS[p]
        cols = width if width <= 768 else DW_IN_COLS

        def imap(b, k):
            active = (b >= first[p]) & (b < first[p] + count[p])
            return (jnp.where(active, k, jnp.where(b < first[p], 0, nk - 1)), jnp.clip(b - first[p], 0, count[p] - 1))

        return pl.BlockSpec((tk, cols), imap)

    return pl.pallas_call(
        body, name="dw_in", grid=(nb, nk),
        in_specs=[piece_spec(p) for p in range(len(DP_WIDTHS))] + [pl.BlockSpec((s, d), lambda b, k: (0, 0))],
        out_specs=[ANY, ANY],
        out_shape=[jax.ShapeDtypeStruct((sum(DP_WIDTHS), d), F32), jax.ShapeDtypeStruct((sum(DP_WIDTHS), d), CD)],
        scratch_shapes=[pltpu.VMEM((2, max_w, d), F32), pltpu.VMEM((2, max_w, d), CD), pltpu.SemaphoreType.DMA((2, 2))],
        compiler_params=_params(2),
    )(*dps, hb)


def _mix_in_fwd(x, g, w_t, tm):
    s, d = x.shape
    n_in = w_t.shape[0]

    def body(x_ref, g_ref, w_ref, qkv_ref, hg_ref, gt_ref, hb_ref):
        xv = x_ref[...]
        r = lax.rsqrt(jnp.mean(xv * xv, axis=-1, keepdims=True) + EPS)
        h = (xv * r * g_ref[...]).astype(CD)
        hb_ref[...] = h
        off = DP_WIDTHS[0]
        qkv_ref[...] = _mm(h, w_ref[0:off, :], NT)
        for c, width in enumerate(DP_WIDTHS[1:6]):
            hg_ref[:, c * width:(c + 1) * width] = _mm(h, w_ref[off:off + width, :], NT)
            off += width
        gate = DP_WIDTHS[6] // 2
        for c in range(2):
            gt_ref[:, c * gate:(c + 1) * gate] = _mm(h, w_ref[off:off + gate, :], NT).astype(CD)
            off += gate

    row = lambda w: pl.BlockSpec((tm, w), lambda i: (i, 0))
    return pl.pallas_call(
        body, name="mix_in_fwd", grid=(s // tm,),
        in_specs=[row(d), pl.BlockSpec((1, d), lambda i: (0, 0)), pl.BlockSpec((n_in, d), lambda i: (0, 0))],
        out_specs=[row(768), row(2560), row(2048), row(d)],
        out_shape=[jax.ShapeDtypeStruct((s, 768), F32), jax.ShapeDtypeStruct((s, 2560), F32),
                   jax.ShapeDtypeStruct((s, 2048), CD), jax.ShapeDtypeStruct((s, d), CD)],
        compiler_params=_params(1),
    )(x, g, w_t)


def _mix_in_bwd(dps, w_t, x, dres, g, tm, after=()):
    s, d = x.shape
    n_in = w_t.shape[0]

    def body(*refs):
        refs = refs[len(after):]
        dp_refs = refs[:7]
        w_ref, x_ref, dr_ref, g_ref, dx_ref, dg_ref = refs[7:]

        @pl.when(pl.program_id(0) == 0)
        def _():
            dg_ref[...] = jnp.zeros_like(dg_ref)

        dhv = jnp.zeros((tm, d), F32)
        off = 0
        for ref, width in zip(dp_refs, DP_WIDTHS):
            dhv = dhv + _mm(ref[...], w_ref[off:off + width, :], NN)
            off += width
        xv = x_ref[...]
        r = lax.rsqrt(jnp.mean(xv * xv, axis=-1, keepdims=True) + EPS)
        u = dhv * g_ref[...]
        dx_ref[...] = dr_ref[...] + r * u - xv * (r * r * r) * jnp.mean(u * xv, axis=-1, keepdims=True)
        dg_ref[...] += jnp.sum(dhv * xv * r, axis=0, keepdims=True)

    row = pl.BlockSpec((tm, d), lambda i: (i, 0))
    vec = pl.BlockSpec((1, d), lambda i: (0, 0))
    return pl.pallas_call(
        body, name="mix_in_bwd", grid=(s // tm,),
        in_specs=[ANY] * len(after) + [pl.BlockSpec((tm, w), lambda i: (i, 0)) for w in DP_WIDTHS]
        + [pl.BlockSpec((n_in, d), lambda i: (0, 0)), row, row, vec],
        out_specs=[row, vec],
        out_shape=[jax.ShapeDtypeStruct((s, d), F32), jax.ShapeDtypeStruct((1, d), F32)],
        compiler_params=_params(1),
    )(*after, *dps, w_t, x, dres, g)


def _headnorm_rope(x, gain, cos, sin, blocksum, swap):
    ss = _xdot(x * x, blocksum)
    r = lax.rsqrt(ss * (1.0 / HEAD_DIM) + EPS)
    y = x * r * gain
    return y * cos + _xdot(y, swap) * sin, r


def _headnorm_rope_bwd(dz, x, gain, cos, sin, blocksum, swap):
    ss = _xdot(x * x, blocksum)
    r = lax.rsqrt(ss * (1.0 / HEAD_DIM) + EPS)
    dy = dz * cos + _xdot(dz * sin, swap)
    u = dy * gain
    mean_ux = _xdot(u * x, blocksum) * (1.0 / HEAD_DIM)
    dx = r * u - x * (r * r * r) * mean_ux
    return dx, jnp.sum(dy * x * r, axis=0, keepdims=True)


def _qk_prep(pqkv, gq, gk, cos2, sin2, tm):
    s = pqkv.shape[0]
    bs512, sw512, eq, swh = _bf(_np_blocksum(512)), _bf(_np_swap32(512)), _bf(_np_expand_q()), _bf(_np_swap_halves())

    def body(q_ref, kv_ref, gq_ref, gk_ref, c_ref, s_ref, bs_ref, sw_ref, eq_ref, swh_ref, qe_ref, k_ref, v_ref, vs_ref):
        c2, s2 = c_ref[...], s_ref[...]
        c8, s8 = jnp.tile(c2, (1, 4)), jnp.tile(s2, (1, 4))
        bs, sw = bs_ref[...], sw_ref[...]
        zq, _ = _headnorm_rope(q_ref[...], gq_ref[...], c8, s8, bs, sw)
        qe_ref[...] = _mm(zq * (HEAD_DIM ** -0.5), eq_ref[...], NN).astype(CD)
        kv = kv_ref[...]
        zk, _ = _headnorm_rope(kv[:, :LANES], gk_ref[...], c2, s2, bs[:LANES, :LANES], sw[:LANES, :LANES])
        k_ref[...] = zk.astype(CD)
        v = kv[:, LANES:]
        v_ref[...] = v.astype(CD)
        vs_ref[...] = _mm(v, swh_ref[...], NN).astype(CD)

    full = lambda a: pl.BlockSpec(a.shape, lambda i: (0,) * a.ndim)
    tab = pl.BlockSpec((tm, LANES), lambda i: (i, 0))
    return pl.pallas_call(
        body, name="qk_prep", grid=(s // tm,),
        in_specs=[pl.BlockSpec((tm, 512), lambda i: (i, 0)), pl.BlockSpec((tm, 256), lambda i: (i, 2)),
                  full(gq), full(gk), tab, tab, full(bs512), full(sw512), full(eq), full(swh)],
        out_specs=[pl.BlockSpec((tm, 1024), lambda i: (i, 0)), tab, tab, tab],
        out_shape=[jax.ShapeDtypeStruct((s, 1024), CD)] + [jax.ShapeDtypeStruct((s, LANES), CD)] * 3,
        compiler_params=_params(1),
    )(pqkv, pqkv, gq, gk, cos2, sin2, bs512, sw512, eq, swh)


def _qk_prep_bwd(pqkv, dq, dk, dv, gq, gk, cos2, sin2, tm):
    s = pqkv.shape[0]
    bs512, sw512 = _bf(_np_blocksum(512)), _bf(_np_swap32(512))

    def body(q_ref, kv_ref, dq_ref, dk_ref, dv_ref, gq_ref, gk_ref, c_ref, s_ref, bs_ref, sw_ref,
             dp_ref, dgq_ref, dgk_ref):
        @pl.when(pl.program_id(0) == 0)
        def _():
            dgq_ref[...] = jnp.zeros_like(dgq_ref)
            dgk_ref[...] = jnp.zeros_like(dgk_ref)

        c2, s2 = c_ref[...], s_ref[...]
        c8, s8 = jnp.tile(c2, (1, 4)), jnp.tile(s2, (1, 4))
        bs, sw = bs_ref[...], sw_ref[...]
        dzq = dq_ref[...] * (HEAD_DIM ** -0.5)
        dxq, dgq = _headnorm_rope_bwd(dzq, q_ref[...], gq_ref[...], c8, s8, bs, sw)
        kv = kv_ref[...]
        dxk, dgk = _headnorm_rope_bwd(dk_ref[...], kv[:, :LANES], gk_ref[...], c2, s2, bs[:LANES, :LANES], sw[:LANES, :LANES])
        dp_ref[...] = jnp.concatenate([dxq, dxk, dv_ref[...]], axis=1).astype(CD)
        dgq_ref[...] += dgq
        dgk_ref[...] += dgk

    full = lambda a: pl.BlockSpec(a.shape, lambda i: (0,) * a.ndim)
    tab = pl.BlockSpec((tm, LANES), lambda i: (i, 0))
    return pl.pallas_call(
        body, name="qk_prep_bwd", grid=(s // tm,),
        in_specs=[pl.BlockSpec((tm, 512), lambda i: (i, 0)), pl.BlockSpec((tm, 256), lambda i: (i, 2)),
                  pl.BlockSpec((tm, 512), lambda i: (i, 0)), tab, tab, full(gq), full(gk), tab, tab,
                  full(bs512), full(sw512)],
        out_specs=[pl.BlockSpec((tm, 768), lambda i: (i, 0)), pl.BlockSpec((1, 512), lambda i: (0, 0)),
                   pl.BlockSpec((1, LANES), lambda i: (0, 0))],
        out_shape=[jax.ShapeDtypeStruct((s, 768), CD), jax.ShapeDtypeStruct((1, 512), F32),
                   jax.ShapeDtypeStruct((1, LANES), F32)],
        compiler_params=_params(1),
    )(pqkv, pqkv, dq, dk, dv, gq, gk, cos2, sin2, bs512, sw512)


def _kv_rows(h):
    return pl.ds(pl.multiple_of((h // 4) * HEAD_DIM, HEAD_DIM), HEAD_DIM)


def _attn_fwd(qe, k, v, vs, tq):
    s = k.shape[0]

    def body(q0_ref, q1_ref, q2_ref, q3_ref, k_ref, v_ref, vs_ref, o_ref, lse_ref):
        grp = pl.program_id(0)
        kk = k_ref[...]
        heads = range(4)
        scores = [_mm(q_ref[...], kk, NT) for q_ref in (q0_ref, q1_ref, q2_ref, q3_ref)]
        mxs = [jnp.max(sc, axis=-1, keepdims=True) for sc in scores]
        es = [jnp.exp(scores[r] - mxs[r]) for r in heads]
        ls = [jnp.sum(e, axis=-1, keepdims=True) for e in es]
        for r in heads:
            lse_ref[r] = mxs[r] + jnp.log(ls[r])
        outs = [_mm(es[r], jnp.where(grp != r % 2, vs_ref[...], v_ref[...]), NN) * (1.0 / ls[r]) for r in heads]
        low = lax.broadcasted_iota(jnp.int32, (1, LANES), 1) < HEAD_DIM
        o_ref[...] = jnp.concatenate([jnp.where(low, outs[0], outs[1]), jnp.where(low, outs[2], outs[3])], axis=1)

    kv = pl.BlockSpec((s, LANES), lambda g, i: (0, 0))
    qblk = lambda r: pl.BlockSpec((tq, LANES), lambda g, i: (i, 4 * g + r))
    return pl.pallas_call(
        body, name="attn_fwd", grid=(2, s // tq),
        in_specs=[qblk(0), qblk(1), qblk(2), qblk(3), kv, kv, kv],
        out_specs=[pl.BlockSpec((tq, 2 * LANES), lambda g, i: (i, g)), pl.BlockSpec((4, tq, 1), lambda g, i: (g, i, 0))],
        out_shape=[jax.ShapeDtypeStruct((s, 512), F32), jax.ShapeDtypeStruct((8, s, 1), F32)],
        compiler_params=_params(2),
    )(qe, qe, qe, qe, k, v, vs)


def _attn_bwd(qe, k, kt, v, doe, delta, lse, tq):
    s = k.shape[0]

    nh = ATTN_BWD_HEADS

    def body(*refs):
        q_refs, (k_ref, kt_ref, v_ref) = refs[:nh], refs[nh:nh + 3]
        do_refs, dl_refs = refs[nh + 3:2 * nh + 3], refs[2 * nh + 3:3 * nh + 3]
        lse_ref, dqt_ref, dkt_ref, dvt_ref, qt, dot = refs[3 * nh + 3:]

        @pl.when((pl.program_id(0) == 0) & (pl.program_id(1) == 0))
        def _():
            dkt_ref[...] = jnp.zeros_like(dkt_ref)
            dvt_ref[...] = jnp.zeros_like(dvt_ref)

        rows = _kv_rows(nh * pl.program_id(0))
        kt = kt_ref[rows, :]
        dkt = jnp.zeros((HEAD_DIM, s), F32)
        dvt = jnp.zeros((HEAD_DIM, s), F32)
        firsts = [(_mm(q_ref[...], k_ref[...], NT), _mm(do_ref[...], v_ref[...], NT))
                  for q_ref, do_ref in zip(q_refs, do_refs)]
        for idx, (q_ref, do_ref, dl_ref) in enumerate(zip(q_refs, do_refs, dl_refs)):
            q, do = q_ref[...], do_ref[...]
            sc, dp = firsts[idx]
            p = jnp.exp(sc - lse_ref[idx])
            ds = p * (dp - jnp.max(dl_ref[...], axis=-1, keepdims=True))
            dqt_ref[idx * HEAD_DIM:(idx + 1) * HEAD_DIM, :] = _mm(kt, ds, NT)
            qt[idx] = jnp.transpose(q.astype(F32))
            dot[idx] = jnp.transpose(do.astype(F32))
            dkt = dkt + _mm(qt[idx, rows, :], ds, NN)
            dvt = dvt + _mm(dot[idx, rows, :], p, NN)
        dkt_ref[rows, :] += dkt
        dvt_ref[rows, :] += dvt

    kv = pl.BlockSpec((s, LANES), lambda m, i: (0, 0))
    kvt = pl.BlockSpec((LANES, s), lambda m, i: (0, 0))
    blks = [pl.BlockSpec((tq, LANES), lambda m, i, r=r: (i, nh * m + r)) for r in range(nh)]
    return pl.pallas_call(
        body, name="attn_bwd", grid=(8 // nh, s // tq),
        in_specs=blks + [kv, kvt, kv] + blks + blks + [pl.BlockSpec((nh, tq, 1), lambda m, i: (m, i, 0))],
        out_specs=[pl.BlockSpec((nh * HEAD_DIM, tq), lambda m, i: (m, i)), kvt, kvt],
        out_shape=[jax.ShapeDtypeStruct((8 * HEAD_DIM, s), F32), jax.ShapeDtypeStruct((LANES, s), F32),
                   jax.ShapeDtypeStruct((LANES, s), F32)],
        scratch_shapes=[pltpu.VMEM((nh, LANES, tq), F32), pltpu.VMEM((nh, LANES, tq), F32)],
        compiler_params=_params(2),
    )(*[qe] * nh, k, kt, v, *[doe] * nh, *[delta] * nh, lse)


@jax.custom_vjp
def _mm_nn(a, b):
    return _mm(a, b, NN)


_mm_nn.defvjp(lambda a, b: (_mm(a, b, NN), (a, b)),
              lambda res, g: (_mm(g, res[1], NT), _mm(res[0], g, TN)))


@jax.custom_vjp
def _mm_nt(a, b):
    return _mm(a, b, NT)


_mm_nt.defvjp(lambda a, b: (_mm(a, b, NT), (a, b)),
              lambda res, g: (_mm(g, res[1], NN), _mm(g, res[0], TN)))


@jax.custom_vjp
def _mm_tn(a, b):
    return _mm(a, b, TN)


_mm_tn.defvjp(lambda a, b: (_mm(a, b, TN), (a, b)),
              lambda res, g: (_mm(res[1], g, NT), _mm(res[0], g, NN)))


@jax.custom_vjp
def _cmm(m, mt, x):
    return _xdot_l(m, x)


_cmm.defvjp(lambda m, mt, x: (_xdot_l(m, x), (m, mt)),
            lambda res, g: (jnp.zeros_like(res[0]), jnp.zeros_like(res[1]), _xdot_l(res[1], g)))


def _hgrn_masks(t, rev):
    n_ch = t // CHUNK
    r = jnp.bitwise_and(lax.broadcasted_iota(jnp.int32, (2 * t, t), 0), t - 1)
    c = lax.broadcasted_iota(jnp.int32, (2 * t, t), 1)
    same = jnp.right_shift(r, 5) == jnp.right_shift(c, 5)
    tri2 = same & ((c >= r) if rev else (c <= r))
    pr = lax.broadcasted_iota(jnp.int32, (LANES, LANES), 0)
    pc = lax.broadcasted_iota(jnp.int32, (LANES, LANES), 1)
    diag = jnp.right_shift(pr, 6) == jnp.right_shift(pc, 6)
    qr = lax.broadcasted_iota(jnp.int32, (t, n_ch * LANES), 0)
    qc = lax.broadcasted_iota(jnp.int32, (t, n_ch * LANES), 1)
    rows_chunk = jnp.right_shift(qc, 7) == jnp.right_shift(qr, 5)
    vr = lax.broadcasted_iota(jnp.int32, (n_ch * LANES, t), 0)
    vc = lax.broadcasted_iota(jnp.int32, (n_ch * LANES, t), 1)
    cols_chunk = jnp.right_shift(vr, 7) == jnp.right_shift(vc, 5)
    return dict(tri2=tri2, diag=diag, rows_chunk=rows_chunk, cols_chunk=cols_chunk)


def _hgrn_gates(xf, lb):
    f = lb + (1.0 - lb) * _sigmoid(xf)
    return 1.0 - f, jnp.log(f)


def _hgrn_dir(*args):
    return _hgrn_dirs([args])[0]


def _hgrn_dirs(arg_sets):
    chains = [_hgrn_phases(*a) for a in arg_sets]
    results = [None] * len(chains)
    while any(r is None for r in results):
        for n, chain in enumerate(chains):
            if results[n] is None:
                try:
                    next(chain)
                except StopIteration as done:
                    results[n] = done.value
    return results


def _hgrn_phases(xq, xf, v, lb, state, cm, cmt, mk, rev):
    t = xq.shape[0]
    n_ch = t // CHUNK
    lo = lax.broadcasted_iota(jnp.int32, (1, LANES), 1) < HEAD_DIM
    k, lf = _hgrn_gates(xf, lb)
    cs = _cmm(cm, cmt, lf)
    yield
    q = xq * _sigmoid(xq)
    b, bm, bl = cs[:t], cs[t:2 * t], cs[2 * t:]
    qd = q * jnp.exp(b - bm)
    kd = k * jnp.exp(bm - b)
    yield
    qd2 = jnp.concatenate([jnp.where(lo, qd, 0.0), jnp.where(lo, 0.0, qd)], axis=0)
    scores = _mm_nt(qd2, kd)
    yield
    kc = k * jnp.exp(bl - b)
    qe = q * jnp.exp(b)
    vexp = jnp.where(mk["cols_chunk"], jnp.concatenate([jnp.transpose(v)] * n_ch, axis=0), 0.0)
    adds = _mm_nn(vexp, kc)
    yield
    o2 = _mm_nn(jnp.where(mk["tri2"], scores, 0.0), v)
    o = jnp.where(lo, o2[:t], o2[t:])
    yield
    dec = jnp.exp(bl)
    entering = [None] * n_ch
    for c in (range(n_ch - 1, -1, -1) if rev else range(n_ch)):
        entering[c] = state
        d = jnp.concatenate([dec[c * CHUNK:(c + 1) * CHUNK]] * (LANES // CHUNK), axis=0)
        state = d * state + jnp.where(mk["diag"], adds[c * LANES:(c + 1) * LANES], 0.0)
    yield
    qexp = jnp.where(mk["rows_chunk"], jnp.concatenate([qe] * n_ch, axis=1), 0.0)
    return o + _mm_nt(qexp, jnp.concatenate(entering, axis=1)), state


def _hgrn_lower_bounds(l):
    out = []
    for d in (0, 1):
        l0, l1 = l[2 * d:2 * d + 1, :], l[2 * d + 1:2 * d + 2, :]
        mx = jnp.maximum(l0, l1)
        e0, e1 = jnp.exp(l0 - mx), jnp.exp(l1 - mx)
        out.append(e0 / (e0 + e1))
    return out


def _hgrn_consts(t):
    cf, cb = _np_hgrn_cums(t, False), _np_hgrn_cums(t, True)
    return (_bf(cf), _bf(cf.T), _bf(cb), _bf(cb.T), _bf(_np_blocksum(LANES)))


def _hgrn_fwd(ph, lbl, ng):
    s = ph.shape[0]
    t = min(HG_TILE, s)
    nt = s // t
    consts = _hgrn_consts(t)

    def body(xq_ref, xff_ref, xfb_ref, xi_ref, xg_ref, lbl_ref, ng_ref, cf_ref, cft_ref, cb_ref, cbt_ref, bs_ref,
             o_ref, pre_ref, st_ref, acc):
        lbf, lbb = _hgrn_lower_bounds(lbl_ref)
        mk_f, mk_b = _hgrn_masks(t, False), _hgrn_masks(t, True)
        zero = jnp.zeros((LANES, LANES), F32)

        def rows_of(i):
            return pl.ds(pl.multiple_of(i * t, t), t)

        acc[...] = jnp.zeros_like(acc)

        def step(i, states):
            tb = nt - 1 - i
            rf, rb = rows_of(i), rows_of(tb)
            st_ref[0, 0, i] = states[0]
            st_ref[0, 1, tb] = states[1]
            (of, sf), (ob, sb) = _hgrn_dirs([
                (xq_ref[rf, :], xff_ref[rf, :], xi_ref[rf, :], lbf, states[0], cf_ref[...], cft_ref[...], mk_f, False),
                (xq_ref[rb, :], xfb_ref[rb, :], xi_ref[rb, :], lbb, states[1], cb_ref[...], cbt_ref[...], mk_b, True)])
            acc[rf, :] += of
            acc[rb, :] += ob
            return sf, sb

        lax.fori_loop(0, nt, step, (zero, zero))

        def step_n(i, carry):
            rows = rows_of(i)
            o = acc[rows, :]
            ss = _xdot(o * o, bs_ref[...])
            r = lax.rsqrt(ss * (1.0 / HEAD_DIM) + EPS)
            xg = xg_ref[rows, :]
            pre_ref[rows, :] = o
            o_ref[rows, :] = ((o * r * ng_ref[...]) * (xg * _sigmoid(xg))).astype(CD)
            return carry

        lax.fori_loop(0, nt, step_n, 0)

    col = lambda off: pl.BlockSpec((s, LANES), lambda m: (0, off + m))
    full = lambda a: pl.BlockSpec(a.shape, lambda m: (0,) * a.ndim)
    return pl.pallas_call(
        body, name="hgrn_fwd", grid=(4,),
        in_specs=[col(0), col(4), col(8), col(12), col(16), pl.BlockSpec((4, LANES), lambda m: (0, m)),
                  pl.BlockSpec((1, LANES), lambda m: (0, m))] + [full(c) for c in consts],
        out_specs=[col(0), col(0), pl.BlockSpec((1, 2, nt, LANES, LANES), lambda m: (m, 0, 0, 0, 0))],
        out_shape=[jax.ShapeDtypeStruct((s, 512), CD), jax.ShapeDtypeStruct((s, 512), F32),
                   jax.ShapeDtypeStruct((4, 2, nt, LANES, LANES), F32)],
        scratch_shapes=[pltpu.VMEM((s, LANES), F32)],
        compiler_params=_params(1),
    )(ph, ph, ph, ph, ph, lbl, ng, *consts)


def _hgrn_bwd(ph, pre, dout, states, lbl, ng):
    s = ph.shape[0]
    t = min(HG_TILE, s)
    nt = s // t
    consts = _hgrn_consts(t)

    def body(xq_ref, xff_ref, xfb_ref, xi_ref, xg_ref, pre_ref, do_ref, st_ref, lbl_ref, ng_ref,
             cf_ref, cft_ref, cb_ref, cbt_ref, bs_ref,
             dq_ref, dff_ref, dfb_ref, di_ref, dg_ref, dlb_ref, dng_ref, dpre, dq_acc, dv_acc):
        lbf, lbb = _hgrn_lower_bounds(lbl_ref)
        mk_f, mk_b = _hgrn_masks(t, False), _hgrn_masks(t, True)
        zero = jnp.zeros((LANES, LANES), F32)
        zrow = jnp.zeros((1, LANES), F32)

        def rows_of(i):
            return pl.ds(pl.multiple_of(i * t, t), t)

        def step_n(i, dng):
            rows = rows_of(i)
            o, xg, do = pre_ref[rows, :], xg_ref[rows, :], do_ref[rows, :]
            bs = bs_ref[...]
            r = lax.rsqrt(_xdot(o * o, bs) * (1.0 / HEAD_DIM) + EPS)
            sg = _sigmoid(xg)
            gate = xg * sg
            don = do * gate
            dg_ref[rows, :] = (do * (o * r * ng_ref[...]) * (sg * (1.0 + xg * (1.0 - sg)))).astype(CD)
            u = don * ng_ref[...]
            dpre[rows, :] = r * u - o * (r * r * r) * (_xdot(u * o, bs) * (1.0 / HEAD_DIM))
            return dng + jnp.sum(don * o * r, axis=0, keepdims=True)

        dng_ref[...] = lax.fori_loop(0, nt, step_n, zrow)

        dq_acc[...] = jnp.zeros_like(dq_acc)
        dv_acc[...] = jnp.zeros_like(dv_acc)

        def step_g(i, carry):
            dsf, dsb, dlbf, dlbb = carry
            tf, tb = nt - 1 - i, i
            rf, rb = rows_of(tf), rows_of(tb)
            cf, cft, cb, cbt = cf_ref[...], cft_ref[...], cb_ref[...], cbt_ref[...]

            def both(xq_f, xf_f, v_f, lb_f, s_f, xq_b, xf_b, v_b, lb_b, s_b):
                (of, sf), (ob, sb) = _hgrn_dirs([(xq_f, xf_f, v_f, lb_f, s_f, cf, cft, mk_f, False),
                                                 (xq_b, xf_b, v_b, lb_b, s_b, cb, cbt, mk_b, True)])
                return of, sf, ob, sb

            _, vjp = jax.vjp(both, xq_ref[rf, :], xff_ref[rf, :], xi_ref[rf, :], lbf, st_ref[0, 0, tf],
                             xq_ref[rb, :], xfb_ref[rb, :], xi_ref[rb, :], lbb, st_ref[0, 1, tb])
            dq_f, dx_f, dv_f, gf, dsf, dq_b, dx_b, dv_b, gb, dsb = vjp((dpre[rf, :], dsf, dpre[rb, :], dsb))
            dff_ref[rf, :] = dx_f.astype(CD)
            dfb_ref[rb, :] = dx_b.astype(CD)
            dq_acc[rf, :] += dq_f
            dv_acc[rf, :] += dv_f
            dq_acc[rb, :] += dq_b
            dv_acc[rb, :] += dv_b
            return dsf, dsb, dlbf + gf, dlbb + gb

        _, _, dlbf, dlbb = lax.fori_loop(0, nt, step_g, (zero, zero, zrow, zrow))
        dlb_ref[0:1, :] = dlbf
        dlb_ref[1:2, :] = dlbb
        dq_ref[...] = dq_acc[...].astype(CD)
        di_ref[...] = dv_acc[...].astype(CD)

    col = lambda off: pl.BlockSpec((s, LANES), lambda m: (0, off + m))
    full = lambda a: pl.BlockSpec(a.shape, lambda m: (0,) * a.ndim)
    stream = jax.ShapeDtypeStruct((s, 512), CD)
    return pl.pallas_call(
        body, name="hgrn_bwd", grid=(4,),
        in_specs=[col(0), col(4), col(8), col(12), col(16), col(0), col(0),
                  pl.BlockSpec((1, 2, nt, LANES, LANES), lambda m: (m, 0, 0, 0, 0)),
                  pl.BlockSpec((4, LANES), lambda m: (0, m)),
                  pl.BlockSpec((1, LANES), lambda m: (0, m))] + [full(c) for c in consts],
        out_specs=[col(0)] * 5 + [pl.BlockSpec((2, LANES), lambda m: (0, m)), pl.BlockSpec((1, LANES), lambda m: (0, m))],
        out_shape=[stream] * 5 + [jax.ShapeDtypeStruct((2, 512), F32), jax.ShapeDtypeStruct((1, 512), F32)],
        scratch_shapes=[pltpu.VMEM((s, LANES), F32), pltpu.VMEM((s, LANES), F32), pltpu.VMEM((s, LANES), F32)],
        compiler_params=_params(1, HGRN_BWD_VMEM),
    )(ph, ph, ph, ph, ph, pre, dout, states, lbl, ng, *consts)


def _branch_out(o, w4):
    o = o.astype(CD)
    return jnp.concatenate([_mm(o, w4[j], NN) for j in range(N_SHARD)], axis=1)


def _mix_out_fwd(x, oa, ob, pg, wa, wb, wo, tm):
    s, d = x.shape

    def body(x_ref, oa_ref, ob_ref, ga_ref, gb_ref, wa_ref, wb_ref, wo_ref, xo_ref):
        ya = _branch_out(oa_ref[...], wa_ref)
        yb = _branch_out(ob_ref[...], wb_ref)
        merged = _sigmoid(ga_ref[...].astype(F32)) * ya + _sigmoid(gb_ref[...].astype(F32)) * yb
        xo_ref[...] = x_ref[...] + _mm(merged, wo_ref[...], NN)

    row = pl.BlockSpec((tm, d), lambda i: (i, 0))
    half = pl.BlockSpec((tm, 512), lambda i: (i, 0))
    full = lambda a: pl.BlockSpec(a.shape, lambda i: (0,) * a.ndim)
    return pl.pallas_call(
        body, name="mix_out_fwd", grid=(s // tm,),
        in_specs=[row, half, half, row, pl.BlockSpec((tm, d), lambda i: (i, 1)), full(wa), full(wb), full(wo)],
        out_specs=row, out_shape=jax.ShapeDtypeStruct((s, d), F32),
        compiler_params=_params(1),
    )(x, oa, ob, pg, pg, wa, wb, wo)


def _mix_out_bwd(dx, oa, ob, pg, wa, wb, wo, tm, after=()):
    s, d = dx.shape
    eq, ebc = _bf(_np_expand_q()), _bf(_np_headsum_spread())

    def body(*refs):
        (dx_ref, oa_ref, ob_ref, ga_ref, gb_ref, wa_ref, wb_ref, wo_ref, eq_ref, ebc_ref,
         dpg_ref, mg_ref, dya_ref, dyb_ref, doe_ref, dl_ref, dob_ref) = refs[len(after):]
        oa = oa_ref[...]
        ya = _branch_out(oa, wa_ref)
        yb = _branch_out(ob_ref[...], wb_ref)
        sa, sb = _sigmoid(ga_ref[...].astype(F32)), _sigmoid(gb_ref[...].astype(F32))
        mg_ref[...] = (sa * ya + sb * yb).astype(CD)
        dm = _mm(dx_ref[...], wo_ref[...], NT)
        dpg_ref[...] = jnp.concatenate([dm * ya * sa * (1.0 - sa), dm * yb * sb * (1.0 - sb)], axis=1).astype(CD)
        dya, dyb = dm * sa, dm * sb
        dya_ref[...] = dya.astype(CD)
        dyb_ref[...] = dyb.astype(CD)
        doa = jnp.zeros(oa.shape, F32)
        dob = jnp.zeros(oa.shape, F32)
        for j in range(N_SHARD):
            doa = doa + _mm(dya[:, 256 * j:256 * j + 256], wa_ref[j], NT)
            dob = dob + _mm(dyb[:, 256 * j:256 * j + 256], wb_ref[j], NT)
        dob_ref[...] = dob
        doe_ref[...] = _mm(doa, eq_ref[...], NN).astype(CD)
        dl_ref[...] = _xdot(doa * oa, ebc_ref[...])

    row = pl.BlockSpec((tm, d), lambda i: (i, 0))
    half = pl.BlockSpec((tm, 512), lambda i: (i, 0))
    full = lambda a: pl.BlockSpec(a.shape, lambda i: (0,) * a.ndim)
    wide = jax.ShapeDtypeStruct((s, d), CD)
    return pl.pallas_call(
        body, name="mix_out_bwd", grid=(s // tm,),
        in_specs=[ANY] * len(after) + [row, half, half, row, pl.BlockSpec((tm, d), lambda i: (i, 1)), full(wa), full(wb),
                                       full(wo), full(eq), full(ebc)],
        out_specs=[pl.BlockSpec((tm, 2048), lambda i: (i, 0)), row, row, row, row, row, half],
        out_shape=[jax.ShapeDtypeStruct((s, 2048), CD), wide, wide, wide, wide, jax.ShapeDtypeStruct((s, d), F32),
                   jax.ShapeDtypeStruct((s, 512), F32)],
        compiler_params=_params(1),
    )(*after, dx, oa, ob, pg, pg, wa, wb, wo, eq, ebc)


def _loss_head(x, g, target, tm):
    s, d = x.shape

    def body(x_ref, g_ref, t_ref, dx_ref, loss_ref, dg_ref):
        @pl.when(pl.program_id(0) == 0)
        def _():
            loss_ref[...] = jnp.zeros_like(loss_ref)
            dg_ref[...] = jnp.zeros_like(dg_ref)

        xv = x_ref[...]
        r = lax.rsqrt(jnp.mean(xv * xv, axis=-1, keepdims=True) + EPS)
        err = xv * r * g_ref[...] - t_ref[...]
        loss_ref[...] += 0.5 * jnp.sum(jnp.mean(err * err, axis=-1, keepdims=True))
        dy = err * (1.0 / d)
        u = dy * g_ref[...]
        dx_ref[...] = r * u - xv * (r * r * r) * jnp.mean(u * xv, axis=-1, keepdims=True)
        dg_ref[...] += jnp.sum(dy * xv * r, axis=0, keepdims=True)

    row = pl.BlockSpec((tm, d), lambda i: (i, 0))
    vec = pl.BlockSpec((1, d), lambda i: (0, 0))
    return pl.pallas_call(
        body, name="loss_head", grid=(s // tm,),
        in_specs=[row, vec, row], out_specs=[row, pl.BlockSpec((8, LANES), lambda i: (0, 0)), vec],
        out_shape=[jax.ShapeDtypeStruct((s, d), F32), jax.ShapeDtypeStruct((8, LANES), F32),
                   jax.ShapeDtypeStruct((1, d), F32)],
        compiler_params=_params(1),
    )(x, g, target)


def _position():
    x, y, c = lax.axis_index("x"), lax.axis_index("y"), lax.axis_index("c")
    return x, y, c, [(1 - x, y), (x, 1 - y), (1 - x, 1 - y)]


def _row_tile(rows, cap=256):
    best = rows
    for cand in range(8, min(rows, cap) + 1, 8):
        if rows % cand == 0:
            best = cand
    return best


def _cast_into_slots(shards, dtypes, me_idx):
    n = len(shards)
    tiles = [_row_tile(s.shape[0]) for s in shards]
    counts = [s.shape[0] // t for s, t in zip(shards, tiles)]
    starts = [sum(counts[:a]) for a in range(n)]

    def body(me_ref, *refs):
        i = pl.program_id(0)
        for a in range(n):
            @pl.when((i >= starts[a]) & (i < starts[a] + counts[a]))
            def _(a=a):
                refs[n + a][0] = refs[a][...].astype(dtypes[a])

    tile_of = [lambda i, a=a: jnp.clip(i - starts[a], 0, counts[a] - 1) for a in range(n)]
    return pl.pallas_call(
        body, name="cast_into_slots",
        grid_spec=pltpu.PrefetchScalarGridSpec(
            num_scalar_prefetch=1, grid=(sum(counts),),
            in_specs=[pl.BlockSpec((tiles[a], shards[a].shape[1]), lambda i, me, a=a: (tile_of[a](i), 0)) for a in range(n)],
            out_specs=[pl.BlockSpec((1, tiles[a], shards[a].shape[1]), lambda i, me, a=a: (me[0], tile_of[a](i), 0))
                       for a in range(n)]),
        out_shape=[jax.ShapeDtypeStruct((N_SHARD,) + s.shape, dt) for s, dt in zip(shards, dtypes)],
        compiler_params=_params(1),
    )(me_idx, *shards)


HBM_SPEC = pl.BlockSpec(memory_space=pltpu.HBM)
SEM_SPEC = pl.BlockSpec(memory_space=pltpu.SEMAPHORE)
DATAFLOW = pltpu.SideEffectType.DATAFLOW_SIDE_EFFECTING


def _exchange_copies(srcs, lands, send, recv, gather):
    x, y, c, chips = _position()
    me = 2 * x + y
    out = []
    for a in range(len(lands)):
        dst = lands[a].at[me]
        if gather and _halved(lands[a]):
            half = lands[a].shape[1] // 2
            dst = lands[a].at[me, pl.ds(c * half, half), :]
        for k, (px, py) in enumerate(chips):
            src = dst if gather else srcs[a].at[2 * px + py]
            out.append(pltpu.make_async_remote_copy(src_ref=src, dst_ref=dst, send_sem=send.at[3 * a + k],
                                                    recv_sem=recv.at[3 * a + k], device_id=(px, py, c), device_id_type=MESH))
    return out


def _halved(land):
    return land.shape[1] % 32 == 0


def _pair_fill(name, lands):
    n = len(lands)

    def body(*refs):
        src, dst = refs[:n], refs[n:2 * n]
        send, recv = refs[2 * n:]
        x, y, c, chips = _position()
        copies = []
        for a in range(n):
            half = src[a].shape[1] // 2
            for k, (px, py) in enumerate(chips):
                rows = (2 * px + py, pl.ds(c * half, half), slice(None))
                cp = pltpu.make_async_remote_copy(src_ref=src[a].at[rows], dst_ref=dst[a].at[rows], send_sem=send.at[a, k],
                                                  recv_sem=recv.at[a, k], device_id=(x, y, 1 - c), device_id_type=MESH)
                cp.start()
                copies.append(cp)
        for cp in copies:
            cp.wait()

    return pl.pallas_call(
        body, name=name, in_specs=[ANY] * n, out_specs=[ANY] * n,
        out_shape=[jax.ShapeDtypeStruct(l.shape, l.dtype) for l in lands],
        input_output_aliases={a: a for a in range(n)},
        scratch_shapes=[pltpu.SemaphoreType.DMA((n, 3)), pltpu.SemaphoreType.DMA((n, 3))],
    )(*lands)


def _exchange_start(name, srcs, lands, after):
    ns, nl, na = len(srcs), len(lands), len(after)
    gather = ns == 0

    def body(*refs):
        src_refs, land_refs = refs[:ns], refs[ns:ns + nl]
        send, recv = refs[ns + nl + na], refs[ns + nl + na + 1]
        token = refs[-1]
        for cp in _exchange_copies(src_refs, land_refs, send, recv, gather):
            cp.start()
        token[...] = jnp.zeros_like(token)

    arrays = [pltpu.with_memory_space_constraint(a, pltpu.HBM) for a in list(srcs) + list(lands)]
    outs = pl.pallas_call(
        body, name=name,
        out_shape=(pltpu.SemaphoreType.DMA((3 * nl,)), pltpu.SemaphoreType.DMA((3 * nl,)),
                   *[pltpu.HBM(a.shape, a.dtype) for a in arrays], jax.ShapeDtypeStruct((8, LANES), F32)),
        in_specs=[HBM_SPEC] * (ns + nl) + [ANY] * na,
        out_specs=(SEM_SPEC, SEM_SPEC, *[HBM_SPEC] * (ns + nl), pl.BlockSpec(memory_space=pltpu.VMEM)),
        input_output_aliases={i: 2 + i for i in range(ns + nl)},
        compiler_params=pltpu.CompilerParams(has_side_effects=DATAFLOW),
    )(*arrays, *after)
    return outs[0], outs[1], list(outs[2:2 + ns]), list(outs[2 + ns:2 + ns + nl]), outs[-1]


def _exchange_wait(name, send, recv, srcs, lands, after):
    ns, nl, na = len(srcs), len(lands), len(after)
    gather = ns == 0

    def body(*refs):
        src_refs, land_refs = refs[:ns], refs[ns:ns + nl]
        send_ref, recv_ref = refs[ns + nl], refs[ns + nl + 1]
        for cp in _exchange_copies(src_refs, land_refs, send_ref, recv_ref, gather):
            cp.wait_send()
            cp.wait_recv()

    outs = pl.pallas_call(
        body, name=name,
        out_shape=tuple(pltpu.HBM(a.shape, a.dtype) for a in list(srcs) + list(lands)),
        in_specs=[HBM_SPEC] * (ns + nl) + [SEM_SPEC, SEM_SPEC] + [ANY] * na,
        out_specs=tuple([HBM_SPEC] * (ns + nl)),
        input_output_aliases={i: i for i in range(ns + nl)},
        compiler_params=pltpu.CompilerParams(has_side_effects=DATAFLOW),
    )(*srcs, *lands, send, recv, *after)
    return list(outs[ns:])


def _pair_exchange(grads):
    n = len(grads)

    def body(*refs):
        src, dst = refs[:n], refs[n:2 * n]
        send, recv = refs[2 * n:]
        x, y, c, _ = _position()
        copies = []
        for a in range(n):
            half = src[a].shape[1] // 2
            cp = pltpu.make_async_remote_copy(
                src_ref=src[a].at[:, pl.ds((1 - c) * half, half), :], dst_ref=dst[a], send_sem=send.at[a],
                recv_sem=recv.at[a], device_id=(x, y, 1 - c), device_id_type=MESH)
            cp.start()
            copies.append(cp)
        for cp in copies:
            cp.wait()

    return pl.pallas_call(
        body, name="grad_pair_exchange", in_specs=[ANY] * n, out_specs=[ANY] * n,
        out_shape=[jax.ShapeDtypeStruct((g.shape[0], g.shape[1] // 2, g.shape[2]), g.dtype) for g in grads],
        scratch_shapes=[pltpu.SemaphoreType.DMA((n,)), pltpu.SemaphoreType.DMA((n,))],
    )(*grads)


def _shard_of(a):
    return lambda i: jnp.clip(i - a * N_SHARD, 0, N_SHARD - 1)


def _pair_sum(gs, gots, c_idx, me_idx):
    n = len(gs)
    halves = [(g.shape[1] // 2, g.shape[2]) for g in gs]

    def body(c_ref, me_ref, *refs):
        g_refs, got_refs, s_refs, own_refs = (refs[k * n:(k + 1) * n] for k in range(4))
        i = pl.program_id(0)
        for a in range(n):
            @pl.when(i // N_SHARD == a)
            def _(a=a):
                sm = g_refs[a][...] + got_refs[a][...].astype(F32)
                s_refs[a][...] = sm.astype(CD)

                @pl.when(i % N_SHARD == me_ref[0])
                def _():
                    own_refs[a][...] = sm[0]

    shard = [_shard_of(a) for a in range(n)]
    return pl.pallas_call(
        body, name="grad_pair_sum",
        grid_spec=pltpu.PrefetchScalarGridSpec(
            num_scalar_prefetch=2, grid=(n * N_SHARD,),
            in_specs=[pl.BlockSpec((1, h, c_), lambda i, c, me, a=a: (shard[a](i), c[0], 0)) for a, (h, c_) in enumerate(halves)]
            + [pl.BlockSpec((1, h, c_), lambda i, c, me, a=a: (shard[a](i), 0, 0)) for a, (h, c_) in enumerate(halves)],
            out_specs=[pl.BlockSpec((1, h, c_), lambda i, c, me, a=a: (shard[a](i), 0, 0)) for a, (h, c_) in enumerate(halves)]
            + [pl.BlockSpec((h, c_), lambda i, c, me: (0, 0)) for h, c_ in halves]),
        out_shape=[jax.ShapeDtypeStruct((N_SHARD, h, c_), CD) for h, c_ in halves]
        + [jax.ShapeDtypeStruct((h, c_), F32) for h, c_ in halves],
        compiler_params=_params(1),
    )(c_idx, me_idx, *gs, *gots)


def _chip_sum(owns, gots, me_idx):
    n = len(owns)
    tiles = [_row_tile(o.shape[0]) for o in owns]
    counts = [o.shape[0] // t for o, t in zip(owns, tiles)]
    starts = [sum(counts[:a]) for a in range(n)]

    def body(me_ref, *refs):
        own_refs, got_refs, out_refs = (refs[k * n:(k + 1) * n] for k in range(3))
        i = pl.program_id(0)
        for a in range(n):
            @pl.when((i >= starts[a]) & (i < starts[a] + counts[a]))
            def _(a=a):
                total = None
                for j in range(N_SHARD):
                    term = jnp.where(j == me_ref[0], own_refs[a][...], got_refs[a][j].astype(F32))
                    total = term if total is None else total + term
                out_refs[a][...] = total

    tile_of = [lambda i, a=a: jnp.clip(i - starts[a], 0, counts[a] - 1) for a in range(n)]
    own_specs = [pl.BlockSpec((tiles[a], owns[a].shape[1]), lambda i, me, a=a: (tile_of[a](i), 0)) for a in range(n)]
    return pl.pallas_call(
        body, name="grad_chip_sum",
        grid_spec=pltpu.PrefetchScalarGridSpec(
            num_scalar_prefetch=1, grid=(sum(counts),),
            in_specs=own_specs + [pl.BlockSpec((N_SHARD, tiles[a], owns[a].shape[1]), lambda i, me, a=a: (0, tile_of[a](i), 0))
                                  for a in range(n)],
            out_specs=own_specs),
        out_shape=[jax.ShapeDtypeStruct(o.shape, F32) for o in owns],
        compiler_params=_params(1),
    )(me_idx, *owns, *gots)


def _pair_share(halves):
    n = len(halves)

    def body(*refs):
        src, dst = refs[:n], refs[n:2 * n]
        send, recv = refs[2 * n:]
        x, y, c, _ = _position()
        copies = []
        for a in range(n):
            cp = pltpu.make_async_remote_copy(src_ref=src[a], dst_ref=dst[a], send_sem=send.at[a],
                                              recv_sem=recv.at[a], device_id=(x, y, 1 - c), device_id_type=MESH)
            cp.start()
            copies.append(cp)
        for cp in copies:
            cp.wait()

    return pl.pallas_call(
        body, name="grad_pair_share", in_specs=[ANY] * n, out_specs=[ANY] * n,
        out_shape=[jax.ShapeDtypeStruct(h.shape, h.dtype) for h in halves],
        scratch_shapes=[pltpu.SemaphoreType.DMA((n,)), pltpu.SemaphoreType.DMA((n,))],
    )(*halves)


def _small_allreduce(buf):
    rows, cols = buf.shape

    def body(src_ref, out_ref, slots, send, recv):
        x, y, c, _ = _position()
        me = 4 * x + 2 * y + c
        slots[me] = src_ref[...]
        copies = []
        k = 0
        for dx in (0, 1):
            for dy in (0, 1):
                for dc in (0, 1):
                    if (dx, dy, dc) == (0, 0, 0):
                        continue
                    peer = (jnp.where(dx, 1 - x, x), jnp.where(dy, 1 - y, y), jnp.where(dc, 1 - c, c))
                    cp = pltpu.make_async_remote_copy(src_ref=src_ref, dst_ref=slots.at[me], send_sem=send.at[k],
                                                      recv_sem=recv.at[k], device_id=peer, device_id_type=MESH)
                    cp.start()
                    copies.append(cp)
                    k += 1
        for cp in copies:
            cp.wait()
        total = slots[0]
        for dev in range(1, N_DEV):
            total = total + slots[dev]
        out_ref[...] = total

    vm = pl.BlockSpec(memory_space=pltpu.VMEM)
    return pl.pallas_call(
        body, name="small_allreduce", in_specs=[vm], out_specs=vm,
        out_shape=jax.ShapeDtypeStruct((rows, cols), F32),
        scratch_shapes=[pltpu.VMEM((N_DEV, rows, cols), F32), pltpu.SemaphoreType.DMA((N_DEV - 1,)),
                        pltpu.SemaphoreType.DMA((N_DEV - 1,))],
    )(buf)


def _adamw_math(w, gv, m, v):
    mn = ADAM_B1 * m + (1.0 - ADAM_B1) * gv
    vn = ADAM_B2 * v + (1.0 - ADAM_B2) * (gv * gv)
    m_hat = mn / (1.0 - ADAM_B1 ** ADAM_STEP)
    v_hat = vn / (1.0 - ADAM_B2 ** ADAM_STEP)
    return -ADAM_LR * (m_hat / (jnp.sqrt(v_hat) + ADAM_EPS) + ADAM_WD * w), mn, vn


def _adamw(w, g, m, v):
    rows, cols = w.shape
    tr = _row_tile(rows)

    def body(w_ref, g_ref, m_ref, v_ref, d_ref, mo_ref, vo_ref):
        d_ref[...], mo_ref[...], vo_ref[...] = _adamw_math(w_ref[...], g_ref[...], m_ref[...], v_ref[...])

    blk = pl.BlockSpec((tr, cols), lambda i: (i, 0))
    shp = jax.ShapeDtypeStruct((rows, cols), F32)
    return pl.pallas_call(
        body, name="adamw", grid=(rows // tr,), in_specs=[blk] * 4, out_specs=[blk] * 3, out_shape=[shp] * 3,
        compiler_params=_params(1),
    )(w, g, m, v)


def _adamw_halves(w, own, got, m, v, c_idx):
    rows, cols = w.shape
    tr = _row_tile(rows // 2)
    per_half = rows // 2 // tr

    def body(c_ref, w_ref, own_ref, got_ref, m_ref, v_ref, d_ref, mo_ref, vo_ref, g_ref):
        mine = (pl.program_id(0) // per_half) == c_ref[0]
        gv = jnp.where(mine, own_ref[...], got_ref[...])
        g_ref[...] = gv
        d_ref[...], mo_ref[...], vo_ref[...] = _adamw_math(w_ref[...], gv, m_ref[...], v_ref[...])

    blk = pl.BlockSpec((tr, cols), lambda i, c: (i, 0))
    own_blk = pl.BlockSpec((tr, cols), lambda i, c: (jnp.where(i // per_half == c[0], i % per_half, 0), 0))
    got_blk = pl.BlockSpec((tr, cols), lambda i, c: (jnp.where(i // per_half == c[0], 0, i % per_half), 0))
    shp = jax.ShapeDtypeStruct((rows, cols), F32)
    return pl.pallas_call(
        body, name="adamw_halves",
        grid_spec=pltpu.PrefetchScalarGridSpec(num_scalar_prefetch=1, grid=(rows // tr,),
                                               in_specs=[blk, own_blk, got_blk, blk, blk], out_specs=[blk] * 4),
        out_shape=[shp] * 4, compiler_params=_params(1),
    )(c_idx, w, own, got, m, v)


def _local_step(x, target, norm_gains, q_g, k_g, ng, weights_of, grads_done):
    s = x.shape[0]
    tm = min(512, s)
    tq = min(256, s)
    tf = min(1024, s)
    g1, gm, g2, gf = norm_gains
    cos2, sin2 = _rope_tables(s)
    gq8 = jnp.tile(q_g, (1, 8))
    gk2 = jnp.tile(k_g, (1, 2))

    tn = min(256, s)
    tk = min(1024, s)
    w1 = weights_of(1, ())
    x1, s1, t1, b1, h1 = _ffn_fwd(x, g1, w1["g1"], w1["u1"], w1["d1"], tf)
    w2 = weights_of(2, (x1,))
    lbl = w2["lbl"]
    pqkv, ph, pg, hm = _mix_in_fwd(x1, gm, w2["in"], tn)
    qe, kr, vr, vs = _qk_prep(pqkv, gq8, gk2, cos2, sin2, tm)
    oa, lse = _attn_fwd(qe, kr, vr, vs, tq)
    ob, pre, hstates = _hgrn_fwd(ph, lbl, ng)
    x2 = _mix_out_fwd(x1, oa, ob, pg, w2["a"], w2["b"], w2["o"], tm)
    w3 = weights_of(3, (x2,))
    x3, s2, t2, b2, h2 = _ffn_fwd(x2, g2, w3["g2"], w3["u2"], w3["d2"], tf)
    dx3, loss, dgf = _loss_head(x3, gf, target, tm)

    dx2, da2, db2, f2, dg2, dx3c = _ffn_bwd(dx3, x2, g2, s2, t2, b2, w3["g2"], w3["u2"], w3["d2"], tm)
    tok = grads_done(3, dict(g2=_dw_shared_b("dw_gate", da2, h2, tk, 1.0), u2=_dw_shared_b("dw_gate", db2, h2, tk, 1.0),
                             d2=_dw_shared_b("dw_down", f2, dx3c, tk, 1.0)))

    dpg, mg, dya, dyb, doe, delta, dob = _mix_out_bwd(dx2, oa, ob, pg, w2["a"], w2["b"], w2["o"], tm, tok)
    g_o = [g.reshape(N_SHARD, D_MODEL // N_SHARD, D_MODEL) for g in _dw_colblocks("dw_out", mg, dx2, 1, tk)]
    g_a = _dw_colblocks("dw_branch", oa, dya, N_SHARD, tk)
    g_b = _dw_colblocks("dw_branch", ob, dyb, N_SHARD, tk)
    dqt, dkt, dvt = _attn_bwd(qe, kr, kr.T, vr, doe, delta, lse, tq)
    dqkv, dgq, dgk = _qk_prep_bwd(pqkv, dqt.T, dkt.T, dvt.T, gq8, gk2, cos2, sin2, tm)
    dhq, dhff, dhfb, dhi, dhg, dlb, dng = _hgrn_bwd(ph, pre, dob, hstates, lbl, ng)
    dps = (dqkv, dhq, dhff, dhfb, dhi, dhg, dpg)
    g_in = [g.reshape(N_SHARD, -1, D_MODEL) for g in _dw_in(dps, hm, tk)]
    tok = grads_done(2, {"in": g_in, "a": g_a, "b": g_b, "o": g_o})
    dx1, dgm = _mix_in_bwd(dps, w2["in"], x1, dx2, gm, tn, tok)

    dx0, da1, db1, f1, dg1, dx1c = _ffn_bwd(dx1, x, g1, s1, t1, b1, w1["g1"], w1["u1"], w1["d1"], tm)
    grads_done(1, dict(g1=_dw_shared_b("dw_gate", da1, h1, tk, 1.0), u1=_dw_shared_b("dw_gate", db1, h1, tk, 1.0),
                       d1=_dw_shared_b("dw_down", f1, dx1c, tk, 1.0)))
    small = dict(g1=dg1, gm=dgm, g2=dg2, gf=dgf, gq=dgq, gk=dgk, lb=dlb, ng=dng)
    return loss, dx0, small, lbl


GROUPS = {1: ("g1", "u1", "d1"), 2: ("in", "a", "b", "o"), 3: ("g2", "u2", "d2")}
BIG = GROUPS[1] + GROUPS[2] + GROUPS[3]
TRANSPOSED = ("g1", "u1", "in", "g2", "u2")


def _pack_rows(vectors, width):
    rows = []
    for vct in vectors:
        flat = vct.reshape(-1)
        pad = (-flat.shape[0]) % width
        rows.append(jnp.pad(flat, (0, pad)).reshape(-1, width))
    return jnp.concatenate(rows, axis=0)


def kernel(x, ffn1_norm_g, ffn1_w_gate, ffn1_w_up, ffn1_w_down, mix_norm_g, w_in, q_norm_g, k_norm_g, hgrn_lb_logits, hgrn_out_norm_g, w_branch_attn, w_branch_hgrn, w_out, ffn2_norm_g, ffn2_w_gate, ffn2_w_up, ffn2_w_down, final_norm_g, loss_target, m_ffn1_norm_g, m_ffn1_w_gate, m_ffn1_w_up, m_ffn1_w_down, m_mix_norm_g, m_w_in, m_q_norm_g, m_k_norm_g, m_hgrn_lb_logits, m_hgrn_out_norm_g, m_w_branch_attn, m_w_branch_hgrn, m_w_out, m_ffn2_norm_g, m_ffn2_w_gate, m_ffn2_w_up, m_ffn2_w_down, m_final_norm_g, v_ffn1_norm_g, v_ffn1_w_gate, v_ffn1_w_up, v_ffn1_w_down, v_mix_norm_g, v_w_in, v_q_norm_g, v_k_norm_g, v_hgrn_lb_logits, v_hgrn_out_norm_g, v_w_branch_attn, v_w_branch_hgrn, v_w_out, v_ffn2_norm_g, v_ffn2_w_gate, v_ffn2_w_up, v_ffn2_w_down, v_final_norm_g):
    xi, yi, ci = lax.axis_index("x"), lax.axis_index("y"), lax.axis_index("c")
    me = 2 * xi + yi
    c_idx = jnp.reshape(ci, (1,)).astype(jnp.int32)
    me_idx = jnp.reshape(me, (1,)).astype(jnp.int32)

    big_w = dict(g1=ffn1_w_gate[0], u1=ffn1_w_up[0], d1=ffn1_w_down[0], a=w_branch_attn[0], b=w_branch_hgrn[0],
                 o=w_out[0], g2=ffn2_w_gate[0], u2=ffn2_w_up[0], d2=ffn2_w_down[0])
    big_w["in"] = w_in[0]
    big_m = dict(g1=m_ffn1_w_gate[0], u1=m_ffn1_w_up[0], d1=m_ffn1_w_down[0], a=m_w_branch_attn[0], b=m_w_branch_hgrn[0],
                 o=m_w_out[0], g2=m_ffn2_w_gate[0], u2=m_ffn2_w_up[0], d2=m_ffn2_w_down[0])
    big_m["in"] = m_w_in[0]
    big_v = dict(g1=v_ffn1_w_gate[0], u1=v_ffn1_w_up[0], d1=v_ffn1_w_down[0], a=v_w_branch_attn[0], b=v_w_branch_hgrn[0],
                 o=v_w_out[0], g2=v_ffn2_w_gate[0], u2=v_ffn2_w_up[0], d2=v_ffn2_w_down[0])
    big_v["in"] = v_w_in[0]
    for table in (big_w, big_m, big_v):
        for n in TRANSPOSED:
            table[n] = table[n].T

    started, token = {}, ()
    for grp in (1, 2, 3):
        shards = [big_w[n] for n in GROUPS[grp]] + ([hgrn_lb_logits.reshape(4, LANES)] if grp == 2 else [])
        dtypes = [CD] * len(GROUPS[grp]) + ([F32] if grp == 2 else [])
        lands = _cast_into_slots(shards, dtypes, me_idx)
        send, recv, _, lands, tok = _exchange_start("gather%d_start" % grp, [], lands, token)
        started[grp], token = (send, recv, lands), (tok,)

    def weights_of(grp, after):
        send, recv, lands = started[grp]
        got = _exchange_wait("gather%d_wait" % grp, send, recv, [], lands, tuple(after) + (token if grp == 1 else ()))
        by_halves = [i for i, land in enumerate(got) if _halved(land)]
        for i, whole in zip(by_halves, _pair_fill("gather%d_fill" % grp, [got[i] for i in by_halves])):
            got[i] = whole
        w = dict(zip(GROUPS[grp], got))
        if grp == 2:
            w["in"] = w["in"].reshape(-1, D_MODEL)
            w["o"] = w["o"].reshape(D_MODEL, D_MODEL)
            w["lbl"] = jnp.transpose(got[-1], (1, 0, 2)).reshape(4, N_SHARD * LANES)
        return w

    pending = {}

    def grads_done(grp, grads):
        names = list(grads)
        got = _pair_exchange([grads[n][1] for n in names])
        res = _pair_sum([grads[n][0] for n in names], got, c_idx, me_idx)
        sums, owns = res[:len(names)], res[len(names):]
        lands = [lax.empty(s_.shape, s_.dtype) for s_ in sums]
        send, recv, srcs, lands, tok = _exchange_start("reduce%d_start" % grp, list(sums), lands, ())
        pending[grp] = (names, send, recv, srcs, lands, owns, tok)
        return (tok,)

    def reduced_halves(grp, after):
        names, send, recv, srcs, lands, owns, _ = pending[grp]
        parts = _exchange_wait("reduce%d_wait" % grp, send, recv, srcs, lands, after)
        return names, list(_chip_sum(list(owns), parts, me_idx))

    loss, dx, small, lbl = _local_step(
        x[0], loss_target[0], (ffn1_norm_g, mix_norm_g, ffn2_norm_g, final_norm_g.reshape(1, -1)),
        q_norm_g, k_norm_g, hgrn_out_norm_g, weights_of, grads_done)

    dgq = small["gq"].reshape(8, HEAD_DIM).sum(axis=0)
    dgk = small["gk"].reshape(2, HEAD_DIM).sum(axis=0)
    lb_full = _hgrn_lower_bounds(lbl)
    dlog = []
    for d in (0, 1):
        t = small["lb"][d:d + 1] * lb_full[d] * (1.0 - lb_full[d])
        dlog += [t, -t]
    small_list = [small["g1"], small["gm"], small["g2"], small["gf"], small["ng"], dgq, dgk, jnp.concatenate(dlog, axis=0), loss[0, 0]]
    packed = _pack_rows(small_list, D_MODEL)
    n_rows = packed.shape[0]
    packed = jnp.pad(packed, ((0, (-n_rows) % 8), (0, 0)))
    red = _small_allreduce(packed)
    loss_out = red[n_rows - 1, 0]
    sg = dict(g1=red[0:1], gm=red[1:2], g2=red[2:3], gf=red[3], ng=red[4:5, :512], gq=red[5:6, :HEAD_DIM],
              gk=red[6:7, :HEAD_DIM])
    dlog_full = red[7:9].reshape(2, 2, 512)
    sg["lb"] = lax.dynamic_slice_in_dim(dlog_full, me * LANES, LANES, axis=2)

    small_w = dict(g1=ffn1_norm_g, gm=mix_norm_g, g2=ffn2_norm_g, gf=final_norm_g, ng=hgrn_out_norm_g, gq=q_norm_g,
                   gk=k_norm_g, lb=hgrn_lb_logits)
    small_m = dict(g1=m_ffn1_norm_g, gm=m_mix_norm_g, g2=m_ffn2_norm_g, gf=m_final_norm_g, ng=m_hgrn_out_norm_g,
                   gq=m_q_norm_g, gk=m_k_norm_g, lb=m_hgrn_lb_logits)
    small_v = dict(g1=v_ffn1_norm_g, gm=v_mix_norm_g, g2=v_ffn2_norm_g, gf=v_final_norm_g, ng=v_hgrn_out_norm_g,
                   gq=v_q_norm_g, gk=v_k_norm_g, lb=v_hgrn_lb_logits)
    small_names = ("g1", "gm", "g2", "gf", "ng", "gq", "gk", "lb")
    pack = lambda dct: _pack_rows([dct[n] for n in small_names], D_MODEL)
    pw, pgr, pm, pv = pack(small_w), pack(sg), pack(small_m), pack(small_v)
    pad8 = lambda a: jnp.pad(a, ((0, (-a.shape[0]) % 8), (0, 0)))
    sd, sm_, sv_ = _adamw(pad8(pw), pad8(pgr), pad8(pm), pad8(pv))

    def unpack(buf):
        out, r = {}, 0
        for n in small_names:
            size = small_w[n].size
            nr = -(-size // D_MODEL)
            out[n] = buf[r:r + nr].reshape(-1)[:size].reshape(small_w[n].shape)
            r += nr
        return out

    sdelta, snew_m, snew_v = unpack(sd), unpack(sm_), unpack(sv_)
    sgrad = {n: sg[n].reshape(small_w[n].shape) for n in small_names}

    bdelta, bnew_m, bnew_v, bgrad = {}, {}, {}, {}

    def update(names, halves):
        for n, own, got in zip(names, halves, _pair_share(halves)):
            res = _adamw_halves(big_w[n], own, got, big_m[n], big_v[n], c_idx)
            if n in TRANSPOSED:
                res = [r.T for r in res]
            bdelta[n], bnew_m[n], bnew_v[n], bgrad[n] = [r[None] for r in res]

    names3, halves3 = reduced_halves(3, (pending[1][-1],))
    names2, halves2 = reduced_halves(2, (halves3[0],))
    update(names3 + names2, halves3 + halves2)
    names1, halves1 = reduced_halves(1, (bdelta[names2[-1]],))
    update(names1, halves1)

    order = [("s", "g1"), ("b", "g1"), ("b", "u1"), ("b", "d1"), ("s", "gm"), ("b", "in"), ("s", "gq"), ("s", "gk"),
             ("s", "lb"), ("s", "ng"), ("b", "a"), ("b", "b"), ("b", "o"), ("s", "g2"), ("b", "g2"), ("b", "u2"),
             ("b", "d2"), ("s", "gf")]
    outs = [loss_out, dx[None]]
    for table_s, table_b in ((sgrad, bgrad), (sdelta, bdelta), (snew_m, bnew_m), (snew_v, bnew_v)):
        outs += [(table_s if kind == "s" else table_b)[n] for kind, n in order]
    return tuple(outs)
```

```python
import functools

import numpy as np
import jax
import jax.numpy as jnp
from jax import lax
from jax.experimental import pallas as pl
from jax.experimental.pallas import tpu as pltpu

F32 = jnp.float32
BF16 = jnp.bfloat16
CD = jnp.bfloat16

EPS = 1e-6
D_MODEL = 1024
HEAD_DIM = 64
GRID_W = 64
ROPE_THETA = 10000.0
CHUNK = 32
N_SHARD = 4
N_DEV = 8
VMEM_LIMIT = 56 * 1024 * 1024
HGRN_BWD_VMEM = 60 * 1024 * 1024
LANES = 128
HG_TILE = 128
ATTN_BWD_HEADS = 2
FFN_ROWS = 256

ADAM_LR = 0.001
ADAM_B1 = 0.9
ADAM_B2 = 0.999
ADAM_EPS = 1e-08
ADAM_WD = 0.01
ADAM_STEP = 10

NN = (((1,), (0,)), ((), ()))
NT = (((1,), (1,)), ((), ()))
TN = (((0,), (0,)), ((), ()))
MESH = pl.DeviceIdType.MESH
ANY = pl.BlockSpec(memory_space=pl.ANY)


def _mm(a, b, dn):
    return lax.dot_general(a.astype(CD), b.astype(CD), dn, preferred_element_type=F32)


def _split3(x):
    hi = x.astype(BF16)
    r = x - hi.astype(F32)
    mid = r.astype(BF16)
    lo = (r - mid.astype(F32)).astype(BF16)
    return hi, mid, lo


def _xdot(x, m):
    rows = x.shape[0]
    hi, mid, _ = _split3(x)
    r = lax.dot_general(jnp.concatenate([hi, mid], axis=0), m, NN, preferred_element_type=F32)
    return r[:rows] + r[rows:]


def _xdot_l(m, x):
    cols = x.shape[1]
    hi, mid, _ = _split3(x)
    r = lax.dot_general(m, jnp.concatenate([hi, mid], axis=1), NN, preferred_element_type=F32)
    return r[:, :cols] + r[:, cols:]


def _params(n_grid, vmem_limit=VMEM_LIMIT):
    return pltpu.CompilerParams(dimension_semantics=("arbitrary",) * n_grid, vmem_limit_bytes=vmem_limit)


def _sigmoid(x):
    return jax.nn.sigmoid(x)


def _np_blocksum(n):
    i = np.arange(n)
    return (i[:, None] // HEAD_DIM == i[None, :] // HEAD_DIM).astype(np.float32)


def _np_swap32(n):
    i = np.arange(n)
    partner = np.where(i % HEAD_DIM < HEAD_DIM // 2, i + HEAD_DIM // 2, i - HEAD_DIM // 2)
    m = np.zeros((n, n), np.float32)
    m[i, partner] = 1.0
    return m


def _np_expand_q():
    m = np.zeros((512, 1024), np.float32)
    for h in range(8):
        g = h // 4
        for d in range(HEAD_DIM):
            m[64 * h + d, 128 * h + 64 * g + d] = 1.0
    return m


def _np_headsum_spread():
    m = np.zeros((512, 1024), np.float32)
    for h in range(8):
        m[64 * h:64 * h + 64, 128 * h:128 * h + 128] = 1.0
    return m


def _np_swap_halves():
    m = np.zeros((128, 128), np.float32)
    i = np.arange(128)
    m[i, (i + 64) % 128] = 1.0
    return m


def _np_hgrn_cums(t, rev):
    r = np.arange(t)[:, None]
    c = np.arange(t)[None, :]
    same = (r // CHUNK) == (c // CHUNK)
    if not rev:
        cum = same & (c <= r)
        mid = same & (c % CHUNK <= CHUNK // 2 - 1)
    else:
        cum = same & (c >= r)
        mid = same & (c % CHUNK >= CHUNK // 2)
    return np.concatenate([cum, mid, same], axis=0).astype(np.float32)


def _bf(a):
    return jnp.asarray(a, dtype=BF16)


def _rope_tables(seq_len):
    rows = seq_len // GRID_W
    row = jnp.repeat(jnp.arange(rows, dtype=F32), GRID_W)
    col = jnp.tile(jnp.arange(GRID_W, dtype=F32), rows)
    n_freq = HEAD_DIM // 4
    inv = ROPE_THETA ** (-jnp.arange(n_freq, dtype=F32) / n_freq)
    ang = jnp.concatenate([row[:, None] * inv, col[:, None] * inv], axis=-1)
    cos, sin = jnp.cos(ang), jnp.sin(ang)
    c64 = jnp.concatenate([cos, cos], axis=-1)
    s64 = jnp.concatenate([-sin, sin], axis=-1)
    return jnp.tile(c64, (1, 2)), jnp.tile(s64, (1, 2))


def _ffn_fwd(x, g, wg, wu, wd, tm):
    s, d = x.shape
    nsh, fs, _ = wg.shape

    def body(x_ref, g_ref, wg_ref, wu_ref, wd_ref, xo_ref, a_ref, da_ref, b_ref, hb_ref, acc, hs):
        j = pl.program_id(1)

        @pl.when(j == 0)
        def _():
            xv = x_ref[...]
            r = lax.rsqrt(jnp.mean(xv * xv, axis=-1, keepdims=True) + EPS)
            h = (xv * r * g_ref[...]).astype(CD)
            hs[...] = h
            hb_ref[...] = h
            acc[...] = jnp.zeros_like(acc)

        blocks = [slice(r0, min(r0 + FFN_ROWS, tm)) for r0 in range(0, tm, FFN_ROWS)]
        firsts = [(_mm(hs[rows, :], wg_ref[0], NT), _mm(hs[rows, :], wu_ref[0], NT)) for rows in blocks]
        for rows, (a, b) in zip(blocks, firsts):
            sg = _sigmoid(a)
            silu = a * sg
            acc[rows, :] += _mm(silu * b, wd_ref[0], NN)
            a_ref[0, rows, :] = silu.astype(CD)
            da_ref[0, rows, :] = (sg * (1.0 + a * (1.0 - sg))).astype(CD)
            b_ref[0, rows, :] = b.astype(CD)

        @pl.when(j == nsh - 1)
        def _():
            xo_ref[...] = x_ref[...] + 0.5 * acc[...]

    return pl.pallas_call(
        body, name="ffn_fwd", grid=(s // tm, nsh),
        in_specs=[pl.BlockSpec((tm, d), lambda i, j: (i, 0)), pl.BlockSpec((1, d), lambda i, j: (0, 0))]
        + [pl.BlockSpec((1, fs, d), lambda i, j: (j, 0, 0))] * 3,
        out_specs=[pl.BlockSpec((tm, d), lambda i, j: (i, 0))] + [pl.BlockSpec((1, tm, fs), lambda i, j: (j, i, 0))] * 3
        + [pl.BlockSpec((tm, d), lambda i, j: (i, 0))],
        out_shape=[jax.ShapeDtypeStruct((s, d), F32)] + [jax.ShapeDtypeStruct((nsh, s, fs), CD)] * 3
        + [jax.ShapeDtypeStruct((s, d), CD)],
        scratch_shapes=[pltpu.VMEM((tm, d), F32), pltpu.VMEM((tm, d), CD)],
        compiler_params=_params(2),
    )(x, g, wg, wu, wd)


def _ffn_bwd(dout, x, g, silu, dsilu, b, wg, wu, wd, tm):
    s, d = x.shape
    nsh, fs, _ = wg.shape

    def body(do_ref, x_ref, g_ref, sl_ref, ds_ref, b_ref, wg_ref, wu_ref, wd_ref,
             dx_ref, da_ref, db_ref, f_ref, dg_ref, do16_ref, dh):
        i = pl.program_id(0)
        j = pl.program_id(1)

        @pl.when(j == 0)
        def _():
            dh[...] = jnp.zeros_like(dh)
            do16_ref[...] = (0.5 * do_ref[...]).astype(CD)

        @pl.when((i == 0) & (j == 0))
        def _():
            dg_ref[...] = jnp.zeros_like(dg_ref)

        blocks = [slice(r0, min(r0 + FFN_ROWS, tm)) for r0 in range(0, tm, FFN_ROWS)]
        dfs = [_mm(do16_ref[rows, :], wd_ref[0], NT) for rows in blocks]
        das, dbs = [], []
        for rows, df in zip(blocks, dfs):
            sl = sl_ref[0, rows, :].astype(F32)
            bv = b_ref[0, rows, :].astype(F32)
            da = (df * bv * ds_ref[0, rows, :].astype(F32)).astype(CD)
            db = (df * sl).astype(CD)
            da_ref[0, rows, :] = da
            db_ref[0, rows, :] = db
            f_ref[0, rows, :] = (sl * bv).astype(CD)
            das.append(da)
            dbs.append(db)
        for rows, da, db in zip(blocks, das, dbs):
            dh[rows, :] += _mm(da, wg_ref[0], NN) + _mm(db, wu_ref[0], NN)

        @pl.when(j == nsh - 1)
        def _():
            xv = x_ref[...]
            r = lax.rsqrt(jnp.mean(xv * xv, axis=-1, keepdims=True) + EPS)
            dhv = dh[...]
            u = dhv * g_ref[...]
            dx_ref[...] = do_ref[...] + r * u - xv * (r * r * r) * jnp.mean(u * xv, axis=-1, keepdims=True)
            dg_ref[...] += jnp.sum(dhv * xv * r, axis=0, keepdims=True)

    act = pl.BlockSpec((1, tm, fs), lambda i, j: (j, i, 0))
    row = pl.BlockSpec((tm, d), lambda i, j: (i, 0))
    return pl.pallas_call(
        body, name="ffn_bwd", grid=(s // tm, nsh),
        in_specs=[row, row, pl.BlockSpec((1, d), lambda i, j: (0, 0)), act, act, act]
        + [pl.BlockSpec((1, fs, d), lambda i, j: (j, 0, 0))] * 3,
        out_specs=[row, act, act, act, pl.BlockSpec((1, d), lambda i, j: (0, 0)), row],
        out_shape=[jax.ShapeDtypeStruct((s, d), F32), jax.ShapeDtypeStruct((nsh, s, fs), CD),
                   jax.ShapeDtypeStruct((nsh, s, fs), CD), jax.ShapeDtypeStruct((nsh, s, fs), CD),
                   jax.ShapeDtypeStruct((1, d), F32), jax.ShapeDtypeStruct((s, d), CD)],
        scratch_shapes=[pltpu.VMEM((tm, d), F32)],
        compiler_params=_params(2),
    )(dout, x, g, silu, dsilu, b, wg, wu, wd)


def _tn_call(name, operands, in_specs, out_shape, out_spec, grid, acc_shape, pick, scale=1.0):
    nk = grid[-1]
    n_in = len(operands)

    def body(*refs):
        out_ref, out16_ref, acc = refs[n_in], refs[n_in + 1], refs[n_in + 2]
        k = pl.program_id(len(grid) - 1)

        @pl.when(k == 0)
        def _():
            acc[...] = jnp.zeros_like(acc)

        pick(refs[:n_in], acc)

        @pl.when(k == nk - 1)
        def _():
            res = (acc[...] if scale == 1.0 else acc[...] * scale).reshape(out_ref.shape)
            out_ref[...] = res
            out16_ref[...] = res.astype(CD)

    return pl.pallas_call(
        body, name=name, grid=grid, in_specs=in_specs, out_specs=[out_spec, out_spec],
        out_shape=[out_shape, jax.ShapeDtypeStruct(out_shape.shape, CD)],
        scratch_shapes=[pltpu.VMEM(acc_shape, F32)], compiler_params=_params(len(grid)),
    )(*operands)


def _dw_shared_b(name, a3, b, tk, scale):
    nj, s, m = a3.shape
    n = b.shape[1]

    def pick(refs, acc):
        rows = pl.ds(pl.multiple_of(pl.program_id(1) * tk, tk), tk)
        acc[...] += _mm(refs[0][0], refs[1][rows, :], TN)

    return _tn_call(name, (a3, b),
                    [pl.BlockSpec((1, tk, m), lambda j, k: (j, k, 0)), pl.BlockSpec((s, n), lambda j, k: (0, 0))],
                    jax.ShapeDtypeStruct((nj, m, n), F32), pl.BlockSpec((1, m, n), lambda j, k: (j, 0, 0)),
                    (nj, s // tk), (m, n), pick, scale)


def _dw_colblocks(name, a, b, nj, tk):
    s, m = a.shape
    n = b.shape[1] // nj
    nk = s // tk

    def body(a_ref, b_ref, out_ref, out16_ref, acc):
        k = pl.program_id(0)

        @pl.when(k == 0)
        def _():
            acc[...] = jnp.zeros_like(acc)

        acc[...] += _mm(a_ref[...], b_ref[...], TN)

        @pl.when(k == nk - 1)
        def _():
            for j in range(nj):
                res = acc[:, j * n:(j + 1) * n]
                out_ref[j] = res
                out16_ref[j] = res.astype(CD)

    whole = pl.BlockSpec((nj, m, n), lambda k: (0, 0, 0))
    return pl.pallas_call(
        body, name=name, grid=(nk,),
        in_specs=[pl.BlockSpec((tk, m), lambda k: (k, 0)), pl.BlockSpec((tk, nj * n), lambda k: (k, 0))],
        out_specs=[whole, whole],
        out_shape=[jax.ShapeDtypeStruct((nj, m, n), F32), jax.ShapeDtypeStruct((nj, m, n), CD)],
        scratch_shapes=[pltpu.VMEM((m, nj * n), F32)], compiler_params=_params(1),
    )(a, b)


DP_WIDTHS = (768, 512, 512, 512, 512, 512, 2048)
DW_IN_COLS = 512


def _dw_in(dps, hb, tk):
    s, d = hb.shape
    nk = s // tk
    blocks, row = [], 0
    for p, width in enumerate(DP_WIDTHS):
        step = width if width <= 768 else DW_IN_COLS
        for c0 in range(0, width, step):
            blocks.append((p, c0, step, row))
            row += step
    nb, max_w = len(blocks), max(b[2] for b in blocks)
    first = [min(i for i, b in enumerate(blocks) if b[0] == p) for p in range(len(DP_WIDTHS))]
    count = [sum(1 for b in blocks if b[0] == p) for p in range(len(DP_WIDTHS))]

    def body(*refs):
        dp_refs, hb_ref, out_ref, out16_ref, acc, acc16, sems = refs[:7], refs[7], refs[8], refs[9], refs[10], refs[11], refs[12]
        b, k = pl.program_id(0), pl.program_id(1)
        rows = pl.ds(pl.multiple_of(k * tk, tk), tk)

        def writes(i):
            _, _, w, r0 = blocks[i]
            slot = i % 2
            return (pltpu.make_async_copy(acc.at[slot, 0:w], out_ref.at[r0:r0 + w], sems.at[slot, 0]),
                    pltpu.make_async_copy(acc16.at[slot, 0:w], out16_ref.at[r0:r0 + w], sems.at[slot, 1]))

        for i, (p, _, w, _) in enumerate(blocks):
            @pl.when(b == i)
            def _(i=i, p=p, w=w):
                slot = i % 2
                prod = _mm(dp_refs[p][...], hb_ref[rows, :], TN)

                @pl.when(k == 0)
                def _():
                    acc[slot, 0:w] = prod

                @pl.when(k > 0)
                def _():
                    acc[slot, 0:w] += prod

                @pl.when(k == nk - 1)
                def _():
                    if i >= 1:
                        for cp in writes(i - 1):
                            cp.wait()
                    acc16[slot, 0:w] = acc[slot, 0:w].astype(CD)
                    for cp in writes(i):
                        cp.start()
                    if i == nb - 1:
                        for cp in writes(i):
                            cp.wait()

    def piece_spec(p):
        width = DP_WIDTHS[p]
        cols = width if width <= 768 else DW_IN_COLS

        def imap(b, k):
            active = (b >= first[p]) & (b < first[p] + count[p])
            return (jnp.where(active, k, jnp.where(b < first[p], 0, nk - 1)), jnp.clip(b - first[p], 0, count[p] - 1))

        return pl.BlockSpec((tk, cols), imap)

    return pl.pallas_call(
        body, name="dw_in", grid=(nb, nk),
        in_specs=[piece_spec(p) for p in range(len(DP_WIDTHS))] + [pl.BlockSpec((s, d), lambda b, k: (0, 0))],
        out_specs=[ANY, ANY],
        out_shape=[jax.ShapeDtypeStruct((sum(DP_WIDTHS), d), F32), jax.ShapeDtypeStruct((sum(DP_WIDTHS), d), CD)],
        scratch_shapes=[pltpu.VMEM((2, max_w, d), F32), pltpu.VMEM((2, max_w, d), CD), pltpu.SemaphoreType.DMA((2, 2))],
        compiler_params=_params(2),
    )(*dps, hb)


def _mix_in_fwd(x, g, w_t, tm):
    s, d = x.shape
    n_in = w_t.shape[0]

    def body(x_ref, g_ref, w_ref, qkv_ref, hg_ref, gt_ref, hb_ref):
        xv = x_ref[...]
        r = lax.rsqrt(jnp.mean(xv * xv, axis=-1, keepdims=True) + EPS)
        h = (xv * r * g_ref[...]).astype(CD)
        hb_ref[...] = h
        off = DP_WIDTHS[0]
        qkv_ref[...] = _mm(h, w_ref[0:off, :], NT)
        for c, width in enumerate(DP_WIDTHS[1:6]):
            hg_ref[:, c * width:(c + 1) * width] = _mm(h, w_ref[off:off + width, :], NT)
            off += width
        gate = DP_WIDTHS[6] // 2
        for c in range(2):
            gt_ref[:, c * gate:(c + 1) * gate] = _mm(h, w_ref[off:off + gate, :], NT).astype(CD)
            off += gate

    row = lambda w: pl.BlockSpec((tm, w), lambda i: (i, 0))
    return pl.pallas_call(
        body, name="mix_in_fwd", grid=(s // tm,),
        in_specs=[row(d), pl.BlockSpec((1, d), lambda i: (0, 0)), pl.BlockSpec((n_in, d), lambda i: (0, 0))],
        out_specs=[row(768), row(2560), row(2048), row(d)],
        out_shape=[jax.ShapeDtypeStruct((s, 768), F32), jax.ShapeDtypeStruct((s, 2560), F32),
                   jax.ShapeDtypeStruct((s, 2048), CD), jax.ShapeDtypeStruct((s, d), CD)],
        compiler_params=_params(1),
    )(x, g, w_t)


def _mix_in_bwd(dps, w_t, x, dres, g, tm, after=()):
    s, d = x.shape
    n_in = w_t.shape[0]

    def body(*refs):
        refs = refs[len(after):]
        dp_refs = refs[:7]
        w_ref, x_ref, dr_ref, g_ref, dx_ref, dg_ref = refs[7:]

        @pl.when(pl.program_id(0) == 0)
        def _():
            dg_ref[...] = jnp.zeros_like(dg_ref)

        dhv = jnp.zeros((tm, d), F32)
        off = 0
        for ref, width in zip(dp_refs, DP_WIDTHS):
            dhv = dhv + _mm(ref[...], w_ref[off:off + width, :], NN)
            off += width
        xv = x_ref[...]
        r = lax.rsqrt(jnp.mean(xv * xv, axis=-1, keepdims=True) + EPS)
        u = dhv * g_ref[...]
        dx_ref[...] = dr_ref[...] + r * u - xv * (r * r * r) * jnp.mean(u * xv, axis=-1, keepdims=True)
        dg_ref[...] += jnp.sum(dhv * xv * r, axis=0, keepdims=True)

    row = pl.BlockSpec((tm, d), lambda i: (i, 0))
    vec = pl.BlockSpec((1, d), lambda i: (0, 0))
    return pl.pallas_call(
        body, name="mix_in_bwd", grid=(s // tm,),
        in_specs=[ANY] * len(after) + [pl.BlockSpec((tm, w), lambda i: (i, 0)) for w in DP_WIDTHS]
        + [pl.BlockSpec((n_in, d), lambda i: (0, 0)), row, row, vec],
        out_specs=[row, vec],
        out_shape=[jax.ShapeDtypeStruct((s, d), F32), jax.ShapeDtypeStruct((1, d), F32)],
        compiler_params=_params(1),
    )(*after, *dps, w_t, x, dres, g)


def _headnorm_rope(x, gain, cos, sin, blocksum, swap):
    ss = _xdot(x * x, blocksum)
    r = lax.rsqrt(ss * (1.0 / HEAD_DIM) + EPS)
    y = x * r * gain
    return y * cos + _xdot(y, swap) * sin, r


def _headnorm_rope_bwd(dz, x, gain, cos, sin, blocksum, swap):
    ss = _xdot(x * x, blocksum)
    r = lax.rsqrt(ss * (1.0 / HEAD_DIM) + EPS)
    dy = dz * cos + _xdot(dz * sin, swap)
    u = dy * gain
    mean_ux = _xdot(u * x, blocksum) * (1.0 / HEAD_DIM)
    dx = r * u - x * (r * r * r) * mean_ux
    return dx, jnp.sum(dy * x * r, axis=0, keepdims=True)


def _qk_prep(pqkv, gq, gk, cos2, sin2, tm):
    s = pqkv.shape[0]
    bs512, sw512, eq, swh = _bf(_np_blocksum(512)), _bf(_np_swap32(512)), _bf(_np_expand_q()), _bf(_np_swap_halves())

    def body(q_ref, kv_ref, gq_ref, gk_ref, c_ref, s_ref, bs_ref, sw_ref, eq_ref, swh_ref, qe_ref, k_ref, v_ref, vs_ref):
        c2, s2 = c_ref[...], s_ref[...]
        c8, s8 = jnp.tile(c2, (1, 4)), jnp.tile(s2, (1, 4))
        bs, sw = bs_ref[...], sw_ref[...]
        zq, _ = _headnorm_rope(q_ref[...], gq_ref[...], c8, s8, bs, sw)
        qe_ref[...] = _mm(zq * (HEAD_DIM ** -0.5), eq_ref[...], NN).astype(CD)
        kv = kv_ref[...]
        zk, _ = _headnorm_rope(kv[:, :LANES], gk_ref[...], c2, s2, bs[:LANES, :LANES], sw[:LANES, :LANES])
        k_ref[...] = zk.astype(CD)
        v = kv[:, LANES:]
        v_ref[...] = v.astype(CD)
        vs_ref[...] = _mm(v, swh_ref[...], NN).astype(CD)

    full = lambda a: pl.BlockSpec(a.shape, lambda i: (0,) * a.ndim)
    tab = pl.BlockSpec((tm, LANES), lambda i: (i, 0))
    return pl.pallas_call(
        body, name="qk_prep", grid=(s // tm,),
        in_specs=[pl.BlockSpec((tm, 512), lambda i: (i, 0)), pl.BlockSpec((tm, 256), lambda i: (i, 2)),
                  full(gq), full(gk), tab, tab, full(bs512), full(sw512), full(eq), full(swh)],
        out_specs=[pl.BlockSpec((tm, 1024), lambda i: (i, 0)), tab, tab, tab],
        out_shape=[jax.ShapeDtypeStruct((s, 1024), CD)] + [jax.ShapeDtypeStruct((s, LANES), CD)] * 3,
        compiler_params=_params(1),
    )(pqkv, pqkv, gq, gk, cos2, sin2, bs512, sw512, eq, swh)


def _qk_prep_bwd(pqkv, dq, dk, dv, gq, gk, cos2, sin2, tm):
    s = pqkv.shape[0]
    bs512, sw512 = _bf(_np_blocksum(512)), _bf(_np_swap32(512))

    def body(q_ref, kv_ref, dq_ref, dk_ref, dv_ref, gq_ref, gk_ref, c_ref, s_ref, bs_ref, sw_ref,
             dp_ref, dgq_ref, dgk_ref):
        @pl.when(pl.program_id(0) == 0)
        def _():
            dgq_ref[...] = jnp.zeros_like(dgq_ref)
            dgk_ref[...] = jnp.zeros_like(dgk_ref)

        c2, s2 = c_ref[...], s_ref[...]
        c8, s8 = jnp.tile(c2, (1, 4)), jnp.tile(s2, (1, 4))
        bs, sw = bs_ref[...], sw_ref[...]
        dzq = dq_ref[...] * (HEAD_DIM ** -0.5)
        dxq, dgq = _headnorm_rope_bwd(dzq, q_ref[...], gq_ref[...], c8, s8, bs, sw)
        kv = kv_ref[...]
        dxk, dgk = _headnorm_rope_bwd(dk_ref[...], kv[:, :LANES], gk_ref[...], c2, s2, bs[:LANES, :LANES], sw[:LANES, :LANES])
        dp_ref[...] = jnp.concatenate([dxq, dxk, dv_ref[...]], axis=1).astype(CD)
        dgq_ref[...] += dgq
        dgk_ref[...] += dgk

    full = lambda a: pl.BlockSpec(a.shape, lambda i: (0,) * a.ndim)
    tab = pl.BlockSpec((tm, LANES), lambda i: (i, 0))
    return pl.pallas_call(
        body, name="qk_prep_bwd", grid=(s // tm,),
        in_specs=[pl.BlockSpec((tm, 512), lambda i: (i, 0)), pl.BlockSpec((tm, 256), lambda i: (i, 2)),
                  pl.BlockSpec((tm, 512), lambda i: (i, 0)), tab, tab, full(gq), full(gk), tab, tab,
                  full(bs512), full(sw512)],
        out_specs=[pl.BlockSpec((tm, 768), lambda i: (i, 0)), pl.BlockSpec((1, 512), lambda i: (0, 0)),
                   pl.BlockSpec((1, LANES), lambda i: (0, 0))],
        out_shape=[jax.ShapeDtypeStruct((s, 768), CD), jax.ShapeDtypeStruct((1, 512), F32),
                   jax.ShapeDtypeStruct((1, LANES), F32)],
        compiler_params=_params(1),
    )(pqkv, pqkv, dq, dk, dv, gq, gk, cos2, sin2, bs512, sw512)


def _kv_rows(h):
    return pl.ds(pl.multiple_of((h // 4) * HEAD_DIM, HEAD_DIM), HEAD_DIM)


def _attn_fwd(qe, k, v, vs, tq):
    s = k.shape[0]

    def body(q0_ref, q1_ref, q2_ref, q3_ref, k_ref, v_ref, vs_ref, o_ref, lse_ref):
        grp = pl.program_id(0)
        kk = k_ref[...]
        heads = range(4)
        scores = [_mm(q_ref[...], kk, NT) for q_ref in (q0_ref, q1_ref, q2_ref, q3_ref)]
        mxs = [jnp.max(sc, axis=-1, keepdims=True) for sc in scores]
        es = [jnp.exp(scores[r] - mxs[r]) for r in heads]
        ls = [jnp.sum(e, axis=-1, keepdims=True) for e in es]
        for r in heads:
            lse_ref[r] = mxs[r] + jnp.log(ls[r])
        outs = [_mm(es[r], jnp.where(grp != r % 2, vs_ref[...], v_ref[...]), NN) * (1.0 / ls[r]) for r in heads]
        low = lax.broadcasted_iota(jnp.int32, (1, LANES), 1) < HEAD_DIM
        o_ref[...] = jnp.concatenate([jnp.where(low, outs[0], outs[1]), jnp.where(low, outs[2], outs[3])], axis=1)

    kv = pl.BlockSpec((s, LANES), lambda g, i: (0, 0))
    qblk = lambda r: pl.BlockSpec((tq, LANES), lambda g, i: (i, 4 * g + r))
    return pl.pallas_call(
        body, name="attn_fwd", grid=(2, s // tq),
        in_specs=[qblk(0), qblk(1), qblk(2), qblk(3), kv, kv, kv],
        out_specs=[pl.BlockSpec((tq, 2 * LANES), lambda g, i: (i, g)), pl.BlockSpec((4, tq, 1), lambda g, i: (g, i, 0))],
        out_shape=[jax.ShapeDtypeStruct((s, 512), F32), jax.ShapeDtypeStruct((8, s, 1), F32)],
        compiler_params=_params(2),
    )(qe, qe, qe, qe, k, v, vs)


def _attn_bwd(qe, k, kt, v, doe, delta, lse, tq):
    s = k.shape[0]

    nh = ATTN_BWD_HEADS

    def body(*refs):
        q_refs, (k_ref, kt_ref, v_ref) = refs[:nh], refs[nh:nh + 3]
        do_refs, dl_refs = refs[nh + 3:2 * nh + 3], refs[2 * nh + 3:3 * nh + 3]
        lse_ref, dqt_ref, dkt_ref, dvt_ref, qt, dot = refs[3 * nh + 3:]

        @pl.when((pl.program_id(0) == 0) & (pl.program_id(1) == 0))
        def _():
            dkt_ref[...] = jnp.zeros_like(dkt_ref)
            dvt_ref[...] = jnp.zeros_like(dvt_ref)

        rows = _kv_rows(nh * pl.program_id(0))
        kt = kt_ref[rows, :]
        dkt = jnp.zeros((HEAD_DIM, s), F32)
        dvt = jnp.zeros((HEAD_DIM, s), F32)
        firsts = [(_mm(q_ref[...], k_ref[...], NT), _mm(do_ref[...], v_ref[...], NT))
                  for q_ref, do_ref in zip(q_refs, do_refs)]
        for idx, (q_ref, do_ref, dl_ref) in enumerate(zip(q_refs, do_refs, dl_refs)):
            q, do = q_ref[...], do_ref[...]
            sc, dp = firsts[idx]
            p = jnp.exp(sc - lse_ref[idx])
            ds = p * (dp - jnp.max(dl_ref[...], axis=-1, keepdims=True))
            dqt_ref[idx * HEAD_DIM:(idx + 1) * HEAD_DIM, :] = _mm(kt, ds, NT)
            qt[idx] = jnp.transpose(q.astype(F32))
            dot[idx] = jnp.transpose(do.astype(F32))
            dkt = dkt + _mm(qt[idx, rows, :], ds, NN)
            dvt = dvt + _mm(dot[idx, rows, :], p, NN)
        dkt_ref[rows, :] += dkt
        dvt_ref[rows, :] += dvt

    kv = pl.BlockSpec((s, LANES), lambda m, i: (0, 0))
    kvt = pl.BlockSpec((LANES, s), lambda m, i: (0, 0))
    blks = [pl.BlockSpec((tq, LANES), lambda m, i, r=r: (i, nh * m + r)) for r in range(nh)]
    return pl.pallas_call(
        body, name="attn_bwd", grid=(8 // nh, s // tq),
        in_specs=blks + [kv, kvt, kv] + blks + blks + [pl.BlockSpec((nh, tq, 1), lambda m, i: (m, i, 0))],
        out_specs=[pl.BlockSpec((nh * HEAD_DIM, tq), lambda m, i: (m, i)), kvt, kvt],
        out_shape=[jax.ShapeDtypeStruct((8 * HEAD_DIM, s), F32), jax.ShapeDtypeStruct((LANES, s), F32),
                   jax.ShapeDtypeStruct((LANES, s), F32)],
        scratch_shapes=[pltpu.VMEM((nh, LANES, tq), F32), pltpu.VMEM((nh, LANES, tq), F32)],
        compiler_params=_params(2),
    )(*[qe] * nh, k, kt, v, *[doe] * nh, *[delta] * nh, lse)


@jax.custom_vjp
def _mm_nn(a, b):
    return _mm(a, b, NN)


_mm_nn.defvjp(lambda a, b: (_mm(a, b, NN), (a, b)),
              lambda res, g: (_mm(g, res[1], NT), _mm(res[0], g, TN)))


@jax.custom_vjp
def _mm_nt(a, b):
    return _mm(a, b, NT)


_mm_nt.defvjp(lambda a, b: (_mm(a, b, NT), (a, b)),
              lambda res, g: (_mm(g, res[1], NN), _mm(g, res[0], TN)))


@jax.custom_vjp
def _mm_tn(a, b):
    return _mm(a, b, TN)


_mm_tn.defvjp(lambda a, b: (_mm(a, b, TN), (a, b)),
              lambda res, g: (_mm(res[1], g, NT), _mm(res[0], g, NN)))


@jax.custom_vjp
def _cmm(m, mt, x):
    return _xdot_l(m, x)


_cmm.defvjp(lambda m, mt, x: (_xdot_l(m, x), (m, mt)),
            lambda res, g: (jnp.zeros_like(res[0]), jnp.zeros_like(res[1]), _xdot_l(res[1], g)))


def _hgrn_masks(t, rev):
    n_ch = t // CHUNK
    r = jnp.bitwise_and(lax.broadcasted_iota(jnp.int32, (2 * t, t), 0), t - 1)
    c = lax.broadcasted_iota(jnp.int32, (2 * t, t), 1)
    same = jnp.right_shift(r, 5) == jnp.right_shift(c, 5)
    tri2 = same & ((c >= r) if rev else (c <= r))
    pr = lax.broadcasted_iota(jnp.int32, (LANES, LANES), 0)
    pc = lax.broadcasted_iota(jnp.int32, (LANES, LANES), 1)
    diag = jnp.right_shift(pr, 6) == jnp.right_shift(pc, 6)
    qr = lax.broadcasted_iota(jnp.int32, (t, n_ch * LANES), 0)
    qc = lax.broadcasted_iota(jnp.int32, (t, n_ch * LANES), 1)
    rows_chunk = jnp.right_shift(qc, 7) == jnp.right_shift(qr, 5)
    vr = lax.broadcasted_iota(jnp.int32, (n_ch * LANES, t), 0)
    vc = lax.broadcasted_iota(jnp.int32, (n_ch * LANES, t), 1)
    cols_chunk = jnp.right_shift(vr, 7) == jnp.right_shift(vc, 5)
    return dict(tri2=tri2, diag=diag, rows_chunk=rows_chunk, cols_chunk=cols_chunk)


def _hgrn_gates(xf, lb):
    f = lb + (1.0 - lb) * _sigmoid(xf)
    return 1.0 - f, jnp.log(f)


def _hgrn_dir(*args):
    return _hgrn_dirs([args])[0][:2]


def _hgrn_dirs(arg_sets):
    chains = [_hgrn_phases(*a) for a in arg_sets]
    results = [None] * len(chains)
    while any(r is None for r in results):
        for n, chain in enumerate(chains):
            if results[n] is None:
                try:
                    next(chain)
                except StopIteration as done:
                    results[n] = done.value
    return results


def _hgrn_phases(xq, xf, v, lb, state, cm, cmt, mk, rev):
    t = xq.shape[0]
    n_ch = t // CHUNK
    lo = lax.broadcasted_iota(jnp.int32, (1, LANES), 1) < HEAD_DIM
    k, lf = _hgrn_gates(xf, lb)
    cs = _cmm(cm, cmt, lf)
    yield
    q = xq * _sigmoid(xq)
    b, bm, bl = cs[:t], cs[t:2 * t], cs[2 * t:]
    qd = q * jnp.exp(b - bm)
    kd = k * jnp.exp(bm - b)
    yield
    qd2 = jnp.concatenate([jnp.where(lo, qd, 0.0), jnp.where(lo, 0.0, qd)], axis=0)
    scores = _mm_nt(qd2, kd)
    yield
    kc = k * jnp.exp(bl - b)
    qe = q * jnp.exp(b)
    vexp = jnp.where(mk["cols_chunk"], jnp.concatenate([jnp.transpose(v)] * n_ch, axis=0), 0.0)
    adds = _mm_nn(vexp, kc)
    yield
    o2 = _mm_nn(jnp.where(mk["tri2"], scores, 0.0), v)
    o = jnp.where(lo, o2[:t], o2[t:])
    yield
    dec = jnp.exp(bl)
    entering = [None] * n_ch
    for c in (range(n_ch - 1, -1, -1) if rev else range(n_ch)):
        entering[c] = state
        d = jnp.concatenate([dec[c * CHUNK:(c + 1) * CHUNK]] * (LANES // CHUNK), axis=0)
        state = d * state + jnp.where(mk["diag"], adds[c * LANES:(c + 1) * LANES], 0.0)
    yield
    qexp = jnp.where(mk["rows_chunk"], jnp.concatenate([qe] * n_ch, axis=1), 0.0)
    middle = entering[n_ch // 2 - 1] if rev else entering[n_ch // 2]
    return o + _mm_nt(qexp, jnp.concatenate(entering, axis=1)), state, middle


def _hgrn_lower_bounds(l):
    out = []
    for d in (0, 1):
        l0, l1 = l[2 * d:2 * d + 1, :], l[2 * d + 1:2 * d + 2, :]
        mx = jnp.maximum(l0, l1)
        e0, e1 = jnp.exp(l0 - mx), jnp.exp(l1 - mx)
        out.append(e0 / (e0 + e1))
    return out


def _hgrn_consts(t):
    cf, cb = _np_hgrn_cums(t, False), _np_hgrn_cums(t, True)
    return (_bf(cf), _bf(cf.T), _bf(cb), _bf(cb.T), _bf(_np_blocksum(LANES)))


def _hgrn_fwd(ph, lbl, ng):
    s = ph.shape[0]
    t = 2 * min(HG_TILE, s // 2)
    nt = s // t
    consts = _hgrn_consts(t)

    def body(xq_ref, xff_ref, xfb_ref, xi_ref, xg_ref, lbl_ref, ng_ref, cf_ref, cft_ref, cb_ref, cbt_ref, bs_ref,
             o_ref, pre_ref, st_ref, acc):
        lbf, lbb = _hgrn_lower_bounds(lbl_ref)
        mk_f, mk_b = _hgrn_masks(t, False), _hgrn_masks(t, True)
        zero = jnp.zeros((LANES, LANES), F32)

        def rows_of(i):
            return pl.ds(pl.multiple_of(i * t, t), t)

        acc[...] = jnp.zeros_like(acc)

        def step(i, states):
            tb = nt - 1 - i
            rf, rb = rows_of(i), rows_of(tb)
            (of, sf, mid_f), (ob, sb, mid_b) = _hgrn_dirs([
                (xq_ref[rf, :], xff_ref[rf, :], xi_ref[rf, :], lbf, states[0], cf_ref[...], cft_ref[...], mk_f, False),
                (xq_ref[rb, :], xfb_ref[rb, :], xi_ref[rb, :], lbb, states[1], cb_ref[...], cbt_ref[...], mk_b, True)])
            st_ref[0, 0, 2 * i] = states[0]
            st_ref[0, 0, 2 * i + 1] = mid_f
            st_ref[0, 1, 2 * tb + 1] = states[1]
            st_ref[0, 1, 2 * tb] = mid_b
            acc[rf, :] += of
            acc[rb, :] += ob
            return sf, sb

        lax.fori_loop(0, nt, step, (zero, zero))

        def step_n(i, carry):
            rows = rows_of(i)
            o = acc[rows, :]
            ss = _xdot(o * o, bs_ref[...])
            r = lax.rsqrt(ss * (1.0 / HEAD_DIM) + EPS)
            xg = xg_ref[rows, :]
            pre_ref[rows, :] = o
            o_ref[rows, :] = ((o * r * ng_ref[...]) * (xg * _sigmoid(xg))).astype(CD)
            return carry

        lax.fori_loop(0, nt, step_n, 0)

    col = lambda off: pl.BlockSpec((s, LANES), lambda m: (0, off + m))
    full = lambda a: pl.BlockSpec(a.shape, lambda m: (0,) * a.ndim)
    return pl.pallas_call(
        body, name="hgrn_fwd", grid=(4,),
        in_specs=[col(0), col(4), col(8), col(12), col(16), pl.BlockSpec((4, LANES), lambda m: (0, m)),
                  pl.BlockSpec((1, LANES), lambda m: (0, m))] + [full(c) for c in consts],
        out_specs=[col(0), col(0), pl.BlockSpec((1, 2, 2 * nt, LANES, LANES), lambda m: (m, 0, 0, 0, 0))],
        out_shape=[jax.ShapeDtypeStruct((s, 512), CD), jax.ShapeDtypeStruct((s, 512), F32),
                   jax.ShapeDtypeStruct((4, 2, 2 * nt, LANES, LANES), F32)],
        scratch_shapes=[pltpu.VMEM((s, LANES), F32)],
        compiler_params=_params(1),
    )(ph, ph, ph, ph, ph, lbl, ng, *consts)


def _hgrn_bwd(ph, pre, dout, states, lbl, ng):
    s = ph.shape[0]
    t = min(HG_TILE, s)
    nt = s // t
    consts = _hgrn_consts(t)

    def body(xq_ref, xff_ref, xfb_ref, xi_ref, xg_ref, pre_ref, do_ref, st_ref, lbl_ref, ng_ref,
             cf_ref, cft_ref, cb_ref, cbt_ref, bs_ref,
             dq_ref, dff_ref, dfb_ref, di_ref, dg_ref, dlb_ref, dng_ref, dpre, dq_acc, dv_acc):
        lbf, lbb = _hgrn_lower_bounds(lbl_ref)
        mk_f, mk_b = _hgrn_masks(t, False), _hgrn_masks(t, True)
        zero = jnp.zeros((LANES, LANES), F32)
        zrow = jnp.zeros((1, LANES), F32)

        def rows_of(i):
            return pl.ds(pl.multiple_of(i * t, t), t)

        def step_n(i, dng):
            rows = rows_of(i)
            o, xg, do = pre_ref[rows, :], xg_ref[rows, :], do_ref[rows, :]
            bs = bs_ref[...]
            r = lax.rsqrt(_xdot(o * o, bs) * (1.0 / HEAD_DIM) + EPS)
            sg = _sigmoid(xg)
            gate = xg * sg
            don = do * gate
            dg_ref[rows, :] = (do * (o * r * ng_ref[...]) * (sg * (1.0 + xg * (1.0 - sg)))).astype(CD)
            u = don * ng_ref[...]
            dpre[rows, :] = r * u - o * (r * r * r) * (_xdot(u * o, bs) * (1.0 / HEAD_DIM))
            return dng + jnp.sum(don * o * r, axis=0, keepdims=True)

        dng_ref[...] = lax.fori_loop(0, nt, step_n, zrow)

        dq_acc[...] = jnp.zeros_like(dq_acc)
        dv_acc[...] = jnp.zeros_like(dv_acc)

        def step_g(i, carry):
            dsf, dsb, dlbf, dlbb = carry
            tf, tb = nt - 1 - i, i
            rf, rb = rows_of(tf), rows_of(tb)
            cf, cft, cb, cbt = cf_ref[...], cft_ref[...], cb_ref[...], cbt_ref[...]

            def both(xq_f, xf_f, v_f, lb_f, s_f, xq_b, xf_b, v_b, lb_b, s_b):
                (of, sf, _), (ob, sb, _) = _hgrn_dirs([(xq_f, xf_f, v_f, lb_f, s_f, cf, cft, mk_f, False),
                                                       (xq_b, xf_b, v_b, lb_b, s_b, cb, cbt, mk_b, True)])
                return of, sf, ob, sb

            _, vjp = jax.vjp(both, xq_ref[rf, :], xff_ref[rf, :], xi_ref[rf, :], lbf, st_ref[0, 0, tf],
                             xq_ref[rb, :], xfb_ref[rb, :], xi_ref[rb, :], lbb, st_ref[0, 1, tb])
            dq_f, dx_f, dv_f, gf, dsf, dq_b, dx_b, dv_b, gb, dsb = vjp((dpre[rf, :], dsf, dpre[rb, :], dsb))
            dff_ref[rf, :] = dx_f.astype(CD)
            dfb_ref[rb, :] = dx_b.astype(CD)
            dq_acc[rf, :] += dq_f
            dv_acc[rf, :] += dv_f
            dq_acc[rb, :] += dq_b
            dv_acc[rb, :] += dv_b
            return dsf, dsb, dlbf + gf, dlbb + gb

        _, _, dlbf, dlbb = lax.fori_loop(0, nt, step_g, (zero, zero, zrow, zrow))
        dlb_ref[0:1, :] = dlbf
        dlb_ref[1:2, :] = dlbb
        dq_ref[...] = dq_acc[...].astype(CD)
        di_ref[...] = dv_acc[...].astype(CD)

    col = lambda off: pl.BlockSpec((s, LANES), lambda m: (0, off + m))
    full = lambda a: pl.BlockSpec(a.shape, lambda m: (0,) * a.ndim)
    stream = jax.ShapeDtypeStruct((s, 512), CD)
    return pl.pallas_call(
        body, name="hgrn_bwd", grid=(4,),
        in_specs=[col(0), col(4), col(8), col(12), col(16), col(0), col(0),
                  pl.BlockSpec((1, 2, nt, LANES, LANES), lambda m: (m, 0, 0, 0, 0)),
                  pl.BlockSpec((4, LANES), lambda m: (0, m)),
                  pl.BlockSpec((1, LANES), lambda m: (0, m))] + [full(c) for c in consts],
        out_specs=[col(0)] * 5 + [pl.BlockSpec((2, LANES), lambda m: (0, m)), pl.BlockSpec((1, LANES), lambda m: (0, m))],
        out_shape=[stream] * 5 + [jax.ShapeDtypeStruct((2, 512), F32), jax.ShapeDtypeStruct((1, 512), F32)],
        scratch_shapes=[pltpu.VMEM((s, LANES), F32), pltpu.VMEM((s, LANES), F32), pltpu.VMEM((s, LANES), F32)],
        compiler_params=_params(1, HGRN_BWD_VMEM),
    )(ph, ph, ph, ph, ph, pre, dout, states, lbl, ng, *consts)


def _branch_out(o, w4):
    o = o.astype(CD)
    return jnp.concatenate([_mm(o, w4[j], NN) for j in range(N_SHARD)], axis=1)


def _mix_out_fwd(x, oa, ob, pg, wa, wb, wo, tm):
    s, d = x.shape

    def body(x_ref, oa_ref, ob_ref, ga_ref, gb_ref, wa_ref, wb_ref, wo_ref, xo_ref):
        ya = _branch_out(oa_ref[...], wa_ref)
        yb = _branch_out(ob_ref[...], wb_ref)
        merged = _sigmoid(ga_ref[...].astype(F32)) * ya + _sigmoid(gb_ref[...].astype(F32)) * yb
        xo_ref[...] = x_ref[...] + _mm(merged, wo_ref[...], NN)

    row = pl.BlockSpec((tm, d), lambda i: (i, 0))
    half = pl.BlockSpec((tm, 512), lambda i: (i, 0))
    full = lambda a: pl.BlockSpec(a.shape, lambda i: (0,) * a.ndim)
    return pl.pallas_call(
        body, name="mix_out_fwd", grid=(s // tm,),
        in_specs=[row, half, half, row, pl.BlockSpec((tm, d), lambda i: (i, 1)), full(wa), full(wb), full(wo)],
        out_specs=row, out_shape=jax.ShapeDtypeStruct((s, d), F32),
        compiler_params=_params(1),
    )(x, oa, ob, pg, pg, wa, wb, wo)


def _mix_out_bwd(dx, oa, ob, pg, wa, wb, wo, tm, after=()):
    s, d = dx.shape
    eq, ebc = _bf(_np_expand_q()), _bf(_np_headsum_spread())

    def body(*refs):
        (dx_ref, oa_ref, ob_ref, ga_ref, gb_ref, wa_ref, wb_ref, wo_ref, eq_ref, ebc_ref,
         dpg_ref, mg_ref, dya_ref, dyb_ref, doe_ref, dl_ref, dob_ref) = refs[len(after):]
        oa = oa_ref[...]
        ya = _branch_out(oa, wa_ref)
        yb = _branch_out(ob_ref[...], wb_ref)
        sa, sb = _sigmoid(ga_ref[...].astype(F32)), _sigmoid(gb_ref[...].astype(F32))
        mg_ref[...] = (sa * ya + sb * yb).astype(CD)
        dm = _mm(dx_ref[...], wo_ref[...], NT)
        dpg_ref[...] = jnp.concatenate([dm * ya * sa * (1.0 - sa), dm * yb * sb * (1.0 - sb)], axis=1).astype(CD)
        dya, dyb = dm * sa, dm * sb
        dya_ref[...] = dya.astype(CD)
        dyb_ref[...] = dyb.astype(CD)
        doa = jnp.zeros(oa.shape, F32)
        dob = jnp.zeros(oa.shape, F32)
        for j in range(N_SHARD):
            doa = doa + _mm(dya[:, 256 * j:256 * j + 256], wa_ref[j], NT)
            dob = dob + _mm(dyb[:, 256 * j:256 * j + 256], wb_ref[j], NT)
        dob_ref[...] = dob
        doe_ref[...] = _mm(doa, eq_ref[...], NN).astype(CD)
        dl_ref[...] = _xdot(doa * oa, ebc_ref[...])

    row = pl.BlockSpec((tm, d), lambda i: (i, 0))
    half = pl.BlockSpec((tm, 512), lambda i: (i, 0))
    full = lambda a: pl.BlockSpec(a.shape, lambda i: (0,) * a.ndim)
    wide = jax.ShapeDtypeStruct((s, d), CD)
    return pl.pallas_call(
        body, name="mix_out_bwd", grid=(s // tm,),
        in_specs=[ANY] * len(after) + [row, half, half, row, pl.BlockSpec((tm, d), lambda i: (i, 1)), full(wa), full(wb),
                                       full(wo), full(eq), full(ebc)],
        out_specs=[pl.BlockSpec((tm, 2048), lambda i: (i, 0)), row, row, row, row, row, half],
        out_shape=[jax.ShapeDtypeStruct((s, 2048), CD), wide, wide, wide, wide, jax.ShapeDtypeStruct((s, d), F32),
                   jax.ShapeDtypeStruct((s, 512), F32)],
        compiler_params=_params(1),
    )(*after, dx, oa, ob, pg, pg, wa, wb, wo, eq, ebc)


def _loss_head(x, g, target, tm):
    s, d = x.shape

    def body(x_ref, g_ref, t_ref, dx_ref, loss_ref, dg_ref):
        @pl.when(pl.program_id(0) == 0)
        def _():
            loss_ref[...] = jnp.zeros_like(loss_ref)
            dg_ref[...] = jnp.zeros_like(dg_ref)

        xv = x_ref[...]
        r = lax.rsqrt(jnp.mean(xv * xv, axis=-1, keepdims=True) + EPS)
        err = xv * r * g_ref[...] - t_ref[...]
        loss_ref[...] += 0.5 * jnp.sum(jnp.mean(err * err, axis=-1, keepdims=True))
        dy = err * (1.0 / d)
        u = dy * g_ref[...]
        dx_ref[...] = r * u - xv * (r * r * r) * jnp.mean(u * xv, axis=-1, keepdims=True)
        dg_ref[...] += jnp.sum(dy * xv * r, axis=0, keepdims=True)

    row = pl.BlockSpec((tm, d), lambda i: (i, 0))
    vec = pl.BlockSpec((1, d), lambda i: (0, 0))
    return pl.pallas_call(
        body, name="loss_head", grid=(s // tm,),
        in_specs=[row, vec, row], out_specs=[row, pl.BlockSpec((8, LANES), lambda i: (0, 0)), vec],
        out_shape=[jax.ShapeDtypeStruct((s, d), F32), jax.ShapeDtypeStruct((8, LANES), F32),
                   jax.ShapeDtypeStruct((1, d), F32)],
        compiler_params=_params(1),
    )(x, g, target)


def _position():
    x, y, c = lax.axis_index("x"), lax.axis_index("y"), lax.axis_index("c")
    return x, y, c, [(1 - x, y), (x, 1 - y), (1 - x, 1 - y)]


def _row_tile(rows, cap=256):
    best = rows
    for cand in range(8, min(rows, cap) + 1, 8):
        if rows % cand == 0:
            best = cand
    return best


def _cast_into_slots(shards, dtypes, me_idx):
    n = len(shards)
    tiles = [_row_tile(s.shape[0]) for s in shards]
    counts = [s.shape[0] // t for s, t in zip(shards, tiles)]
    starts = [sum(counts[:a]) for a in range(n)]

    def body(me_ref, *refs):
        i = pl.program_id(0)
        for a in range(n):
            @pl.when((i >= starts[a]) & (i < starts[a] + counts[a]))
            def _(a=a):
                refs[n + a][0] = refs[a][...].astype(dtypes[a])

    tile_of = [lambda i, a=a: jnp.clip(i - starts[a], 0, counts[a] - 1) for a in range(n)]
    return pl.pallas_call(
        body, name="cast_into_slots",
        grid_spec=pltpu.PrefetchScalarGridSpec(
            num_scalar_prefetch=1, grid=(sum(counts),),
            in_specs=[pl.BlockSpec((tiles[a], shards[a].shape[1]), lambda i, me, a=a: (tile_of[a](i), 0)) for a in range(n)],
            out_specs=[pl.BlockSpec((1, tiles[a], shards[a].shape[1]), lambda i, me, a=a: (me[0], tile_of[a](i), 0))
                       for a in range(n)]),
        out_shape=[jax.ShapeDtypeStruct((N_SHARD,) + s.shape, dt) for s, dt in zip(shards, dtypes)],
        compiler_params=_params(1),
    )(me_idx, *shards)


HBM_SPEC = pl.BlockSpec(memory_space=pltpu.HBM)
SEM_SPEC = pl.BlockSpec(memory_space=pltpu.SEMAPHORE)
DATAFLOW = pltpu.SideEffectType.DATAFLOW_SIDE_EFFECTING


def _exchange_copies(srcs, lands, send, recv, gather):
    x, y, c, chips = _position()
    me = 2 * x + y
    out = []
    for a in range(len(lands)):
        dst = lands[a].at[me]
        if gather and _halved(lands[a]):
            half = lands[a].shape[1] // 2
            dst = lands[a].at[me, pl.ds(c * half, half), :]
        for k, (px, py) in enumerate(chips):
            src = dst if gather else srcs[a].at[2 * px + py]
            out.append(pltpu.make_async_remote_copy(src_ref=src, dst_ref=dst, send_sem=send.at[3 * a + k],
                                                    recv_sem=recv.at[3 * a + k], device_id=(px, py, c), device_id_type=MESH))
    return out


def _halved(land):
    return land.shape[1] % 32 == 0


def _pair_fill(name, lands):
    n = len(lands)

    def body(*refs):
        src, dst = refs[:n], refs[n:2 * n]
        send, recv = refs[2 * n:]
        x, y, c, chips = _position()
        copies = []
        for a in range(n):
            half = src[a].shape[1] // 2
            for k, (px, py) in enumerate(chips):
                rows = (2 * px + py, pl.ds(c * half, half), slice(None))
                cp = pltpu.make_async_remote_copy(src_ref=src[a].at[rows], dst_ref=dst[a].at[rows], send_sem=send.at[a, k],
                                                  recv_sem=recv.at[a, k], device_id=(x, y, 1 - c), device_id_type=MESH)
                cp.start()
                copies.append(cp)
        for cp in copies:
            cp.wait()

    return pl.pallas_call(
        body, name=name, in_specs=[ANY] * n, out_specs=[ANY] * n,
        out_shape=[jax.ShapeDtypeStruct(l.shape, l.dtype) for l in lands],
        input_output_aliases={a: a for a in range(n)},
        scratch_shapes=[pltpu.SemaphoreType.DMA((n, 3)), pltpu.SemaphoreType.DMA((n, 3))],
    )(*lands)


def _exchange_start(name, srcs, lands, after):
    ns, nl, na = len(srcs), len(lands), len(after)
    gather = ns == 0

    def body(*refs):
        src_refs, land_refs = refs[:ns], refs[ns:ns + nl]
        send, recv = refs[ns + nl + na], refs[ns + nl + na + 1]
        token = refs[-1]
        for cp in _exchange_copies(src_refs, land_refs, send, recv, gather):
            cp.start()
        token[...] = jnp.zeros_like(token)

    arrays = [pltpu.with_memory_space_constraint(a, pltpu.HBM) for a in list(srcs) + list(lands)]
    outs = pl.pallas_call(
        body, name=name,
        out_shape=(pltpu.SemaphoreType.DMA((3 * nl,)), pltpu.SemaphoreType.DMA((3 * nl,)),
                   *[pltpu.HBM(a.shape, a.dtype) for a in arrays], jax.ShapeDtypeStruct((8, LANES), F32)),
        in_specs=[HBM_SPEC] * (ns + nl) + [ANY] * na,
        out_specs=(SEM_SPEC, SEM_SPEC, *[HBM_SPEC] * (ns + nl), pl.BlockSpec(memory_space=pltpu.VMEM)),
        input_output_aliases={i: 2 + i for i in range(ns + nl)},
        compiler_params=pltpu.CompilerParams(has_side_effects=DATAFLOW),
    )(*arrays, *after)
    return outs[0], outs[1], list(outs[2:2 + ns]), list(outs[2 + ns:2 + ns + nl]), outs[-1]


def _exchange_wait(name, send, recv, srcs, lands, after):
    ns, nl, na = len(srcs), len(lands), len(after)
    gather = ns == 0

    def body(*refs):
        src_refs, land_refs = refs[:ns], refs[ns:ns + nl]
        send_ref, recv_ref = refs[ns + nl], refs[ns + nl + 1]
        for cp in _exchange_copies(src_refs, land_refs, send_ref, recv_ref, gather):
            cp.wait_send()
            cp.wait_recv()

    outs = pl.pallas_call(
        body, name=name,
        out_shape=tuple(pltpu.HBM(a.shape, a.dtype) for a in list(srcs) + list(lands)),
        in_specs=[HBM_SPEC] * (ns + nl) + [SEM_SPEC, SEM_SPEC] + [ANY] * na,
        out_specs=tuple([HBM_SPEC] * (ns + nl)),
        input_output_aliases={i: i for i in range(ns + nl)},
        compiler_params=pltpu.CompilerParams(has_side_effects=DATAFLOW),
    )(*srcs, *lands, send, recv, *after)
    return list(outs[ns:])


def _pair_exchange(grads):
    n = len(grads)

    def body(*refs):
        src, dst = refs[:n], refs[n:2 * n]
        send, recv = refs[2 * n:]
        x, y, c, _ = _position()
        copies = []
        for a in range(n):
            half = src[a].shape[1] // 2
            cp = pltpu.make_async_remote_copy(
                src_ref=src[a].at[:, pl.ds((1 - c) * half, half), :], dst_ref=dst[a], send_sem=send.at[a],
                recv_sem=recv.at[a], device_id=(x, y, 1 - c), device_id_type=MESH)
            cp.start()
            copies.append(cp)
        for cp in copies:
            cp.wait()

    return pl.pallas_call(
        body, name="grad_pair_exchange", in_specs=[ANY] * n, out_specs=[ANY] * n,
        out_shape=[jax.ShapeDtypeStruct((g.shape[0], g.shape[1] // 2, g.shape[2]), g.dtype) for g in grads],
        scratch_shapes=[pltpu.SemaphoreType.DMA((n,)), pltpu.SemaphoreType.DMA((n,))],
    )(*grads)


def _shard_of(a):
    return lambda i: jnp.clip(i - a * N_SHARD, 0, N_SHARD - 1)


def _pair_sum(gs, gots, c_idx, me_idx):
    n = len(gs)
    halves = [(g.shape[1] // 2, g.shape[2]) for g in gs]

    def body(c_ref, me_ref, *refs):
        g_refs, got_refs, s_refs, own_refs = (refs[k * n:(k + 1) * n] for k in range(4))
        i = pl.program_id(0)
        for a in range(n):
            @pl.when(i // N_SHARD == a)
            def _(a=a):
                sm = g_refs[a][...] + got_refs[a][...].astype(F32)
                s_refs[a][...] = sm.astype(CD)

                @pl.when(i % N_SHARD == me_ref[0])
                def _():
                    own_refs[a][...] = sm[0]

    shard = [_shard_of(a) for a in range(n)]
    return pl.pallas_call(
        body, name="grad_pair_sum",
        grid_spec=pltpu.PrefetchScalarGridSpec(
            num_scalar_prefetch=2, grid=(n * N_SHARD,),
            in_specs=[pl.BlockSpec((1, h, c_), lambda i, c, me, a=a: (shard[a](i), c[0], 0)) for a, (h, c_) in enumerate(halves)]
            + [pl.BlockSpec((1, h, c_), lambda i, c, me, a=a: (shard[a](i), 0, 0)) for a, (h, c_) in enumerate(halves)],
            out_specs=[pl.BlockSpec((1, h, c_), lambda i, c, me, a=a: (shard[a](i), 0, 0)) for a, (h, c_) in enumerate(halves)]
            + [pl.BlockSpec((h, c_), lambda i, c, me: (0, 0)) for h, c_ in halves]),
        out_shape=[jax.ShapeDtypeStruct((N_SHARD, h, c_), CD) for h, c_ in halves]
        + [jax.ShapeDtypeStruct((h, c_), F32) for h, c_ in halves],
        compiler_params=_params(1),
    )(c_idx, me_idx, *gs, *gots)


def _chip_sum(owns, gots, me_idx):
    n = len(owns)
    tiles = [_row_tile(o.shape[0]) for o in owns]
    counts = [o.shape[0] // t for o, t in zip(owns, tiles)]
    starts = [sum(counts[:a]) for a in range(n)]

    def body(me_ref, *refs):
        own_refs, got_refs, out_refs = (refs[k * n:(k + 1) * n] for k in range(3))
        i = pl.program_id(0)
        for a in range(n):
            @pl.when((i >= starts[a]) & (i < starts[a] + counts[a]))
            def _(a=a):
                total = None
                for j in range(N_SHARD):
                    term = jnp.where(j == me_ref[0], own_refs[a][...], got_refs[a][j].astype(F32))
                    total = term if total is None else total + term
                out_refs[a][...] = total

    tile_of = [lambda i, a=a: jnp.clip(i - starts[a], 0, counts[a] - 1) for a in range(n)]
    own_specs = [pl.BlockSpec((tiles[a], owns[a].shape[1]), lambda i, me, a=a: (tile_of[a](i), 0)) for a in range(n)]
    return pl.pallas_call(
        body, name="grad_chip_sum",
        grid_spec=pltpu.PrefetchScalarGridSpec(
            num_scalar_prefetch=1, grid=(sum(counts),),
            in_specs=own_specs + [pl.BlockSpec((N_SHARD, tiles[a], owns[a].shape[1]), lambda i, me, a=a: (0, tile_of[a](i), 0))
                                  for a in range(n)],
            out_specs=own_specs),
        out_shape=[jax.ShapeDtypeStruct(o.shape, F32) for o in owns],
        compiler_params=_params(1),
    )(me_idx, *owns, *gots)


def _pair_share(halves):
    n = len(halves)

    def body(*refs):
        src, dst = refs[:n], refs[n:2 * n]
        send, recv = refs[2 * n:]
        x, y, c, _ = _position()
        copies = []
        for a in range(n):
            cp = pltpu.make_async_remote_copy(src_ref=src[a], dst_ref=dst[a], send_sem=send.at[a],
                                              recv_sem=recv.at[a], device_id=(x, y, 1 - c), device_id_type=MESH)
            cp.start()
            copies.append(cp)
        for cp in copies:
            cp.wait()

    return pl.pallas_call(
        body, name="grad_pair_share", in_specs=[ANY] * n, out_specs=[ANY] * n,
        out_shape=[jax.ShapeDtypeStruct(h.shape, h.dtype) for h in halves],
        scratch_shapes=[pltpu.SemaphoreType.DMA((n,)), pltpu.SemaphoreType.DMA((n,))],
    )(*halves)


def _small_allreduce(buf):
    rows, cols = buf.shape

    def body(src_ref, out_ref, slots, send, recv):
        x, y, c, _ = _position()
        me = 4 * x + 2 * y + c
        slots[me] = src_ref[...]
        copies = []
        k = 0
        for dx in (0, 1):
            for dy in (0, 1):
                for dc in (0, 1):
                    if (dx, dy, dc) == (0, 0, 0):
                        continue
                    peer = (jnp.where(dx, 1 - x, x), jnp.where(dy, 1 - y, y), jnp.where(dc, 1 - c, c))
                    cp = pltpu.make_async_remote_copy(src_ref=src_ref, dst_ref=slots.at[me], send_sem=send.at[k],
                                                      recv_sem=recv.at[k], device_id=peer, device_id_type=MESH)
                    cp.start()
                    copies.append(cp)
                    k += 1
        for cp in copies:
            cp.wait()
        total = slots[0]
        for dev in range(1, N_DEV):
            total = total + slots[dev]
        out_ref[...] = total

    vm = pl.BlockSpec(memory_space=pltpu.VMEM)
    return pl.pallas_call(
        body, name="small_allreduce", in_specs=[vm], out_specs=vm,
        out_shape=jax.ShapeDtypeStruct((rows, cols), F32),
        scratch_shapes=[pltpu.VMEM((N_DEV, rows, cols), F32), pltpu.SemaphoreType.DMA((N_DEV - 1,)),
                        pltpu.SemaphoreType.DMA((N_DEV - 1,))],
    )(buf)


def _adamw_math(w, gv, m, v):
    mn = ADAM_B1 * m + (1.0 - ADAM_B1) * gv
    vn = ADAM_B2 * v + (1.0 - ADAM_B2) * (gv * gv)
    m_hat = mn / (1.0 - ADAM_B1 ** ADAM_STEP)
    v_hat = vn / (1.0 - ADAM_B2 ** ADAM_STEP)
    return -ADAM_LR * (m_hat / (jnp.sqrt(v_hat) + ADAM_EPS) + ADAM_WD * w), mn, vn


def _adamw(w, g, m, v):
    rows, cols = w.shape
    tr = _row_tile(rows)

    def body(w_ref, g_ref, m_ref, v_ref, d_ref, mo_ref, vo_ref):
        d_ref[...], mo_ref[...], vo_ref[...] = _adamw_math(w_ref[...], g_ref[...], m_ref[...], v_ref[...])

    blk = pl.BlockSpec((tr, cols), lambda i: (i, 0))
    shp = jax.ShapeDtypeStruct((rows, cols), F32)
    return pl.pallas_call(
        body, name="adamw", grid=(rows // tr,), in_specs=[blk] * 4, out_specs=[blk] * 3, out_shape=[shp] * 3,
        compiler_params=_params(1),
    )(w, g, m, v)


def _adamw_halves(w, own, got, m, v, c_idx):
    rows, cols = w.shape
    tr = _row_tile(rows // 2)
    per_half = rows // 2 // tr

    def body(c_ref, w_ref, own_ref, got_ref, m_ref, v_ref, d_ref, mo_ref, vo_ref, g_ref):
        mine = (pl.program_id(0) // per_half) == c_ref[0]
        gv = jnp.where(mine, own_ref[...], got_ref[...])
        g_ref[...] = gv
        d_ref[...], mo_ref[...], vo_ref[...] = _adamw_math(w_ref[...], gv, m_ref[...], v_ref[...])

    blk = pl.BlockSpec((tr, cols), lambda i, c: (i, 0))
    own_blk = pl.BlockSpec((tr, cols), lambda i, c: (jnp.where(i // per_half == c[0], i % per_half, 0), 0))
    got_blk = pl.BlockSpec((tr, cols), lambda i, c: (jnp.where(i // per_half == c[0], 0, i % per_half), 0))
    shp = jax.ShapeDtypeStruct((rows, cols), F32)
    return pl.pallas_call(
        body, name="adamw_halves",
        grid_spec=pltpu.PrefetchScalarGridSpec(num_scalar_prefetch=1, grid=(rows // tr,),
                                               in_specs=[blk, own_blk, got_blk, blk, blk], out_specs=[blk] * 4),
        out_shape=[shp] * 4, compiler_params=_params(1),
    )(c_idx, w, own, got, m, v)


def _local_step(x, target, norm_gains, q_g, k_g, ng, weights_of, grads_done):
    s = x.shape[0]
    tm = min(512, s)
    tq = min(256, s)
    tf = min(1024, s)
    g1, gm, g2, gf = norm_gains
    cos2, sin2 = _rope_tables(s)
    gq8 = jnp.tile(q_g, (1, 8))
    gk2 = jnp.tile(k_g, (1, 2))

    tn = min(256, s)
    tk = min(1024, s)
    w1 = weights_of(1, ())
    x1, s1, t1, b1, h1 = _ffn_fwd(x, g1, w1["g1"], w1["u1"], w1["d1"], tf)
    w2 = weights_of(2, (x1,))
    lbl = w2["lbl"]
    pqkv, ph, pg, hm = _mix_in_fwd(x1, gm, w2["in"], tn)
    qe, kr, vr, vs = _qk_prep(pqkv, gq8, gk2, cos2, sin2, tm)
    oa, lse = _attn_fwd(qe, kr, vr, vs, tq)
    ob, pre, hstates = _hgrn_fwd(ph, lbl, ng)
    x2 = _mix_out_fwd(x1, oa, ob, pg, w2["a"], w2["b"], w2["o"], tm)
    w3 = weights_of(3, (x2,))
    x3, s2, t2, b2, h2 = _ffn_fwd(x2, g2, w3["g2"], w3["u2"], w3["d2"], tf)
    dx3, loss, dgf = _loss_head(x3, gf, target, tm)

    dx2, da2, db2, f2, dg2, dx3c = _ffn_bwd(dx3, x2, g2, s2, t2, b2, w3["g2"], w3["u2"], w3["d2"], tm)
    tok = grads_done(3, dict(g2=_dw_shared_b("dw_gate", da2, h2, tk, 1.0), u2=_dw_shared_b("dw_gate", db2, h2, tk, 1.0),
                             d2=_dw_shared_b("dw_down", f2, dx3c, tk, 1.0)))

    dpg, mg, dya, dyb, doe, delta, dob = _mix_out_bwd(dx2, oa, ob, pg, w2["a"], w2["b"], w2["o"], tm, tok)
    g_o = [g.reshape(N_SHARD, D_MODEL // N_SHARD, D_MODEL) for g in _dw_colblocks("dw_out", mg, dx2, 1, tk)]
    g_a = _dw_colblocks("dw_branch", oa, dya, N_SHARD, tk)
    g_b = _dw_colblocks("dw_branch", ob, dyb, N_SHARD, tk)
    dqt, dkt, dvt = _attn_bwd(qe, kr, kr.T, vr, doe, delta, lse, tq)
    dqkv, dgq, dgk = _qk_prep_bwd(pqkv, dqt.T, dkt.T, dvt.T, gq8, gk2, cos2, sin2, tm)
    dhq, dhff, dhfb, dhi, dhg, dlb, dng = _hgrn_bwd(ph, pre, dob, hstates, lbl, ng)
    dps = (dqkv, dhq, dhff, dhfb, dhi, dhg, dpg)
    g_in = [g.reshape(N_SHARD, -1, D_MODEL) for g in _dw_in(dps, hm, tk)]
    tok = grads_done(2, {"in": g_in, "a": g_a, "b": g_b, "o": g_o})
    dx1, dgm = _mix_in_bwd(dps, w2["in"], x1, dx2, gm, tn, tok)

    dx0, da1, db1, f1, dg1, dx1c = _ffn_bwd(dx1, x, g1, s1, t1, b1, w1["g1"], w1["u1"], w1["d1"], tm)
    grads_done(1, dict(g1=_dw_shared_b("dw_gate", da1, h1, tk, 1.0), u1=_dw_shared_b("dw_gate", db1, h1, tk, 1.0),
                       d1=_dw_shared_b("dw_down", f1, dx1c, tk, 1.0)))
    small = dict(g1=dg1, gm=dgm, g2=dg2, gf=dgf, gq=dgq, gk=dgk, lb=dlb, ng=dng)
    return loss, dx0, small, lbl


GROUPS = {1: ("g1", "u1", "d1"), 2: ("in", "a", "b", "o"), 3: ("g2", "u2", "d2")}
BIG = GROUPS[1] + GROUPS[2] + GROUPS[3]
TRANSPOSED = ("g1", "u1", "in", "g2", "u2")


def _pack_rows(vectors, width):
    rows = []
    for vct in vectors:
        flat = vct.reshape(-1)
        pad = (-flat.shape[0]) % width
        rows.append(jnp.pad(flat, (0, pad)).reshape(-1, width))
    return jnp.concatenate(rows, axis=0)


def kernel(x, ffn1_norm_g, ffn1_w_gate, ffn1_w_up, ffn1_w_down, mix_norm_g, w_in, q_norm_g, k_norm_g, hgrn_lb_logits, hgrn_out_norm_g, w_branch_attn, w_branch_hgrn, w_out, ffn2_norm_g, ffn2_w_gate, ffn2_w_up, ffn2_w_down, final_norm_g, loss_target, m_ffn1_norm_g, m_ffn1_w_gate, m_ffn1_w_up, m_ffn1_w_down, m_mix_norm_g, m_w_in, m_q_norm_g, m_k_norm_g, m_hgrn_lb_logits, m_hgrn_out_norm_g, m_w_branch_attn, m_w_branch_hgrn, m_w_out, m_ffn2_norm_g, m_ffn2_w_gate, m_ffn2_w_up, m_ffn2_w_down, m_final_norm_g, v_ffn1_norm_g, v_ffn1_w_gate, v_ffn1_w_up, v_ffn1_w_down, v_mix_norm_g, v_w_in, v_q_norm_g, v_k_norm_g, v_hgrn_lb_logits, v_hgrn_out_norm_g, v_w_branch_attn, v_w_branch_hgrn, v_w_out, v_ffn2_norm_g, v_ffn2_w_gate, v_ffn2_w_up, v_ffn2_w_down, v_final_norm_g):
    xi, yi, ci = lax.axis_index("x"), lax.axis_index("y"), lax.axis_index("c")
    me = 2 * xi + yi
    c_idx = jnp.reshape(ci, (1,)).astype(jnp.int32)
    me_idx = jnp.reshape(me, (1,)).astype(jnp.int32)

    big_w = dict(g1=ffn1_w_gate[0], u1=ffn1_w_up[0], d1=ffn1_w_down[0], a=w_branch_attn[0], b=w_branch_hgrn[0],
                 o=w_out[0], g2=ffn2_w_gate[0], u2=ffn2_w_up[0], d2=ffn2_w_down[0])
    big_w["in"] = w_in[0]
    big_m = dict(g1=m_ffn1_w_gate[0], u1=m_ffn1_w_up[0], d1=m_ffn1_w_down[0], a=m_w_branch_attn[0], b=m_w_branch_hgrn[0],
                 o=m_w_out[0], g2=m_ffn2_w_gate[0], u2=m_ffn2_w_up[0], d2=m_ffn2_w_down[0])
    big_m["in"] = m_w_in[0]
    big_v = dict(g1=v_ffn1_w_gate[0], u1=v_ffn1_w_up[0], d1=v_ffn1_w_down[0], a=v_w_branch_attn[0], b=v_w_branch_hgrn[0],
                 o=v_w_out[0], g2=v_ffn2_w_gate[0], u2=v_ffn2_w_up[0], d2=v_ffn2_w_down[0])
    big_v["in"] = v_w_in[0]
    for table in (big_w, big_m, big_v):
        for n in TRANSPOSED:
            table[n] = table[n].T

    started, token = {}, ()
    for grp in (1, 2, 3):
        shards = [big_w[n] for n in GROUPS[grp]] + ([hgrn_lb_logits.reshape(4, LANES)] if grp == 2 else [])
        dtypes = [CD] * len(GROUPS[grp]) + ([F32] if grp == 2 else [])
        lands = _cast_into_slots(shards, dtypes, me_idx)
        send, recv, _, lands, tok = _exchange_start("gather%d_start" % grp, [], lands, token)
        started[grp], token = (send, recv, lands), (tok,)

    def weights_of(grp, after):
        send, recv, lands = started[grp]
        got = _exchange_wait("gather%d_wait" % grp, send, recv, [], lands, tuple(after) + (token if grp == 1 else ()))
        by_halves = [i for i, land in enumerate(got) if _halved(land)]
        for i, whole in zip(by_halves, _pair_fill("gather%d_fill" % grp, [got[i] for i in by_halves])):
            got[i] = whole
        w = dict(zip(GROUPS[grp], got))
        if grp == 2:
            w["in"] = w["in"].reshape(-1, D_MODEL)
            w["o"] = w["o"].reshape(D_MODEL, D_MODEL)
            w["lbl"] = jnp.transpose(got[-1], (1, 0, 2)).reshape(4, N_SHARD * LANES)
        return w

    pending = {}

    def grads_done(grp, grads):
        names = list(grads)
        got = _pair_exchange([grads[n][1] for n in names])
        res = _pair_sum([grads[n][0] for n in names], got, c_idx, me_idx)
        sums, owns = res[:len(names)], res[len(names):]
        lands = [lax.empty(s_.shape, s_.dtype) for s_ in sums]
        send, recv, srcs, lands, tok = _exchange_start("reduce%d_start" % grp, list(sums), lands, ())
        pending[grp] = (names, send, recv, srcs, lands, owns, tok)
        return (tok,)

    def reduced_halves(grp, after):
        names, send, recv, srcs, lands, owns, _ = pending[grp]
        parts = _exchange_wait("reduce%d_wait" % grp, send, recv, srcs, lands, after)
        return names, list(_chip_sum(list(owns), parts, me_idx))

    loss, dx, small, lbl = _local_step(
        x[0], loss_target[0], (ffn1_norm_g, mix_norm_g, ffn2_norm_g, final_norm_g.reshape(1, -1)),
        q_norm_g, k_norm_g, hgrn_out_norm_g, weights_of, grads_done)

    dgq = small["gq"].reshape(8, HEAD_DIM).sum(axis=0)
    dgk = small["gk"].reshape(2, HEAD_DIM).sum(axis=0)
    lb_full = _hgrn_lower_bounds(lbl)
    dlog = []
    for d in (0, 1):
        t = small["lb"][d:d + 1] * lb_full[d] * (1.0 - lb_full[d])
        dlog += [t, -t]
    small_list = [small["g1"], small["gm"], small["g2"], small["gf"], small["ng"], dgq, dgk, jnp.concatenate(dlog, axis=0), loss[0, 0]]
    packed = _pack_rows(small_list, D_MODEL)
    n_rows = packed.shape[0]
    packed = jnp.pad(packed, ((0, (-n_rows) % 8), (0, 0)))
    red = _small_allreduce(packed)
    loss_out = red[n_rows - 1, 0]
    sg = dict(g1=red[0:1], gm=red[1:2], g2=red[2:3], gf=red[3], ng=red[4:5, :512], gq=red[5:6, :HEAD_DIM],
              gk=red[6:7, :HEAD_DIM])
    dlog_full = red[7:9].reshape(2, 2, 512)
    sg["lb"] = lax.dynamic_slice_in_dim(dlog_full, me * LANES, LANES, axis=2)

    small_w = dict(g1=ffn1_norm_g, gm=mix_norm_g, g2=ffn2_norm_g, gf=final_norm_g, ng=hgrn_out_norm_g, gq=q_norm_g,
                   gk=k_norm_g, lb=hgrn_lb_logits)
    small_m = dict(g1=m_ffn1_norm_g, gm=m_mix_norm_g, g2=m_ffn2_norm_g, gf=m_final_norm_g, ng=m_hgrn_out_norm_g,
                   gq=m_q_norm_g, gk=m_k_norm_g, lb=m_hgrn_lb_logits)
    small_v = dict(g1=v_ffn1_norm_g, gm=v_mix_norm_g, g2=v_ffn2_norm_g, gf=v_final_norm_g, ng=v_hgrn_out_norm_g,
                   gq=v_q_norm_g, gk=v_k_norm_g, lb=v_hgrn_lb_logits)
    small_names = ("g1", "gm", "g2", "gf", "ng", "gq", "gk", "lb")
    pack = lambda dct: _pack_rows([dct[n] for n in small_names], D_MODEL)
    pw, pgr, pm, pv = pack(small_w), pack(sg), pack(small_m), pack(small_v)
    pad8 = lambda a: jnp.pad(a, ((0, (-a.shape[0]) % 8), (0, 0)))
    sd, sm_, sv_ = _adamw(pad8(pw), pad8(pgr), pad8(pm), pad8(pv))

    def unpack(buf):
        out, r = {}, 0
        for n in small_names:
            size = small_w[n].size
            nr = -(-size // D_MODEL)
            out[n] = buf[r:r + nr].reshape(-1)[:size].reshape(small_w[n].shape)
            r += nr
        return out

    sdelta, snew_m, snew_v = unpack(sd), unpack(sm_), unpack(sv_)
    sgrad = {n: sg[n].reshape(small_w[n].shape) for n in small_names}

    bdelta, bnew_m, bnew_v, bgrad = {}, {}, {}, {}

    def update(names, halves):
        for n, own, got in zip(names, halves, _pair_share(halves)):
            res = _adamw_halves(big_w[n], own, got, big_m[n], big_v[n], c_idx)
            if n in TRANSPOSED:
                res = [r.T for r in res]
            bdelta[n], bnew_m[n], bnew_v[n], bgrad[n] = [r[None] for r in res]

    names3, halves3 = reduced_halves(3, (pending[1][-1],))
    names2, halves2 = reduced_halves(2, (halves3[0],))
    update(names3 + names2, halves3 + halves2)
    names1, halves1 = reduced_halves(1, (bdelta[names2[-1]],))
    update(names1, halves1)

    order = [("s", "g1"), ("b", "g1"), ("b", "u1"), ("b", "d1"), ("s", "gm"), ("b", "in"), ("s", "gq"), ("s", "gk"),
             ("s", "lb"), ("s", "ng"), ("b", "a"), ("b", "b"), ("b", "o"), ("s", "g2"), ("b", "g2"), ("b", "u2"),
             ("b", "d2"), ("s", "gf")]
    outs = [loss_out, dx[None]]
    for table_s, table_b in ((sgrad, bgrad), (sdelta, bdelta), (snew_m, bnew_m), (snew_v, bnew_v)):
        outs += [(table_s if kind == "s" else table_b)[n] for kind, n in order]
    return tuple(outs)
```

```python
import functools

import numpy as np
import jax
import jax.numpy as jnp
from jax import lax
from jax.experimental import pallas as pl
from jax.experimental.pallas import tpu as pltpu

F32 = jnp.float32
BF16 = jnp.bfloat16
CD = jnp.bfloat16

EPS = 1e-6
D_MODEL = 1024
HEAD_DIM = 64
GRID_W = 64
ROPE_THETA = 10000.0
CHUNK = 32
N_SHARD = 4
N_DEV = 8
VMEM_LIMIT = 56 * 1024 * 1024
HGRN_BWD_VMEM = 60 * 1024 * 1024
LANES = 128
HG_TILE = 128
ATTN_BWD_HEADS = 2
FFN_ROWS = 256

ADAM_LR = 0.001
ADAM_B1 = 0.9
ADAM_B2 = 0.999
ADAM_EPS = 1e-08
ADAM_WD = 0.01
ADAM_STEP = 10

NN = (((1,), (0,)), ((), ()))
NT = (((1,), (1,)), ((), ()))
TN = (((0,), (0,)), ((), ()))
MESH = pl.DeviceIdType.MESH
ANY = pl.BlockSpec(memory_space=pl.ANY)


def _mm(a, b, dn):
    return lax.dot_general(a.astype(CD), b.astype(CD), dn, preferred_element_type=F32)


def _split3(x):
    hi = x.astype(BF16)
    r = x - hi.astype(F32)
    mid = r.astype(BF16)
    lo = (r - mid.astype(F32)).astype(BF16)
    return hi, mid, lo


def _xdot(x, m):
    rows = x.shape[0]
    hi, mid, _ = _split3(x)
    r = lax.dot_general(jnp.concatenate([hi, mid], axis=0), m, NN, preferred_element_type=F32)
    return r[:rows] + r[rows:]


def _xdot_l(m, x):
    cols = x.shape[1]
    hi, mid, _ = _split3(x)
    r = lax.dot_general(m, jnp.concatenate([hi, mid], axis=1), NN, preferred_element_type=F32)
    return r[:, :cols] + r[:, cols:]


def _params(n_grid, vmem_limit=VMEM_LIMIT):
    return pltpu.CompilerParams(dimension_semantics=("arbitrary",) * n_grid, vmem_limit_bytes=vmem_limit)


def _sigmoid(x):
    return jax.nn.sigmoid(x)


def _np_blocksum(n):
    i = np.arange(n)
    return (i[:, None] // HEAD_DIM == i[None, :] // HEAD_DIM).astype(np.float32)


def _np_swap32(n):
    i = np.arange(n)
    partner = np.where(i % HEAD_DIM < HEAD_DIM // 2, i + HEAD_DIM // 2, i - HEAD_DIM // 2)
    m = np.zeros((n, n), np.float32)
    m[i, partner] = 1.0
    return m


def _np_expand_q():
    m = np.zeros((512, 1024), np.float32)
    for h in range(8):
        g = h // 4
        for d in range(HEAD_DIM):
            m[64 * h + d, 128 * h + 64 * g + d] = 1.0
    return m


def _np_headsum_spread():
    m = np.zeros((512, 1024), np.float32)
    for h in range(8):
        m[64 * h:64 * h + 64, 128 * h:128 * h + 128] = 1.0
    return m


def _np_swap_halves():
    m = np.zeros((128, 128), np.float32)
    i = np.arange(128)
    m[i, (i + 64) % 128] = 1.0
    return m


def _np_hgrn_cums(t, rev):
    r = np.arange(t)[:, None]
    c = np.arange(t)[None, :]
    same = (r // CHUNK) == (c // CHUNK)
    if not rev:
        cum = same & (c <= r)
        mid = same & (c % CHUNK <= CHUNK // 2 - 1)
    else:
        cum = same & (c >= r)
        mid = same & (c % CHUNK >= CHUNK // 2)
    return np.concatenate([cum, mid, same], axis=0).astype(np.float32)


def _bf(a):
    return jnp.asarray(a, dtype=BF16)


def _rope_tables(seq_len):
    rows = seq_len // GRID_W
    row = jnp.repeat(jnp.arange(rows, dtype=F32), GRID_W)
    col = jnp.tile(jnp.arange(GRID_W, dtype=F32), rows)
    n_freq = HEAD_DIM // 4
    inv = ROPE_THETA ** (-jnp.arange(n_freq, dtype=F32) / n_freq)
    ang = jnp.concatenate([row[:, None] * inv, col[:, None] * inv], axis=-1)
    cos, sin = jnp.cos(ang), jnp.sin(ang)
    c64 = jnp.concatenate([cos, cos], axis=-1)
    s64 = jnp.concatenate([-sin, sin], axis=-1)
    return jnp.tile(c64, (1, 2)), jnp.tile(s64, (1, 2))


def _ffn_fwd(x, g, wg, wu, wd, tm):
    s, d = x.shape
    nsh, fs, _ = wg.shape

    def body(x_ref, g_ref, wg_ref, wu_ref, wd_ref, xo_ref, a_ref, da_ref, b_ref, hb_ref, acc, hs):
        j = pl.program_id(1)

        @pl.when(j == 0)
        def _():
            xv = x_ref[...]
            r = lax.rsqrt(jnp.mean(xv * xv, axis=-1, keepdims=True) + EPS)
            h = (xv * r * g_ref[...]).astype(CD)
            hs[...] = h
            hb_ref[...] = h
            acc[...] = jnp.zeros_like(acc)

        blocks = [slice(r0, min(r0 + FFN_ROWS, tm)) for r0 in range(0, tm, FFN_ROWS)]
        firsts = [(_mm(hs[rows, :], wg_ref[0], NT), _mm(hs[rows, :], wu_ref[0], NT)) for rows in blocks]
        for rows, (a, b) in zip(blocks, firsts):
            sg = _sigmoid(a)
            silu = a * sg
            acc[rows, :] += _mm(silu * b, wd_ref[0], NN)
            a_ref[0, rows, :] = silu.astype(CD)
            da_ref[0, rows, :] = (sg * (1.0 + a * (1.0 - sg))).astype(CD)
            b_ref[0, rows, :] = b.astype(CD)

        @pl.when(j == nsh - 1)
        def _():
            xo_ref[...] = x_ref[...] + 0.5 * acc[...]

    return pl.pallas_call(
        body, name="ffn_fwd", grid=(s // tm, nsh),
        in_specs=[pl.BlockSpec((tm, d), lambda i, j: (i, 0)), pl.BlockSpec((1, d), lambda i, j: (0, 0))]
        + [pl.BlockSpec((1, fs, d), lambda i, j: (j, 0, 0))] * 3,
        out_specs=[pl.BlockSpec((tm, d), lambda i, j: (i, 0))] + [pl.BlockSpec((1, tm, fs), lambda i, j: (j, i, 0))] * 3
        + [pl.BlockSpec((tm, d), lambda i, j: (i, 0))],
        out_shape=[jax.ShapeDtypeStruct((s, d), F32)] + [jax.ShapeDtypeStruct((nsh, s, fs), CD)] * 3
        + [jax.ShapeDtypeStruct((s, d), CD)],
        scratch_shapes=[pltpu.VMEM((tm, d), F32), pltpu.VMEM((tm, d), CD)],
        compiler_params=_params(2),
    )(x, g, wg, wu, wd)


def _ffn_bwd(dout, x, g, silu, dsilu, b, wg, wu, wd, tm):
    s, d = x.shape
    nsh, fs, _ = wg.shape

    def body(do_ref, x_ref, g_ref, sl_ref, ds_ref, b_ref, wg_ref, wu_ref, wd_ref,
             dx_ref, da_ref, db_ref, f_ref, dg_ref, do16_ref, dh):
        i = pl.program_id(0)
        j = pl.program_id(1)

        @pl.when(j == 0)
        def _():
            dh[...] = jnp.zeros_like(dh)
            do16_ref[...] = (0.5 * do_ref[...]).astype(CD)

        @pl.when((i == 0) & (j == 0))
        def _():
            dg_ref[...] = jnp.zeros_like(dg_ref)

        blocks = [slice(r0, min(r0 + FFN_ROWS, tm)) for r0 in range(0, tm, FFN_ROWS)]
        dfs = [_mm(do16_ref[rows, :], wd_ref[0], NT) for rows in blocks]
        das, dbs = [], []
        for rows, df in zip(blocks, dfs):
            sl = sl_ref[0, rows, :].astype(F32)
            bv = b_ref[0, rows, :].astype(F32)
            da = (df * bv * ds_ref[0, rows, :].astype(F32)).astype(CD)
            db = (df * sl).astype(CD)
            da_ref[0, rows, :] = da
            db_ref[0, rows, :] = db
            f_ref[0, rows, :] = (sl * bv).astype(CD)
            das.append(da)
            dbs.append(db)
        for rows, da, db in zip(blocks, das, dbs):
            dh[rows, :] += _mm(da, wg_ref[0], NN) + _mm(db, wu_ref[0], NN)

        @pl.when(j == nsh - 1)
        def _():
            xv = x_ref[...]
            r = lax.rsqrt(jnp.mean(xv * xv, axis=-1, keepdims=True) + EPS)
            dhv = dh[...]
            u = dhv * g_ref[...]
            dx_ref[...] = do_ref[...] + r * u - xv * (r * r * r) * jnp.mean(u * xv, axis=-1, keepdims=True)
            dg_ref[...] += jnp.sum(dhv * xv * r, axis=0, keepdims=True)

    act = pl.BlockSpec((1, tm, fs), lambda i, j: (j, i, 0))
    row = pl.BlockSpec((tm, d), lambda i, j: (i, 0))
    row_in = pl.BlockSpec((tm, d), lambda i, j: (i, 0), pipeline_mode=pl.Buffered(1))
    return pl.pallas_call(
        body, name="ffn_bwd", grid=(s // tm, nsh),
        in_specs=[row_in, row_in, pl.BlockSpec((1, d), lambda i, j: (0, 0)), act, act, act]
        + [pl.BlockSpec((1, fs, d), lambda i, j: (j, 0, 0))] * 3,
        out_specs=[row, act, act, act, pl.BlockSpec((1, d), lambda i, j: (0, 0)), row],
        out_shape=[jax.ShapeDtypeStruct((s, d), F32), jax.ShapeDtypeStruct((nsh, s, fs), CD),
                   jax.ShapeDtypeStruct((nsh, s, fs), CD), jax.ShapeDtypeStruct((nsh, s, fs), CD),
                   jax.ShapeDtypeStruct((1, d), F32), jax.ShapeDtypeStruct((s, d), CD)],
        scratch_shapes=[pltpu.VMEM((tm, d), F32)],
        compiler_params=_params(2, HGRN_BWD_VMEM),
    )(dout, x, g, silu, dsilu, b, wg, wu, wd)


def _tn_call(name, operands, in_specs, out_shape, out_spec, grid, acc_shape, pick, scale=1.0):
    nk = grid[-1]
    n_in = len(operands)

    def body(*refs):
        out_ref, out16_ref, acc = refs[n_in], refs[n_in + 1], refs[n_in + 2]
        k = pl.program_id(len(grid) - 1)

        @pl.when(k == 0)
        def _():
            acc[...] = jnp.zeros_like(acc)

        pick(refs[:n_in], acc)

        @pl.when(k == nk - 1)
        def _():
            res = (acc[...] if scale == 1.0 else acc[...] * scale).reshape(out_ref.shape)
            out_ref[...] = res
            out16_ref[...] = res.astype(CD)

    return pl.pallas_call(
        body, name=name, grid=grid, in_specs=in_specs, out_specs=[out_spec, out_spec],
        out_shape=[out_shape, jax.ShapeDtypeStruct(out_shape.shape, CD)],
        scratch_shapes=[pltpu.VMEM(acc_shape, F32)], compiler_params=_params(len(grid)),
    )(*operands)


def _dw_shared_b(name, a3, b, tk, scale):
    nj, s, m = a3.shape
    n = b.shape[1]

    def pick(refs, acc):
        rows = pl.ds(pl.multiple_of(pl.program_id(1) * tk, tk), tk)
        acc[...] += _mm(refs[0][0], refs[1][rows, :], TN)

    return _tn_call(name, (a3, b),
                    [pl.BlockSpec((1, tk, m), lambda j, k: (j, k, 0)), pl.BlockSpec((s, n), lambda j, k: (0, 0))],
                    jax.ShapeDtypeStruct((nj, m, n), F32), pl.BlockSpec((1, m, n), lambda j, k: (j, 0, 0)),
                    (nj, s // tk), (m, n), pick, scale)


def _dw_colblocks(name, a, b, nj, tk):
    s, m = a.shape
    n = b.shape[1] // nj
    nk = s // tk

    def body(a_ref, b_ref, out_ref, out16_ref, acc):
        k = pl.program_id(0)

        @pl.when(k == 0)
        def _():
            acc[...] = jnp.zeros_like(acc)

        acc[...] += _mm(a_ref[...], b_ref[...], TN)

        @pl.when(k == nk - 1)
        def _():
            for j in range(nj):
                res = acc[:, j * n:(j + 1) * n]
                out_ref[j] = res
                out16_ref[j] = res.astype(CD)

    whole = pl.BlockSpec((nj, m, n), lambda k: (0, 0, 0))
    return pl.pallas_call(
        body, name=name, grid=(nk,),
        in_specs=[pl.BlockSpec((tk, m), lambda k: (k, 0)), pl.BlockSpec((tk, nj * n), lambda k: (k, 0))],
        out_specs=[whole, whole],
        out_shape=[jax.ShapeDtypeStruct((nj, m, n), F32), jax.ShapeDtypeStruct((nj, m, n), CD)],
        scratch_shapes=[pltpu.VMEM((m, nj * n), F32)], compiler_params=_params(1),
    )(a, b)


DP_WIDTHS = (768, 512, 512, 512, 512, 512, 2048)
DW_IN_COLS = 512


def _dw_in(dps, hb, tk):
    s, d = hb.shape
    nk = s // tk
    blocks, row = [], 0
    for p, width in enumerate(DP_WIDTHS):
        step = width if width <= 768 else DW_IN_COLS
        for c0 in range(0, width, step):
            blocks.append((p, c0, step, row))
            row += step
    nb, max_w = len(blocks), max(b[2] for b in blocks)
    first = [min(i for i, b in enumerate(blocks) if b[0] == p) for p in range(len(DP_WIDTHS))]
    count = [sum(1 for b in blocks if b[0] == p) for p in range(len(DP_WIDTHS))]

    def body(*refs):
        dp_refs, hb_ref, out_ref, out16_ref, acc, acc16, sems = refs[:7], refs[7], refs[8], refs[9], refs[10], refs[11], refs[12]
        b, k = pl.program_id(0), pl.program_id(1)
        rows = pl.ds(pl.multiple_of(k * tk, tk), tk)

        def writes(i):
            _, _, w, r0 = blocks[i]
            slot = i % 2
            return (pltpu.make_async_copy(acc.at[slot, 0:w], out_ref.at[r0:r0 + w], sems.at[slot, 0]),
                    pltpu.make_async_copy(acc16.at[slot, 0:w], out16_ref.at[r0:r0 + w], sems.at[slot, 1]))

        for i, (p, _, w, _) in enumerate(blocks):
            @pl.when(b == i)
            def _(i=i, p=p, w=w):
                slot = i % 2
                prod = _mm(dp_refs[p][...], hb_ref[rows, :], TN)

                @pl.when(k == 0)
                def _():
                    acc[slot, 0:w] = prod

                @pl.when(k > 0)
                def _():
                    acc[slot, 0:w] += prod

                @pl.when(k == nk - 1)
                def _():
                    if i >= 1:
                        for cp in writes(i - 1):
                            cp.wait()
                    acc16[slot, 0:w] = acc[slot, 0:w].astype(CD)
                    for cp in writes(i):
                        cp.start()
                    if i == nb - 1:
                        for cp in writes(i):
                            cp.wait()

    def piece_spec(p):
        width = DP_WIDTHS[p]
        cols = width if width <= 768 else DW_IN_COLS

        def imap(b, k):
            active = (b >= first[p]) & (b < first[p] + count[p])
            return (jnp.where(active, k, jnp.where(b < first[p], 0, nk - 1)), jnp.clip(b - first[p], 0, count[p] - 1))

        return pl.BlockSpec((tk, cols), imap)

    return pl.pallas_call(
        body, name="dw_in", grid=(nb, nk),
        in_specs=[piece_spec(p) for p in range(len(DP_WIDTHS))] + [pl.BlockSpec((s, d), lambda b, k: (0, 0))],
        out_specs=[ANY, ANY],
        out_shape=[jax.ShapeDtypeStruct((sum(DP_WIDTHS), d), F32), jax.ShapeDtypeStruct((sum(DP_WIDTHS), d), CD)],
        scratch_shapes=[pltpu.VMEM((2, max_w, d), F32), pltpu.VMEM((2, max_w, d), CD), pltpu.SemaphoreType.DMA((2, 2))],
        compiler_params=_params(2),
    )(*dps, hb)


def _mix_in_fwd(x, g, w_t, tm):
    s, d = x.shape
    n_in = w_t.shape[0]

    def body(x_ref, g_ref, w_ref, qkv_ref, hg_ref, gt_ref, hb_ref):
        xv = x_ref[...]
        r = lax.rsqrt(jnp.mean(xv * xv, axis=-1, keepdims=True) + EPS)
        h = (xv * r * g_ref[...]).astype(CD)
        hb_ref[...] = h
        off = DP_WIDTHS[0]
        qkv_ref[...] = _mm(h, w_ref[0:off, :], NT)
        for c, width in enumerate(DP_WIDTHS[1:6]):
            hg_ref[:, c * width:(c + 1) * width] = _mm(h, w_ref[off:off + width, :], NT)
            off += width
        gate = DP_WIDTHS[6] // 2
        for c in range(2):
            gt_ref[:, c * gate:(c + 1) * gate] = _mm(h, w_ref[off:off + gate, :], NT).astype(CD)
            off += gate

    row = lambda w: pl.BlockSpec((tm, w), lambda i: (i, 0))
    return pl.pallas_call(
        body, name="mix_in_fwd", grid=(s // tm,),
        in_specs=[row(d), pl.BlockSpec((1, d), lambda i: (0, 0)), pl.BlockSpec((n_in, d), lambda i: (0, 0))],
        out_specs=[row(768), row(2560), row(2048), row(d)],
        out_shape=[jax.ShapeDtypeStruct((s, 768), F32), jax.ShapeDtypeStruct((s, 2560), F32),
                   jax.ShapeDtypeStruct((s, 2048), CD), jax.ShapeDtypeStruct((s, d), CD)],
        compiler_params=_params(1),
    )(x, g, w_t)


def _mix_in_bwd(dps, w_t, x, dres, g, tm, after=()):
    s, d = x.shape
    n_in = w_t.shape[0]

    def body(*refs):
        refs = refs[len(after):]
        dp_refs = refs[:7]
        w_ref, x_ref, dr_ref, g_ref, dx_ref, dg_ref = refs[7:]

        @pl.when(pl.program_id(0) == 0)
        def _():
            dg_ref[...] = jnp.zeros_like(dg_ref)

        dhv = jnp.zeros((tm, d), F32)
        off = 0
        for ref, width in zip(dp_refs, DP_WIDTHS):
            dhv = dhv + _mm(ref[...], w_ref[off:off + width, :], NN)
            off += width
        xv = x_ref[...]
        r = lax.rsqrt(jnp.mean(xv * xv, axis=-1, keepdims=True) + EPS)
        u = dhv * g_ref[...]
        dx_ref[...] = dr_ref[...] + r * u - xv * (r * r * r) * jnp.mean(u * xv, axis=-1, keepdims=True)
        dg_ref[...] += jnp.sum(dhv * xv * r, axis=0, keepdims=True)

    row = pl.BlockSpec((tm, d), lambda i: (i, 0))
    vec = pl.BlockSpec((1, d), lambda i: (0, 0))
    return pl.pallas_call(
        body, name="mix_in_bwd", grid=(s // tm,),
        in_specs=[ANY] * len(after) + [pl.BlockSpec((tm, w), lambda i: (i, 0)) for w in DP_WIDTHS]
        + [pl.BlockSpec((n_in, d), lambda i: (0, 0)), row, row, vec],
        out_specs=[row, vec],
        out_shape=[jax.ShapeDtypeStruct((s, d), F32), jax.ShapeDtypeStruct((1, d), F32)],
        compiler_params=_params(1),
    )(*after, *dps, w_t, x, dres, g)


def _headnorm_rope(x, gain, cos, sin, blocksum, swap):
    ss = _xdot(x * x, blocksum)
    r = lax.rsqrt(ss * (1.0 / HEAD_DIM) + EPS)
    y = x * r * gain
    return y * cos + _xdot(y, swap) * sin, r


def _headnorm_rope_bwd(dz, x, gain, cos, sin, blocksum, swap):
    ss = _xdot(x * x, blocksum)
    r = lax.rsqrt(ss * (1.0 / HEAD_DIM) + EPS)
    dy = dz * cos + _xdot(dz * sin, swap)
    u = dy * gain
    mean_ux = _xdot(u * x, blocksum) * (1.0 / HEAD_DIM)
    dx = r * u - x * (r * r * r) * mean_ux
    return dx, jnp.sum(dy * x * r, axis=0, keepdims=True)


def _qk_prep(pqkv, gq, gk, cos2, sin2, tm):
    s = pqkv.shape[0]
    bs512, sw512, eq, swh = _bf(_np_blocksum(512)), _bf(_np_swap32(512)), _bf(_np_expand_q()), _bf(_np_swap_halves())

    def body(q_ref, kv_ref, gq_ref, gk_ref, c_ref, s_ref, bs_ref, sw_ref, eq_ref, swh_ref, qe_ref, k_ref, v_ref, vs_ref):
        c2, s2 = c_ref[...], s_ref[...]
        c8, s8 = jnp.tile(c2, (1, 4)), jnp.tile(s2, (1, 4))
        bs, sw = bs_ref[...], sw_ref[...]
        zq, _ = _headnorm_rope(q_ref[...], gq_ref[...], c8, s8, bs, sw)
        qe_ref[...] = _mm(zq * (HEAD_DIM ** -0.5), eq_ref[...], NN).astype(CD)
        kv = kv_ref[...]
        zk, _ = _headnorm_rope(kv[:, :LANES], gk_ref[...], c2, s2, bs[:LANES, :LANES], sw[:LANES, :LANES])
        k_ref[...] = zk.astype(CD)
        v = kv[:, LANES:]
        v_ref[...] = v.astype(CD)
        vs_ref[...] = _mm(v, swh_ref[...], NN).astype(CD)

    full = lambda a: pl.BlockSpec(a.shape, lambda i: (0,) * a.ndim)
    tab = pl.BlockSpec((tm, LANES), lambda i: (i, 0))
    return pl.pallas_call(
        body, name="qk_prep", grid=(s // tm,),
        in_specs=[pl.BlockSpec((tm, 512), lambda i: (i, 0)), pl.BlockSpec((tm, 256), lambda i: (i, 2)),
                  full(gq), full(gk), tab, tab, full(bs512), full(sw512), full(eq), full(swh)],
        out_specs=[pl.BlockSpec((tm, 1024), lambda i: (i, 0)), tab, tab, tab],
        out_shape=[jax.ShapeDtypeStruct((s, 1024), CD)] + [jax.ShapeDtypeStruct((s, LANES), CD)] * 3,
        compiler_params=_params(1),
    )(pqkv, pqkv, gq, gk, cos2, sin2, bs512, sw512, eq, swh)


def _qk_prep_bwd(pqkv, dq, dk, dv, gq, gk, cos2, sin2, tm):
    s = pqkv.shape[0]
    bs512, sw512 = _bf(_np_blocksum(512)), _bf(_np_swap32(512))

    def body(q_ref, kv_ref, dq_ref, dk_ref, dv_ref, gq_ref, gk_ref, c_ref, s_ref, bs_ref, sw_ref,
             dp_ref, dgq_ref, dgk_ref):
        @pl.when(pl.program_id(0) == 0)
        def _():
            dgq_ref[...] = jnp.zeros_like(dgq_ref)
            dgk_ref[...] = jnp.zeros_like(dgk_ref)

        c2, s2 = c_ref[...], s_ref[...]
        c8, s8 = jnp.tile(c2, (1, 4)), jnp.tile(s2, (1, 4))
        bs, sw = bs_ref[...], sw_ref[...]
        dzq = dq_ref[...] * (HEAD_DIM ** -0.5)
        dxq, dgq = _headnorm_rope_bwd(dzq, q_ref[...], gq_ref[...], c8, s8, bs, sw)
        kv = kv_ref[...]
        dxk, dgk = _headnorm_rope_bwd(dk_ref[...], kv[:, :LANES], gk_ref[...], c2, s2, bs[:LANES, :LANES], sw[:LANES, :LANES])
        dp_ref[...] = jnp.concatenate([dxq, dxk, dv_ref[...]], axis=1).astype(CD)
        dgq_ref[...] += dgq
        dgk_ref[...] += dgk

    full = lambda a: pl.BlockSpec(a.shape, lambda i: (0,) * a.ndim)
    tab = pl.BlockSpec((tm, LANES), lambda i: (i, 0))
    return pl.pallas_call(
        body, name="qk_prep_bwd", grid=(s // tm,),
        in_specs=[pl.BlockSpec((tm, 512), lambda i: (i, 0)), pl.BlockSpec((tm, 256), lambda i: (i, 2)),
                  pl.BlockSpec((tm, 512), lambda i: (i, 0)), tab, tab, full(gq), full(gk), tab, tab,
                  full(bs512), full(sw512)],
        out_specs=[pl.BlockSpec((tm, 768), lambda i: (i, 0)), pl.BlockSpec((1, 512), lambda i: (0, 0)),
                   pl.BlockSpec((1, LANES), lambda i: (0, 0))],
        out_shape=[jax.ShapeDtypeStruct((s, 768), CD), jax.ShapeDtypeStruct((1, 512), F32),
                   jax.ShapeDtypeStruct((1, LANES), F32)],
        compiler_params=_params(1),
    )(pqkv, pqkv, dq, dk, dv, gq, gk, cos2, sin2, bs512, sw512)


def _kv_rows(h):
    return pl.ds(pl.multiple_of((h // 4) * HEAD_DIM, HEAD_DIM), HEAD_DIM)


def _attn_fwd(qe, k, v, vs, tq):
    s = k.shape[0]

    def body(q0_ref, q1_ref, q2_ref, q3_ref, k_ref, v_ref, vs_ref, o_ref, lse_ref):
        grp = pl.program_id(0)
        kk = k_ref[...]
        heads = range(4)
        scores = [_mm(q_ref[...], kk, NT) for q_ref in (q0_ref, q1_ref, q2_ref, q3_ref)]
        mxs = [jnp.max(sc, axis=-1, keepdims=True) for sc in scores]
        es = [jnp.exp(scores[r] - mxs[r]) for r in heads]
        ls = [jnp.sum(e, axis=-1, keepdims=True) for e in es]
        for r in heads:
            lse_ref[r] = mxs[r] + jnp.log(ls[r])
        outs = [_mm(es[r], jnp.where(grp != r % 2, vs_ref[...], v_ref[...]), NN) * (1.0 / ls[r]) for r in heads]
        low = lax.broadcasted_iota(jnp.int32, (1, LANES), 1) < HEAD_DIM
        o_ref[...] = jnp.concatenate([jnp.where(low, outs[0], outs[1]), jnp.where(low, outs[2], outs[3])], axis=1)

    kv = pl.BlockSpec((s, LANES), lambda g, i: (0, 0))
    qblk = lambda r: pl.BlockSpec((tq, LANES), lambda g, i: (i, 4 * g + r))
    return pl.pallas_call(
        body, name="attn_fwd", grid=(2, s // tq),
        in_specs=[qblk(0), qblk(1), qblk(2), qblk(3), kv, kv, kv],
        out_specs=[pl.BlockSpec((tq, 2 * LANES), lambda g, i: (i, g)), pl.BlockSpec((4, tq, 1), lambda g, i: (g, i, 0))],
        out_shape=[jax.ShapeDtypeStruct((s, 512), F32), jax.ShapeDtypeStruct((8, s, 1), F32)],
        compiler_params=_params(2),
    )(qe, qe, qe, qe, k, v, vs)


def _attn_bwd(qe, k, kt, v, doe, delta, lse, tq):
    s = k.shape[0]

    nh = ATTN_BWD_HEADS

    def body(*refs):
        q_refs, (k_ref, kt_ref, v_ref) = refs[:nh], refs[nh:nh + 3]
        do_refs, dl_refs = refs[nh + 3:2 * nh + 3], refs[2 * nh + 3:3 * nh + 3]
        lse_ref, dqt_ref, dkt_ref, dvt_ref, qt, dot = refs[3 * nh + 3:]

        @pl.when((pl.program_id(0) == 0) & (pl.program_id(1) == 0))
        def _():
            dkt_ref[...] = jnp.zeros_like(dkt_ref)
            dvt_ref[...] = jnp.zeros_like(dvt_ref)

        rows = _kv_rows(nh * pl.program_id(0))
        kt = kt_ref[rows, :]
        dkt = jnp.zeros((HEAD_DIM, s), F32)
        dvt = jnp.zeros((HEAD_DIM, s), F32)
        firsts = [(_mm(q_ref[...], k_ref[...], NT), _mm(do_ref[...], v_ref[...], NT))
                  for q_ref, do_ref in zip(q_refs, do_refs)]
        for idx, (q_ref, do_ref, dl_ref) in enumerate(zip(q_refs, do_refs, dl_refs)):
            q, do = q_ref[...], do_ref[...]
            sc, dp = firsts[idx]
            p = jnp.exp(sc - lse_ref[idx])
            ds = p * (dp - jnp.max(dl_ref[...], axis=-1, keepdims=True))
            dqt_ref[idx * HEAD_DIM:(idx + 1) * HEAD_DIM, :] = _mm(kt, ds, NT)
            qt[idx] = jnp.transpose(q.astype(F32))
            dot[idx] = jnp.transpose(do.astype(F32))
            dkt = dkt + _mm(qt[idx, rows, :], ds, NN)
            dvt = dvt + _mm(dot[idx, rows, :], p, NN)
        dkt_ref[rows, :] += dkt
        dvt_ref[rows, :] += dvt

    kv = pl.BlockSpec((s, LANES), lambda m, i: (0, 0))
    kvt = pl.BlockSpec((LANES, s), lambda m, i: (0, 0))
    blks = [pl.BlockSpec((tq, LANES), lambda m, i, r=r: (i, nh * m + r)) for r in range(nh)]
    return pl.pallas_call(
        body, name="attn_bwd", grid=(8 // nh, s // tq),
        in_specs=blks + [kv, kvt, kv] + blks + blks + [pl.BlockSpec((nh, tq, 1), lambda m, i: (m, i, 0))],
        out_specs=[pl.BlockSpec((nh * HEAD_DIM, tq), lambda m, i: (m, i)), kvt, kvt],
        out_shape=[jax.ShapeDtypeStruct((8 * HEAD_DIM, s), F32), jax.ShapeDtypeStruct((LANES, s), F32),
                   jax.ShapeDtypeStruct((LANES, s), F32)],
        scratch_shapes=[pltpu.VMEM((nh, LANES, tq), F32), pltpu.VMEM((nh, LANES, tq), F32)],
        compiler_params=_params(2),
    )(*[qe] * nh, k, kt, v, *[doe] * nh, *[delta] * nh, lse)


@jax.custom_vjp
def _mm_nn(a, b):
    return _mm(a, b, NN)


_mm_nn.defvjp(lambda a, b: (_mm(a, b, NN), (a, b)),
              lambda res, g: (_mm(g, res[1], NT), _mm(res[0], g, TN)))


@jax.custom_vjp
def _mm_nt(a, b):
    return _mm(a, b, NT)


_mm_nt.defvjp(lambda a, b: (_mm(a, b, NT), (a, b)),
              lambda res, g: (_mm(g, res[1], NN), _mm(g, res[0], TN)))


@jax.custom_vjp
def _mm_tn(a, b):
    return _mm(a, b, TN)


_mm_tn.defvjp(lambda a, b: (_mm(a, b, TN), (a, b)),
              lambda res, g: (_mm(res[1], g, NT), _mm(res[0], g, NN)))


@jax.custom_vjp
def _cmm(m, mt, x):
    return _xdot_l(m, x)


_cmm.defvjp(lambda m, mt, x: (_xdot_l(m, x), (m, mt)),
            lambda res, g: (jnp.zeros_like(res[0]), jnp.zeros_like(res[1]), _xdot_l(res[1], g)))


def _hgrn_masks(t, rev):
    n_ch = t // CHUNK
    r = jnp.bitwise_and(lax.broadcasted_iota(jnp.int32, (2 * t, t), 0), t - 1)
    c = lax.broadcasted_iota(jnp.int32, (2 * t, t), 1)
    same = jnp.right_shift(r, 5) == jnp.right_shift(c, 5)
    tri2 = same & ((c >= r) if rev else (c <= r))
    pr = lax.broadcasted_iota(jnp.int32, (LANES, LANES), 0)
    pc = lax.broadcasted_iota(jnp.int32, (LANES, LANES), 1)
    diag = jnp.right_shift(pr, 6) == jnp.right_shift(pc, 6)
    qr = lax.broadcasted_iota(jnp.int32, (t, n_ch * LANES), 0)
    qc = lax.broadcasted_iota(jnp.int32, (t, n_ch * LANES), 1)
    rows_chunk = jnp.right_shift(qc, 7) == jnp.right_shift(qr, 5)
    vr = lax.broadcasted_iota(jnp.int32, (n_ch * LANES, t), 0)
    vc = lax.broadcasted_iota(jnp.int32, (n_ch * LANES, t), 1)
    cols_chunk = jnp.right_shift(vr, 7) == jnp.right_shift(vc, 5)
    return dict(tri2=tri2, diag=diag, rows_chunk=rows_chunk, cols_chunk=cols_chunk)


def _hgrn_gates(xf, lb):
    f = lb + (1.0 - lb) * _sigmoid(xf)
    return 1.0 - f, jnp.log(f)


def _hgrn_dir(*args):
    return _hgrn_dirs([args])[0][:2]


def _hgrn_dirs(arg_sets):
    chains = [_hgrn_phases(*a) for a in arg_sets]
    results = [None] * len(chains)
    while any(r is None for r in results):
        for n, chain in enumerate(chains):
            if results[n] is None:
                try:
                    next(chain)
                except StopIteration as done:
                    results[n] = done.value
    return results


def _hgrn_phases(xq, xf, v, lb, state, cm, cmt, mk, rev):
    t = xq.shape[0]
    n_ch = t // CHUNK
    lo = lax.broadcasted_iota(jnp.int32, (1, LANES), 1) < HEAD_DIM
    k, lf = _hgrn_gates(xf, lb)
    cs = _cmm(cm, cmt, lf)
    yield
    q = xq * _sigmoid(xq)
    b, bm, bl = cs[:t], cs[t:2 * t], cs[2 * t:]
    qd = q * jnp.exp(b - bm)
    kd = k * jnp.exp(bm - b)
    yield
    qd2 = jnp.concatenate([jnp.where(lo, qd, 0.0), jnp.where(lo, 0.0, qd)], axis=0)
    scores = _mm_nt(qd2, kd)
    yield
    kc = k * jnp.exp(bl - b)
    qe = q * jnp.exp(b)
    vexp = jnp.where(mk["cols_chunk"], jnp.concatenate([jnp.transpose(v)] * n_ch, axis=0), 0.0)
    adds = _mm_nn(vexp, kc)
    yield
    o2 = _mm_nn(jnp.where(mk["tri2"], scores, 0.0), v)
    o = jnp.where(lo, o2[:t], o2[t:])
    yield
    dec = jnp.exp(bl)
    entering = [None] * n_ch
    for c in (range(n_ch - 1, -1, -1) if rev else range(n_ch)):
        entering[c] = state
        d = jnp.concatenate([dec[c * CHUNK:(c + 1) * CHUNK]] * (LANES // CHUNK), axis=0)
        state = d * state + jnp.where(mk["diag"], adds[c * LANES:(c + 1) * LANES], 0.0)
    yield
    qexp = jnp.where(mk["rows_chunk"], jnp.concatenate([qe] * n_ch, axis=1), 0.0)
    middle = entering[n_ch // 2 - 1] if rev else entering[n_ch // 2]
    return o + _mm_nt(qexp, jnp.concatenate(entering, axis=1)), state, middle


def _hgrn_lower_bounds(l):
    out = []
    for d in (0, 1):
        l0, l1 = l[2 * d:2 * d + 1, :], l[2 * d + 1:2 * d + 2, :]
        mx = jnp.maximum(l0, l1)
        e0, e1 = jnp.exp(l0 - mx), jnp.exp(l1 - mx)
        out.append(e0 / (e0 + e1))
    return out


def _hgrn_consts(t):
    cf, cb = _np_hgrn_cums(t, False), _np_hgrn_cums(t, True)
    return (_bf(cf), _bf(cf.T), _bf(cb), _bf(cb.T), _bf(_np_blocksum(LANES)))


def _hgrn_fwd(ph, lbl, ng):
    s = ph.shape[0]
    t = 2 * min(HG_TILE, s // 2)
    nt = s // t
    consts = _hgrn_consts(t)

    def body(xq_ref, xff_ref, xfb_ref, xi_ref, xg_ref, lbl_ref, ng_ref, cf_ref, cft_ref, cb_ref, cbt_ref, bs_ref,
             o_ref, pre_ref, st_ref, acc):
        lbf, lbb = _hgrn_lower_bounds(lbl_ref)
        mk_f, mk_b = _hgrn_masks(t, False), _hgrn_masks(t, True)
        zero = jnp.zeros((LANES, LANES), F32)

        def rows_of(i):
            return pl.ds(pl.multiple_of(i * t, t), t)

        acc[...] = jnp.zeros_like(acc)

        def step(i, states):
            tb = nt - 1 - i
            rf, rb = rows_of(i), rows_of(tb)
            (of, sf, mid_f), (ob, sb, mid_b) = _hgrn_dirs([
                (xq_ref[rf, :], xff_ref[rf, :], xi_ref[rf, :], lbf, states[0], cf_ref[...], cft_ref[...], mk_f, False),
                (xq_ref[rb, :], xfb_ref[rb, :], xi_ref[rb, :], lbb, states[1], cb_ref[...], cbt_ref[...], mk_b, True)])
            st_ref[0, 0, 2 * i] = states[0]
            st_ref[0, 0, 2 * i + 1] = mid_f
            st_ref[0, 1, 2 * tb + 1] = states[1]
            st_ref[0, 1, 2 * tb] = mid_b
            acc[rf, :] += of
            acc[rb, :] += ob
            return sf, sb

        lax.fori_loop(0, nt, step, (zero, zero))

        def step_n(i, carry):
            rows = rows_of(i)
            o = acc[rows, :]
            ss = _xdot(o * o, bs_ref[...])
            r = lax.rsqrt(ss * (1.0 / HEAD_DIM) + EPS)
            xg = xg_ref[rows, :]
            pre_ref[rows, :] = o
            o_ref[rows, :] = ((o * r * ng_ref[...]) * (xg * _sigmoid(xg))).astype(CD)
            return carry

        lax.fori_loop(0, nt, step_n, 0)

    col = lambda off: pl.BlockSpec((s, LANES), lambda m: (0, off + m))
    full = lambda a: pl.BlockSpec(a.shape, lambda m: (0,) * a.ndim)
    return pl.pallas_call(
        body, name="hgrn_fwd", grid=(4,),
        in_specs=[col(0), col(4), col(8), col(12), col(16), pl.BlockSpec((4, LANES), lambda m: (0, m)),
                  pl.BlockSpec((1, LANES), lambda m: (0, m))] + [full(c) for c in consts],
        out_specs=[col(0), col(0), pl.BlockSpec((1, 2, 2 * nt, LANES, LANES), lambda m: (m, 0, 0, 0, 0))],
        out_shape=[jax.ShapeDtypeStruct((s, 512), CD), jax.ShapeDtypeStruct((s, 512), F32),
                   jax.ShapeDtypeStruct((4, 2, 2 * nt, LANES, LANES), F32)],
        scratch_shapes=[pltpu.VMEM((s, LANES), F32)],
        compiler_params=_params(1),
    )(ph, ph, ph, ph, ph, lbl, ng, *consts)


def _hgrn_bwd(ph, pre, dout, states, lbl, ng):
    s = ph.shape[0]
    t = min(HG_TILE, s)
    nt = s // t
    consts = _hgrn_consts(t)

    def body(xq_ref, xff_ref, xfb_ref, xi_ref, xg_ref, pre_ref, do_ref, st_ref, lbl_ref, ng_ref,
             cf_ref, cft_ref, cb_ref, cbt_ref, bs_ref,
             dq_ref, dff_ref, dfb_ref, di_ref, dg_ref, dlb_ref, dng_ref, dpre, dq_acc, dv_acc):
        lbf, lbb = _hgrn_lower_bounds(lbl_ref)
        mk_f, mk_b = _hgrn_masks(t, False), _hgrn_masks(t, True)
        zero = jnp.zeros((LANES, LANES), F32)
        zrow = jnp.zeros((1, LANES), F32)

        def rows_of(i):
            return pl.ds(pl.multiple_of(i * t, t), t)

        def step_n(i, dng):
            rows = rows_of(i)
            o, xg, do = pre_ref[rows, :], xg_ref[rows, :], do_ref[rows, :]
            bs = bs_ref[...]
            r = lax.rsqrt(_xdot(o * o, bs) * (1.0 / HEAD_DIM) + EPS)
            sg = _sigmoid(xg)
            gate = xg * sg
            don = do * gate
            dg_ref[rows, :] = (do * (o * r * ng_ref[...]) * (sg * (1.0 + xg * (1.0 - sg)))).astype(CD)
            u = don * ng_ref[...]
            dpre[rows, :] = r * u - o * (r * r * r) * (_xdot(u * o, bs) * (1.0 / HEAD_DIM))
            return dng + jnp.sum(don * o * r, axis=0, keepdims=True)

        dng_ref[...] = lax.fori_loop(0, nt, step_n, zrow)

        dq_acc[...] = jnp.zeros_like(dq_acc)
        dv_acc[...] = jnp.zeros_like(dv_acc)

        def step_g(i, carry):
            dsf, dsb, dlbf, dlbb = carry
            tf, tb = nt - 1 - i, i
            rf, rb = rows_of(tf), rows_of(tb)
            cf, cft, cb, cbt = cf_ref[...], cft_ref[...], cb_ref[...], cbt_ref[...]

            def both(xq_f, xf_f, v_f, lb_f, s_f, xq_b, xf_b, v_b, lb_b, s_b):
                (of, sf, _), (ob, sb, _) = _hgrn_dirs([(xq_f, xf_f, v_f, lb_f, s_f, cf, cft, mk_f, False),
                                                       (xq_b, xf_b, v_b, lb_b, s_b, cb, cbt, mk_b, True)])
                return of, sf, ob, sb

            _, vjp = jax.vjp(both, xq_ref[rf, :], xff_ref[rf, :], xi_ref[rf, :], lbf, st_ref[0, 0, tf],
                             xq_ref[rb, :], xfb_ref[rb, :], xi_ref[rb, :], lbb, st_ref[0, 1, tb])
            dq_f, dx_f, dv_f, gf, dsf, dq_b, dx_b, dv_b, gb, dsb = vjp((dpre[rf, :], dsf, dpre[rb, :], dsb))
            dff_ref[rf, :] = dx_f.astype(CD)
            dfb_ref[rb, :] = dx_b.astype(CD)
            dq_acc[rf, :] += dq_f
            dv_acc[rf, :] += dv_f
            dq_acc[rb, :] += dq_b
            dv_acc[rb, :] += dv_b
            return dsf, dsb, dlbf + gf, dlbb + gb

        _, _, dlbf, dlbb = lax.fori_loop(0, nt, step_g, (zero, zero, zrow, zrow))
        dlb_ref[0:1, :] = dlbf
        dlb_ref[1:2, :] = dlbb
        dq_ref[...] = dq_acc[...].astype(CD)
        di_ref[...] = dv_acc[...].astype(CD)

    col = lambda off: pl.BlockSpec((s, LANES), lambda m: (0, off + m))
    full = lambda a: pl.BlockSpec(a.shape, lambda m: (0,) * a.ndim)
    stream = jax.ShapeDtypeStruct((s, 512), CD)
    return pl.pallas_call(
        body, name="hgrn_bwd", grid=(4,),
        in_specs=[col(0), col(4), col(8), col(12), col(16), col(0), col(0),
                  pl.BlockSpec((1, 2, nt, LANES, LANES), lambda m: (m, 0, 0, 0, 0)),
                  pl.BlockSpec((4, LANES), lambda m: (0, m)),
                  pl.BlockSpec((1, LANES), lambda m: (0, m))] + [full(c) for c in consts],
        out_specs=[col(0)] * 5 + [pl.BlockSpec((2, LANES), lambda m: (0, m)), pl.BlockSpec((1, LANES), lambda m: (0, m))],
        out_shape=[stream] * 5 + [jax.ShapeDtypeStruct((2, 512), F32), jax.ShapeDtypeStruct((1, 512), F32)],
        scratch_shapes=[pltpu.VMEM((s, LANES), F32), pltpu.VMEM((s, LANES), F32), pltpu.VMEM((s, LANES), F32)],
        compiler_params=_params(1, HGRN_BWD_VMEM),
    )(ph, ph, ph, ph, ph, pre, dout, states, lbl, ng, *consts)


def _branch_out(o, w4):
    o = o.astype(CD)
    return jnp.concatenate([_mm(o, w4[j], NN) for j in range(N_SHARD)], axis=1)


def _mix_out_fwd(x, oa, ob, pg, wa, wb, wo, tm):
    s, d = x.shape

    def body(x_ref, oa_ref, ob_ref, ga_ref, gb_ref, wa_ref, wb_ref, wo_ref, xo_ref):
        ya = _branch_out(oa_ref[...], wa_ref)
        yb = _branch_out(ob_ref[...], wb_ref)
        merged = _sigmoid(ga_ref[...].astype(F32)) * ya + _sigmoid(gb_ref[...].astype(F32)) * yb
        xo_ref[...] = x_ref[...] + _mm(merged, wo_ref[...], NN)

    row = pl.BlockSpec((tm, d), lambda i: (i, 0))
    half = pl.BlockSpec((tm, 512), lambda i: (i, 0))
    full = lambda a: pl.BlockSpec(a.shape, lambda i: (0,) * a.ndim)
    return pl.pallas_call(
        body, name="mix_out_fwd", grid=(s // tm,),
        in_specs=[row, half, half, row, pl.BlockSpec((tm, d), lambda i: (i, 1)), full(wa), full(wb), full(wo)],
        out_specs=row, out_shape=jax.ShapeDtypeStruct((s, d), F32),
        compiler_params=_params(1),
    )(x, oa, ob, pg, pg, wa, wb, wo)


def _mix_out_bwd(dx, oa, ob, pg, wa, wb, wo, tm, after=()):
    s, d = dx.shape
    eq, ebc = _bf(_np_expand_q()), _bf(_np_headsum_spread())

    def body(*refs):
        (dx_ref, oa_ref, ob_ref, ga_ref, gb_ref, wa_ref, wb_ref, wo_ref, eq_ref, ebc_ref,
         dpg_ref, mg_ref, dya_ref, dyb_ref, doe_ref, dl_ref, dob_ref) = refs[len(after):]
        oa = oa_ref[...]
        ya = _branch_out(oa, wa_ref)
        yb = _branch_out(ob_ref[...], wb_ref)
        sa, sb = _sigmoid(ga_ref[...].astype(F32)), _sigmoid(gb_ref[...].astype(F32))
        mg_ref[...] = (sa * ya + sb * yb).astype(CD)
        dm = _mm(dx_ref[...], wo_ref[...], NT)
        dpg_ref[...] = jnp.concatenate([dm * ya * sa * (1.0 - sa), dm * yb * sb * (1.0 - sb)], axis=1).astype(CD)
        dya, dyb = dm * sa, dm * sb
        dya_ref[...] = dya.astype(CD)
        dyb_ref[...] = dyb.astype(CD)
        doa = jnp.zeros(oa.shape, F32)
        dob = jnp.zeros(oa.shape, F32)
        for j in range(N_SHARD):
            doa = doa + _mm(dya[:, 256 * j:256 * j + 256], wa_ref[j], NT)
            dob = dob + _mm(dyb[:, 256 * j:256 * j + 256], wb_ref[j], NT)
        dob_ref[...] = dob
        doe_ref[...] = _mm(doa, eq_ref[...], NN).astype(CD)
        dl_ref[...] = _xdot(doa * oa, ebc_ref[...])

    row = pl.BlockSpec((tm, d), lambda i: (i, 0))
    half = pl.BlockSpec((tm, 512), lambda i: (i, 0))
    full = lambda a: pl.BlockSpec(a.shape, lambda i: (0,) * a.ndim)
    wide = jax.ShapeDtypeStruct((s, d), CD)
    return pl.pallas_call(
        body, name="mix_out_bwd", grid=(s // tm,),
        in_specs=[ANY] * len(after) + [row, half, half, row, pl.BlockSpec((tm, d), lambda i: (i, 1)), full(wa), full(wb),
                                       full(wo), full(eq), full(ebc)],
        out_specs=[pl.BlockSpec((tm, 2048), lambda i: (i, 0)), row, row, row, row, row, half],
        out_shape=[jax.ShapeDtypeStruct((s, 2048), CD), wide, wide, wide, wide, jax.ShapeDtypeStruct((s, d), F32),
                   jax.ShapeDtypeStruct((s, 512), F32)],
        compiler_params=_params(1),
    )(*after, dx, oa, ob, pg, pg, wa, wb, wo, eq, ebc)


def _loss_head(x, g, target, tm):
    s, d = x.shape

    def body(x_ref, g_ref, t_ref, dx_ref, loss_ref, dg_ref):
        @pl.when(pl.program_id(0) == 0)
        def _():
            loss_ref[...] = jnp.zeros_like(loss_ref)
            dg_ref[...] = jnp.zeros_like(dg_ref)

        xv = x_ref[...]
        r = lax.rsqrt(jnp.mean(xv * xv, axis=-1, keepdims=True) + EPS)
        err = xv * r * g_ref[...] - t_ref[...]
        loss_ref[...] += 0.5 * jnp.sum(jnp.mean(err * err, axis=-1, keepdims=True))
        dy = err * (1.0 / d)
        u = dy * g_ref[...]
        dx_ref[...] = r * u - xv * (r * r * r) * jnp.mean(u * xv, axis=-1, keepdims=True)
        dg_ref[...] += jnp.sum(dy * xv * r, axis=0, keepdims=True)

    row = pl.BlockSpec((tm, d), lambda i: (i, 0))
    vec = pl.BlockSpec((1, d), lambda i: (0, 0))
    return pl.pallas_call(
        body, name="loss_head", grid=(s // tm,),
        in_specs=[row, vec, row], out_specs=[row, pl.BlockSpec((8, LANES), lambda i: (0, 0)), vec],
        out_shape=[jax.ShapeDtypeStruct((s, d), F32), jax.ShapeDtypeStruct((8, LANES), F32),
                   jax.ShapeDtypeStruct((1, d), F32)],
        compiler_params=_params(1),
    )(x, g, target)


def _position():
    x, y, c = lax.axis_index("x"), lax.axis_index("y"), lax.axis_index("c")
    return x, y, c, [(1 - x, y), (x, 1 - y), (1 - x, 1 - y)]


def _row_tile(rows, cap=256):
    best = rows
    for cand in range(8, min(rows, cap) + 1, 8):
        if rows % cand == 0:
            best = cand
    return best


def _cast_into_slots(shards, dtypes, me_idx):
    n = len(shards)
    tiles = [_row_tile(s.shape[0]) for s in shards]
    counts = [s.shape[0] // t for s, t in zip(shards, tiles)]
    starts = [sum(counts[:a]) for a in range(n)]

    def body(me_ref, *refs):
        i = pl.program_id(0)
        for a in range(n):
            @pl.when((i >= starts[a]) & (i < starts[a] + counts[a]))
            def _(a=a):
                refs[n + a][0] = refs[a][...].astype(dtypes[a])

    tile_of = [lambda i, a=a: jnp.clip(i - starts[a], 0, counts[a] - 1) for a in range(n)]
    return pl.pallas_call(
        body, name="cast_into_slots",
        grid_spec=pltpu.PrefetchScalarGridSpec(
            num_scalar_prefetch=1, grid=(sum(counts),),
            in_specs=[pl.BlockSpec((tiles[a], shards[a].shape[1]), lambda i, me, a=a: (tile_of[a](i), 0)) for a in range(n)],
            out_specs=[pl.BlockSpec((1, tiles[a], shards[a].shape[1]), lambda i, me, a=a: (me[0], tile_of[a](i), 0))
                       for a in range(n)]),
        out_shape=[jax.ShapeDtypeStruct((N_SHARD,) + s.shape, dt) for s, dt in zip(shards, dtypes)],
        compiler_params=_params(1),
    )(me_idx, *shards)


HBM_SPEC = pl.BlockSpec(memory_space=pltpu.HBM)
SEM_SPEC = pl.BlockSpec(memory_space=pltpu.SEMAPHORE)
DATAFLOW = pltpu.SideEffectType.DATAFLOW_SIDE_EFFECTING


def _exchange_copies(srcs, lands, send, recv, gather):
    x, y, c, chips = _position()
    me = 2 * x + y
    out = []
    for a in range(len(lands)):
        dst = lands[a].at[me]
        if gather and _halved(lands[a]):
            half = lands[a].shape[1] // 2
            dst = lands[a].at[me, pl.ds(c * half, half), :]
        for k, (px, py) in enumerate(chips):
            src = dst if gather else srcs[a].at[2 * px + py]
            out.append(pltpu.make_async_remote_copy(src_ref=src, dst_ref=dst, send_sem=send.at[3 * a + k],
                                                    recv_sem=recv.at[3 * a + k], device_id=(px, py, c), device_id_type=MESH))
    return out


def _halved(land):
    return land.shape[1] % 32 == 0


def _pair_fill(name, lands):
    n = len(lands)

    def body(*refs):
        src, dst = refs[:n], refs[n:2 * n]
        send, recv = refs[2 * n:]
        x, y, c, chips = _position()
        copies = []
        for a in range(n):
            half = src[a].shape[1] // 2
            for k, (px, py) in enumerate(chips):
                rows = (2 * px + py, pl.ds(c * half, half), slice(None))
                cp = pltpu.make_async_remote_copy(src_ref=src[a].at[rows], dst_ref=dst[a].at[rows], send_sem=send.at[a, k],
                                                  recv_sem=recv.at[a, k], device_id=(x, y, 1 - c), device_id_type=MESH)
                cp.start()
                copies.append(cp)
        for cp in copies:
            cp.wait()

    return pl.pallas_call(
        body, name=name, in_specs=[ANY] * n, out_specs=[ANY] * n,
        out_shape=[jax.ShapeDtypeStruct(l.shape, l.dtype) for l in lands],
        input_output_aliases={a: a for a in range(n)},
        scratch_shapes=[pltpu.SemaphoreType.DMA((n, 3)), pltpu.SemaphoreType.DMA((n, 3))],
    )(*lands)


def _exchange_start(name, srcs, lands, after):
    ns, nl, na = len(srcs), len(lands), len(after)
    gather = ns == 0

    def body(*refs):
        src_refs, land_refs = refs[:ns], refs[ns:ns + nl]
        send, recv = refs[ns + nl + na], refs[ns + nl + na + 1]
        token = refs[-1]
        for cp in _exchange_copies(src_refs, land_refs, send, recv, gather):
            cp.start()
        token[...] = jnp.zeros_like(token)

    arrays = [pltpu.with_memory_space_constraint(a, pltpu.HBM) for a in list(srcs) + list(lands)]
    outs = pl.pallas_call(
        body, name=name,
        out_shape=(pltpu.SemaphoreType.DMA((3 * nl,)), pltpu.SemaphoreType.DMA((3 * nl,)),
                   *[pltpu.HBM(a.shape, a.dtype) for a in arrays], jax.ShapeDtypeStruct((8, LANES), F32)),
        in_specs=[HBM_SPEC] * (ns + nl) + [ANY] * na,
        out_specs=(SEM_SPEC, SEM_SPEC, *[HBM_SPEC] * (ns + nl), pl.BlockSpec(memory_space=pltpu.VMEM)),
        input_output_aliases={i: 2 + i for i in range(ns + nl)},
        compiler_params=pltpu.CompilerParams(has_side_effects=DATAFLOW),
    )(*arrays, *after)
    return outs[0], outs[1], list(outs[2:2 + ns]), list(outs[2 + ns:2 + ns + nl]), outs[-1]


def _exchange_wait(name, send, recv, srcs, lands, after):
    ns, nl, na = len(srcs), len(lands), len(after)
    gather = ns == 0

    def body(*refs):
        src_refs, land_refs = refs[:ns], refs[ns:ns + nl]
        send_ref, recv_ref = refs[ns + nl], refs[ns + nl + 1]
        for cp in _exchange_copies(src_refs, land_refs, send_ref, recv_ref, gather):
            cp.wait_send()
            cp.wait_recv()

    outs = pl.pallas_call(
        body, name=name,
        out_shape=tuple(pltpu.HBM(a.shape, a.dtype) for a in list(srcs) + list(lands)),
        in_specs=[HBM_SPEC] * (ns + nl) + [SEM_SPEC, SEM_SPEC] + [ANY] * na,
        out_specs=tuple([HBM_SPEC] * (ns + nl)),
        input_output_aliases={i: i for i in range(ns + nl)},
        compiler_params=pltpu.CompilerParams(has_side_effects=DATAFLOW),
    )(*srcs, *lands, send, recv, *after)
    return list(outs[ns:])


def _pair_exchange(grads):
    n = len(grads)

    def body(*refs):
        src, dst = refs[:n], refs[n:2 * n]
        send, recv = refs[2 * n:]
        x, y, c, _ = _position()
        copies = []
        for a in range(n):
            half = src[a].shape[1] // 2
            cp = pltpu.make_async_remote_copy(
                src_ref=src[a].at[:, pl.ds((1 - c) * half, half), :], dst_ref=dst[a], send_sem=send.at[a],
                recv_sem=recv.at[a], device_id=(x, y, 1 - c), device_id_type=MESH)
            cp.start()
            copies.append(cp)
        for cp in copies:
            cp.wait()

    return pl.pallas_call(
        body, name="grad_pair_exchange", in_specs=[ANY] * n, out_specs=[ANY] * n,
        out_shape=[jax.ShapeDtypeStruct((g.shape[0], g.shape[1] // 2, g.shape[2]), g.dtype) for g in grads],
        scratch_shapes=[pltpu.SemaphoreType.DMA((n,)), pltpu.SemaphoreType.DMA((n,))],
    )(*grads)


def _shard_of(a):
    return lambda i: jnp.clip(i - a * N_SHARD, 0, N_SHARD - 1)


def _pair_sum(gs, gots, c_idx, me_idx):
    n = len(gs)
    halves = [(g.shape[1] // 2, g.shape[2]) for g in gs]

    def body(c_ref, me_ref, *refs):
        g_refs, got_refs, s_refs, own_refs = (refs[k * n:(k + 1) * n] for k in range(4))
        i = pl.program_id(0)
        for a in range(n):
            @pl.when(i // N_SHARD == a)
            def _(a=a):
                sm = g_refs[a][...] + got_refs[a][...].astype(F32)
                s_refs[a][...] = sm.astype(CD)

                @pl.when(i % N_SHARD == me_ref[0])
                def _():
                    own_refs[a][...] = sm[0]

    shard = [_shard_of(a) for a in range(n)]
    return pl.pallas_call(
        body, name="grad_pair_sum",
        grid_spec=pltpu.PrefetchScalarGridSpec(
            num_scalar_prefetch=2, grid=(n * N_SHARD,),
            in_specs=[pl.BlockSpec((1, h, c_), lambda i, c, me, a=a: (shard[a](i), c[0], 0)) for a, (h, c_) in enumerate(halves)]
            + [pl.BlockSpec((1, h, c_), lambda i, c, me, a=a: (shard[a](i), 0, 0)) for a, (h, c_) in enumerate(halves)],
            out_specs=[pl.BlockSpec((1, h, c_), lambda i, c, me, a=a: (shard[a](i), 0, 0)) for a, (h, c_) in enumerate(halves)]
            + [pl.BlockSpec((h, c_), lambda i, c, me: (0, 0)) for h, c_ in halves]),
        out_shape=[jax.ShapeDtypeStruct((N_SHARD, h, c_), CD) for h, c_ in halves]
        + [jax.ShapeDtypeStruct((h, c_), F32) for h, c_ in halves],
        compiler_params=_params(1),
    )(c_idx, me_idx, *gs, *gots)


def _chip_sum(owns, gots, me_idx):
    n = len(owns)
    tiles = [_row_tile(o.shape[0]) for o in owns]
    counts = [o.shape[0] // t for o, t in zip(owns, tiles)]
    starts = [sum(counts[:a]) for a in range(n)]

    def body(me_ref, *refs):
        own_refs, got_refs, out_refs = (refs[k * n:(k + 1) * n] for k in range(3))
        i = pl.program_id(0)
        for a in range(n):
            @pl.when((i >= starts[a]) & (i < starts[a] + counts[a]))
            def _(a=a):
                total = None
                for j in range(N_SHARD):
                    term = jnp.where(j == me_ref[0], own_refs[a][...], got_refs[a][j].astype(F32))
                    total = term if total is None else total + term
                out_refs[a][...] = total

    tile_of = [lambda i, a=a: jnp.clip(i - starts[a], 0, counts[a] - 1) for a in range(n)]
    own_specs = [pl.BlockSpec((tiles[a], owns[a].shape[1]), lambda i, me, a=a: (tile_of[a](i), 0)) for a in range(n)]
    return pl.pallas_call(
        body, name="grad_chip_sum",
        grid_spec=pltpu.PrefetchScalarGridSpec(
            num_scalar_prefetch=1, grid=(sum(counts),),
            in_specs=own_specs + [pl.BlockSpec((N_SHARD, tiles[a], owns[a].shape[1]), lambda i, me, a=a: (0, tile_of[a](i), 0))
                                  for a in range(n)],
            out_specs=own_specs),
        out_shape=[jax.ShapeDtypeStruct(o.shape, F32) for o in owns],
        compiler_params=_params(1),
    )(me_idx, *owns, *gots)


def _pair_share(halves):
    n = len(halves)

    def body(*refs):
        src, dst = refs[:n], refs[n:2 * n]
        send, recv = refs[2 * n:]
        x, y, c, _ = _position()
        copies = []
        for a in range(n):
            cp = pltpu.make_async_remote_copy(src_ref=src[a], dst_ref=dst[a], send_sem=send.at[a],
                                              recv_sem=recv.at[a], device_id=(x, y, 1 - c), device_id_type=MESH)
            cp.start()
            copies.append(cp)
        for cp in copies:
            cp.wait()

    return pl.pallas_call(
        body, name="grad_pair_share", in_specs=[ANY] * n, out_specs=[ANY] * n,
        out_shape=[jax.ShapeDtypeStruct(h.shape, h.dtype) for h in halves],
        scratch_shapes=[pltpu.SemaphoreType.DMA((n,)), pltpu.SemaphoreType.DMA((n,))],
    )(*halves)


def _small_allreduce(buf):
    rows, cols = buf.shape

    def body(src_ref, out_ref, slots, send, recv):
        x, y, c, _ = _position()
        me = 4 * x + 2 * y + c
        slots[me] = src_ref[...]
        copies = []
        k = 0
        for dx in (0, 1):
            for dy in (0, 1):
                for dc in (0, 1):
                    if (dx, dy, dc) == (0, 0, 0):
                        continue
                    peer = (jnp.where(dx, 1 - x, x), jnp.where(dy, 1 - y, y), jnp.where(dc, 1 - c, c))
                    cp = pltpu.make_async_remote_copy(src_ref=src_ref, dst_ref=slots.at[me], send_sem=send.at[k],
                                                      recv_sem=recv.at[k], device_id=peer, device_id_type=MESH)
                    cp.start()
                    copies.append(cp)
                    k += 1
        for cp in copies:
            cp.wait()
        total = slots[0]
        for dev in range(1, N_DEV):
            total = total + slots[dev]
        out_ref[...] = total

    vm = pl.BlockSpec(memory_space=pltpu.VMEM)
    return pl.pallas_call(
        body, name="small_allreduce", in_specs=[vm], out_specs=vm,
        out_shape=jax.ShapeDtypeStruct((rows, cols), F32),
        scratch_shapes=[pltpu.VMEM((N_DEV, rows, cols), F32), pltpu.SemaphoreType.DMA((N_DEV - 1,)),
                        pltpu.SemaphoreType.DMA((N_DEV - 1,))],
    )(buf)


def _adamw_math(w, gv, m, v):
    mn = ADAM_B1 * m + (1.0 - ADAM_B1) * gv
    vn = ADAM_B2 * v + (1.0 - ADAM_B2) * (gv * gv)
    m_hat = mn / (1.0 - ADAM_B1 ** ADAM_STEP)
    v_hat = vn / (1.0 - ADAM_B2 ** ADAM_STEP)
    return -ADAM_LR * (m_hat / (jnp.sqrt(v_hat) + ADAM_EPS) + ADAM_WD * w), mn, vn


def _adamw(w, g, m, v):
    rows, cols = w.shape
    tr = _row_tile(rows)

    def body(w_ref, g_ref, m_ref, v_ref, d_ref, mo_ref, vo_ref):
        d_ref[...], mo_ref[...], vo_ref[...] = _adamw_math(w_ref[...], g_ref[...], m_ref[...], v_ref[...])

    blk = pl.BlockSpec((tr, cols), lambda i: (i, 0))
    shp = jax.ShapeDtypeStruct((rows, cols), F32)
    return pl.pallas_call(
        body, name="adamw", grid=(rows // tr,), in_specs=[blk] * 4, out_specs=[blk] * 3, out_shape=[shp] * 3,
        compiler_params=_params(1),
    )(w, g, m, v)


def _adamw_halves(w, own, got, m, v, c_idx):
    rows, cols = w.shape
    tr = _row_tile(rows // 2)
    per_half = rows // 2 // tr

    def body(c_ref, w_ref, own_ref, got_ref, m_ref, v_ref, d_ref, mo_ref, vo_ref, g_ref):
        mine = (pl.program_id(0) // per_half) == c_ref[0]
        gv = jnp.where(mine, own_ref[...], got_ref[...])
        g_ref[...] = gv
        d_ref[...], mo_ref[...], vo_ref[...] = _adamw_math(w_ref[...], gv, m_ref[...], v_ref[...])

    blk = pl.BlockSpec((tr, cols), lambda i, c: (i, 0))
    own_blk = pl.BlockSpec((tr, cols), lambda i, c: (jnp.where(i // per_half == c[0], i % per_half, 0), 0))
    got_blk = pl.BlockSpec((tr, cols), lambda i, c: (jnp.where(i // per_half == c[0], 0, i % per_half), 0))
    shp = jax.ShapeDtypeStruct((rows, cols), F32)
    return pl.pallas_call(
        body, name="adamw_halves",
        grid_spec=pltpu.PrefetchScalarGridSpec(num_scalar_prefetch=1, grid=(rows // tr,),
                                               in_specs=[blk, own_blk, got_blk, blk, blk], out_specs=[blk] * 4),
        out_shape=[shp] * 4, compiler_params=_params(1),
    )(c_idx, w, own, got, m, v)


def _local_step(x, target, norm_gains, q_g, k_g, ng, weights_of, grads_done):
    s = x.shape[0]
    tm = min(512, s)
    tq = min(256, s)
    tf = min(1024, s)
    g1, gm, g2, gf = norm_gains
    cos2, sin2 = _rope_tables(s)
    gq8 = jnp.tile(q_g, (1, 8))
    gk2 = jnp.tile(k_g, (1, 2))

    tn = min(256, s)
    tk = min(1024, s)
    w1 = weights_of(1, ())
    x1, s1, t1, b1, h1 = _ffn_fwd(x, g1, w1["g1"], w1["u1"], w1["d1"], tf)
    w2 = weights_of(2, (x1,))
    lbl = w2["lbl"]
    pqkv, ph, pg, hm = _mix_in_fwd(x1, gm, w2["in"], tn)
    qe, kr, vr, vs = _qk_prep(pqkv, gq8, gk2, cos2, sin2, tm)
    oa, lse = _attn_fwd(qe, kr, vr, vs, tq)
    ob, pre, hstates = _hgrn_fwd(ph, lbl, ng)
    x2 = _mix_out_fwd(x1, oa, ob, pg, w2["a"], w2["b"], w2["o"], tm)
    w3 = weights_of(3, (x2,))
    x3, s2, t2, b2, h2 = _ffn_fwd(x2, g2, w3["g2"], w3["u2"], w3["d2"], tf)
    dx3, loss, dgf = _loss_head(x3, gf, target, tm)

    dx2, da2, db2, f2, dg2, dx3c = _ffn_bwd(dx3, x2, g2, s2, t2, b2, w3["g2"], w3["u2"], w3["d2"], tf)
    tok = grads_done(3, dict(g2=_dw_shared_b("dw_gate", da2, h2, tk, 1.0), u2=_dw_shared_b("dw_gate", db2, h2, tk, 1.0),
                             d2=_dw_shared_b("dw_down", f2, dx3c, tk, 1.0)))

    dpg, mg, dya, dyb, doe, delta, dob = _mix_out_bwd(dx2, oa, ob, pg, w2["a"], w2["b"], w2["o"], tm, tok)
    g_o = [g.reshape(N_SHARD, D_MODEL // N_SHARD, D_MODEL) for g in _dw_colblocks("dw_out", mg, dx2, 1, tk)]
    g_a = _dw_colblocks("dw_branch", oa, dya, N_SHARD, tk)
    g_b = _dw_colblocks("dw_branch", ob, dyb, N_SHARD, tk)
    dqt, dkt, dvt = _attn_bwd(qe, kr, kr.T, vr, doe, delta, lse, tq)
    dqkv, dgq, dgk = _qk_prep_bwd(pqkv, dqt.T, dkt.T, dvt.T, gq8, gk2, cos2, sin2, tm)
    dhq, dhff, dhfb, dhi, dhg, dlb, dng = _hgrn_bwd(ph, pre, dob, hstates, lbl, ng)
    dps = (dqkv, dhq, dhff, dhfb, dhi, dhg, dpg)
    g_in = [g.reshape(N_SHARD, -1, D_MODEL) for g in _dw_in(dps, hm, tk)]
    tok = grads_done(2, {"in": g_in, "a": g_a, "b": g_b, "o": g_o})
    dx1, dgm = _mix_in_bwd(dps, w2["in"], x1, dx2, gm, tn, tok)

    dx0, da1, db1, f1, dg1, dx1c = _ffn_bwd(dx1, x, g1, s1, t1, b1, w1["g1"], w1["u1"], w1["d1"], tf)
    grads_done(1, dict(g1=_dw_shared_b("dw_gate", da1, h1, tk, 1.0), u1=_dw_shared_b("dw_gate", db1, h1, tk, 1.0),
                       d1=_dw_shared_b("dw_down", f1, dx1c, tk, 1.0)))
    small = dict(g1=dg1, gm=dgm, g2=dg2, gf=dgf, gq=dgq, gk=dgk, lb=dlb, ng=dng)
    return loss, dx0, small, lbl


GROUPS = {1: ("g1", "u1", "d1"), 2: ("in", "a", "b", "o"), 3: ("g2", "u2", "d2")}
BIG = GROUPS[1] + GROUPS[2] + GROUPS[3]
TRANSPOSED = ("g1", "u1", "in", "g2", "u2")


def _pack_rows(vectors, width):
    rows = []
    for vct in vectors:
        flat = vct.reshape(-1)
        pad = (-flat.shape[0]) % width
        rows.append(jnp.pad(flat, (0, pad)).reshape(-1, width))
    return jnp.concatenate(rows, axis=0)


def kernel(x, ffn1_norm_g, ffn1_w_gate, ffn1_w_up, ffn1_w_down, mix_norm_g, w_in, q_norm_g, k_norm_g, hgrn_lb_logits, hgrn_out_norm_g, w_branch_attn, w_branch_hgrn, w_out, ffn2_norm_g, ffn2_w_gate, ffn2_w_up, ffn2_w_down, final_norm_g, loss_target, m_ffn1_norm_g, m_ffn1_w_gate, m_ffn1_w_up, m_ffn1_w_down, m_mix_norm_g, m_w_in, m_q_norm_g, m_k_norm_g, m_hgrn_lb_logits, m_hgrn_out_norm_g, m_w_branch_attn, m_w_branch_hgrn, m_w_out, m_ffn2_norm_g, m_ffn2_w_gate, m_ffn2_w_up, m_ffn2_w_down, m_final_norm_g, v_ffn1_norm_g, v_ffn1_w_gate, v_ffn1_w_up, v_ffn1_w_down, v_mix_norm_g, v_w_in, v_q_norm_g, v_k_norm_g, v_hgrn_lb_logits, v_hgrn_out_norm_g, v_w_branch_attn, v_w_branch_hgrn, v_w_out, v_ffn2_norm_g, v_ffn2_w_gate, v_ffn2_w_up, v_ffn2_w_down, v_final_norm_g):
    xi, yi, ci = lax.axis_index("x"), lax.axis_index("y"), lax.axis_index("c")
    me = 2 * xi + yi
    c_idx = jnp.reshape(ci, (1,)).astype(jnp.int32)
    me_idx = jnp.reshape(me, (1,)).astype(jnp.int32)

    big_w = dict(g1=ffn1_w_gate[0], u1=ffn1_w_up[0], d1=ffn1_w_down[0], a=w_branch_attn[0], b=w_branch_hgrn[0],
                 o=w_out[0], g2=ffn2_w_gate[0], u2=ffn2_w_up[0], d2=ffn2_w_down[0])
    big_w["in"] = w_in[0]
    big_m = dict(g1=m_ffn1_w_gate[0], u1=m_ffn1_w_up[0], d1=m_ffn1_w_down[0], a=m_w_branch_attn[0], b=m_w_branch_hgrn[0],
                 o=m_w_out[0], g2=m_ffn2_w_gate[0], u2=m_ffn2_w_up[0], d2=m_ffn2_w_down[0])
    big_m["in"] = m_w_in[0]
    big_v = dict(g1=v_ffn1_w_gate[0], u1=v_ffn1_w_up[0], d1=v_ffn1_w_down[0], a=v_w_branch_attn[0], b=v_w_branch_hgrn[0],
                 o=v_w_out[0], g2=v_ffn2_w_gate[0], u2=v_ffn2_w_up[0], d2=v_ffn2_w_down[0])
    big_v["in"] = v_w_in[0]
    for table in (big_w, big_m, big_v):
        for n in TRANSPOSED:
            table[n] = table[n].T

    started, token = {}, ()
    for grp in (1, 2, 3):
        shards = [big_w[n] for n in GROUPS[grp]] + ([hgrn_lb_logits.reshape(4, LANES)] if grp == 2 else [])
        dtypes = [CD] * len(GROUPS[grp]) + ([F32] if grp == 2 else [])
        lands = _cast_into_slots(shards, dtypes, me_idx)
        send, recv, _, lands, tok = _exchange_start("gather%d_start" % grp, [], lands, token)
        started[grp], token = (send, recv, lands), (tok,)

    def weights_of(grp, after):
        send, recv, lands = started[grp]
        got = _exchange_wait("gather%d_wait" % grp, send, recv, [], lands, tuple(after) + (token if grp == 1 else ()))
        by_halves = [i for i, land in enumerate(got) if _halved(land)]
        for i, whole in zip(by_halves, _pair_fill("gather%d_fill" % grp, [got[i] for i in by_halves])):
            got[i] = whole
        w = dict(zip(GROUPS[grp], got))
        if grp == 2:
            w["in"] = w["in"].reshape(-1, D_MODEL)
            w["o"] = w["o"].reshape(D_MODEL, D_MODEL)
            w["lbl"] = jnp.transpose(got[-1], (1, 0, 2)).reshape(4, N_SHARD * LANES)
        return w

    pending = {}

    def grads_done(grp, grads):
        names = list(grads)
        got = _pair_exchange([grads[n][1] for n in names])
        res = _pair_sum([grads[n][0] for n in names], got, c_idx, me_idx)
        sums, owns = res[:len(names)], res[len(names):]
        lands = [lax.empty(s_.shape, s_.dtype) for s_ in sums]
        send, recv, srcs, lands, tok = _exchange_start("reduce%d_start" % grp, list(sums), lands, ())
        pending[grp] = (names, send, recv, srcs, lands, owns, tok)
        return (tok,)

    def reduced_halves(grp, after):
        names, send, recv, srcs, lands, owns, _ = pending[grp]
        parts = _exchange_wait("reduce%d_wait" % grp, send, recv, srcs, lands, after)
        return names, list(_chip_sum(list(owns), parts, me_idx))

    loss, dx, small, lbl = _local_step(
        x[0], loss_target[0], (ffn1_norm_g, mix_norm_g, ffn2_norm_g, final_norm_g.reshape(1, -1)),
        q_norm_g, k_norm_g, hgrn_out_norm_g, weights_of, grads_done)

    dgq = small["gq"].reshape(8, HEAD_DIM).sum(axis=0)
    dgk = small["gk"].reshape(2, HEAD_DIM).sum(axis=0)
    lb_full = _hgrn_lower_bounds(lbl)
    dlog = []
    for d in (0, 1):
        t = small["lb"][d:d + 1] * lb_full[d] * (1.0 - lb_full[d])
        dlog += [t, -t]
    small_list = [small["g1"], small["gm"], small["g2"], small["gf"], small["ng"], dgq, dgk, jnp.concatenate(dlog, axis=0), loss[0, 0]]
    packed = _pack_rows(small_list, D_MODEL)
    n_rows = packed.shape[0]
    packed = jnp.pad(packed, ((0, (-n_rows) % 8), (0, 0)))
    red = _small_allreduce(packed)
    loss_out = red[n_rows - 1, 0]
    sg = dict(g1=red[0:1], gm=red[1:2], g2=red[2:3], gf=red[3], ng=red[4:5, :512], gq=red[5:6, :HEAD_DIM],
              gk=red[6:7, :HEAD_DIM])
    dlog_full = red[7:9].reshape(2, 2, 512)
    sg["lb"] = lax.dynamic_slice_in_dim(dlog_full, me * LANES, LANES, axis=2)

    small_w = dict(g1=ffn1_norm_g, gm=mix_norm_g, g2=ffn2_norm_g, gf=final_norm_g, ng=hgrn_out_norm_g, gq=q_norm_g,
                   gk=k_norm_g, lb=hgrn_lb_logits)
    small_m = dict(g1=m_ffn1_norm_g, gm=m_mix_norm_g, g2=m_ffn2_norm_g, gf=m_final_norm_g, ng=m_hgrn_out_norm_g,
                   gq=m_q_norm_g, gk=m_k_norm_g, lb=m_hgrn_lb_logits)
    small_v = dict(g1=v_ffn1_norm_g, gm=v_mix_norm_g, g2=v_ffn2_norm_g, gf=v_final_norm_g, ng=v_hgrn_out_norm_g,
                   gq=v_q_norm_g, gk=v_k_norm_g, lb=v_hgrn_lb_logits)
    small_names = ("g1", "gm", "g2", "gf", "ng", "gq", "gk", "lb")
    pack = lambda dct: _pack_rows([dct[n] for n in small_names], D_MODEL)
    pw, pgr, pm, pv = pack(small_w), pack(sg), pack(small_m), pack(small_v)
    pad8 = lambda a: jnp.pad(a, ((0, (-a.shape[0]) % 8), (0, 0)))
    sd, sm_, sv_ = _adamw(pad8(pw), pad8(pgr), pad8(pm), pad8(pv))

    def unpack(buf):
        out, r = {}, 0
        for n in small_names:
            size = small_w[n].size
            nr = -(-size // D_MODEL)
            out[n] = buf[r:r + nr].reshape(-1)[:size].reshape(small_w[n].shape)
            r += nr
        return out

    sdelta, snew_m, snew_v = unpack(sd), unpack(sm_), unpack(sv_)
    sgrad = {n: sg[n].reshape(small_w[n].shape) for n in small_names}

    bdelta, bnew_m, bnew_v, bgrad = {}, {}, {}, {}

    def update(names, halves):
        for n, own, got in zip(names, halves, _pair_share(halves)):
            res = _adamw_halves(big_w[n], own, got, big_m[n], big_v[n], c_idx)
            if n in TRANSPOSED:
                res = [r.T for r in res]
            bdelta[n], bnew_m[n], bnew_v[n], bgrad[n] = [r[None] for r in res]

    names3, halves3 = reduced_halves(3, (pending[1][-1],))
    names2, halves2 = reduced_halves(2, (halves3[0],))
    update(names3 + names2, halves3 + halves2)
    names1, halves1 = reduced_halves(1, (bdelta[names2[-1]],))
    update(names1, halves1)

    order = [("s", "g1"), ("b", "g1"), ("b", "u1"), ("b", "d1"), ("s", "gm"), ("b", "in"), ("s", "gq"), ("s", "gk"),
             ("s", "lb"), ("s", "ng"), ("b", "a"), ("b", "b"), ("b", "o"), ("s", "g2"), ("b", "g2"), ("b", "u2"),
             ("b", "d2"), ("s", "gf")]
    outs = [loss_out, dx[None]]
    for table_s, table_b in ((sgrad, bgrad), (sdelta, bdelta), (snew_m, bnew_m), (snew_v, bnew_v)):
        outs += [(table_s if kind == "s" else table_b)[n] for kind, n in order]
    return tuple(outs)
```

```python
import functools

import numpy as np
import jax
import jax.numpy as jnp
from jax import lax
from jax.experimental import pallas as pl
from jax.experimental.pallas import tpu as pltpu

F32 = jnp.float32
BF16 = jnp.bfloat16
CD = jnp.bfloat16

EPS = 1e-6
D_MODEL = 1024
HEAD_DIM = 64
GRID_W = 64
ROPE_THETA = 10000.0
CHUNK = 32
N_SHARD = 4
N_DEV = 8
VMEM_LIMIT = 56 * 1024 * 1024
HGRN_BWD_VMEM = 60 * 1024 * 1024
LANES = 128
HG_TILE = 128
ATTN_BWD_HEADS = 2
FFN_ROWS = 256

ADAM_LR = 0.001
ADAM_B1 = 0.9
ADAM_B2 = 0.999
ADAM_EPS = 1e-08
ADAM_WD = 0.01
ADAM_STEP = 10

NN = (((1,), (0,)), ((), ()))
NT = (((1,), (1,)), ((), ()))
TN = (((0,), (0,)), ((), ()))
MESH = pl.DeviceIdType.MESH
ANY = pl.BlockSpec(memory_space=pl.ANY)


def _mm(a, b, dn):
    return lax.dot_general(a.astype(CD), b.astype(CD), dn, preferred_element_type=F32)


def _split3(x):
    hi = x.astype(BF16)
    r = x - hi.astype(F32)
    mid = r.astype(BF16)
    lo = (r - mid.astype(F32)).astype(BF16)
    return hi, mid, lo


def _xdot(x, m):
    rows = x.shape[0]
    hi, mid, _ = _split3(x)
    r = lax.dot_general(jnp.concatenate([hi, mid], axis=0), m, NN, preferred_element_type=F32)
    return r[:rows] + r[rows:]


def _xdot_l(m, x):
    cols = x.shape[1]
    hi, mid, _ = _split3(x)
    r = lax.dot_general(m, jnp.concatenate([hi, mid], axis=1), NN, preferred_element_type=F32)
    return r[:, :cols] + r[:, cols:]


def _params(n_grid, vmem_limit=VMEM_LIMIT):
    return pltpu.CompilerParams(dimension_semantics=("arbitrary",) * n_grid, vmem_limit_bytes=vmem_limit)


def _sigmoid(x):
    return jax.nn.sigmoid(x)


def _np_blocksum(n):
    i = np.arange(n)
    return (i[:, None] // HEAD_DIM == i[None, :] // HEAD_DIM).astype(np.float32)


def _np_swap32(n):
    i = np.arange(n)
    partner = np.where(i % HEAD_DIM < HEAD_DIM // 2, i + HEAD_DIM // 2, i - HEAD_DIM // 2)
    m = np.zeros((n, n), np.float32)
    m[i, partner] = 1.0
    return m


def _np_expand_q():
    m = np.zeros((512, 1024), np.float32)
    for h in range(8):
        g = h // 4
        for d in range(HEAD_DIM):
            m[64 * h + d, 128 * h + 64 * g + d] = 1.0
    return m


def _np_headsum_spread():
    m = np.zeros((512, 1024), np.float32)
    for h in range(8):
        m[64 * h:64 * h + 64, 128 * h:128 * h + 128] = 1.0
    return m


def _np_swap_halves():
    m = np.zeros((128, 128), np.float32)
    i = np.arange(128)
    m[i, (i + 64) % 128] = 1.0
    return m


def _np_hgrn_cums(t, rev):
    r = np.arange(t)[:, None]
    c = np.arange(t)[None, :]
    same = (r // CHUNK) == (c // CHUNK)
    if not rev:
        cum = same & (c <= r)
        mid = same & (c % CHUNK <= CHUNK // 2 - 1)
    else:
        cum = same & (c >= r)
        mid = same & (c % CHUNK >= CHUNK // 2)
    return np.concatenate([cum, mid, same], axis=0).astype(np.float32)


def _bf(a):
    return jnp.asarray(a, dtype=BF16)


def _rope_tables(seq_len):
    rows = seq_len // GRID_W
    row = jnp.repeat(jnp.arange(rows, dtype=F32), GRID_W)
    col = jnp.tile(jnp.arange(GRID_W, dtype=F32), rows)
    n_freq = HEAD_DIM // 4
    inv = ROPE_THETA ** (-jnp.arange(n_freq, dtype=F32) / n_freq)
    ang = jnp.concatenate([row[:, None] * inv, col[:, None] * inv], axis=-1)
    cos, sin = jnp.cos(ang), jnp.sin(ang)
    c64 = jnp.concatenate([cos, cos], axis=-1)
    s64 = jnp.concatenate([-sin, sin], axis=-1)
    return jnp.tile(c64, (1, 2)), jnp.tile(s64, (1, 2))


def _ffn_fwd(x, g, wg, wu, wd, tm):
    s, d = x.shape
    nsh, fs, _ = wg.shape

    def body(x_ref, g_ref, wg_ref, wu_ref, wd_ref, xo_ref, a_ref, da_ref, b_ref, hb_ref, acc, hs):
        j = pl.program_id(1)

        @pl.when(j == 0)
        def _():
            xv = x_ref[...]
            r = lax.rsqrt(jnp.mean(xv * xv, axis=-1, keepdims=True) + EPS)
            h = (xv * r * g_ref[...]).astype(CD)
            hs[...] = h
            hb_ref[...] = h
            acc[...] = jnp.zeros_like(acc)

        blocks = [slice(r0, min(r0 + FFN_ROWS, tm)) for r0 in range(0, tm, FFN_ROWS)]
        firsts = [(_mm(hs[rows, :], wg_ref[0], NT), _mm(hs[rows, :], wu_ref[0], NT)) for rows in blocks]
        for rows, (a, b) in zip(blocks, firsts):
            sg = _sigmoid(a)
            silu = a * sg
            acc[rows, :] += _mm(silu * b, wd_ref[0], NN)
            a_ref[0, rows, :] = silu.astype(CD)
            da_ref[0, rows, :] = (sg * (1.0 + a * (1.0 - sg))).astype(CD)
            b_ref[0, rows, :] = b.astype(CD)

        @pl.when(j == nsh - 1)
        def _():
            xo_ref[...] = x_ref[...] + 0.5 * acc[...]

    return pl.pallas_call(
        body, name="ffn_fwd", grid=(s // tm, nsh),
        in_specs=[pl.BlockSpec((tm, d), lambda i, j: (i, 0)), pl.BlockSpec((1, d), lambda i, j: (0, 0))]
        + [pl.BlockSpec((1, fs, d), lambda i, j: (j, 0, 0))] * 3,
        out_specs=[pl.BlockSpec((tm, d), lambda i, j: (i, 0))] + [pl.BlockSpec((1, tm, fs), lambda i, j: (j, i, 0))] * 3
        + [pl.BlockSpec((tm, d), lambda i, j: (i, 0))],
        out_shape=[jax.ShapeDtypeStruct((s, d), F32)] + [jax.ShapeDtypeStruct((nsh, s, fs), CD)] * 3
        + [jax.ShapeDtypeStruct((s, d), CD)],
        scratch_shapes=[pltpu.VMEM((tm, d), F32), pltpu.VMEM((tm, d), CD)],
        compiler_params=_params(2),
    )(x, g, wg, wu, wd)


def _ffn_bwd(dout, x, g, silu, dsilu, b, wg, wu, wd, tm):
    s, d = x.shape
    nsh, fs, _ = wg.shape

    def body(do_ref, x_ref, g_ref, sl_ref, ds_ref, b_ref, wg_ref, wu_ref, wd_ref,
             dx_ref, da_ref, db_ref, f_ref, dg_ref, do16_ref, dh):
        i = pl.program_id(0)
        j = pl.program_id(1)

        @pl.when(j == 0)
        def _():
            dh[...] = jnp.zeros_like(dh)
            do16_ref[...] = (0.5 * do_ref[...]).astype(CD)

        @pl.when((i == 0) & (j == 0))
        def _():
            dg_ref[...] = jnp.zeros_like(dg_ref)

        blocks = [slice(r0, min(r0 + FFN_ROWS, tm)) for r0 in range(0, tm, FFN_ROWS)]
        dfs = [_mm(do16_ref[rows, :], wd_ref[0], NT) for rows in blocks]
        das, dbs = [], []
        for rows, df in zip(blocks, dfs):
            sl = sl_ref[0, rows, :].astype(F32)
            bv = b_ref[0, rows, :].astype(F32)
            da = (df * bv * ds_ref[0, rows, :].astype(F32)).astype(CD)
            db = (df * sl).astype(CD)
            da_ref[0, rows, :] = da
            db_ref[0, rows, :] = db
            f_ref[0, rows, :] = (sl * bv).astype(CD)
            das.append(da)
            dbs.append(db)
        for rows, da, db in zip(blocks, das, dbs):
            dh[rows, :] += _mm(da, wg_ref[0], NN) + _mm(db, wu_ref[0], NN)

        @pl.when(j == nsh - 1)
        def _():
            xv = x_ref[...]
            r = lax.rsqrt(jnp.mean(xv * xv, axis=-1, keepdims=True) + EPS)
            dhv = dh[...]
            u = dhv * g_ref[...]
            dx_ref[...] = do_ref[...] + r * u - xv * (r * r * r) * jnp.mean(u * xv, axis=-1, keepdims=True)
            dg_ref[...] += jnp.sum(dhv * xv * r, axis=0, keepdims=True)

    act = pl.BlockSpec((1, tm, fs), lambda i, j: (j, i, 0))
    row = pl.BlockSpec((tm, d), lambda i, j: (i, 0))
    return pl.pallas_call(
        body, name="ffn_bwd", grid=(s // tm, nsh),
        in_specs=[row, row, pl.BlockSpec((1, d), lambda i, j: (0, 0)), act, act, act]
        + [pl.BlockSpec((1, fs, d), lambda i, j: (j, 0, 0))] * 3,
        out_specs=[row, act, act, act, pl.BlockSpec((1, d), lambda i, j: (0, 0)), row],
        out_shape=[jax.ShapeDtypeStruct((s, d), F32), jax.ShapeDtypeStruct((nsh, s, fs), CD),
                   jax.ShapeDtypeStruct((nsh, s, fs), CD), jax.ShapeDtypeStruct((nsh, s, fs), CD),
                   jax.ShapeDtypeStruct((1, d), F32), jax.ShapeDtypeStruct((s, d), CD)],
        scratch_shapes=[pltpu.VMEM((tm, d), F32)],
        compiler_params=_params(2),
    )(dout, x, g, silu, dsilu, b, wg, wu, wd)


def _tn_call(name, operands, in_specs, out_shape, out_spec, grid, acc_shape, pick, scale=1.0):
    nk = grid[-1]
    n_in = len(operands)

    def body(*refs):
        out_ref, out16_ref, acc = refs[n_in], refs[n_in + 1], refs[n_in + 2]
        k = pl.program_id(len(grid) - 1)

        @pl.when(k == 0)
        def _():
            acc[...] = jnp.zeros_like(acc)

        pick(refs[:n_in], acc)

        @pl.when(k == nk - 1)
        def _():
            res = (acc[...] if scale == 1.0 else acc[...] * scale).reshape(out_ref.shape)
            out_ref[...] = res
            out16_ref[...] = res.astype(CD)

    return pl.pallas_call(
        body, name=name, grid=grid, in_specs=in_specs, out_specs=[out_spec, out_spec],
        out_shape=[out_shape, jax.ShapeDtypeStruct(out_shape.shape, CD)],
        scratch_shapes=[pltpu.VMEM(acc_shape, F32)], compiler_params=_params(len(grid)),
    )(*operands)


def _dw_shared_b(name, a3, b, tk, scale):
    nj, s, m = a3.shape
    n = b.shape[1]

    def pick(refs, acc):
        rows = pl.ds(pl.multiple_of(pl.program_id(1) * tk, tk), tk)
        acc[...] += _mm(refs[0][0], refs[1][rows, :], TN)

    return _tn_call(name, (a3, b),
                    [pl.BlockSpec((1, tk, m), lambda j, k: (j, k, 0)), pl.BlockSpec((s, n), lambda j, k: (0, 0))],
                    jax.ShapeDtypeStruct((nj, m, n), F32), pl.BlockSpec((1, m, n), lambda j, k: (j, 0, 0)),
                    (nj, s // tk), (m, n), pick, scale)


def _dw_colblocks(name, a, b, nj, tk):
    s, m = a.shape
    n = b.shape[1] // nj
    nk = s // tk

    def body(a_ref, b_ref, out_ref, out16_ref, acc):
        k = pl.program_id(0)

        @pl.when(k == 0)
        def _():
            acc[...] = jnp.zeros_like(acc)

        acc[...] += _mm(a_ref[...], b_ref[...], TN)

        @pl.when(k == nk - 1)
        def _():
            for j in range(nj):
                res = acc[:, j * n:(j + 1) * n]
                out_ref[j] = res
                out16_ref[j] = res.astype(CD)

    whole = pl.BlockSpec((nj, m, n), lambda k: (0, 0, 0))
    return pl.pallas_call(
        body, name=name, grid=(nk,),
        in_specs=[pl.BlockSpec((tk, m), lambda k: (k, 0)), pl.BlockSpec((tk, nj * n), lambda k: (k, 0))],
        out_specs=[whole, whole],
        out_shape=[jax.ShapeDtypeStruct((nj, m, n), F32), jax.ShapeDtypeStruct((nj, m, n), CD)],
        scratch_shapes=[pltpu.VMEM((m, nj * n), F32)], compiler_params=_params(1),
    )(a, b)


DP_WIDTHS = (768, 512, 512, 512, 512, 512, 2048)
DW_IN_COLS = 512


def _dw_in(dps, hb, tk):
    s, d = hb.shape
    nk = s // tk
    blocks, row = [], 0
    for p, width in enumerate(DP_WIDTHS):
        step = width if width <= 768 else DW_IN_COLS
        for c0 in range(0, width, step):
            blocks.append((p, c0, step, row))
            row += step
    nb, max_w = len(blocks), max(b[2] for b in blocks)
    first = [min(i for i, b in enumerate(blocks) if b[0] == p) for p in range(len(DP_WIDTHS))]
    count = [sum(1 for b in blocks if b[0] == p) for p in range(len(DP_WIDTHS))]

    def body(*refs):
        dp_refs, hb_ref, out_ref, out16_ref, acc, acc16, sems = refs[:7], refs[7], refs[8], refs[9], refs[10], refs[11], refs[12]
        b, k = pl.program_id(0), pl.program_id(1)
        rows = pl.ds(pl.multiple_of(k * tk, tk), tk)

        def writes(i):
            _, _, w, r0 = blocks[i]
            slot = i % 2
            return (pltpu.make_async_copy(acc.at[slot, 0:w], out_ref.at[r0:r0 + w], sems.at[slot, 0]),
                    pltpu.make_async_copy(acc16.at[slot, 0:w], out16_ref.at[r0:r0 + w], sems.at[slot, 1]))

        for i, (p, _, w, _) in enumerate(blocks):
            @pl.when(b == i)
            def _(i=i, p=p, w=w):
                slot = i % 2
                prod = _mm(dp_refs[p][...], hb_ref[rows, :], TN)

                @pl.when(k == 0)
                def _():
                    acc[slot, 0:w] = prod

                @pl.when(k > 0)
                def _():
                    acc[slot, 0:w] += prod

                @pl.when(k == nk - 1)
                def _():
                    if i >= 1:
                        for cp in writes(i - 1):
                            cp.wait()
                    acc16[slot, 0:w] = acc[slot, 0:w].astype(CD)
                    for cp in writes(i):
                        cp.start()
                    if i == nb - 1:
                        for cp in writes(i):
                            cp.wait()

    def piece_spec(p):
        width = DP_WIDTHS[p]
        cols = width if width <= 768 else DW_IN_COLS

        def imap(b, k):
            active = (b >= first[p]) & (b < first[p] + count[p])
            return (jnp.where(active, k, jnp.where(b < first[p], 0, nk - 1)), jnp.clip(b - first[p], 0, count[p] - 1))

        return pl.BlockSpec((tk, cols), imap)

    return pl.pallas_call(
        body, name="dw_in", grid=(nb, nk),
        in_specs=[piece_spec(p) for p in range(len(DP_WIDTHS))] + [pl.BlockSpec((s, d), lambda b, k: (0, 0))],
        out_specs=[ANY, ANY],
        out_shape=[jax.ShapeDtypeStruct((sum(DP_WIDTHS), d), F32), jax.ShapeDtypeStruct((sum(DP_WIDTHS), d), CD)],
        scratch_shapes=[pltpu.VMEM((2, max_w, d), F32), pltpu.VMEM((2, max_w, d), CD), pltpu.SemaphoreType.DMA((2, 2))],
        compiler_params=_params(2),
    )(*dps, hb)


def _mix_in_fwd(x, g, w_t, tm):
    s, d = x.shape
    n_in = w_t.shape[0]

    def body(x_ref, g_ref, w_ref, qkv_ref, hg_ref, gt_ref, hb_ref):
        xv = x_ref[...]
        r = lax.rsqrt(jnp.mean(xv * xv, axis=-1, keepdims=True) + EPS)
        h = (xv * r * g_ref[...]).astype(CD)
        hb_ref[...] = h
        off = DP_WIDTHS[0]
        qkv_ref[...] = _mm(h, w_ref[0:off, :], NT)
        for c, width in enumerate(DP_WIDTHS[1:6]):
            hg_ref[:, c * width:(c + 1) * width] = _mm(h, w_ref[off:off + width, :], NT)
            off += width
        gate = DP_WIDTHS[6] // 2
        for c in range(2):
            gt_ref[:, c * gate:(c + 1) * gate] = _mm(h, w_ref[off:off + gate, :], NT).astype(CD)
            off += gate

    row = lambda w: pl.BlockSpec((tm, w), lambda i: (i, 0))
    return pl.pallas_call(
        body, name="mix_in_fwd", grid=(s // tm,),
        in_specs=[row(d), pl.BlockSpec((1, d), lambda i: (0, 0)), pl.BlockSpec((n_in, d), lambda i: (0, 0))],
        out_specs=[row(768), row(2560), row(2048), row(d)],
        out_shape=[jax.ShapeDtypeStruct((s, 768), F32), jax.ShapeDtypeStruct((s, 2560), F32),
                   jax.ShapeDtypeStruct((s, 2048), CD), jax.ShapeDtypeStruct((s, d), CD)],
        compiler_params=_params(1),
    )(x, g, w_t)


def _mix_in_bwd(dps, w_t, x, dres, g, tm, after=()):
    s, d = x.shape
    n_in = w_t.shape[0]

    def body(*refs):
        refs = refs[len(after):]
        dp_refs = refs[:7]
        w_ref, x_ref, dr_ref, g_ref, dx_ref, dg_ref = refs[7:]

        @pl.when(pl.program_id(0) == 0)
        def _():
            dg_ref[...] = jnp.zeros_like(dg_ref)

        dhv = jnp.zeros((tm, d), F32)
        off = 0
        for ref, width in zip(dp_refs, DP_WIDTHS):
            dhv = dhv + _mm(ref[...], w_ref[off:off + width, :], NN)
            off += width
        xv = x_ref[...]
        r = lax.rsqrt(jnp.mean(xv * xv, axis=-1, keepdims=True) + EPS)
        u = dhv * g_ref[...]
        dx_ref[...] = dr_ref[...] + r * u - xv * (r * r * r) * jnp.mean(u * xv, axis=-1, keepdims=True)
        dg_ref[...] += jnp.sum(dhv * xv * r, axis=0, keepdims=True)

    row = pl.BlockSpec((tm, d), lambda i: (i, 0))
    vec = pl.BlockSpec((1, d), lambda i: (0, 0))
    return pl.pallas_call(
        body, name="mix_in_bwd", grid=(s // tm,),
        in_specs=[ANY] * len(after) + [pl.BlockSpec((tm, w), lambda i: (i, 0)) for w in DP_WIDTHS]
        + [pl.BlockSpec((n_in, d), lambda i: (0, 0)), row, row, vec],
        out_specs=[row, vec],
        out_shape=[jax.ShapeDtypeStruct((s, d), F32), jax.ShapeDtypeStruct((1, d), F32)],
        compiler_params=_params(1),
    )(*after, *dps, w_t, x, dres, g)


def _headnorm_rope(x, gain, cos, sin, blocksum, swap):
    ss = _xdot(x * x, blocksum)
    r = lax.rsqrt(ss * (1.0 / HEAD_DIM) + EPS)
    y = x * r * gain
    return y * cos + _xdot(y, swap) * sin, r


def _headnorm_rope_bwd(dz, x, gain, cos, sin, blocksum, swap):
    ss = _xdot(x * x, blocksum)
    r = lax.rsqrt(ss * (1.0 / HEAD_DIM) + EPS)
    dy = dz * cos + _xdot(dz * sin, swap)
    u = dy * gain
    mean_ux = _xdot(u * x, blocksum) * (1.0 / HEAD_DIM)
    dx = r * u - x * (r * r * r) * mean_ux
    return dx, jnp.sum(dy * x * r, axis=0, keepdims=True)


def _qk_prep(pqkv, gq, gk, cos2, sin2, tm):
    s = pqkv.shape[0]
    bs512, sw512, eq, swh = _bf(_np_blocksum(512)), _bf(_np_swap32(512)), _bf(_np_expand_q()), _bf(_np_swap_halves())

    def body(q_ref, kv_ref, gq_ref, gk_ref, c_ref, s_ref, bs_ref, sw_ref, eq_ref, swh_ref, qe_ref, k_ref, v_ref, vs_ref):
        c2, s2 = c_ref[...], s_ref[...]
        c8, s8 = jnp.tile(c2, (1, 4)), jnp.tile(s2, (1, 4))
        bs, sw = bs_ref[...], sw_ref[...]
        zq, _ = _headnorm_rope(q_ref[...], gq_ref[...], c8, s8, bs, sw)
        qe_ref[...] = _mm(zq * (HEAD_DIM ** -0.5), eq_ref[...], NN).astype(CD)
        kv = kv_ref[...]
        zk, _ = _headnorm_rope(kv[:, :LANES], gk_ref[...], c2, s2, bs[:LANES, :LANES], sw[:LANES, :LANES])
        k_ref[...] = zk.astype(CD)
        v = kv[:, LANES:]
        v_ref[...] = v.astype(CD)
        vs_ref[...] = _mm(v, swh_ref[...], NN).astype(CD)

    full = lambda a: pl.BlockSpec(a.shape, lambda i: (0,) * a.ndim)
    tab = pl.BlockSpec((tm, LANES), lambda i: (i, 0))
    return pl.pallas_call(
        body, name="qk_prep", grid=(s // tm,),
        in_specs=[pl.BlockSpec((tm, 512), lambda i: (i, 0)), pl.BlockSpec((tm, 256), lambda i: (i, 2)),
                  full(gq), full(gk), tab, tab, full(bs512), full(sw512), full(eq), full(swh)],
        out_specs=[pl.BlockSpec((tm, 1024), lambda i: (i, 0)), tab, tab, tab],
        out_shape=[jax.ShapeDtypeStruct((s, 1024), CD)] + [jax.ShapeDtypeStruct((s, LANES), CD)] * 3,
        compiler_params=_params(1),
    )(pqkv, pqkv, gq, gk, cos2, sin2, bs512, sw512, eq, swh)


def _qk_prep_bwd(pqkv, dq, dk, dv, gq, gk, cos2, sin2, tm):
    s = pqkv.shape[0]
    bs512, sw512 = _bf(_np_blocksum(512)), _bf(_np_swap32(512))

    def body(q_ref, kv_ref, dq_ref, dk_ref, dv_ref, gq_ref, gk_ref, c_ref, s_ref, bs_ref, sw_ref,
             dp_ref, dgq_ref, dgk_ref):
        @pl.when(pl.program_id(0) == 0)
        def _():
            dgq_ref[...] = jnp.zeros_like(dgq_ref)
            dgk_ref[...] = jnp.zeros_like(dgk_ref)

        c2, s2 = c_ref[...], s_ref[...]
        c8, s8 = jnp.tile(c2, (1, 4)), jnp.tile(s2, (1, 4))
        bs, sw = bs_ref[...], sw_ref[...]
        dzq = dq_ref[...] * (HEAD_DIM ** -0.5)
        dxq, dgq = _headnorm_rope_bwd(dzq, q_ref[...], gq_ref[...], c8, s8, bs, sw)
        kv = kv_ref[...]
        dxk, dgk = _headnorm_rope_bwd(dk_ref[...], kv[:, :LANES], gk_ref[...], c2, s2, bs[:LANES, :LANES], sw[:LANES, :LANES])
        dp_ref[...] = jnp.concatenate([dxq, dxk, dv_ref[...]], axis=1).astype(CD)
        dgq_ref[...] += dgq
        dgk_ref[...] += dgk

    full = lambda a: pl.BlockSpec(a.shape, lambda i: (0,) * a.ndim)
    tab = pl.BlockSpec((tm, LANES), lambda i: (i, 0))
    return pl.pallas_call(
        body, name="qk_prep_bwd", grid=(s // tm,),
        in_specs=[pl.BlockSpec((tm, 512), lambda i: (i, 0)), pl.BlockSpec((tm, 256), lambda i: (i, 2)),
                  pl.BlockSpec((tm, 512), lambda i: (i, 0)), tab, tab, full(gq), full(gk), tab, tab,
                  full(bs512), full(sw512)],
        out_specs=[pl.BlockSpec((tm, 768), lambda i: (i, 0)), pl.BlockSpec((1, 512), lambda i: (0, 0)),
                   pl.BlockSpec((1, LANES), lambda i: (0, 0))],
        out_shape=[jax.ShapeDtypeStruct((s, 768), CD), jax.ShapeDtypeStruct((1, 512), F32),
                   jax.ShapeDtypeStruct((1, LANES), F32)],
        compiler_params=_params(1),
    )(pqkv, pqkv, dq, dk, dv, gq, gk, cos2, sin2, bs512, sw512)


def _kv_rows(h):
    return pl.ds(pl.multiple_of((h // 4) * HEAD_DIM, HEAD_DIM), HEAD_DIM)


def _attn_fwd(qe, k, v, vs, tq):
    s = k.shape[0]

    def body(q0_ref, q1_ref, q2_ref, q3_ref, k_ref, v_ref, vs_ref, o_ref, lse_ref):
        grp = pl.program_id(0)
        kk = k_ref[...]
        heads = range(4)
        scores = [_mm(q_ref[...], kk, NT) for q_ref in (q0_ref, q1_ref, q2_ref, q3_ref)]
        mxs = [jnp.max(sc, axis=-1, keepdims=True) for sc in scores]
        es = [jnp.exp(scores[r] - mxs[r]) for r in heads]
        ls = [jnp.sum(e, axis=-1, keepdims=True) for e in es]
        for r in heads:
            lse_ref[r] = mxs[r] + jnp.log(ls[r])
        vsel = [jnp.where(grp != par, vs_ref[...], v_ref[...]) for par in (0, 1)]
        outs = [_mm(es[r], vsel[r % 2], NN) * (1.0 / ls[r]) for r in heads]
        low = lax.broadcasted_iota(jnp.int32, (1, LANES), 1) < HEAD_DIM
        o_ref[...] = jnp.concatenate([jnp.where(low, outs[0], outs[1]), jnp.where(low, outs[2], outs[3])], axis=1)

    kv = pl.BlockSpec((s, LANES), lambda g, i: (0, 0))
    qblk = lambda r: pl.BlockSpec((tq, LANES), lambda g, i: (i, 4 * g + r))
    return pl.pallas_call(
        body, name="attn_fwd", grid=(2, s // tq),
        in_specs=[qblk(0), qblk(1), qblk(2), qblk(3), kv, kv, kv],
        out_specs=[pl.BlockSpec((tq, 2 * LANES), lambda g, i: (i, g)), pl.BlockSpec((4, tq, 1), lambda g, i: (g, i, 0))],
        out_shape=[jax.ShapeDtypeStruct((s, 512), F32), jax.ShapeDtypeStruct((8, s, 1), F32)],
        compiler_params=_params(2),
    )(qe, qe, qe, qe, k, v, vs)


def _attn_bwd(qe, k, kt, v, doe, delta, lse, tq):
    s = k.shape[0]

    nh = ATTN_BWD_HEADS

    def body(*refs):
        q_refs, (k_ref, kt_ref, v_ref) = refs[:nh], refs[nh:nh + 3]
        do_refs, dl_refs = refs[nh + 3:2 * nh + 3], refs[2 * nh + 3:3 * nh + 3]
        lse_ref, dqt_ref, dkt_ref, dvt_ref, qt, dot = refs[3 * nh + 3:]

        @pl.when((pl.program_id(0) == 0) & (pl.program_id(1) == 0))
        def _():
            dkt_ref[...] = jnp.zeros_like(dkt_ref)
            dvt_ref[...] = jnp.zeros_like(dvt_ref)

        rows = _kv_rows(nh * pl.program_id(0))
        kt = kt_ref[rows, :]
        dkt = jnp.zeros((HEAD_DIM, s), F32)
        dvt = jnp.zeros((HEAD_DIM, s), F32)
        firsts = [(_mm(q_ref[...], k_ref[...], NT), _mm(do_ref[...], v_ref[...], NT))
                  for q_ref, do_ref in zip(q_refs, do_refs)]
        for idx, (q_ref, do_ref, dl_ref) in enumerate(zip(q_refs, do_refs, dl_refs)):
            q, do = q_ref[...], do_ref[...]
            sc, dp = firsts[idx]
            p = jnp.exp(sc - lse_ref[idx])
            ds = p * (dp - jnp.max(dl_ref[...], axis=-1, keepdims=True))
            dqt_ref[idx * HEAD_DIM:(idx + 1) * HEAD_DIM, :] = _mm(kt, ds, NT)
            qt[idx] = jnp.transpose(q.astype(F32))
            dot[idx] = jnp.transpose(do.astype(F32))
            dkt = dkt + _mm(qt[idx, rows, :], ds, NN)
            dvt = dvt + _mm(dot[idx, rows, :], p, NN)
        dkt_ref[rows, :] += dkt
        dvt_ref[rows, :] += dvt

    kv = pl.BlockSpec((s, LANES), lambda m, i: (0, 0))
    kvt = pl.BlockSpec((LANES, s), lambda m, i: (0, 0))
    blks = [pl.BlockSpec((tq, LANES), lambda m, i, r=r: (i, nh * m + r)) for r in range(nh)]
    return pl.pallas_call(
        body, name="attn_bwd", grid=(8 // nh, s // tq),
        in_specs=blks + [kv, kvt, kv] + blks + blks + [pl.BlockSpec((nh, tq, 1), lambda m, i: (m, i, 0))],
        out_specs=[pl.BlockSpec((nh * HEAD_DIM, tq), lambda m, i: (m, i)), kvt, kvt],
        out_shape=[jax.ShapeDtypeStruct((8 * HEAD_DIM, s), F32), jax.ShapeDtypeStruct((LANES, s), F32),
                   jax.ShapeDtypeStruct((LANES, s), F32)],
        scratch_shapes=[pltpu.VMEM((nh, LANES, tq), F32), pltpu.VMEM((nh, LANES, tq), F32)],
        compiler_params=_params(2),
    )(*[qe] * nh, k, kt, v, *[doe] * nh, *[delta] * nh, lse)


@jax.custom_vjp
def _mm_nn(a, b):
    return _mm(a, b, NN)


_mm_nn.defvjp(lambda a, b: (_mm(a, b, NN), (a, b)),
              lambda res, g: (_mm(g, res[1], NT), _mm(res[0], g, TN)))


@jax.custom_vjp
def _mm_nt(a, b):
    return _mm(a, b, NT)


_mm_nt.defvjp(lambda a, b: (_mm(a, b, NT), (a, b)),
              lambda res, g: (_mm(g, res[1], NN), _mm(g, res[0], TN)))


@jax.custom_vjp
def _mm_tn(a, b):
    return _mm(a, b, TN)


_mm_tn.defvjp(lambda a, b: (_mm(a, b, TN), (a, b)),
              lambda res, g: (_mm(res[1], g, NT), _mm(res[0], g, NN)))


@jax.custom_vjp
def _cmm(m, mt, x):
    return _xdot_l(m, x)


_cmm.defvjp(lambda m, mt, x: (_xdot_l(m, x), (m, mt)),
            lambda res, g: (jnp.zeros_like(res[0]), jnp.zeros_like(res[1]), _xdot_l(res[1], g)))


def _hgrn_masks(t, rev):
    n_ch = t // CHUNK
    r = jnp.bitwise_and(lax.broadcasted_iota(jnp.int32, (2 * t, t), 0), t - 1)
    c = lax.broadcasted_iota(jnp.int32, (2 * t, t), 1)
    same = jnp.right_shift(r, 5) == jnp.right_shift(c, 5)
    tri2 = same & ((c >= r) if rev else (c <= r))
    pr = lax.broadcasted_iota(jnp.int32, (LANES, LANES), 0)
    pc = lax.broadcasted_iota(jnp.int32, (LANES, LANES), 1)
    diag = jnp.right_shift(pr, 6) == jnp.right_shift(pc, 6)
    qr = lax.broadcasted_iota(jnp.int32, (t, n_ch * LANES), 0)
    qc = lax.broadcasted_iota(jnp.int32, (t, n_ch * LANES), 1)
    rows_chunk = jnp.right_shift(qc, 7) == jnp.right_shift(qr, 5)
    vr = lax.broadcasted_iota(jnp.int32, (n_ch * LANES, t), 0)
    vc = lax.broadcasted_iota(jnp.int32, (n_ch * LANES, t), 1)
    cols_chunk = jnp.right_shift(vr, 7) == jnp.right_shift(vc, 5)
    return dict(tri2=tri2, diag=diag, rows_chunk=rows_chunk, cols_chunk=cols_chunk)


def _hgrn_gates(xf, lb):
    f = lb + (1.0 - lb) * _sigmoid(xf)
    return 1.0 - f, jnp.log(f)


def _hgrn_dir(*args):
    return _hgrn_dirs([args])[0][:2]


def _hgrn_dirs(arg_sets):
    chains = [_hgrn_phases(*a) for a in arg_sets]
    results = [None] * len(chains)
    while any(r is None for r in results):
        for n, chain in enumerate(chains):
            if results[n] is None:
                try:
                    next(chain)
                except StopIteration as done:
                    results[n] = done.value
    return results


def _hgrn_phases(xq, xf, v, lb, state, cm, cmt, mk, rev):
    t = xq.shape[0]
    n_ch = t // CHUNK
    lo = lax.broadcasted_iota(jnp.int32, (1, LANES), 1) < HEAD_DIM
    k, lf = _hgrn_gates(xf, lb)
    cs = _cmm(cm, cmt, lf)
    yield
    q = xq * _sigmoid(xq)
    b, bm, bl = cs[:t], cs[t:2 * t], cs[2 * t:]
    qd = q * jnp.exp(b - bm)
    kd = k * jnp.exp(bm - b)
    yield
    qd2 = jnp.concatenate([jnp.where(lo, qd, 0.0), jnp.where(lo, 0.0, qd)], axis=0)
    scores = _mm_nt(qd2, kd)
    yield
    kc = k * jnp.exp(bl - b)
    qe = q * jnp.exp(b)
    vexp = jnp.where(mk["cols_chunk"], jnp.concatenate([jnp.transpose(v)] * n_ch, axis=0), 0.0)
    adds = _mm_nn(vexp, kc)
    yield
    o2 = _mm_nn(jnp.where(mk["tri2"], scores, 0.0), v)
    o = jnp.where(lo, o2[:t], o2[t:])
    yield
    dec = jnp.exp(bl)
    entering = [None] * n_ch
    for c in (range(n_ch - 1, -1, -1) if rev else range(n_ch)):
        entering[c] = state
        d = jnp.concatenate([dec[c * CHUNK:(c + 1) * CHUNK]] * (LANES // CHUNK), axis=0)
        state = d * state + jnp.where(mk["diag"], adds[c * LANES:(c + 1) * LANES], 0.0)
    yield
    qexp = jnp.where(mk["rows_chunk"], jnp.concatenate([qe] * n_ch, axis=1), 0.0)
    middle = entering[n_ch // 2 - 1] if rev else entering[n_ch // 2]
    return o + _mm_nt(qexp, jnp.concatenate(entering, axis=1)), state, middle


def _hgrn_lower_bounds(l):
    out = []
    for d in (0, 1):
        l0, l1 = l[2 * d:2 * d + 1, :], l[2 * d + 1:2 * d + 2, :]
        mx = jnp.maximum(l0, l1)
        e0, e1 = jnp.exp(l0 - mx), jnp.exp(l1 - mx)
        out.append(e0 / (e0 + e1))
    return out


def _hgrn_consts(t):
    cf, cb = _np_hgrn_cums(t, False), _np_hgrn_cums(t, True)
    return (_bf(cf), _bf(cf.T), _bf(cb), _bf(cb.T), _bf(_np_blocksum(LANES)))


def _hgrn_fwd(ph, lbl, ng):
    s = ph.shape[0]
    t = 2 * min(HG_TILE, s // 2)
    nt = s // t
    consts = _hgrn_consts(t)

    def body(xq_ref, xff_ref, xfb_ref, xi_ref, xg_ref, lbl_ref, ng_ref, cf_ref, cft_ref, cb_ref, cbt_ref, bs_ref,
             o_ref, pre_ref, st_ref, acc):
        lbf, lbb = _hgrn_lower_bounds(lbl_ref)
        mk_f, mk_b = _hgrn_masks(t, False), _hgrn_masks(t, True)
        zero = jnp.zeros((LANES, LANES), F32)

        def rows_of(i):
            return pl.ds(pl.multiple_of(i * t, t), t)

        acc[...] = jnp.zeros_like(acc)

        def step(i, states):
            tb = nt - 1 - i
            rf, rb = rows_of(i), rows_of(tb)
            (of, sf, mid_f), (ob, sb, mid_b) = _hgrn_dirs([
                (xq_ref[rf, :], xff_ref[rf, :], xi_ref[rf, :], lbf, states[0], cf_ref[...], cft_ref[...], mk_f, False),
                (xq_ref[rb, :], xfb_ref[rb, :], xi_ref[rb, :], lbb, states[1], cb_ref[...], cbt_ref[...], mk_b, True)])
            st_ref[0, 0, 2 * i] = states[0]
            st_ref[0, 0, 2 * i + 1] = mid_f
            st_ref[0, 1, 2 * tb + 1] = states[1]
            st_ref[0, 1, 2 * tb] = mid_b
            acc[rf, :] += of
            acc[rb, :] += ob
            return sf, sb

        lax.fori_loop(0, nt, step, (zero, zero))

        def step_n(i, carry):
            rows = rows_of(i)
            o = acc[rows, :]
            ss = _xdot(o * o, bs_ref[...])
            r = lax.rsqrt(ss * (1.0 / HEAD_DIM) + EPS)
            xg = xg_ref[rows, :]
            pre_ref[rows, :] = o
            o_ref[rows, :] = ((o * r * ng_ref[...]) * (xg * _sigmoid(xg))).astype(CD)
            return carry

        lax.fori_loop(0, nt, step_n, 0)

    col = lambda off: pl.BlockSpec((s, LANES), lambda m: (0, off + m))
    full = lambda a: pl.BlockSpec(a.shape, lambda m: (0,) * a.ndim)
    return pl.pallas_call(
        body, name="hgrn_fwd", grid=(4,),
        in_specs=[col(0), col(4), col(8), col(12), col(16), pl.BlockSpec((4, LANES), lambda m: (0, m)),
                  pl.BlockSpec((1, LANES), lambda m: (0, m))] + [full(c) for c in consts],
        out_specs=[col(0), col(0), pl.BlockSpec((1, 2, 2 * nt, LANES, LANES), lambda m: (m, 0, 0, 0, 0))],
        out_shape=[jax.ShapeDtypeStruct((s, 512), CD), jax.ShapeDtypeStruct((s, 512), F32),
                   jax.ShapeDtypeStruct((4, 2, 2 * nt, LANES, LANES), F32)],
        scratch_shapes=[pltpu.VMEM((s, LANES), F32)],
        compiler_params=_params(1),
    )(ph, ph, ph, ph, ph, lbl, ng, *consts)


def _hgrn_bwd(ph, pre, dout, states, lbl, ng):
    s = ph.shape[0]
    t = min(HG_TILE, s)
    nt = s // t
    consts = _hgrn_consts(t)

    def body(xq_ref, xff_ref, xfb_ref, xi_ref, xg_ref, pre_ref, do_ref, st_ref, lbl_ref, ng_ref,
             cf_ref, cft_ref, cb_ref, cbt_ref, bs_ref,
             dq_ref, dff_ref, dfb_ref, di_ref, dg_ref, dlb_ref, dng_ref, dpre, dq_acc, dv_acc):
        lbf, lbb = _hgrn_lower_bounds(lbl_ref)
        mk_f, mk_b = _hgrn_masks(t, False), _hgrn_masks(t, True)
        zero = jnp.zeros((LANES, LANES), F32)
        zrow = jnp.zeros((1, LANES), F32)

        def rows_of(i):
            return pl.ds(pl.multiple_of(i * t, t), t)

        def step_n(i, dng):
            rows = rows_of(i)
            o, xg, do = pre_ref[rows, :], xg_ref[rows, :], do_ref[rows, :]
            bs = bs_ref[...]
            r = lax.rsqrt(_xdot(o * o, bs) * (1.0 / HEAD_DIM) + EPS)
            sg = _sigmoid(xg)
            gate = xg * sg
            don = do * gate
            dg_ref[rows, :] = (do * (o * r * ng_ref[...]) * (sg * (1.0 + xg * (1.0 - sg)))).astype(CD)
            u = don * ng_ref[...]
            dpre[rows, :] = r * u - o * (r * r * r) * (_xdot(u * o, bs) * (1.0 / HEAD_DIM))
            return dng + jnp.sum(don * o * r, axis=0, keepdims=True)

        dng_ref[...] = lax.fori_loop(0, nt, step_n, zrow)

        dq_acc[...] = jnp.zeros_like(dq_acc)
        dv_acc[...] = jnp.zeros_like(dv_acc)

        def step_g(i, carry):
            dsf, dsb, dlbf, dlbb = carry
            tf, tb = nt - 1 - i, i
            rf, rb = rows_of(tf), rows_of(tb)
            cf, cft, cb, cbt = cf_ref[...], cft_ref[...], cb_ref[...], cbt_ref[...]

            def both(xq_f, xf_f, v_f, lb_f, s_f, xq_b, xf_b, v_b, lb_b, s_b):
                (of, sf, _), (ob, sb, _) = _hgrn_dirs([(xq_f, xf_f, v_f, lb_f, s_f, cf, cft, mk_f, False),
                                                       (xq_b, xf_b, v_b, lb_b, s_b, cb, cbt, mk_b, True)])
                return of, sf, ob, sb

            _, vjp = jax.vjp(both, xq_ref[rf, :], xff_ref[rf, :], xi_ref[rf, :], lbf, st_ref[0, 0, tf],
                             xq_ref[rb, :], xfb_ref[rb, :], xi_ref[rb, :], lbb, st_ref[0, 1, tb])
            dq_f, dx_f, dv_f, gf, dsf, dq_b, dx_b, dv_b, gb, dsb = vjp((dpre[rf, :], dsf, dpre[rb, :], dsb))
            dff_ref[rf, :] = dx_f.astype(CD)
            dfb_ref[rb, :] = dx_b.astype(CD)
            dq_acc[rf, :] += dq_f
            dv_acc[rf, :] += dv_f
            dq_acc[rb, :] += dq_b
            dv_acc[rb, :] += dv_b
            return dsf, dsb, dlbf + gf, dlbb + gb

        _, _, dlbf, dlbb = lax.fori_loop(0, nt, step_g, (zero, zero, zrow, zrow))
        dlb_ref[0:1, :] = dlbf
        dlb_ref[1:2, :] = dlbb
        dq_ref[...] = dq_acc[...].astype(CD)
        di_ref[...] = dv_acc[...].astype(CD)

    col = lambda off: pl.BlockSpec((s, LANES), lambda m: (0, off + m))
    full = lambda a: pl.BlockSpec(a.shape, lambda m: (0,) * a.ndim)
    stream = jax.ShapeDtypeStruct((s, 512), CD)
    return pl.pallas_call(
        body, name="hgrn_bwd", grid=(4,),
        in_specs=[col(0), col(4), col(8), col(12), col(16), col(0), col(0),
                  pl.BlockSpec((1, 2, nt, LANES, LANES), lambda m: (m, 0, 0, 0, 0)),
                  pl.BlockSpec((4, LANES), lambda m: (0, m)),
                  pl.BlockSpec((1, LANES), lambda m: (0, m))] + [full(c) for c in consts],
        out_specs=[col(0)] * 5 + [pl.BlockSpec((2, LANES), lambda m: (0, m)), pl.BlockSpec((1, LANES), lambda m: (0, m))],
        out_shape=[stream] * 5 + [jax.ShapeDtypeStruct((2, 512), F32), jax.ShapeDtypeStruct((1, 512), F32)],
        scratch_shapes=[pltpu.VMEM((s, LANES), F32), pltpu.VMEM((s, LANES), F32), pltpu.VMEM((s, LANES), F32)],
        compiler_params=_params(1, HGRN_BWD_VMEM),
    )(ph, ph, ph, ph, ph, pre, dout, states, lbl, ng, *consts)


def _branch_out(o, w4):
    o = o.astype(CD)
    return jnp.concatenate([_mm(o, w4[j], NN) for j in range(N_SHARD)], axis=1)


def _mix_out_fwd(x, oa, ob, pg, wa, wb, wo, tm):
    s, d = x.shape

    def body(x_ref, oa_ref, ob_ref, ga_ref, gb_ref, wa_ref, wb_ref, wo_ref, xo_ref):
        ya = _branch_out(oa_ref[...], wa_ref)
        yb = _branch_out(ob_ref[...], wb_ref)
        merged = _sigmoid(ga_ref[...].astype(F32)) * ya + _sigmoid(gb_ref[...].astype(F32)) * yb
        xo_ref[...] = x_ref[...] + _mm(merged, wo_ref[...], NN)

    row = pl.BlockSpec((tm, d), lambda i: (i, 0))
    half = pl.BlockSpec((tm, 512), lambda i: (i, 0))
    full = lambda a: pl.BlockSpec(a.shape, lambda i: (0,) * a.ndim)
    return pl.pallas_call(
        body, name="mix_out_fwd", grid=(s // tm,),
        in_specs=[row, half, half, row, pl.BlockSpec((tm, d), lambda i: (i, 1)), full(wa), full(wb), full(wo)],
        out_specs=row, out_shape=jax.ShapeDtypeStruct((s, d), F32),
        compiler_params=_params(1),
    )(x, oa, ob, pg, pg, wa, wb, wo)


def _mix_out_bwd(dx, oa, ob, pg, wa, wb, wo, tm, after=()):
    s, d = dx.shape
    eq, ebc = _bf(_np_expand_q()), _bf(_np_headsum_spread())

    def body(*refs):
        (dx_ref, oa_ref, ob_ref, ga_ref, gb_ref, wa_ref, wb_ref, wo_ref, eq_ref, ebc_ref,
         dpg_ref, mg_ref, dya_ref, dyb_ref, doe_ref, dl_ref, dob_ref) = refs[len(after):]
        oa = oa_ref[...]
        ya = _branch_out(oa, wa_ref)
        yb = _branch_out(ob_ref[...], wb_ref)
        sa, sb = _sigmoid(ga_ref[...].astype(F32)), _sigmoid(gb_ref[...].astype(F32))
        mg_ref[...] = (sa * ya + sb * yb).astype(CD)
        dm = _mm(dx_ref[...], wo_ref[...], NT)
        dpg_ref[...] = jnp.concatenate([dm * ya * sa * (1.0 - sa), dm * yb * sb * (1.0 - sb)], axis=1).astype(CD)
        dya, dyb = dm * sa, dm * sb
        dya_ref[...] = dya.astype(CD)
        dyb_ref[...] = dyb.astype(CD)
        doa = jnp.zeros(oa.shape, F32)
        dob = jnp.zeros(oa.shape, F32)
        for j in range(N_SHARD):
            doa = doa + _mm(dya[:, 256 * j:256 * j + 256], wa_ref[j], NT)
            dob = dob + _mm(dyb[:, 256 * j:256 * j + 256], wb_ref[j], NT)
        dob_ref[...] = dob
        doe_ref[...] = _mm(doa, eq_ref[...], NN).astype(CD)
        dl_ref[...] = _xdot(doa * oa, ebc_ref[...])

    row = pl.BlockSpec((tm, d), lambda i: (i, 0))
    half = pl.BlockSpec((tm, 512), lambda i: (i, 0))
    full = lambda a: pl.BlockSpec(a.shape, lambda i: (0,) * a.ndim)
    wide = jax.ShapeDtypeStruct((s, d), CD)
    return pl.pallas_call(
        body, name="mix_out_bwd", grid=(s // tm,),
        in_specs=[ANY] * len(after) + [row, half, half, row, pl.BlockSpec((tm, d), lambda i: (i, 1)), full(wa), full(wb),
                                       full(wo), full(eq), full(ebc)],
        out_specs=[pl.BlockSpec((tm, 2048), lambda i: (i, 0)), row, row, row, row, row, half],
        out_shape=[jax.ShapeDtypeStruct((s, 2048), CD), wide, wide, wide, wide, jax.ShapeDtypeStruct((s, d), F32),
                   jax.ShapeDtypeStruct((s, 512), F32)],
        compiler_params=_params(1),
    )(*after, dx, oa, ob, pg, pg, wa, wb, wo, eq, ebc)


def _loss_head(x, g, target, tm):
    s, d = x.shape

    def body(x_ref, g_ref, t_ref, dx_ref, loss_ref, dg_ref):
        @pl.when(pl.program_id(0) == 0)
        def _():
            loss_ref[...] = jnp.zeros_like(loss_ref)
            dg_ref[...] = jnp.zeros_like(dg_ref)

        xv = x_ref[...]
        r = lax.rsqrt(jnp.mean(xv * xv, axis=-1, keepdims=True) + EPS)
        err = xv * r * g_ref[...] - t_ref[...]
        loss_ref[...] += 0.5 * jnp.sum(jnp.mean(err * err, axis=-1, keepdims=True))
        dy = err * (1.0 / d)
        u = dy * g_ref[...]
        dx_ref[...] = r * u - xv * (r * r * r) * jnp.mean(u * xv, axis=-1, keepdims=True)
        dg_ref[...] += jnp.sum(dy * xv * r, axis=0, keepdims=True)

    row = pl.BlockSpec((tm, d), lambda i: (i, 0))
    vec = pl.BlockSpec((1, d), lambda i: (0, 0))
    return pl.pallas_call(
        body, name="loss_head", grid=(s // tm,),
        in_specs=[row, vec, row], out_specs=[row, pl.BlockSpec((8, LANES), lambda i: (0, 0)), vec],
        out_shape=[jax.ShapeDtypeStruct((s, d), F32), jax.ShapeDtypeStruct((8, LANES), F32),
                   jax.ShapeDtypeStruct((1, d), F32)],
        compiler_params=_params(1),
    )(x, g, target)


def _position():
    x, y, c = lax.axis_index("x"), lax.axis_index("y"), lax.axis_index("c")
    return x, y, c, [(1 - x, y), (x, 1 - y), (1 - x, 1 - y)]


def _row_tile(rows, cap=256):
    best = rows
    for cand in range(8, min(rows, cap) + 1, 8):
        if rows % cand == 0:
            best = cand
    return best


def _cast_into_slots(shards, dtypes, me_idx):
    n = len(shards)
    tiles = [_row_tile(s.shape[0]) for s in shards]
    counts = [s.shape[0] // t for s, t in zip(shards, tiles)]
    starts = [sum(counts[:a]) for a in range(n)]

    def body(me_ref, *refs):
        i = pl.program_id(0)
        for a in range(n):
            @pl.when((i >= starts[a]) & (i < starts[a] + counts[a]))
            def _(a=a):
                refs[n + a][0] = refs[a][...].astype(dtypes[a])

    tile_of = [lambda i, a=a: jnp.clip(i - starts[a], 0, counts[a] - 1) for a in range(n)]
    return pl.pallas_call(
        body, name="cast_into_slots",
        grid_spec=pltpu.PrefetchScalarGridSpec(
            num_scalar_prefetch=1, grid=(sum(counts),),
            in_specs=[pl.BlockSpec((tiles[a], shards[a].shape[1]), lambda i, me, a=a: (tile_of[a](i), 0)) for a in range(n)],
            out_specs=[pl.BlockSpec((1, tiles[a], shards[a].shape[1]), lambda i, me, a=a: (me[0], tile_of[a](i), 0))
                       for a in range(n)]),
        out_shape=[jax.ShapeDtypeStruct((N_SHARD,) + s.shape, dt) for s, dt in zip(shards, dtypes)],
        compiler_params=_params(1),
    )(me_idx, *shards)


HBM_SPEC = pl.BlockSpec(memory_space=pltpu.HBM)
SEM_SPEC = pl.BlockSpec(memory_space=pltpu.SEMAPHORE)
DATAFLOW = pltpu.SideEffectType.DATAFLOW_SIDE_EFFECTING


def _exchange_copies(srcs, lands, send, recv, gather):
    x, y, c, chips = _position()
    me = 2 * x + y
    out = []
    for a in range(len(lands)):
        dst = lands[a].at[me]
        if gather and _halved(lands[a]):
            half = lands[a].shape[1] // 2
            dst = lands[a].at[me, pl.ds(c * half, half), :]
        for k, (px, py) in enumerate(chips):
            src = dst if gather else srcs[a].at[2 * px + py]
            out.append(pltpu.make_async_remote_copy(src_ref=src, dst_ref=dst, send_sem=send.at[3 * a + k],
                                                    recv_sem=recv.at[3 * a + k], device_id=(px, py, c), device_id_type=MESH))
    return out


def _halved(land):
    return land.shape[1] % 32 == 0


def _pair_fill(name, lands):
    n = len(lands)

    def body(*refs):
        src, dst = refs[:n], refs[n:2 * n]
        send, recv = refs[2 * n:]
        x, y, c, chips = _position()
        copies = []
        for a in range(n):
            half = src[a].shape[1] // 2
            for k, (px, py) in enumerate(chips):
                rows = (2 * px + py, pl.ds(c * half, half), slice(None))
                cp = pltpu.make_async_remote_copy(src_ref=src[a].at[rows], dst_ref=dst[a].at[rows], send_sem=send.at[a, k],
                                                  recv_sem=recv.at[a, k], device_id=(x, y, 1 - c), device_id_type=MESH)
                cp.start()
                copies.append(cp)
        for cp in copies:
            cp.wait()

    return pl.pallas_call(
        body, name=name, in_specs=[ANY] * n, out_specs=[ANY] * n,
        out_shape=[jax.ShapeDtypeStruct(l.shape, l.dtype) for l in lands],
        input_output_aliases={a: a for a in range(n)},
        scratch_shapes=[pltpu.SemaphoreType.DMA((n, 3)), pltpu.SemaphoreType.DMA((n, 3))],
    )(*lands)


def _exchange_start(name, srcs, lands, after):
    ns, nl, na = len(srcs), len(lands), len(after)
    gather = ns == 0

    def body(*refs):
        src_refs, land_refs = refs[:ns], refs[ns:ns + nl]
        send, recv = refs[ns + nl + na], refs[ns + nl + na + 1]
        token = refs[-1]
        for cp in _exchange_copies(src_refs, land_refs, send, recv, gather):
            cp.start()
        token[...] = jnp.zeros_like(token)

    arrays = [pltpu.with_memory_space_constraint(a, pltpu.HBM) for a in list(srcs) + list(lands)]
    outs = pl.pallas_call(
        body, name=name,
        out_shape=(pltpu.SemaphoreType.DMA((3 * nl,)), pltpu.SemaphoreType.DMA((3 * nl,)),
                   *[pltpu.HBM(a.shape, a.dtype) for a in arrays], jax.ShapeDtypeStruct((8, LANES), F32)),
        in_specs=[HBM_SPEC] * (ns + nl) + [ANY] * na,
        out_specs=(SEM_SPEC, SEM_SPEC, *[HBM_SPEC] * (ns + nl), pl.BlockSpec(memory_space=pltpu.VMEM)),
        input_output_aliases={i: 2 + i for i in range(ns + nl)},
        compiler_params=pltpu.CompilerParams(has_side_effects=DATAFLOW),
    )(*arrays, *after)
    return outs[0], outs[1], list(outs[2:2 + ns]), list(outs[2 + ns:2 + ns + nl]), outs[-1]


def _exchange_wait(name, send, recv, srcs, lands, after):
    ns, nl, na = len(srcs), len(lands), len(after)
    gather = ns == 0

    def body(*refs):
        src_refs, land_refs = refs[:ns], refs[ns:ns + nl]
        send_ref, recv_ref = refs[ns + nl], refs[ns + nl + 1]
        for cp in _exchange_copies(src_refs, land_refs, send_ref, recv_ref, gather):
            cp.wait_send()
            cp.wait_recv()

    outs = pl.pallas_call(
        body, name=name,
        out_shape=tuple(pltpu.HBM(a.shape, a.dtype) for a in list(srcs) + list(lands)),
        in_specs=[HBM_SPEC] * (ns + nl) + [SEM_SPEC, SEM_SPEC] + [ANY] * na,
        out_specs=tuple([HBM_SPEC] * (ns + nl)),
        input_output_aliases={i: i for i in range(ns + nl)},
        compiler_params=pltpu.CompilerParams(has_side_effects=DATAFLOW),
    )(*srcs, *lands, send, recv, *after)
    return list(outs[ns:])


def _pair_exchange(grads):
    n = len(grads)

    def body(*refs):
        src, dst = refs[:n], refs[n:2 * n]
        send, recv = refs[2 * n:]
        x, y, c, _ = _position()
        copies = []
        for a in range(n):
            half = src[a].shape[1] // 2
            cp = pltpu.make_async_remote_copy(
                src_ref=src[a].at[:, pl.ds((1 - c) * half, half), :], dst_ref=dst[a], send_sem=send.at[a],
                recv_sem=recv.at[a], device_id=(x, y, 1 - c), device_id_type=MESH)
            cp.start()
            copies.append(cp)
        for cp in copies:
            cp.wait()

    return pl.pallas_call(
        body, name="grad_pair_exchange", in_specs=[ANY] * n, out_specs=[ANY] * n,
        out_shape=[jax.ShapeDtypeStruct((g.shape[0], g.shape[1] // 2, g.shape[2]), g.dtype) for g in grads],
        scratch_shapes=[pltpu.SemaphoreType.DMA((n,)), pltpu.SemaphoreType.DMA((n,))],
    )(*grads)


def _shard_of(a):
    return lambda i: jnp.clip(i - a * N_SHARD, 0, N_SHARD - 1)


def _pair_sum(gs, gots, c_idx, me_idx):
    n = len(gs)
    halves = [(g.shape[1] // 2, g.shape[2]) for g in gs]

    def body(c_ref, me_ref, *refs):
        g_refs, got_refs, s_refs, own_refs = (refs[k * n:(k + 1) * n] for k in range(4))
        i = pl.program_id(0)
        for a in range(n):
            @pl.when(i // N_SHARD == a)
            def _(a=a):
                sm = g_refs[a][...] + got_refs[a][...].astype(F32)
                s_refs[a][...] = sm.astype(CD)

                @pl.when(i % N_SHARD == me_ref[0])
                def _():
                    own_refs[a][...] = sm[0]

    shard = [_shard_of(a) for a in range(n)]
    return pl.pallas_call(
        body, name="grad_pair_sum",
        grid_spec=pltpu.PrefetchScalarGridSpec(
            num_scalar_prefetch=2, grid=(n * N_SHARD,),
            in_specs=[pl.BlockSpec((1, h, c_), lambda i, c, me, a=a: (shard[a](i), c[0], 0)) for a, (h, c_) in enumerate(halves)]
            + [pl.BlockSpec((1, h, c_), lambda i, c, me, a=a: (shard[a](i), 0, 0)) for a, (h, c_) in enumerate(halves)],
            out_specs=[pl.BlockSpec((1, h, c_), lambda i, c, me, a=a: (shard[a](i), 0, 0)) for a, (h, c_) in enumerate(halves)]
            + [pl.BlockSpec((h, c_), lambda i, c, me: (0, 0)) for h, c_ in halves]),
        out_shape=[jax.ShapeDtypeStruct((N_SHARD, h, c_), CD) for h, c_ in halves]
        + [jax.ShapeDtypeStruct((h, c_), F32) for h, c_ in halves],
        compiler_params=_params(1),
    )(c_idx, me_idx, *gs, *gots)


def _chip_sum(owns, gots, me_idx):
    n = len(owns)
    tiles = [_row_tile(o.shape[0]) for o in owns]
    counts = [o.shape[0] // t for o, t in zip(owns, tiles)]
    starts = [sum(counts[:a]) for a in range(n)]

    def body(me_ref, *refs):
        own_refs, got_refs, out_refs = (refs[k * n:(k + 1) * n] for k in range(3))
        i = pl.program_id(0)
        for a in range(n):
            @pl.when((i >= starts[a]) & (i < starts[a] + counts[a]))
            def _(a=a):
                total = None
                for j in range(N_SHARD):
                    term = jnp.where(j == me_ref[0], own_refs[a][...], got_refs[a][j].astype(F32))
                    total = term if total is None else total + term
                out_refs[a][...] = total

    tile_of = [lambda i, a=a: jnp.clip(i - starts[a], 0, counts[a] - 1) for a in range(n)]
    own_specs = [pl.BlockSpec((tiles[a], owns[a].shape[1]), lambda i, me, a=a: (tile_of[a](i), 0)) for a in range(n)]
    return pl.pallas_call(
        body, name="grad_chip_sum",
        grid_spec=pltpu.PrefetchScalarGridSpec(
            num_scalar_prefetch=1, grid=(sum(counts),),
            in_specs=own_specs + [pl.BlockSpec((N_SHARD, tiles[a], owns[a].shape[1]), lambda i, me, a=a: (0, tile_of[a](i), 0))
                                  for a in range(n)],
            out_specs=own_specs),
        out_shape=[jax.ShapeDtypeStruct(o.shape, F32) for o in owns],
        compiler_params=_params(1),
    )(me_idx, *owns, *gots)


def _pair_share(halves):
    n = len(halves)

    def body(*refs):
        src, dst = refs[:n], refs[n:2 * n]
        send, recv = refs[2 * n:]
        x, y, c, _ = _position()
        copies = []
        for a in range(n):
            cp = pltpu.make_async_remote_copy(src_ref=src[a], dst_ref=dst[a], send_sem=send.at[a],
                                              recv_sem=recv.at[a], device_id=(x, y, 1 - c), device_id_type=MESH)
            cp.start()
            copies.append(cp)
        for cp in copies:
            cp.wait()

    return pl.pallas_call(
        body, name="grad_pair_share", in_specs=[ANY] * n, out_specs=[ANY] * n,
        out_shape=[jax.ShapeDtypeStruct(h.shape, h.dtype) for h in halves],
        scratch_shapes=[pltpu.SemaphoreType.DMA((n,)), pltpu.SemaphoreType.DMA((n,))],
    )(*halves)


def _small_allreduce(buf):
    rows, cols = buf.shape

    def body(src_ref, out_ref, slots, send, recv):
        x, y, c, _ = _position()
        me = 4 * x + 2 * y + c
        slots[me] = src_ref[...]
        copies = []
        k = 0
        for dx in (0, 1):
            for dy in (0, 1):
                for dc in (0, 1):
                    if (dx, dy, dc) == (0, 0, 0):
                        continue
                    peer = (jnp.where(dx, 1 - x, x), jnp.where(dy, 1 - y, y), jnp.where(dc, 1 - c, c))
                    cp = pltpu.make_async_remote_copy(src_ref=src_ref, dst_ref=slots.at[me], send_sem=send.at[k],
                                                      recv_sem=recv.at[k], device_id=peer, device_id_type=MESH)
                    cp.start()
                    copies.append(cp)
                    k += 1
        for cp in copies:
            cp.wait()
        total = slots[0]
        for dev in range(1, N_DEV):
            total = total + slots[dev]
        out_ref[...] = total

    vm = pl.BlockSpec(memory_space=pltpu.VMEM)
    return pl.pallas_call(
        body, name="small_allreduce", in_specs=[vm], out_specs=vm,
        out_shape=jax.ShapeDtypeStruct((rows, cols), F32),
        scratch_shapes=[pltpu.VMEM((N_DEV, rows, cols), F32), pltpu.SemaphoreType.DMA((N_DEV - 1,)),
                        pltpu.SemaphoreType.DMA((N_DEV - 1,))],
    )(buf)


def _adamw_math(w, gv, m, v):
    mn = ADAM_B1 * m + (1.0 - ADAM_B1) * gv
    vn = ADAM_B2 * v + (1.0 - ADAM_B2) * (gv * gv)
    m_hat = mn / (1.0 - ADAM_B1 ** ADAM_STEP)
    v_hat = vn / (1.0 - ADAM_B2 ** ADAM_STEP)
    return -ADAM_LR * (m_hat / (jnp.sqrt(v_hat) + ADAM_EPS) + ADAM_WD * w), mn, vn


def _adamw(w, g, m, v):
    rows, cols = w.shape
    tr = _row_tile(rows)

    def body(w_ref, g_ref, m_ref, v_ref, d_ref, mo_ref, vo_ref):
        d_ref[...], mo_ref[...], vo_ref[...] = _adamw_math(w_ref[...], g_ref[...], m_ref[...], v_ref[...])

    blk = pl.BlockSpec((tr, cols), lambda i: (i, 0))
    shp = jax.ShapeDtypeStruct((rows, cols), F32)
    return pl.pallas_call(
        body, name="adamw", grid=(rows // tr,), in_specs=[blk] * 4, out_specs=[blk] * 3, out_shape=[shp] * 3,
        compiler_params=_params(1),
    )(w, g, m, v)


def _adamw_halves(w, own, got, m, v, c_idx):
    rows, cols = w.shape
    tr = _row_tile(rows // 2)
    per_half = rows // 2 // tr

    def body(c_ref, w_ref, own_ref, got_ref, m_ref, v_ref, d_ref, mo_ref, vo_ref, g_ref):
        mine = (pl.program_id(0) // per_half) == c_ref[0]
        gv = jnp.where(mine, own_ref[...], got_ref[...])
        g_ref[...] = gv
        d_ref[...], mo_ref[...], vo_ref[...] = _adamw_math(w_ref[...], gv, m_ref[...], v_ref[...])

    blk = pl.BlockSpec((tr, cols), lambda i, c: (i, 0))
    own_blk = pl.BlockSpec((tr, cols), lambda i, c: (jnp.where(i // per_half == c[0], i % per_half, 0), 0))
    got_blk = pl.BlockSpec((tr, cols), lambda i, c: (jnp.where(i // per_half == c[0], 0, i % per_half), 0))
    shp = jax.ShapeDtypeStruct((rows, cols), F32)
    return pl.pallas_call(
        body, name="adamw_halves",
        grid_spec=pltpu.PrefetchScalarGridSpec(num_scalar_prefetch=1, grid=(rows // tr,),
                                               in_specs=[blk, own_blk, got_blk, blk, blk], out_specs=[blk] * 4),
        out_shape=[shp] * 4, compiler_params=_params(1),
    )(c_idx, w, own, got, m, v)


def _local_step(x, target, norm_gains, q_g, k_g, ng, weights_of, grads_done):
    s = x.shape[0]
    tm = min(512, s)
    tq = min(256, s)
    tf = min(1024, s)
    g1, gm, g2, gf = norm_gains
    cos2, sin2 = _rope_tables(s)
    gq8 = jnp.tile(q_g, (1, 8))
    gk2 = jnp.tile(k_g, (1, 2))

    tn = min(256, s)
    tk = min(1024, s)
    w1 = weights_of(1, ())
    x1, s1, t1, b1, h1 = _ffn_fwd(x, g1, w1["g1"], w1["u1"], w1["d1"], tf)
    w2 = weights_of(2, (x1,))
    lbl = w2["lbl"]
    pqkv, ph, pg, hm = _mix_in_fwd(x1, gm, w2["in"], tn)
    qe, kr, vr, vs = _qk_prep(pqkv, gq8, gk2, cos2, sin2, tm)
    oa, lse = _attn_fwd(qe, kr, vr, vs, tq)
    ob, pre, hstates = _hgrn_fwd(ph, lbl, ng)
    x2 = _mix_out_fwd(x1, oa, ob, pg, w2["a"], w2["b"], w2["o"], tm)
    w3 = weights_of(3, (x2,))
    x3, s2, t2, b2, h2 = _ffn_fwd(x2, g2, w3["g2"], w3["u2"], w3["d2"], tf)
    dx3, loss, dgf = _loss_head(x3, gf, target, tm)

    dx2, da2, db2, f2, dg2, dx3c = _ffn_bwd(dx3, x2, g2, s2, t2, b2, w3["g2"], w3["u2"], w3["d2"], tm)
    tok = grads_done(3, dict(g2=_dw_shared_b("dw_gate", da2, h2, tk, 1.0), u2=_dw_shared_b("dw_gate", db2, h2, tk, 1.0),
                             d2=_dw_shared_b("dw_down", f2, dx3c, tk, 1.0)))

    dpg, mg, dya, dyb, doe, delta, dob = _mix_out_bwd(dx2, oa, ob, pg, w2["a"], w2["b"], w2["o"], tm, tok)
    g_o = [g.reshape(N_SHARD, D_MODEL // N_SHARD, D_MODEL) for g in _dw_colblocks("dw_out", mg, dx2, 1, tk)]
    g_a = _dw_colblocks("dw_branch", oa, dya, N_SHARD, tk)
    g_b = _dw_colblocks("dw_branch", ob, dyb, N_SHARD, tk)
    dqt, dkt, dvt = _attn_bwd(qe, kr, kr.T, vr, doe, delta, lse, tq)
    dqkv, dgq, dgk = _qk_prep_bwd(pqkv, dqt.T, dkt.T, dvt.T, gq8, gk2, cos2, sin2, tm)
    dhq, dhff, dhfb, dhi, dhg, dlb, dng = _hgrn_bwd(ph, pre, dob, hstates, lbl, ng)
    dps = (dqkv, dhq, dhff, dhfb, dhi, dhg, dpg)
    g_in = [g.reshape(N_SHARD, -1, D_MODEL) for g in _dw_in(dps, hm, tk)]
    tok = grads_done(2, {"in": g_in, "a": g_a, "b": g_b, "o": g_o})
    dx1, dgm = _mix_in_bwd(dps, w2["in"], x1, dx2, gm, tn, tok)

    dx0, da1, db1, f1, dg1, dx1c = _ffn_bwd(dx1, x, g1, s1, t1, b1, w1["g1"], w1["u1"], w1["d1"], tm)
    grads_done(1, dict(g1=_dw_shared_b("dw_gate", da1, h1, tk, 1.0), u1=_dw_shared_b("dw_gate", db1, h1, tk, 1.0),
                       d1=_dw_shared_b("dw_down", f1, dx1c, tk, 1.0)))
    small = dict(g1=dg1, gm=dgm, g2=dg2, gf=dgf, gq=dgq, gk=dgk, lb=dlb, ng=dng)
    return loss, dx0, small, lbl


GROUPS = {1: ("g1", "u1", "d1"), 2: ("in", "a", "b", "o"), 3: ("g2", "u2", "d2")}
BIG = GROUPS[1] + GROUPS[2] + GROUPS[3]
TRANSPOSED = ("g1", "u1", "in", "g2", "u2")


def _pack_rows(vectors, width):
    rows = []
    for vct in vectors:
        flat = vct.reshape(-1)
        pad = (-flat.shape[0]) % width
        rows.append(jnp.pad(flat, (0, pad)).reshape(-1, width))
    return jnp.concatenate(rows, axis=0)


def kernel(x, ffn1_norm_g, ffn1_w_gate, ffn1_w_up, ffn1_w_down, mix_norm_g, w_in, q_norm_g, k_norm_g, hgrn_lb_logits, hgrn_out_norm_g, w_branch_attn, w_branch_hgrn, w_out, ffn2_norm_g, ffn2_w_gate, ffn2_w_up, ffn2_w_down, final_norm_g, loss_target, m_ffn1_norm_g, m_ffn1_w_gate, m_ffn1_w_up, m_ffn1_w_down, m_mix_norm_g, m_w_in, m_q_norm_g, m_k_norm_g, m_hgrn_lb_logits, m_hgrn_out_norm_g, m_w_branch_attn, m_w_branch_hgrn, m_w_out, m_ffn2_norm_g, m_ffn2_w_gate, m_ffn2_w_up, m_ffn2_w_down, m_final_norm_g, v_ffn1_norm_g, v_ffn1_w_gate, v_ffn1_w_up, v_ffn1_w_down, v_mix_norm_g, v_w_in, v_q_norm_g, v_k_norm_g, v_hgrn_lb_logits, v_hgrn_out_norm_g, v_w_branch_attn, v_w_branch_hgrn, v_w_out, v_ffn2_norm_g, v_ffn2_w_gate, v_ffn2_w_up, v_ffn2_w_down, v_final_norm_g):
    xi, yi, ci = lax.axis_index("x"), lax.axis_index("y"), lax.axis_index("c")
    me = 2 * xi + yi
    c_idx = jnp.reshape(ci, (1,)).astype(jnp.int32)
    me_idx = jnp.reshape(me, (1,)).astype(jnp.int32)

    big_w = dict(g1=ffn1_w_gate[0], u1=ffn1_w_up[0], d1=ffn1_w_down[0], a=w_branch_attn[0], b=w_branch_hgrn[0],
                 o=w_out[0], g2=ffn2_w_gate[0], u2=ffn2_w_up[0], d2=ffn2_w_down[0])
    big_w["in"] = w_in[0]
    big_m = dict(g1=m_ffn1_w_gate[0], u1=m_ffn1_w_up[0], d1=m_ffn1_w_down[0], a=m_w_branch_attn[0], b=m_w_branch_hgrn[0],
                 o=m_w_out[0], g2=m_ffn2_w_gate[0], u2=m_ffn2_w_up[0], d2=m_ffn2_w_down[0])
    big_m["in"] = m_w_in[0]
    big_v = dict(g1=v_ffn1_w_gate[0], u1=v_ffn1_w_up[0], d1=v_ffn1_w_down[0], a=v_w_branch_attn[0], b=v_w_branch_hgrn[0],
                 o=v_w_out[0], g2=v_ffn2_w_gate[0], u2=v_ffn2_w_up[0], d2=v_ffn2_w_down[0])
    big_v["in"] = v_w_in[0]
    for table in (big_w, big_m, big_v):
        for n in TRANSPOSED:
            table[n] = table[n].T

    started, token = {}, ()
    for grp in (1, 2, 3):
        shards = [big_w[n] for n in GROUPS[grp]] + ([hgrn_lb_logits.reshape(4, LANES)] if grp == 2 else [])
        dtypes = [CD] * len(GROUPS[grp]) + ([F32] if grp == 2 else [])
        lands = _cast_into_slots(shards, dtypes, me_idx)
        send, recv, _, lands, tok = _exchange_start("gather%d_start" % grp, [], lands, token)
        started[grp], token = (send, recv, lands), (tok,)

    def weights_of(grp, after):
        send, recv, lands = started[grp]
        got = _exchange_wait("gather%d_wait" % grp, send, recv, [], lands, tuple(after) + (token if grp == 1 else ()))
        by_halves = [i for i, land in enumerate(got) if _halved(land)]
        for i, whole in zip(by_halves, _pair_fill("gather%d_fill" % grp, [got[i] for i in by_halves])):
            got[i] = whole
        w = dict(zip(GROUPS[grp], got))
        if grp == 2:
            w["in"] = w["in"].reshape(-1, D_MODEL)
            w["o"] = w["o"].reshape(D_MODEL, D_MODEL)
            w["lbl"] = jnp.transpose(got[-1], (1, 0, 2)).reshape(4, N_SHARD * LANES)
        return w

    pending = {}

    def grads_done(grp, grads):
        names = list(grads)
        got = _pair_exchange([grads[n][1] for n in names])
        res = _pair_sum([grads[n][0] for n in names], got, c_idx, me_idx)
        sums, owns = res[:len(names)], res[len(names):]
        lands = [lax.empty(s_.shape, s_.dtype) for s_ in sums]
        send, recv, srcs, lands, tok = _exchange_start("reduce%d_start" % grp, list(sums), lands, ())
        pending[grp] = (names, send, recv, srcs, lands, owns, tok)
        return (tok,)

    def reduced_halves(grp, after):
        names, send, recv, srcs, lands, owns, _ = pending[grp]
        parts = _exchange_wait("reduce%d_wait" % grp, send, recv, srcs, lands, after)
        return names, list(_chip_sum(list(owns), parts, me_idx))

    loss, dx, small, lbl = _local_step(
        x[0], loss_target[0], (ffn1_norm_g, mix_norm_g, ffn2_norm_g, final_norm_g.reshape(1, -1)),
        q_norm_g, k_norm_g, hgrn_out_norm_g, weights_of, grads_done)

    dgq = small["gq"].reshape(8, HEAD_DIM).sum(axis=0)
    dgk = small["gk"].reshape(2, HEAD_DIM).sum(axis=0)
    lb_full = _hgrn_lower_bounds(lbl)
    dlog = []
    for d in (0, 1):
        t = small["lb"][d:d + 1] * lb_full[d] * (1.0 - lb_full[d])
        dlog += [t, -t]
    small_list = [small["g1"], small["gm"], small["g2"], small["gf"], small["ng"], dgq, dgk, jnp.concatenate(dlog, axis=0), loss[0, 0]]
    packed = _pack_rows(small_list, D_MODEL)
    n_rows = packed.shape[0]
    packed = jnp.pad(packed, ((0, (-n_rows) % 8), (0, 0)))
    red = _small_allreduce(packed)
    loss_out = red[n_rows - 1, 0]
    sg = dict(g1=red[0:1], gm=red[1:2], g2=red[2:3], gf=red[3], ng=red[4:5, :512], gq=red[5:6, :HEAD_DIM],
              gk=red[6:7, :HEAD_DIM])
    dlog_full = red[7:9].reshape(2, 2, 512)
    sg["lb"] = lax.dynamic_slice_in_dim(dlog_full, me * LANES, LANES, axis=2)

    small_w = dict(g1=ffn1_norm_g, gm=mix_norm_g, g2=ffn2_norm_g, gf=final_norm_g, ng=hgrn_out_norm_g, gq=q_norm_g,
                   gk=k_norm_g, lb=hgrn_lb_logits)
    small_m = dict(g1=m_ffn1_norm_g, gm=m_mix_norm_g, g2=m_ffn2_norm_g, gf=m_final_norm_g, ng=m_hgrn_out_norm_g,
                   gq=m_q_norm_g, gk=m_k_norm_g, lb=m_hgrn_lb_logits)
    small_v = dict(g1=v_ffn1_norm_g, gm=v_mix_norm_g, g2=v_ffn2_norm_g, gf=v_final_norm_g, ng=v_hgrn_out_norm_g,
                   gq=v_q_norm_g, gk=v_k_norm_g, lb=v_hgrn_lb_logits)
    small_names = ("g1", "gm", "g2", "gf", "ng", "gq", "gk", "lb")
    pack = lambda dct: _pack_rows([dct[n] for n in small_names], D_MODEL)
    pw, pgr, pm, pv = pack(small_w), pack(sg), pack(small_m), pack(small_v)
    pad8 = lambda a: jnp.pad(a, ((0, (-a.shape[0]) % 8), (0, 0)))
    sd, sm_, sv_ = _adamw(pad8(pw), pad8(pgr), pad8(pm), pad8(pv))

    def unpack(buf):
        out, r = {}, 0
        for n in small_names:
            size = small_w[n].size
            nr = -(-size // D_MODEL)
            out[n] = buf[r:r + nr].reshape(-1)[:size].reshape(small_w[n].shape)
            r += nr
        return out

    sdelta, snew_m, snew_v = unpack(sd), unpack(sm_), unpack(sv_)
    sgrad = {n: sg[n].reshape(small_w[n].shape) for n in small_names}

    bdelta, bnew_m, bnew_v, bgrad = {}, {}, {}, {}

    def update(names, halves):
        for n, own, got in zip(names, halves, _pair_share(halves)):
            res = _adamw_halves(big_w[n], own, got, big_m[n], big_v[n], c_idx)
            if n in TRANSPOSED:
                res = [r.T for r in res]
            bdelta[n], bnew_m[n], bnew_v[n], bgrad[n] = [r[None] for r in res]

    names3, halves3 = reduced_halves(3, (pending[1][-1],))
    names2, halves2 = reduced_halves(2, (halves3[0],))
    update(names3 + names2, halves3 + halves2)
    names1, halves1 = reduced_halves(1, (bdelta[names2[-1]],))
    update(names1, halves1)

    order = [("s", "g1"), ("b", "g1"), ("b", "u1"), ("b", "d1"), ("s", "gm"), ("b", "in"), ("s", "gq"), ("s", "gk"),
             ("s", "lb"), ("s", "ng"), ("b", "a"), ("b", "b"), ("b", "o"), ("s", "g2"), ("b", "g2"), ("b", "u2"),
             ("b", "d2"), ("s", "gf")]
    outs = [loss_out, dx[None]]
    for table_s, table_b in ((sgrad, bgrad), (sdelta, bdelta), (snew_m, bnew_m), (snew_v, bnew_v)):
        outs += [(table_s if kind == "s" else table_b)[n] for kind, n in order]
    return tuple(outs)
```
